```python
import jax, jax.numpy as jnp
from jax import lax
import numpy as np

D_MODEL = 1024
BATCH = 8
SEQ = 4096
DEPTH = 4

N_MIXERS = 3
EXPAND = 2
D_INNER = EXPAND * D_MODEL
POOL_WINDOWS = (2, 4, 8, 16)
N_POOL_GROUPS = len(POOL_WINDOWS)
POOL_GROUP = D_INNER // N_POOL_GROUPS
CONV_WIDTH = 3
N_HEADS = 16
QK_NOPE_DIM = 128
QK_ROPE_DIM = 64
V_HEAD_DIM = D_INNER // N_HEADS
Q_LORA_RANK = 384
KV_LORA_RANK = 256
MLA_IN_DIM = Q_LORA_RANK + KV_LORA_RANK + QK_ROPE_DIM + D_INNER
ATTN_SCALE = (QK_NOPE_DIM + QK_ROPE_DIM) ** -0.5
ROPE_BASE = 10000.0
Q_BLOCK = 128
NORM_EPS = 1e-6
MAX_POS_OFFSET = 1024
N_POOL = (DEPTH + 2) // 3
N_CONV = (DEPTH + 1) // 3
N_MLA = DEPTH // 3

kernel_name = "hybrid_pool_conv_mla_gated_trunk"


def rms_norm(x, g):
    xf = x.astype(jnp.float32)
    y = xf * lax.rsqrt(jnp.mean(xf * xf, axis=-1, keepdims=True) + NORM_EPS)
    return (y * g.astype(jnp.float32)).astype(x.dtype)


def pool_mixer(xn, w_in, w_grp, scale, w_out):
    B, S, _ = xn.shape
    u, z = jnp.split(xn @ w_in, 2, axis=-1)
    uf = u.astype(jnp.float32).reshape(B, S, N_POOL_GROUPS, POOL_GROUP)
    cs = jnp.cumsum(uf, axis=1)
    count_base = jnp.arange(1, S + 1, dtype=jnp.float32)
    pooled = []
    for g, w in enumerate(POOL_WINDOWS):
        c = cs[:, :, g]
        prev = jnp.pad(c, ((0, 0), (w, 0), (0, 0)))[:, :S]
        mean = (c - prev) / jnp.minimum(count_base, float(w))[None, :, None]
        pooled.append(mean - uf[:, :, g])
    pooled = jnp.stack(pooled, axis=2).astype(u.dtype)
    mixed = jnp.einsum('bsgc,gcd->bsgd', pooled, w_grp).reshape(B, S, D_INNER) * scale
    return (mixed * jax.nn.silu(z)) @ w_out


def causal_depthwise_conv(x, w):
    return lax.conv_general_dilated(
        x, w[:, None, :].astype(x.dtype), window_strides=(1,),
        padding=[(CONV_WIDTH - 1, 0)], dimension_numbers=('NWC', 'WIO', 'NWC'),
        feature_group_count=x.shape[-1])


def conv_mixer(xn, w_in, conv_w, w_out):
    b, c, h, z = jnp.split(xn @ w_in, 4, axis=-1)
    y = b * causal_depthwise_conv(c * h, conv_w)
    return (y * jax.nn.silu(z)) @ w_out


def apply_rope(x, cos, sin):
    half = x.shape[-1] // 2
    x1, x2 = x[..., :half], x[..., half:]
    return jnp.concatenate([x1 * cos - x2 * sin, x2 * cos + x1 * sin], axis=-1).astype(x.dtype)


def causal_block_attention(q_nope, q_rope, k_nope, k_rope, v):
    B, S, H, _ = q_nope.shape
    nb = S // Q_BLOCK
    qn = q_nope.reshape(B, nb, Q_BLOCK, H, QK_NOPE_DIM).transpose(1, 0, 2, 3, 4)
    qr = q_rope.reshape(B, nb, Q_BLOCK, H, QK_ROPE_DIM).transpose(1, 0, 2, 3, 4)
    starts = jnp.arange(nb, dtype=jnp.int32) * Q_BLOCK
    key_idx = jnp.arange(S, dtype=jnp.int32)

    def one_block(args):
        qn_b, qr_b, start = args
        s = (jnp.einsum('bqhd,bkhd->bhqk', qn_b, k_nope).astype(jnp.float32)
             + jnp.einsum('bqhr,bkr->bhqk', qr_b, k_rope).astype(jnp.float32)) * ATTN_SCALE
        q_idx = start + jnp.arange(Q_BLOCK, dtype=jnp.int32)
        mask = key_idx[None, :] <= q_idx[:, None]
        s = jnp.where(mask[None, None], s, jnp.float32(-1e30))
        p = jax.nn.softmax(s, axis=-1).astype(v.dtype)
        return jnp.einsum('bhqk,bkhd->bqhd', p, v)

    o = lax.map(one_block, (qn, qr, starts))
    return o.transpose(1, 0, 2, 3, 4).reshape(B, S, H, V_HEAD_DIM)


def mla_mixer(xn, cos, sin, w_in, q_norm, w_q_up, kv_norm, w_kv_up, w_out):
    B, S, _ = xn.shape
    h = xn @ w_in
    q_lat, kv_lat, k_rope, z = jnp.split(
        h, [Q_LORA_RANK, Q_LORA_RANK + KV_LORA_RANK,
            Q_LORA_RANK + KV_LORA_RANK + QK_ROPE_DIM], axis=-1)
    q = (rms_norm(q_lat, q_norm) @ w_q_up).reshape(B, S, N_HEADS, QK_NOPE_DIM + QK_ROPE_DIM)
    q_nope = q[..., :QK_NOPE_DIM]
    q_rope = apply_rope(q[..., QK_NOPE_DIM:], cos[:, :, None, :], sin[:, :, None, :])
    kv = (rms_norm(kv_lat, kv_norm) @ w_kv_up).reshape(B, S, N_HEADS, QK_NOPE_DIM + V_HEAD_DIM)
    k_nope, v = kv[..., :QK_NOPE_DIM], kv[..., QK_NOPE_DIM:]
    k_rope = apply_rope(k_rope, cos, sin)
    o = causal_block_attention(q_nope, q_rope, k_nope, k_rope, v)
    return (o.reshape(B, S, D_INNER) * jax.nn.silu(z)) @ w_out


def _fwd_setup_inputs(seed: int = 0) -> dict:
    key = jax.random.key(seed)
    ks = jax.random.split(key, 20)

    def normal(k, shape, scale):
        return jax.random.normal(k, shape, jnp.float32) * scale

    def gain(k, shape):
        return 1.0 + 0.02 * jax.random.normal(k, shape, jnp.float32)

    x = normal(ks[0], (BATCH, SEQ, D_MODEL), 1.0)
    offset = jax.random.randint(ks[1], (BATCH, 1), 0, MAX_POS_OFFSET, dtype=jnp.int32)
    positions = (offset + jnp.arange(SEQ, dtype=jnp.int32)[None, :]).astype(jnp.int32)
    return {
        "x": x,
        "positions": positions,
        "pool_norm": gain(ks[2], (N_POOL, D_MODEL)),
        "pool_w_in": normal(ks[3], (N_POOL, D_MODEL, 2 * D_INNER), D_MODEL ** -0.5),
        "pool_w_grp": normal(ks[4], (N_POOL, N_POOL_GROUPS, POOL_GROUP, POOL_GROUP), POOL_GROUP ** -0.5),
        "pool_scale": gain(ks[5], (N_POOL, D_INNER)),
        "pool_w_out": normal(ks[6], (N_POOL, D_INNER, D_MODEL), D_INNER ** -0.5),
        "conv_norm": gain(ks[7], (N_CONV, D_MODEL)),
        "conv_w_in": normal(ks[8], (N_CONV, D_MODEL, 4 * D_INNER), D_MODEL ** -0.5),
        "conv_w": normal(ks[9], (N_CONV, CONV_WIDTH, D_INNER), CONV_WIDTH ** -0.5),
        "conv_w_out": normal(ks[10], (N_CONV, D_INNER, D_MODEL), D_INNER ** -0.5),
        "mla_norm": gain(ks[11], (N_MLA, D_MODEL)),
        "mla_w_in": normal(ks[12], (N_MLA, D_MODEL, MLA_IN_DIM), D_MODEL ** -0.5),
        "mla_q_norm": gain(ks[13], (N_MLA, Q_LORA_RANK)),
        "mla_w_q_up": normal(ks[14], (N_MLA, Q_LORA_RANK, N_HEADS * (QK_NOPE_DIM + QK_ROPE_DIM)), Q_LORA_RANK ** -0.5),
        "mla_kv_norm": gain(ks[15], (N_MLA, KV_LORA_RANK)),
        "mla_w_kv_up": normal(ks[16], (N_MLA, KV_LORA_RANK, N_HEADS * (QK_NOPE_DIM + V_HEAD_DIM)), KV_LORA_RANK ** -0.5),
        "mla_w_out": normal(ks[17], (N_MLA, D_INNER, D_MODEL), D_INNER ** -0.5),
        "final_norm": gain(ks[18], (D_MODEL,)),
    }


def _fwd_reference(x, positions, pool_norm, pool_w_in, pool_w_grp, pool_scale, pool_w_out,
              conv_norm, conv_w_in, conv_w, conv_w_out,
              mla_norm, mla_w_in, mla_q_norm, mla_w_q_up, mla_kv_norm, mla_w_kv_up, mla_w_out,
              final_norm):
    inv_freq = ROPE_BASE ** (-jnp.arange(0, QK_ROPE_DIM, 2, dtype=jnp.float32) / QK_ROPE_DIM)
    angles = positions.astype(jnp.float32)[..., None] * inv_freq
    cos, sin = jnp.cos(angles).astype(x.dtype), jnp.sin(angles).astype(x.dtype)

    for i in range(DEPTH):
        kind, j = i % N_MIXERS, i // N_MIXERS
        if kind == 0:
            xn = rms_norm(x, pool_norm[j])
            x = x + pool_mixer(xn, pool_w_in[j], pool_w_grp[j], pool_scale[j], pool_w_out[j])
        elif kind == 1:
            xn = rms_norm(x, conv_norm[j])
            x = x + conv_mixer(xn, conv_w_in[j], conv_w[j], conv_w_out[j])
        else:
            xn = rms_norm(x, mla_norm[j])
            x = x + mla_mixer(xn, cos, sin, mla_w_in[j], mla_q_norm[j], mla_w_q_up[j],
                              mla_kv_norm[j], mla_w_kv_up[j], mla_w_out[j])
    return rms_norm(x, final_norm)


import jax as _jax
import jax.numpy as _jnp

TWIN_FORMAT = 'train_step'
FWD_PARAMS = ['x', 'positions', 'pool_norm', 'pool_w_in', 'pool_w_grp', 'pool_scale', 'pool_w_out', 'conv_norm', 'conv_w_in', 'conv_w', 'conv_w_out', 'mla_norm', 'mla_w_in', 'mla_q_norm', 'mla_w_q_up', 'mla_kv_norm', 'mla_w_kv_up', 'mla_w_out', 'final_norm']
TWIN_WEIGHTS = ['pool_norm', 'pool_w_in', 'pool_w_grp', 'pool_scale', 'pool_w_out', 'conv_norm', 'conv_w_in', 'conv_w', 'conv_w_out', 'mla_norm', 'mla_w_in', 'mla_q_norm', 'mla_w_q_up', 'mla_kv_norm', 'mla_w_kv_up', 'mla_w_out', 'final_norm']
TWIN_DIFF_INPUT = 'x'
TWIN_INPUTS = ['x', 'positions', 'pool_norm', 'pool_w_in', 'pool_w_grp', 'pool_scale', 'pool_w_out', 'conv_norm', 'conv_w_in', 'conv_w', 'conv_w_out', 'mla_norm', 'mla_w_in', 'mla_q_norm', 'mla_w_q_up', 'mla_kv_norm', 'mla_w_kv_up', 'mla_w_out', 'final_norm', 'loss_target', 'm_pool_norm', 'm_pool_w_in', 'm_pool_w_grp', 'm_pool_scale', 'm_pool_w_out', 'm_conv_norm', 'm_conv_w_in', 'm_conv_w', 'm_conv_w_out', 'm_mla_norm', 'm_mla_w_in', 'm_mla_q_norm', 'm_mla_w_q_up', 'm_mla_kv_norm', 'm_mla_w_kv_up', 'm_mla_w_out', 'm_final_norm', 'v_pool_norm', 'v_pool_w_in', 'v_pool_w_grp', 'v_pool_scale', 'v_pool_w_out', 'v_conv_norm', 'v_conv_w_in', 'v_conv_w', 'v_conv_w_out', 'v_mla_norm', 'v_mla_w_in', 'v_mla_q_norm', 'v_mla_w_q_up', 'v_mla_kv_norm', 'v_mla_w_kv_up', 'v_mla_w_out', 'v_final_norm']
TWIN_OUTPUTS = ['loss', 'grad_x', 'grad_pool_norm', 'grad_pool_w_in', 'grad_pool_w_grp', 'grad_pool_scale', 'grad_pool_w_out', 'grad_conv_norm', 'grad_conv_w_in', 'grad_conv_w', 'grad_conv_w_out', 'grad_mla_norm', 'grad_mla_w_in', 'grad_mla_q_norm', 'grad_mla_w_q_up', 'grad_mla_kv_norm', 'grad_mla_w_kv_up', 'grad_mla_w_out', 'grad_final_norm', 'delta_pool_norm', 'delta_pool_w_in', 'delta_pool_w_grp', 'delta_pool_scale', 'delta_pool_w_out', 'delta_conv_norm', 'delta_conv_w_in', 'delta_conv_w', 'delta_conv_w_out', 'delta_mla_norm', 'delta_mla_w_in', 'delta_mla_q_norm', 'delta_mla_w_q_up', 'delta_mla_kv_norm', 'delta_mla_w_kv_up', 'delta_mla_w_out', 'delta_final_norm', 'new_m_pool_norm', 'new_m_pool_w_in', 'new_m_pool_w_grp', 'new_m_pool_scale', 'new_m_pool_w_out', 'new_m_conv_norm', 'new_m_conv_w_in', 'new_m_conv_w', 'new_m_conv_w_out', 'new_m_mla_norm', 'new_m_mla_w_in', 'new_m_mla_q_norm', 'new_m_mla_w_q_up', 'new_m_mla_kv_norm', 'new_m_mla_w_kv_up', 'new_m_mla_w_out', 'new_m_final_norm', 'new_v_pool_norm', 'new_v_pool_w_in', 'new_v_pool_w_grp', 'new_v_pool_scale', 'new_v_pool_w_out', 'new_v_conv_norm', 'new_v_conv_w_in', 'new_v_conv_w', 'new_v_conv_w_out', 'new_v_mla_norm', 'new_v_mla_w_in', 'new_v_mla_q_norm', 'new_v_mla_w_q_up', 'new_v_mla_kv_norm', 'new_v_mla_w_kv_up', 'new_v_mla_w_out', 'new_v_final_norm']
TWIN_LEAF_KINDS = {'loss': 'loss', 'grad_x': 'grad_x', 'grad_pool_norm': 'grad_w', 'grad_pool_w_in': 'grad_w', 'grad_pool_w_grp': 'grad_w', 'grad_pool_scale': 'grad_w', 'grad_pool_w_out': 'grad_w', 'grad_conv_norm': 'grad_w', 'grad_conv_w_in': 'grad_w', 'grad_conv_w': 'grad_w', 'grad_conv_w_out': 'grad_w', 'grad_mla_norm': 'grad_w', 'grad_mla_w_in': 'grad_w', 'grad_mla_q_norm': 'grad_w', 'grad_mla_w_q_up': 'grad_w', 'grad_mla_kv_norm': 'grad_w', 'grad_mla_w_kv_up': 'grad_w', 'grad_mla_w_out': 'grad_w', 'grad_final_norm': 'grad_w', 'delta_pool_norm': 'delta_w', 'delta_pool_w_in': 'delta_w', 'delta_pool_w_grp': 'delta_w', 'delta_pool_scale': 'delta_w', 'delta_pool_w_out': 'delta_w', 'delta_conv_norm': 'delta_w', 'delta_conv_w_in': 'delta_w', 'delta_conv_w': 'delta_w', 'delta_conv_w_out': 'delta_w', 'delta_mla_norm': 'delta_w', 'delta_mla_w_in': 'delta_w', 'delta_mla_q_norm': 'delta_w', 'delta_mla_w_q_up': 'delta_w', 'delta_mla_kv_norm': 'delta_w', 'delta_mla_w_kv_up': 'delta_w', 'delta_mla_w_out': 'delta_w', 'delta_final_norm': 'delta_w', 'new_m_pool_norm': 'new_m', 'new_m_pool_w_in': 'new_m', 'new_m_pool_w_grp': 'new_m', 'new_m_pool_scale': 'new_m', 'new_m_pool_w_out': 'new_m', 'new_m_conv_norm': 'new_m', 'new_m_conv_w_in': 'new_m', 'new_m_conv_w': 'new_m', 'new_m_conv_w_out': 'new_m', 'new_m_mla_norm': 'new_m', 'new_m_mla_w_in': 'new_m', 'new_m_mla_q_norm': 'new_m', 'new_m_mla_w_q_up': 'new_m', 'new_m_mla_kv_norm': 'new_m', 'new_m_mla_w_kv_up': 'new_m', 'new_m_mla_w_out': 'new_m', 'new_m_final_norm': 'new_m', 'new_v_pool_norm': 'new_v', 'new_v_pool_w_in': 'new_v', 'new_v_pool_w_grp': 'new_v', 'new_v_pool_scale': 'new_v', 'new_v_pool_w_out': 'new_v', 'new_v_conv_norm': 'new_v', 'new_v_conv_w_in': 'new_v', 'new_v_conv_w': 'new_v', 'new_v_conv_w_out': 'new_v', 'new_v_mla_norm': 'new_v', 'new_v_mla_w_in': 'new_v', 'new_v_mla_q_norm': 'new_v', 'new_v_mla_w_q_up': 'new_v', 'new_v_mla_kv_norm': 'new_v', 'new_v_mla_w_kv_up': 'new_v', 'new_v_mla_w_out': 'new_v', 'new_v_final_norm': 'new_v'}


def _forward(args):
    return _fwd_reference(*[args[k] for k in FWD_PARAMS])


def _output_shape():
    def fwd():
        inp = _fwd_setup_inputs(0)
        return _fwd_reference(*[inp[k] for k in FWD_PARAMS])
    out = _jax.eval_shape(fwd)
    return out.shape, out.dtype

N_MICROBATCH = 1
ADAM_LR = 0.001
ADAM_B1 = 0.9
ADAM_B2 = 0.999
ADAM_EPS = 1e-08
ADAM_WD = 0.01
ADAM_STEP = 10
PER_EXAMPLE_BATCH_AXIS = {'x': 0, 'positions': 0, 'loss_target': 0}
SHARED_INPUTS = []
_WEIGHT_DTYPES = {'pool_norm': _jnp.float32, 'pool_w_in': _jnp.float32, 'pool_w_grp': _jnp.float32, 'pool_scale': _jnp.float32, 'pool_w_out': _jnp.float32, 'conv_norm': _jnp.float32, 'conv_w_in': _jnp.float32, 'conv_w': _jnp.float32, 'conv_w_out': _jnp.float32, 'mla_norm': _jnp.float32, 'mla_w_in': _jnp.float32, 'mla_q_norm': _jnp.float32, 'mla_w_q_up': _jnp.float32, 'mla_kv_norm': _jnp.float32, 'mla_w_kv_up': _jnp.float32, 'mla_w_out': _jnp.float32, 'final_norm': _jnp.float32}
MOMENT_SCALE = {'pool_norm': 1.327481e-01, 'pool_w_in': 6.719710e-02, 'pool_w_grp': 6.606018e-02, 'pool_scale': 6.555051e-02, 'pool_w_out': 9.343975e-02, 'conv_norm': 1.852576e-01, 'conv_w_in': 6.366064e-02, 'conv_w': 6.309336e-02, 'conv_w_out': 8.896152e-02, 'mla_norm': 3.619204e-02, 'mla_w_in': 2.179741e-02, 'mla_q_norm': 2.582629e-02, 'mla_w_q_up': 9.270226e-03, 'mla_kv_norm': 4.732328e-02, 'mla_w_kv_up': 1.178729e-02, 'mla_w_out': 1.937060e-02, 'final_norm': 3.202453e+01}


def _to_microbatches(a, axis):
    t = _jnp.moveaxis(a, axis, 0)
    t = t.reshape((N_MICROBATCH, t.shape[0] // N_MICROBATCH) + t.shape[1:])
    return _jnp.moveaxis(t, 1, axis + 1)


def setup_inputs(seed: int = 0) -> dict:
    inp = _fwd_setup_inputs(seed)
    key = _jax.random.fold_in(_jax.random.key(seed), 7919)
    shape, _ = _output_shape()
    out = dict(inp)
    out["loss_target"] = _jax.random.normal(_jax.random.fold_in(key, 0), shape, _jnp.float32)
    for i, name in enumerate(TWIN_WEIGHTS):
        w = inp[name].astype(_jnp.float32)
        if MOMENT_SCALE is None:
            s = _jnp.sqrt(_jnp.mean(_jnp.square(w)) + 1e-30)
        else:
            s = MOMENT_SCALE[name]
        km, kv = _jax.random.split(_jax.random.fold_in(key, i + 1))
        out[name] = w
        out["m_" + name] = s * _jax.random.normal(km, w.shape, _jnp.float32)
        out["v_" + name] = (s * s) * _jax.random.uniform(kv, w.shape, _jnp.float32, 0.5, 1.5)
    if N_MICROBATCH > 1:
        for name, axis in PER_EXAMPLE_BATCH_AXIS.items():
            out[name] = _to_microbatches(out[name], axis)
    return {'x': out['x'], 'positions': out['positions'], 'pool_norm': out['pool_norm'], 'pool_w_in': out['pool_w_in'], 'pool_w_grp': out['pool_w_grp'], 'pool_scale': out['pool_scale'], 'pool_w_out': out['pool_w_out'], 'conv_norm': out['conv_norm'], 'conv_w_in': out['conv_w_in'], 'conv_w': out['conv_w'], 'conv_w_out': out['conv_w_out'], 'mla_norm': out['mla_norm'], 'mla_w_in': out['mla_w_in'], 'mla_q_norm': out['mla_q_norm'], 'mla_w_q_up': out['mla_w_q_up'], 'mla_kv_norm': out['mla_kv_norm'], 'mla_w_kv_up': out['mla_w_kv_up'], 'mla_w_out': out['mla_w_out'], 'final_norm': out['final_norm'], 'loss_target': out['loss_target'], 'm_pool_norm': out['m_pool_norm'], 'm_pool_w_in': out['m_pool_w_in'], 'm_pool_w_grp': out['m_pool_w_grp'], 'm_pool_scale': out['m_pool_scale'], 'm_pool_w_out': out['m_pool_w_out'], 'm_conv_norm': out['m_conv_norm'], 'm_conv_w_in': out['m_conv_w_in'], 'm_conv_w': out['m_conv_w'], 'm_conv_w_out': out['m_conv_w_out'], 'm_mla_norm': out['m_mla_norm'], 'm_mla_w_in': out['m_mla_w_in'], 'm_mla_q_norm': out['m_mla_q_norm'], 'm_mla_w_q_up': out['m_mla_w_q_up'], 'm_mla_kv_norm': out['m_mla_kv_norm'], 'm_mla_w_kv_up': out['m_mla_w_kv_up'], 'm_mla_w_out': out['m_mla_w_out'], 'm_final_norm': out['m_final_norm'], 'v_pool_norm': out['v_pool_norm'], 'v_pool_w_in': out['v_pool_w_in'], 'v_pool_w_grp': out['v_pool_w_grp'], 'v_pool_scale': out['v_pool_scale'], 'v_pool_w_out': out['v_pool_w_out'], 'v_conv_norm': out['v_conv_norm'], 'v_conv_w_in': out['v_conv_w_in'], 'v_conv_w': out['v_conv_w'], 'v_conv_w_out': out['v_conv_w_out'], 'v_mla_norm': out['v_mla_norm'], 'v_mla_w_in': out['v_mla_w_in'], 'v_mla_q_norm': out['v_mla_q_norm'], 'v_mla_w_q_up': out['v_mla_w_q_up'], 'v_mla_kv_norm': out['v_mla_kv_norm'], 'v_mla_w_kv_up': out['v_mla_w_kv_up'], 'v_mla_w_out': out['v_mla_w_out'], 'v_final_norm': out['v_final_norm']}


def _loss(weights, diff, rest, loss_target):
    with _jax.named_scope("forward"):
        args = {**rest, TWIN_DIFF_INPUT: diff, **{k: w.astype(_WEIGHT_DTYPES[k]) for k, w in weights.items()}}
        y = _forward(args)
    with _jax.named_scope("loss_head"):
        err = _jnp.square(y.astype(_jnp.float32) - loss_target)
        return 0.5 * _jnp.sum(_jnp.mean(err, axis=-1)) if err.ndim else 0.5 * err


def _adamw(w, g, m, v):
    m = ADAM_B1 * m + (1.0 - ADAM_B1) * g
    v = ADAM_B2 * v + (1.0 - ADAM_B2) * _jnp.square(g)
    m_hat = m / (1.0 - ADAM_B1 ** ADAM_STEP)
    v_hat = v / (1.0 - ADAM_B2 ** ADAM_STEP)
    delta = -ADAM_LR * (m_hat / (_jnp.sqrt(v_hat) + ADAM_EPS) + ADAM_WD * w)
    return delta, m, v


def reference(x, positions, pool_norm, pool_w_in, pool_w_grp, pool_scale, pool_w_out, conv_norm, conv_w_in, conv_w, conv_w_out, mla_norm, mla_w_in, mla_q_norm, mla_w_q_up, mla_kv_norm, mla_w_kv_up, mla_w_out, final_norm, loss_target, m_pool_norm, m_pool_w_in, m_pool_w_grp, m_pool_scale, m_pool_w_out, m_conv_norm, m_conv_w_in, m_conv_w, m_conv_w_out, m_mla_norm, m_mla_w_in, m_mla_q_norm, m_mla_w_q_up, m_mla_kv_norm, m_mla_w_kv_up, m_mla_w_out, m_final_norm, v_pool_norm, v_pool_w_in, v_pool_w_grp, v_pool_scale, v_pool_w_out, v_conv_norm, v_conv_w_in, v_conv_w, v_conv_w_out, v_mla_norm, v_mla_w_in, v_mla_q_norm, v_mla_w_q_up, v_mla_kv_norm, v_mla_w_kv_up, v_mla_w_out, v_final_norm):
    given = dict(x=x, positions=positions, pool_norm=pool_norm, pool_w_in=pool_w_in, pool_w_grp=pool_w_grp, pool_scale=pool_scale, pool_w_out=pool_w_out, conv_norm=conv_norm, conv_w_in=conv_w_in, conv_w=conv_w, conv_w_out=conv_w_out, mla_norm=mla_norm, mla_w_in=mla_w_in, mla_q_norm=mla_q_norm, mla_w_q_up=mla_w_q_up, mla_kv_norm=mla_kv_norm, mla_w_kv_up=mla_w_kv_up, mla_w_out=mla_w_out, final_norm=final_norm, loss_target=loss_target, m_pool_norm=m_pool_norm, m_pool_w_in=m_pool_w_in, m_pool_w_grp=m_pool_w_grp, m_pool_scale=m_pool_scale, m_pool_w_out=m_pool_w_out, m_conv_norm=m_conv_norm, m_conv_w_in=m_conv_w_in, m_conv_w=m_conv_w, m_conv_w_out=m_conv_w_out, m_mla_norm=m_mla_norm, m_mla_w_in=m_mla_w_in, m_mla_q_norm=m_mla_q_norm, m_mla_w_q_up=m_mla_w_q_up, m_mla_kv_norm=m_mla_kv_norm, m_mla_w_kv_up=m_mla_w_kv_up, m_mla_w_out=m_mla_w_out, m_final_norm=m_final_norm, v_pool_norm=v_pool_norm, v_pool_w_in=v_pool_w_in, v_pool_w_grp=v_pool_w_grp, v_pool_scale=v_pool_scale, v_pool_w_out=v_pool_w_out, v_conv_norm=v_conv_norm, v_conv_w_in=v_conv_w_in, v_conv_w=v_conv_w, v_conv_w_out=v_conv_w_out, v_mla_norm=v_mla_norm, v_mla_w_in=v_mla_w_in, v_mla_q_norm=v_mla_q_norm, v_mla_w_q_up=v_mla_w_q_up, v_mla_kv_norm=v_mla_kv_norm, v_mla_w_kv_up=v_mla_w_kv_up, v_mla_w_out=v_mla_w_out, v_final_norm=v_final_norm)
    weights = {n: given[n] for n in TWIN_WEIGHTS}
    shared = {n: given[n] for n in SHARED_INPUTS}
    per_example = {n: given[n] for n in ['x', 'positions']}
    grad_fn = _jax.value_and_grad(_loss, argnums=(0, 1))

    def one_microbatch(ex, loss_target):
        ex = dict(ex)
        diff = ex.pop(TWIN_DIFF_INPUT)
        return grad_fn(weights, diff, {**shared, **ex}, loss_target)

    if N_MICROBATCH == 1:
        loss, (grad_w, grad_x) = one_microbatch(per_example, given["loss_target"])
    else:
        def body(carry, xs):
            loss_sum, grad_sum = carry
            l_k, (gw_k, gx_k) = one_microbatch(xs[0], xs[1])
            with _jax.named_scope("update"):
                return (loss_sum + l_k, _jax.tree.map(_jnp.add, grad_sum, gw_k)), gx_k

        init = (_jnp.zeros((), _jnp.float32), _jax.tree.map(_jnp.zeros_like, weights))
        (loss, grad_w), grad_x = _jax.lax.scan(body, init, (per_example, given["loss_target"]))
    with _jax.named_scope("update"):
        delta_w, new_m, new_v = {}, {}, {}
        for n in TWIN_WEIGHTS:
            delta_w[n], new_m[n], new_v[n] = _adamw(weights[n], grad_w[n], given["m_" + n], given["v_" + n])
    return (loss, grad_x, *[grad_w[n] for n in TWIN_WEIGHTS], *[delta_w[n] for n in TWIN_WEIGHTS],
            *[new_m[n] for n in TWIN_WEIGHTS], *[new_v[n] for n in TWIN_WEIGHTS])
```

```python
import functools

import jax
import jax.numpy as jnp
from jax import lax
from jax.experimental import pallas as pl
from jax.experimental.pallas import tpu as pltpu

BF = jnp.bfloat16
F32 = jnp.float32

N_DEV = 8
D_MODEL = 1024
D_INNER = 2048
POOL_WINDOWS = (2, 4, 8, 16)
POOL_GROUP = 512
N_HEADS = 16
QK_NOPE = 128
QK_ROPE = 64
QK_DIM = QK_NOPE + QK_ROPE
V_DIM = 128
Q_RANK = 384
KV_RANK = 256
ATTN_SCALE = QK_DIM ** -0.5
ROPE_BASE = 10000.0
NORM_EPS = 1e-6
NEG_BIG = -1e30

ADAM_LR = 0.001
ADAM_B1 = 0.9
ADAM_B2 = 0.999
ADAM_EPS = 1e-08
ADAM_WD = 0.01
ADAM_STEP = 10

VMEM_LIMIT_BYTES = 52 * 1024 * 1024
POOL_HALO = 32
CONV_HALO = 8

NN = (((1,), (0,)), ((), ()))
NT = (((1,), (1,)), ((), ()))
TN = (((0,), (0,)), ((), ()))

R512_W_IN, R512_KV, R512_GRP, R512_ROWS = 0, 2048, 2304, 2816
R1024_CONV_IN, R1024_POOL_OUT, R1024_CONV_OUT, R1024_MLA_OUT, R1024_ROWS = 0, 1024, 1536, 1792, 2048
SMALL_ROWS_AG = 16
SMALL_ROWS_RS = 32


def _sds(shape, dtype):
    return jax.ShapeDtypeStruct(tuple(shape), dtype)


def _params():
    return pltpu.CompilerParams(vmem_limit_bytes=VMEM_LIMIT_BYTES)


def _dot(a, b, dims):
    return lax.dot_general(a, b, dims, preferred_element_type=F32)


def _sig(z):
    return 1.0 / (1.0 + jnp.exp(-z))


def _silu_and_grad(z):
    sig = _sig(z)
    return z * sig, sig * (1.0 + z * (1.0 - sig))


def _rope_swap(x, p):
    return jnp.dot(x, p, precision=lax.Precision.HIGHEST, preferred_element_type=F32)


def _rope_fwd(x, cosf, sinf, p):
    return x * cosf + _rope_swap(x, p) * sinf


def _rope_bwd(dy, cosf, sinf, p):
    return dy * cosf + _rope_swap(dy * sinf, p)


def _rms_bwd(dxn, x, g, res):
    r = lax.rsqrt(jnp.mean(x * x, axis=-1, keepdims=True) + NORM_EPS)
    v = dxn * g
    dx = r * v - x * ((r * r * r) * jnp.mean(v * x, axis=-1, keepdims=True))
    if res is not None:
        dx = dx + res
    dg = jnp.sum(dxn * (x * r), axis=0, keepdims=True)
    return dx, dg


def _accumulate(ref, val, step):
    @pl.when(step == 0)
    def _():
        ref[...] = val

    @pl.when(step > 0)
    def _():
        ref[...] += val


def _mm(name, grid, ins, in_specs, outs, out_specs, dims, epi, red=None, acc_shape=None):
    n_in, n_out = len(ins), len(outs)
    n_red = None if red is None else grid[red]

    def body(*refs):
        in_refs, out_refs = refs[:n_in], refs[n_in:n_in + n_out]
        pids = tuple(pl.program_id(ax) for ax in range(len(grid)))
        a, b = in_refs[0][...], in_refs[1][...]
        if a.ndim == 3:
            a = a.reshape(-1, a.shape[-1])
        if b.ndim == 3:
            b = b.reshape(-1, b.shape[-1])
        part = _dot(a.astype(BF), b.astype(BF), dims)
        if red is None:
            epi(part, in_refs[2:], out_refs, pids)
        else:
            acc = refs[n_in + n_out]
            k = pids[red]
            _accumulate(acc, part, k)

            @pl.when(k == n_red - 1)
            def _():
                epi(acc[...], in_refs[2:], out_refs, pids)

    scratch = [] if red is None else [pltpu.VMEM(acc_shape, F32)]
    return pl.pallas_call(body, name=name, grid=grid, in_specs=in_specs, out_specs=out_specs, out_shape=outs,
                          scratch_shapes=scratch, compiler_params=_params())(*ins)


def _store(part, extra, outs, pids):
    outs[0][...] = part.astype(outs[0].dtype)


def _rms_fwd(name, x, g, tm):
    s, d = x.shape

    def body(x_ref, g_ref, o_ref):
        xv = x_ref[...]
        r = lax.rsqrt(jnp.mean(xv * xv, axis=-1, keepdims=True) + NORM_EPS)
        o_ref[...] = ((xv * r) * g_ref[...]).astype(BF)

    return pl.pallas_call(body, name=name, grid=(s // tm,),
                          in_specs=[pl.BlockSpec((tm, d), lambda i: (i, 0)), pl.BlockSpec((1, d), lambda i: (0, 0))],
                          out_specs=pl.BlockSpec((tm, d), lambda i: (i, 0)), out_shape=_sds((s, d), BF),
                          compiler_params=_params())(x, g.reshape(1, d))


def _tn(name, a, b, out_shape, out_block, out_index, a_cols, b_cols, grid, a_index, b_index):
    s = a.shape[-2]
    a_block = (s, a_cols) if a.ndim == 2 else (None, s, a_cols)
    b_block = (s, b_cols) if b.ndim == 2 else (None, s, b_cols)

    def epi(part, extra, outs, pids):
        outs[0][...] = part.astype(BF).reshape(outs[0].shape)

    return _mm(name, grid, [a, b], [pl.BlockSpec(a_block, a_index), pl.BlockSpec(b_block, b_index)],
               [_sds(out_shape, BF)], [pl.BlockSpec(out_block, out_index)], TN, epi)[0]


def _pool_window_fwd(name, h, tm):
    s = h.shape[0]
    hb = POOL_HALO

    def body(u_ref, halo_ref, o_ref, e_ref, a_ref, b_ref):
        i = pl.program_id(0)
        row = lax.broadcasted_iota(jnp.int32, (tm, 1), 0) + i * tm
        for g, w in enumerate(POOL_WINDOWS):
            cs = slice(g * POOL_GROUP, (g + 1) * POOL_GROUP)
            e_ref[0:hb, :] = jnp.where(i > 0, halo_ref[:, cs], 0.0)
            e_ref[hb:, :] = u_ref[:, cs]
            src, bufs = e_ref, (a_ref, b_ref)
            for lv in range(1, w.bit_length()):
                dst, st, sh = bufs[(lv - 1) % 2], 8 * lv, 2 ** (lv - 1)
                n = hb + tm - st
                dst[st:, :] = src[st:, :] + src[pl.ds(st - sh, n), :]
                src = dst
            cnt = jnp.minimum(row + 1, w).astype(F32)
            o_ref[:, cs] = (src[hb:, :] / cnt - u_ref[:, cs]).astype(BF)

    per = tm // hb
    return pl.pallas_call(
        body, name=name, grid=(s // tm,),
        in_specs=[pl.BlockSpec((tm, D_INNER), lambda i: (i, 0)),
                  pl.BlockSpec((hb, D_INNER), lambda i: (jnp.maximum(i * per - 1, 0), 0))],
        out_specs=pl.BlockSpec((tm, D_INNER), lambda i: (i, 0)), out_shape=_sds((s, D_INNER), BF),
        scratch_shapes=[pltpu.VMEM((hb + tm, POOL_GROUP), F32)] * 3, compiler_params=_params())(h, h)


def _pool_window_bwd(name, dp, tm):
    s = dp.shape[0]
    nt = s // tm
    hb = POOL_HALO

    def body(d_ref, halo_ref, o_ref, e_ref, a_ref, b_ref):
        i = pl.program_id(0)
        row = lax.broadcasted_iota(jnp.int32, (tm, 1), 0) + i * tm
        hrow = lax.broadcasted_iota(jnp.int32, (hb, 1), 0) + (i + 1) * tm
        for g, w in enumerate(POOL_WINDOWS):
            cs = slice(g * POOL_GROUP, (g + 1) * POOL_GROUP)
            e_ref[0:tm, :] = d_ref[:, cs] / jnp.minimum(row + 1, w).astype(F32)
            e_ref[tm:, :] = jnp.where(i < nt - 1, halo_ref[:, cs] / jnp.minimum(hrow + 1, w).astype(F32), 0.0)
            src, bufs = e_ref, (a_ref, b_ref)
            for lv in range(1, w.bit_length()):
                dst, sh = bufs[(lv - 1) % 2], 2 ** (lv - 1)
                n = tm + hb - 8 * lv
                dst[0:n, :] = src[0:n, :] + src[pl.ds(sh, n), :]
                src = dst
            o_ref[:, cs] = (src[0:tm, :] - d_ref[:, cs]).astype(BF)

    per = tm // hb
    last = s // hb - 1
    return pl.pallas_call(
        body, name=name, grid=(nt,),
        in_specs=[pl.BlockSpec((tm, D_INNER), lambda i: (i, 0)),
                  pl.BlockSpec((hb, D_INNER), lambda i: (jnp.minimum((i + 1) * per, last), 0))],
        out_specs=pl.BlockSpec((tm, D_INNER), lambda i: (i, 0)), out_shape=_sds((s, D_INNER), BF),
        scratch_shapes=[pltpu.VMEM((hb + tm, POOL_GROUP), F32)] * 3, compiler_params=_params())(dp, dp)


def _grp_block(l):
    base = R512_GRP // 64 + l * 4
    return pl.BlockSpec((N_DEV, 64, POOL_GROUP), lambda i, g: (0, base + g, 0))


def _pool_fwd(x, l, w, tm):
    s = x.shape[0]
    nt = s // tm
    n = f"pool{l}"
    xn = _rms_fwd(n + "_rms", x, w["pool_norm"][l], tm)
    (h,) = _mm(n + "_in", (nt, 8), [xn, w["g512"]],
               [pl.BlockSpec((tm, D_MODEL), lambda i, j: (i, 0)), pl.BlockSpec((None, D_MODEL, 512), lambda i, j: (j, l, 0))],
               [_sds((s, 2 * D_INNER), F32)], [pl.BlockSpec((tm, 512), lambda i, j: (i, j))], NN, _store)
    pooled = _pool_window_fwd(n + "_win", h, tm)

    def gate(part, extra, outs, pids):
        z = extra[0][...]
        outs[0][...] = ((part * extra[1][...]) * (z * _sig(z))).astype(BF)

    (gated,) = _mm(n + "_grp", (nt, 4), [pooled, w["g512"], h, w["pool_scale"][l].reshape(1, D_INNER)],
                   [pl.BlockSpec((tm, 512), lambda i, g: (i, g)), _grp_block(l),
                    pl.BlockSpec((tm, 512), lambda i, g: (i, 4 + g)), pl.BlockSpec((1, 512), lambda i, g: (0, g))],
                   [_sds((s, D_INNER), BF)], [pl.BlockSpec((tm, 512), lambda i, g: (i, g))], NN, gate)
    y = _out_proj(n + "_out", gated, w["g1024"], R1024_POOL_OUT // 256 + l, x, tm)
    return y, dict(x=x, xn=xn, h=h, pooled=pooled, gated=gated)


def _out_proj(name, gated, g1024, row_block, x, tm):
    s = x.shape[0]

    def epi(part, extra, outs, pids):
        outs[0][...] = part + extra[0][...]

    return _mm(name, (s // tm, 2), [gated, g1024, x],
               [pl.BlockSpec((tm, D_INNER), lambda i, j: (i, 0)), pl.BlockSpec((N_DEV, 256, 512), lambda i, j: (0, row_block, j)),
                pl.BlockSpec((tm, 512), lambda i, j: (i, j))],
               [_sds((s, D_MODEL), F32)], [pl.BlockSpec((tm, 512), lambda i, j: (i, j))], NN, epi)[0]


def _w_out_nt_block(row_block):
    return pl.BlockSpec((2, 256, D_MODEL), lambda j, i: (j, row_block, 0))


def _in_proj_bwd(name, dh, wbuf, w_index, n_k, x, g, dy, tm):
    s = x.shape[0]

    def epi(acc, extra, outs, pids):
        dx, dg = _rms_bwd(acc, extra[0][...], extra[1][...], extra[2][...])
        outs[0][...] = dx
        outs[1][...] = dx.astype(BF)
        _accumulate(outs[2], dg, pids[0])

    row = lambda i, k: (i, 0)
    return _mm(name, (s // tm, n_k), [dh, wbuf, x, g.reshape(1, D_MODEL), dy],
               [pl.BlockSpec((tm, 512), lambda i, k: (i, k)), pl.BlockSpec((None, D_MODEL, 512), w_index),
                pl.BlockSpec((tm, D_MODEL), row), pl.BlockSpec((1, D_MODEL), lambda i, k: (0, 0)), pl.BlockSpec((tm, D_MODEL), row)],
               [_sds((s, D_MODEL), F32), _sds((s, D_MODEL), BF), _sds((1, D_MODEL), F32)],
               [pl.BlockSpec((tm, D_MODEL), row), pl.BlockSpec((tm, D_MODEL), row), pl.BlockSpec((1, D_MODEL), lambda i, k: (0, 0))],
               NT, epi, red=1, acc_shape=(tm, D_MODEL))


def _w_out_grad(name, gated, dyb):
    s = gated.shape[0]
    return _tn(name, gated, dyb, (D_INNER, D_MODEL), (512, D_MODEL), lambda i: (i, 0), 512, D_MODEL, (4,),
               lambda i: (0, i), lambda i: (0, 0))


def _pool_bwd(dy, dyb, l, w, sv, tm):
    s = dy.shape[0]
    nt = s // tm
    n = f"pool{l}b"
    h, pooled = sv["h"], sv["pooled"]
    scale = w["pool_scale"][l].reshape(1, D_INNER)
    (mp,) = _mm(n + "_grp", (nt, 4), [pooled, w["g512"]],
                [pl.BlockSpec((tm, 512), lambda i, g: (i, g)), _grp_block(l)],
                [_sds((s, D_INNER), F32)], [pl.BlockSpec((tm, 512), lambda i, g: (i, g))], NN, _store)

    def gate_bwd(part, extra, outs, pids):
        z, mpv, sc = extra[0][...], extra[1][...], extra[2][...]
        sz, dsz = _silu_and_grad(z)
        dm = part * sz
        outs[0][...] = (dm * sc).astype(BF)
        outs[1][...] = (part * (mpv * sc) * dsz).astype(BF)
        _accumulate(outs[2], jnp.sum(dm * mpv, axis=0, keepdims=True), pids[1])

    tile = lambda j, i: (i, j)
    dmp, dz, dscale = _mm(
        n + "_out", (4, nt), [dyb, w["g1024"], h, mp, scale],
        [pl.BlockSpec((tm, D_MODEL), lambda j, i: (i, 0)), _w_out_nt_block(R1024_POOL_OUT // 256 + l),
         pl.BlockSpec((tm, 512), lambda j, i: (i, 4 + j)), pl.BlockSpec((tm, 512), tile), pl.BlockSpec((1, 512), lambda j, i: (0, j))],
        [_sds((s, D_INNER), BF), _sds((s, D_INNER), BF), _sds((1, D_INNER), F32)],
        [pl.BlockSpec((tm, 512), tile), pl.BlockSpec((tm, 512), tile), pl.BlockSpec((1, 512), lambda j, i: (0, j))], NT, gate_bwd)
    (dpool,) = _mm(n + "_grpT", (nt, 4), [dmp, w["g512"]],
                   [pl.BlockSpec((tm, 512), lambda i, g: (i, g)), _grp_block(l)],
                   [_sds((s, D_INNER), F32)], [pl.BlockSpec((tm, 512), lambda i, g: (i, g))], NT, _store)
    du = _pool_window_bwd(n + "_win", dpool, tm)
    dh = jnp.concatenate([du, dz], axis=1)
    dx, dxb, dnorm = _in_proj_bwd(n + "_in", dh, w["g512"], lambda i, k: (k, l, 0), 8, sv["x"], w["pool_norm"][l], dy, tm)
    g_in = _tn(n + "_gin", sv["xn"], dh, (N_DEV, D_MODEL, 512), (None, D_MODEL, 512), lambda j: (j, 0, 0),
               D_MODEL, 512, (8,), lambda j: (0, 0), lambda j: (0, j))
    g_out = _w_out_grad(n + "_gout", sv["gated"], dyb)
    g_grp = _tn(n + "_ggrp", pooled, dmp, (N_DEV, 256, 512), (N_DEV, 64, 512), lambda g: (0, g, 0),
                512, 512, (4,), lambda g: (0, g), lambda g: (0, g))
    return dx, dxb, dict(w_in=g_in, w_out=g_out, w_grp=g_grp, norm=dnorm[0], scale=dscale[0])


def _conv_in_index(i, j):
    return (j // 2, 0, j % 2)


def _conv_fwd(x, w, tm):
    s = x.shape[0]
    nt = s // tm
    xn = _rms_fwd("conv_rms", x, w["conv_norm"], tm)
    (h,) = _mm("conv_in", (nt, 16), [xn, w["g1024"]],
               [pl.BlockSpec((tm, D_MODEL), lambda i, j: (i, 0)), pl.BlockSpec((None, D_MODEL, 512), _conv_in_index)],
               [_sds((s, 4 * D_INNER), F32)], [pl.BlockSpec((tm, 512), lambda i, j: (i, j))], NN, _store)
    per = tm // CONV_HALO

    def body(b_ref, c_ref, h_ref, z_ref, cp_ref, hp_ref, w_ref, o_ref, e_ref):
        i = pl.program_id(0)
        ch = c_ref[...] * h_ref[...]
        e_ref[0:CONV_HALO, :] = jnp.where(i > 0, cp_ref[...] * hp_ref[...], 0.0)
        e_ref[CONV_HALO:, :] = ch
        co = (w_ref[2:3, :] * ch + w_ref[1:2, :] * e_ref[pl.ds(CONV_HALO - 1, tm), :]
              + w_ref[0:1, :] * e_ref[pl.ds(CONV_HALO - 2, tm), :])
        z = z_ref[...]
        o_ref[...] = ((b_ref[...] * co) * (z * _sig(z))).astype(BF)

    def col(q):
        return pl.BlockSpec((tm, 512), lambda i, j: (i, 4 * q + j))

    def prev(q):
        return pl.BlockSpec((CONV_HALO, 512), lambda i, j: (jnp.maximum(i * per - 1, 0), 4 * q + j))

    gated = pl.pallas_call(
        body, name="conv_mix", grid=(nt, 4),
        in_specs=[col(0), col(1), col(2), col(3), prev(1), prev(2), pl.BlockSpec((3, 512), lambda i, j: (0, j))],
        out_specs=pl.BlockSpec((tm, 512), lambda i, j: (i, j)), out_shape=_sds((s, D_INNER), BF),
        scratch_shapes=[pltpu.VMEM((CONV_HALO + tm, 512), F32)], compiler_params=_params())(h, h, h, h, h, h, w["conv_w"])
    y = _out_proj("conv_out", gated, w["g1024"], R1024_CONV_OUT // 256, x, tm)
    return y, dict(x=x, xn=xn, h=h, gated=gated)


def _conv_bwd(dy, dyb, w, sv, tm):
    s = dy.shape[0]
    nt = s // tm
    h = sv["h"]
    (dg,) = _mm("convb_out", (4, nt), [dyb, w["g1024"]],
                [pl.BlockSpec((tm, D_MODEL), lambda j, i: (i, 0)), _w_out_nt_block(R1024_CONV_OUT // 256)],
                [_sds((s, D_INNER), F32)], [pl.BlockSpec((tm, 512), lambda j, i: (i, j))], NT, _store)
    per = tm // CONV_HALO
    last = s // CONV_HALO - 1

    def body(dg_ref, b_ref, c_ref, h_ref, z_ref, cp_ref, hp_ref, dgn_ref, bn_ref, zn_ref, w_ref,
             db_ref, dc_ref, dh_ref, dz_ref, dw_ref, e_ref, f_ref):
        i = pl.program_id(1)
        w0, w1, w2 = w_ref[0:1, :], w_ref[1:2, :], w_ref[2:3, :]
        c, hh, b = c_ref[...], h_ref[...], b_ref[...]
        ch = c * hh
        e_ref[0:CONV_HALO, :] = jnp.where(i > 0, cp_ref[...] * hp_ref[...], 0.0)
        e_ref[CONV_HALO:, :] = ch
        ch1 = e_ref[pl.ds(CONV_HALO - 1, tm), :]
        ch2 = e_ref[pl.ds(CONV_HALO - 2, tm), :]
        co = w2 * ch + w1 * ch1 + w0 * ch2
        sz, dsz = _silu_and_grad(z_ref[...])
        dgv = dg_ref[...]
        dyv = dgv * sz
        dz_ref[...] = (dgv * (b * co) * dsz).astype(BF)
        db_ref[...] = (dyv * co).astype(BF)
        dco = dyv * b
        zn = zn_ref[...]
        f_ref[0:tm, :] = dco
        f_ref[tm:, :] = jnp.where(i < nt - 1, dgn_ref[...] * (zn * _sig(zn)) * bn_ref[...], 0.0)
        dch = w2 * dco + w1 * f_ref[pl.ds(1, tm), :] + w0 * f_ref[pl.ds(2, tm), :]
        dc_ref[...] = (dch * hh).astype(BF)
        dh_ref[...] = (dch * c).astype(BF)
        for tap, shifted in enumerate((ch2, ch1, ch)):
            _accumulate(dw_ref.at[tap:tap + 1, :], jnp.sum(dco * shifted, axis=0, keepdims=True), i)

    def col(q):
        return pl.BlockSpec((tm, 512), lambda j, i: (i, 4 * q + j))

    def prev(q):
        return pl.BlockSpec((CONV_HALO, 512), lambda j, i: (jnp.maximum(i * per - 1, 0), 4 * q + j))

    def nxt(q):
        return pl.BlockSpec((CONV_HALO, 512), lambda j, i: (jnp.minimum((i + 1) * per, last), 4 * q + j))

    tile = pl.BlockSpec((tm, 512), lambda j, i: (i, j))
    wspec = pl.BlockSpec((3, 512), lambda j, i: (0, j))
    db, dc, dhh, dz, dw = pl.pallas_call(
        body, name="convb_mix", grid=(4, nt),
        in_specs=[tile, col(0), col(1), col(2), col(3), prev(1), prev(2), nxt(0), nxt(0), nxt(3), wspec],
        out_specs=[tile, tile, tile, tile, wspec],
        out_shape=[_sds((s, D_INNER), BF)] * 4 + [_sds((3, D_INNER), F32)],
        scratch_shapes=[pltpu.VMEM((CONV_HALO + tm, 512), F32)] * 2, compiler_params=_params(),
    )(dg, h, h, h, h, h, h, dg, h, h, w["conv_w"])
    dh = jnp.concatenate([db, dc, dhh, dz], axis=1)
    dx, dxb, dnorm = _in_proj_bwd("convb_in", dh, w["g1024"], lambda i, k: (k // 2, 0, k % 2), 16, sv["x"], w["conv_norm"], dy, tm)
    g_in = _tn("convb_gin", sv["xn"], dh, (N_DEV, D_MODEL, D_MODEL), (None, D_MODEL, 512), lambda j: (j // 2, 0, j % 2),
               D_MODEL, 512, (16,), lambda j: (0, 0), lambda j: (0, j))
    g_out = _w_out_grad("convb_gout", sv["gated"], dyb)
    return dx, dxb, dict(w_in=g_in, w_out=g_out, norm=dnorm[0], conv_w=dw)


def _attn_tiles(s):
    t = min(256, s)
    return t, s // t


def _mla_fwd(x, w, rope, tm):
    s = x.shape[0]
    nt = s // tm
    cosf, sinf, perm = rope
    xn = _rms_fwd("mla_rms", x, w["mla_norm"], tm)

    def in_body(xn_ref, wq_ref, wkv_ref, wkr_ref, wz_ref, gq_ref, gkv_ref, cos_ref, sin_ref, p_ref,
                ql_ref, kvl_ref, qn_ref, kvn_ref, krr_ref, z_ref):
        xv = xn_ref[...]
        ql = _dot(xv, wq_ref[...], NN)
        kvl = _dot(xv, wkv_ref[...], NN)
        ql_ref[...] = ql
        kvl_ref[...] = kvl
        rq = lax.rsqrt(jnp.mean(ql * ql, axis=-1, keepdims=True) + NORM_EPS)
        qn_ref[...] = ((ql * rq) * gq_ref[...]).astype(BF)
        rkv = lax.rsqrt(jnp.mean(kvl * kvl, axis=-1, keepdims=True) + NORM_EPS)
        kvn_ref[...] = ((kvl * rkv) * gkv_ref[...]).astype(BF)
        kr = _dot(xv, wkr_ref[...], NN)
        krr_ref[...] = _rope_fwd(kr, cos_ref[...], sin_ref[...], p_ref[...]).astype(BF)
        z_ref[...] = _dot(xv, wz_ref[...], NN)

    def full(a):
        return pl.BlockSpec(a.shape, lambda i: (0,) * a.ndim)

    def rows(c):
        return pl.BlockSpec((tm, c), lambda i: (i, 0))

    gq, gkv = w["q_norm"].reshape(1, Q_RANK), w["kv_norm"].reshape(1, KV_RANK)
    q_lat, kv_lat, qn, kvn, krr, z = pl.pallas_call(
        in_body, name="mla_in", grid=(nt,),
        in_specs=[rows(D_MODEL), full(w["w_q"]), full(w["w_kv"]), full(w["w_kr"]), full(w["w_z"]), full(gq), full(gkv),
                  rows(QK_ROPE), rows(QK_ROPE), full(perm)],
        out_specs=[rows(Q_RANK), rows(KV_RANK), rows(Q_RANK), rows(KV_RANK), rows(QK_ROPE), rows(D_INNER)],
        out_shape=[_sds((s, Q_RANK), F32), _sds((s, KV_RANK), F32), _sds((s, Q_RANK), BF), _sds((s, KV_RANK), BF),
                   _sds((s, QK_ROPE), BF), _sds((s, D_INNER), F32)],
        compiler_params=_params())(xn, w["w_q"], w["w_kv"], w["w_kr"], w["w_z"], gq, gkv, cosf, sinf, perm)

    def q_epi(part, extra, outs, pids):
        outs[0][:, 0:QK_NOPE] = part[:, 0:QK_NOPE].astype(BF)
        outs[0][:, QK_NOPE:QK_DIM] = _rope_fwd(part[:, QK_NOPE:QK_DIM], extra[0][...], extra[1][...], extra[2][...]).astype(BF)

    rope_row = pl.BlockSpec((tm, QK_ROPE), lambda h, i: (i, 0))
    (q,) = _mm("mla_qup", (N_HEADS, nt), [qn, w["w_qh"], cosf, sinf, perm],
               [pl.BlockSpec((tm, Q_RANK), lambda h, i: (i, 0)), pl.BlockSpec((None, Q_RANK, QK_DIM), lambda h, i: (h, 0, 0)),
                rope_row, rope_row, pl.BlockSpec((QK_ROPE, QK_ROPE), lambda h, i: (0, 0))],
               [_sds((N_HEADS, s, QK_DIM), BF)], [pl.BlockSpec((None, tm, QK_DIM), lambda h, i: (h, i, 0))], NN, q_epi)

    def kv_epi(part, extra, outs, pids):
        outs[0][:, 0:QK_NOPE] = part[:, 0:QK_NOPE].astype(BF)
        outs[0][:, QK_NOPE:QK_DIM] = extra[0][...]
        outs[1][...] = part[:, QK_NOPE:].astype(BF)

    k, v = _mm("mla_kvup", (N_HEADS, nt), [kvn, w["g512"], krr],
               [pl.BlockSpec((tm, KV_RANK), lambda h, i: (i, 0)),
                pl.BlockSpec((None, KV_RANK, 256), lambda h, i: (h // 2, R512_KV // KV_RANK, h % 2)), rope_row],
               [_sds((N_HEADS, s, QK_DIM), BF), _sds((N_HEADS, s, V_DIM), BF)],
               [pl.BlockSpec((None, tm, QK_DIM), lambda h, i: (h, i, 0)), pl.BlockSpec((None, tm, V_DIM), lambda h, i: (h, i, 0))],
               NN, kv_epi)

    t, nq = _attn_tiles(s)

    def attn_body(q_ref, k_ref, v_ref, z_ref, o_ref, g_ref, lse_ref):
        i = pl.program_id(1)
        qv = q_ref[...]

        def block(j, carry, masked):
            m, lsum, acc = carry
            start = pl.multiple_of(j * t, t)
            sc = _dot(qv, k_ref[pl.ds(start, t), :], NT) * ATTN_SCALE
            if masked:
                keep = lax.broadcasted_iota(jnp.int32, (t, t), 1) <= lax.broadcasted_iota(jnp.int32, (t, t), 0)
                sc = jnp.where(keep, sc, NEG_BIG)
            mn = jnp.maximum(m, jnp.max(sc, axis=-1, keepdims=True))
            alpha = jnp.exp(m - mn)
            p = jnp.exp(sc - mn)
            lsum = alpha * lsum + jnp.sum(p, axis=-1, keepdims=True)
            acc = alpha * acc + _dot(p.astype(BF), v_ref[pl.ds(start, t), :], NN)
            return mn, lsum, acc

        init = (jnp.full((t, 1), NEG_BIG, F32), jnp.zeros((t, 1), F32), jnp.zeros((t, V_DIM), F32))
        carry = lax.fori_loop(0, i, lambda j, c: block(j, c, False), init)
        m, lsum, acc = block(i, carry, True)
        o = acc / lsum
        z = z_ref[...]
        o_ref[...] = o
        g_ref[...] = (o * (z * _sig(z))).astype(BF)
        lse_ref[...] = m + jnp.log(lsum)

    head_col = pl.BlockSpec((t, V_DIM), lambda h, i: (i, h))
    o, gated, lse = pl.pallas_call(
        attn_body, name="mla_attn", grid=(N_HEADS, nq),
        in_specs=[pl.BlockSpec((None, t, QK_DIM), lambda h, i: (h, i, 0)), pl.BlockSpec((None, s, QK_DIM), lambda h, i: (h, 0, 0)),
                  pl.BlockSpec((None, s, V_DIM), lambda h, i: (h, 0, 0)), head_col],
        out_specs=[head_col, head_col, pl.BlockSpec((None, t, 1), lambda h, i: (h, i, 0))],
        out_shape=[_sds((s, D_INNER), F32), _sds((s, D_INNER), BF), _sds((N_HEADS, s, 1), F32)],
        compiler_params=_params())(q, k, v, z)
    y = _out_proj("mla_out", gated, w["g1024"], R1024_MLA_OUT // 256, x, tm)
    return y, dict(x=x, xn=xn, q_lat=q_lat, kv_lat=kv_lat, qn=qn, kvn=kvn, z=z, q=q, k=k, v=v, o=o, lse=lse, gated=gated)


def _mla_bwd(dy, dyb, w, sv, rope, tm):
    s = dy.shape[0]
    nt = s // tm
    cosf, sinf, perm = rope
    t, nq = _attn_tiles(s)
    q, k, v, lse = sv["q"], sv["k"], sv["v"], sv["lse"]

    def gate_bwd(part, extra, outs, pids):
        z, o = extra[0][...], extra[1][...]
        sz, dsz = _silu_and_grad(z)
        do = part * sz
        outs[0][...] = do.astype(BF)
        outs[1][...] = (part * o * dsz).astype(BF)
        prod = do * o
        for hh in range(4):
            outs[2][hh] = jnp.sum(prod[:, hh * V_DIM:(hh + 1) * V_DIM], axis=-1, keepdims=True)

    tile = lambda j, i: (i, j)
    dob, dz, delta = _mm(
        "mlab_out", (4, nt), [dyb, w["g1024"], sv["z"], sv["o"]],
        [pl.BlockSpec((tm, D_MODEL), lambda j, i: (i, 0)), _w_out_nt_block(R1024_MLA_OUT // 256),
         pl.BlockSpec((tm, 512), tile), pl.BlockSpec((tm, 512), tile)],
        [_sds((s, D_INNER), BF), _sds((s, D_INNER), BF), _sds((N_HEADS, s, 1), F32)],
        [pl.BlockSpec((tm, 512), tile), pl.BlockSpec((tm, 512), tile), pl.BlockSpec((4, tm, 1), lambda j, i: (j, i, 0))],
        NT, gate_bwd)

    def probs(qb, kb, lse_b, masked):
        sc = _dot(qb, kb, NT) * ATTN_SCALE
        if masked:
            keep = lax.broadcasted_iota(jnp.int32, (t, t), 1) <= lax.broadcasted_iota(jnp.int32, (t, t), 0)
            sc = jnp.where(keep, sc, NEG_BIG)
        return jnp.exp(sc - lse_b)

    def dkv_body(k_ref, v_ref, q_ref, do_ref, lse_ref, dl_ref, dkv_ref, dkr_ref):
        j = pl.program_id(1)
        kb, vb = k_ref[...], v_ref[...]

        def block(i, carry, masked):
            dk, dv = carry
            rows = pl.ds(pl.multiple_of(i * t, t), t)
            qb, dob_ = q_ref[rows, :], do_ref[rows, :]
            p = probs(qb, kb, lse_ref[rows, :], masked)
            dv = dv + _dot(p.astype(BF), dob_, TN)
            ds = p * (_dot(dob_, vb, NT) - dl_ref[rows, :]) * ATTN_SCALE
            dk = dk + _dot(ds.astype(BF), qb, TN)
            return dk, dv

        carry = block(j, (jnp.zeros((t, QK_DIM), F32), jnp.zeros((t, V_DIM), F32)), True)
        dk, dv = lax.fori_loop(j + 1, nq, lambda i, c: block(i, c, False), carry)
        dkv_ref[:, 0:QK_NOPE] = dk[:, 0:QK_NOPE].astype(BF)
        dkv_ref[:, QK_NOPE:] = dv.astype(BF)
        dkr_ref[...] = dk[:, QK_NOPE:]

    seq1 = pl.BlockSpec((None, s, 1), lambda h, j: (h, 0, 0))
    dkv, dkr_h = pl.pallas_call(
        dkv_body, name="mlab_dkv", grid=(N_HEADS, nq),
        in_specs=[pl.BlockSpec((None, t, QK_DIM), lambda h, j: (h, j, 0)), pl.BlockSpec((None, t, V_DIM), lambda h, j: (h, j, 0)),
                  pl.BlockSpec((None, s, QK_DIM), lambda h, j: (h, 0, 0)), pl.BlockSpec((s, V_DIM), lambda h, j: (0, h)), seq1, seq1],
        out_specs=[pl.BlockSpec((t, 2 * V_DIM), lambda h, j: (j, h)), pl.BlockSpec((None, t, QK_ROPE), lambda h, j: (h, j, 0))],
        out_shape=[_sds((s, N_HEADS * 2 * V_DIM), BF), _sds((N_HEADS, s, QK_ROPE), F32)],
        compiler_params=_params())(k, v, q, dob, lse, delta)

    def dq_body(q_ref, do_ref, lse_ref, dl_ref, k_ref, v_ref, cos_ref, sin_ref, p_ref, dq_ref):
        i = pl.program_id(1)
        qb, dob_, lse_b, dl_b = q_ref[...], do_ref[...], lse_ref[...], dl_ref[...]

        def block(j, dq, masked):
            rows = pl.ds(pl.multiple_of(j * t, t), t)
            kb = k_ref[rows, :]
            p = probs(qb, kb, lse_b, masked)
            ds = p * (_dot(dob_, v_ref[rows, :], NT) - dl_b) * ATTN_SCALE
            return dq + _dot(ds.astype(BF), kb, NN)

        dq = lax.fori_loop(0, i, lambda j, c: block(j, c, False), jnp.zeros((t, QK_DIM), F32))
        dq = block(i, dq, True)
        dq_ref[:, 0:QK_NOPE] = dq[:, 0:QK_NOPE].astype(BF)
        dq_ref[:, QK_NOPE:] = _rope_bwd(dq[:, QK_NOPE:], cos_ref[...], sin_ref[...], p_ref[...]).astype(BF)

    col1 = pl.BlockSpec((None, t, 1), lambda h, i: (h, i, 0))
    rope_row = pl.BlockSpec((t, QK_ROPE), lambda h, i: (i, 0))
    dq = pl.pallas_call(
        dq_body, name="mlab_dq", grid=(N_HEADS, nq),
        in_specs=[pl.BlockSpec((None, t, QK_DIM), lambda h, i: (h, i, 0)), pl.BlockSpec((t, V_DIM), lambda h, i: (i, h)), col1, col1,
                  pl.BlockSpec((None, s, QK_DIM), lambda h, i: (h, 0, 0)), pl.BlockSpec((None, s, V_DIM), lambda h, i: (h, 0, 0)),
                  rope_row, rope_row, pl.BlockSpec((QK_ROPE, QK_ROPE), lambda h, i: (0, 0))],
        out_specs=pl.BlockSpec((None, t, QK_DIM), lambda h, i: (h, i, 0)), out_shape=_sds((N_HEADS, s, QK_DIM), BF),
        compiler_params=_params())(q, dob, lse, delta, k, v, cosf, sinf, perm)

    def dkr_body(d_ref, cos_ref, sin_ref, p_ref, o_ref):
        tot = d_ref[0]
        for hh in range(1, N_HEADS):
            tot = tot + d_ref[hh]
        o_ref[...] = _rope_bwd(tot, cos_ref[...], sin_ref[...], p_ref[...]).astype(BF)

    r64 = pl.BlockSpec((tm, QK_ROPE), lambda i: (i, 0))
    dkr = pl.pallas_call(
        dkr_body, name="mlab_dkr", grid=(nt,),
        in_specs=[pl.BlockSpec((N_HEADS, tm, QK_ROPE), lambda i: (0, i, 0)), r64, r64, pl.BlockSpec((QK_ROPE, QK_ROPE), lambda i: (0, 0))],
        out_specs=r64, out_shape=_sds((s, QK_ROPE), BF), compiler_params=_params())(dkr_h, cosf, sinf, perm)

    def lat_epi(acc, extra, outs, pids):
        dx, dg = _rms_bwd(acc, extra[0][...], extra[1][...], None)
        outs[0][...] = dx.astype(BF)
        _accumulate(outs[1], dg, pids[0])

    def lat_bwd(name, a, a_spec, b, b_spec, n_k, lat, g, rank):
        row = lambda i, k: (i, 0)
        one = lambda i, k: (0, 0)
        return _mm(name, (nt, n_k), [a, b, lat, g.reshape(1, rank)],
                   [a_spec, b_spec, pl.BlockSpec((tm, rank), row), pl.BlockSpec((1, rank), one)],
                   [_sds((s, rank), BF), _sds((1, rank), F32)], [pl.BlockSpec((tm, rank), row), pl.BlockSpec((1, rank), one)],
                   NT, lat_epi, red=1, acc_shape=(tm, rank))

    d_ql, g_qnorm = lat_bwd("mlab_qup", dq, pl.BlockSpec((None, tm, QK_DIM), lambda i, h: (h, i, 0)),
                            w["w_qh"], pl.BlockSpec((None, Q_RANK, QK_DIM), lambda i, h: (h, 0, 0)), N_HEADS,
                            sv["q_lat"], w["q_norm"], Q_RANK)
    d_kvl, g_kvnorm = lat_bwd("mlab_kvup", dkv, pl.BlockSpec((tm, 512), lambda i, kk: (i, kk)),
                              w["g512"], pl.BlockSpec((None, KV_RANK, 512), lambda i, kk: (kk, R512_KV // KV_RANK, 0)), N_DEV,
                              sv["kv_lat"], w["kv_norm"], KV_RANK)

    def in_bwd(dql_ref, dkvl_ref, dkr_ref, dz_ref, wq_ref, wkv_ref, wkr_ref, wz_ref, x_ref, g_ref, dy_ref, dx_ref, dxb_ref, dg_ref):
        acc = (_dot(dql_ref[...], wq_ref[...], NT) + _dot(dkvl_ref[...], wkv_ref[...], NT)
               + _dot(dkr_ref[...], wkr_ref[...], NT) + _dot(dz_ref[...], wz_ref[...], NT))
        dx, dg = _rms_bwd(acc, x_ref[...], g_ref[...], dy_ref[...])
        dx_ref[...] = dx
        dxb_ref[...] = dx.astype(BF)
        _accumulate(dg_ref, dg, pl.program_id(0))

    def full(a):
        return pl.BlockSpec(a.shape, lambda i: (0,) * a.ndim)

    def rows(c):
        return pl.BlockSpec((tm, c), lambda i: (i, 0))

    gm = w["mla_norm"].reshape(1, D_MODEL)
    dx, dxb, g_norm = pl.pallas_call(
        in_bwd, name="mlab_in", grid=(nt,),
        in_specs=[rows(Q_RANK), rows(KV_RANK), rows(QK_ROPE), rows(D_INNER), full(w["w_q"]), full(w["w_kv"]), full(w["w_kr"]),
                  full(w["w_z"]), rows(D_MODEL), full(gm), rows(D_MODEL)],
        out_specs=[rows(D_MODEL), rows(D_MODEL), full(gm)],
        out_shape=[_sds((s, D_MODEL), F32), _sds((s, D_MODEL), BF), _sds((1, D_MODEL), F32)],
        compiler_params=_params())(d_ql, d_kvl, dkr, dz, w["w_q"], w["w_kv"], w["w_kr"], w["w_z"], sv["x"], gm, dy)

    xn = sv["xn"]
    one = lambda j: (0, 0)
    g_q = _tn("mlab_gq", xn, d_ql, (D_MODEL, Q_RANK), (D_MODEL, Q_RANK), one, D_MODEL, Q_RANK, (1,), one, one)
    g_kv = _tn("mlab_gkv", xn, d_kvl, (D_MODEL, KV_RANK), (D_MODEL, KV_RANK), one, D_MODEL, KV_RANK, (1,), one, one)
    g_kr = _tn("mlab_gkr", xn, dkr, (D_MODEL, QK_ROPE), (D_MODEL, QK_ROPE), one, D_MODEL, QK_ROPE, (1,), one, one)
    g_z = _tn("mlab_gz", xn, dz, (D_MODEL, D_INNER), (D_MODEL, 512), lambda j: (0, j), D_MODEL, 512, (4,), one, lambda j: (0, j))
    g_in = jnp.concatenate([g_q, g_kv, g_kr, g_z], axis=1)
    g_qh = _tn("mlab_gqup", sv["qn"], dq, (N_HEADS, Q_RANK, QK_DIM), (None, Q_RANK, QK_DIM), lambda h: (h, 0, 0),
               Q_RANK, QK_DIM, (N_HEADS,), lambda h: (0, 0), lambda h: (h, 0, 0))
    g_kvup = _tn("mlab_gkvup", sv["kvn"], dkv, (N_DEV, KV_RANK, 512), (None, KV_RANK, 512), lambda j: (j, 0, 0),
                 KV_RANK, 512, (N_DEV,), lambda j: (0, 0), lambda j: (0, j))
    g_out = _w_out_grad("mlab_gout", sv["gated"], dyb)
    return dx, dxb, dict(w_in=g_in, w_qh=g_qh, w_kvup=g_kvup, w_out=g_out, norm=g_norm[0], q_norm=g_qnorm[0], kv_norm=g_kvnorm[0])


def _loss_head(x, g, target, tm):
    s, d = x.shape

    def body(x_ref, g_ref, t_ref, dx_ref, dxb_ref, dg_ref, loss_ref):
        i = pl.program_id(0)
        xv, gv = x_ref[...], g_ref[...]
        r = lax.rsqrt(jnp.mean(xv * xv, axis=-1, keepdims=True) + NORM_EPS)
        err = (xv * r) * gv - t_ref[...]
        part = 0.5 * jnp.sum(jnp.mean(err * err, axis=-1, keepdims=True), axis=0, keepdims=True)
        dx, dg = _rms_bwd(err * (1.0 / d), xv, gv, None)
        dx_ref[...] = dx
        dxb_ref[...] = dx.astype(BF)
        _accumulate(dg_ref, dg, i)
        _accumulate(loss_ref, jnp.broadcast_to(part, loss_ref.shape), i)

    row = pl.BlockSpec((tm, d), lambda i: (i, 0))
    one = pl.BlockSpec((1, d), lambda i: (0, 0))
    return pl.pallas_call(
        body, name="loss_head", grid=(s // tm,), in_specs=[row, one, row],
        out_specs=[row, row, one, pl.BlockSpec((8, 128), lambda i: (0, 0))],
        out_shape=[_sds((s, d), F32), _sds((s, d), BF), _sds((1, d), F32), _sds((8, 128), F32)],
        compiler_params=_params())(x, g.reshape(1, d), target)


def _rope_tables(pos):
    inv_freq = ROPE_BASE ** (-jnp.arange(0, QK_ROPE, 2, dtype=F32) / QK_ROPE)
    ang = pos.astype(F32)[:, None] * inv_freq
    cos, sin = jnp.cos(ang), jnp.sin(ang)
    idx = jnp.arange(QK_ROPE)
    perm = (idx[:, None] == (idx[None, :] + QK_ROPE // 2) % QK_ROPE).astype(F32)
    return jnp.concatenate([cos, cos], axis=1), jnp.concatenate([-sin, sin], axis=1), perm


def _local_step(x, pos, target, w):
    s = x.shape[0]
    tm = min(512, s)
    rope = _rope_tables(pos)
    x1, sv0 = _pool_fwd(x, 0, w, tm)
    x2, sv1 = _conv_fwd(x1, w, tm)
    x3, sv2 = _mla_fwd(x2, w, rope, tm)
    x4, sv3 = _pool_fwd(x3, 1, w, tm)
    d4, d4b, g_final, loss = _loss_head(x4, w["final_norm"], target, tm)
    d3, d3b, gp1 = _pool_bwd(d4, d4b, 1, w, sv3, tm)
    d2, d2b, gm = _mla_bwd(d3, d3b, w, sv2, rope, tm)
    d1, d1b, gc = _conv_bwd(d2, d2b, w, sv1, tm)
    d0, _, gp0 = _pool_bwd(d1, d1b, 0, w, sv0, tm)
    grads = dict(pool0=gp0, pool1=gp1, conv=gc, mla=gm, final_norm=g_final[0])
    return loss[0, 0], d0, grads


def _pack_weights(p):
    b512 = jnp.concatenate([p["pool_w_in"].reshape(2048, 512), p["mla_w_kv_up"].reshape(256, 512),
                            p["pool_w_grp"].reshape(512, 512)], axis=0).astype(BF)
    b1024 = jnp.concatenate([p["conv_w_in"].reshape(1024, 1024), p["pool_w_out"].reshape(512, 1024),
                             p["conv_w_out"].reshape(256, 1024), p["mla_w_out"].reshape(256, 1024)], axis=0).astype(BF)
    b384 = p["mla_w_q_up"].reshape(384, 384).astype(BF)
    b344 = p["mla_w_in"].reshape(1024, 344).astype(BF)
    return b512, b1024, b384, b344, _pack_small(p, SMALL_ROWS_AG)


_SMALL_SHARDED = ("pool_norm", "pool_scale", "mla_norm", "mla_q_norm", "mla_kv_norm", "conv_w")
_SMALL_REPLICATED = ("conv_norm", "final_norm")


def _pack_small(p, rows, with_replicated=False):
    parts = [p[k].reshape(-1) for k in _SMALL_SHARDED]
    if with_replicated:
        parts += [p[k].reshape(-1) for k in _SMALL_REPLICATED]
    flat = jnp.concatenate(parts)
    return jnp.pad(flat, (0, rows * 128 - flat.shape[0])).reshape(rows, 128)


_SMALL_SHARD_SHAPES = dict(pool_norm=(2, 128), pool_scale=(2, 256), mla_norm=(1, 128), mla_q_norm=(1, 48),
                           mla_kv_norm=(1, 32), conv_w=(1, 3, 256), conv_norm=(1, 1024), final_norm=(1024,))


def _unpack_small(buf, with_replicated=False):
    flat = buf.reshape(-1)
    out, off = {}, 0
    for k in _SMALL_SHARDED + (_SMALL_REPLICATED if with_replicated else ()):
        shp = _SMALL_SHARD_SHAPES[k]
        n = 1
        for d in shp:
            n *= d
        out[k] = flat[off:off + n].reshape(shp)
        off += n
    return out


def _gathered_views(g512, g1024, g384, g344, gsmall, conv_norm, final_norm):
    flat = gsmall.reshape(N_DEV, -1)

    def cols(off, rows, width):
        return flat[:, off:off + rows * width].reshape(N_DEV, rows, width).transpose(1, 0, 2).reshape(rows, N_DEV * width)

    w_in = g344.transpose(1, 0, 2).reshape(D_MODEL, N_DEV * 344)
    return dict(
        g512=g512, g1024=g1024,
        w_q=w_in[:, :Q_RANK], w_kv=w_in[:, Q_RANK:Q_RANK + KV_RANK],
        w_kr=w_in[:, Q_RANK + KV_RANK:Q_RANK + KV_RANK + QK_ROPE], w_z=w_in[:, Q_RANK + KV_RANK + QK_ROPE:],
        w_qh=g384.reshape(N_DEV, Q_RANK, 2, QK_DIM).transpose(0, 2, 1, 3).reshape(N_HEADS, Q_RANK, QK_DIM),
        pool_norm=cols(0, 2, 128), pool_scale=cols(256, 2, 256), mla_norm=cols(768, 1, 128)[0],
        q_norm=cols(896, 1, 48)[0], kv_norm=cols(944, 1, 32)[0], conv_w=cols(976, 3, 256),
        conv_norm=conv_norm.reshape(D_MODEL), final_norm=final_norm)


def _pack_grads(g):
    p0, p1, cv, ml = g["pool0"], g["pool1"], g["conv"], g["mla"]
    s512 = jnp.concatenate([p0["w_in"], p1["w_in"], ml["w_kvup"], p0["w_grp"], p1["w_grp"]], axis=1)
    out = lambda a: a.reshape(N_DEV, 256, D_MODEL)
    s1024 = jnp.concatenate([cv["w_in"], out(p0["w_out"]), out(p1["w_out"]), out(cv["w_out"]), out(ml["w_out"])], axis=1)
    s384 = ml["w_qh"].reshape(N_DEV, 2, Q_RANK, QK_DIM).transpose(0, 2, 1, 3).reshape(N_DEV, Q_RANK, 2 * QK_DIM)
    s344 = ml["w_in"].reshape(D_MODEL, N_DEV, 344).transpose(1, 0, 2)

    def split(a, rows, width):
        return a.reshape(rows, N_DEV, width).transpose(1, 0, 2).reshape(N_DEV, rows * width)

    rep = lambda a: jnp.broadcast_to(a.reshape(1, -1), (N_DEV, a.size))
    flat = jnp.concatenate([
        split(jnp.stack([p0["norm"], p1["norm"]]), 2, 128), split(jnp.stack([p0["scale"], p1["scale"]]), 2, 256),
        split(ml["norm"], 1, 128), split(ml["q_norm"], 1, 48), split(ml["kv_norm"], 1, 32), split(cv["conv_w"], 3, 256),
        rep(cv["norm"]), rep(g["final_norm"])], axis=1)
    small = jnp.pad(flat, ((0, 0), (0, SMALL_ROWS_RS * 128 - flat.shape[1]))).reshape(N_DEV, SMALL_ROWS_RS, 128)
    return s512, s1024, s384, s344, small


def _exchange(name, arrays, gather):
    n = len(arrays)
    outs = [_sds(((N_DEV,) + a.shape) if gather else a.shape, a.dtype) for a in arrays]

    def body(*refs):
        ins, dsts = refs[:n], refs[n:2 * n]
        send_sems, recv_sems, local_sems = refs[2 * n:]
        x, y, c = lax.axis_index("x"), lax.axis_index("y"), lax.axis_index("c")
        me = 4 * x + 2 * y + c
        local = []
        for a in range(n):
            cp = pltpu.make_async_copy(ins[a] if gather else ins[a].at[me], dsts[a].at[me], local_sems.at[a])
            cp.start()
            local.append(cp)
        remote = []
        for k in range(1, N_DEV):
            px = 1 - x if k & 4 else x
            py = 1 - y if k & 2 else y
            pc = 1 - c if k & 1 else c
            peer = 4 * px + 2 * py + pc
            for a in range(n):
                cp = pltpu.make_async_remote_copy(
                    src_ref=ins[a] if gather else ins[a].at[peer], dst_ref=dsts[a].at[me],
                    send_sem=send_sems.at[a, k - 1], recv_sem=recv_sems.at[a, k - 1],
                    device_id=(px, py, pc), device_id_type=pl.DeviceIdType.MESH)
                cp.start()
                remote.append(cp)
        for cp in remote:
            cp.wait()
        for cp in local:
            cp.wait()

    hbm = pl.BlockSpec(memory_space=pltpu.HBM)
    return pl.pallas_call(
        body, name=name, in_specs=[hbm] * n, out_specs=[hbm] * n, out_shape=outs,
        scratch_shapes=[pltpu.SemaphoreType.DMA((n, N_DEV - 1)), pltpu.SemaphoreType.DMA((n, N_DEV - 1)),
                        pltpu.SemaphoreType.DMA((n,))])(*arrays)


def _adamw_math(g, w, m, v):
    m = ADAM_B1 * m + (1.0 - ADAM_B1) * g
    v = ADAM_B2 * v + (1.0 - ADAM_B2) * (g * g)
    m_hat = m / (1.0 - ADAM_B1 ** ADAM_STEP)
    v_hat = v / (1.0 - ADAM_B2 ** ADAM_STEP)
    delta = -ADAM_LR * (m_hat / (jnp.sqrt(v_hat) + ADAM_EPS) + ADAM_WD * w)
    return delta, m, v


def _sum_adamw(name, recv, row_off, w, m, v, tr):
    shape = w.shape
    width = recv.shape[-1]
    w2, m2, v2 = (a.reshape(-1, width) for a in (w, m, v))
    rows = w2.shape[0]
    base = row_off // tr

    def body(r_ref, w_ref, m_ref, v_ref, g_ref, d_ref, mo_ref, vo_ref):
        g = r_ref[0].astype(F32)
        for src in range(1, N_DEV):
            g = g + r_ref[src].astype(F32)
        delta, mn, vn = _adamw_math(g, w_ref[...], m_ref[...], v_ref[...])
        g_ref[...] = g
        d_ref[...] = delta
        mo_ref[...] = mn
        vo_ref[...] = vn

    blk = pl.BlockSpec((tr, width), lambda i: (i, 0))
    outs = pl.pallas_call(
        body, name=name, grid=(rows // tr,),
        in_specs=[pl.BlockSpec((N_DEV, tr, width), lambda i: (0, base + i, 0)), blk, blk, blk],
        out_specs=[blk] * 4, out_shape=[_sds((rows, width), F32)] * 4, compiler_params=_params())(recv, w2, m2, v2)
    return tuple(o.reshape(shape) for o in outs)


_WEIGHTS = ("pool_norm", "pool_w_in", "pool_w_grp", "pool_scale", "pool_w_out", "conv_norm", "conv_w_in", "conv_w", "conv_w_out",
            "mla_norm", "mla_w_in", "mla_q_norm", "mla_w_q_up", "mla_kv_norm", "mla_w_kv_up", "mla_w_out", "final_norm")


def _step(x, positions, loss_target, p, m, v):
    gathered = _exchange("gather_weights", list(_pack_weights(p)), gather=True)
    w = _gathered_views(*gathered, p["conv_norm"], p["final_norm"])
    loss, grad_x, grads = _local_step(x[0], positions[0], loss_target[0], w)
    r512, r1024, r384, r344, rsmall = _exchange("scatter_grads", list(_pack_grads(grads)), gather=False)

    res = {}
    res["pool_w_in"] = _sum_adamw("adam_pool_w_in", r512, R512_W_IN, p["pool_w_in"], m["pool_w_in"], v["pool_w_in"], 256)
    res["mla_w_kv_up"] = _sum_adamw("adam_mla_w_kv_up", r512, R512_KV, p["mla_w_kv_up"], m["mla_w_kv_up"], v["mla_w_kv_up"], 256)
    res["pool_w_grp"] = _sum_adamw("adam_pool_w_grp", r512, R512_GRP, p["pool_w_grp"], m["pool_w_grp"], v["pool_w_grp"], 256)
    res["conv_w_in"] = _sum_adamw("adam_conv_w_in", r1024, R1024_CONV_IN, p["conv_w_in"], m["conv_w_in"], v["conv_w_in"], 256)
    res["pool_w_out"] = _sum_adamw("adam_pool_w_out", r1024, R1024_POOL_OUT, p["pool_w_out"], m["pool_w_out"], v["pool_w_out"], 256)
    res["conv_w_out"] = _sum_adamw("adam_conv_w_out", r1024, R1024_CONV_OUT, p["conv_w_out"], m["conv_w_out"], v["conv_w_out"], 256)
    res["mla_w_out"] = _sum_adamw("adam_mla_w_out", r1024, R1024_MLA_OUT, p["mla_w_out"], m["mla_w_out"], v["mla_w_out"], 256)
    res["mla_w_q_up"] = _sum_adamw("adam_mla_w_q_up", r384, 0, p["mla_w_q_up"], m["mla_w_q_up"], v["mla_w_q_up"], 384)
    res["mla_w_in"] = _sum_adamw("adam_mla_w_in", r344, 0, p["mla_w_in"], m["mla_w_in"], v["mla_w_in"], 256)
    small = _sum_adamw("adam_small", rsmall, 0, _pack_small(p, SMALL_ROWS_RS, True), _pack_small(m, SMALL_ROWS_RS, True),
                       _pack_small(v, SMALL_ROWS_RS, True), SMALL_ROWS_RS)
    small = [_unpack_small(a, True) for a in small]
    for k in _SMALL_SHARDED + _SMALL_REPLICATED:
        res[k] = tuple(part[k] for part in small)

    loss = lax.psum(loss, ("x", "y", "c"))
    out = [loss, grad_x[None]]
    for part in range(4):
        out += [res[k][part] for k in _WEIGHTS]
    return tuple(out)


def kernel(x, positions, pool_norm, pool_w_in, pool_w_grp, pool_scale, pool_w_out, conv_norm, conv_w_in, conv_w, conv_w_out, mla_norm, mla_w_in, mla_q_norm, mla_w_q_up, mla_kv_norm, mla_w_kv_up, mla_w_out, final_norm, loss_target, m_pool_norm, m_pool_w_in, m_pool_w_grp, m_pool_scale, m_pool_w_out, m_conv_norm, m_conv_w_in, m_conv_w, m_conv_w_out, m_mla_norm, m_mla_w_in, m_mla_q_norm, m_mla_w_q_up, m_mla_kv_norm, m_mla_w_kv_up, m_mla_w_out, m_final_norm, v_pool_norm, v_pool_w_in, v_pool_w_grp, v_pool_scale, v_pool_w_out, v_conv_norm, v_conv_w_in, v_conv_w, v_conv_w_out, v_mla_norm, v_mla_w_in, v_mla_q_norm, v_mla_w_q_up, v_mla_kv_norm, v_mla_w_kv_up, v_mla_w_out, v_final_norm):
    p = dict(pool_norm=pool_norm, pool_w_in=pool_w_in, pool_w_grp=pool_w_grp, pool_scale=pool_scale, pool_w_out=pool_w_out,
             conv_norm=conv_norm, conv_w_in=conv_w_in, conv_w=conv_w, conv_w_out=conv_w_out, mla_norm=mla_norm, mla_w_in=mla_w_in,
             mla_q_norm=mla_q_norm, mla_w_q_up=mla_w_q_up, mla_kv_norm=mla_kv_norm, mla_w_kv_up=mla_w_kv_up, mla_w_out=mla_w_out,
             final_norm=final_norm)
    m = dict(pool_norm=m_pool_norm, pool_w_in=m_pool_w_in, pool_w_grp=m_pool_w_grp, pool_scale=m_pool_scale, pool_w_out=m_pool_w_out,
             conv_norm=m_conv_norm, conv_w_in=m_conv_w_in, conv_w=m_conv_w, conv_w_out=m_conv_w_out, mla_norm=m_mla_norm,
             mla_w_in=m_mla_w_in, mla_q_norm=m_mla_q_norm, mla_w_q_up=m_mla_w_q_up, mla_kv_norm=m_mla_kv_norm,
             mla_w_kv_up=m_mla_w_kv_up, mla_w_out=m_mla_w_out, final_norm=m_final_norm)
    v = dict(pool_norm=v_pool_norm, pool_w_in=v_pool_w_in, pool_w_grp=v_pool_w_grp, pool_scale=v_pool_scale, pool_w_out=v_pool_w_out,
             conv_norm=v_conv_norm, conv_w_in=v_conv_w_in, conv_w=v_conv_w, conv_w_out=v_conv_w_out, mla_norm=v_mla_norm,
             mla_w_in=v_mla_w_in, mla_q_norm=v_mla_q_norm, mla_w_q_up=v_mla_w_q_up, mla_kv_norm=v_mla_kv_norm,
             mla_w_kv_up=v_mla_w_kv_up, mla_w_out=v_mla_w_out, final_norm=v_final_norm)
    return _step(x, positions, loss_target, p, m, v)
```

```python
import functools

import jax
import jax.numpy as jnp
from jax import lax
from jax.experimental import pallas as pl
from jax.experimental.pallas import tpu as pltpu

BF = jnp.bfloat16
F32 = jnp.float32

N_DEV = 8
D_MODEL = 1024
D_INNER = 2048
POOL_WINDOWS = (2, 4, 8, 16)
POOL_GROUP = 512
N_HEADS = 16
QK_NOPE = 128
QK_ROPE = 64
QK_DIM = QK_NOPE + QK_ROPE
V_DIM = 128
Q_RANK = 384
KV_RANK = 256
ATTN_SCALE = QK_DIM ** -0.5
LOG2_E = 1.4426950408889634
LN_2 = 0.6931471805599453
Q_PRESCALE = ATTN_SCALE * LOG2_E
ATTN_TILE = 512
ROPE_BASE = 10000.0
NORM_EPS = 1e-6
NEG_BIG = -1e30

ADAM_LR = 0.001
ADAM_B1 = 0.9
ADAM_B2 = 0.999
ADAM_EPS = 1e-08
ADAM_WD = 0.01
ADAM_STEP = 10

VMEM_LIMIT_BYTES = 52 * 1024 * 1024
POOL_HALO = 32
CONV_HALO = 8

NN = (((1,), (0,)), ((), ()))
NT = (((1,), (1,)), ((), ()))
TN = (((0,), (0,)), ((), ()))

R512_W_IN, R512_KV, R512_GRP, R512_ROWS = 0, 2048, 2304, 2816
R1024_CONV_IN, R1024_POOL_OUT, R1024_CONV_OUT, R1024_MLA_OUT, R1024_ROWS = 0, 1024, 1536, 1792, 2048
SMALL_ROWS_AG = 16
SMALL_ROWS_RS = 32


def _sds(shape, dtype):
    return jax.ShapeDtypeStruct(tuple(shape), dtype)


def _params():
    return pltpu.CompilerParams(vmem_limit_bytes=VMEM_LIMIT_BYTES)


def _dot(a, b, dims):
    return lax.dot_general(a, b, dims, preferred_element_type=F32)


def _sig(z):
    return 1.0 / (1.0 + jnp.exp(-z))


def _silu_and_grad(z):
    sig = _sig(z)
    return z * sig, sig * (1.0 + z * (1.0 - sig))


def _rope_swap(x, p):
    return jnp.dot(x, p, precision=lax.Precision.HIGHEST, preferred_element_type=F32)


def _rope_fwd(x, cosf, sinf, p):
    return x * cosf + _rope_swap(x, p) * sinf


def _rope_bwd(dy, cosf, sinf, p):
    return dy * cosf + _rope_swap(dy * sinf, p)


def _rms_bwd(dxn, x, g, res):
    r = lax.rsqrt(jnp.mean(x * x, axis=-1, keepdims=True) + NORM_EPS)
    v = dxn * g
    dx = r * v - x * ((r * r * r) * jnp.mean(v * x, axis=-1, keepdims=True))
    if res is not None:
        dx = dx + res
    dg = jnp.sum(dxn * (x * r), axis=0, keepdims=True)
    return dx, dg


def _accumulate(ref, val, step):
    @pl.when(step == 0)
    def _():
        ref[...] = val

    @pl.when(step > 0)
    def _():
        ref[...] += val


def _mm(name, grid, ins, in_specs, outs, out_specs, dims, epi, red=None, acc_shape=None):
    n_in, n_out = len(ins), len(outs)
    n_red = None if red is None else grid[red]

    def body(*refs):
        in_refs, out_refs = refs[:n_in], refs[n_in:n_in + n_out]
        pids = tuple(pl.program_id(ax) for ax in range(len(grid)))
        a, b = in_refs[0][...], in_refs[1][...]
        if a.ndim == 3:
            a = a.reshape(-1, a.shape[-1])
        if b.ndim == 3:
            b = b.reshape(-1, b.shape[-1])
        part = _dot(a.astype(BF), b.astype(BF), dims)
        if red is None:
            epi(part, in_refs[2:], out_refs, pids)
        else:
            acc = refs[n_in + n_out]
            k = pids[red]
            _accumulate(acc, part, k)

            @pl.when(k == n_red - 1)
            def _():
                epi(acc[...], in_refs[2:], out_refs, pids)

    scratch = [] if red is None else [pltpu.VMEM(acc_shape, F32)]
    return pl.pallas_call(body, name=name, grid=grid, in_specs=in_specs, out_specs=out_specs, out_shape=outs,
                          scratch_shapes=scratch, compiler_params=_params())(*ins)


def _store(part, extra, outs, pids):
    outs[0][...] = part.astype(outs[0].dtype)


def _rms_fwd(name, x, g, tm):
    s, d = x.shape

    def body(x_ref, g_ref, o_ref):
        xv = x_ref[...]
        r = lax.rsqrt(jnp.mean(xv * xv, axis=-1, keepdims=True) + NORM_EPS)
        o_ref[...] = ((xv * r) * g_ref[...]).astype(BF)

    return pl.pallas_call(body, name=name, grid=(s // tm,),
                          in_specs=[pl.BlockSpec((tm, d), lambda i: (i, 0)), pl.BlockSpec((1, d), lambda i: (0, 0))],
                          out_specs=pl.BlockSpec((tm, d), lambda i: (i, 0)), out_shape=_sds((s, d), BF),
                          compiler_params=_params())(x, g.reshape(1, d))


def _tn(name, a, b, out_shape, out_block, out_index, a_cols, b_cols, grid, a_index, b_index):
    s = a.shape[-2]
    a_block = (s, a_cols) if a.ndim == 2 else (None, s, a_cols)
    b_block = (s, b_cols) if b.ndim == 2 else (None, s, b_cols)

    def epi(part, extra, outs, pids):
        outs[0][...] = part.astype(BF).reshape(outs[0].shape)

    return _mm(name, grid, [a, b], [pl.BlockSpec(a_block, a_index), pl.BlockSpec(b_block, b_index)],
               [_sds(out_shape, BF)], [pl.BlockSpec(out_block, out_index)], TN, epi)[0]


def _pool_window_fwd(name, h, tm):
    s = h.shape[0]
    hb = POOL_HALO

    def body(u_ref, halo_ref, o_ref, e_ref, a_ref, b_ref):
        i = pl.program_id(0)
        row = lax.broadcasted_iota(jnp.int32, (tm, 1), 0) + i * tm
        for g, w in enumerate(POOL_WINDOWS):
            cs = slice(g * POOL_GROUP, (g + 1) * POOL_GROUP)
            e_ref[0:hb, :] = jnp.where(i > 0, halo_ref[:, cs], 0.0)
            e_ref[hb:, :] = u_ref[:, cs]
            src, bufs = e_ref, (a_ref, b_ref)
            for lv in range(1, w.bit_length()):
                dst, st, sh = bufs[(lv - 1) % 2], 8 * lv, 2 ** (lv - 1)
                n = hb + tm - st
                dst[st:, :] = src[st:, :] + src[pl.ds(st - sh, n), :]
                src = dst
            cnt = jnp.minimum(row + 1, w).astype(F32)
            o_ref[:, cs] = (src[hb:, :] / cnt - u_ref[:, cs]).astype(BF)

    per = tm // hb
    return pl.pallas_call(
        body, name=name, grid=(s // tm,),
        in_specs=[pl.BlockSpec((tm, D_INNER), lambda i: (i, 0)),
                  pl.BlockSpec((hb, D_INNER), lambda i: (jnp.maximum(i * per - 1, 0), 0))],
        out_specs=pl.BlockSpec((tm, D_INNER), lambda i: (i, 0)), out_shape=_sds((s, D_INNER), BF),
        scratch_shapes=[pltpu.VMEM((hb + tm, POOL_GROUP), F32)] * 3, compiler_params=_params())(h, h)


def _pool_window_bwd(name, dp, tm):
    s = dp.shape[0]
    nt = s // tm
    hb = POOL_HALO

    def body(d_ref, halo_ref, o_ref, e_ref, a_ref, b_ref):
        i = pl.program_id(0)
        row = lax.broadcasted_iota(jnp.int32, (tm, 1), 0) + i * tm
        hrow = lax.broadcasted_iota(jnp.int32, (hb, 1), 0) + (i + 1) * tm
        for g, w in enumerate(POOL_WINDOWS):
            cs = slice(g * POOL_GROUP, (g + 1) * POOL_GROUP)
            e_ref[0:tm, :] = d_ref[:, cs] / jnp.minimum(row + 1, w).astype(F32)
            e_ref[tm:, :] = jnp.where(i < nt - 1, halo_ref[:, cs] / jnp.minimum(hrow + 1, w).astype(F32), 0.0)
            src, bufs = e_ref, (a_ref, b_ref)
            for lv in range(1, w.bit_length()):
                dst, sh = bufs[(lv - 1) % 2], 2 ** (lv - 1)
                n = tm + hb - 8 * lv
                dst[0:n, :] = src[0:n, :] + src[pl.ds(sh, n), :]
                src = dst
            o_ref[:, cs] = (src[0:tm, :] - d_ref[:, cs]).astype(BF)

    per = tm // hb
    last = s // hb - 1
    return pl.pallas_call(
        body, name=name, grid=(nt,),
        in_specs=[pl.BlockSpec((tm, D_INNER), lambda i: (i, 0)),
                  pl.BlockSpec((hb, D_INNER), lambda i: (jnp.minimum((i + 1) * per, last), 0))],
        out_specs=pl.BlockSpec((tm, D_INNER), lambda i: (i, 0)), out_shape=_sds((s, D_INNER), BF),
        scratch_shapes=[pltpu.VMEM((hb + tm, POOL_GROUP), F32)] * 3, compiler_params=_params())(dp, dp)


def _grp_block(l):
    base = R512_GRP // 64 + l * 4
    return pl.BlockSpec((N_DEV, 64, POOL_GROUP), lambda i, g: (0, base + g, 0))


def _pool_fwd(x, l, w, tm):
    s = x.shape[0]
    nt = s // tm
    n = f"pool{l}"
    xn = _rms_fwd(n + "_rms", x, w["pool_norm"][l], tm)
    (h,) = _mm(n + "_in", (nt, 8), [xn, w["g512"]],
               [pl.BlockSpec((tm, D_MODEL), lambda i, j: (i, 0)), pl.BlockSpec((None, D_MODEL, 512), lambda i, j: (j, l, 0))],
               [_sds((s, 2 * D_INNER), F32)], [pl.BlockSpec((tm, 512), lambda i, j: (i, j))], NN, _store)
    pooled = _pool_window_fwd(n + "_win", h, tm)

    def gate(part, extra, outs, pids):
        z = extra[0][...]
        outs[0][...] = ((part * extra[1][...]) * (z * _sig(z))).astype(BF)

    (gated,) = _mm(n + "_grp", (nt, 4), [pooled, w["g512"], h, w["pool_scale"][l].reshape(1, D_INNER)],
                   [pl.BlockSpec((tm, 512), lambda i, g: (i, g)), _grp_block(l),
                    pl.BlockSpec((tm, 512), lambda i, g: (i, 4 + g)), pl.BlockSpec((1, 512), lambda i, g: (0, g))],
                   [_sds((s, D_INNER), BF)], [pl.BlockSpec((tm, 512), lambda i, g: (i, g))], NN, gate)
    y = _out_proj(n + "_out", gated, w["g1024"], R1024_POOL_OUT // 256 + l, x, tm)
    return y, dict(x=x, xn=xn, h=h, pooled=pooled, gated=gated)


def _out_proj(name, gated, g1024, row_block, x, tm):
    s = x.shape[0]

    def epi(part, extra, outs, pids):
        outs[0][...] = part + extra[0][...]

    return _mm(name, (s // tm, 2), [gated, g1024, x],
               [pl.BlockSpec((tm, D_INNER), lambda i, j: (i, 0)), pl.BlockSpec((N_DEV, 256, 512), lambda i, j: (0, row_block, j)),
                pl.BlockSpec((tm, 512), lambda i, j: (i, j))],
               [_sds((s, D_MODEL), F32)], [pl.BlockSpec((tm, 512), lambda i, j: (i, j))], NN, epi)[0]


def _w_out_nt_block(row_block):
    return pl.BlockSpec((2, 256, D_MODEL), lambda j, i: (j, row_block, 0))


def _in_proj_bwd(name, dh, wbuf, w_index, n_k, x, g, dy, tm):
    s = x.shape[0]

    def epi(acc, extra, outs, pids):
        dx, dg = _rms_bwd(acc, extra[0][...], extra[1][...], extra[2][...])
        outs[0][...] = dx
        outs[1][...] = dx.astype(BF)
        _accumulate(outs[2], dg, pids[0])

    row = lambda i, k: (i, 0)
    return _mm(name, (s // tm, n_k), [dh, wbuf, x, g.reshape(1, D_MODEL), dy],
               [pl.BlockSpec((tm, 512), lambda i, k: (i, k)), pl.BlockSpec((None, D_MODEL, 512), w_index),
                pl.BlockSpec((tm, D_MODEL), row), pl.BlockSpec((1, D_MODEL), lambda i, k: (0, 0)), pl.BlockSpec((tm, D_MODEL), row)],
               [_sds((s, D_MODEL), F32), _sds((s, D_MODEL), BF), _sds((1, D_MODEL), F32)],
               [pl.BlockSpec((tm, D_MODEL), row), pl.BlockSpec((tm, D_MODEL), row), pl.BlockSpec((1, D_MODEL), lambda i, k: (0, 0))],
               NT, epi, red=1, acc_shape=(tm, D_MODEL))


def _w_out_grad(name, gated, dyb):
    s = gated.shape[0]
    return _tn(name, gated, dyb, (D_INNER, D_MODEL), (512, D_MODEL), lambda i: (i, 0), 512, D_MODEL, (4,),
               lambda i: (0, i), lambda i: (0, 0))


def _pool_bwd(dy, dyb, l, w, sv, tm):
    s = dy.shape[0]
    nt = s // tm
    n = f"pool{l}b"
    h, pooled = sv["h"], sv["pooled"]
    scale = w["pool_scale"][l].reshape(1, D_INNER)
    (mp,) = _mm(n + "_grp", (nt, 4), [pooled, w["g512"]],
                [pl.BlockSpec((tm, 512), lambda i, g: (i, g)), _grp_block(l)],
                [_sds((s, D_INNER), F32)], [pl.BlockSpec((tm, 512), lambda i, g: (i, g))], NN, _store)

    def gate_bwd(part, extra, outs, pids):
        z, mpv, sc = extra[0][...], extra[1][...], extra[2][...]
        sz, dsz = _silu_and_grad(z)
        dm = part * sz
        outs[0][...] = (dm * sc).astype(BF)
        outs[1][...] = (part * (mpv * sc) * dsz).astype(BF)
        _accumulate(outs[2], jnp.sum(dm * mpv, axis=0, keepdims=True), pids[1])

    tile = lambda j, i: (i, j)
    dmp, dz, dscale = _mm(
        n + "_out", (4, nt), [dyb, w["g1024"], h, mp, scale],
        [pl.BlockSpec((tm, D_MODEL), lambda j, i: (i, 0)), _w_out_nt_block(R1024_POOL_OUT // 256 + l),
         pl.BlockSpec((tm, 512), lambda j, i: (i, 4 + j)), pl.BlockSpec((tm, 512), tile), pl.BlockSpec((1, 512), lambda j, i: (0, j))],
        [_sds((s, D_INNER), BF), _sds((s, D_INNER), BF), _sds((1, D_INNER), F32)],
        [pl.BlockSpec((tm, 512), tile), pl.BlockSpec((tm, 512), tile), pl.BlockSpec((1, 512), lambda j, i: (0, j))], NT, gate_bwd)
    (dpool,) = _mm(n + "_grpT", (nt, 4), [dmp, w["g512"]],
                   [pl.BlockSpec((tm, 512), lambda i, g: (i, g)), _grp_block(l)],
                   [_sds((s, D_INNER), F32)], [pl.BlockSpec((tm, 512), lambda i, g: (i, g))], NT, _store)
    du = _pool_window_bwd(n + "_win", dpool, tm)
    dh = jnp.concatenate([du, dz], axis=1)
    dx, dxb, dnorm = _in_proj_bwd(n + "_in", dh, w["g512"], lambda i, k: (k, l, 0), 8, sv["x"], w["pool_norm"][l], dy, tm)
    g_in = _tn(n + "_gin", sv["xn"], dh, (N_DEV, D_MODEL, 512), (None, D_MODEL, 512), lambda j: (j, 0, 0),
               D_MODEL, 512, (8,), lambda j: (0, 0), lambda j: (0, j))
    g_out = _w_out_grad(n + "_gout", sv["gated"], dyb)
    g_grp = _tn(n + "_ggrp", pooled, dmp, (N_DEV, 256, 512), (N_DEV, 64, 512), lambda g: (0, g, 0),
                512, 512, (4,), lambda g: (0, g), lambda g: (0, g))
    return dx, dxb, dict(w_in=g_in, w_out=g_out, w_grp=g_grp, norm=dnorm[0], scale=dscale[0])


def _conv_in_index(i, j):
    return (j // 2, 0, j % 2)


def _conv_fwd(x, w, tm):
    s = x.shape[0]
    nt = s // tm
    xn = _rms_fwd("conv_rms", x, w["conv_norm"], tm)
    (h,) = _mm("conv_in", (nt, 16), [xn, w["g1024"]],
               [pl.BlockSpec((tm, D_MODEL), lambda i, j: (i, 0)), pl.BlockSpec((None, D_MODEL, 512), _conv_in_index)],
               [_sds((s, 4 * D_INNER), F32)], [pl.BlockSpec((tm, 512), lambda i, j: (i, j))], NN, _store)
    per = tm // CONV_HALO

    def body(b_ref, c_ref, h_ref, z_ref, cp_ref, hp_ref, w_ref, o_ref, e_ref):
        i = pl.program_id(0)
        ch = c_ref[...] * h_ref[...]
        e_ref[0:CONV_HALO, :] = jnp.where(i > 0, cp_ref[...] * hp_ref[...], 0.0)
        e_ref[CONV_HALO:, :] = ch
        co = (w_ref[2:3, :] * ch + w_ref[1:2, :] * e_ref[pl.ds(CONV_HALO - 1, tm), :]
              + w_ref[0:1, :] * e_ref[pl.ds(CONV_HALO - 2, tm), :])
        z = z_ref[...]
        o_ref[...] = ((b_ref[...] * co) * (z * _sig(z))).astype(BF)

    def col(q):
        return pl.BlockSpec((tm, 512), lambda i, j: (i, 4 * q + j))

    def prev(q):
        return pl.BlockSpec((CONV_HALO, 512), lambda i, j: (jnp.maximum(i * per - 1, 0), 4 * q + j))

    gated = pl.pallas_call(
        body, name="conv_mix", grid=(nt, 4),
        in_specs=[col(0), col(1), col(2), col(3), prev(1), prev(2), pl.BlockSpec((3, 512), lambda i, j: (0, j))],
        out_specs=pl.BlockSpec((tm, 512), lambda i, j: (i, j)), out_shape=_sds((s, D_INNER), BF),
        scratch_shapes=[pltpu.VMEM((CONV_HALO + tm, 512), F32)], compiler_params=_params())(h, h, h, h, h, h, w["conv_w"])
    y = _out_proj("conv_out", gated, w["g1024"], R1024_CONV_OUT // 256, x, tm)
    return y, dict(x=x, xn=xn, h=h, gated=gated)


def _conv_bwd(dy, dyb, w, sv, tm):
    s = dy.shape[0]
    nt = s // tm
    h = sv["h"]
    (dg,) = _mm("convb_out", (4, nt), [dyb, w["g1024"]],
                [pl.BlockSpec((tm, D_MODEL), lambda j, i: (i, 0)), _w_out_nt_block(R1024_CONV_OUT // 256)],
                [_sds((s, D_INNER), F32)], [pl.BlockSpec((tm, 512), lambda j, i: (i, j))], NT, _store)
    per = tm // CONV_HALO
    last = s // CONV_HALO - 1

    def body(dg_ref, b_ref, c_ref, h_ref, z_ref, cp_ref, hp_ref, dgn_ref, bn_ref, zn_ref, w_ref,
             db_ref, dc_ref, dh_ref, dz_ref, dw_ref, e_ref, f_ref):
        i = pl.program_id(1)
        w0, w1, w2 = w_ref[0:1, :], w_ref[1:2, :], w_ref[2:3, :]
        c, hh, b = c_ref[...], h_ref[...], b_ref[...]
        ch = c * hh
        e_ref[0:CONV_HALO, :] = jnp.where(i > 0, cp_ref[...] * hp_ref[...], 0.0)
        e_ref[CONV_HALO:, :] = ch
        ch1 = e_ref[pl.ds(CONV_HALO - 1, tm), :]
        ch2 = e_ref[pl.ds(CONV_HALO - 2, tm), :]
        co = w2 * ch + w1 * ch1 + w0 * ch2
        sz, dsz = _silu_and_grad(z_ref[...])
        dgv = dg_ref[...]
        dyv = dgv * sz
        dz_ref[...] = (dgv * (b * co) * dsz).astype(BF)
        db_ref[...] = (dyv * co).astype(BF)
        dco = dyv * b
        zn = zn_ref[...]
        f_ref[0:tm, :] = dco
        f_ref[tm:, :] = jnp.where(i < nt - 1, dgn_ref[...] * (zn * _sig(zn)) * bn_ref[...], 0.0)
        dch = w2 * dco + w1 * f_ref[pl.ds(1, tm), :] + w0 * f_ref[pl.ds(2, tm), :]
        dc_ref[...] = (dch * hh).astype(BF)
        dh_ref[...] = (dch * c).astype(BF)
        for tap, shifted in enumerate((ch2, ch1, ch)):
            _accumulate(dw_ref.at[tap:tap + 1, :], jnp.sum(dco * shifted, axis=0, keepdims=True), i)

    def col(q):
        return pl.BlockSpec((tm, 512), lambda j, i: (i, 4 * q + j))

    def prev(q):
        return pl.BlockSpec((CONV_HALO, 512), lambda j, i: (jnp.maximum(i * per - 1, 0), 4 * q + j))

    def nxt(q):
        return pl.BlockSpec((CONV_HALO, 512), lambda j, i: (jnp.minimum((i + 1) * per, last), 4 * q + j))

    tile = pl.BlockSpec((tm, 512), lambda j, i: (i, j))
    wspec = pl.BlockSpec((3, 512), lambda j, i: (0, j))
    db, dc, dhh, dz, dw = pl.pallas_call(
        body, name="convb_mix", grid=(4, nt),
        in_specs=[tile, col(0), col(1), col(2), col(3), prev(1), prev(2), nxt(0), nxt(0), nxt(3), wspec],
        out_specs=[tile, tile, tile, tile, wspec],
        out_shape=[_sds((s, D_INNER), BF)] * 4 + [_sds((3, D_INNER), F32)],
        scratch_shapes=[pltpu.VMEM((CONV_HALO + tm, 512), F32)] * 2, compiler_params=_params(),
    )(dg, h, h, h, h, h, h, dg, h, h, w["conv_w"])
    dh = jnp.concatenate([db, dc, dhh, dz], axis=1)
    dx, dxb, dnorm = _in_proj_bwd("convb_in", dh, w["g1024"], lambda i, k: (k // 2, 0, k % 2), 16, sv["x"], w["conv_norm"], dy, tm)
    g_in = _tn("convb_gin", sv["xn"], dh, (N_DEV, D_MODEL, D_MODEL), (None, D_MODEL, 512), lambda j: (j // 2, 0, j % 2),
               D_MODEL, 512, (16,), lambda j: (0, 0), lambda j: (0, j))
    g_out = _w_out_grad("convb_gout", sv["gated"], dyb)
    return dx, dxb, dict(w_in=g_in, w_out=g_out, norm=dnorm[0], conv_w=dw)


def _attn_tiles(s):
    t = min(ATTN_TILE, s)
    return t, s // t


def _causal_keep(t, keys_on_rows):
    r = lax.broadcasted_iota(jnp.int32, (t, t), 0)
    c = lax.broadcasted_iota(jnp.int32, (t, t), 1)
    return (r <= c) if keys_on_rows else (c <= r)


def _mla_fwd(x, w, rope, tm):
    s = x.shape[0]
    nt = s // tm
    cosf, sinf, perm = rope
    xn = _rms_fwd("mla_rms", x, w["mla_norm"], tm)

    def in_body(xn_ref, wq_ref, wkv_ref, wkr_ref, wz_ref, gq_ref, gkv_ref, cos_ref, sin_ref, p_ref,
                ql_ref, kvl_ref, qn_ref, kvn_ref, krr_ref, z_ref):
        xv = xn_ref[...]
        ql = _dot(xv, wq_ref[...], NN)
        kvl = _dot(xv, wkv_ref[...], NN)
        ql_ref[...] = ql
        kvl_ref[...] = kvl
        rq = lax.rsqrt(jnp.mean(ql * ql, axis=-1, keepdims=True) + NORM_EPS)
        qn_ref[...] = ((ql * rq) * gq_ref[...]).astype(BF)
        rkv = lax.rsqrt(jnp.mean(kvl * kvl, axis=-1, keepdims=True) + NORM_EPS)
        kvn_ref[...] = ((kvl * rkv) * gkv_ref[...]).astype(BF)
        kr = _dot(xv, wkr_ref[...], NN)
        krr_ref[...] = _rope_fwd(kr, cos_ref[...], sin_ref[...], p_ref[...]).astype(BF)
        z_ref[...] = _dot(xv, wz_ref[...], NN)

    def full(a):
        return pl.BlockSpec(a.shape, lambda i: (0,) * a.ndim)

    def rows(c):
        return pl.BlockSpec((tm, c), lambda i: (i, 0))

    gq, gkv = w["q_norm"].reshape(1, Q_RANK), w["kv_norm"].reshape(1, KV_RANK)
    q_lat, kv_lat, qn, kvn, krr, z = pl.pallas_call(
        in_body, name="mla_in", grid=(nt,),
        in_specs=[rows(D_MODEL), full(w["w_q"]), full(w["w_kv"]), full(w["w_kr"]), full(w["w_z"]), full(gq), full(gkv),
                  rows(QK_ROPE), rows(QK_ROPE), full(perm)],
        out_specs=[rows(Q_RANK), rows(KV_RANK), rows(Q_RANK), rows(KV_RANK), rows(QK_ROPE), rows(D_INNER)],
        out_shape=[_sds((s, Q_RANK), F32), _sds((s, KV_RANK), F32), _sds((s, Q_RANK), BF), _sds((s, KV_RANK), BF),
                   _sds((s, QK_ROPE), BF), _sds((s, D_INNER), F32)],
        compiler_params=_params())(xn, w["w_q"], w["w_kv"], w["w_kr"], w["w_z"], gq, gkv, cosf, sinf, perm)

    def q_epi(part, extra, outs, pids):
        outs[0][:, 0:QK_NOPE] = (part[:, 0:QK_NOPE] * Q_PRESCALE).astype(BF)
        roped = _rope_fwd(part[:, QK_NOPE:QK_DIM], extra[0][...], extra[1][...], extra[2][...])
        outs[0][:, QK_NOPE:QK_DIM] = (roped * Q_PRESCALE).astype(BF)

    rope_row = pl.BlockSpec((tm, QK_ROPE), lambda h, i: (i, 0))
    (q,) = _mm("mla_qup", (N_HEADS, nt), [qn, w["w_qh"], cosf, sinf, perm],
               [pl.BlockSpec((tm, Q_RANK), lambda h, i: (i, 0)), pl.BlockSpec((None, Q_RANK, QK_DIM), lambda h, i: (h, 0, 0)),
                rope_row, rope_row, pl.BlockSpec((QK_ROPE, QK_ROPE), lambda h, i: (0, 0))],
               [_sds((N_HEADS, s, QK_DIM), BF)], [pl.BlockSpec((None, tm, QK_DIM), lambda h, i: (h, i, 0))], NN, q_epi)

    def kv_epi(part, extra, outs, pids):
        outs[0][:, 0:QK_NOPE] = part[:, 0:QK_NOPE].astype(BF)
        outs[0][:, QK_NOPE:QK_DIM] = extra[0][...]
        outs[1][...] = part[:, QK_NOPE:].astype(BF)

    k, v = _mm("mla_kvup", (N_HEADS, nt), [kvn, w["g512"], krr],
               [pl.BlockSpec((tm, KV_RANK), lambda h, i: (i, 0)),
                pl.BlockSpec((None, KV_RANK, 256), lambda h, i: (h // 2, R512_KV // KV_RANK, h % 2)), rope_row],
               [_sds((N_HEADS, s, QK_DIM), BF), _sds((N_HEADS, s, V_DIM), BF)],
               [pl.BlockSpec((None, tm, QK_DIM), lambda h, i: (h, i, 0)), pl.BlockSpec((None, tm, V_DIM), lambda h, i: (h, i, 0))],
               NN, kv_epi)

    t, nq = _attn_tiles(s)

    def attn_body(q_ref, k_ref, v_ref, z_ref, o_ref, g_ref, lse_ref):
        i = pl.program_id(1)
        qv = q_ref[...]

        def block(j, carry, masked):
            m, lsum, acc = carry
            start = pl.multiple_of(j * t, t)
            sc = _dot(qv, k_ref[pl.ds(start, t), :], NT)
            if masked:
                sc = jnp.where(_causal_keep(t, False), sc, NEG_BIG)
            mn = jnp.maximum(m, jnp.max(sc, axis=-1, keepdims=True))
            alpha = jnp.exp2(m - mn)
            p = jnp.exp2(sc - mn)
            lsum = alpha * lsum + jnp.sum(p, axis=-1, keepdims=True)
            acc = alpha * acc + _dot(p.astype(BF), v_ref[pl.ds(start, t), :], NN)
            return mn, lsum, acc

        init = (jnp.full((t, 1), NEG_BIG, F32), jnp.zeros((t, 1), F32), jnp.zeros((t, V_DIM), F32))
        carry = lax.fori_loop(0, i, lambda j, c: block(j, c, False), init)
        m, lsum, acc = block(i, carry, True)
        o = acc / lsum
        z = z_ref[...]
        o_ref[...] = o
        g_ref[...] = (o * (z * _sig(z))).astype(BF)
        lse_ref[...] = m + jnp.log(lsum) * LOG2_E

    head_col = pl.BlockSpec((t, V_DIM), lambda h, i: (i, h))
    o, gated, lse = pl.pallas_call(
        attn_body, name="mla_attn", grid=(N_HEADS, nq),
        in_specs=[pl.BlockSpec((None, t, QK_DIM), lambda h, i: (h, i, 0)), pl.BlockSpec((None, s, QK_DIM), lambda h, i: (h, 0, 0)),
                  pl.BlockSpec((None, s, V_DIM), lambda h, i: (h, 0, 0)), head_col],
        out_specs=[head_col, head_col, pl.BlockSpec((None, t, 1), lambda h, i: (h, i, 0))],
        out_shape=[_sds((s, D_INNER), F32), _sds((s, D_INNER), BF), _sds((N_HEADS, s, 1), F32)],
        compiler_params=_params())(q, k, v, z)
    y = _out_proj("mla_out", gated, w["g1024"], R1024_MLA_OUT // 256, x, tm)
    return y, dict(x=x, xn=xn, q_lat=q_lat, kv_lat=kv_lat, qn=qn, kvn=kvn, z=z, q=q, k=k, v=v, o=o, lse=lse, gated=gated)


def _mla_bwd(dy, dyb, w, sv, rope, tm):
    s = dy.shape[0]
    nt = s // tm
    cosf, sinf, perm = rope
    t, nq = _attn_tiles(s)
    q, k, v, lse = sv["q"], sv["k"], sv["v"], sv["lse"]

    def gate_bwd(part, extra, outs, pids):
        z, o = extra[0][...], extra[1][...]
        sz, dsz = _silu_and_grad(z)
        do = part * sz
        outs[0][...] = do.astype(BF)
        outs[1][...] = (part * o * dsz).astype(BF)
        prod = do * o
        for hh in range(4):
            outs[2][hh] = jnp.sum(prod[:, hh * V_DIM:(hh + 1) * V_DIM], axis=-1, keepdims=True)

    tile = lambda j, i: (i, j)
    dob, dz, delta = _mm(
        "mlab_out", (4, nt), [dyb, w["g1024"], sv["z"], sv["o"]],
        [pl.BlockSpec((tm, D_MODEL), lambda j, i: (i, 0)), _w_out_nt_block(R1024_MLA_OUT // 256),
         pl.BlockSpec((tm, 512), tile), pl.BlockSpec((tm, 512), tile)],
        [_sds((s, D_INNER), BF), _sds((s, D_INNER), BF), _sds((N_HEADS, s, 1), F32)],
        [pl.BlockSpec((tm, 512), tile), pl.BlockSpec((tm, 512), tile), pl.BlockSpec((4, tm, 1), lambda j, i: (j, i, 0))],
        NT, gate_bwd)

    def attn_bwd_body(k_ref, v_ref, q_ref, do_ref, lse_ref, dl_ref, cos_ref, sin_ref, p_ref, dkv_ref, dkr_ref, dq_ref, dq_acc):
        j = pl.program_id(1)
        kb, vb = k_ref[...], v_ref[...]

        @pl.when(j == 0)
        def _():
            dq_acc[...] = jnp.zeros(dq_acc.shape, F32)

        def block(i, carry, masked):
            dk, dv = carry
            rows = pl.ds(pl.multiple_of(i * t, t), t)
            qb, dob_ = q_ref[rows, :], do_ref[rows, :]
            st = _dot(kb, qb, NT)
            if masked:
                st = jnp.where(_causal_keep(t, True), st, NEG_BIG)
            pt = jnp.exp2(st - lse_ref[i])
            dv = dv + _dot(pt.astype(BF), dob_, NN)
            dst = (pt * (_dot(vb, dob_, NT) - dl_ref[i])).astype(BF)
            dk = dk + _dot(dst, qb, NN)
            dq_acc[rows, :] += _dot(dst, kb, TN)
            return dk, dv

        carry = block(j, (jnp.zeros((t, QK_DIM), F32), jnp.zeros((t, V_DIM), F32)), True)
        dk, dv = lax.fori_loop(j + 1, nq, lambda i, c: block(i, c, False), carry)
        dk = dk * LN_2
        dkv_ref[:, 0:QK_NOPE] = dk[:, 0:QK_NOPE].astype(BF)
        dkv_ref[:, QK_NOPE:] = dv.astype(BF)
        dkr_ref[...] = dk[:, QK_NOPE:]

        @pl.when(j == nq - 1)
        def _():
            for c in range(nq):
                rows = slice(c * t, (c + 1) * t)
                dq = dq_acc[rows, :] * ATTN_SCALE
                dq_ref[rows, 0:QK_NOPE] = dq[:, 0:QK_NOPE].astype(BF)
                dq_ref[rows, QK_NOPE:] = _rope_bwd(dq[:, QK_NOPE:], cos_ref[rows, :], sin_ref[rows, :], p_ref[...]).astype(BF)

    row_stats = pl.BlockSpec((None, nq, 1, t), lambda h, j: (h, 0, 0, 0))
    seq_rope = pl.BlockSpec((s, QK_ROPE), lambda h, j: (0, 0))
    head_seq = pl.BlockSpec((None, s, QK_DIM), lambda h, j: (h, 0, 0))
    dkv, dkr_h, dq = pl.pallas_call(
        attn_bwd_body, name="mlab_attn", grid=(N_HEADS, nq),
        in_specs=[pl.BlockSpec((None, t, QK_DIM), lambda h, j: (h, j, 0)), pl.BlockSpec((None, t, V_DIM), lambda h, j: (h, j, 0)),
                  head_seq, pl.BlockSpec((s, V_DIM), lambda h, j: (0, h)), row_stats, row_stats, seq_rope, seq_rope,
                  pl.BlockSpec((QK_ROPE, QK_ROPE), lambda h, j: (0, 0))],
        out_specs=[pl.BlockSpec((t, 2 * V_DIM), lambda h, j: (j, h)), pl.BlockSpec((None, t, QK_ROPE), lambda h, j: (h, j, 0)), head_seq],
        out_shape=[_sds((s, N_HEADS * 2 * V_DIM), BF), _sds((N_HEADS, s, QK_ROPE), F32), _sds((N_HEADS, s, QK_DIM), BF)],
        scratch_shapes=[pltpu.VMEM((s, QK_DIM), F32)],
        compiler_params=_params())(k, v, q, dob, lse.reshape(N_HEADS, nq, 1, t), delta.reshape(N_HEADS, nq, 1, t), cosf, sinf, perm)

    def dkr_body(d_ref, cos_ref, sin_ref, p_ref, o_ref):
        tot = d_ref[0]
        for hh in range(1, N_HEADS):
            tot = tot + d_ref[hh]
        o_ref[...] = _rope_bwd(tot, cos_ref[...], sin_ref[...], p_ref[...]).astype(BF)

    r64 = pl.BlockSpec((tm, QK_ROPE), lambda i: (i, 0))
    dkr = pl.pallas_call(
        dkr_body, name="mlab_dkr", grid=(nt,),
        in_specs=[pl.BlockSpec((N_HEADS, tm, QK_ROPE), lambda i: (0, i, 0)), r64, r64, pl.BlockSpec((QK_ROPE, QK_ROPE), lambda i: (0, 0))],
        out_specs=r64, out_shape=_sds((s, QK_ROPE), BF), compiler_params=_params())(dkr_h, cosf, sinf, perm)

    def lat_epi(acc, extra, outs, pids):
        dx, dg = _rms_bwd(acc, extra[0][...], extra[1][...], None)
        outs[0][...] = dx.astype(BF)
        _accumulate(outs[1], dg, pids[0])

    def lat_bwd(name, a, a_spec, b, b_spec, n_k, lat, g, rank):
        row = lambda i, k: (i, 0)
        one = lambda i, k: (0, 0)
        return _mm(name, (nt, n_k), [a, b, lat, g.reshape(1, rank)],
                   [a_spec, b_spec, pl.BlockSpec((tm, rank), row), pl.BlockSpec((1, rank), one)],
                   [_sds((s, rank), BF), _sds((1, rank), F32)], [pl.BlockSpec((tm, rank), row), pl.BlockSpec((1, rank), one)],
                   NT, lat_epi, red=1, acc_shape=(tm, rank))

    d_ql, g_qnorm = lat_bwd("mlab_qup", dq, pl.BlockSpec((None, tm, QK_DIM), lambda i, h: (h, i, 0)),
                            w["w_qh"], pl.BlockSpec((None, Q_RANK, QK_DIM), lambda i, h: (h, 0, 0)), N_HEADS,
                            sv["q_lat"], w["q_norm"], Q_RANK)
    d_kvl, g_kvnorm = lat_bwd("mlab_kvup", dkv, pl.BlockSpec((tm, 512), lambda i, kk: (i, kk)),
                              w["g512"], pl.BlockSpec((None, KV_RANK, 512), lambda i, kk: (kk, R512_KV // KV_RANK, 0)), N_DEV,
                              sv["kv_lat"], w["kv_norm"], KV_RANK)

    def in_bwd(dql_ref, dkvl_ref, dkr_ref, dz_ref, wq_ref, wkv_ref, wkr_ref, wz_ref, x_ref, g_ref, dy_ref, dx_ref, dxb_ref, dg_ref):
        acc = (_dot(dql_ref[...], wq_ref[...], NT) + _dot(dkvl_ref[...], wkv_ref[...], NT)
               + _dot(dkr_ref[...], wkr_ref[...], NT) + _dot(dz_ref[...], wz_ref[...], NT))
        dx, dg = _rms_bwd(acc, x_ref[...], g_ref[...], dy_ref[...])
        dx_ref[...] = dx
        dxb_ref[...] = dx.astype(BF)
        _accumulate(dg_ref, dg, pl.program_id(0))

    def full(a):
        return pl.BlockSpec(a.shape, lambda i: (0,) * a.ndim)

    def rows(c):
        return pl.BlockSpec((tm, c), lambda i: (i, 0))

    gm = w["mla_norm"].reshape(1, D_MODEL)
    dx, dxb, g_norm = pl.pallas_call(
        in_bwd, name="mlab_in", grid=(nt,),
        in_specs=[rows(Q_RANK), rows(KV_RANK), rows(QK_ROPE), rows(D_INNER), full(w["w_q"]), full(w["w_kv"]), full(w["w_kr"]),
                  full(w["w_z"]), rows(D_MODEL), full(gm), rows(D_MODEL)],
        out_specs=[rows(D_MODEL), rows(D_MODEL), full(gm)],
        out_shape=[_sds((s, D_MODEL), F32), _sds((s, D_MODEL), BF), _sds((1, D_MODEL), F32)],
        compiler_params=_params())(d_ql, d_kvl, dkr, dz, w["w_q"], w["w_kv"], w["w_kr"], w["w_z"], sv["x"], gm, dy)

    xn = sv["xn"]
    one = lambda j: (0, 0)
    g_q = _tn("mlab_gq", xn, d_ql, (D_MODEL, Q_RANK), (D_MODEL, Q_RANK), one, D_MODEL, Q_RANK, (1,), one, one)
    g_kv = _tn("mlab_gkv", xn, d_kvl, (D_MODEL, KV_RANK), (D_MODEL, KV_RANK), one, D_MODEL, KV_RANK, (1,), one, one)
    g_kr = _tn("mlab_gkr", xn, dkr, (D_MODEL, QK_ROPE), (D_MODEL, QK_ROPE), one, D_MODEL, QK_ROPE, (1,), one, one)
    g_z = _tn("mlab_gz", xn, dz, (D_MODEL, D_INNER), (D_MODEL, 512), lambda j: (0, j), D_MODEL, 512, (4,), one, lambda j: (0, j))
    g_in = jnp.concatenate([g_q, g_kv, g_kr, g_z], axis=1)
    g_qh = _tn("mlab_gqup", sv["qn"], dq, (N_HEADS, Q_RANK, QK_DIM), (None, Q_RANK, QK_DIM), lambda h: (h, 0, 0),
               Q_RANK, QK_DIM, (N_HEADS,), lambda h: (0, 0), lambda h: (h, 0, 0))
    g_kvup = _tn("mlab_gkvup", sv["kvn"], dkv, (N_DEV, KV_RANK, 512), (None, KV_RANK, 512), lambda j: (j, 0, 0),
                 KV_RANK, 512, (N_DEV,), lambda j: (0, 0), lambda j: (0, j))
    g_out = _w_out_grad("mlab_gout", sv["gated"], dyb)
    return dx, dxb, dict(w_in=g_in, w_qh=g_qh, w_kvup=g_kvup, w_out=g_out, norm=g_norm[0], q_norm=g_qnorm[0], kv_norm=g_kvnorm[0])


def _loss_head(x, g, target, tm):
    s, d = x.shape

    def body(x_ref, g_ref, t_ref, dx_ref, dxb_ref, dg_ref, loss_ref):
        i = pl.program_id(0)
        xv, gv = x_ref[...], g_ref[...]
        r = lax.rsqrt(jnp.mean(xv * xv, axis=-1, keepdims=True) + NORM_EPS)
        err = (xv * r) * gv - t_ref[...]
        part = 0.5 * jnp.sum(jnp.mean(err * err, axis=-1, keepdims=True), axis=0, keepdims=True)
        dx, dg = _rms_bwd(err * (1.0 / d), xv, gv, None)
        dx_ref[...] = dx
        dxb_ref[...] = dx.astype(BF)
        _accumulate(dg_ref, dg, i)
        _accumulate(loss_ref, jnp.broadcast_to(part, loss_ref.shape), i)

    row = pl.BlockSpec((tm, d), lambda i: (i, 0))
    one = pl.BlockSpec((1, d), lambda i: (0, 0))
    return pl.pallas_call(
        body, name="loss_head", grid=(s // tm,), in_specs=[row, one, row],
        out_specs=[row, row, one, pl.BlockSpec((8, 128), lambda i: (0, 0))],
        out_shape=[_sds((s, d), F32), _sds((s, d), BF), _sds((1, d), F32), _sds((8, 128), F32)],
        compiler_params=_params())(x, g.reshape(1, d), target)


def _rope_tables(pos):
    inv_freq = ROPE_BASE ** (-jnp.arange(0, QK_ROPE, 2, dtype=F32) / QK_ROPE)
    ang = pos.astype(F32)[:, None] * inv_freq
    cos, sin = jnp.cos(ang), jnp.sin(ang)
    idx = jnp.arange(QK_ROPE)
    perm = (idx[:, None] == (idx[None, :] + QK_ROPE // 2) % QK_ROPE).astype(F32)
    return jnp.concatenate([cos, cos], axis=1), jnp.concatenate([-sin, sin], axis=1), perm


def _local_step(x, pos, target, w):
    s = x.shape[0]
    tm = min(512, s)
    rope = _rope_tables(pos)
    x1, sv0 = _pool_fwd(x, 0, w, tm)
    x2, sv1 = _conv_fwd(x1, w, tm)
    x3, sv2 = _mla_fwd(x2, w, rope, tm)
    x4, sv3 = _pool_fwd(x3, 1, w, tm)
    d4, d4b, g_final, loss = _loss_head(x4, w["final_norm"], target, tm)
    d3, d3b, gp1 = _pool_bwd(d4, d4b, 1, w, sv3, tm)
    d2, d2b, gm = _mla_bwd(d3, d3b, w, sv2, rope, tm)
    d1, d1b, gc = _conv_bwd(d2, d2b, w, sv1, tm)
    d0, _, gp0 = _pool_bwd(d1, d1b, 0, w, sv0, tm)
    grads = dict(pool0=gp0, pool1=gp1, conv=gc, mla=gm, final_norm=g_final[0])
    return loss[0, 0], d0, grads


def _pack_weights(p):
    b512 = jnp.concatenate([p["pool_w_in"].reshape(2048, 512), p["mla_w_kv_up"].reshape(256, 512),
                            p["pool_w_grp"].reshape(512, 512)], axis=0).astype(BF)
    b1024 = jnp.concatenate([p["conv_w_in"].reshape(1024, 1024), p["pool_w_out"].reshape(512, 1024),
                             p["conv_w_out"].reshape(256, 1024), p["mla_w_out"].reshape(256, 1024)], axis=0).astype(BF)
    b384 = p["mla_w_q_up"].reshape(384, 384).astype(BF)
    b344 = p["mla_w_in"].reshape(1024, 344).astype(BF)
    return b512, b1024, b384, b344, _pack_small(p, SMALL_ROWS_AG)


_SMALL_SHARDED = ("pool_norm", "pool_scale", "mla_norm", "mla_q_norm", "mla_kv_norm", "conv_w")
_SMALL_REPLICATED = ("conv_norm", "final_norm")


def _pack_small(p, rows, with_replicated=False):
    parts = [p[k].reshape(-1) for k in _SMALL_SHARDED]
    if with_replicated:
        parts += [p[k].reshape(-1) for k in _SMALL_REPLICATED]
    flat = jnp.concatenate(parts)
    return jnp.pad(flat, (0, rows * 128 - flat.shape[0])).reshape(rows, 128)


_SMALL_SHARD_SHAPES = dict(pool_norm=(2, 128), pool_scale=(2, 256), mla_norm=(1, 128), mla_q_norm=(1, 48),
                           mla_kv_norm=(1, 32), conv_w=(1, 3, 256), conv_norm=(1, 1024), final_norm=(1024,))


def _unpack_small(buf, with_replicated=False):
    flat = buf.reshape(-1)
    out, off = {}, 0
    for k in _SMALL_SHARDED + (_SMALL_REPLICATED if with_replicated else ()):
        shp = _SMALL_SHARD_SHAPES[k]
        n = 1
        for d in shp:
            n *= d
        out[k] = flat[off:off + n].reshape(shp)
        off += n
    return out


def _gathered_views(g512, g1024, g384, g344, gsmall, conv_norm, final_norm):
    flat = gsmall.reshape(N_DEV, -1)

    def cols(off, rows, width):
        return flat[:, off:off + rows * width].reshape(N_DEV, rows, width).transpose(1, 0, 2).reshape(rows, N_DEV * width)

    w_in = g344.transpose(1, 0, 2).reshape(D_MODEL, N_DEV * 344)
    return dict(
        g512=g512, g1024=g1024,
        w_q=w_in[:, :Q_RANK], w_kv=w_in[:, Q_RANK:Q_RANK + KV_RANK],
        w_kr=w_in[:, Q_RANK + KV_RANK:Q_RANK + KV_RANK + QK_ROPE], w_z=w_in[:, Q_RANK + KV_RANK + QK_ROPE:],
        w_qh=g384.reshape(N_DEV, Q_RANK, 2, QK_DIM).transpose(0, 2, 1, 3).reshape(N_HEADS, Q_RANK, QK_DIM),
        pool_norm=cols(0, 2, 128), pool_scale=cols(256, 2, 256), mla_norm=cols(768, 1, 128)[0],
        q_norm=cols(896, 1, 48)[0], kv_norm=cols(944, 1, 32)[0], conv_w=cols(976, 3, 256),
        conv_norm=conv_norm.reshape(D_MODEL), final_norm=final_norm)


def _pack_grads(g):
    p0, p1, cv, ml = g["pool0"], g["pool1"], g["conv"], g["mla"]
    s512 = jnp.concatenate([p0["w_in"], p1["w_in"], ml["w_kvup"], p0["w_grp"], p1["w_grp"]], axis=1)
    out = lambda a: a.reshape(N_DEV, 256, D_MODEL)
    s1024 = jnp.concatenate([cv["w_in"], out(p0["w_out"]), out(p1["w_out"]), out(cv["w_out"]), out(ml["w_out"])], axis=1)
    s384 = ml["w_qh"].reshape(N_DEV, 2, Q_RANK, QK_DIM).transpose(0, 2, 1, 3).reshape(N_DEV, Q_RANK, 2 * QK_DIM)
    s344 = ml["w_in"].reshape(D_MODEL, N_DEV, 344).transpose(1, 0, 2)

    def split(a, rows, width):
        return a.reshape(rows, N_DEV, width).transpose(1, 0, 2).reshape(N_DEV, rows * width)

    rep = lambda a: jnp.broadcast_to(a.reshape(1, -1), (N_DEV, a.size))
    flat = jnp.concatenate([
        split(jnp.stack([p0["norm"], p1["norm"]]), 2, 128), split(jnp.stack([p0["scale"], p1["scale"]]), 2, 256),
        split(ml["norm"], 1, 128), split(ml["q_norm"], 1, 48), split(ml["kv_norm"], 1, 32), split(cv["conv_w"], 3, 256),
        rep(cv["norm"]), rep(g["final_norm"])], axis=1)
    small = jnp.pad(flat, ((0, 0), (0, SMALL_ROWS_RS * 128 - flat.shape[1]))).reshape(N_DEV, SMALL_ROWS_RS, 128)
    return s512, s1024, s384, s344, small


def _exchange(name, arrays, gather):
    n = len(arrays)
    outs = [_sds(((N_DEV,) + a.shape) if gather else a.shape, a.dtype) for a in arrays]

    def body(*refs):
        ins, dsts = refs[:n], refs[n:2 * n]
        send_sems, recv_sems, local_sems = refs[2 * n:]
        x, y, c = lax.axis_index("x"), lax.axis_index("y"), lax.axis_index("c")
        me = 4 * x + 2 * y + c
        local = []
        for a in range(n):
            cp = pltpu.make_async_copy(ins[a] if gather else ins[a].at[me], dsts[a].at[me], local_sems.at[a])
            cp.start()
            local.append(cp)
        remote = []
        for k in range(1, N_DEV):
            px = 1 - x if k & 4 else x
            py = 1 - y if k & 2 else y
            pc = 1 - c if k & 1 else c
            peer = 4 * px + 2 * py + pc
            for a in range(n):
                cp = pltpu.make_async_remote_copy(
                    src_ref=ins[a] if gather else ins[a].at[peer], dst_ref=dsts[a].at[me],
                    send_sem=send_sems.at[a, k - 1], recv_sem=recv_sems.at[a, k - 1],
                    device_id=(px, py, pc), device_id_type=pl.DeviceIdType.MESH)
                cp.start()
                remote.append(cp)
        for cp in remote:
            cp.wait()
        for cp in local:
            cp.wait()

    hbm = pl.BlockSpec(memory_space=pltpu.HBM)
    return pl.pallas_call(
        body, name=name, in_specs=[hbm] * n, out_specs=[hbm] * n, out_shape=outs,
        scratch_shapes=[pltpu.SemaphoreType.DMA((n, N_DEV - 1)), pltpu.SemaphoreType.DMA((n, N_DEV - 1)),
                        pltpu.SemaphoreType.DMA((n,))])(*arrays)


def _adamw_math(g, w, m, v):
    m = ADAM_B1 * m + (1.0 - ADAM_B1) * g
    v = ADAM_B2 * v + (1.0 - ADAM_B2) * (g * g)
    m_hat = m / (1.0 - ADAM_B1 ** ADAM_STEP)
    v_hat = v / (1.0 - ADAM_B2 ** ADAM_STEP)
    delta = -ADAM_LR * (m_hat / (jnp.sqrt(v_hat) + ADAM_EPS) + ADAM_WD * w)
    return delta, m, v


def _sum_adamw(name, recv, row_off, w, m, v, tr):
    shape = w.shape
    width = recv.shape[-1]
    w2, m2, v2 = (a.reshape(-1, width) for a in (w, m, v))
    rows = w2.shape[0]
    base = row_off // tr

    def body(r_ref, w_ref, m_ref, v_ref, g_ref, d_ref, mo_ref, vo_ref):
        g = r_ref[0].astype(F32)
        for src in range(1, N_DEV):
            g = g + r_ref[src].astype(F32)
        delta, mn, vn = _adamw_math(g, w_ref[...], m_ref[...], v_ref[...])
        g_ref[...] = g
        d_ref[...] = delta
        mo_ref[...] = mn
        vo_ref[...] = vn

    blk = pl.BlockSpec((tr, width), lambda i: (i, 0))
    outs = pl.pallas_call(
        body, name=name, grid=(rows // tr,),
        in_specs=[pl.BlockSpec((N_DEV, tr, width), lambda i: (0, base + i, 0)), blk, blk, blk],
        out_specs=[blk] * 4, out_shape=[_sds((rows, width), F32)] * 4, compiler_params=_params())(recv, w2, m2, v2)
    return tuple(o.reshape(shape) for o in outs)


_WEIGHTS = ("pool_norm", "pool_w_in", "pool_w_grp", "pool_scale", "pool_w_out", "conv_norm", "conv_w_in", "conv_w", "conv_w_out",
            "mla_norm", "mla_w_in", "mla_q_norm", "mla_w_q_up", "mla_kv_norm", "mla_w_kv_up", "mla_w_out", "final_norm")


def _step(x, positions, loss_target, p, m, v):
    gathered = _exchange("gather_weights", list(_pack_weights(p)), gather=True)
    w = _gathered_views(*gathered, p["conv_norm"], p["final_norm"])
    loss, grad_x, grads = _local_step(x[0], positions[0], loss_target[0], w)
    r512, r1024, r384, r344, rsmall = _exchange("scatter_grads", list(_pack_grads(grads)), gather=False)

    res = {}
    res["pool_w_in"] = _sum_adamw("adam_pool_w_in", r512, R512_W_IN, p["pool_w_in"], m["pool_w_in"], v["pool_w_in"], 256)
    res["mla_w_kv_up"] = _sum_adamw("adam_mla_w_kv_up", r512, R512_KV, p["mla_w_kv_up"], m["mla_w_kv_up"], v["mla_w_kv_up"], 256)
    res["pool_w_grp"] = _sum_adamw("adam_pool_w_grp", r512, R512_GRP, p["pool_w_grp"], m["pool_w_grp"], v["pool_w_grp"], 256)
    res["conv_w_in"] = _sum_adamw("adam_conv_w_in", r1024, R1024_CONV_IN, p["conv_w_in"], m["conv_w_in"], v["conv_w_in"], 256)
    res["pool_w_out"] = _sum_adamw("adam_pool_w_out", r1024, R1024_POOL_OUT, p["pool_w_out"], m["pool_w_out"], v["pool_w_out"], 256)
    res["conv_w_out"] = _sum_adamw("adam_conv_w_out", r1024, R1024_CONV_OUT, p["conv_w_out"], m["conv_w_out"], v["conv_w_out"], 256)
    res["mla_w_out"] = _sum_adamw("adam_mla_w_out", r1024, R1024_MLA_OUT, p["mla_w_out"], m["mla_w_out"], v["mla_w_out"], 256)
    res["mla_w_q_up"] = _sum_adamw("adam_mla_w_q_up", r384, 0, p["mla_w_q_up"], m["mla_w_q_up"], v["mla_w_q_up"], 384)
    res["mla_w_in"] = _sum_adamw("adam_mla_w_in", r344, 0, p["mla_w_in"], m["mla_w_in"], v["mla_w_in"], 256)
    small = _sum_adamw("adam_small", rsmall, 0, _pack_small(p, SMALL_ROWS_RS, True), _pack_small(m, SMALL_ROWS_RS, True),
                       _pack_small(v, SMALL_ROWS_RS, True), SMALL_ROWS_RS)
    small = [_unpack_small(a, True) for a in small]
    for k in _SMALL_SHARDED + _SMALL_REPLICATED:
        res[k] = tuple(part[k] for part in small)

    loss = lax.psum(loss, ("x", "y", "c"))
    out = [loss, grad_x[None]]
    for part in range(4):
        out += [res[k][part] for k in _WEIGHTS]
    return tuple(out)


def kernel(x, positions, pool_norm, pool_w_in, pool_w_grp, pool_scale, pool_w_out, conv_norm, conv_w_in, conv_w, conv_w_out, mla_norm, mla_w_in, mla_q_norm, mla_w_q_up, mla_kv_norm, mla_w_kv_up, mla_w_out, final_norm, loss_target, m_pool_norm, m_pool_w_in, m_pool_w_grp, m_pool_scale, m_pool_w_out, m_conv_norm, m_conv_w_in, m_conv_w, m_conv_w_out, m_mla_norm, m_mla_w_in, m_mla_q_norm, m_mla_w_q_up, m_mla_kv_norm, m_mla_w_kv_up, m_mla_w_out, m_final_norm, v_pool_norm, v_pool_w_in, v_pool_w_grp, v_pool_scale, v_pool_w_out, v_conv_norm, v_conv_w_in, v_conv_w, v_conv_w_out, v_mla_norm, v_mla_w_in, v_mla_q_norm, v_mla_w_q_up, v_mla_kv_norm, v_mla_w_kv_up, v_mla_w_out, v_final_norm):
    p = dict(pool_norm=pool_norm, pool_w_in=pool_w_in, pool_w_grp=pool_w_grp, pool_scale=pool_scale, pool_w_out=pool_w_out,
             conv_norm=conv_norm, conv_w_in=conv_w_in, conv_w=conv_w, conv_w_out=conv_w_out, mla_norm=mla_norm, mla_w_in=mla_w_in,
             mla_q_norm=mla_q_norm, mla_w_q_up=mla_w_q_up, mla_kv_norm=mla_kv_norm, mla_w_kv_up=mla_w_kv_up, mla_w_out=mla_w_out,
             final_norm=final_norm)
    m = dict(pool_norm=m_pool_norm, pool_w_in=m_pool_w_in, pool_w_grp=m_pool_w_grp, pool_scale=m_pool_scale, pool_w_out=m_pool_w_out,
             conv_norm=m_conv_norm, conv_w_in=m_conv_w_in, conv_w=m_conv_w, conv_w_out=m_conv_w_out, mla_norm=m_mla_norm,
             mla_w_in=m_mla_w_in, mla_q_norm=m_mla_q_norm, mla_w_q_up=m_mla_w_q_up, mla_kv_norm=m_mla_kv_norm,
             mla_w_kv_up=m_mla_w_kv_up, mla_w_out=m_mla_w_out, final_norm=m_final_norm)
    v = dict(pool_norm=v_pool_norm, pool_w_in=v_pool_w_in, pool_w_grp=v_pool_w_grp, pool_scale=v_pool_scale, pool_w_out=v_pool_w_out,
             conv_norm=v_conv_norm, conv_w_in=v_conv_w_in, conv_w=v_conv_w, conv_w_out=v_conv_w_out, mla_norm=v_mla_norm,
             mla_w_in=v_mla_w_in, mla_q_norm=v_mla_q_norm, mla_w_q_up=v_mla_w_q_up, mla_kv_norm=v_mla_kv_norm,
             mla_w_kv_up=v_mla_w_kv_up, mla_w_out=v_mla_w_out, final_norm=v_final_norm)
    return _step(x, positions, loss_target, p, m, v)
```

```python
import functools

import jax
import jax.numpy as jnp
from jax import lax
from jax.experimental import pallas as pl
from jax.experimental.pallas import tpu as pltpu

BF = jnp.bfloat16
F32 = jnp.float32

N_DEV = 8
D_MODEL = 1024
D_INNER = 2048
POOL_WINDOWS = (2, 4, 8, 16)
POOL_GROUP = 512
N_HEADS = 16
QK_NOPE = 128
QK_ROPE = 64
QK_DIM = QK_NOPE + QK_ROPE
V_DIM = 128
Q_RANK = 384
KV_RANK = 256
ATTN_SCALE = QK_DIM ** -0.5
LOG2_E = 1.4426950408889634
LN_2 = 0.6931471805599453
Q_PRESCALE = ATTN_SCALE * LOG2_E
ATTN_TILE = 512
ROPE_BASE = 10000.0
NORM_EPS = 1e-6
NEG_BIG = -1e30

ADAM_LR = 0.001
ADAM_B1 = 0.9
ADAM_B2 = 0.999
ADAM_EPS = 1e-08
ADAM_WD = 0.01
ADAM_STEP = 10

VMEM_LIMIT_BYTES = 52 * 1024 * 1024
POOL_HALO = 32
CONV_HALO = 8

NN = (((1,), (0,)), ((), ()))
NT = (((1,), (1,)), ((), ()))
TN = (((0,), (0,)), ((), ()))

POOL_GRP_ROW, POOL_512_ROWS = 1024, 1280
CONV_OUT_ROW, CONV_1024_ROWS = 1024, 1280
SMALL_ROWS_AG = 16
SMALL_ROWS_RS = 32


def _sds(shape, dtype):
    return jax.ShapeDtypeStruct(tuple(shape), dtype)


def _params():
    return pltpu.CompilerParams(vmem_limit_bytes=VMEM_LIMIT_BYTES)


_ANY = pl.BlockSpec(memory_space=pl.ANY)


def _dot(a, b, dims):
    return lax.dot_general(a, b, dims, preferred_element_type=F32)


def _sig(z):
    return 1.0 / (1.0 + jnp.exp(-z))


def _silu_and_grad(z):
    sig = _sig(z)
    return z * sig, sig * (1.0 + z * (1.0 - sig))


def _rope_swap(x, p):
    return jnp.dot(x, p, precision=lax.Precision.HIGHEST, preferred_element_type=F32)


def _rope_fwd(x, cosf, sinf, p):
    return x * cosf + _rope_swap(x, p) * sinf


def _rope_bwd(dy, cosf, sinf, p):
    return dy * cosf + _rope_swap(dy * sinf, p)


def _rms_bwd(dxn, x, g, res):
    r = lax.rsqrt(jnp.mean(x * x, axis=-1, keepdims=True) + NORM_EPS)
    v = dxn * g
    dx = r * v - x * ((r * r * r) * jnp.mean(v * x, axis=-1, keepdims=True))
    if res is not None:
        dx = dx + res
    dg = jnp.sum(dxn * (x * r), axis=0, keepdims=True)
    return dx, dg


def _accumulate(ref, val, step):
    @pl.when(step == 0)
    def _():
        ref[...] = val

    @pl.when(step > 0)
    def _():
        ref[...] += val


def _mm(name, grid, ins, in_specs, outs, out_specs, dims, epi, red=None, acc_shape=None):
    n_in, n_out = len(ins), len(outs)
    n_red = None if red is None else grid[red]

    def body(*refs):
        in_refs, out_refs = refs[:n_in], refs[n_in:n_in + n_out]
        pids = tuple(pl.program_id(ax) for ax in range(len(grid)))
        a, b = in_refs[0][...], in_refs[1][...]
        if a.ndim == 3:
            a = a.reshape(-1, a.shape[-1])
        if b.ndim == 3:
            b = b.reshape(-1, b.shape[-1])
        part = _dot(a.astype(BF), b.astype(BF), dims)
        if red is None:
            epi(part, in_refs[2:], out_refs, pids)
        else:
            acc = refs[n_in + n_out]
            k = pids[red]
            _accumulate(acc, part, k)

            @pl.when(k == n_red - 1)
            def _():
                epi(acc[...], in_refs[2:], out_refs, pids)

    scratch = [] if red is None else [pltpu.VMEM(acc_shape, F32)]
    return pl.pallas_call(body, name=name, grid=grid, in_specs=in_specs, out_specs=out_specs, out_shape=outs,
                          scratch_shapes=scratch, compiler_params=_params())(*ins)


def _store(part, extra, outs, pids):
    outs[0][...] = part.astype(outs[0].dtype)


def _rms_fwd(name, x, g, tm):
    s, d = x.shape

    def body(x_ref, g_ref, o_ref):
        xv = x_ref[...]
        r = lax.rsqrt(jnp.mean(xv * xv, axis=-1, keepdims=True) + NORM_EPS)
        o_ref[...] = ((xv * r) * g_ref[...]).astype(BF)

    return pl.pallas_call(body, name=name, grid=(s // tm,),
                          in_specs=[pl.BlockSpec((tm, d), lambda i: (i, 0)), pl.BlockSpec((1, d), lambda i: (0, 0))],
                          out_specs=pl.BlockSpec((tm, d), lambda i: (i, 0)), out_shape=_sds((s, d), BF),
                          compiler_params=_params())(x, g.reshape(1, d))


def _tn(name, a, b, out_shape, out_block, out_index, a_cols, b_cols, grid, a_index, b_index):
    s = a.shape[-2]
    a_block = (s, a_cols) if a.ndim == 2 else (None, s, a_cols)
    b_block = (s, b_cols) if b.ndim == 2 else (None, s, b_cols)

    def epi(part, extra, outs, pids):
        outs[0][...] = part.astype(BF).reshape(outs[0].shape)

    return _mm(name, grid, [a, b], [pl.BlockSpec(a_block, a_index), pl.BlockSpec(b_block, b_index)],
               [_sds(out_shape, BF)], [pl.BlockSpec(out_block, out_index)], TN, epi)[0]


def _pool_window_fwd(name, h, tm):
    s = h.shape[0]
    hb = POOL_HALO

    def body(u_ref, halo_ref, o_ref, e_ref, a_ref, b_ref):
        i = pl.program_id(0)
        row = lax.broadcasted_iota(jnp.int32, (tm, 1), 0) + i * tm
        for g, w in enumerate(POOL_WINDOWS):
            cs = slice(g * POOL_GROUP, (g + 1) * POOL_GROUP)
            e_ref[0:hb, :] = jnp.where(i > 0, halo_ref[:, cs], 0.0)
            e_ref[hb:, :] = u_ref[:, cs]
            src, bufs = e_ref, (a_ref, b_ref)
            for lv in range(1, w.bit_length()):
                dst, st, sh = bufs[(lv - 1) % 2], 8 * lv, 2 ** (lv - 1)
                n = hb + tm - st
                dst[st:, :] = src[st:, :] + src[pl.ds(st - sh, n), :]
                src = dst
            cnt = jnp.minimum(row + 1, w).astype(F32)
            o_ref[:, cs] = (src[hb:, :] / cnt - u_ref[:, cs]).astype(BF)

    per = tm // hb
    return pl.pallas_call(
        body, name=name, grid=(s // tm,),
        in_specs=[pl.BlockSpec((tm, D_INNER), lambda i: (i, 0)),
                  pl.BlockSpec((hb, D_INNER), lambda i: (jnp.maximum(i * per - 1, 0), 0))],
        out_specs=pl.BlockSpec((tm, D_INNER), lambda i: (i, 0)), out_shape=_sds((s, D_INNER), BF),
        scratch_shapes=[pltpu.VMEM((hb + tm, POOL_GROUP), F32)] * 3, compiler_params=_params())(h, h)


def _pool_window_bwd(name, dp, tm):
    s = dp.shape[0]
    nt = s // tm
    hb = POOL_HALO

    def body(d_ref, halo_ref, o_ref, e_ref, a_ref, b_ref):
        i = pl.program_id(0)
        row = lax.broadcasted_iota(jnp.int32, (tm, 1), 0) + i * tm
        hrow = lax.broadcasted_iota(jnp.int32, (hb, 1), 0) + (i + 1) * tm
        for g, w in enumerate(POOL_WINDOWS):
            cs = slice(g * POOL_GROUP, (g + 1) * POOL_GROUP)
            e_ref[0:tm, :] = d_ref[:, cs] / jnp.minimum(row + 1, w).astype(F32)
            e_ref[tm:, :] = jnp.where(i < nt - 1, halo_ref[:, cs] / jnp.minimum(hrow + 1, w).astype(F32), 0.0)
            src, bufs = e_ref, (a_ref, b_ref)
            for lv in range(1, w.bit_length()):
                dst, sh = bufs[(lv - 1) % 2], 2 ** (lv - 1)
                n = tm + hb - 8 * lv
                dst[0:n, :] = src[0:n, :] + src[pl.ds(sh, n), :]
                src = dst
            o_ref[:, cs] = (src[0:tm, :] - d_ref[:, cs]).astype(BF)

    per = tm // hb
    last = s // hb - 1
    return pl.pallas_call(
        body, name=name, grid=(nt,),
        in_specs=[pl.BlockSpec((tm, D_INNER), lambda i: (i, 0)),
                  pl.BlockSpec((hb, D_INNER), lambda i: (jnp.minimum((i + 1) * per, last), 0))],
        out_specs=pl.BlockSpec((tm, D_INNER), lambda i: (i, 0)), out_shape=_sds((s, D_INNER), BF),
        scratch_shapes=[pltpu.VMEM((hb + tm, POOL_GROUP), F32)] * 3, compiler_params=_params())(dp, dp)


def _grp_block():
    base = POOL_GRP_ROW // 64
    return pl.BlockSpec((N_DEV, 64, POOL_GROUP), lambda i, g: (0, base + g, 0))


def _pool_fwd(x, l, w, tm):
    s = x.shape[0]
    nt = s // tm
    n = f"pool{l}"
    xn = _rms_fwd(n + "_rms", x, w["norm"], tm)
    (h,) = _mm(n + "_in", (nt, 8), [xn, w["g512"]],
               [pl.BlockSpec((tm, D_MODEL), lambda i, j: (i, 0)), pl.BlockSpec((None, D_MODEL, 512), lambda i, j: (j, 0, 0))],
               [_sds((s, 2 * D_INNER), F32)], [pl.BlockSpec((tm, 512), lambda i, j: (i, j))], NN, _store)
    pooled = _pool_window_fwd(n + "_win", h, tm)

    def gate(part, extra, outs, pids):
        z = extra[0][...]
        outs[0][...] = ((part * extra[1][...]) * (z * _sig(z))).astype(BF)

    (gated,) = _mm(n + "_grp", (nt, 4), [pooled, w["g512"], h, w["scale"].reshape(1, D_INNER)],
                   [pl.BlockSpec((tm, 512), lambda i, g: (i, g)), _grp_block(),
                    pl.BlockSpec((tm, 512), lambda i, g: (i, 4 + g)), pl.BlockSpec((1, 512), lambda i, g: (0, g))],
                   [_sds((s, D_INNER), BF)], [pl.BlockSpec((tm, 512), lambda i, g: (i, g))], NN, gate)
    y = _out_proj(n + "_out", gated, w["g1024"], 0, x, tm)
    return y, dict(x=x, xn=xn, h=h, pooled=pooled, gated=gated)


def _out_proj(name, gated, g1024, row_block, x, tm):
    s = x.shape[0]

    def epi(part, extra, outs, pids):
        outs[0][...] = part + extra[0][...]

    return _mm(name, (s // tm, 2), [gated, g1024, x],
               [pl.BlockSpec((tm, D_INNER), lambda i, j: (i, 0)), pl.BlockSpec((N_DEV, 256, 512), lambda i, j: (0, row_block, j)),
                pl.BlockSpec((tm, 512), lambda i, j: (i, j))],
               [_sds((s, D_MODEL), F32)], [pl.BlockSpec((tm, 512), lambda i, j: (i, j))], NN, epi)[0]


def _w_out_nt_block(row_block):
    return pl.BlockSpec((2, 256, D_MODEL), lambda j, i: (j, row_block, 0))


def _in_proj_bwd(name, dh, wbuf, w_index, n_k, x, g, dy, tm):
    s = x.shape[0]

    def epi(acc, extra, outs, pids):
        dx, dg = _rms_bwd(acc, extra[0][...], extra[1][...], extra[2][...])
        outs[0][...] = dx
        outs[1][...] = dx.astype(BF)
        _accumulate(outs[2], dg, pids[0])

    row = lambda i, k: (i, 0)
    return _mm(name, (s // tm, n_k), [dh, wbuf, x, g.reshape(1, D_MODEL), dy],
               [pl.BlockSpec((tm, 512), lambda i, k: (i, k)), pl.BlockSpec((None, D_MODEL, 512), w_index),
                pl.BlockSpec((tm, D_MODEL), row), pl.BlockSpec((1, D_MODEL), lambda i, k: (0, 0)), pl.BlockSpec((tm, D_MODEL), row)],
               [_sds((s, D_MODEL), F32), _sds((s, D_MODEL), BF), _sds((1, D_MODEL), F32)],
               [pl.BlockSpec((tm, D_MODEL), row), pl.BlockSpec((tm, D_MODEL), row), pl.BlockSpec((1, D_MODEL), lambda i, k: (0, 0))],
               NT, epi, red=1, acc_shape=(tm, D_MODEL))


def _w_out_grad(name, gated, dyb):
    s = gated.shape[0]
    return _tn(name, gated, dyb, (D_INNER, D_MODEL), (512, D_MODEL), lambda i: (i, 0), 512, D_MODEL, (4,),
               lambda i: (0, i), lambda i: (0, 0))


def _pool_bwd(dy, dyb, l, w, sv, tm, dep):
    s = dy.shape[0]
    nt = s // tm
    n = f"pool{l}b"
    h, pooled = sv["h"], sv["pooled"]
    scale = w["scale"].reshape(1, D_INNER)
    (mp,) = _mm(n + "_grp", (nt, 4), [pooled, w["g512"]] + dep,
                [pl.BlockSpec((tm, 512), lambda i, g: (i, g)), _grp_block()] + [_ANY] * len(dep),
                [_sds((s, D_INNER), F32)], [pl.BlockSpec((tm, 512), lambda i, g: (i, g))], NN, _store)

    def gate_bwd(part, extra, outs, pids):
        z, mpv, sc = extra[0][...], extra[1][...], extra[2][...]
        sz, dsz = _silu_and_grad(z)
        dm = part * sz
        outs[0][...] = (dm * sc).astype(BF)
        outs[1][...] = (part * (mpv * sc) * dsz).astype(BF)
        _accumulate(outs[2], jnp.sum(dm * mpv, axis=0, keepdims=True), pids[1])

    tile = lambda j, i: (i, j)
    dmp, dz, dscale = _mm(
        n + "_out", (4, nt), [dyb, w["g1024"], h, mp, scale],
        [pl.BlockSpec((tm, D_MODEL), lambda j, i: (i, 0)), _w_out_nt_block(0),
         pl.BlockSpec((tm, 512), lambda j, i: (i, 4 + j)), pl.BlockSpec((tm, 512), tile), pl.BlockSpec((1, 512), lambda j, i: (0, j))],
        [_sds((s, D_INNER), BF), _sds((s, D_INNER), BF), _sds((1, D_INNER), F32)],
        [pl.BlockSpec((tm, 512), tile), pl.BlockSpec((tm, 512), tile), pl.BlockSpec((1, 512), lambda j, i: (0, j))], NT, gate_bwd)
    (dpool,) = _mm(n + "_grpT", (nt, 4), [dmp, w["g512"]],
                   [pl.BlockSpec((tm, 512), lambda i, g: (i, g)), _grp_block()],
                   [_sds((s, D_INNER), F32)], [pl.BlockSpec((tm, 512), lambda i, g: (i, g))], NT, _store)
    du = _pool_window_bwd(n + "_win", dpool, tm)
    dh = jnp.concatenate([du, dz], axis=1)
    dx, dxb, dnorm = _in_proj_bwd(n + "_in", dh, w["g512"], lambda i, k: (k, 0, 0), 8, sv["x"], w["norm"], dy, tm)
    g_in = _tn(n + "_gin", sv["xn"], dh, (N_DEV, D_MODEL, 512), (None, D_MODEL, 512), lambda j: (j, 0, 0),
               D_MODEL, 512, (8,), lambda j: (0, 0), lambda j: (0, j))
    g_out = _w_out_grad(n + "_gout", sv["gated"], dyb)
    g_grp = _tn(n + "_ggrp", pooled, dmp, (N_DEV, 256, 512), (N_DEV, 64, 512), lambda g: (0, g, 0),
                512, 512, (4,), lambda g: (0, g), lambda g: (0, g))
    grads = dict(s512=jnp.concatenate([g_in, g_grp], axis=1), s1024=g_out.reshape(N_DEV, 256, D_MODEL), norm=dnorm[0], scale=dscale[0])
    return dx, dxb, grads


def _conv_in_index(i, j):
    return (j // 2, 0, j % 2)


def _conv_fwd(x, w, tm):
    s = x.shape[0]
    nt = s // tm
    xn = _rms_fwd("conv_rms", x, w["norm"], tm)
    (h,) = _mm("conv_in", (nt, 16), [xn, w["g1024"]],
               [pl.BlockSpec((tm, D_MODEL), lambda i, j: (i, 0)), pl.BlockSpec((None, D_MODEL, 512), _conv_in_index)],
               [_sds((s, 4 * D_INNER), F32)], [pl.BlockSpec((tm, 512), lambda i, j: (i, j))], NN, _store)
    per = tm // CONV_HALO

    def body(b_ref, c_ref, h_ref, z_ref, cp_ref, hp_ref, w_ref, o_ref, e_ref):
        i = pl.program_id(0)
        ch = c_ref[...] * h_ref[...]
        e_ref[0:CONV_HALO, :] = jnp.where(i > 0, cp_ref[...] * hp_ref[...], 0.0)
        e_ref[CONV_HALO:, :] = ch
        co = (w_ref[2:3, :] * ch + w_ref[1:2, :] * e_ref[pl.ds(CONV_HALO - 1, tm), :]
              + w_ref[0:1, :] * e_ref[pl.ds(CONV_HALO - 2, tm), :])
        z = z_ref[...]
        o_ref[...] = ((b_ref[...] * co) * (z * _sig(z))).astype(BF)

    def col(q):
        return pl.BlockSpec((tm, 512), lambda i, j: (i, 4 * q + j))

    def prev(q):
        return pl.BlockSpec((CONV_HALO, 512), lambda i, j: (jnp.maximum(i * per - 1, 0), 4 * q + j))

    gated = pl.pallas_call(
        body, name="conv_mix", grid=(nt, 4),
        in_specs=[col(0), col(1), col(2), col(3), prev(1), prev(2), pl.BlockSpec((3, 512), lambda i, j: (0, j))],
        out_specs=pl.BlockSpec((tm, 512), lambda i, j: (i, j)), out_shape=_sds((s, D_INNER), BF),
        scratch_shapes=[pltpu.VMEM((CONV_HALO + tm, 512), F32)], compiler_params=_params())(h, h, h, h, h, h, w["conv_w"])
    y = _out_proj("conv_out", gated, w["g1024"], CONV_OUT_ROW // 256, x, tm)
    return y, dict(x=x, xn=xn, h=h, gated=gated)


def _conv_bwd(dy, dyb, w, sv, tm, dep):
    s = dy.shape[0]
    nt = s // tm
    h = sv["h"]
    (dg,) = _mm("convb_out", (4, nt), [dyb, w["g1024"]] + dep,
                [pl.BlockSpec((tm, D_MODEL), lambda j, i: (i, 0)), _w_out_nt_block(CONV_OUT_ROW // 256)] + [_ANY] * len(dep),
                [_sds((s, D_INNER), F32)], [pl.BlockSpec((tm, 512), lambda j, i: (i, j))], NT, _store)
    per = tm // CONV_HALO
    last = s // CONV_HALO - 1

    def body(dg_ref, b_ref, c_ref, h_ref, z_ref, cp_ref, hp_ref, dgn_ref, bn_ref, zn_ref, w_ref,
             db_ref, dc_ref, dh_ref, dz_ref, dw_ref, e_ref, f_ref):
        i = pl.program_id(1)
        w0, w1, w2 = w_ref[0:1, :], w_ref[1:2, :], w_ref[2:3, :]
        c, hh, b = c_ref[...], h_ref[...], b_ref[...]
        ch = c * hh
        e_ref[0:CONV_HALO, :] = jnp.where(i > 0, cp_ref[...] * hp_ref[...], 0.0)
        e_ref[CONV_HALO:, :] = ch
        ch1 = e_ref[pl.ds(CONV_HALO - 1, tm), :]
        ch2 = e_ref[pl.ds(CONV_HALO - 2, tm), :]
        co = w2 * ch + w1 * ch1 + w0 * ch2
        sz, dsz = _silu_and_grad(z_ref[...])
        dgv = dg_ref[...]
        dyv = dgv * sz
        dz_ref[...] = (dgv * (b * co) * dsz).astype(BF)
        db_ref[...] = (dyv * co).astype(BF)
        dco = dyv * b
        zn = zn_ref[...]
        f_ref[0:tm, :] = dco
        f_ref[tm:, :] = jnp.where(i < nt - 1, dgn_ref[...] * (zn * _sig(zn)) * bn_ref[...], 0.0)
        dch = w2 * dco + w1 * f_ref[pl.ds(1, tm), :] + w0 * f_ref[pl.ds(2, tm), :]
        dc_ref[...] = (dch * hh).astype(BF)
        dh_ref[...] = (dch * c).astype(BF)
        for tap, shifted in enumerate((ch2, ch1, ch)):
            _accumulate(dw_ref.at[tap:tap + 1, :], jnp.sum(dco * shifted, axis=0, keepdims=True), i)

    def col(q):
        return pl.BlockSpec((tm, 512), lambda j, i: (i, 4 * q + j))

    def prev(q):
        return pl.BlockSpec((CONV_HALO, 512), lambda j, i: (jnp.maximum(i * per - 1, 0), 4 * q + j))

    def nxt(q):
        return pl.BlockSpec((CONV_HALO, 512), lambda j, i: (jnp.minimum((i + 1) * per, last), 4 * q + j))

    tile = pl.BlockSpec((tm, 512), lambda j, i: (i, j))
    wspec = pl.BlockSpec((3, 512), lambda j, i: (0, j))
    db, dc, dhh, dz, dw = pl.pallas_call(
        body, name="convb_mix", grid=(4, nt),
        in_specs=[tile, col(0), col(1), col(2), col(3), prev(1), prev(2), nxt(0), nxt(0), nxt(3), wspec],
        out_specs=[tile, tile, tile, tile, wspec],
        out_shape=[_sds((s, D_INNER), BF)] * 4 + [_sds((3, D_INNER), F32)],
        scratch_shapes=[pltpu.VMEM((CONV_HALO + tm, 512), F32)] * 2, compiler_params=_params(),
    )(dg, h, h, h, h, h, h, dg, h, h, w["conv_w"])
    dh = jnp.concatenate([db, dc, dhh, dz], axis=1)
    dx, dxb, dnorm = _in_proj_bwd("convb_in", dh, w["g1024"], lambda i, k: (k // 2, 0, k % 2), 16, sv["x"], w["norm"], dy, tm)
    g_in = _tn("convb_gin", sv["xn"], dh, (N_DEV, D_MODEL, D_MODEL), (None, D_MODEL, 512), lambda j: (j // 2, 0, j % 2),
               D_MODEL, 512, (16,), lambda j: (0, 0), lambda j: (0, j))
    g_out = _w_out_grad("convb_gout", sv["gated"], dyb)
    s1024 = jnp.concatenate([g_in, g_out.reshape(N_DEV, 256, D_MODEL)], axis=1)
    return dx, dxb, dict(s1024=s1024, norm=dnorm[0], conv_w=dw)


def _attn_tiles(s):
    t = min(ATTN_TILE, s)
    return t, s // t


def _causal_keep(t, keys_on_rows):
    r = lax.broadcasted_iota(jnp.int32, (t, t), 0)
    c = lax.broadcasted_iota(jnp.int32, (t, t), 1)
    return (r <= c) if keys_on_rows else (c <= r)


def _mla_fwd(x, w, rope, tm):
    s = x.shape[0]
    nt = s // tm
    cosf, sinf, perm = rope
    xn = _rms_fwd("mla_rms", x, w["norm"], tm)

    def in_body(xn_ref, wq_ref, wkv_ref, wkr_ref, wz_ref, gq_ref, gkv_ref, cos_ref, sin_ref, p_ref,
                ql_ref, kvl_ref, qn_ref, kvn_ref, krr_ref, z_ref):
        xv = xn_ref[...]
        ql = _dot(xv, wq_ref[...], NN)
        kvl = _dot(xv, wkv_ref[...], NN)
        ql_ref[...] = ql
        kvl_ref[...] = kvl
        rq = lax.rsqrt(jnp.mean(ql * ql, axis=-1, keepdims=True) + NORM_EPS)
        qn_ref[...] = ((ql * rq) * gq_ref[...]).astype(BF)
        rkv = lax.rsqrt(jnp.mean(kvl * kvl, axis=-1, keepdims=True) + NORM_EPS)
        kvn_ref[...] = ((kvl * rkv) * gkv_ref[...]).astype(BF)
        kr = _dot(xv, wkr_ref[...], NN)
        krr_ref[...] = _rope_fwd(kr, cos_ref[...], sin_ref[...], p_ref[...]).astype(BF)
        z_ref[...] = _dot(xv, wz_ref[...], NN)

    def full(a):
        return pl.BlockSpec(a.shape, lambda i: (0,) * a.ndim)

    def rows(c):
        return pl.BlockSpec((tm, c), lambda i: (i, 0))

    gq, gkv = w["q_norm"].reshape(1, Q_RANK), w["kv_norm"].reshape(1, KV_RANK)
    q_lat, kv_lat, qn, kvn, krr, z = pl.pallas_call(
        in_body, name="mla_in", grid=(nt,),
        in_specs=[rows(D_MODEL), full(w["w_q"]), full(w["w_kv"]), full(w["w_kr"]), full(w["w_z"]), full(gq), full(gkv),
                  rows(QK_ROPE), rows(QK_ROPE), full(perm)],
        out_specs=[rows(Q_RANK), rows(KV_RANK), rows(Q_RANK), rows(KV_RANK), rows(QK_ROPE), rows(D_INNER)],
        out_shape=[_sds((s, Q_RANK), F32), _sds((s, KV_RANK), F32), _sds((s, Q_RANK), BF), _sds((s, KV_RANK), BF),
                   _sds((s, QK_ROPE), BF), _sds((s, D_INNER), F32)],
        compiler_params=_params())(xn, w["w_q"], w["w_kv"], w["w_kr"], w["w_z"], gq, gkv, cosf, sinf, perm)

    def q_epi(part, extra, outs, pids):
        outs[0][:, 0:QK_NOPE] = (part[:, 0:QK_NOPE] * Q_PRESCALE).astype(BF)
        roped = _rope_fwd(part[:, QK_NOPE:QK_DIM], extra[0][...], extra[1][...], extra[2][...])
        outs[0][:, QK_NOPE:QK_DIM] = (roped * Q_PRESCALE).astype(BF)

    rope_row = pl.BlockSpec((tm, QK_ROPE), lambda h, i: (i, 0))
    (q,) = _mm("mla_qup", (N_HEADS, nt), [qn, w["w_qh"], cosf, sinf, perm],
               [pl.BlockSpec((tm, Q_RANK), lambda h, i: (i, 0)), pl.BlockSpec((None, Q_RANK, QK_DIM), lambda h, i: (h, 0, 0)),
                rope_row, rope_row, pl.BlockSpec((QK_ROPE, QK_ROPE), lambda h, i: (0, 0))],
               [_sds((N_HEADS, s, QK_DIM), BF)], [pl.BlockSpec((None, tm, QK_DIM), lambda h, i: (h, i, 0))], NN, q_epi)

    def kv_epi(part, extra, outs, pids):
        outs[0][:, 0:QK_NOPE] = part[:, 0:QK_NOPE].astype(BF)
        outs[0][:, QK_NOPE:QK_DIM] = extra[0][...]
        outs[1][...] = part[:, QK_NOPE:].astype(BF)

    k, v = _mm("mla_kvup", (N_HEADS, nt), [kvn, w["g512"], krr],
               [pl.BlockSpec((tm, KV_RANK), lambda h, i: (i, 0)),
                pl.BlockSpec((None, KV_RANK, 256), lambda h, i: (h // 2, 0, h % 2)), rope_row],
               [_sds((N_HEADS, s, QK_DIM), BF), _sds((N_HEADS, s, V_DIM), BF)],
               [pl.BlockSpec((None, tm, QK_DIM), lambda h, i: (h, i, 0)), pl.BlockSpec((None, tm, V_DIM), lambda h, i: (h, i, 0))],
               NN, kv_epi)

    t, nq = _attn_tiles(s)

    def attn_body(q_ref, k_ref, v_ref, z_ref, o_ref, g_ref, lse_ref):
        i = pl.program_id(1)
        qv = q_ref[...]

        def block(j, carry, masked):
            m, lsum, acc = carry
            start = pl.multiple_of(j * t, t)
            sc = _dot(qv, k_ref[pl.ds(start, t), :], NT)
            if masked:
                sc = jnp.where(_causal_keep(t, False), sc, NEG_BIG)
            mn = jnp.maximum(m, jnp.max(sc, axis=-1, keepdims=True))
            alpha = jnp.exp2(m - mn)
            p = jnp.exp2(sc - mn)
            lsum = alpha * lsum + jnp.sum(p, axis=-1, keepdims=True)
            acc = alpha * acc + _dot(p.astype(BF), v_ref[pl.ds(start, t), :], NN)
            return mn, lsum, acc

        init = (jnp.full((t, 1), NEG_BIG, F32), jnp.zeros((t, 1), F32), jnp.zeros((t, V_DIM), F32))
        carry = lax.fori_loop(0, i, lambda j, c: block(j, c, False), init)
        m, lsum, acc = block(i, carry, True)
        o = acc / lsum
        z = z_ref[...]
        o_ref[...] = o
        g_ref[...] = (o * (z * _sig(z))).astype(BF)
        lse_ref[...] = m + jnp.log(lsum) * LOG2_E

    head_col = pl.BlockSpec((t, V_DIM), lambda h, i: (i, h))
    o, gated, lse = pl.pallas_call(
        attn_body, name="mla_attn", grid=(N_HEADS, nq),
        in_specs=[pl.BlockSpec((None, t, QK_DIM), lambda h, i: (h, i, 0)), pl.BlockSpec((None, s, QK_DIM), lambda h, i: (h, 0, 0)),
                  pl.BlockSpec((None, s, V_DIM), lambda h, i: (h, 0, 0)), head_col],
        out_specs=[head_col, head_col, pl.BlockSpec((None, t, 1), lambda h, i: (h, i, 0))],
        out_shape=[_sds((s, D_INNER), F32), _sds((s, D_INNER), BF), _sds((N_HEADS, s, 1), F32)],
        compiler_params=_params())(q, k, v, z)
    y = _out_proj("mla_out", gated, w["g1024"], 0, x, tm)
    return y, dict(x=x, xn=xn, q_lat=q_lat, kv_lat=kv_lat, qn=qn, kvn=kvn, z=z, q=q, k=k, v=v, o=o, lse=lse, gated=gated)


def _mla_bwd(dy, dyb, w, sv, rope, tm, dep):
    s = dy.shape[0]
    nt = s // tm
    cosf, sinf, perm = rope
    t, nq = _attn_tiles(s)
    q, k, v, lse = sv["q"], sv["k"], sv["v"], sv["lse"]

    def gate_bwd(part, extra, outs, pids):
        z, o = extra[0][...], extra[1][...]
        sz, dsz = _silu_and_grad(z)
        do = part * sz
        outs[0][...] = do.astype(BF)
        outs[1][...] = (part * o * dsz).astype(BF)
        prod = do * o
        for hh in range(4):
            outs[2][hh] = jnp.sum(prod[:, hh * V_DIM:(hh + 1) * V_DIM], axis=-1, keepdims=True)

    tile = lambda j, i: (i, j)
    dob, dz, delta = _mm(
        "mlab_out", (4, nt), [dyb, w["g1024"], sv["z"], sv["o"]] + dep,
        [pl.BlockSpec((tm, D_MODEL), lambda j, i: (i, 0)), _w_out_nt_block(0),
         pl.BlockSpec((tm, 512), tile), pl.BlockSpec((tm, 512), tile)] + [_ANY] * len(dep),
        [_sds((s, D_INNER), BF), _sds((s, D_INNER), BF), _sds((N_HEADS, s, 1), F32)],
        [pl.BlockSpec((tm, 512), tile), pl.BlockSpec((tm, 512), tile), pl.BlockSpec((4, tm, 1), lambda j, i: (j, i, 0))],
        NT, gate_bwd)

    def attn_bwd_body(k_ref, v_ref, q_ref, do_ref, lse_ref, dl_ref, cos_ref, sin_ref, p_ref, dkv_ref, dkr_ref, dq_ref, dq_acc):
        j = pl.program_id(1)
        kb, vb = k_ref[...], v_ref[...]

        @pl.when(j == 0)
        def _():
            dq_acc[...] = jnp.zeros(dq_acc.shape, F32)

        def block(i, carry, masked):
            dk, dv = carry
            rows = pl.ds(pl.multiple_of(i * t, t), t)
            qb, dob_ = q_ref[rows, :], do_ref[rows, :]
            st = _dot(kb, qb, NT)
            if masked:
                st = jnp.where(_causal_keep(t, True), st, NEG_BIG)
            pt = jnp.exp2(st - lse_ref[i])
            dv = dv + _dot(pt.astype(BF), dob_, NN)
            dst = (pt * (_dot(vb, dob_, NT) - dl_ref[i])).astype(BF)
            dk = dk + _dot(dst, qb, NN)
            dq_acc[rows, :] += _dot(dst, kb, TN)
            return dk, dv

        carry = block(j, (jnp.zeros((t, QK_DIM), F32), jnp.zeros((t, V_DIM), F32)), True)
        dk, dv = lax.fori_loop(j + 1, nq, lambda i, c: block(i, c, False), carry)
        dk = dk * LN_2
        dkv_ref[:, 0:QK_NOPE] = dk[:, 0:QK_NOPE].astype(BF)
        dkv_ref[:, QK_NOPE:] = dv.astype(BF)
        dkr_ref[...] = dk[:, QK_NOPE:]

        @pl.when(j == nq - 1)
        def _():
            for c in range(nq):
                rows = slice(c * t, (c + 1) * t)
                dq = dq_acc[rows, :] * ATTN_SCALE
                dq_ref[rows, 0:QK_NOPE] = dq[:, 0:QK_NOPE].astype(BF)
                dq_ref[rows, QK_NOPE:] = _rope_bwd(dq[:, QK_NOPE:], cos_ref[rows, :], sin_ref[rows, :], p_ref[...]).astype(BF)

    row_stats = pl.BlockSpec((None, nq, 1, t), lambda h, j: (h, 0, 0, 0))
    seq_rope = pl.BlockSpec((s, QK_ROPE), lambda h, j: (0, 0))
    head_seq = pl.BlockSpec((None, s, QK_DIM), lambda h, j: (h, 0, 0))
    dkv, dkr_h, dq = pl.pallas_call(
        attn_bwd_body, name="mlab_attn", grid=(N_HEADS, nq),
        in_specs=[pl.BlockSpec((None, t, QK_DIM), lambda h, j: (h, j, 0)), pl.BlockSpec((None, t, V_DIM), lambda h, j: (h, j, 0)),
                  head_seq, pl.BlockSpec((s, V_DIM), lambda h, j: (0, h)), row_stats, row_stats, seq_rope, seq_rope,
                  pl.BlockSpec((QK_ROPE, QK_ROPE), lambda h, j: (0, 0))],
        out_specs=[pl.BlockSpec((t, 2 * V_DIM), lambda h, j: (j, h)), pl.BlockSpec((None, t, QK_ROPE), lambda h, j: (h, j, 0)), head_seq],
        out_shape=[_sds((s, N_HEADS * 2 * V_DIM), BF), _sds((N_HEADS, s, QK_ROPE), F32), _sds((N_HEADS, s, QK_DIM), BF)],
        scratch_shapes=[pltpu.VMEM((s, QK_DIM), F32)],
        compiler_params=_params())(k, v, q, dob, lse.reshape(N_HEADS, nq, 1, t), delta.reshape(N_HEADS, nq, 1, t), cosf, sinf, perm)

    def dkr_body(d_ref, cos_ref, sin_ref, p_ref, o_ref):
        tot = d_ref[0]
        for hh in range(1, N_HEADS):
            tot = tot + d_ref[hh]
        o_ref[...] = _rope_bwd(tot, cos_ref[...], sin_ref[...], p_ref[...]).astype(BF)

    r64 = pl.BlockSpec((tm, QK_ROPE), lambda i: (i, 0))
    dkr = pl.pallas_call(
        dkr_body, name="mlab_dkr", grid=(nt,),
        in_specs=[pl.BlockSpec((N_HEADS, tm, QK_ROPE), lambda i: (0, i, 0)), r64, r64, pl.BlockSpec((QK_ROPE, QK_ROPE), lambda i: (0, 0))],
        out_specs=r64, out_shape=_sds((s, QK_ROPE), BF), compiler_params=_params())(dkr_h, cosf, sinf, perm)

    def lat_epi(acc, extra, outs, pids):
        dx, dg = _rms_bwd(acc, extra[0][...], extra[1][...], None)
        outs[0][...] = dx.astype(BF)
        _accumulate(outs[1], dg, pids[0])

    def lat_bwd(name, a, a_spec, b, b_spec, n_k, lat, g, rank):
        row = lambda i, k: (i, 0)
        one = lambda i, k: (0, 0)
        return _mm(name, (nt, n_k), [a, b, lat, g.reshape(1, rank)],
                   [a_spec, b_spec, pl.BlockSpec((tm, rank), row), pl.BlockSpec((1, rank), one)],
                   [_sds((s, rank), BF), _sds((1, rank), F32)], [pl.BlockSpec((tm, rank), row), pl.BlockSpec((1, rank), one)],
                   NT, lat_epi, red=1, acc_shape=(tm, rank))

    d_ql, g_qnorm = lat_bwd("mlab_qup", dq, pl.BlockSpec((None, tm, QK_DIM), lambda i, h: (h, i, 0)),
                            w["w_qh"], pl.BlockSpec((None, Q_RANK, QK_DIM), lambda i, h: (h, 0, 0)), N_HEADS,
                            sv["q_lat"], w["q_norm"], Q_RANK)
    d_kvl, g_kvnorm = lat_bwd("mlab_kvup", dkv, pl.BlockSpec((tm, 512), lambda i, kk: (i, kk)),
                              w["g512"], pl.BlockSpec((None, KV_RANK, 512), lambda i, kk: (kk, 0, 0)), N_DEV,
                              sv["kv_lat"], w["kv_norm"], KV_RANK)

    def in_bwd(dql_ref, dkvl_ref, dkr_ref, dz_ref, wq_ref, wkv_ref, wkr_ref, wz_ref, x_ref, g_ref, dy_ref, dx_ref, dxb_ref, dg_ref):
        acc = (_dot(dql_ref[...], wq_ref[...], NT) + _dot(dkvl_ref[...], wkv_ref[...], NT)
               + _dot(dkr_ref[...], wkr_ref[...], NT) + _dot(dz_ref[...], wz_ref[...], NT))
        dx, dg = _rms_bwd(acc, x_ref[...], g_ref[...], dy_ref[...])
        dx_ref[...] = dx
        dxb_ref[...] = dx.astype(BF)
        _accumulate(dg_ref, dg, pl.program_id(0))

    def full(a):
        return pl.BlockSpec(a.shape, lambda i: (0,) * a.ndim)

    def rows(c):
        return pl.BlockSpec((tm, c), lambda i: (i, 0))

    gm = w["norm"].reshape(1, D_MODEL)
    dx, dxb, g_norm = pl.pallas_call(
        in_bwd, name="mlab_in", grid=(nt,),
        in_specs=[rows(Q_RANK), rows(KV_RANK), rows(QK_ROPE), rows(D_INNER), full(w["w_q"]), full(w["w_kv"]), full(w["w_kr"]),
                  full(w["w_z"]), rows(D_MODEL), full(gm), rows(D_MODEL)],
        out_specs=[rows(D_MODEL), rows(D_MODEL), full(gm)],
        out_shape=[_sds((s, D_MODEL), F32), _sds((s, D_MODEL), BF), _sds((1, D_MODEL), F32)],
        compiler_params=_params())(d_ql, d_kvl, dkr, dz, w["w_q"], w["w_kv"], w["w_kr"], w["w_z"], sv["x"], gm, dy)

    xn = sv["xn"]
    one = lambda j: (0, 0)
    g_q = _tn("mlab_gq", xn, d_ql, (D_MODEL, Q_RANK), (D_MODEL, Q_RANK), one, D_MODEL, Q_RANK, (1,), one, one)
    g_kv = _tn("mlab_gkv", xn, d_kvl, (D_MODEL, KV_RANK), (D_MODEL, KV_RANK), one, D_MODEL, KV_RANK, (1,), one, one)
    g_kr = _tn("mlab_gkr", xn, dkr, (D_MODEL, QK_ROPE), (D_MODEL, QK_ROPE), one, D_MODEL, QK_ROPE, (1,), one, one)
    g_z = _tn("mlab_gz", xn, dz, (D_MODEL, D_INNER), (D_MODEL, 512), lambda j: (0, j), D_MODEL, 512, (4,), one, lambda j: (0, j))
    g_in = jnp.concatenate([g_q, g_kv, g_kr, g_z], axis=1)
    g_qh = _tn("mlab_gqup", sv["qn"], dq, (N_HEADS, Q_RANK, QK_DIM), (None, Q_RANK, QK_DIM), lambda h: (h, 0, 0),
               Q_RANK, QK_DIM, (N_HEADS,), lambda h: (0, 0), lambda h: (h, 0, 0))
    g_kvup = _tn("mlab_gkvup", sv["kvn"], dkv, (N_DEV, KV_RANK, 512), (None, KV_RANK, 512), lambda j: (j, 0, 0),
                 KV_RANK, 512, (N_DEV,), lambda j: (0, 0), lambda j: (0, j))
    g_out = _w_out_grad("mlab_gout", sv["gated"], dyb)
    s384 = g_qh.reshape(N_DEV, 2, Q_RANK, QK_DIM).transpose(0, 2, 1, 3).reshape(N_DEV, Q_RANK, 2 * QK_DIM)
    s344 = g_in.reshape(D_MODEL, N_DEV, 344).transpose(1, 0, 2)
    return dx, dxb, dict(s344=s344, s384=s384, s512=g_kvup, s1024=g_out.reshape(N_DEV, 256, D_MODEL),
                         norm=g_norm[0], q_norm=g_qnorm[0], kv_norm=g_kvnorm[0])


def _loss_head(x, g, target, tm):
    s, d = x.shape

    def body(x_ref, g_ref, t_ref, dx_ref, dxb_ref, dg_ref, loss_ref):
        i = pl.program_id(0)
        xv, gv = x_ref[...], g_ref[...]
        r = lax.rsqrt(jnp.mean(xv * xv, axis=-1, keepdims=True) + NORM_EPS)
        err = (xv * r) * gv - t_ref[...]
        part = 0.5 * jnp.sum(jnp.mean(err * err, axis=-1, keepdims=True), axis=0, keepdims=True)
        dx, dg = _rms_bwd(err * (1.0 / d), xv, gv, None)
        dx_ref[...] = dx
        dxb_ref[...] = dx.astype(BF)
        _accumulate(dg_ref, dg, i)
        _accumulate(loss_ref, jnp.broadcast_to(part, loss_ref.shape), i)

    row = pl.BlockSpec((tm, d), lambda i: (i, 0))
    one = pl.BlockSpec((1, d), lambda i: (0, 0))
    return pl.pallas_call(
        body, name="loss_head", grid=(s // tm,), in_specs=[row, one, row],
        out_specs=[row, row, one, pl.BlockSpec((8, 128), lambda i: (0, 0))],
        out_shape=[_sds((s, d), F32), _sds((s, d), BF), _sds((1, d), F32), _sds((8, 128), F32)],
        compiler_params=_params())(x, g.reshape(1, d), target)


def _rope_tables(pos):
    inv_freq = ROPE_BASE ** (-jnp.arange(0, QK_ROPE, 2, dtype=F32) / QK_ROPE)
    ang = pos.astype(F32)[:, None] * inv_freq
    cos, sin = jnp.cos(ang), jnp.sin(ang)
    idx = jnp.arange(QK_ROPE)
    perm = (idx[:, None] == (idx[None, :] + QK_ROPE // 2) % QK_ROPE).astype(F32)
    return jnp.concatenate([cos, cos], axis=1), jnp.concatenate([-sin, sin], axis=1), perm


def _local_step(x, pos, target, final_norm, get_w, put_g):
    s = x.shape[0]
    tm = min(512, s)
    rope = _rope_tables(pos)
    w0 = get_w(0, [])
    x1, sv0 = _pool_fwd(x, 0, w0, tm)
    w1 = get_w(1, [x1])
    x2, sv1 = _conv_fwd(x1, w1, tm)
    w2 = get_w(2, [x2])
    x3, sv2 = _mla_fwd(x2, w2, rope, tm)
    w3 = get_w(3, [x3])
    x4, sv3 = _pool_fwd(x3, 1, w3, tm)
    d4, d4b, g_final, loss = _loss_head(x4, final_norm, target, tm)
    d3, d3b, gp1 = _pool_bwd(d4, d4b, 1, w3, sv3, tm, [])
    dep = put_g(3, gp1)
    d2, d2b, gm = _mla_bwd(d3, d3b, w2, sv2, rope, tm, dep)
    dep = put_g(2, gm)
    d1, d1b, gc = _conv_bwd(d2, d2b, w1, sv1, tm, dep)
    dep = put_g(1, gc)
    d0, _, gp0 = _pool_bwd(d1, d1b, 0, w0, sv0, tm, dep)
    put_g(0, dict(gp0, final_norm=g_final[0]))
    return loss[0, 0], d0


def _pack_groups(p):
    def pool(l):
        both = jnp.concatenate([p["pool_w_in"][l], p["pool_w_grp"][l].reshape(POOL_512_ROWS - POOL_GRP_ROW, POOL_GROUP)], axis=0)
        return [both.astype(BF), p["pool_w_out"][l].astype(BF)]

    conv = [jnp.concatenate([p["conv_w_in"][0], p["conv_w_out"][0]], axis=0).astype(BF)]
    mla = [p[k][0].astype(BF) for k in ("mla_w_in", "mla_w_q_up", "mla_w_kv_up", "mla_w_out")]
    return [pool(0) + [_pack_small(p, SMALL_ROWS_AG)], conv, mla, pool(1)]


_SMALL_SHARDED = ("pool_norm", "pool_scale", "mla_norm", "mla_q_norm", "mla_kv_norm", "conv_w")
_SMALL_REPLICATED = ("conv_norm", "final_norm")


def _pack_small(p, rows, with_replicated=False):
    parts = [p[k].reshape(-1) for k in _SMALL_SHARDED]
    if with_replicated:
        parts += [p[k].reshape(-1) for k in _SMALL_REPLICATED]
    flat = jnp.concatenate(parts)
    return jnp.pad(flat, (0, rows * 128 - flat.shape[0])).reshape(rows, 128)


_SMALL_SHARD_SHAPES = dict(pool_norm=(2, 128), pool_scale=(2, 256), mla_norm=(1, 128), mla_q_norm=(1, 48),
                           mla_kv_norm=(1, 32), conv_w=(1, 3, 256), conv_norm=(1, 1024), final_norm=(1024,))


def _unpack_small(buf, with_replicated=False):
    flat = buf.reshape(-1)
    out, off = {}, 0
    for k in _SMALL_SHARDED + (_SMALL_REPLICATED if with_replicated else ()):
        shp = _SMALL_SHARD_SHAPES[k]
        n = 1
        for d in shp:
            n *= d
        out[k] = flat[off:off + n].reshape(shp)
        off += n
    return out


def _small_views(gsmall):
    flat = gsmall.reshape(N_DEV, -1)

    def cols(off, rows, width):
        return flat[:, off:off + rows * width].reshape(N_DEV, rows, width).transpose(1, 0, 2).reshape(rows, N_DEV * width)

    return dict(pool_norm=cols(0, 2, 128), pool_scale=cols(256, 2, 256), mla_norm=cols(768, 1, 128)[0],
                q_norm=cols(896, 1, 48)[0], kv_norm=cols(944, 1, 32)[0], conv_w=cols(976, 3, 256))


def _layer_weights(layer, bufs, small, conv_norm):
    if layer in (0, 3):
        l = 0 if layer == 0 else 1
        return dict(g512=bufs[0], g1024=bufs[1], norm=small["pool_norm"][l], scale=small["pool_scale"][l])
    if layer == 1:
        return dict(g1024=bufs[0], norm=conv_norm.reshape(D_MODEL), conv_w=small["conv_w"])
    g344, g384, g512, g1024 = bufs
    w_in = g344.transpose(1, 0, 2).reshape(D_MODEL, N_DEV * 344)
    return dict(
        g512=g512, g1024=g1024,
        w_q=w_in[:, :Q_RANK], w_kv=w_in[:, Q_RANK:Q_RANK + KV_RANK],
        w_kr=w_in[:, Q_RANK + KV_RANK:Q_RANK + KV_RANK + QK_ROPE], w_z=w_in[:, Q_RANK + KV_RANK + QK_ROPE:],
        w_qh=g384.reshape(N_DEV, Q_RANK, 2, QK_DIM).transpose(0, 2, 1, 3).reshape(N_HEADS, Q_RANK, QK_DIM),
        norm=small["mla_norm"], q_norm=small["q_norm"], kv_norm=small["kv_norm"])


def _grad_group(layer, g):
    if layer in (0, 3):
        return [g["s512"], g["s1024"]]
    if layer == 1:
        return [g["s1024"]]
    return [g["s344"], g["s384"], g["s512"], g["s1024"]]


def _pack_small_grads(g):
    def split(a, rows, width):
        return a.reshape(rows, N_DEV, width).transpose(1, 0, 2).reshape(N_DEV, rows * width)

    rep = lambda a: jnp.broadcast_to(a.reshape(1, -1), (N_DEV, a.size))
    flat = jnp.concatenate([
        split(jnp.stack([g[0]["norm"], g[3]["norm"]]), 2, 128), split(jnp.stack([g[0]["scale"], g[3]["scale"]]), 2, 256),
        split(g[2]["norm"], 1, 128), split(g[2]["q_norm"], 1, 48), split(g[2]["kv_norm"], 1, 32), split(g[1]["conv_w"], 3, 256),
        rep(g[1]["norm"]), rep(g[0]["final_norm"])], axis=1)
    return jnp.pad(flat, ((0, 0), (0, SMALL_ROWS_RS * 128 - flat.shape[1]))).reshape(N_DEV, SMALL_ROWS_RS, 128)


def _peers(x, y, c):
    for k in range(1, N_DEV):
        px = 1 - x if k & 4 else x
        py = 1 - y if k & 2 else y
        pc = 1 - c if k & 1 else c
        yield k - 1, (px, py, pc), 4 * px + 2 * py + pc


def _remote_copies(srcs, lands, send_sems, recv_sems, gather):
    x, y, c = lax.axis_index("x"), lax.axis_index("y"), lax.axis_index("c")
    me = 4 * x + 2 * y + c
    copies = []
    for k, peer, pidx in _peers(x, y, c):
        for a, (src, land) in enumerate(zip(srcs, lands)):
            copies.append(pltpu.make_async_remote_copy(
                src_ref=src if gather else src.at[pidx], dst_ref=land.at[me],
                send_sem=send_sems.at[a * (N_DEV - 1) + k], recv_sem=recv_sems.at[a * (N_DEV - 1) + k],
                device_id=peer, device_id_type=pl.DeviceIdType.MESH))
    return copies


_HBM = pl.BlockSpec(memory_space=pltpu.HBM)
_SEM = pl.BlockSpec(memory_space=pltpu.SEMAPHORE)
_EFFECT = pltpu.SideEffectType.DATAFLOW_SIDE_EFFECTING


def _own_slabs(name, arrays, gather, dep):
    n, nd = len(arrays), len(dep)

    def body(*refs):
        me = 4 * lax.axis_index("x") + 2 * lax.axis_index("y") + lax.axis_index("c")
        sems = refs[2 * n + nd]
        copies = [pltpu.make_async_copy(refs[a] if gather else refs[a].at[me], refs[n + nd + a].at[me], sems.at[a])
                  for a in range(n)]
        for cp in copies:
            cp.start()
        for cp in copies:
            cp.wait()

    outs = [_sds(((N_DEV,) + a.shape) if gather else a.shape, a.dtype) for a in arrays]
    return pl.pallas_call(body, name=name, in_specs=[_HBM] * n + [_ANY] * nd, out_specs=[_HBM] * n, out_shape=outs,
                          scratch_shapes=[pltpu.SemaphoreType.DMA((n,))])(*arrays, *dep)


def _exchange_start(name, arrays, lands, gather):
    n = len(arrays)

    def body(*refs):
        srcs, lnds, send_sems, recv_sems, token = refs[:n], refs[n:2 * n], refs[2 * n], refs[2 * n + 1], refs[-1]
        for cp in _remote_copies(srcs, lnds, send_sems, recv_sems, gather):
            cp.start()
        token[...] = jnp.zeros(token.shape, F32)

    sems = pltpu.SemaphoreType.DMA((n * (N_DEV - 1),))
    thru = [pltpu.HBM(a.shape, a.dtype) for a in list(arrays) + list(lands)]
    res = pl.pallas_call(
        body, name=name, in_specs=[_HBM] * (2 * n),
        out_specs=[_SEM, _SEM] + [_HBM] * (2 * n) + [pl.BlockSpec(memory_space=pltpu.VMEM)],
        out_shape=[sems, sems] + thru + [_sds((8, 128), F32)],
        input_output_aliases={i: 2 + i for i in range(2 * n)},
        compiler_params=pltpu.CompilerParams(has_side_effects=_EFFECT),
    )(*[pltpu.with_memory_space_constraint(a, pltpu.HBM) for a in list(arrays) + list(lands)])
    return res[0], res[1], list(res[2:2 + n]), list(res[2 + n:2 + 2 * n]), res[-1]


def _exchange_wait(name, send_sems, recv_sems, arrays, lands, after, gather):
    n = len(arrays)
    n_after = len(after)

    def body(*refs):
        srcs, lnds = refs[:n], refs[n:2 * n]
        copies = _remote_copies(srcs, lnds, refs[2 * n], refs[2 * n + 1], gather)
        for cp in copies:
            cp.wait_send()
        for cp in copies:
            cp.wait_recv()

    thru = [pltpu.HBM(a.shape, a.dtype) for a in list(arrays) + list(lands)]
    res = pl.pallas_call(
        body, name=name, in_specs=[_HBM] * (2 * n) + [_SEM, _SEM] + [pl.BlockSpec(memory_space=pl.ANY)] * n_after,
        out_specs=[_HBM] * (2 * n), out_shape=thru, input_output_aliases={i: i for i in range(2 * n)},
        compiler_params=pltpu.CompilerParams(has_side_effects=_EFFECT),
    )(*arrays, *lands, send_sems, recv_sems, *after)
    return list(res[n:])


def _adamw_math(g, w, m, v):
    m = ADAM_B1 * m + (1.0 - ADAM_B1) * g
    v = ADAM_B2 * v + (1.0 - ADAM_B2) * (g * g)
    m_hat = m / (1.0 - ADAM_B1 ** ADAM_STEP)
    v_hat = v / (1.0 - ADAM_B2 ** ADAM_STEP)
    delta = -ADAM_LR * (m_hat / (jnp.sqrt(v_hat) + ADAM_EPS) + ADAM_WD * w)
    return delta, m, v


def _sum_adamw(name, recv, row_off, w, m, v, tr, layer=0):
    width = recv.shape[-1]
    w2, m2, v2 = (a.reshape(a.shape[0], -1, width) for a in (w, m, v))
    rows = w2.shape[1]
    base = row_off // tr

    def body(r_ref, w_ref, m_ref, v_ref, g_ref, d_ref, mo_ref, vo_ref):
        g = r_ref[0].astype(F32)
        for src in range(1, N_DEV):
            g = g + r_ref[src].astype(F32)
        delta, mn, vn = _adamw_math(g, w_ref[...], m_ref[...], v_ref[...])
        g_ref[...] = g
        d_ref[...] = delta
        mo_ref[...] = mn
        vo_ref[...] = vn

    blk = pl.BlockSpec((tr, width), lambda i: (i, 0))
    wblk = pl.BlockSpec((None, tr, width), lambda i: (layer, i, 0))
    return pl.pallas_call(
        body, name=name, grid=(rows // tr,),
        in_specs=[pl.BlockSpec((N_DEV, tr, width), lambda i: (0, base + i, 0)), wblk, wblk, wblk],
        out_specs=[blk] * 4, out_shape=[_sds((rows, width), F32)] * 4, compiler_params=_params())(recv, w2, m2, v2)


_WEIGHTS = ("pool_norm", "pool_w_in", "pool_w_grp", "pool_scale", "pool_w_out", "conv_norm", "conv_w_in", "conv_w", "conv_w_out",
            "mla_norm", "mla_w_in", "mla_q_norm", "mla_w_q_up", "mla_kv_norm", "mla_w_kv_up", "mla_w_out", "final_norm")


def _step(x, positions, loss_target, p, m, v):
    gathers, tokens, dep = [], [], []
    for layer, arrays in enumerate(_pack_groups(p)):
        lands = _own_slabs(f"gather{layer}_own", arrays, True, dep)
        ssem, rsem, arrays, lands, token = _exchange_start(f"gather{layer}_start", arrays, lands, True)
        gathers.append((ssem, rsem, arrays, lands))
        tokens.append(token)
        dep = [token]
    state = {}

    def get_w(layer, after):
        bufs = _exchange_wait(f"gather{layer}_wait", *gathers[layer], after if layer else tokens, True)
        if layer == 0:
            state["small"] = _small_views(bufs[2])
        return _layer_weights(layer, bufs, state["small"], p["conv_norm"])

    scatters, small_grads = {}, {}

    def put_g(layer, g):
        arrays = _grad_group(layer, g)
        small_grads[layer] = g
        if layer == 0:
            arrays = arrays + [_pack_small_grads(small_grads)]
        lands = _own_slabs(f"scatter{layer}_own", arrays, False, [])
        ssem, rsem, arrays, lands, token = _exchange_start(f"scatter{layer}_start", arrays, lands, False)
        scatters[layer] = (ssem, rsem, arrays, lands)
        tokens.append(token)
        return [token]

    loss, grad_x = _local_step(x[0], positions[0], loss_target[0], p["final_norm"], get_w, put_g)

    def adam(name, recv, row_off, key, tr, layer=0):
        return _sum_adamw("adam_" + name, recv, row_off, p[key], m[key], v[key], tr, layer)

    res, after = {}, [tokens[-1]]
    for layer in (3, 2, 1, 0):
        recv = _exchange_wait(f"scatter{layer}_wait", *scatters[layer], after, False)
        if layer in (0, 3):
            l = 0 if layer == 0 else 1
            res["pool_w_in", l] = adam(f"pool_w_in{l}", recv[0], 0, "pool_w_in", 256, l)
            res["pool_w_grp", l] = adam(f"pool_w_grp{l}", recv[0], POOL_GRP_ROW, "pool_w_grp", 256, l)
            res["pool_w_out", l] = adam(f"pool_w_out{l}", recv[1], 0, "pool_w_out", 256, l)
            after = [res["pool_w_out", l][1]]
        elif layer == 1:
            res["conv_w_in", 0] = adam("conv_w_in", recv[0], 0, "conv_w_in", 256)
            res["conv_w_out", 0] = adam("conv_w_out", recv[0], CONV_OUT_ROW, "conv_w_out", 256)
            after = [res["conv_w_out", 0][1]]
        else:
            res["mla_w_in", 0] = adam("mla_w_in", recv[0], 0, "mla_w_in", 256)
            res["mla_w_q_up", 0] = adam("mla_w_q_up", recv[1], 0, "mla_w_q_up", 384)
            res["mla_w_kv_up", 0] = adam("mla_w_kv_up", recv[2], 0, "mla_w_kv_up", 256)
            res["mla_w_out", 0] = adam("mla_w_out", recv[3], 0, "mla_w_out", 256)
            after = [res["mla_w_out", 0][1]]
    small = _sum_adamw("adam_small", recv[2], 0, _pack_small(p, SMALL_ROWS_RS, True)[None], _pack_small(m, SMALL_ROWS_RS, True)[None],
                       _pack_small(v, SMALL_ROWS_RS, True)[None], SMALL_ROWS_RS)
    small = [_unpack_small(a, True) for a in small]
    final = {k: tuple(part[k] for part in small) for k in _SMALL_SHARDED + _SMALL_REPLICATED}
    for k in _WEIGHTS:
        if k not in final:
            layers = [res[k, l] for l in range(p[k].shape[0])]
            final[k] = tuple(jnp.stack([lay[part] for lay in layers]).reshape(p[k].shape) for part in range(4))
    res = final

    loss = lax.psum(loss, ("x", "y", "c"))
    out = [loss, grad_x[None]]
    for part in range(4):
        out += [res[k][part] for k in _WEIGHTS]
    return tuple(out)


def kernel(x, positions, pool_norm, pool_w_in, pool_w_grp, pool_scale, pool_w_out, conv_norm, conv_w_in, conv_w, conv_w_out, mla_norm, mla_w_in, mla_q_norm, mla_w_q_up, mla_kv_norm, mla_w_kv_up, mla_w_out, final_norm, loss_target, m_pool_norm, m_pool_w_in, m_pool_w_grp, m_pool_scale, m_pool_w_out, m_conv_norm, m_conv_w_in, m_conv_w, m_conv_w_out, m_mla_norm, m_mla_w_in, m_mla_q_norm, m_mla_w_q_up, m_mla_kv_norm, m_mla_w_kv_up, m_mla_w_out, m_final_norm, v_pool_norm, v_pool_w_in, v_pool_w_grp, v_pool_scale, v_pool_w_out, v_conv_norm, v_conv_w_in, v_conv_w, v_conv_w_out, v_mla_norm, v_mla_w_in, v_mla_q_norm, v_mla_w_q_up, v_mla_kv_norm, v_mla_w_kv_up, v_mla_w_out, v_final_norm):
    p = dict(pool_norm=pool_norm, pool_w_in=pool_w_in, pool_w_grp=pool_w_grp, pool_scale=pool_scale, pool_w_out=pool_w_out,
             conv_norm=conv_norm, conv_w_in=conv_w_in, conv_w=conv_w, conv_w_out=conv_w_out, mla_norm=mla_norm, mla_w_in=mla_w_in,
             mla_q_norm=mla_q_norm, mla_w_q_up=mla_w_q_up, mla_kv_norm=mla_kv_norm, mla_w_kv_up=mla_w_kv_up, mla_w_out=mla_w_out,
             final_norm=final_norm)
    m = dict(pool_norm=m_pool_norm, pool_w_in=m_pool_w_in, pool_w_grp=m_pool_w_grp, pool_scale=m_pool_scale, pool_w_out=m_pool_w_out,
             conv_norm=m_conv_norm, conv_w_in=m_conv_w_in, conv_w=m_conv_w, conv_w_out=m_conv_w_out, mla_norm=m_mla_norm,
             mla_w_in=m_mla_w_in, mla_q_norm=m_mla_q_norm, mla_w_q_up=m_mla_w_q_up, mla_kv_norm=m_mla_kv_norm,
             mla_w_kv_up=m_mla_w_kv_up, mla_w_out=m_mla_w_out, final_norm=m_final_norm)
    v = dict(pool_norm=v_pool_norm, pool_w_in=v_pool_w_in, pool_w_grp=v_pool_w_grp, pool_scale=v_pool_scale, pool_w_out=v_pool_w_out,
             conv_norm=v_conv_norm, conv_w_in=v_conv_w_in, conv_w=v_conv_w, conv_w_out=v_conv_w_out, mla_norm=v_mla_norm,
             mla_w_in=v_mla_w_in, mla_q_norm=v_mla_q_norm, mla_w_q_up=v_mla_w_q_up, mla_kv_norm=v_mla_kv_norm,
             mla_w_kv_up=v_mla_w_kv_up, mla_w_out=v_mla_w_out, final_norm=v_final_norm)
    return _step(x, positions, loss_target, p, m, v)
```

```python
import functools

import jax
import jax.numpy as jnp
from jax import lax
from jax.experimental import pallas as pl
from jax.experimental.pallas import tpu as pltpu

BF = jnp.bfloat16
F32 = jnp.float32

N_DEV = 8
D_MODEL = 1024
D_INNER = 2048
POOL_WINDOWS = (2, 4, 8, 16)
POOL_GROUP = 512
N_HEADS = 16
QK_NOPE = 128
QK_ROPE = 64
QK_DIM = QK_NOPE + QK_ROPE
V_DIM = 128
Q_RANK = 384
KV_RANK = 256
ATTN_SCALE = QK_DIM ** -0.5
LOG2_E = 1.4426950408889634
LN_2 = 0.6931471805599453
Q_PRESCALE = ATTN_SCALE * LOG2_E
ATTN_TILE = 512
ROPE_BASE = 10000.0
NORM_EPS = 1e-6
NEG_BIG = -1e30

ADAM_LR = 0.001
ADAM_B1 = 0.9
ADAM_B2 = 0.999
ADAM_EPS = 1e-08
ADAM_WD = 0.01
ADAM_STEP = 10

VMEM_LIMIT_BYTES = 52 * 1024 * 1024
POOL_HALO = 32
CONV_HALO = 8

NN = (((1,), (0,)), ((), ()))
NT = (((1,), (1,)), ((), ()))
TN = (((0,), (0,)), ((), ()))

POOL_GRP_ROW, POOL_512_ROWS = 1024, 1280
CONV_OUT_ROW, CONV_1024_ROWS = 1024, 1280
SMALL_ROWS_AG = 16
SMALL_ROWS_RS = 32


def _sds(shape, dtype):
    return jax.ShapeDtypeStruct(tuple(shape), dtype)


def _params():
    return pltpu.CompilerParams(vmem_limit_bytes=VMEM_LIMIT_BYTES)


_ANY = pl.BlockSpec(memory_space=pl.ANY)


def _dot(a, b, dims):
    return lax.dot_general(a, b, dims, preferred_element_type=F32)


def _sig(z):
    return 1.0 / (1.0 + jnp.exp(-z))


def _silu_and_grad(z):
    sig = _sig(z)
    return z * sig, sig * (1.0 + z * (1.0 - sig))


def _rope_swap(x, p):
    return jnp.dot(x, p, precision=lax.Precision.HIGHEST, preferred_element_type=F32)


def _rope_fwd(x, cosf, sinf, p):
    return x * cosf + _rope_swap(x, p) * sinf


def _rope_bwd(dy, cosf, sinf, p):
    return dy * cosf + _rope_swap(dy * sinf, p)


def _rms_bwd(dxn, x, g, res):
    r = lax.rsqrt(jnp.mean(x * x, axis=-1, keepdims=True) + NORM_EPS)
    v = dxn * g
    dx = r * v - x * ((r * r * r) * jnp.mean(v * x, axis=-1, keepdims=True))
    if res is not None:
        dx = dx + res
    dg = jnp.sum(dxn * (x * r), axis=0, keepdims=True)
    return dx, dg


def _accumulate(ref, val, step):
    @pl.when(step == 0)
    def _():
        ref[...] = val

    @pl.when(step > 0)
    def _():
        ref[...] += val


def _mm(name, grid, ins, in_specs, outs, out_specs, dims, epi, red=None, acc_shape=None):
    n_in, n_out = len(ins), len(outs)
    n_red = None if red is None else grid[red]

    def body(*refs):
        in_refs, out_refs = refs[:n_in], refs[n_in:n_in + n_out]
        pids = tuple(pl.program_id(ax) for ax in range(len(grid)))
        a, b = in_refs[0][...], in_refs[1][...]
        if a.ndim == 3:
            a = a.reshape(-1, a.shape[-1])
        if b.ndim == 3:
            b = b.reshape(-1, b.shape[-1])
        part = _dot(a.astype(BF), b.astype(BF), dims)
        if red is None:
            epi(part, in_refs[2:], out_refs, pids)
        else:
            acc = refs[n_in + n_out]
            k = pids[red]
            _accumulate(acc, part, k)

            @pl.when(k == n_red - 1)
            def _():
                epi(acc[...], in_refs[2:], out_refs, pids)

    scratch = [] if red is None else [pltpu.VMEM(acc_shape, F32)]
    return pl.pallas_call(body, name=name, grid=grid, in_specs=in_specs, out_specs=out_specs, out_shape=outs,
                          scratch_shapes=scratch, compiler_params=_params())(*ins)


def _store(part, extra, outs, pids):
    outs[0][...] = part.astype(outs[0].dtype)


def _rms_fwd(name, x, g, tm):
    s, d = x.shape

    def body(x_ref, g_ref, o_ref):
        xv = x_ref[...]
        r = lax.rsqrt(jnp.mean(xv * xv, axis=-1, keepdims=True) + NORM_EPS)
        o_ref[...] = ((xv * r) * g_ref[...]).astype(BF)

    return pl.pallas_call(body, name=name, grid=(s // tm,),
                          in_specs=[pl.BlockSpec((tm, d), lambda i: (i, 0)), pl.BlockSpec((1, d), lambda i: (0, 0))],
                          out_specs=pl.BlockSpec((tm, d), lambda i: (i, 0)), out_shape=_sds((s, d), BF),
                          compiler_params=_params())(x, g.reshape(1, d))


def _tn(name, a, b, out_shape, out_block, out_index, a_cols, b_cols, grid, a_index, b_index):
    s = a.shape[-2]
    a_block = (s, a_cols) if a.ndim == 2 else (None, s, a_cols)
    b_block = (s, b_cols) if b.ndim == 2 else (None, s, b_cols)

    def epi(part, extra, outs, pids):
        outs[0][...] = part.astype(BF).reshape(outs[0].shape)

    return _mm(name, grid, [a, b], [pl.BlockSpec(a_block, a_index), pl.BlockSpec(b_block, b_index)],
               [_sds(out_shape, BF)], [pl.BlockSpec(out_block, out_index)], TN, epi)[0]


def _pool_window_fwd(name, h, tm):
    s = h.shape[0]
    hb = POOL_HALO

    def body(u_ref, halo_ref, o_ref, e_ref, a_ref, b_ref):
        i = pl.program_id(0)
        row = lax.broadcasted_iota(jnp.int32, (tm, 1), 0) + i * tm
        for g, w in enumerate(POOL_WINDOWS):
            cs = slice(g * POOL_GROUP, (g + 1) * POOL_GROUP)
            e_ref[0:hb, :] = jnp.where(i > 0, halo_ref[:, cs], 0.0)
            e_ref[hb:, :] = u_ref[:, cs]
            src, bufs = e_ref, (a_ref, b_ref)
            for lv in range(1, w.bit_length()):
                dst, st, sh = bufs[(lv - 1) % 2], 8 * lv, 2 ** (lv - 1)
                n = hb + tm - st
                dst[st:, :] = src[st:, :] + src[pl.ds(st - sh, n), :]
                src = dst
            cnt = jnp.minimum(row + 1, w).astype(F32)
            o_ref[:, cs] = (src[hb:, :] / cnt - u_ref[:, cs]).astype(BF)

    per = tm // hb
    return pl.pallas_call(
        body, name=name, grid=(s // tm,),
        in_specs=[pl.BlockSpec((tm, D_INNER), lambda i: (i, 0)),
                  pl.BlockSpec((hb, D_INNER), lambda i: (jnp.maximum(i * per - 1, 0), 0))],
        out_specs=pl.BlockSpec((tm, D_INNER), lambda i: (i, 0)), out_shape=_sds((s, D_INNER), BF),
        scratch_shapes=[pltpu.VMEM((hb + tm, POOL_GROUP), F32)] * 3, compiler_params=_params())(h, h)


def _pool_window_bwd(name, dp, tm):
    s = dp.shape[0]
    nt = s // tm
    hb = POOL_HALO

    def body(d_ref, halo_ref, o_ref, e_ref, a_ref, b_ref):
        i = pl.program_id(0)
        row = lax.broadcasted_iota(jnp.int32, (tm, 1), 0) + i * tm
        hrow = lax.broadcasted_iota(jnp.int32, (hb, 1), 0) + (i + 1) * tm
        for g, w in enumerate(POOL_WINDOWS):
            cs = slice(g * POOL_GROUP, (g + 1) * POOL_GROUP)
            e_ref[0:tm, :] = d_ref[:, cs] / jnp.minimum(row + 1, w).astype(F32)
            e_ref[tm:, :] = jnp.where(i < nt - 1, halo_ref[:, cs] / jnp.minimum(hrow + 1, w).astype(F32), 0.0)
            src, bufs = e_ref, (a_ref, b_ref)
            for lv in range(1, w.bit_length()):
                dst, sh = bufs[(lv - 1) % 2], 2 ** (lv - 1)
                n = tm + hb - 8 * lv
                dst[0:n, :] = src[0:n, :] + src[pl.ds(sh, n), :]
                src = dst
            o_ref[:, cs] = (src[0:tm, :] - d_ref[:, cs]).astype(BF)

    per = tm // hb
    last = s // hb - 1
    return pl.pallas_call(
        body, name=name, grid=(nt,),
        in_specs=[pl.BlockSpec((tm, D_INNER), lambda i: (i, 0)),
                  pl.BlockSpec((hb, D_INNER), lambda i: (jnp.minimum((i + 1) * per, last), 0))],
        out_specs=pl.BlockSpec((tm, D_INNER), lambda i: (i, 0)), out_shape=_sds((s, D_INNER), BF),
        scratch_shapes=[pltpu.VMEM((hb + tm, POOL_GROUP), F32)] * 3, compiler_params=_params())(dp, dp)


def _grp_block():
    base = POOL_GRP_ROW // 64
    return pl.BlockSpec((N_DEV, 64, POOL_GROUP), lambda i, g: (0, base + g, 0))


def _pool_fwd(x, l, w, tm):
    s = x.shape[0]
    nt = s // tm
    n = f"pool{l}"
    xn = _rms_fwd(n + "_rms", x, w["norm"], tm)
    (h,) = _mm(n + "_in", (nt, 8), [xn, w["g512"]],
               [pl.BlockSpec((tm, D_MODEL), lambda i, j: (i, 0)), pl.BlockSpec((None, D_MODEL, 512), lambda i, j: (j, 0, 0))],
               [_sds((s, 2 * D_INNER), F32)], [pl.BlockSpec((tm, 512), lambda i, j: (i, j))], NN, _store)
    pooled = _pool_window_fwd(n + "_win", h, tm)

    def gate(part, extra, outs, pids):
        z = extra[0][...]
        outs[0][...] = ((part * extra[1][...]) * (z * _sig(z))).astype(BF)

    (gated,) = _mm(n + "_grp", (nt, 4), [pooled, w["g512"], h, w["scale"].reshape(1, D_INNER)],
                   [pl.BlockSpec((tm, 512), lambda i, g: (i, g)), _grp_block(),
                    pl.BlockSpec((tm, 512), lambda i, g: (i, 4 + g)), pl.BlockSpec((1, 512), lambda i, g: (0, g))],
                   [_sds((s, D_INNER), BF)], [pl.BlockSpec((tm, 512), lambda i, g: (i, g))], NN, gate)
    y = _out_proj(n + "_out", gated, w["g1024"], 0, x, tm)
    return y, dict(x=x, xn=xn, h=h, pooled=pooled, gated=gated)


def _out_proj(name, gated, g1024, row_block, x, tm):
    s = x.shape[0]

    def epi(part, extra, outs, pids):
        outs[0][...] = part + extra[0][...]

    return _mm(name, (s // tm, 2), [gated, g1024, x],
               [pl.BlockSpec((tm, D_INNER), lambda i, j: (i, 0)), pl.BlockSpec((N_DEV, 256, 512), lambda i, j: (0, row_block, j)),
                pl.BlockSpec((tm, 512), lambda i, j: (i, j))],
               [_sds((s, D_MODEL), F32)], [pl.BlockSpec((tm, 512), lambda i, j: (i, j))], NN, epi)[0]


def _w_out_nt_block(row_block):
    return pl.BlockSpec((2, 256, D_MODEL), lambda j, i: (j, row_block, 0))


def _in_proj_bwd(name, dh, wbuf, w_index, n_k, x, g, dy, tm, dep=()):
    s = x.shape[0]

    def epi(acc, extra, outs, pids):
        dx, dg = _rms_bwd(acc, extra[0][...], extra[1][...], extra[2][...])
        outs[0][...] = dx
        outs[1][...] = dx.astype(BF)
        _accumulate(outs[2], dg, pids[0])

    row = lambda i, k: (i, 0)
    return _mm(name, (s // tm, n_k), [dh, wbuf, x, g.reshape(1, D_MODEL), dy] + list(dep),
               [pl.BlockSpec((tm, 512), lambda i, k: (i, k)), pl.BlockSpec((None, D_MODEL, 512), w_index),
                pl.BlockSpec((tm, D_MODEL), row), pl.BlockSpec((1, D_MODEL), lambda i, k: (0, 0)), pl.BlockSpec((tm, D_MODEL), row)]
               + [_ANY] * len(dep),
               [_sds((s, D_MODEL), F32), _sds((s, D_MODEL), BF), _sds((1, D_MODEL), F32)],
               [pl.BlockSpec((tm, D_MODEL), row), pl.BlockSpec((tm, D_MODEL), row), pl.BlockSpec((1, D_MODEL), lambda i, k: (0, 0))],
               NT, epi, red=1, acc_shape=(tm, D_MODEL))


def _w_out_grad(name, gated, dyb):
    s = gated.shape[0]
    return _tn(name, gated, dyb, (D_INNER, D_MODEL), (512, D_MODEL), lambda i: (i, 0), 512, D_MODEL, (4,),
               lambda i: (0, i), lambda i: (0, 0))


def _pool_bwd(dy, dyb, l, w, sv, tm, dep, early=None):
    s = dy.shape[0]
    nt = s // tm
    n = f"pool{l}b"
    h, pooled = sv["h"], sv["pooled"]
    scale = w["scale"].reshape(1, D_INNER)
    (mp,) = _mm(n + "_grp", (nt, 4), [pooled, w["g512"]] + dep,
                [pl.BlockSpec((tm, 512), lambda i, g: (i, g)), _grp_block()] + [_ANY] * len(dep),
                [_sds((s, D_INNER), F32)], [pl.BlockSpec((tm, 512), lambda i, g: (i, g))], NN, _store)

    def gate_bwd(part, extra, outs, pids):
        z, mpv, sc = extra[0][...], extra[1][...], extra[2][...]
        sz, dsz = _silu_and_grad(z)
        dm = part * sz
        outs[0][...] = (dm * sc).astype(BF)
        outs[1][...] = (part * (mpv * sc) * dsz).astype(BF)
        _accumulate(outs[2], jnp.sum(dm * mpv, axis=0, keepdims=True), pids[1])

    tile = lambda j, i: (i, j)
    dmp, dz, dscale = _mm(
        n + "_out", (4, nt), [dyb, w["g1024"], h, mp, scale],
        [pl.BlockSpec((tm, D_MODEL), lambda j, i: (i, 0)), _w_out_nt_block(0),
         pl.BlockSpec((tm, 512), lambda j, i: (i, 4 + j)), pl.BlockSpec((tm, 512), tile), pl.BlockSpec((1, 512), lambda j, i: (0, j))],
        [_sds((s, D_INNER), BF), _sds((s, D_INNER), BF), _sds((1, D_INNER), F32)],
        [pl.BlockSpec((tm, 512), tile), pl.BlockSpec((tm, 512), tile), pl.BlockSpec((1, 512), lambda j, i: (0, j))], NT, gate_bwd)
    (dpool,) = _mm(n + "_grpT", (nt, 4), [dmp, w["g512"]],
                   [pl.BlockSpec((tm, 512), lambda i, g: (i, g)), _grp_block()],
                   [_sds((s, D_INNER), F32)], [pl.BlockSpec((tm, 512), lambda i, g: (i, g))], NT, _store)
    du = _pool_window_bwd(n + "_win", dpool, tm)
    dh = jnp.concatenate([du, dz], axis=1)
    g_in = _tn(n + "_gin", sv["xn"], dh, (N_DEV, D_MODEL, 512), (None, D_MODEL, 512), lambda j: (j, 0, 0),
               D_MODEL, 512, (8,), lambda j: (0, 0), lambda j: (0, j))
    g_out = _w_out_grad(n + "_gout", sv["gated"], dyb)
    g_grp = _tn(n + "_ggrp", pooled, dmp, (N_DEV, 256, 512), (N_DEV, 64, 512), lambda g: (0, g, 0),
                512, 512, (4,), lambda g: (0, g), lambda g: (0, g))
    grads = dict(s512=jnp.concatenate([g_in, g_grp], axis=1), s1024=g_out.reshape(N_DEV, 256, D_MODEL))
    dep = early(grads) if early is not None else ()
    dx, dxb, dnorm = _in_proj_bwd(n + "_in", dh, w["g512"], lambda i, k: (k, 0, 0), 8, sv["x"], w["norm"], dy, tm, dep)
    return dx, dxb, dict(grads, norm=dnorm[0], scale=dscale[0])


def _conv_in_index(i, j):
    return (j // 2, 0, j % 2)


def _conv_fwd(x, w, tm):
    s = x.shape[0]
    nt = s // tm
    xn = _rms_fwd("conv_rms", x, w["norm"], tm)
    (h,) = _mm("conv_in", (nt, 16), [xn, w["g1024"]],
               [pl.BlockSpec((tm, D_MODEL), lambda i, j: (i, 0)), pl.BlockSpec((None, D_MODEL, 512), _conv_in_index)],
               [_sds((s, 4 * D_INNER), F32)], [pl.BlockSpec((tm, 512), lambda i, j: (i, j))], NN, _store)
    per = tm // CONV_HALO

    def body(b_ref, c_ref, h_ref, z_ref, cp_ref, hp_ref, w_ref, o_ref, e_ref):
        i = pl.program_id(0)
        ch = c_ref[...] * h_ref[...]
        e_ref[0:CONV_HALO, :] = jnp.where(i > 0, cp_ref[...] * hp_ref[...], 0.0)
        e_ref[CONV_HALO:, :] = ch
        co = (w_ref[2:3, :] * ch + w_ref[1:2, :] * e_ref[pl.ds(CONV_HALO - 1, tm), :]
              + w_ref[0:1, :] * e_ref[pl.ds(CONV_HALO - 2, tm), :])
        z = z_ref[...]
        o_ref[...] = ((b_ref[...] * co) * (z * _sig(z))).astype(BF)

    def col(q):
        return pl.BlockSpec((tm, 512), lambda i, j: (i, 4 * q + j))

    def prev(q):
        return pl.BlockSpec((CONV_HALO, 512), lambda i, j: (jnp.maximum(i * per - 1, 0), 4 * q + j))

    gated = pl.pallas_call(
        body, name="conv_mix", grid=(nt, 4),
        in_specs=[col(0), col(1), col(2), col(3), prev(1), prev(2), pl.BlockSpec((3, 512), lambda i, j: (0, j))],
        out_specs=pl.BlockSpec((tm, 512), lambda i, j: (i, j)), out_shape=_sds((s, D_INNER), BF),
        scratch_shapes=[pltpu.VMEM((CONV_HALO + tm, 512), F32)], compiler_params=_params())(h, h, h, h, h, h, w["conv_w"])
    y = _out_proj("conv_out", gated, w["g1024"], CONV_OUT_ROW // 256, x, tm)
    return y, dict(x=x, xn=xn, h=h, gated=gated)


def _conv_bwd(dy, dyb, w, sv, tm, dep):
    s = dy.shape[0]
    nt = s // tm
    h = sv["h"]
    (dg,) = _mm("convb_out", (4, nt), [dyb, w["g1024"]] + dep,
                [pl.BlockSpec((tm, D_MODEL), lambda j, i: (i, 0)), _w_out_nt_block(CONV_OUT_ROW // 256)] + [_ANY] * len(dep),
                [_sds((s, D_INNER), F32)], [pl.BlockSpec((tm, 512), lambda j, i: (i, j))], NT, _store)
    per = tm // CONV_HALO
    last = s // CONV_HALO - 1

    def body(dg_ref, b_ref, c_ref, h_ref, z_ref, cp_ref, hp_ref, dgn_ref, bn_ref, zn_ref, w_ref,
             db_ref, dc_ref, dh_ref, dz_ref, dw_ref, e_ref, f_ref):
        i = pl.program_id(1)
        w0, w1, w2 = w_ref[0:1, :], w_ref[1:2, :], w_ref[2:3, :]
        c, hh, b = c_ref[...], h_ref[...], b_ref[...]
        ch = c * hh
        e_ref[0:CONV_HALO, :] = jnp.where(i > 0, cp_ref[...] * hp_ref[...], 0.0)
        e_ref[CONV_HALO:, :] = ch
        ch1 = e_ref[pl.ds(CONV_HALO - 1, tm), :]
        ch2 = e_ref[pl.ds(CONV_HALO - 2, tm), :]
        co = w2 * ch + w1 * ch1 + w0 * ch2
        sz, dsz = _silu_and_grad(z_ref[...])
        dgv = dg_ref[...]
        dyv = dgv * sz
        dz_ref[...] = (dgv * (b * co) * dsz).astype(BF)
        db_ref[...] = (dyv * co).astype(BF)
        dco = dyv * b
        zn = zn_ref[...]
        f_ref[0:tm, :] = dco
        f_ref[tm:, :] = jnp.where(i < nt - 1, dgn_ref[...] * (zn * _sig(zn)) * bn_ref[...], 0.0)
        dch = w2 * dco + w1 * f_ref[pl.ds(1, tm), :] + w0 * f_ref[pl.ds(2, tm), :]
        dc_ref[...] = (dch * hh).astype(BF)
        dh_ref[...] = (dch * c).astype(BF)
        for tap, shifted in enumerate((ch2, ch1, ch)):
            _accumulate(dw_ref.at[tap:tap + 1, :], jnp.sum(dco * shifted, axis=0, keepdims=True), i)

    def col(q):
        return pl.BlockSpec((tm, 512), lambda j, i: (i, 4 * q + j))

    def prev(q):
        return pl.BlockSpec((CONV_HALO, 512), lambda j, i: (jnp.maximum(i * per - 1, 0), 4 * q + j))

    def nxt(q):
        return pl.BlockSpec((CONV_HALO, 512), lambda j, i: (jnp.minimum((i + 1) * per, last), 4 * q + j))

    tile = pl.BlockSpec((tm, 512), lambda j, i: (i, j))
    wspec = pl.BlockSpec((3, 512), lambda j, i: (0, j))
    db, dc, dhh, dz, dw = pl.pallas_call(
        body, name="convb_mix", grid=(4, nt),
        in_specs=[tile, col(0), col(1), col(2), col(3), prev(1), prev(2), nxt(0), nxt(0), nxt(3), wspec],
        out_specs=[tile, tile, tile, tile, wspec],
        out_shape=[_sds((s, D_INNER), BF)] * 4 + [_sds((3, D_INNER), F32)],
        scratch_shapes=[pltpu.VMEM((CONV_HALO + tm, 512), F32)] * 2, compiler_params=_params(),
    )(dg, h, h, h, h, h, h, dg, h, h, w["conv_w"])
    dh = jnp.concatenate([db, dc, dhh, dz], axis=1)
    dx, dxb, dnorm = _in_proj_bwd("convb_in", dh, w["g1024"], lambda i, k: (k // 2, 0, k % 2), 16, sv["x"], w["norm"], dy, tm)
    g_in = _tn("convb_gin", sv["xn"], dh, (N_DEV, D_MODEL, D_MODEL), (None, D_MODEL, 512), lambda j: (j // 2, 0, j % 2),
               D_MODEL, 512, (16,), lambda j: (0, 0), lambda j: (0, j))
    g_out = _w_out_grad("convb_gout", sv["gated"], dyb)
    s1024 = jnp.concatenate([g_in, g_out.reshape(N_DEV, 256, D_MODEL)], axis=1)
    return dx, dxb, dict(s1024=s1024, norm=dnorm[0], conv_w=dw)


def _attn_tiles(s):
    t = min(ATTN_TILE, s)
    return t, s // t


def _causal_keep(t, keys_on_rows):
    r = lax.broadcasted_iota(jnp.int32, (t, t), 0)
    c = lax.broadcasted_iota(jnp.int32, (t, t), 1)
    return (r <= c) if keys_on_rows else (c <= r)


def _mla_fwd(x, w, rope, tm):
    s = x.shape[0]
    nt = s // tm
    cosf, sinf, perm = rope
    xn = _rms_fwd("mla_rms", x, w["norm"], tm)

    def in_body(xn_ref, wq_ref, wkv_ref, wkr_ref, wz_ref, gq_ref, gkv_ref, cos_ref, sin_ref, p_ref,
                ql_ref, kvl_ref, qn_ref, kvn_ref, krr_ref, z_ref):
        xv = xn_ref[...]
        ql = _dot(xv, wq_ref[...], NN)
        kvl = _dot(xv, wkv_ref[...], NN)
        ql_ref[...] = ql
        kvl_ref[...] = kvl
        rq = lax.rsqrt(jnp.mean(ql * ql, axis=-1, keepdims=True) + NORM_EPS)
        qn_ref[...] = ((ql * rq) * gq_ref[...]).astype(BF)
        rkv = lax.rsqrt(jnp.mean(kvl * kvl, axis=-1, keepdims=True) + NORM_EPS)
        kvn_ref[...] = ((kvl * rkv) * gkv_ref[...]).astype(BF)
        kr = _dot(xv, wkr_ref[...], NN)
        krr_ref[...] = _rope_fwd(kr, cos_ref[...], sin_ref[...], p_ref[...]).astype(BF)
        z_ref[...] = _dot(xv, wz_ref[...], NN)

    def full(a):
        return pl.BlockSpec(a.shape, lambda i: (0,) * a.ndim)

    def rows(c):
        return pl.BlockSpec((tm, c), lambda i: (i, 0))

    gq, gkv = w["q_norm"].reshape(1, Q_RANK), w["kv_norm"].reshape(1, KV_RANK)
    q_lat, kv_lat, qn, kvn, krr, z = pl.pallas_call(
        in_body, name="mla_in", grid=(nt,),
        in_specs=[rows(D_MODEL), full(w["w_q"]), full(w["w_kv"]), full(w["w_kr"]), full(w["w_z"]), full(gq), full(gkv),
                  rows(QK_ROPE), rows(QK_ROPE), full(perm)],
        out_specs=[rows(Q_RANK), rows(KV_RANK), rows(Q_RANK), rows(KV_RANK), rows(QK_ROPE), rows(D_INNER)],
        out_shape=[_sds((s, Q_RANK), F32), _sds((s, KV_RANK), F32), _sds((s, Q_RANK), BF), _sds((s, KV_RANK), BF),
                   _sds((s, QK_ROPE), BF), _sds((s, D_INNER), F32)],
        compiler_params=_params())(xn, w["w_q"], w["w_kv"], w["w_kr"], w["w_z"], gq, gkv, cosf, sinf, perm)

    def q_epi(part, extra, outs, pids):
        outs[0][:, 0:QK_NOPE] = (part[:, 0:QK_NOPE] * Q_PRESCALE).astype(BF)
        roped = _rope_fwd(part[:, QK_NOPE:QK_DIM], extra[0][...], extra[1][...], extra[2][...])
        outs[0][:, QK_NOPE:QK_DIM] = (roped * Q_PRESCALE).astype(BF)

    rope_row = pl.BlockSpec((tm, QK_ROPE), lambda h, i: (i, 0))
    (q,) = _mm("mla_qup", (N_HEADS, nt), [qn, w["w_qh"], cosf, sinf, perm],
               [pl.BlockSpec((tm, Q_RANK), lambda h, i: (i, 0)), pl.BlockSpec((None, Q_RANK, QK_DIM), lambda h, i: (h, 0, 0)),
                rope_row, rope_row, pl.BlockSpec((QK_ROPE, QK_ROPE), lambda h, i: (0, 0))],
               [_sds((N_HEADS, s, QK_DIM), BF)], [pl.BlockSpec((None, tm, QK_DIM), lambda h, i: (h, i, 0))], NN, q_epi)

    def kv_epi(part, extra, outs, pids):
        outs[0][:, 0:QK_NOPE] = part[:, 0:QK_NOPE].astype(BF)
        outs[0][:, QK_NOPE:QK_DIM] = extra[0][...]
        outs[1][...] = part[:, QK_NOPE:].astype(BF)

    k, v = _mm("mla_kvup", (N_HEADS, nt), [kvn, w["g512"], krr],
               [pl.BlockSpec((tm, KV_RANK), lambda h, i: (i, 0)),
                pl.BlockSpec((None, KV_RANK, 256), lambda h, i: (h // 2, 0, h % 2)), rope_row],
               [_sds((N_HEADS, s, QK_DIM), BF), _sds((N_HEADS, s, V_DIM), BF)],
               [pl.BlockSpec((None, tm, QK_DIM), lambda h, i: (h, i, 0)), pl.BlockSpec((None, tm, V_DIM), lambda h, i: (h, i, 0))],
               NN, kv_epi)

    t, nq = _attn_tiles(s)

    def attn_body(q_ref, k_ref, v_ref, z_ref, o_ref, g_ref, lse_ref):
        i = pl.program_id(1)
        qv = q_ref[...]

        def block(j, carry, masked):
            m, lsum, acc = carry
            start = pl.multiple_of(j * t, t)
            sc = _dot(qv, k_ref[pl.ds(start, t), :], NT)
            if masked:
                sc = jnp.where(_causal_keep(t, False), sc, NEG_BIG)
            mn = jnp.maximum(m, jnp.max(sc, axis=-1, keepdims=True))
            alpha = jnp.exp2(m - mn)
            p = jnp.exp2(sc - mn)
            lsum = alpha * lsum + jnp.sum(p, axis=-1, keepdims=True)
            acc = alpha * acc + _dot(p.astype(BF), v_ref[pl.ds(start, t), :], NN)
            return mn, lsum, acc

        init = (jnp.full((t, 1), NEG_BIG, F32), jnp.zeros((t, 1), F32), jnp.zeros((t, V_DIM), F32))
        carry = lax.fori_loop(0, i, lambda j, c: block(j, c, False), init)
        m, lsum, acc = block(i, carry, True)
        o = acc / lsum
        z = z_ref[...]
        o_ref[...] = o
        g_ref[...] = (o * (z * _sig(z))).astype(BF)
        lse_ref[...] = m + jnp.log(lsum) * LOG2_E

    head_col = pl.BlockSpec((t, V_DIM), lambda h, i: (i, h))
    o, gated, lse = pl.pallas_call(
        attn_body, name="mla_attn", grid=(N_HEADS, nq),
        in_specs=[pl.BlockSpec((None, t, QK_DIM), lambda h, i: (h, i, 0)), pl.BlockSpec((None, s, QK_DIM), lambda h, i: (h, 0, 0)),
                  pl.BlockSpec((None, s, V_DIM), lambda h, i: (h, 0, 0)), head_col],
        out_specs=[head_col, head_col, pl.BlockSpec((None, t, 1), lambda h, i: (h, i, 0))],
        out_shape=[_sds((s, D_INNER), F32), _sds((s, D_INNER), BF), _sds((N_HEADS, s, 1), F32)],
        compiler_params=_params())(q, k, v, z)
    y = _out_proj("mla_out", gated, w["g1024"], 0, x, tm)
    return y, dict(x=x, xn=xn, q_lat=q_lat, kv_lat=kv_lat, qn=qn, kvn=kvn, z=z, q=q, k=k, v=v, o=o, lse=lse, gated=gated)


def _mla_bwd(dy, dyb, w, sv, rope, tm, dep):
    s = dy.shape[0]
    nt = s // tm
    cosf, sinf, perm = rope
    t, nq = _attn_tiles(s)
    q, k, v, lse = sv["q"], sv["k"], sv["v"], sv["lse"]

    def gate_bwd(part, extra, outs, pids):
        z, o = extra[0][...], extra[1][...]
        sz, dsz = _silu_and_grad(z)
        do = part * sz
        outs[0][...] = do.astype(BF)
        outs[1][...] = (part * o * dsz).astype(BF)
        prod = do * o
        for hh in range(4):
            outs[2][hh] = jnp.sum(prod[:, hh * V_DIM:(hh + 1) * V_DIM], axis=-1, keepdims=True)

    tile = lambda j, i: (i, j)
    dob, dz, delta = _mm(
        "mlab_out", (4, nt), [dyb, w["g1024"], sv["z"], sv["o"]] + dep,
        [pl.BlockSpec((tm, D_MODEL), lambda j, i: (i, 0)), _w_out_nt_block(0),
         pl.BlockSpec((tm, 512), tile), pl.BlockSpec((tm, 512), tile)] + [_ANY] * len(dep),
        [_sds((s, D_INNER), BF), _sds((s, D_INNER), BF), _sds((N_HEADS, s, 1), F32)],
        [pl.BlockSpec((tm, 512), tile), pl.BlockSpec((tm, 512), tile), pl.BlockSpec((4, tm, 1), lambda j, i: (j, i, 0))],
        NT, gate_bwd)

    def attn_bwd_body(k_ref, v_ref, q_ref, do_ref, lse_ref, dl_ref, cos_ref, sin_ref, p_ref, dkv_ref, dkr_ref, dq_ref, dq_acc):
        j = pl.program_id(1)
        kb, vb = k_ref[...], v_ref[...]

        @pl.when(j == 0)
        def _():
            dq_acc[...] = jnp.zeros(dq_acc.shape, F32)

        def block(i, carry, masked):
            dk, dv = carry
            rows = pl.ds(pl.multiple_of(i * t, t), t)
            qb, dob_ = q_ref[rows, :], do_ref[rows, :]
            st = _dot(kb, qb, NT)
            if masked:
                st = jnp.where(_causal_keep(t, True), st, NEG_BIG)
            pt = jnp.exp2(st - lse_ref[i])
            dv = dv + _dot(pt.astype(BF), dob_, NN)
            dst = (pt * (_dot(vb, dob_, NT) - dl_ref[i])).astype(BF)
            dk = dk + _dot(dst, qb, NN)
            dq_acc[rows, :] += _dot(dst, kb, TN)
            return dk, dv

        carry = block(j, (jnp.zeros((t, QK_DIM), F32), jnp.zeros((t, V_DIM), F32)), True)
        dk, dv = lax.fori_loop(j + 1, nq, lambda i, c: block(i, c, False), carry)
        dk = dk * LN_2
        dkv_ref[:, 0:QK_NOPE] = dk[:, 0:QK_NOPE].astype(BF)
        dkv_ref[:, QK_NOPE:] = dv.astype(BF)
        dkr_ref[...] = dk[:, QK_NOPE:]

        @pl.when(j == nq - 1)
        def _():
            for c in range(nq):
                rows = slice(c * t, (c + 1) * t)
                dq = dq_acc[rows, :] * ATTN_SCALE
                dq_ref[rows, 0:QK_NOPE] = dq[:, 0:QK_NOPE].astype(BF)
                dq_ref[rows, QK_NOPE:] = _rope_bwd(dq[:, QK_NOPE:], cos_ref[rows, :], sin_ref[rows, :], p_ref[...]).astype(BF)

    row_stats = pl.BlockSpec((None, nq, 1, t), lambda h, j: (h, 0, 0, 0))
    seq_rope = pl.BlockSpec((s, QK_ROPE), lambda h, j: (0, 0))
    head_seq = pl.BlockSpec((None, s, QK_DIM), lambda h, j: (h, 0, 0))
    dkv, dkr_h, dq = pl.pallas_call(
        attn_bwd_body, name="mlab_attn", grid=(N_HEADS, nq),
        in_specs=[pl.BlockSpec((None, t, QK_DIM), lambda h, j: (h, j, 0)), pl.BlockSpec((None, t, V_DIM), lambda h, j: (h, j, 0)),
                  head_seq, pl.BlockSpec((s, V_DIM), lambda h, j: (0, h)), row_stats, row_stats, seq_rope, seq_rope,
                  pl.BlockSpec((QK_ROPE, QK_ROPE), lambda h, j: (0, 0))],
        out_specs=[pl.BlockSpec((t, 2 * V_DIM), lambda h, j: (j, h)), pl.BlockSpec((None, t, QK_ROPE), lambda h, j: (h, j, 0)), head_seq],
        out_shape=[_sds((s, N_HEADS * 2 * V_DIM), BF), _sds((N_HEADS, s, QK_ROPE), F32), _sds((N_HEADS, s, QK_DIM), BF)],
        scratch_shapes=[pltpu.VMEM((s, QK_DIM), F32)],
        compiler_params=_params())(k, v, q, dob, lse.reshape(N_HEADS, nq, 1, t), delta.reshape(N_HEADS, nq, 1, t), cosf, sinf, perm)

    def dkr_body(d_ref, cos_ref, sin_ref, p_ref, o_ref):
        tot = d_ref[0]
        for hh in range(1, N_HEADS):
            tot = tot + d_ref[hh]
        o_ref[...] = _rope_bwd(tot, cos_ref[...], sin_ref[...], p_ref[...]).astype(BF)

    r64 = pl.BlockSpec((tm, QK_ROPE), lambda i: (i, 0))
    dkr = pl.pallas_call(
        dkr_body, name="mlab_dkr", grid=(nt,),
        in_specs=[pl.BlockSpec((N_HEADS, tm, QK_ROPE), lambda i: (0, i, 0)), r64, r64, pl.BlockSpec((QK_ROPE, QK_ROPE), lambda i: (0, 0))],
        out_specs=r64, out_shape=_sds((s, QK_ROPE), BF), compiler_params=_params())(dkr_h, cosf, sinf, perm)

    def lat_epi(acc, extra, outs, pids):
        dx, dg = _rms_bwd(acc, extra[0][...], extra[1][...], None)
        outs[0][...] = dx.astype(BF)
        _accumulate(outs[1], dg, pids[0])

    def lat_bwd(name, a, a_spec, b, b_spec, n_k, lat, g, rank):
        row = lambda i, k: (i, 0)
        one = lambda i, k: (0, 0)
        return _mm(name, (nt, n_k), [a, b, lat, g.reshape(1, rank)],
                   [a_spec, b_spec, pl.BlockSpec((tm, rank), row), pl.BlockSpec((1, rank), one)],
                   [_sds((s, rank), BF), _sds((1, rank), F32)], [pl.BlockSpec((tm, rank), row), pl.BlockSpec((1, rank), one)],
                   NT, lat_epi, red=1, acc_shape=(tm, rank))

    d_ql, g_qnorm = lat_bwd("mlab_qup", dq, pl.BlockSpec((None, tm, QK_DIM), lambda i, h: (h, i, 0)),
                            w["w_qh"], pl.BlockSpec((None, Q_RANK, QK_DIM), lambda i, h: (h, 0, 0)), N_HEADS,
                            sv["q_lat"], w["q_norm"], Q_RANK)
    d_kvl, g_kvnorm = lat_bwd("mlab_kvup", dkv, pl.BlockSpec((tm, 512), lambda i, kk: (i, kk)),
                              w["g512"], pl.BlockSpec((None, KV_RANK, 512), lambda i, kk: (kk, 0, 0)), N_DEV,
                              sv["kv_lat"], w["kv_norm"], KV_RANK)

    def in_bwd(dql_ref, dkvl_ref, dkr_ref, dz_ref, wq_ref, wkv_ref, wkr_ref, wz_ref, x_ref, g_ref, dy_ref, dx_ref, dxb_ref, dg_ref):
        acc = (_dot(dql_ref[...], wq_ref[...], NT) + _dot(dkvl_ref[...], wkv_ref[...], NT)
               + _dot(dkr_ref[...], wkr_ref[...], NT) + _dot(dz_ref[...], wz_ref[...], NT))
        dx, dg = _rms_bwd(acc, x_ref[...], g_ref[...], dy_ref[...])
        dx_ref[...] = dx
        dxb_ref[...] = dx.astype(BF)
        _accumulate(dg_ref, dg, pl.program_id(0))

    def full(a):
        return pl.BlockSpec(a.shape, lambda i: (0,) * a.ndim)

    def rows(c):
        return pl.BlockSpec((tm, c), lambda i: (i, 0))

    gm = w["norm"].reshape(1, D_MODEL)
    dx, dxb, g_norm = pl.pallas_call(
        in_bwd, name="mlab_in", grid=(nt,),
        in_specs=[rows(Q_RANK), rows(KV_RANK), rows(QK_ROPE), rows(D_INNER), full(w["w_q"]), full(w["w_kv"]), full(w["w_kr"]),
                  full(w["w_z"]), rows(D_MODEL), full(gm), rows(D_MODEL)],
        out_specs=[rows(D_MODEL), rows(D_MODEL), full(gm)],
        out_shape=[_sds((s, D_MODEL), F32), _sds((s, D_MODEL), BF), _sds((1, D_MODEL), F32)],
        compiler_params=_params())(d_ql, d_kvl, dkr, dz, w["w_q"], w["w_kv"], w["w_kr"], w["w_z"], sv["x"], gm, dy)

    xn = sv["xn"]
    one = lambda j: (0, 0)
    g_q = _tn("mlab_gq", xn, d_ql, (D_MODEL, Q_RANK), (D_MODEL, Q_RANK), one, D_MODEL, Q_RANK, (1,), one, one)
    g_kv = _tn("mlab_gkv", xn, d_kvl, (D_MODEL, KV_RANK), (D_MODEL, KV_RANK), one, D_MODEL, KV_RANK, (1,), one, one)
    g_kr = _tn("mlab_gkr", xn, dkr, (D_MODEL, QK_ROPE), (D_MODEL, QK_ROPE), one, D_MODEL, QK_ROPE, (1,), one, one)
    g_z = _tn("mlab_gz", xn, dz, (D_MODEL, D_INNER), (D_MODEL, 512), lambda j: (0, j), D_MODEL, 512, (4,), one, lambda j: (0, j))
    g_in = jnp.concatenate([g_q, g_kv, g_kr, g_z], axis=1)
    g_qh = _tn("mlab_gqup", sv["qn"], dq, (N_HEADS, Q_RANK, QK_DIM), (None, Q_RANK, QK_DIM), lambda h: (h, 0, 0),
               Q_RANK, QK_DIM, (N_HEADS,), lambda h: (0, 0), lambda h: (h, 0, 0))
    g_kvup = _tn("mlab_gkvup", sv["kvn"], dkv, (N_DEV, KV_RANK, 512), (None, KV_RANK, 512), lambda j: (j, 0, 0),
                 KV_RANK, 512, (N_DEV,), lambda j: (0, 0), lambda j: (0, j))
    g_out = _w_out_grad("mlab_gout", sv["gated"], dyb)
    s384 = g_qh.reshape(N_DEV, 2, Q_RANK, QK_DIM).transpose(0, 2, 1, 3).reshape(N_DEV, Q_RANK, 2 * QK_DIM)
    s344 = g_in.reshape(D_MODEL, N_DEV, 344).transpose(1, 0, 2)
    return dx, dxb, dict(s344=s344, s384=s384, s512=g_kvup, s1024=g_out.reshape(N_DEV, 256, D_MODEL),
                         norm=g_norm[0], q_norm=g_qnorm[0], kv_norm=g_kvnorm[0])


def _loss_head(x, g, target, tm):
    s, d = x.shape

    def body(x_ref, g_ref, t_ref, dx_ref, dxb_ref, dg_ref, loss_ref):
        i = pl.program_id(0)
        xv, gv = x_ref[...], g_ref[...]
        r = lax.rsqrt(jnp.mean(xv * xv, axis=-1, keepdims=True) + NORM_EPS)
        err = (xv * r) * gv - t_ref[...]
        part = 0.5 * jnp.sum(jnp.mean(err * err, axis=-1, keepdims=True), axis=0, keepdims=True)
        dx, dg = _rms_bwd(err * (1.0 / d), xv, gv, None)
        dx_ref[...] = dx
        dxb_ref[...] = dx.astype(BF)
        _accumulate(dg_ref, dg, i)
        _accumulate(loss_ref, jnp.broadcast_to(part, loss_ref.shape), i)

    row = pl.BlockSpec((tm, d), lambda i: (i, 0))
    one = pl.BlockSpec((1, d), lambda i: (0, 0))
    return pl.pallas_call(
        body, name="loss_head", grid=(s // tm,), in_specs=[row, one, row],
        out_specs=[row, row, one, pl.BlockSpec((8, 128), lambda i: (0, 0))],
        out_shape=[_sds((s, d), F32), _sds((s, d), BF), _sds((1, d), F32), _sds((8, 128), F32)],
        compiler_params=_params())(x, g.reshape(1, d), target)


def _rope_tables(pos):
    inv_freq = ROPE_BASE ** (-jnp.arange(0, QK_ROPE, 2, dtype=F32) / QK_ROPE)
    ang = pos.astype(F32)[:, None] * inv_freq
    cos, sin = jnp.cos(ang), jnp.sin(ang)
    idx = jnp.arange(QK_ROPE)
    perm = (idx[:, None] == (idx[None, :] + QK_ROPE // 2) % QK_ROPE).astype(F32)
    return jnp.concatenate([cos, cos], axis=1), jnp.concatenate([-sin, sin], axis=1), perm


def _local_step(x, pos, target, final_norm, get_w, put_g):
    s = x.shape[0]
    tm = min(512, s)
    rope = _rope_tables(pos)
    w0 = get_w(0, [])
    x1, sv0 = _pool_fwd(x, 0, w0, tm)
    w1 = get_w(1, [x1])
    x2, sv1 = _conv_fwd(x1, w1, tm)
    w2 = get_w(2, [x2])
    x3, sv2 = _mla_fwd(x2, w2, rope, tm)
    w3 = get_w(3, [x3])
    x4, sv3 = _pool_fwd(x3, 1, w3, tm)
    d4, d4b, g_final, loss = _loss_head(x4, final_norm, target, tm)
    d3, d3b, gp1 = _pool_bwd(d4, d4b, 1, w3, sv3, tm, [])
    dep = put_g(3, gp1)
    d2, d2b, gm = _mla_bwd(d3, d3b, w2, sv2, rope, tm, dep)
    dep = put_g(2, gm)
    d1, d1b, gc = _conv_bwd(d2, d2b, w1, sv1, tm, dep)
    dep = put_g(1, gc)
    d0, _, gp0 = _pool_bwd(d1, d1b, 0, w0, sv0, tm, dep, early=lambda big: put_g(0, big))
    put_g(4, dict(gp0, final_norm=g_final[0]))
    return loss[0, 0], d0


def _pack_groups(p):
    def pool(l):
        both = jnp.concatenate([p["pool_w_in"][l], p["pool_w_grp"][l].reshape(POOL_512_ROWS - POOL_GRP_ROW, POOL_GROUP)], axis=0)
        return [both.astype(BF), p["pool_w_out"][l].astype(BF)]

    conv = [jnp.concatenate([p["conv_w_in"][0], p["conv_w_out"][0]], axis=0).astype(BF)]
    mla = [p[k][0].astype(BF) for k in ("mla_w_in", "mla_w_q_up", "mla_w_kv_up", "mla_w_out")]
    return [pool(0) + [_pack_small(p, SMALL_ROWS_AG)], conv, mla, pool(1)]


_SMALL_SHARDED = ("pool_norm", "pool_scale", "mla_norm", "mla_q_norm", "mla_kv_norm", "conv_w")
_SMALL_REPLICATED = ("conv_norm", "final_norm")


def _pack_small(p, rows, with_replicated=False):
    parts = [p[k].reshape(-1) for k in _SMALL_SHARDED]
    if with_replicated:
        parts += [p[k].reshape(-1) for k in _SMALL_REPLICATED]
    flat = jnp.concatenate(parts)
    return jnp.pad(flat, (0, rows * 128 - flat.shape[0])).reshape(rows, 128)


_SMALL_SHARD_SHAPES = dict(pool_norm=(2, 128), pool_scale=(2, 256), mla_norm=(1, 128), mla_q_norm=(1, 48),
                           mla_kv_norm=(1, 32), conv_w=(1, 3, 256), conv_norm=(1, 1024), final_norm=(1024,))


def _unpack_small(buf, with_replicated=False):
    flat = buf.reshape(-1)
    out, off = {}, 0
    for k in _SMALL_SHARDED + (_SMALL_REPLICATED if with_replicated else ()):
        shp = _SMALL_SHARD_SHAPES[k]
        n = 1
        for d in shp:
            n *= d
        out[k] = flat[off:off + n].reshape(shp)
        off += n
    return out


def _small_views(gsmall):
    flat = gsmall.reshape(N_DEV, -1)

    def cols(off, rows, width):
        return flat[:, off:off + rows * width].reshape(N_DEV, rows, width).transpose(1, 0, 2).reshape(rows, N_DEV * width)

    return dict(pool_norm=cols(0, 2, 128), pool_scale=cols(256, 2, 256), mla_norm=cols(768, 1, 128)[0],
                q_norm=cols(896, 1, 48)[0], kv_norm=cols(944, 1, 32)[0], conv_w=cols(976, 3, 256))


def _layer_weights(layer, bufs, small, conv_norm):
    if layer in (0, 3):
        l = 0 if layer == 0 else 1
        return dict(g512=bufs[0], g1024=bufs[1], norm=small["pool_norm"][l], scale=small["pool_scale"][l])
    if layer == 1:
        return dict(g1024=bufs[0], norm=conv_norm.reshape(D_MODEL), conv_w=small["conv_w"])
    g344, g384, g512, g1024 = bufs
    w_in = g344.transpose(1, 0, 2).reshape(D_MODEL, N_DEV * 344)
    return dict(
        g512=g512, g1024=g1024,
        w_q=w_in[:, :Q_RANK], w_kv=w_in[:, Q_RANK:Q_RANK + KV_RANK],
        w_kr=w_in[:, Q_RANK + KV_RANK:Q_RANK + KV_RANK + QK_ROPE], w_z=w_in[:, Q_RANK + KV_RANK + QK_ROPE:],
        w_qh=g384.reshape(N_DEV, Q_RANK, 2, QK_DIM).transpose(0, 2, 1, 3).reshape(N_HEADS, Q_RANK, QK_DIM),
        norm=small["mla_norm"], q_norm=small["q_norm"], kv_norm=small["kv_norm"])


def _grad_group(layer, g):
    if layer in (0, 3):
        return [g["s512"], g["s1024"]]
    if layer == 1:
        return [g["s1024"]]
    return [g["s344"], g["s384"], g["s512"], g["s1024"]]


def _pack_small_grads(g):
    def split(a, rows, width):
        return a.reshape(rows, N_DEV, width).transpose(1, 0, 2).reshape(N_DEV, rows * width)

    rep = lambda a: jnp.broadcast_to(a.reshape(1, -1), (N_DEV, a.size))
    flat = jnp.concatenate([
        split(jnp.stack([g[0]["norm"], g[3]["norm"]]), 2, 128), split(jnp.stack([g[0]["scale"], g[3]["scale"]]), 2, 256),
        split(g[2]["norm"], 1, 128), split(g[2]["q_norm"], 1, 48), split(g[2]["kv_norm"], 1, 32), split(g[1]["conv_w"], 3, 256),
        rep(g[1]["norm"]), rep(g[0]["final_norm"])], axis=1)
    return jnp.pad(flat, ((0, 0), (0, SMALL_ROWS_RS * 128 - flat.shape[1]))).reshape(N_DEV, SMALL_ROWS_RS, 128)


def _peers(x, y, c):
    for k in range(1, N_DEV):
        px = 1 - x if k & 4 else x
        py = 1 - y if k & 2 else y
        pc = 1 - c if k & 1 else c
        yield k - 1, (px, py, pc), 4 * px + 2 * py + pc


def _remote_copies(srcs, lands, send_sems, recv_sems, gather):
    x, y, c = lax.axis_index("x"), lax.axis_index("y"), lax.axis_index("c")
    me = 4 * x + 2 * y + c
    copies = []
    for k, peer, pidx in _peers(x, y, c):
        for a, (src, land) in enumerate(zip(srcs, lands)):
            copies.append(pltpu.make_async_remote_copy(
                src_ref=src if gather else src.at[pidx], dst_ref=land.at[me],
                send_sem=send_sems.at[a * (N_DEV - 1) + k], recv_sem=recv_sems.at[a * (N_DEV - 1) + k],
                device_id=peer, device_id_type=pl.DeviceIdType.MESH))
    return copies


_HBM = pl.BlockSpec(memory_space=pltpu.HBM)
_SEM = pl.BlockSpec(memory_space=pltpu.SEMAPHORE)
_EFFECT = pltpu.SideEffectType.DATAFLOW_SIDE_EFFECTING


def _own_slabs(name, arrays, gather, dep):
    n, nd = len(arrays), len(dep)
    me = (4 * lax.axis_index("x") + 2 * lax.axis_index("y") + lax.axis_index("c")).astype(jnp.int32).reshape(1)

    def body(me_ref, *refs):
        for a in range(n):
            refs[n + nd + a][...] = refs[a][...]

    def slab(shape):
        return pl.BlockSpec((None,) + tuple(shape), lambda i, me_ref: (me_ref[0],) + (0,) * len(shape))

    def whole(shape):
        return pl.BlockSpec(tuple(shape), lambda i, me_ref: (0,) * len(shape))

    outs = [_sds(((N_DEV,) + a.shape) if gather else a.shape, a.dtype) for a in arrays]
    grid_spec = pltpu.PrefetchScalarGridSpec(
        num_scalar_prefetch=1, grid=(1,),
        in_specs=[whole(a.shape) if gather else slab(a.shape[1:]) for a in arrays] + [_ANY] * nd,
        out_specs=[slab(o.shape[1:]) for o in outs])
    return pl.pallas_call(body, name=name, grid_spec=grid_spec, out_shape=outs, compiler_params=_params())(me, *arrays, *dep)


def _exchange_start(name, arrays, lands, gather):
    n = len(arrays)

    def body(*refs):
        srcs, lnds, send_sems, recv_sems, token = refs[:n], refs[n:2 * n], refs[2 * n], refs[2 * n + 1], refs[-1]
        for cp in _remote_copies(srcs, lnds, send_sems, recv_sems, gather):
            cp.start()
        token[...] = jnp.zeros(token.shape, F32)

    sems = pltpu.SemaphoreType.DMA((n * (N_DEV - 1),))
    thru = [pltpu.HBM(a.shape, a.dtype) for a in list(arrays) + list(lands)]
    res = pl.pallas_call(
        body, name=name, in_specs=[_HBM] * (2 * n),
        out_specs=[_SEM, _SEM] + [_HBM] * (2 * n) + [pl.BlockSpec(memory_space=pltpu.VMEM)],
        out_shape=[sems, sems] + thru + [_sds((8, 128), F32)],
        input_output_aliases={i: 2 + i for i in range(2 * n)},
        compiler_params=pltpu.CompilerParams(has_side_effects=_EFFECT),
    )(*[pltpu.with_memory_space_constraint(a, pltpu.HBM) for a in list(arrays) + list(lands)])
    return res[0], res[1], list(res[2:2 + n]), list(res[2 + n:2 + 2 * n]), res[-1]


def _exchange_wait(name, send_sems, recv_sems, arrays, lands, after, gather):
    n = len(arrays)
    n_after = len(after)

    def body(*refs):
        srcs, lnds = refs[:n], refs[n:2 * n]
        copies = _remote_copies(srcs, lnds, refs[2 * n], refs[2 * n + 1], gather)
        for cp in copies:
            cp.wait_send()
        for cp in copies:
            cp.wait_recv()

    thru = [pltpu.HBM(a.shape, a.dtype) for a in list(arrays) + list(lands)]
    res = pl.pallas_call(
        body, name=name, in_specs=[_HBM] * (2 * n) + [_SEM, _SEM] + [pl.BlockSpec(memory_space=pl.ANY)] * n_after,
        out_specs=[_HBM] * (2 * n), out_shape=thru, input_output_aliases={i: i for i in range(2 * n)},
        compiler_params=pltpu.CompilerParams(has_side_effects=_EFFECT),
    )(*arrays, *lands, send_sems, recv_sems, *after)
    return list(res[n:])


def _adamw_math(g, w, m, v):
    m = ADAM_B1 * m + (1.0 - ADAM_B1) * g
    v = ADAM_B2 * v + (1.0 - ADAM_B2) * (g * g)
    m_hat = m / (1.0 - ADAM_B1 ** ADAM_STEP)
    v_hat = v / (1.0 - ADAM_B2 ** ADAM_STEP)
    delta = -ADAM_LR * (m_hat / (jnp.sqrt(v_hat) + ADAM_EPS) + ADAM_WD * w)
    return delta, m, v


def _sum_adamw(name, recv, row_off, w, m, v, tr, layer=0):
    width = recv.shape[-1]
    w2, m2, v2 = (a.reshape(a.shape[0], -1, width) for a in (w, m, v))
    rows = w2.shape[1]
    base = row_off // tr

    def body(r_ref, w_ref, m_ref, v_ref, g_ref, d_ref, mo_ref, vo_ref):
        g = r_ref[0].astype(F32)
        for src in range(1, N_DEV):
            g = g + r_ref[src].astype(F32)
        delta, mn, vn = _adamw_math(g, w_ref[...], m_ref[...], v_ref[...])
        g_ref[...] = g
        d_ref[...] = delta
        mo_ref[...] = mn
        vo_ref[...] = vn

    blk = pl.BlockSpec((tr, width), lambda i: (i, 0))
    wblk = pl.BlockSpec((None, tr, width), lambda i: (layer, i, 0))
    return pl.pallas_call(
        body, name=name, grid=(rows // tr,),
        in_specs=[pl.BlockSpec((N_DEV, tr, width), lambda i: (0, base + i, 0)), wblk, wblk, wblk],
        out_specs=[blk] * 4, out_shape=[_sds((rows, width), F32)] * 4, compiler_params=_params())(recv, w2, m2, v2)


_WEIGHTS = ("pool_norm", "pool_w_in", "pool_w_grp", "pool_scale", "pool_w_out", "conv_norm", "conv_w_in", "conv_w", "conv_w_out",
            "mla_norm", "mla_w_in", "mla_q_norm", "mla_w_q_up", "mla_kv_norm", "mla_w_kv_up", "mla_w_out", "final_norm")


def _step(x, positions, loss_target, p, m, v):
    gathers, tokens, dep = [], [], []
    for layer, arrays in enumerate(_pack_groups(p)):
        lands = _own_slabs(f"gather{layer}_own", arrays, True, dep)
        ssem, rsem, arrays, lands, token = _exchange_start(f"gather{layer}_start", arrays, lands, True)
        gathers.append((ssem, rsem, arrays, lands))
        tokens.append(token)
        dep = [token]
    state = {}

    def get_w(layer, after):
        bufs = _exchange_wait(f"gather{layer}_wait", *gathers[layer], after if layer else tokens, True)
        if layer == 0:
            state["small"] = _small_views(bufs[2])
        return _layer_weights(layer, bufs, state["small"], p["conv_norm"])

    scatters, small_grads = {}, {}

    def put_g(layer, g):
        if layer == 4:
            small_grads[0] = g
            arrays = [_pack_small_grads(small_grads)]
        else:
            small_grads[layer] = g
            arrays = _grad_group(layer, g)
        lands = _own_slabs(f"scatter{layer}_own", arrays, False, [])
        ssem, rsem, arrays, lands, token = _exchange_start(f"scatter{layer}_start", arrays, lands, False)
        scatters[layer] = (ssem, rsem, arrays, lands)
        tokens.append(token)
        return [token]

    loss, grad_x = _local_step(x[0], positions[0], loss_target[0], p["final_norm"], get_w, put_g)

    def adam(name, recv, row_off, key, tr, layer=0):
        return _sum_adamw("adam_" + name, recv, row_off, p[key], m[key], v[key], tr, layer)

    res, after = {}, [tokens[-1]]
    for layer in (3, 2, 1, 0):
        recv = _exchange_wait(f"scatter{layer}_wait", *scatters[layer], after, False)
        if layer in (0, 3):
            l = 0 if layer == 0 else 1
            res["pool_w_in", l] = adam(f"pool_w_in{l}", recv[0], 0, "pool_w_in", 256, l)
            res["pool_w_grp", l] = adam(f"pool_w_grp{l}", recv[0], POOL_GRP_ROW, "pool_w_grp", 256, l)
            res["pool_w_out", l] = adam(f"pool_w_out{l}", recv[1], 0, "pool_w_out", 256, l)
            after = [res["pool_w_out", l][1]]
        elif layer == 1:
            res["conv_w_in", 0] = adam("conv_w_in", recv[0], 0, "conv_w_in", 256)
            res["conv_w_out", 0] = adam("conv_w_out", recv[0], CONV_OUT_ROW, "conv_w_out", 256)
            after = [res["conv_w_out", 0][1]]
        else:
            res["mla_w_in", 0] = adam("mla_w_in", recv[0], 0, "mla_w_in", 256)
            res["mla_w_q_up", 0] = adam("mla_w_q_up", recv[1], 0, "mla_w_q_up", 384)
            res["mla_w_kv_up", 0] = adam("mla_w_kv_up", recv[2], 0, "mla_w_kv_up", 256)
            res["mla_w_out", 0] = adam("mla_w_out", recv[3], 0, "mla_w_out", 256)
            after = [res["mla_w_out", 0][1]]
    recv = _exchange_wait("scatter4_wait", *scatters[4], after, False)
    small = _sum_adamw("adam_small", recv[0], 0, _pack_small(p, SMALL_ROWS_RS, True)[None], _pack_small(m, SMALL_ROWS_RS, True)[None],
                       _pack_small(v, SMALL_ROWS_RS, True)[None], SMALL_ROWS_RS)
    small = [_unpack_small(a, True) for a in small]
    final = {k: tuple(part[k] for part in small) for k in _SMALL_SHARDED + _SMALL_REPLICATED}
    for k in _WEIGHTS:
        if k not in final:
            layers = [res[k, l] for l in range(p[k].shape[0])]
            final[k] = tuple(jnp.stack([lay[part] for lay in layers]).reshape(p[k].shape) for part in range(4))
    res = final

    loss = lax.psum(loss, ("x", "y", "c"))
    out = [loss, grad_x[None]]
    for part in range(4):
        out += [res[k][part] for k in _WEIGHTS]
    return tuple(out)


def kernel(x, positions, pool_norm, pool_w_in, pool_w_grp, pool_scale, pool_w_out, conv_norm, conv_w_in, conv_w, conv_w_out, mla_norm, mla_w_in, mla_q_norm, mla_w_q_up, mla_kv_norm, mla_w_kv_up, mla_w_out, final_norm, loss_target, m_pool_norm, m_pool_w_in, m_pool_w_grp, m_pool_scale, m_pool_w_out, m_conv_norm, m_conv_w_in, m_conv_w, m_conv_w_out, m_mla_norm, m_mla_w_in, m_mla_q_norm, m_mla_w_q_up, m_mla_kv_norm, m_mla_w_kv_up, m_mla_w_out, m_final_norm, v_pool_norm, v_pool_w_in, v_pool_w_grp, v_pool_scale, v_pool_w_out, v_conv_norm, v_conv_w_in, v_conv_w, v_conv_w_out, v_mla_norm, v_mla_w_in, v_mla_q_norm, v_mla_w_q_up, v_mla_kv_norm, v_mla_w_kv_up, v_mla_w_out, v_final_norm):
    p = dict(pool_norm=pool_norm, pool_w_in=pool_w_in, pool_w_grp=pool_w_grp, pool_scale=pool_scale, pool_w_out=pool_w_out,
             conv_norm=conv_norm, conv_w_in=conv_w_in, conv_w=conv_w, conv_w_out=conv_w_out, mla_norm=mla_norm, mla_w_in=mla_w_in,
             mla_q_norm=mla_q_norm, mla_w_q_up=mla_w_q_up, mla_kv_norm=mla_kv_norm, mla_w_kv_up=mla_w_kv_up, mla_w_out=mla_w_out,
             final_norm=final_norm)
    m = dict(pool_norm=m_pool_norm, pool_w_in=m_pool_w_in, pool_w_grp=m_pool_w_grp, pool_scale=m_pool_scale, pool_w_out=m_pool_w_out,
             conv_norm=m_conv_norm, conv_w_in=m_conv_w_in, conv_w=m_conv_w, conv_w_out=m_conv_w_out, mla_norm=m_mla_norm,
             mla_w_in=m_mla_w_in, mla_q_norm=m_mla_q_norm, mla_w_q_up=m_mla_w_q_up, mla_kv_norm=m_mla_kv_norm,
             mla_w_kv_up=m_mla_w_kv_up, mla_w_out=m_mla_w_out, final_norm=m_final_norm)
    v = dict(pool_norm=v_pool_norm, pool_w_in=v_pool_w_in, pool_w_grp=v_pool_w_grp, pool_scale=v_pool_scale, pool_w_out=v_pool_w_out,
             conv_norm=v_conv_norm, conv_w_in=v_conv_w_in, conv_w=v_conv_w, conv_w_out=v_conv_w_out, mla_norm=v_mla_norm,
             mla_w_in=v_mla_w_in, mla_q_norm=v_mla_q_norm, mla_w_q_up=v_mla_w_q_up, mla_kv_norm=v_mla_kv_norm,
             mla_w_kv_up=v_mla_w_kv_up, mla_w_out=v_mla_w_out, final_norm=v_final_norm)
    return _step(x, positions, loss_target, p, m, v)
```

```python
import functools

import jax
import jax.numpy as jnp
from jax import lax
from jax.experimental import pallas as pl
from jax.experimental.pallas import tpu as pltpu

BF = jnp.bfloat16
F32 = jnp.float32

N_DEV = 8
D_MODEL = 1024
D_INNER = 2048
POOL_WINDOWS = (2, 4, 8, 16)
POOL_GROUP = 512
N_HEADS = 16
QK_NOPE = 128
QK_ROPE = 64
QK_DIM = QK_NOPE + QK_ROPE
V_DIM = 128
Q_RANK = 384
KV_RANK = 256
ATTN_SCALE = QK_DIM ** -0.5
LOG2_E = 1.4426950408889634
LN_2 = 0.6931471805599453
Q_PRESCALE = ATTN_SCALE * LOG2_E
ATTN_TILE = 512
ROPE_BASE = 10000.0
NORM_EPS = 1e-6
NEG_BIG = -1e30

ADAM_LR = 0.001
ADAM_B1 = 0.9
ADAM_B2 = 0.999
ADAM_EPS = 1e-08
ADAM_WD = 0.01
ADAM_STEP = 10

VMEM_LIMIT_BYTES = 52 * 1024 * 1024
IN_PROJ_ROWS = 1024
POOL_HALO = 32
CONV_HALO = 16

NN = (((1,), (0,)), ((), ()))
NT = (((1,), (1,)), ((), ()))
TN = (((0,), (0,)), ((), ()))

SMALL_ROWS_AG = 16
SMALL_ROWS_RS = 32


def _sds(shape, dtype):
    return jax.ShapeDtypeStruct(tuple(shape), dtype)


def _params():
    return pltpu.CompilerParams(vmem_limit_bytes=VMEM_LIMIT_BYTES)


_ANY = pl.BlockSpec(memory_space=pl.ANY)


def _dot(a, b, dims):
    return lax.dot_general(a, b, dims, preferred_element_type=F32)


def _sig(z):
    return 1.0 / (1.0 + jnp.exp(-z))


def _silu_and_grad(z):
    sig = _sig(z)
    return z * sig, sig * (1.0 + z * (1.0 - sig))


def _rope_swap(x, p):
    return jnp.dot(x, p, precision=lax.Precision.HIGHEST, preferred_element_type=F32)


def _rope_fwd(x, cosf, sinf, p):
    return x * cosf + _rope_swap(x, p) * sinf


def _rope_bwd(dy, cosf, sinf, p):
    return dy * cosf + _rope_swap(dy * sinf, p)


def _rms_bwd(dxn, x, g, res):
    r = lax.rsqrt(jnp.mean(x * x, axis=-1, keepdims=True) + NORM_EPS)
    v = dxn * g
    dx = r * v - x * ((r * r * r) * jnp.mean(v * x, axis=-1, keepdims=True))
    if res is not None:
        dx = dx + res
    dg = jnp.sum(dxn * (x * r), axis=0, keepdims=True)
    return dx, dg


def _accumulate(ref, val, step):
    @pl.when(step == 0)
    def _():
        ref[...] = val

    @pl.when(step > 0)
    def _():
        ref[...] += val


def _mm(name, grid, ins, in_specs, outs, out_specs, dims, epi, red=None, acc_shape=None):
    n_in, n_out = len(ins), len(outs)
    n_red = None if red is None else grid[red]

    def body(*refs):
        in_refs, out_refs = refs[:n_in], refs[n_in:n_in + n_out]
        pids = tuple(pl.program_id(ax) for ax in range(len(grid)))
        a, b = in_refs[0][...], in_refs[1][...]
        if a.ndim == 3:
            a = a.reshape(-1, a.shape[-1])
        if b.ndim == 3:
            b = b.reshape(-1, b.shape[-1])
        part = _dot(a.astype(BF), b.astype(BF), dims)
        if red is None:
            epi(part, in_refs[2:], out_refs, pids)
        else:
            acc = refs[n_in + n_out]
            k = pids[red]
            _accumulate(acc, part, k)

            @pl.when(k == n_red - 1)
            def _():
                epi(acc[...], in_refs[2:], out_refs, pids)

    scratch = [] if red is None else [pltpu.VMEM(acc_shape, F32)]
    return pl.pallas_call(body, name=name, grid=grid, in_specs=in_specs, out_specs=out_specs, out_shape=outs,
                          scratch_shapes=scratch, compiler_params=_params())(*ins)


def _store(part, extra, outs, pids):
    outs[0][...] = part.astype(outs[0].dtype)


def _rms_fwd(name, x, g, tm):
    s, d = x.shape

    def body(x_ref, g_ref, o_ref):
        xv = x_ref[...]
        r = lax.rsqrt(jnp.mean(xv * xv, axis=-1, keepdims=True) + NORM_EPS)
        o_ref[...] = ((xv * r) * g_ref[...]).astype(BF)

    return pl.pallas_call(body, name=name, grid=(s // tm,),
                          in_specs=[pl.BlockSpec((tm, d), lambda i: (i, 0)), pl.BlockSpec((1, d), lambda i: (0, 0))],
                          out_specs=pl.BlockSpec((tm, d), lambda i: (i, 0)), out_shape=_sds((s, d), BF),
                          compiler_params=_params())(x, g.reshape(1, d))


def _tn(name, a, b, out_shape, out_block, out_index, a_cols, b_cols, grid, a_index, b_index):
    s = a.shape[-2]
    a_block = (s, a_cols) if a.ndim == 2 else (None, s, a_cols)
    b_block = (s, b_cols) if b.ndim == 2 else (None, s, b_cols)

    def epi(part, extra, outs, pids):
        outs[0][...] = part.astype(BF).reshape(outs[0].shape)

    return _mm(name, grid, [a, b], [pl.BlockSpec(a_block, a_index), pl.BlockSpec(b_block, b_index)],
               [_sds(out_shape, BF)], [pl.BlockSpec(out_block, out_index)], TN, epi)[0]


def _pool_window_fwd(name, h, tm):
    s = h.shape[0]
    hb = POOL_HALO

    def body(u_ref, halo_ref, o_ref, e_ref, a_ref, b_ref):
        i = pl.program_id(0)
        row = lax.broadcasted_iota(jnp.int32, (tm, 1), 0) + i * tm
        for g, w in enumerate(POOL_WINDOWS):
            cs = slice(g * POOL_GROUP, (g + 1) * POOL_GROUP)
            e_ref[0:hb, :] = jnp.where(i > 0, halo_ref[:, cs].astype(F32), 0.0)
            e_ref[hb:, :] = u_ref[:, cs].astype(F32)
            src, bufs = e_ref, (a_ref, b_ref)
            for lv in range(1, w.bit_length()):
                dst, st, sh = bufs[(lv - 1) % 2], 8 * lv, 2 ** (lv - 1)
                n = hb + tm - st
                dst[st:, :] = src[st:, :] + src[pl.ds(st - sh, n), :]
                src = dst
            cnt = jnp.minimum(row + 1, w).astype(F32)
            o_ref[:, cs] = (src[hb:, :] / cnt - u_ref[:, cs].astype(F32)).astype(BF)

    per = tm // hb
    return pl.pallas_call(
        body, name=name, grid=(s // tm,),
        in_specs=[pl.BlockSpec((tm, D_INNER), lambda i: (i, 0)),
                  pl.BlockSpec((hb, D_INNER), lambda i: (jnp.maximum(i * per - 1, 0), 0))],
        out_specs=pl.BlockSpec((tm, D_INNER), lambda i: (i, 0)), out_shape=_sds((s, D_INNER), BF),
        scratch_shapes=[pltpu.VMEM((hb + tm, POOL_GROUP), F32)] * 3, compiler_params=_params())(h, h)


def _pool_window_bwd(name, dp, tm, dh):
    s = dp.shape[0]
    nt = s // tm
    hb = POOL_HALO

    def body(d_ref, halo_ref, dh_in_ref, o_ref, e_ref, a_ref, b_ref):
        i = pl.program_id(0)
        row = lax.broadcasted_iota(jnp.int32, (tm, 1), 0) + i * tm
        hrow = lax.broadcasted_iota(jnp.int32, (hb, 1), 0) + (i + 1) * tm
        for g, w in enumerate(POOL_WINDOWS):
            cs = slice(g * POOL_GROUP, (g + 1) * POOL_GROUP)
            e_ref[0:tm, :] = d_ref[:, cs] / jnp.minimum(row + 1, w).astype(F32)
            e_ref[tm:, :] = jnp.where(i < nt - 1, halo_ref[:, cs] / jnp.minimum(hrow + 1, w).astype(F32), 0.0)
            src, bufs = e_ref, (a_ref, b_ref)
            for lv in range(1, w.bit_length()):
                dst, sh = bufs[(lv - 1) % 2], 2 ** (lv - 1)
                n = tm + hb - 8 * lv
                dst[0:n, :] = src[0:n, :] + src[pl.ds(sh, n), :]
                src = dst
            o_ref[:, cs] = (src[0:tm, :] - d_ref[:, cs]).astype(BF)

    per = tm // hb
    last = s // hb - 1
    return pl.pallas_call(
        body, name=name, grid=(nt,),
        in_specs=[pl.BlockSpec((tm, D_INNER), lambda i: (i, 0)),
                  pl.BlockSpec((hb, D_INNER), lambda i: (jnp.minimum((i + 1) * per, last), 0)), _ANY],
        out_specs=pl.BlockSpec((tm, D_INNER), lambda i: (i, 0)), out_shape=_sds(dh.shape, BF),
        input_output_aliases={2: 0},
        scratch_shapes=[pltpu.VMEM((hb + tm, POOL_GROUP), F32)] * 3, compiler_params=_params())(dp, dp, dh)


def _grp_block():
    return pl.BlockSpec((N_DEV, 64, POOL_GROUP), lambda i, g: (0, g, 0))


def _pool_fwd(x, l, w, tm):
    s = x.shape[0]
    nt = s // tm
    n = f"pool{l}"
    xn = _rms_fwd(n + "_rms", x, w["norm"], tm)
    ti = IN_PROJ_ROWS if s % IN_PROJ_ROWS == 0 else tm
    (h,) = _mm(n + "_in", (s // ti, 8), [xn, w["g_in"]],
               [pl.BlockSpec((ti, D_MODEL), lambda i, j: (i, 0)), pl.BlockSpec((None, D_MODEL, 512), lambda i, j: (j, 0, 0))],
               [_sds((s, 2 * D_INNER), BF)], [pl.BlockSpec((ti, 512), lambda i, j: (i, j))], NN, _store)
    pooled = _pool_window_fwd(n + "_win", h, tm)
    w = dict(w, **w["rest"]([h]))

    def gate(part, extra, outs, pids):
        z = extra[0][...].astype(F32)
        outs[0][...] = ((part * extra[1][...]) * (z * _sig(z))).astype(BF)

    (gated,) = _mm(n + "_grp", (nt, 4), [pooled, w["g_grp"], h, w["scale"].reshape(1, D_INNER)],
                   [pl.BlockSpec((tm, 512), lambda i, g: (i, g)), _grp_block(),
                    pl.BlockSpec((tm, 512), lambda i, g: (i, 4 + g)), pl.BlockSpec((1, 512), lambda i, g: (0, g))],
                   [_sds((s, D_INNER), BF)], [pl.BlockSpec((tm, 512), lambda i, g: (i, g))], NN, gate)
    y = _out_proj(n + "_out", gated, w["g_out"], 0, x, tm)
    return y, dict(x=x, xn=xn, h=h, pooled=pooled, gated=gated, w=w)


def _out_proj(name, gated, g1024, row_block, x, tm):
    s = x.shape[0]

    def epi(part, extra, outs, pids):
        outs[0][...] = part + extra[0][...]

    return _mm(name, (s // tm, 2), [gated, g1024, x],
               [pl.BlockSpec((tm, D_INNER), lambda i, j: (i, 0)), pl.BlockSpec((N_DEV, 256, 512), lambda i, j: (0, row_block, j)),
                pl.BlockSpec((tm, 512), lambda i, j: (i, j))],
               [_sds((s, D_MODEL), F32)], [pl.BlockSpec((tm, 512), lambda i, j: (i, j))], NN, epi)[0]


def _w_out_nt_block(row_block):
    return pl.BlockSpec((2, 256, D_MODEL), lambda j, i: (j, row_block, 0))


def _in_proj_bwd(name, dh, wbuf, w_index, n_k, x, g, dy, tm, dep=()):
    s = x.shape[0]

    def epi(acc, extra, outs, pids):
        dx, dg = _rms_bwd(acc, extra[0][...], extra[1][...], extra[2][...])
        outs[0][...] = dx
        outs[1][...] = dx.astype(BF)
        _accumulate(outs[2], dg, pids[0])

    row = lambda i, k: (i, 0)
    return _mm(name, (s // tm, n_k), [dh, wbuf, x, g.reshape(1, D_MODEL), dy] + list(dep),
               [pl.BlockSpec((tm, 512), lambda i, k: (i, k)), pl.BlockSpec((None, D_MODEL, 512), w_index),
                pl.BlockSpec((tm, D_MODEL), row), pl.BlockSpec((1, D_MODEL), lambda i, k: (0, 0)), pl.BlockSpec((tm, D_MODEL), row)]
               + [_ANY] * len(dep),
               [_sds((s, D_MODEL), F32), _sds((s, D_MODEL), BF), _sds((1, D_MODEL), F32)],
               [pl.BlockSpec((tm, D_MODEL), row), pl.BlockSpec((tm, D_MODEL), row), pl.BlockSpec((1, D_MODEL), lambda i, k: (0, 0))],
               NT, epi, red=1, acc_shape=(tm, D_MODEL))


def _w_out_grad(name, gated, dyb):
    s = gated.shape[0]
    return _tn(name, gated, dyb, (D_INNER, D_MODEL), (512, D_MODEL), lambda i: (i, 0), 512, D_MODEL, (4,),
               lambda i: (0, i), lambda i: (0, 0))


def _pool_bwd(dy, dyb, l, w, sv, tm, dep, early=None):
    s = dy.shape[0]
    nt = s // tm
    n = f"pool{l}b"
    h, pooled = sv["h"], sv["pooled"]
    scale = w["scale"].reshape(1, D_INNER)
    (mp,) = _mm(n + "_grp", (nt, 4), [pooled, w["g_grp"]] + dep,
                [pl.BlockSpec((tm, 512), lambda i, g: (i, g)), _grp_block()] + [_ANY] * len(dep),
                [_sds((s, D_INNER), F32)], [pl.BlockSpec((tm, 512), lambda i, g: (i, g))], NN, _store)

    def gate_bwd(part, extra, outs, pids):
        z, mpv, sc = extra[0][...].astype(F32), extra[1][...], extra[2][...]
        sz, dsz = _silu_and_grad(z)
        dm = part * sz
        outs[0][...] = (dm * sc).astype(BF)
        outs[1][...] = (part * (mpv * sc) * dsz).astype(BF)
        _accumulate(outs[2], jnp.sum(dm * mpv, axis=0, keepdims=True), pids[1])

    tile = lambda j, i: (i, j)
    dmp, dz, dscale = _mm(
        n + "_out", (4, nt), [dyb, w["g_out"], h, mp, scale],
        [pl.BlockSpec((tm, D_MODEL), lambda j, i: (i, 0)), _w_out_nt_block(0),
         pl.BlockSpec((tm, 512), lambda j, i: (i, 4 + j)), pl.BlockSpec((tm, 512), tile), pl.BlockSpec((1, 512), lambda j, i: (0, j))],
        [_sds((s, D_INNER), BF), _sds((s, 2 * D_INNER), BF), _sds((1, D_INNER), F32)],
        [pl.BlockSpec((tm, 512), tile), pl.BlockSpec((tm, 512), lambda j, i: (i, 4 + j)), pl.BlockSpec((1, 512), lambda j, i: (0, j))],
        NT, gate_bwd)
    (dpool,) = _mm(n + "_grpT", (nt, 4), [dmp, w["g_grp"]],
                   [pl.BlockSpec((tm, 512), lambda i, g: (i, g)), _grp_block()],
                   [_sds((s, D_INNER), F32)], [pl.BlockSpec((tm, 512), lambda i, g: (i, g))], NT, _store)
    dh = _pool_window_bwd(n + "_win", dpool, tm, dz)
    g_in = _tn(n + "_gin", sv["xn"], dh, (N_DEV, D_MODEL, 512), (None, D_MODEL, 512), lambda j: (j, 0, 0),
               D_MODEL, 512, (8,), lambda j: (0, 0), lambda j: (0, j))
    g_out = _w_out_grad(n + "_gout", sv["gated"], dyb)
    g_grp = _tn(n + "_ggrp", pooled, dmp, (N_DEV, 256, 512), (N_DEV, 64, 512), lambda g: (0, g, 0),
                512, 512, (4,), lambda g: (0, g), lambda g: (0, g))
    grads = dict(g_in=g_in, g_grp=g_grp, g_out=g_out.reshape(N_DEV, 256, D_MODEL))
    dep = early(grads) if early is not None else ()
    dx, dxb, dnorm = _in_proj_bwd(n + "_in", dh, w["g_in"], lambda i, k: (k, 0, 0), 8, sv["x"], w["norm"], dy, tm, dep)
    return dx, dxb, dict(grads, norm=dnorm[0], scale=dscale[0])


def _conv_in_index(i, j):
    return (j // 2, 0, j % 2)


def _conv_fwd(x, w, tm):
    s = x.shape[0]
    nt = s // tm
    xn = _rms_fwd("conv_rms", x, w["norm"], tm)
    ti = IN_PROJ_ROWS if s % IN_PROJ_ROWS == 0 else tm
    (h,) = _mm("conv_in", (s // ti, 16), [xn, w["g_in"]],
               [pl.BlockSpec((ti, D_MODEL), lambda i, j: (i, 0)), pl.BlockSpec((None, D_MODEL, 512), _conv_in_index)],
               [_sds((s, 4 * D_INNER), BF)], [pl.BlockSpec((ti, 512), lambda i, j: (i, j))], NN, _store)
    per = tm // CONV_HALO

    def body(b_ref, c_ref, h_ref, z_ref, cp_ref, hp_ref, w_ref, o_ref, e_ref):
        i = pl.program_id(0)
        ch = c_ref[...].astype(F32) * h_ref[...].astype(F32)
        e_ref[0:CONV_HALO, :] = jnp.where(i > 0, cp_ref[...].astype(F32) * hp_ref[...].astype(F32), 0.0)
        e_ref[CONV_HALO:, :] = ch
        co = (w_ref[2:3, :] * ch + w_ref[1:2, :] * e_ref[pl.ds(CONV_HALO - 1, tm), :]
              + w_ref[0:1, :] * e_ref[pl.ds(CONV_HALO - 2, tm), :])
        z = z_ref[...].astype(F32)
        o_ref[...] = ((b_ref[...].astype(F32) * co) * (z * _sig(z))).astype(BF)

    def col(q):
        return pl.BlockSpec((tm, 512), lambda i, j: (i, 4 * q + j))

    def prev(q):
        return pl.BlockSpec((CONV_HALO, 512), lambda i, j: (jnp.maximum(i * per - 1, 0), 4 * q + j))

    gated = pl.pallas_call(
        body, name="conv_mix", grid=(nt, 4),
        in_specs=[col(0), col(1), col(2), col(3), prev(1), prev(2), pl.BlockSpec((3, 512), lambda i, j: (0, j))],
        out_specs=pl.BlockSpec((tm, 512), lambda i, j: (i, j)), out_shape=_sds((s, D_INNER), BF),
        scratch_shapes=[pltpu.VMEM((CONV_HALO + tm, 512), F32)], compiler_params=_params())(h, h, h, h, h, h, w["conv_w"])
    y = _out_proj("conv_out", gated, w["g_out"], 0, x, tm)
    return y, dict(x=x, xn=xn, h=h, gated=gated)


def _conv_bwd(dy, dyb, w, sv, tm, dep):
    s = dy.shape[0]
    nt = s // tm
    h = sv["h"]
    (dg,) = _mm("convb_out", (4, nt), [dyb, w["g_out"]] + dep,
                [pl.BlockSpec((tm, D_MODEL), lambda j, i: (i, 0)), _w_out_nt_block(0)] + [_ANY] * len(dep),
                [_sds((s, D_INNER), F32)], [pl.BlockSpec((tm, 512), lambda j, i: (i, j))], NT, _store)
    per = tm // CONV_HALO
    last = s // CONV_HALO - 1

    def body(dg_ref, b_ref, c_ref, h_ref, z_ref, cp_ref, hp_ref, dgn_ref, bn_ref, zn_ref, w_ref,
             dall_ref, dw_ref, e_ref, f_ref):
        db_ref, dc_ref, dh_ref, dz_ref = (dall_ref.at[:, q * 512:(q + 1) * 512] for q in range(4))
        i = pl.program_id(1)
        w0, w1, w2 = w_ref[0:1, :], w_ref[1:2, :], w_ref[2:3, :]
        c, hh, b = c_ref[...].astype(F32), h_ref[...].astype(F32), b_ref[...].astype(F32)
        ch = c * hh
        e_ref[0:CONV_HALO, :] = jnp.where(i > 0, cp_ref[...].astype(F32) * hp_ref[...].astype(F32), 0.0)
        e_ref[CONV_HALO:, :] = ch
        ch1 = e_ref[pl.ds(CONV_HALO - 1, tm), :]
        ch2 = e_ref[pl.ds(CONV_HALO - 2, tm), :]
        co = w2 * ch + w1 * ch1 + w0 * ch2
        sz, dsz = _silu_and_grad(z_ref[...].astype(F32))
        dgv = dg_ref[...]
        dyv = dgv * sz
        dz_ref[...] = (dgv * (b * co) * dsz).astype(BF)
        db_ref[...] = (dyv * co).astype(BF)
        dco = dyv * b
        zn = zn_ref[...].astype(F32)
        f_ref[0:tm, :] = dco
        f_ref[tm:, :] = jnp.where(i < nt - 1, dgn_ref[...] * (zn * _sig(zn)) * bn_ref[...].astype(F32), 0.0)
        dch = w2 * dco + w1 * f_ref[pl.ds(1, tm), :] + w0 * f_ref[pl.ds(2, tm), :]
        dc_ref[...] = (dch * hh).astype(BF)
        dh_ref[...] = (dch * c).astype(BF)
        for tap, shifted in enumerate((ch2, ch1, ch)):
            _accumulate(dw_ref.at[tap:tap + 1, :], jnp.sum(dco * shifted, axis=0, keepdims=True), i)

    def col(q):
        return pl.BlockSpec((tm, 512), lambda j, i: (i, 4 * q + j))

    def prev(q):
        return pl.BlockSpec((CONV_HALO, 512), lambda j, i: (jnp.maximum(i * per - 1, 0), 4 * q + j))

    def nxt(q):
        return pl.BlockSpec((CONV_HALO, 512), lambda j, i: (jnp.minimum((i + 1) * per, last), 4 * q + j))

    tile = pl.BlockSpec((tm, 512), lambda j, i: (i, j))
    wspec = pl.BlockSpec((3, 512), lambda j, i: (0, j))
    dh, dw = pl.pallas_call(
        body, name="convb_mix", grid=(4, nt),
        in_specs=[tile, col(0), col(1), col(2), col(3), prev(1), prev(2), nxt(0), nxt(0), nxt(3), wspec],
        out_specs=[pl.BlockSpec((tm, D_INNER), lambda j, i: (i, j)), wspec],
        out_shape=[_sds((s, 4 * D_INNER), BF), _sds((3, D_INNER), F32)],
        scratch_shapes=[pltpu.VMEM((CONV_HALO + tm, 512), F32)] * 2, compiler_params=_params(),
    )(dg, h, h, h, h, h, h, dg, h, h, w["conv_w"])

    def w_block(kp):
        k = 4 * (kp % 4) + kp // 4
        return (k // 2, 0, k % 2)

    dx, dxb, dnorm = _in_proj_bwd("convb_in", dh, w["g_in"], lambda i, kp: w_block(kp), 16, sv["x"], w["norm"], dy, tm)
    g_in = _tn("convb_gin", sv["xn"], dh, (N_DEV, D_MODEL, D_MODEL), (None, D_MODEL, 512), lambda j: (j // 2, 0, j % 2),
               D_MODEL, 512, (16,), lambda j: (0, 0), lambda j: (0, 4 * (j % 4) + j // 4))
    g_out = _w_out_grad("convb_gout", sv["gated"], dyb)
    return dx, dxb, dict(g_in=g_in, g_out=g_out.reshape(N_DEV, 256, D_MODEL), norm=dnorm[0], conv_w=dw)


def _attn_tiles(s):
    t = min(ATTN_TILE, s)
    return t, s // t


def _causal_keep(t, keys_on_rows):
    r = lax.broadcasted_iota(jnp.int32, (t, t), 0)
    c = lax.broadcasted_iota(jnp.int32, (t, t), 1)
    return (r <= c) if keys_on_rows else (c <= r)


def _mla_fwd(x, w, rope, tm):
    s = x.shape[0]
    nt = s // tm
    cosf, sinf, perm = rope
    xn = _rms_fwd("mla_rms", x, w["norm"], tm)

    def in_body(xn_ref, wq_ref, wkv_ref, wkr_ref, wz_ref, gq_ref, gkv_ref, cos_ref, sin_ref, p_ref,
                ql_ref, kvl_ref, qn_ref, kvn_ref, krr_ref, z_ref):
        xv = xn_ref[...]
        ql = _dot(xv, wq_ref[...], NN)
        kvl = _dot(xv, wkv_ref[...], NN)
        ql_ref[...] = ql
        kvl_ref[...] = kvl
        rq = lax.rsqrt(jnp.mean(ql * ql, axis=-1, keepdims=True) + NORM_EPS)
        qn_ref[...] = ((ql * rq) * gq_ref[...]).astype(BF)
        rkv = lax.rsqrt(jnp.mean(kvl * kvl, axis=-1, keepdims=True) + NORM_EPS)
        kvn_ref[...] = ((kvl * rkv) * gkv_ref[...]).astype(BF)
        kr = _dot(xv, wkr_ref[...], NN)
        krr_ref[...] = _rope_fwd(kr, cos_ref[...], sin_ref[...], p_ref[...]).astype(BF)
        z_ref[...] = _dot(xv, wz_ref[...], NN).astype(BF)

    def full(a):
        return pl.BlockSpec(a.shape, lambda i: (0,) * a.ndim)

    def rows(c):
        return pl.BlockSpec((tm, c), lambda i: (i, 0))

    gq, gkv = w["q_norm"].reshape(1, Q_RANK), w["kv_norm"].reshape(1, KV_RANK)
    q_lat, kv_lat, qn, kvn, krr, z = pl.pallas_call(
        in_body, name="mla_in", grid=(nt,),
        in_specs=[rows(D_MODEL), full(w["w_q"]), full(w["w_kv"]), full(w["w_kr"]), full(w["w_z"]), full(gq), full(gkv),
                  rows(QK_ROPE), rows(QK_ROPE), full(perm)],
        out_specs=[rows(Q_RANK), rows(KV_RANK), rows(Q_RANK), rows(KV_RANK), rows(QK_ROPE), rows(D_INNER)],
        out_shape=[_sds((s, Q_RANK), F32), _sds((s, KV_RANK), F32), _sds((s, Q_RANK), BF), _sds((s, KV_RANK), BF),
                   _sds((s, QK_ROPE), BF), _sds((s, D_INNER), BF)],
        compiler_params=_params())(xn, w["w_q"], w["w_kv"], w["w_kr"], w["w_z"], gq, gkv, cosf, sinf, perm)

    def q_epi(part, extra, outs, pids):
        outs[0][:, 0:QK_NOPE] = (part[:, 0:QK_NOPE] * Q_PRESCALE).astype(BF)
        roped = _rope_fwd(part[:, QK_NOPE:QK_DIM], extra[0][...], extra[1][...], extra[2][...])
        outs[0][:, QK_NOPE:QK_DIM] = (roped * Q_PRESCALE).astype(BF)

    rope_row = pl.BlockSpec((tm, QK_ROPE), lambda h, i: (i, 0))
    (q,) = _mm("mla_qup", (N_HEADS, nt), [qn, w["w_qh"], cosf, sinf, perm],
               [pl.BlockSpec((tm, Q_RANK), lambda h, i: (i, 0)), pl.BlockSpec((None, Q_RANK, QK_DIM), lambda h, i: (h, 0, 0)),
                rope_row, rope_row, pl.BlockSpec((QK_ROPE, QK_ROPE), lambda h, i: (0, 0))],
               [_sds((N_HEADS, s, QK_DIM), BF)], [pl.BlockSpec((None, tm, QK_DIM), lambda h, i: (h, i, 0))], NN, q_epi)

    def kv_epi(part, extra, outs, pids):
        outs[0][:, 0:QK_NOPE] = part[:, 0:QK_NOPE].astype(BF)
        outs[0][:, QK_NOPE:QK_DIM] = extra[0][...]
        outs[1][...] = part[:, QK_NOPE:].astype(BF)

    k, v = _mm("mla_kvup", (N_HEADS, nt), [kvn, w["g512"], krr],
               [pl.BlockSpec((tm, KV_RANK), lambda h, i: (i, 0)),
                pl.BlockSpec((None, KV_RANK, 256), lambda h, i: (h // 2, 0, h % 2)), rope_row],
               [_sds((N_HEADS, s, QK_DIM), BF), _sds((N_HEADS, s, V_DIM), BF)],
               [pl.BlockSpec((None, tm, QK_DIM), lambda h, i: (h, i, 0)), pl.BlockSpec((None, tm, V_DIM), lambda h, i: (h, i, 0))],
               NN, kv_epi)

    t, nq = _attn_tiles(s)

    def attn_body(q_ref, k_ref, v_ref, z_ref, o_ref, g_ref, lse_ref):
        i = pl.program_id(1)
        qv = q_ref[...]

        def block(j, carry, masked):
            m, lsum, acc = carry
            start = pl.multiple_of(j * t, t)
            sc = _dot(qv, k_ref[pl.ds(start, t), :], NT)
            if masked:
                sc = jnp.where(_causal_keep(t, False), sc, NEG_BIG)
            mn = jnp.maximum(m, jnp.max(sc, axis=-1, keepdims=True))
            alpha = jnp.exp2(m - mn)
            p = jnp.exp2(sc - mn)
            lsum = alpha * lsum + jnp.sum(p, axis=-1, keepdims=True)
            acc = alpha * acc + _dot(p.astype(BF), v_ref[pl.ds(start, t), :], NN)
            return mn, lsum, acc

        init = (jnp.full((t, 1), NEG_BIG, F32), jnp.zeros((t, 1), F32), jnp.zeros((t, V_DIM), F32))
        carry = lax.fori_loop(0, i, lambda j, c: block(j, c, False), init)
        m, lsum, acc = block(i, carry, True)
        o = acc / lsum
        z = z_ref[...].astype(F32)
        o_ref[...] = o
        g_ref[...] = (o * (z * _sig(z))).astype(BF)
        lse_ref[...] = m + jnp.log(lsum) * LOG2_E

    head_col = pl.BlockSpec((t, V_DIM), lambda h, i: (i, h))
    o, gated, lse = pl.pallas_call(
        attn_body, name="mla_attn", grid=(N_HEADS, nq),
        in_specs=[pl.BlockSpec((None, t, QK_DIM), lambda h, i: (h, i, 0)), pl.BlockSpec((None, s, QK_DIM), lambda h, i: (h, 0, 0)),
                  pl.BlockSpec((None, s, V_DIM), lambda h, i: (h, 0, 0)), head_col],
        out_specs=[head_col, head_col, pl.BlockSpec((None, t, 1), lambda h, i: (h, i, 0))],
        out_shape=[_sds((s, D_INNER), F32), _sds((s, D_INNER), BF), _sds((N_HEADS, s, 1), F32)],
        compiler_params=_params())(q, k, v, z)
    y = _out_proj("mla_out", gated, w["g1024"], 0, x, tm)
    return y, dict(x=x, xn=xn, q_lat=q_lat, kv_lat=kv_lat, qn=qn, kvn=kvn, z=z, q=q, k=k, v=v, o=o, lse=lse, gated=gated)


def _mla_bwd(dy, dyb, w, sv, rope, tm, dep):
    s = dy.shape[0]
    nt = s // tm
    cosf, sinf, perm = rope
    t, nq = _attn_tiles(s)
    q, k, v, lse = sv["q"], sv["k"], sv["v"], sv["lse"]

    def gate_bwd(part, extra, outs, pids):
        z, o = extra[0][...].astype(F32), extra[1][...]
        sz, dsz = _silu_and_grad(z)
        do = part * sz
        outs[0][...] = do.astype(BF)
        outs[1][...] = (part * o * dsz).astype(BF)
        prod = do * o
        for hh in range(4):
            outs[2][hh] = jnp.sum(prod[:, hh * V_DIM:(hh + 1) * V_DIM], axis=-1, keepdims=True)

    tile = lambda j, i: (i, j)
    dob, dz, delta = _mm(
        "mlab_out", (4, nt), [dyb, w["g1024"], sv["z"], sv["o"]] + dep,
        [pl.BlockSpec((tm, D_MODEL), lambda j, i: (i, 0)), _w_out_nt_block(0),
         pl.BlockSpec((tm, 512), tile), pl.BlockSpec((tm, 512), tile)] + [_ANY] * len(dep),
        [_sds((s, D_INNER), BF), _sds((s, D_INNER), BF), _sds((N_HEADS, s, 1), F32)],
        [pl.BlockSpec((tm, 512), tile), pl.BlockSpec((tm, 512), tile), pl.BlockSpec((4, tm, 1), lambda j, i: (j, i, 0))],
        NT, gate_bwd)

    def attn_bwd_body(k_ref, v_ref, q_ref, do_ref, lse_ref, dl_ref, cos_ref, sin_ref, p_ref, dkv_ref, dkr_ref, dq_ref, dq_acc):
        j = pl.program_id(1)
        kb, vb = k_ref[...], v_ref[...]

        @pl.when(j == 0)
        def _():
            dq_acc[...] = jnp.zeros(dq_acc.shape, F32)

        def block(i, carry, masked):
            dk, dv = carry
            rows = pl.ds(pl.multiple_of(i * t, t), t)
            qb, dob_ = q_ref[rows, :], do_ref[rows, :]
            st = _dot(kb, qb, NT)
            if masked:
                st = jnp.where(_causal_keep(t, True), st, NEG_BIG)
            pt = jnp.exp2(st - lse_ref[i])
            dv = dv + _dot(pt.astype(BF), dob_, NN)
            dst = (pt * (_dot(vb, dob_, NT) - dl_ref[i])).astype(BF)
            dk = dk + _dot(dst, qb, NN)
            dq_acc[rows, :] += _dot(dst, kb, TN)
            return dk, dv

        carry = block(j, (jnp.zeros((t, QK_DIM), F32), jnp.zeros((t, V_DIM), F32)), True)
        dk, dv = lax.fori_loop(j + 1, nq, lambda i, c: block(i, c, False), carry)
        dk = dk * LN_2
        dkv_ref[:, 0:QK_NOPE] = dk[:, 0:QK_NOPE].astype(BF)
        dkv_ref[:, QK_NOPE:] = dv.astype(BF)
        dkr_ref[...] = dk[:, QK_NOPE:]

        @pl.when(j == nq - 1)
        def _():
            for c in range(nq):
                rows = slice(c * t, (c + 1) * t)
                dq = dq_acc[rows, :] * ATTN_SCALE
                dq_ref[rows, 0:QK_NOPE] = dq[:, 0:QK_NOPE].astype(BF)
                dq_ref[rows, QK_NOPE:] = _rope_bwd(dq[:, QK_NOPE:], cos_ref[rows, :], sin_ref[rows, :], p_ref[...]).astype(BF)

    row_stats = pl.BlockSpec((None, nq, 1, t), lambda h, j: (h, 0, 0, 0))
    seq_rope = pl.BlockSpec((s, QK_ROPE), lambda h, j: (0, 0))
    head_seq = pl.BlockSpec((None, s, QK_DIM), lambda h, j: (h, 0, 0))
    dkv, dkr_h, dq = pl.pallas_call(
        attn_bwd_body, name="mlab_attn", grid=(N_HEADS, nq),
        in_specs=[pl.BlockSpec((None, t, QK_DIM), lambda h, j: (h, j, 0)), pl.BlockSpec((None, t, V_DIM), lambda h, j: (h, j, 0)),
                  head_seq, pl.BlockSpec((s, V_DIM), lambda h, j: (0, h)), row_stats, row_stats, seq_rope, seq_rope,
                  pl.BlockSpec((QK_ROPE, QK_ROPE), lambda h, j: (0, 0))],
        out_specs=[pl.BlockSpec((t, 2 * V_DIM), lambda h, j: (j, h)), pl.BlockSpec((None, t, QK_ROPE), lambda h, j: (h, j, 0)), head_seq],
        out_shape=[_sds((s, N_HEADS * 2 * V_DIM), BF), _sds((N_HEADS, s, QK_ROPE), F32), _sds((N_HEADS, s, QK_DIM), BF)],
        scratch_shapes=[pltpu.VMEM((s, QK_DIM), F32)],
        compiler_params=_params())(k, v, q, dob, lse.reshape(N_HEADS, nq, 1, t), delta.reshape(N_HEADS, nq, 1, t), cosf, sinf, perm)

    def dkr_body(d_ref, cos_ref, sin_ref, p_ref, o_ref):
        tot = d_ref[0]
        for hh in range(1, N_HEADS):
            tot = tot + d_ref[hh]
        o_ref[...] = _rope_bwd(tot, cos_ref[...], sin_ref[...], p_ref[...]).astype(BF)

    r64 = pl.BlockSpec((tm, QK_ROPE), lambda i: (i, 0))
    dkr = pl.pallas_call(
        dkr_body, name="mlab_dkr", grid=(nt,),
        in_specs=[pl.BlockSpec((N_HEADS, tm, QK_ROPE), lambda i: (0, i, 0)), r64, r64, pl.BlockSpec((QK_ROPE, QK_ROPE), lambda i: (0, 0))],
        out_specs=r64, out_shape=_sds((s, QK_ROPE), BF), compiler_params=_params())(dkr_h, cosf, sinf, perm)

    def lat_epi(acc, extra, outs, pids):
        dx, dg = _rms_bwd(acc, extra[0][...], extra[1][...], None)
        outs[0][...] = dx.astype(BF)
        _accumulate(outs[1], dg, pids[0])

    def lat_bwd(name, a, a_spec, b, b_spec, n_k, lat, g, rank):
        row = lambda i, k: (i, 0)
        one = lambda i, k: (0, 0)
        return _mm(name, (nt, n_k), [a, b, lat, g.reshape(1, rank)],
                   [a_spec, b_spec, pl.BlockSpec((tm, rank), row), pl.BlockSpec((1, rank), one)],
                   [_sds((s, rank), BF), _sds((1, rank), F32)], [pl.BlockSpec((tm, rank), row), pl.BlockSpec((1, rank), one)],
                   NT, lat_epi, red=1, acc_shape=(tm, rank))

    d_ql, g_qnorm = lat_bwd("mlab_qup", dq, pl.BlockSpec((None, tm, QK_DIM), lambda i, h: (h, i, 0)),
                            w["w_qh"], pl.BlockSpec((None, Q_RANK, QK_DIM), lambda i, h: (h, 0, 0)), N_HEADS,
                            sv["q_lat"], w["q_norm"], Q_RANK)
    d_kvl, g_kvnorm = lat_bwd("mlab_kvup", dkv, pl.BlockSpec((tm, 512), lambda i, kk: (i, kk)),
                              w["g512"], pl.BlockSpec((None, KV_RANK, 512), lambda i, kk: (kk, 0, 0)), N_DEV,
                              sv["kv_lat"], w["kv_norm"], KV_RANK)

    def in_bwd(dql_ref, dkvl_ref, dkr_ref, dz_ref, wq_ref, wkv_ref, wkr_ref, wz_ref, x_ref, g_ref, dy_ref, dx_ref, dxb_ref, dg_ref):
        acc = (_dot(dql_ref[...], wq_ref[...], NT) + _dot(dkvl_ref[...], wkv_ref[...], NT)
               + _dot(dkr_ref[...], wkr_ref[...], NT) + _dot(dz_ref[...], wz_ref[...], NT))
        dx, dg = _rms_bwd(acc, x_ref[...], g_ref[...], dy_ref[...])
        dx_ref[...] = dx
        dxb_ref[...] = dx.astype(BF)
        _accumulate(dg_ref, dg, pl.program_id(0))

    def full(a):
        return pl.BlockSpec(a.shape, lambda i: (0,) * a.ndim)

    def rows(c):
        return pl.BlockSpec((tm, c), lambda i: (i, 0))

    gm = w["norm"].reshape(1, D_MODEL)
    dx, dxb, g_norm = pl.pallas_call(
        in_bwd, name="mlab_in", grid=(nt,),
        in_specs=[rows(Q_RANK), rows(KV_RANK), rows(QK_ROPE), rows(D_INNER), full(w["w_q"]), full(w["w_kv"]), full(w["w_kr"]),
                  full(w["w_z"]), rows(D_MODEL), full(gm), rows(D_MODEL)],
        out_specs=[rows(D_MODEL), rows(D_MODEL), full(gm)],
        out_shape=[_sds((s, D_MODEL), F32), _sds((s, D_MODEL), BF), _sds((1, D_MODEL), F32)],
        compiler_params=_params())(d_ql, d_kvl, dkr, dz, w["w_q"], w["w_kv"], w["w_kr"], w["w_z"], sv["x"], gm, dy)

    xn = sv["xn"]
    one = lambda j: (0, 0)
    g_q = _tn("mlab_gq", xn, d_ql, (D_MODEL, Q_RANK), (D_MODEL, Q_RANK), one, D_MODEL, Q_RANK, (1,), one, one)
    g_kv = _tn("mlab_gkv", xn, d_kvl, (D_MODEL, KV_RANK), (D_MODEL, KV_RANK), one, D_MODEL, KV_RANK, (1,), one, one)
    g_kr = _tn("mlab_gkr", xn, dkr, (D_MODEL, QK_ROPE), (D_MODEL, QK_ROPE), one, D_MODEL, QK_ROPE, (1,), one, one)
    g_z = _tn("mlab_gz", xn, dz, (D_MODEL, D_INNER), (D_MODEL, 512), lambda j: (0, j), D_MODEL, 512, (4,), one, lambda j: (0, j))
    g_in = jnp.concatenate([g_q, g_kv, g_kr, g_z], axis=1)
    g_qh = _tn("mlab_gqup", sv["qn"], dq, (N_HEADS, Q_RANK, QK_DIM), (None, Q_RANK, QK_DIM), lambda h: (h, 0, 0),
               Q_RANK, QK_DIM, (N_HEADS,), lambda h: (0, 0), lambda h: (h, 0, 0))
    g_kvup = _tn("mlab_gkvup", sv["kvn"], dkv, (N_DEV, KV_RANK, 512), (None, KV_RANK, 512), lambda j: (j, 0, 0),
                 KV_RANK, 512, (N_DEV,), lambda j: (0, 0), lambda j: (0, j))
    g_out = _w_out_grad("mlab_gout", sv["gated"], dyb)
    s384 = g_qh.reshape(N_DEV, 2, Q_RANK, QK_DIM).transpose(0, 2, 1, 3).reshape(N_DEV, Q_RANK, 2 * QK_DIM)
    s344 = g_in.reshape(D_MODEL, N_DEV, 344).transpose(1, 0, 2)
    return dx, dxb, dict(s344=s344, s384=s384, s512=g_kvup, s1024=g_out.reshape(N_DEV, 256, D_MODEL),
                         norm=g_norm[0], q_norm=g_qnorm[0], kv_norm=g_kvnorm[0])


def _loss_head(x, g, target, tm):
    s, d = x.shape

    def body(x_ref, g_ref, t_ref, dx_ref, dxb_ref, dg_ref, loss_ref):
        i = pl.program_id(0)
        xv, gv = x_ref[...], g_ref[...]
        r = lax.rsqrt(jnp.mean(xv * xv, axis=-1, keepdims=True) + NORM_EPS)
        err = (xv * r) * gv - t_ref[...]
        part = 0.5 * jnp.sum(jnp.mean(err * err, axis=-1, keepdims=True), axis=0, keepdims=True)
        dx, dg = _rms_bwd(err * (1.0 / d), xv, gv, None)
        dx_ref[...] = dx
        dxb_ref[...] = dx.astype(BF)
        _accumulate(dg_ref, dg, i)
        _accumulate(loss_ref, jnp.broadcast_to(part, loss_ref.shape), i)

    row = pl.BlockSpec((tm, d), lambda i: (i, 0))
    one = pl.BlockSpec((1, d), lambda i: (0, 0))
    return pl.pallas_call(
        body, name="loss_head", grid=(s // tm,), in_specs=[row, one, row],
        out_specs=[row, row, one, pl.BlockSpec((8, 128), lambda i: (0, 0))],
        out_shape=[_sds((s, d), F32), _sds((s, d), BF), _sds((1, d), F32), _sds((8, 128), F32)],
        compiler_params=_params())(x, g.reshape(1, d), target)


def _rope_tables(pos):
    inv_freq = ROPE_BASE ** (-jnp.arange(0, QK_ROPE, 2, dtype=F32) / QK_ROPE)
    ang = pos.astype(F32)[:, None] * inv_freq
    cos, sin = jnp.cos(ang), jnp.sin(ang)
    idx = jnp.arange(QK_ROPE)
    perm = (idx[:, None] == (idx[None, :] + QK_ROPE // 2) % QK_ROPE).astype(F32)
    return jnp.concatenate([cos, cos], axis=1), jnp.concatenate([-sin, sin], axis=1), perm


def _local_step(x, pos, target, final_norm, get_w, put_g):
    s = x.shape[0]
    tm = min(512, s)
    rope = _rope_tables(pos)
    w0 = get_w(0, [])
    x1, sv0 = _pool_fwd(x, 0, w0, tm)
    w1 = get_w(1, [x1])
    x2, sv1 = _conv_fwd(x1, w1, tm)
    w2 = get_w(2, [x2])
    x3, sv2 = _mla_fwd(x2, w2, rope, tm)
    w3 = get_w(3, [x3])
    x4, sv3 = _pool_fwd(x3, 1, w3, tm)
    d4, d4b, g_final, loss = _loss_head(x4, final_norm, target, tm)
    d3, d3b, gp1 = _pool_bwd(d4, d4b, 1, sv3["w"], sv3, tm, [])
    dep = put_g(3, gp1)
    d2, d2b, gm = _mla_bwd(d3, d3b, w2, sv2, rope, tm, dep)
    dep = put_g(2, gm)
    d1, d1b, gc = _conv_bwd(d2, d2b, w1, sv1, tm, dep)
    dep = put_g(1, gc)
    d0, _, gp0 = _pool_bwd(d1, d1b, 0, sv0["w"], sv0, tm, dep, early=lambda big: put_g(0, big))
    put_g(4, dict(gp0, final_norm=g_final[0]))
    return loss[0, 0], d0


def _pack_groups(p):
    bf = lambda a: a.astype(BF)
    grp = lambda l: bf(p["pool_w_grp"][l].reshape(4 * 64, POOL_GROUP))
    return [[bf(p["pool_w_in"][0]), _pack_small(p, SMALL_ROWS_AG)],
            [grp(0), bf(p["pool_w_out"][0])],
            [bf(p["conv_w_in"][0]), bf(p["conv_w_out"][0])],
            [bf(p[k][0]) for k in ("mla_w_in", "mla_w_q_up", "mla_w_kv_up", "mla_w_out")],
            [bf(p["pool_w_in"][1]), grp(1), bf(p["pool_w_out"][1])]]


_SMALL_SHARDED = ("pool_norm", "pool_scale", "mla_norm", "mla_q_norm", "mla_kv_norm", "conv_w")
_SMALL_REPLICATED = ("conv_norm", "final_norm")


def _pack_small(p, rows, with_replicated=False):
    parts = [p[k].reshape(-1) for k in _SMALL_SHARDED]
    if with_replicated:
        parts += [p[k].reshape(-1) for k in _SMALL_REPLICATED]
    flat = jnp.concatenate(parts)
    return jnp.pad(flat, (0, rows * 128 - flat.shape[0])).reshape(rows, 128)


_SMALL_SHARD_SHAPES = dict(pool_norm=(2, 128), pool_scale=(2, 256), mla_norm=(1, 128), mla_q_norm=(1, 48),
                           mla_kv_norm=(1, 32), conv_w=(1, 3, 256), conv_norm=(1, 1024), final_norm=(1024,))


def _unpack_small(buf, with_replicated=False):
    flat = buf.reshape(-1)
    out, off = {}, 0
    for k in _SMALL_SHARDED + (_SMALL_REPLICATED if with_replicated else ()):
        shp = _SMALL_SHARD_SHAPES[k]
        n = 1
        for d in shp:
            n *= d
        out[k] = flat[off:off + n].reshape(shp)
        off += n
    return out


def _small_views(gsmall):
    flat = gsmall.reshape(N_DEV, -1)

    def cols(off, rows, width):
        return flat[:, off:off + rows * width].reshape(N_DEV, rows, width).transpose(1, 0, 2).reshape(rows, N_DEV * width)

    return dict(pool_norm=cols(0, 2, 128), pool_scale=cols(256, 2, 256), mla_norm=cols(768, 1, 128)[0],
                q_norm=cols(896, 1, 48)[0], kv_norm=cols(944, 1, 32)[0], conv_w=cols(976, 3, 256))


def _layer_weights(layer, bufs, small, conv_norm):
    if layer in (0, 3):
        l = 0 if layer == 0 else 1
        return dict(g_in=bufs[0], norm=small["pool_norm"][l], scale=small["pool_scale"][l])
    if layer == 1:
        return dict(g_in=bufs[0], g_out=bufs[1], norm=conv_norm.reshape(D_MODEL), conv_w=small["conv_w"])
    g344, g384, g512, g1024 = bufs
    w_in = g344.transpose(1, 0, 2).reshape(D_MODEL, N_DEV * 344)
    return dict(
        g512=g512, g1024=g1024,
        w_q=w_in[:, :Q_RANK], w_kv=w_in[:, Q_RANK:Q_RANK + KV_RANK],
        w_kr=w_in[:, Q_RANK + KV_RANK:Q_RANK + KV_RANK + QK_ROPE], w_z=w_in[:, Q_RANK + KV_RANK + QK_ROPE:],
        w_qh=g384.reshape(N_DEV, Q_RANK, 2, QK_DIM).transpose(0, 2, 1, 3).reshape(N_HEADS, Q_RANK, QK_DIM),
        norm=small["mla_norm"], q_norm=small["q_norm"], kv_norm=small["kv_norm"])


def _grad_group(layer, g):
    if layer in (0, 3):
        return [g["g_in"], g["g_grp"], g["g_out"]]
    if layer == 1:
        return [g["g_in"], g["g_out"]]
    return [g["s344"], g["s384"], g["s512"], g["s1024"]]


def _pack_small_grads(g):
    def split(a, rows, width):
        return a.reshape(rows, N_DEV, width).transpose(1, 0, 2).reshape(N_DEV, rows * width)

    rep = lambda a: jnp.broadcast_to(a.reshape(1, -1), (N_DEV, a.size))
    flat = jnp.concatenate([
        split(jnp.stack([g[0]["norm"], g[3]["norm"]]), 2, 128), split(jnp.stack([g[0]["scale"], g[3]["scale"]]), 2, 256),
        split(g[2]["norm"], 1, 128), split(g[2]["q_norm"], 1, 48), split(g[2]["kv_norm"], 1, 32), split(g[1]["conv_w"], 3, 256),
        rep(g[1]["norm"]), rep(g[0]["final_norm"])], axis=1)
    return jnp.pad(flat, ((0, 0), (0, SMALL_ROWS_RS * 128 - flat.shape[1]))).reshape(N_DEV, SMALL_ROWS_RS, 128)


def _peers(x, y, c):
    for k in range(1, N_DEV):
        px = 1 - x if k & 4 else x
        py = 1 - y if k & 2 else y
        pc = 1 - c if k & 1 else c
        yield k - 1, (px, py, pc), 4 * px + 2 * py + pc


def _remote_copies(srcs, lands, send_sems, recv_sems, gather):
    x, y, c = lax.axis_index("x"), lax.axis_index("y"), lax.axis_index("c")
    me = 4 * x + 2 * y + c
    copies = []
    for k, peer, pidx in _peers(x, y, c):
        for a, (src, land) in enumerate(zip(srcs, lands)):
            copies.append(pltpu.make_async_remote_copy(
                src_ref=src if gather else src.at[pidx], dst_ref=land.at[me],
                send_sem=send_sems.at[a * (N_DEV - 1) + k], recv_sem=recv_sems.at[a * (N_DEV - 1) + k],
                device_id=peer, device_id_type=pl.DeviceIdType.MESH))
    return copies


_HBM = pl.BlockSpec(memory_space=pltpu.HBM)
_SEM = pl.BlockSpec(memory_space=pltpu.SEMAPHORE)
_EFFECT = pltpu.SideEffectType.DATAFLOW_SIDE_EFFECTING


def _own_slabs(name, arrays, gather, dep):
    n, nd = len(arrays), len(dep)
    me = (4 * lax.axis_index("x") + 2 * lax.axis_index("y") + lax.axis_index("c")).astype(jnp.int32).reshape(1)

    def body(me_ref, *refs):
        for a in range(n):
            refs[n + nd + a][...] = refs[a][...]

    def slab(shape):
        return pl.BlockSpec((None,) + tuple(shape), lambda i, me_ref: (me_ref[0],) + (0,) * len(shape))

    def whole(shape):
        return pl.BlockSpec(tuple(shape), lambda i, me_ref: (0,) * len(shape))

    outs = [_sds(((N_DEV,) + a.shape) if gather else a.shape, a.dtype) for a in arrays]
    grid_spec = pltpu.PrefetchScalarGridSpec(
        num_scalar_prefetch=1, grid=(1,),
        in_specs=[whole(a.shape) if gather else slab(a.shape[1:]) for a in arrays] + [_ANY] * nd,
        out_specs=[slab(o.shape[1:]) for o in outs])
    return pl.pallas_call(body, name=name, grid_spec=grid_spec, out_shape=outs, compiler_params=_params())(me, *arrays, *dep)


def _exchange_start(name, arrays, lands, gather):
    n = len(arrays)

    def body(*refs):
        srcs, lnds, send_sems, recv_sems, token = refs[:n], refs[n:2 * n], refs[2 * n], refs[2 * n + 1], refs[-1]
        for cp in _remote_copies(srcs, lnds, send_sems, recv_sems, gather):
            cp.start()
        token[...] = jnp.zeros(token.shape, F32)

    sems = pltpu.SemaphoreType.DMA((n * (N_DEV - 1),))
    thru = [pltpu.HBM(a.shape, a.dtype) for a in list(arrays) + list(lands)]
    res = pl.pallas_call(
        body, name=name, in_specs=[_HBM] * (2 * n),
        out_specs=[_SEM, _SEM] + [_HBM] * (2 * n) + [pl.BlockSpec(memory_space=pltpu.VMEM)],
        out_shape=[sems, sems] + thru + [_sds((8, 128), F32)],
        input_output_aliases={i: 2 + i for i in range(2 * n)},
        compiler_params=pltpu.CompilerParams(has_side_effects=_EFFECT),
    )(*[pltpu.with_memory_space_constraint(a, pltpu.HBM) for a in list(arrays) + list(lands)])
    return res[0], res[1], list(res[2:2 + n]), list(res[2 + n:2 + 2 * n]), res[-1]


def _exchange_wait(name, send_sems, recv_sems, arrays, lands, after, gather):
    n = len(arrays)
    n_after = len(after)

    def body(*refs):
        srcs, lnds = refs[:n], refs[n:2 * n]
        copies = _remote_copies(srcs, lnds, refs[2 * n], refs[2 * n + 1], gather)
        for cp in copies:
            cp.wait_send()
        for cp in copies:
            cp.wait_recv()

    thru = [pltpu.HBM(a.shape, a.dtype) for a in list(arrays) + list(lands)]
    res = pl.pallas_call(
        body, name=name, in_specs=[_HBM] * (2 * n) + [_SEM, _SEM] + [pl.BlockSpec(memory_space=pl.ANY)] * n_after,
        out_specs=[_HBM] * (2 * n), out_shape=thru, input_output_aliases={i: i for i in range(2 * n)},
        compiler_params=pltpu.CompilerParams(has_side_effects=_EFFECT),
    )(*arrays, *lands, send_sems, recv_sems, *after)
    return list(res[n:])


def _adamw_math(g, w, m, v):
    m = ADAM_B1 * m + (1.0 - ADAM_B1) * g
    v = ADAM_B2 * v + (1.0 - ADAM_B2) * (g * g)
    m_hat = m / (1.0 - ADAM_B1 ** ADAM_STEP)
    v_hat = v / (1.0 - ADAM_B2 ** ADAM_STEP)
    delta = -ADAM_LR * (m_hat / (jnp.sqrt(v_hat) + ADAM_EPS) + ADAM_WD * w)
    return delta, m, v


def _sum_adamw(name, recv, row_off, w, m, v, tr, layer=0):
    width = recv.shape[-1]
    w2, m2, v2 = (a.reshape(a.shape[0], -1, width) for a in (w, m, v))
    rows = w2.shape[1]
    base = row_off // tr

    def body(r_ref, w_ref, m_ref, v_ref, g_ref, d_ref, mo_ref, vo_ref):
        g = r_ref[0].astype(F32)
        for src in range(1, N_DEV):
            g = g + r_ref[src].astype(F32)
        delta, mn, vn = _adamw_math(g, w_ref[...], m_ref[...], v_ref[...])
        g_ref[...] = g
        d_ref[...] = delta
        mo_ref[...] = mn
        vo_ref[...] = vn

    blk = pl.BlockSpec((tr, width), lambda i: (i, 0))
    wblk = pl.BlockSpec((None, tr, width), lambda i: (layer, i, 0))
    return pl.pallas_call(
        body, name=name, grid=(rows // tr,),
        in_specs=[pl.BlockSpec((N_DEV, tr, width), lambda i: (0, base + i, 0)), wblk, wblk, wblk],
        out_specs=[blk] * 4, out_shape=[_sds((rows, width), F32)] * 4, compiler_params=_params())(recv, w2, m2, v2)


_WEIGHTS = ("pool_norm", "pool_w_in", "pool_w_grp", "pool_scale", "pool_w_out", "conv_norm", "conv_w_in", "conv_w", "conv_w_out",
            "mla_norm", "mla_w_in", "mla_q_norm", "mla_w_q_up", "mla_kv_norm", "mla_w_kv_up", "mla_w_out", "final_norm")


def _step(x, positions, loss_target, p, m, v):
    gathers, tokens, dep = [], [], []
    for group, arrays in enumerate(_pack_groups(p)):
        lands = _own_slabs(f"gather{group}_own", arrays, True, dep)
        ssem, rsem, arrays, lands, token = _exchange_start(f"gather{group}_start", arrays, lands, True)
        gathers.append((ssem, rsem, arrays, lands))
        tokens.append(token)
        dep = [token]
    state = {}

    def wait_group(group, after):
        return _exchange_wait(f"gather{group}_wait", *gathers[group], after, True)

    def get_w(layer, after):
        if layer == 0:
            bufs = wait_group(0, list(tokens))
            state["small"] = _small_views(bufs[1])
            rest = lambda later: dict(zip(("g_grp", "g_out"), wait_group(1, later)))
        else:
            bufs = wait_group(layer + 1, after)
            rest = lambda later: dict(g_grp=bufs[1], g_out=bufs[2])
        w = _layer_weights(layer, bufs, state["small"], p["conv_norm"])
        if layer in (0, 3):
            w["rest"] = rest
        return w

    scatters, small_grads = {}, {}

    def put_g(layer, g):
        if layer == 4:
            small_grads[0] = g
            arrays = [_pack_small_grads(small_grads)]
        else:
            small_grads[layer] = g
            arrays = _grad_group(layer, g)
        lands = _own_slabs(f"scatter{layer}_own", arrays, False, [])
        ssem, rsem, arrays, lands, token = _exchange_start(f"scatter{layer}_start", arrays, lands, False)
        scatters[layer] = (ssem, rsem, arrays, lands)
        tokens.append(token)
        return [token]

    loss, grad_x = _local_step(x[0], positions[0], loss_target[0], p["final_norm"], get_w, put_g)

    def adam(name, recv, row_off, key, tr, layer=0):
        return _sum_adamw("adam_" + name, recv, row_off, p[key], m[key], v[key], tr, layer)

    res, after = {}, [tokens[-1]]
    for layer in (3, 2, 1, 0):
        recv = _exchange_wait(f"scatter{layer}_wait", *scatters[layer], after, False)
        if layer in (0, 3):
            l = 0 if layer == 0 else 1
            res["pool_w_in", l] = adam(f"pool_w_in{l}", recv[0], 0, "pool_w_in", 256, l)
            res["pool_w_grp", l] = adam(f"pool_w_grp{l}", recv[1], 0, "pool_w_grp", 256, l)
            res["pool_w_out", l] = adam(f"pool_w_out{l}", recv[2], 0, "pool_w_out", 256, l)
            after = [res["pool_w_out", l][1]]
        elif layer == 1:
            res["conv_w_in", 0] = adam("conv_w_in", recv[0], 0, "conv_w_in", 256)
            res["conv_w_out", 0] = adam("conv_w_out", recv[1], 0, "conv_w_out", 256)
            after = [res["conv_w_out", 0][1]]
        else:
            res["mla_w_in", 0] = adam("mla_w_in", recv[0], 0, "mla_w_in", 256)
            res["mla_w_q_up", 0] = adam("mla_w_q_up", recv[1], 0, "mla_w_q_up", 384)
            res["mla_w_kv_up", 0] = adam("mla_w_kv_up", recv[2], 0, "mla_w_kv_up", 256)
            res["mla_w_out", 0] = adam("mla_w_out", recv[3], 0, "mla_w_out", 256)
            after = [res["mla_w_out", 0][1]]
    recv = _exchange_wait("scatter4_wait", *scatters[4], after, False)
    small = _sum_adamw("adam_small", recv[0], 0, _pack_small(p, SMALL_ROWS_RS, True)[None], _pack_small(m, SMALL_ROWS_RS, True)[None],
                       _pack_small(v, SMALL_ROWS_RS, True)[None], SMALL_ROWS_RS)
    small = [_unpack_small(a, True) for a in small]
    final = {k: tuple(part[k] for part in small) for k in _SMALL_SHARDED + _SMALL_REPLICATED}
    for k in _WEIGHTS:
        if k not in final:
            layers = [res[k, l] for l in range(p[k].shape[0])]
            final[k] = tuple(jnp.stack([lay[part] for lay in layers]).reshape(p[k].shape) for part in range(4))
    res = final

    loss = lax.psum(loss, ("x", "y", "c"))
    out = [loss, grad_x[None]]
    for part in range(4):
        out += [res[k][part] for k in _WEIGHTS]
    return tuple(out)


def kernel(x, positions, pool_norm, pool_w_in, pool_w_grp, pool_scale, pool_w_out, conv_norm, conv_w_in, conv_w, conv_w_out, mla_norm, mla_w_in, mla_q_norm, mla_w_q_up, mla_kv_norm, mla_w_kv_up, mla_w_out, final_norm, loss_target, m_pool_norm, m_pool_w_in, m_pool_w_grp, m_pool_scale, m_pool_w_out, m_conv_norm, m_conv_w_in, m_conv_w, m_conv_w_out, m_mla_norm, m_mla_w_in, m_mla_q_norm, m_mla_w_q_up, m_mla_kv_norm, m_mla_w_kv_up, m_mla_w_out, m_final_norm, v_pool_norm, v_pool_w_in, v_pool_w_grp, v_pool_scale, v_pool_w_out, v_conv_norm, v_conv_w_in, v_conv_w, v_conv_w_out, v_mla_norm, v_mla_w_in, v_mla_q_norm, v_mla_w_q_up, v_mla_kv_norm, v_mla_w_kv_up, v_mla_w_out, v_final_norm):
    p = dict(pool_norm=pool_norm, pool_w_in=pool_w_in, pool_w_grp=pool_w_grp, pool_scale=pool_scale, pool_w_out=pool_w_out,
             conv_norm=conv_norm, conv_w_in=conv_w_in, conv_w=conv_w, conv_w_out=conv_w_out, mla_norm=mla_norm, mla_w_in=mla_w_in,
             mla_q_norm=mla_q_norm, mla_w_q_up=mla_w_q_up, mla_kv_norm=mla_kv_norm, mla_w_kv_up=mla_w_kv_up, mla_w_out=mla_w_out,
             final_norm=final_norm)
    m = dict(pool_norm=m_pool_norm, pool_w_in=m_pool_w_in, pool_w_grp=m_pool_w_grp, pool_scale=m_pool_scale, pool_w_out=m_pool_w_out,
             conv_norm=m_conv_norm, conv_w_in=m_conv_w_in, conv_w=m_conv_w, conv_w_out=m_conv_w_out, mla_norm=m_mla_norm,
             mla_w_in=m_mla_w_in, mla_q_norm=m_mla_q_norm, mla_w_q_up=m_mla_w_q_up, mla_kv_norm=m_mla_kv_norm,
             mla_w_kv_up=m_mla_w_kv_up, mla_w_out=m_mla_w_out, final_norm=m_final_norm)
    v = dict(pool_norm=v_pool_norm, pool_w_in=v_pool_w_in, pool_w_grp=v_pool_w_grp, pool_scale=v_pool_scale, pool_w_out=v_pool_w_out,
             conv_norm=v_conv_norm, conv_w_in=v_conv_w_in, conv_w=v_conv_w, conv_w_out=v_conv_w_out, mla_norm=v_mla_norm,
             mla_w_in=v_mla_w_in, mla_q_norm=v_mla_q_norm, mla_w_q_up=v_mla_w_q_up, mla_kv_norm=v_mla_kv_norm,
             mla_w_kv_up=v_mla_w_kv_up, mla_w_out=v_mla_w_out, final_norm=v_final_norm)
    return _step(x, positions, loss_target, p, m, v)
```

```python
import functools

import jax
import jax.numpy as jnp
from jax import lax
from jax.experimental import pallas as pl
from jax.experimental.pallas import tpu as pltpu

BF = jnp.bfloat16
F32 = jnp.float32

N_DEV = 8
D_MODEL = 1024
D_INNER = 2048
POOL_WINDOWS = (2, 4, 8, 16)
POOL_GROUP = 512
N_HEADS = 16
QK_NOPE = 128
QK_ROPE = 64
QK_DIM = QK_NOPE + QK_ROPE
V_DIM = 128
Q_RANK = 384
KV_RANK = 256
ATTN_SCALE = QK_DIM ** -0.5
LOG2_E = 1.4426950408889634
LN_2 = 0.6931471805599453
Q_PRESCALE = ATTN_SCALE * LOG2_E
ATTN_TILE = 512
ATTN_HEADS_PER_STEP = 2
ROPE_BASE = 10000.0
NORM_EPS = 1e-6
NEG_BIG = -1e30

ADAM_LR = 0.001
ADAM_B1 = 0.9
ADAM_B2 = 0.999
ADAM_EPS = 1e-08
ADAM_WD = 0.01
ADAM_STEP = 10

VMEM_LIMIT_BYTES = 52 * 1024 * 1024
IN_PROJ_ROWS = 1024
POOL_HALO = 32
CONV_HALO = 16

NN = (((1,), (0,)), ((), ()))
NT = (((1,), (1,)), ((), ()))
TN = (((0,), (0,)), ((), ()))

SMALL_ROWS_AG = 16
SMALL_ROWS_RS = 32


def _sds(shape, dtype):
    return jax.ShapeDtypeStruct(tuple(shape), dtype)


def _params():
    return pltpu.CompilerParams(vmem_limit_bytes=VMEM_LIMIT_BYTES)


_ANY = pl.BlockSpec(memory_space=pl.ANY)


def _dot(a, b, dims):
    return lax.dot_general(a, b, dims, preferred_element_type=F32)


def _sig(z):
    return 1.0 / (1.0 + jnp.exp(-z))


def _silu_and_grad(z):
    sig = _sig(z)
    return z * sig, sig * (1.0 + z * (1.0 - sig))


def _rope_swap(x, p):
    pb = p.astype(BF)
    hi = x.astype(BF)
    r1 = x - hi.astype(F32)
    mid = r1.astype(BF)
    lo = (r1 - mid.astype(F32)).astype(BF)
    return (_dot(hi, pb, NN) + _dot(mid, pb, NN)) + _dot(lo, pb, NN)


def _rope_fwd(x, cosf, sinf, p):
    return x * cosf + _rope_swap(x, p) * sinf


def _rope_bwd(dy, cosf, sinf, p):
    return dy * cosf + _rope_swap(dy * sinf, p)


def _rms_bwd(dxn, x, g, res):
    r = lax.rsqrt(jnp.mean(x * x, axis=-1, keepdims=True) + NORM_EPS)
    v = dxn * g
    dx = r * v - x * ((r * r * r) * jnp.mean(v * x, axis=-1, keepdims=True))
    if res is not None:
        dx = dx + res
    dg = jnp.sum(dxn * (x * r), axis=0, keepdims=True)
    return dx, dg


def _accumulate(ref, val, step):
    @pl.when(step == 0)
    def _():
        ref[...] = val

    @pl.when(step > 0)
    def _():
        ref[...] += val


def _mm(name, grid, ins, in_specs, outs, out_specs, dims, epi, red=None, acc_shape=None):
    n_in, n_out = len(ins), len(outs)
    n_red = None if red is None else grid[red]

    def body(*refs):
        in_refs, out_refs = refs[:n_in], refs[n_in:n_in + n_out]
        pids = tuple(pl.program_id(ax) for ax in range(len(grid)))
        a, b = in_refs[0][...], in_refs[1][...]
        if a.ndim == 3:
            a = a.reshape(-1, a.shape[-1])
        if b.ndim == 3:
            b = b.reshape(-1, b.shape[-1])
        part = _dot(a.astype(BF), b.astype(BF), dims)
        if red is None:
            epi(part, in_refs[2:], out_refs, pids)
        else:
            acc = refs[n_in + n_out]
            k = pids[red]
            _accumulate(acc, part, k)

            @pl.when(k == n_red - 1)
            def _():
                epi(acc[...], in_refs[2:], out_refs, pids)

    scratch = [] if red is None else [pltpu.VMEM(acc_shape, F32)]
    return pl.pallas_call(body, name=name, grid=grid, in_specs=in_specs, out_specs=out_specs, out_shape=outs,
                          scratch_shapes=scratch, compiler_params=_params())(*ins)


def _store(part, extra, outs, pids):
    outs[0][...] = part.astype(outs[0].dtype)


def _rms_fwd(name, x, g, tm):
    s, d = x.shape

    def body(x_ref, g_ref, o_ref):
        xv = x_ref[...]
        r = lax.rsqrt(jnp.mean(xv * xv, axis=-1, keepdims=True) + NORM_EPS)
        o_ref[...] = ((xv * r) * g_ref[...]).astype(BF)

    return pl.pallas_call(body, name=name, grid=(s // tm,),
                          in_specs=[pl.BlockSpec((tm, d), lambda i: (i, 0)), pl.BlockSpec((1, d), lambda i: (0, 0))],
                          out_specs=pl.BlockSpec((tm, d), lambda i: (i, 0)), out_shape=_sds((s, d), BF),
                          compiler_params=_params())(x, g.reshape(1, d))


def _tn(name, a, b, out_shape, out_block, out_index, a_cols, b_cols, grid, a_index, b_index):
    s = a.shape[-2]
    a_block = (s, a_cols) if a.ndim == 2 else (None, s, a_cols)
    b_block = (s, b_cols) if b.ndim == 2 else (None, s, b_cols)

    def epi(part, extra, outs, pids):
        outs[0][...] = part.astype(BF).reshape(outs[0].shape)

    return _mm(name, grid, [a, b], [pl.BlockSpec(a_block, a_index), pl.BlockSpec(b_block, b_index)],
               [_sds(out_shape, BF)], [pl.BlockSpec(out_block, out_index)], TN, epi)[0]


def _pool_window_fwd(name, h, tm):
    s = h.shape[0]
    hb = POOL_HALO

    def body(u_ref, halo_ref, o_ref, e_ref, a_ref, b_ref):
        i = pl.program_id(0)
        row = lax.broadcasted_iota(jnp.int32, (tm, 1), 0) + i * tm
        for g, w in enumerate(POOL_WINDOWS):
            cs = slice(g * POOL_GROUP, (g + 1) * POOL_GROUP)
            e_ref[0:hb, :] = jnp.where(i > 0, halo_ref[:, cs].astype(F32), 0.0)
            e_ref[hb:, :] = u_ref[:, cs].astype(F32)
            src, bufs = e_ref, (a_ref, b_ref)
            for lv in range(1, w.bit_length()):
                dst, st, sh = bufs[(lv - 1) % 2], 8 * lv, 2 ** (lv - 1)
                n = hb + tm - st
                dst[st:, :] = src[st:, :] + src[pl.ds(st - sh, n), :]
                src = dst
            cnt = jnp.minimum(row + 1, w).astype(F32)
            o_ref[:, cs] = (src[hb:, :] / cnt - u_ref[:, cs].astype(F32)).astype(BF)

    per = tm // hb
    return pl.pallas_call(
        body, name=name, grid=(s // tm,),
        in_specs=[pl.BlockSpec((tm, D_INNER), lambda i: (i, 0)),
                  pl.BlockSpec((hb, D_INNER), lambda i: (jnp.maximum(i * per - 1, 0), 0))],
        out_specs=pl.BlockSpec((tm, D_INNER), lambda i: (i, 0)), out_shape=_sds((s, D_INNER), BF),
        scratch_shapes=[pltpu.VMEM((hb + tm, POOL_GROUP), F32)] * 3, compiler_params=_params())(h, h)


def _pool_window_bwd(name, dp, tm, dh):
    s = dp.shape[0]
    nt = s // tm
    hb = POOL_HALO

    def body(d_ref, halo_ref, dh_in_ref, o_ref, e_ref, a_ref, b_ref):
        i = pl.program_id(0)
        row = lax.broadcasted_iota(jnp.int32, (tm, 1), 0) + i * tm
        hrow = lax.broadcasted_iota(jnp.int32, (hb, 1), 0) + (i + 1) * tm
        for g, w in enumerate(POOL_WINDOWS):
            cs = slice(g * POOL_GROUP, (g + 1) * POOL_GROUP)
            e_ref[0:tm, :] = d_ref[:, cs] / jnp.minimum(row + 1, w).astype(F32)
            e_ref[tm:, :] = jnp.where(i < nt - 1, halo_ref[:, cs] / jnp.minimum(hrow + 1, w).astype(F32), 0.0)
            src, bufs = e_ref, (a_ref, b_ref)
            for lv in range(1, w.bit_length()):
                dst, sh = bufs[(lv - 1) % 2], 2 ** (lv - 1)
                n = tm + hb - 8 * lv
                dst[0:n, :] = src[0:n, :] + src[pl.ds(sh, n), :]
                src = dst
            o_ref[:, cs] = (src[0:tm, :] - d_ref[:, cs]).astype(BF)

    per = tm // hb
    last = s // hb - 1
    return pl.pallas_call(
        body, name=name, grid=(nt,),
        in_specs=[pl.BlockSpec((tm, D_INNER), lambda i: (i, 0)),
                  pl.BlockSpec((hb, D_INNER), lambda i: (jnp.minimum((i + 1) * per, last), 0)), _ANY],
        out_specs=pl.BlockSpec((tm, D_INNER), lambda i: (i, 0)), out_shape=_sds(dh.shape, BF),
        input_output_aliases={2: 0},
        scratch_shapes=[pltpu.VMEM((hb + tm, POOL_GROUP), F32)] * 3, compiler_params=_params())(dp, dp, dh)


def _grp_block():
    return pl.BlockSpec((N_DEV, 64, POOL_GROUP), lambda i, g: (0, g, 0))


def _pool_fwd(x, l, w, tm):
    s = x.shape[0]
    nt = s // tm
    n = f"pool{l}"
    xn = _rms_fwd(n + "_rms", x, w["norm"], tm)
    ti = IN_PROJ_ROWS if s % IN_PROJ_ROWS == 0 else tm
    (h,) = _mm(n + "_in", (s // ti, 8), [xn, w["g_in"]],
               [pl.BlockSpec((ti, D_MODEL), lambda i, j: (i, 0)), pl.BlockSpec((None, D_MODEL, 512), lambda i, j: (j, 0, 0))],
               [_sds((s, 2 * D_INNER), BF)], [pl.BlockSpec((ti, 512), lambda i, j: (i, j))], NN, _store)
    pooled = _pool_window_fwd(n + "_win", h, tm)
    w = dict(w, **w["rest"]([h]))

    def gate(part, extra, outs, pids):
        z = extra[0][...].astype(F32)
        outs[0][...] = ((part * extra[1][...]) * (z * _sig(z))).astype(BF)

    (gated,) = _mm(n + "_grp", (nt, 4), [pooled, w["g_grp"], h, w["scale"].reshape(1, D_INNER)],
                   [pl.BlockSpec((tm, 512), lambda i, g: (i, g)), _grp_block(),
                    pl.BlockSpec((tm, 512), lambda i, g: (i, 4 + g)), pl.BlockSpec((1, 512), lambda i, g: (0, g))],
                   [_sds((s, D_INNER), BF)], [pl.BlockSpec((tm, 512), lambda i, g: (i, g))], NN, gate)
    y = _out_proj(n + "_out", gated, w["g_out"], 0, x, tm)
    return y, dict(x=x, xn=xn, h=h, pooled=pooled, gated=gated, w=w)


def _out_proj(name, gated, g1024, row_block, x, tm):
    s = x.shape[0]

    def epi(part, extra, outs, pids):
        outs[0][...] = part + extra[0][...]

    return _mm(name, (s // tm, 2), [gated, g1024, x],
               [pl.BlockSpec((tm, D_INNER), lambda i, j: (i, 0)), pl.BlockSpec((N_DEV, 256, 512), lambda i, j: (0, row_block, j)),
                pl.BlockSpec((tm, 512), lambda i, j: (i, j))],
               [_sds((s, D_MODEL), F32)], [pl.BlockSpec((tm, 512), lambda i, j: (i, j))], NN, epi)[0]


def _w_out_nt_block(row_block):
    return pl.BlockSpec((2, 256, D_MODEL), lambda j, i: (j, row_block, 0))


def _in_proj_bwd(name, dh, wbuf, w_index, n_k, x, g, dy, tm, dep=()):
    s = x.shape[0]
    tm = IN_PROJ_ROWS if s % IN_PROJ_ROWS == 0 else tm

    def epi(acc, extra, outs, pids):
        dx, dg = _rms_bwd(acc, extra[0][...], extra[1][...], extra[2][...])
        outs[0][...] = dx
        outs[1][...] = dx.astype(BF)
        _accumulate(outs[2], dg, pids[0])

    row = lambda i, k: (i, 0)
    return _mm(name, (s // tm, n_k), [dh, wbuf, x, g.reshape(1, D_MODEL), dy] + list(dep),
               [pl.BlockSpec((tm, 512), lambda i, k: (i, k)), pl.BlockSpec((None, D_MODEL, 512), w_index),
                pl.BlockSpec((tm, D_MODEL), row), pl.BlockSpec((1, D_MODEL), lambda i, k: (0, 0)), pl.BlockSpec((tm, D_MODEL), row)]
               + [_ANY] * len(dep),
               [_sds((s, D_MODEL), F32), _sds((s, D_MODEL), BF), _sds((1, D_MODEL), F32)],
               [pl.BlockSpec((tm, D_MODEL), row), pl.BlockSpec((tm, D_MODEL), row), pl.BlockSpec((1, D_MODEL), lambda i, k: (0, 0))],
               NT, epi, red=1, acc_shape=(tm, D_MODEL))


def _w_out_grad(name, gated, dyb):
    s = gated.shape[0]
    return _tn(name, gated, dyb, (D_INNER, D_MODEL), (512, D_MODEL), lambda i: (i, 0), 512, D_MODEL, (4,),
               lambda i: (0, i), lambda i: (0, 0))


def _pool_bwd(dy, dyb, l, w, sv, tm, dep, early=None):
    s = dy.shape[0]
    nt = s // tm
    n = f"pool{l}b"
    h, pooled = sv["h"], sv["pooled"]
    scale = w["scale"].reshape(1, D_INNER)

    def gate_bwd(part, extra, outs, pids):
        z, sc = extra[0][...].astype(F32), extra[1][...]
        wg = extra[3][...].reshape(POOL_GROUP, POOL_GROUP)
        mpv = _dot(extra[2][...], wg, NN)
        sz, dsz = _silu_and_grad(z)
        dm = part * sz
        dmp = (dm * sc).astype(BF)
        outs[0][...] = dmp
        outs[1][...] = (part * (mpv * sc) * dsz).astype(BF)
        _accumulate(outs[2], jnp.sum(dm * mpv, axis=0, keepdims=True), pids[1])
        outs[3][...] = _dot(dmp, wg, NT)

    tile = lambda j, i: (i, j)
    dmp, dz, dscale, dpool = _mm(
        n + "_out", (4, nt), [dyb, w["g_out"], h, scale, pooled, w["g_grp"]] + dep,
        [pl.BlockSpec((tm, D_MODEL), lambda j, i: (i, 0)), _w_out_nt_block(0),
         pl.BlockSpec((tm, 512), lambda j, i: (i, 4 + j)), pl.BlockSpec((1, 512), lambda j, i: (0, j)), pl.BlockSpec((tm, 512), tile),
         pl.BlockSpec((N_DEV, 64, POOL_GROUP), lambda j, i: (0, j, 0))] + [_ANY] * len(dep),
        [_sds((s, D_INNER), BF), _sds((s, 2 * D_INNER), BF), _sds((1, D_INNER), F32), _sds((s, D_INNER), F32)],
        [pl.BlockSpec((tm, 512), tile), pl.BlockSpec((tm, 512), lambda j, i: (i, 4 + j)), pl.BlockSpec((1, 512), lambda j, i: (0, j)),
         pl.BlockSpec((tm, 512), tile)],
        NT, gate_bwd)
    dh = _pool_window_bwd(n + "_win", dpool, tm, dz)
    g_in = _tn(n + "_gin", sv["xn"], dh, (N_DEV, D_MODEL, 512), (None, D_MODEL, 512), lambda j: (j, 0, 0),
               D_MODEL, 512, (8,), lambda j: (0, 0), lambda j: (0, j))
    g_out = _w_out_grad(n + "_gout", sv["gated"], dyb)
    g_grp = _tn(n + "_ggrp", pooled, dmp, (N_DEV, 256, 512), (N_DEV, 64, 512), lambda g: (0, g, 0),
                512, 512, (4,), lambda g: (0, g), lambda g: (0, g))
    grads = dict(g_in=g_in, g_grp=g_grp, g_out=g_out.reshape(N_DEV, 256, D_MODEL))
    dep = early(grads) if early is not None else ()
    dx, dxb, dnorm = _in_proj_bwd(n + "_in", dh, w["g_in"], lambda i, k: (k, 0, 0), 8, sv["x"], w["norm"], dy, tm, dep)
    return dx, dxb, dict(grads, norm=dnorm[0], scale=dscale[0])


def _conv_in_index(i, j):
    return (j // 2, 0, j % 2)


def _conv_fwd(x, w, tm):
    s = x.shape[0]
    nt = s // tm
    xn = _rms_fwd("conv_rms", x, w["norm"], tm)
    ti = IN_PROJ_ROWS if s % IN_PROJ_ROWS == 0 else tm
    (h,) = _mm("conv_in", (s // ti, 16), [xn, w["g_in"]],
               [pl.BlockSpec((ti, D_MODEL), lambda i, j: (i, 0)), pl.BlockSpec((None, D_MODEL, 512), _conv_in_index)],
               [_sds((s, 4 * D_INNER), BF)], [pl.BlockSpec((ti, 512), lambda i, j: (i, j))], NN, _store)
    per = tm // CONV_HALO

    def body(b_ref, c_ref, h_ref, z_ref, cp_ref, hp_ref, w_ref, o_ref, e_ref):
        i = pl.program_id(0)
        ch = c_ref[...].astype(F32) * h_ref[...].astype(F32)
        e_ref[0:CONV_HALO, :] = jnp.where(i > 0, cp_ref[...].astype(F32) * hp_ref[...].astype(F32), 0.0)
        e_ref[CONV_HALO:, :] = ch
        co = (w_ref[2:3, :] * ch + w_ref[1:2, :] * e_ref[pl.ds(CONV_HALO - 1, tm), :]
              + w_ref[0:1, :] * e_ref[pl.ds(CONV_HALO - 2, tm), :])
        z = z_ref[...].astype(F32)
        o_ref[...] = ((b_ref[...].astype(F32) * co) * (z * _sig(z))).astype(BF)

    def col(q):
        return pl.BlockSpec((tm, 512), lambda i, j: (i, 4 * q + j))

    def prev(q):
        return pl.BlockSpec((CONV_HALO, 512), lambda i, j: (jnp.maximum(i * per - 1, 0), 4 * q + j))

    gated = pl.pallas_call(
        body, name="conv_mix", grid=(nt, 4),
        in_specs=[col(0), col(1), col(2), col(3), prev(1), prev(2), pl.BlockSpec((3, 512), lambda i, j: (0, j))],
        out_specs=pl.BlockSpec((tm, 512), lambda i, j: (i, j)), out_shape=_sds((s, D_INNER), BF),
        scratch_shapes=[pltpu.VMEM((CONV_HALO + tm, 512), F32)], compiler_params=_params())(h, h, h, h, h, h, w["conv_w"])
    y = _out_proj("conv_out", gated, w["g_out"], 0, x, tm)
    return y, dict(x=x, xn=xn, h=h, gated=gated)


def _conv_bwd(dy, dyb, w, sv, tm, dep):
    s = dy.shape[0]
    nt = s // tm
    h = sv["h"]
    (dg,) = _mm("convb_out", (4, nt), [dyb, w["g_out"]] + dep,
                [pl.BlockSpec((tm, D_MODEL), lambda j, i: (i, 0)), _w_out_nt_block(0)] + [_ANY] * len(dep),
                [_sds((s, D_INNER), F32)], [pl.BlockSpec((tm, 512), lambda j, i: (i, j))], NT, _store)
    per = tm // CONV_HALO
    last = s // CONV_HALO - 1

    def body(dg_ref, b_ref, c_ref, h_ref, z_ref, cp_ref, hp_ref, dgn_ref, bn_ref, zn_ref, w_ref,
             dall_ref, dw_ref, e_ref, f_ref):
        db_ref, dc_ref, dh_ref, dz_ref = (dall_ref.at[:, q * 512:(q + 1) * 512] for q in range(4))
        i = pl.program_id(1)
        w0, w1, w2 = w_ref[0:1, :], w_ref[1:2, :], w_ref[2:3, :]
        c, hh, b = c_ref[...].astype(F32), h_ref[...].astype(F32), b_ref[...].astype(F32)
        ch = c * hh
        e_ref[0:CONV_HALO, :] = jnp.where(i > 0, cp_ref[...].astype(F32) * hp_ref[...].astype(F32), 0.0)
        e_ref[CONV_HALO:, :] = ch
        ch1 = e_ref[pl.ds(CONV_HALO - 1, tm), :]
        ch2 = e_ref[pl.ds(CONV_HALO - 2, tm), :]
        co = w2 * ch + w1 * ch1 + w0 * ch2
        sz, dsz = _silu_and_grad(z_ref[...].astype(F32))
        dgv = dg_ref[...]
        dyv = dgv * sz
        dz_ref[...] = (dgv * (b * co) * dsz).astype(BF)
        db_ref[...] = (dyv * co).astype(BF)
        dco = dyv * b
        zn = zn_ref[...].astype(F32)
        f_ref[0:tm, :] = dco
        f_ref[tm:, :] = jnp.where(i < nt - 1, dgn_ref[...] * (zn * _sig(zn)) * bn_ref[...].astype(F32), 0.0)
        dch = w2 * dco + w1 * f_ref[pl.ds(1, tm), :] + w0 * f_ref[pl.ds(2, tm), :]
        dc_ref[...] = (dch * hh).astype(BF)
        dh_ref[...] = (dch * c).astype(BF)
        for tap, shifted in enumerate((ch2, ch1, ch)):
            _accumulate(dw_ref.at[tap:tap + 1, :], jnp.sum(dco * shifted, axis=0, keepdims=True), i)

    def col(q):
        return pl.BlockSpec((tm, 512), lambda j, i: (i, 4 * q + j))

    def prev(q):
        return pl.BlockSpec((CONV_HALO, 512), lambda j, i: (jnp.maximum(i * per - 1, 0), 4 * q + j))

    def nxt(q):
        return pl.BlockSpec((CONV_HALO, 512), lambda j, i: (jnp.minimum((i + 1) * per, last), 4 * q + j))

    tile = pl.BlockSpec((tm, 512), lambda j, i: (i, j))
    wspec = pl.BlockSpec((3, 512), lambda j, i: (0, j))
    dh, dw = pl.pallas_call(
        body, name="convb_mix", grid=(4, nt),
        in_specs=[tile, col(0), col(1), col(2), col(3), prev(1), prev(2), nxt(0), nxt(0), nxt(3), wspec],
        out_specs=[pl.BlockSpec((tm, D_INNER), lambda j, i: (i, j)), wspec],
        out_shape=[_sds((s, 4 * D_INNER), BF), _sds((3, D_INNER), F32)],
        scratch_shapes=[pltpu.VMEM((CONV_HALO + tm, 512), F32)] * 2, compiler_params=_params(),
    )(dg, h, h, h, h, h, h, dg, h, h, w["conv_w"])

    def w_block(kp):
        k = 4 * (kp % 4) + kp // 4
        return (k // 2, 0, k % 2)

    dx, dxb, dnorm = _in_proj_bwd("convb_in", dh, w["g_in"], lambda i, kp: w_block(kp), 16, sv["x"], w["norm"], dy, tm)
    g_in = _tn("convb_gin", sv["xn"], dh, (N_DEV, D_MODEL, D_MODEL), (None, D_MODEL, 512), lambda j: (j // 2, 0, j % 2),
               D_MODEL, 512, (16,), lambda j: (0, 0), lambda j: (0, 4 * (j % 4) + j // 4))
    g_out = _w_out_grad("convb_gout", sv["gated"], dyb)
    return dx, dxb, dict(g_in=g_in, g_out=g_out.reshape(N_DEV, 256, D_MODEL), norm=dnorm[0], conv_w=dw)


def _attn_tiles(s):
    t = min(ATTN_TILE, s)
    return t, s // t


def _causal_keep(t, keys_on_rows):
    r = lax.broadcasted_iota(jnp.int32, (t, t), 0)
    c = lax.broadcasted_iota(jnp.int32, (t, t), 1)
    return (r <= c) if keys_on_rows else (c <= r)


def _mla_fwd(x, w, rope, tm):
    s = x.shape[0]
    nt = s // tm
    cosf, sinf, perm = rope
    xn = _rms_fwd("mla_rms", x, w["norm"], tm)

    def in_body(xn_ref, wq_ref, wkv_ref, wkr_ref, wz_ref, gq_ref, gkv_ref, cos_ref, sin_ref, p_ref,
                ql_ref, kvl_ref, qn_ref, kvn_ref, krr_ref, z_ref):
        xv = xn_ref[...]
        ql = _dot(xv, wq_ref[...], NN)
        kvl = _dot(xv, wkv_ref[...], NN)
        ql_ref[...] = ql
        kvl_ref[...] = kvl
        rq = lax.rsqrt(jnp.mean(ql * ql, axis=-1, keepdims=True) + NORM_EPS)
        qn_ref[...] = ((ql * rq) * gq_ref[...]).astype(BF)
        rkv = lax.rsqrt(jnp.mean(kvl * kvl, axis=-1, keepdims=True) + NORM_EPS)
        kvn_ref[...] = ((kvl * rkv) * gkv_ref[...]).astype(BF)
        kr = _dot(xv, wkr_ref[...], NN)
        krr_ref[...] = _rope_fwd(kr, cos_ref[...], sin_ref[...], p_ref[...]).astype(BF)
        z_ref[...] = _dot(xv, wz_ref[...], NN).astype(BF)

    def full(a):
        return pl.BlockSpec(a.shape, lambda i: (0,) * a.ndim)

    def rows(c):
        return pl.BlockSpec((tm, c), lambda i: (i, 0))

    gq, gkv = w["q_norm"].reshape(1, Q_RANK), w["kv_norm"].reshape(1, KV_RANK)
    q_lat, kv_lat, qn, kvn, krr, z = pl.pallas_call(
        in_body, name="mla_in", grid=(nt,),
        in_specs=[rows(D_MODEL), full(w["w_q"]), full(w["w_kv"]), full(w["w_kr"]), full(w["w_z"]), full(gq), full(gkv),
                  rows(QK_ROPE), rows(QK_ROPE), full(perm)],
        out_specs=[rows(Q_RANK), rows(KV_RANK), rows(Q_RANK), rows(KV_RANK), rows(QK_ROPE), rows(D_INNER)],
        out_shape=[_sds((s, Q_RANK), F32), _sds((s, KV_RANK), F32), _sds((s, Q_RANK), BF), _sds((s, KV_RANK), BF),
                   _sds((s, QK_ROPE), BF), _sds((s, D_INNER), BF)],
        compiler_params=_params())(xn, w["w_q"], w["w_kv"], w["w_kr"], w["w_z"], gq, gkv, cosf, sinf, perm)

    def q_epi(part, extra, outs, pids):
        outs[0][:, 0:QK_NOPE] = (part[:, 0:QK_NOPE] * Q_PRESCALE).astype(BF)
        roped = _rope_fwd(part[:, QK_NOPE:QK_DIM], extra[0][...], extra[1][...], extra[2][...])
        outs[0][:, QK_NOPE:QK_DIM] = (roped * Q_PRESCALE).astype(BF)

    rope_row = pl.BlockSpec((tm, QK_ROPE), lambda h, i: (i, 0))
    (q,) = _mm("mla_qup", (N_HEADS, nt), [qn, w["w_qh"], cosf, sinf, perm],
               [pl.BlockSpec((tm, Q_RANK), lambda h, i: (i, 0)), pl.BlockSpec((None, Q_RANK, QK_DIM), lambda h, i: (h, 0, 0)),
                rope_row, rope_row, pl.BlockSpec((QK_ROPE, QK_ROPE), lambda h, i: (0, 0))],
               [_sds((N_HEADS, s, QK_DIM), BF)], [pl.BlockSpec((None, tm, QK_DIM), lambda h, i: (h, i, 0))], NN, q_epi)

    def kv_epi(part, extra, outs, pids):
        outs[0][:, 0:QK_NOPE] = part[:, 0:QK_NOPE].astype(BF)
        outs[0][:, QK_NOPE:QK_DIM] = extra[0][...]
        outs[1][...] = part[:, QK_NOPE:].astype(BF)

    k, v = _mm("mla_kvup", (N_HEADS, nt), [kvn, w["g512"], krr],
               [pl.BlockSpec((tm, KV_RANK), lambda h, i: (i, 0)),
                pl.BlockSpec((None, KV_RANK, 256), lambda h, i: (h // 2, 0, h % 2)), rope_row],
               [_sds((N_HEADS, s, QK_DIM), BF), _sds((N_HEADS, s, V_DIM), BF)],
               [pl.BlockSpec((None, tm, QK_DIM), lambda h, i: (h, i, 0)), pl.BlockSpec((None, tm, V_DIM), lambda h, i: (h, i, 0))],
               NN, kv_epi)

    t, nq = _attn_tiles(s)

    def attn_body(q_ref, k_ref, v_ref, z_ref, o_ref, g_ref, lse_ref):
        i = pl.program_id(1)

        def block(j, carry, masked):
            start = pl.multiple_of(j * t, t)
            out = []
            for hh, (m, lsum, acc) in enumerate(carry):
                sc = _dot(q_ref[hh], k_ref[hh, pl.ds(start, t), :], NT)
                if masked:
                    sc = jnp.where(_causal_keep(t, False), sc, NEG_BIG)
                mn = jnp.maximum(m, jnp.max(sc, axis=-1, keepdims=True))
                alpha = jnp.exp2(m - mn)
                p = jnp.exp2(sc - mn)
                lsum = alpha * lsum + jnp.sum(p, axis=-1, keepdims=True)
                acc = alpha * acc + _dot(p.astype(BF), v_ref[hh, pl.ds(start, t), :], NN)
                out.append((mn, lsum, acc))
            return tuple(out)

        init = ((jnp.full((t, 1), NEG_BIG, F32), jnp.zeros((t, 1), F32), jnp.zeros((t, V_DIM), F32)),) * ATTN_HEADS_PER_STEP
        carry = lax.fori_loop(0, i, lambda j, c: block(j, c, False), init)
        for hh, (m, lsum, acc) in enumerate(block(i, carry, True)):
            cols = slice(hh * V_DIM, (hh + 1) * V_DIM)
            o = acc / lsum
            z = z_ref[:, cols].astype(F32)
            o_ref[:, cols] = o
            g_ref[:, cols] = (o * (z * _sig(z))).astype(BF)
            lse_ref[hh] = m + jnp.log(lsum) * LOG2_E

    hp = ATTN_HEADS_PER_STEP
    head_col = pl.BlockSpec((t, hp * V_DIM), lambda h, i: (i, h))
    o, gated, lse = pl.pallas_call(
        attn_body, name="mla_attn", grid=(N_HEADS // hp, nq),
        in_specs=[pl.BlockSpec((hp, t, QK_DIM), lambda h, i: (h, i, 0)), pl.BlockSpec((hp, s, QK_DIM), lambda h, i: (h, 0, 0)),
                  pl.BlockSpec((hp, s, V_DIM), lambda h, i: (h, 0, 0)), head_col],
        out_specs=[head_col, head_col, pl.BlockSpec((hp, t, 1), lambda h, i: (h, i, 0))],
        out_shape=[_sds((s, D_INNER), F32), _sds((s, D_INNER), BF), _sds((N_HEADS, s, 1), F32)],
        compiler_params=_params())(q, k, v, z)
    y = _out_proj("mla_out", gated, w["g1024"], 0, x, tm)
    return y, dict(x=x, xn=xn, q_lat=q_lat, kv_lat=kv_lat, qn=qn, kvn=kvn, z=z, q=q, k=k, v=v, o=o, lse=lse, gated=gated)


def _mla_bwd(dy, dyb, w, sv, rope, tm, dep):
    s = dy.shape[0]
    nt = s // tm
    cosf, sinf, perm = rope
    t, nq = _attn_tiles(s)
    q, k, v, lse = sv["q"], sv["k"], sv["v"], sv["lse"]

    def gate_bwd(part, extra, outs, pids):
        z, o = extra[0][...].astype(F32), extra[1][...]
        sz, dsz = _silu_and_grad(z)
        do = part * sz
        outs[0][...] = do.astype(BF)
        outs[1][...] = (part * o * dsz).astype(BF)
        prod = do * o
        for hh in range(4):
            outs[2][hh] = jnp.sum(prod[:, hh * V_DIM:(hh + 1) * V_DIM], axis=-1, keepdims=True)

    tile = lambda j, i: (i, j)
    dob, dz, delta = _mm(
        "mlab_out", (4, nt), [dyb, w["g1024"], sv["z"], sv["o"]] + dep,
        [pl.BlockSpec((tm, D_MODEL), lambda j, i: (i, 0)), _w_out_nt_block(0),
         pl.BlockSpec((tm, 512), tile), pl.BlockSpec((tm, 512), tile)] + [_ANY] * len(dep),
        [_sds((s, D_INNER), BF), _sds((s, D_INNER), BF), _sds((N_HEADS, s, 1), F32)],
        [pl.BlockSpec((tm, 512), tile), pl.BlockSpec((tm, 512), tile), pl.BlockSpec((4, tm, 1), lambda j, i: (j, i, 0))],
        NT, gate_bwd)

    def attn_bwd_body(k_ref, v_ref, q_ref, do_ref, lse_ref, dl_ref, cos_ref, sin_ref, p_ref, dkv_ref, dkr_ref, dq_ref, dq_acc):
        j = pl.program_id(1)
        kb, vb = k_ref[...], v_ref[...]

        @pl.when(j == 0)
        def _():
            dq_acc[...] = jnp.zeros(dq_acc.shape, F32)

        def block(i, carry, masked):
            dk, dv = carry
            rows = pl.ds(pl.multiple_of(i * t, t), t)
            qb, dob_ = q_ref[rows, :], do_ref[rows, :]
            st = _dot(kb, qb, NT)
            if masked:
                st = jnp.where(_causal_keep(t, True), st, NEG_BIG)
            pt = jnp.exp2(st - lse_ref[i])
            dv = dv + _dot(pt.astype(BF), dob_, NN)
            dst = (pt * (_dot(vb, dob_, NT) - dl_ref[i])).astype(BF)
            dk = dk + _dot(dst, qb, NN)
            dq_acc[rows, :] += _dot(dst, kb, TN)
            return dk, dv

        carry = block(j, (jnp.zeros((t, QK_DIM), F32), jnp.zeros((t, V_DIM), F32)), True)
        dk, dv = lax.fori_loop(j + 1, nq, lambda i, c: block(i, c, False), carry)
        dk = dk * LN_2
        dkv_ref[:, 0:QK_NOPE] = dk[:, 0:QK_NOPE].astype(BF)
        dkv_ref[:, QK_NOPE:] = dv.astype(BF)
        dkr_ref[...] = dk[:, QK_NOPE:]

        @pl.when(j == nq - 1)
        def _():
            for c in range(nq):
                rows = slice(c * t, (c + 1) * t)
                dq = dq_acc[rows, :] * ATTN_SCALE
                dq_ref[rows, 0:QK_NOPE] = dq[:, 0:QK_NOPE].astype(BF)
                dq_ref[rows, QK_NOPE:] = _rope_bwd(dq[:, QK_NOPE:], cos_ref[rows, :], sin_ref[rows, :], p_ref[...]).astype(BF)

    row_stats = pl.BlockSpec((None, nq, 1, t), lambda h, j: (h, 0, 0, 0))
    seq_rope = pl.BlockSpec((s, QK_ROPE), lambda h, j: (0, 0))
    head_seq = pl.BlockSpec((None, s, QK_DIM), lambda h, j: (h, 0, 0))
    dkv, dkr_h, dq = pl.pallas_call(
        attn_bwd_body, name="mlab_attn", grid=(N_HEADS, nq),
        in_specs=[pl.BlockSpec((None, t, QK_DIM), lambda h, j: (h, j, 0)), pl.BlockSpec((None, t, V_DIM), lambda h, j: (h, j, 0)),
                  head_seq, pl.BlockSpec((s, V_DIM), lambda h, j: (0, h)), row_stats, row_stats, seq_rope, seq_rope,
                  pl.BlockSpec((QK_ROPE, QK_ROPE), lambda h, j: (0, 0))],
        out_specs=[pl.BlockSpec((t, 2 * V_DIM), lambda h, j: (j, h)), pl.BlockSpec((None, t, QK_ROPE), lambda h, j: (h, j, 0)), head_seq],
        out_shape=[_sds((s, N_HEADS * 2 * V_DIM), BF), _sds((N_HEADS, s, QK_ROPE), F32), _sds((N_HEADS, s, QK_DIM), BF)],
        scratch_shapes=[pltpu.VMEM((s, QK_DIM), F32)],
        compiler_params=_params())(k, v, q, dob, lse.reshape(N_HEADS, nq, 1, t), delta.reshape(N_HEADS, nq, 1, t), cosf, sinf, perm)

    def dkr_body(d_ref, cos_ref, sin_ref, p_ref, o_ref):
        tot = d_ref[0]
        for hh in range(1, N_HEADS):
            tot = tot + d_ref[hh]
        o_ref[...] = _rope_bwd(tot, cos_ref[...], sin_ref[...], p_ref[...]).astype(BF)

    r64 = pl.BlockSpec((tm, QK_ROPE), lambda i: (i, 0))
    dkr = pl.pallas_call(
        dkr_body, name="mlab_dkr", grid=(nt,),
        in_specs=[pl.BlockSpec((N_HEADS, tm, QK_ROPE), lambda i: (0, i, 0)), r64, r64, pl.BlockSpec((QK_ROPE, QK_ROPE), lambda i: (0, 0))],
        out_specs=r64, out_shape=_sds((s, QK_ROPE), BF), compiler_params=_params())(dkr_h, cosf, sinf, perm)

    def lat_epi(acc, extra, outs, pids):
        dx, dg = _rms_bwd(acc, extra[0][...], extra[1][...], None)
        outs[0][...] = dx.astype(BF)
        _accumulate(outs[1], dg, pids[0])

    def lat_bwd(name, a, a_spec, b, b_spec, n_k, lat, g, rank):
        row = lambda i, k: (i, 0)
        one = lambda i, k: (0, 0)
        return _mm(name, (nt, n_k), [a, b, lat, g.reshape(1, rank)],
                   [a_spec, b_spec, pl.BlockSpec((tm, rank), row), pl.BlockSpec((1, rank), one)],
                   [_sds((s, rank), BF), _sds((1, rank), F32)], [pl.BlockSpec((tm, rank), row), pl.BlockSpec((1, rank), one)],
                   NT, lat_epi, red=1, acc_shape=(tm, rank))

    d_ql, g_qnorm = lat_bwd("mlab_qup", dq, pl.BlockSpec((None, tm, QK_DIM), lambda i, h: (h, i, 0)),
                            w["w_qh"], pl.BlockSpec((None, Q_RANK, QK_DIM), lambda i, h: (h, 0, 0)), N_HEADS,
                            sv["q_lat"], w["q_norm"], Q_RANK)
    d_kvl, g_kvnorm = lat_bwd("mlab_kvup", dkv, pl.BlockSpec((tm, 512), lambda i, kk: (i, kk)),
                              w["g512"], pl.BlockSpec((None, KV_RANK, 512), lambda i, kk: (kk, 0, 0)), N_DEV,
                              sv["kv_lat"], w["kv_norm"], KV_RANK)

    def in_bwd(dql_ref, dkvl_ref, dkr_ref, dz_ref, wq_ref, wkv_ref, wkr_ref, wz_ref, x_ref, g_ref, dy_ref, dx_ref, dxb_ref, dg_ref):
        acc = (_dot(dql_ref[...], wq_ref[...], NT) + _dot(dkvl_ref[...], wkv_ref[...], NT)
               + _dot(dkr_ref[...], wkr_ref[...], NT) + _dot(dz_ref[...], wz_ref[...], NT))
        dx, dg = _rms_bwd(acc, x_ref[...], g_ref[...], dy_ref[...])
        dx_ref[...] = dx
        dxb_ref[...] = dx.astype(BF)
        _accumulate(dg_ref, dg, pl.program_id(0))

    def full(a):
        return pl.BlockSpec(a.shape, lambda i: (0,) * a.ndim)

    def rows(c):
        return pl.BlockSpec((tm, c), lambda i: (i, 0))

    gm = w["norm"].reshape(1, D_MODEL)
    dx, dxb, g_norm = pl.pallas_call(
        in_bwd, name="mlab_in", grid=(nt,),
        in_specs=[rows(Q_RANK), rows(KV_RANK), rows(QK_ROPE), rows(D_INNER), full(w["w_q"]), full(w["w_kv"]), full(w["w_kr"]),
                  full(w["w_z"]), rows(D_MODEL), full(gm), rows(D_MODEL)],
        out_specs=[rows(D_MODEL), rows(D_MODEL), full(gm)],
        out_shape=[_sds((s, D_MODEL), F32), _sds((s, D_MODEL), BF), _sds((1, D_MODEL), F32)],
        compiler_params=_params())(d_ql, d_kvl, dkr, dz, w["w_q"], w["w_kv"], w["w_kr"], w["w_z"], sv["x"], gm, dy)

    xn = sv["xn"]
    one = lambda j: (0, 0)
    g_q = _tn("mlab_gq", xn, d_ql, (D_MODEL, Q_RANK), (D_MODEL, Q_RANK), one, D_MODEL, Q_RANK, (1,), one, one)
    g_kv = _tn("mlab_gkv", xn, d_kvl, (D_MODEL, KV_RANK), (D_MODEL, KV_RANK), one, D_MODEL, KV_RANK, (1,), one, one)
    g_kr = _tn("mlab_gkr", xn, dkr, (D_MODEL, QK_ROPE), (D_MODEL, QK_ROPE), one, D_MODEL, QK_ROPE, (1,), one, one)
    g_z = _tn("mlab_gz", xn, dz, (D_MODEL, D_INNER), (D_MODEL, 512), lambda j: (0, j), D_MODEL, 512, (4,), one, lambda j: (0, j))
    g_in = jnp.concatenate([g_q, g_kv, g_kr, g_z], axis=1)
    g_qh = _tn("mlab_gqup", sv["qn"], dq, (N_HEADS, Q_RANK, QK_DIM), (None, Q_RANK, QK_DIM), lambda h: (h, 0, 0),
               Q_RANK, QK_DIM, (N_HEADS,), lambda h: (0, 0), lambda h: (h, 0, 0))
    g_kvup = _tn("mlab_gkvup", sv["kvn"], dkv, (N_DEV, KV_RANK, 512), (None, KV_RANK, 512), lambda j: (j, 0, 0),
                 KV_RANK, 512, (N_DEV,), lambda j: (0, 0), lambda j: (0, j))
    g_out = _w_out_grad("mlab_gout", sv["gated"], dyb)
    s384 = g_qh.reshape(N_DEV, 2, Q_RANK, QK_DIM).transpose(0, 2, 1, 3).reshape(N_DEV, Q_RANK, 2 * QK_DIM)
    s344 = g_in.reshape(D_MODEL, N_DEV, 344).transpose(1, 0, 2)
    return dx, dxb, dict(s344=s344, s384=s384, s512=g_kvup, s1024=g_out.reshape(N_DEV, 256, D_MODEL),
                         norm=g_norm[0], q_norm=g_qnorm[0], kv_norm=g_kvnorm[0])


def _loss_head(x, g, target, tm):
    s, d = x.shape

    def body(x_ref, g_ref, t_ref, dx_ref, dxb_ref, dg_ref, loss_ref):
        i = pl.program_id(0)
        xv, gv = x_ref[...], g_ref[...]
        r = lax.rsqrt(jnp.mean(xv * xv, axis=-1, keepdims=True) + NORM_EPS)
        err = (xv * r) * gv - t_ref[...]
        part = 0.5 * jnp.sum(jnp.mean(err * err, axis=-1, keepdims=True), axis=0, keepdims=True)
        dx, dg = _rms_bwd(err * (1.0 / d), xv, gv, None)
        dx_ref[...] = dx
        dxb_ref[...] = dx.astype(BF)
        _accumulate(dg_ref, dg, i)
        _accumulate(loss_ref, jnp.broadcast_to(part, loss_ref.shape), i)

    row = pl.BlockSpec((tm, d), lambda i: (i, 0))
    one = pl.BlockSpec((1, d), lambda i: (0, 0))
    return pl.pallas_call(
        body, name="loss_head", grid=(s // tm,), in_specs=[row, one, row],
        out_specs=[row, row, one, pl.BlockSpec((8, 128), lambda i: (0, 0))],
        out_shape=[_sds((s, d), F32), _sds((s, d), BF), _sds((1, d), F32), _sds((8, 128), F32)],
        compiler_params=_params())(x, g.reshape(1, d), target)


def _rope_tables(pos):
    inv_freq = ROPE_BASE ** (-jnp.arange(0, QK_ROPE, 2, dtype=F32) / QK_ROPE)
    ang = pos.astype(F32)[:, None] * inv_freq
    cos, sin = jnp.cos(ang), jnp.sin(ang)
    idx = jnp.arange(QK_ROPE)
    perm = (idx[:, None] == (idx[None, :] + QK_ROPE // 2) % QK_ROPE).astype(F32)
    return jnp.concatenate([cos, cos], axis=1), jnp.concatenate([-sin, sin], axis=1), perm


def _local_step(x, pos, target, final_norm, get_w, put_g):
    s = x.shape[0]
    tm = min(512, s)
    rope = _rope_tables(pos)
    w0 = get_w(0, [])
    x1, sv0 = _pool_fwd(x, 0, w0, tm)
    w1 = get_w(1, [x1])
    x2, sv1 = _conv_fwd(x1, w1, tm)
    w2 = get_w(2, [x2])
    x3, sv2 = _mla_fwd(x2, w2, rope, tm)
    w3 = get_w(3, [x3])
    x4, sv3 = _pool_fwd(x3, 1, w3, tm)
    d4, d4b, g_final, loss = _loss_head(x4, final_norm, target, tm)
    d3, d3b, gp1 = _pool_bwd(d4, d4b, 1, sv3["w"], sv3, tm, [])
    dep = put_g(3, gp1)
    d2, d2b, gm = _mla_bwd(d3, d3b, w2, sv2, rope, tm, dep)
    dep = put_g(2, gm)
    d1, d1b, gc = _conv_bwd(d2, d2b, w1, sv1, tm, dep)
    dep = put_g(1, gc)
    d0, _, gp0 = _pool_bwd(d1, d1b, 0, sv0["w"], sv0, tm, dep, early=lambda big: put_g(0, big))
    put_g(4, dict(gp0, final_norm=g_final[0]))
    return loss[0, 0], d0


def _pack_groups(p):
    bf = lambda a: a.astype(BF)
    grp = lambda l: bf(p["pool_w_grp"][l].reshape(4 * 64, POOL_GROUP))
    return [[bf(p["pool_w_in"][0]), _pack_small(p, SMALL_ROWS_AG)],
            [grp(0), bf(p["pool_w_out"][0])],
            [bf(p["conv_w_in"][0]), bf(p["conv_w_out"][0])],
            [bf(p[k][0]) for k in ("mla_w_in", "mla_w_q_up", "mla_w_kv_up", "mla_w_out")],
            [bf(p["pool_w_in"][1]), grp(1), bf(p["pool_w_out"][1])]]


_SMALL_SHARDED = ("pool_norm", "pool_scale", "mla_norm", "mla_q_norm", "mla_kv_norm", "conv_w")
_SMALL_REPLICATED = ("conv_norm", "final_norm")


def _pack_small(p, rows, with_replicated=False):
    parts = [p[k].reshape(-1) for k in _SMALL_SHARDED]
    if with_replicated:
        parts += [p[k].reshape(-1) for k in _SMALL_REPLICATED]
    flat = jnp.concatenate(parts)
    return jnp.pad(flat, (0, rows * 128 - flat.shape[0])).reshape(rows, 128)


_SMALL_SHARD_SHAPES = dict(pool_norm=(2, 128), pool_scale=(2, 256), mla_norm=(1, 128), mla_q_norm=(1, 48),
                           mla_kv_norm=(1, 32), conv_w=(1, 3, 256), conv_norm=(1, 1024), final_norm=(1024,))


def _unpack_small(buf, with_replicated=False):
    flat = buf.reshape(-1)
    out, off = {}, 0
    for k in _SMALL_SHARDED + (_SMALL_REPLICATED if with_replicated else ()):
        shp = _SMALL_SHARD_SHAPES[k]
        n = 1
        for d in shp:
            n *= d
        out[k] = flat[off:off + n].reshape(shp)
        off += n
    return out


def _small_views(gsmall):
    flat = gsmall.reshape(N_DEV, -1)

    def cols(off, rows, width):
        return flat[:, off:off + rows * width].reshape(N_DEV, rows, width).transpose(1, 0, 2).reshape(rows, N_DEV * width)

    return dict(pool_norm=cols(0, 2, 128), pool_scale=cols(256, 2, 256), mla_norm=cols(768, 1, 128)[0],
                q_norm=cols(896, 1, 48)[0], kv_norm=cols(944, 1, 32)[0], conv_w=cols(976, 3, 256))


def _layer_weights(layer, bufs, small, conv_norm):
    if layer in (0, 3):
        l = 0 if layer == 0 else 1
        return dict(g_in=bufs[0], norm=small["pool_norm"][l], scale=small["pool_scale"][l])
    if layer == 1:
        return dict(g_in=bufs[0], g_out=bufs[1], norm=conv_norm.reshape(D_MODEL), conv_w=small["conv_w"])
    g344, g384, g512, g1024 = bufs
    w_in = g344.transpose(1, 0, 2).reshape(D_MODEL, N_DEV * 344)
    return dict(
        g512=g512, g1024=g1024,
        w_q=w_in[:, :Q_RANK], w_kv=w_in[:, Q_RANK:Q_RANK + KV_RANK],
        w_kr=w_in[:, Q_RANK + KV_RANK:Q_RANK + KV_RANK + QK_ROPE], w_z=w_in[:, Q_RANK + KV_RANK + QK_ROPE:],
        w_qh=g384.reshape(N_DEV, Q_RANK, 2, QK_DIM).transpose(0, 2, 1, 3).reshape(N_HEADS, Q_RANK, QK_DIM),
        norm=small["mla_norm"], q_norm=small["q_norm"], kv_norm=small["kv_norm"])


def _grad_group(layer, g):
    if layer in (0, 3):
        return [g["g_in"], g["g_grp"], g["g_out"]]
    if layer == 1:
        return [g["g_in"], g["g_out"]]
    return [g["s344"], g["s384"], g["s512"], g["s1024"]]


def _pack_small_grads(g):
    def split(a, rows, width):
        return a.reshape(rows, N_DEV, width).transpose(1, 0, 2).reshape(N_DEV, rows * width)

    rep = lambda a: jnp.broadcast_to(a.reshape(1, -1), (N_DEV, a.size))
    flat = jnp.concatenate([
        split(jnp.stack([g[0]["norm"], g[3]["norm"]]), 2, 128), split(jnp.stack([g[0]["scale"], g[3]["scale"]]), 2, 256),
        split(g[2]["norm"], 1, 128), split(g[2]["q_norm"], 1, 48), split(g[2]["kv_norm"], 1, 32), split(g[1]["conv_w"], 3, 256),
        rep(g[1]["norm"]), rep(g[0]["final_norm"])], axis=1)
    return jnp.pad(flat, ((0, 0), (0, SMALL_ROWS_RS * 128 - flat.shape[1]))).reshape(N_DEV, SMALL_ROWS_RS, 128)


def _peers(x, y, c):
    for k in range(1, N_DEV):
        px = 1 - x if k & 4 else x
        py = 1 - y if k & 2 else y
        pc = 1 - c if k & 1 else c
        yield k - 1, (px, py, pc), 4 * px + 2 * py + pc


def _remote_copies(srcs, lands, send_sems, recv_sems, gather):
    x, y, c = lax.axis_index("x"), lax.axis_index("y"), lax.axis_index("c")
    me = 4 * x + 2 * y + c
    copies = []
    for k, peer, pidx in _peers(x, y, c):
        for a, (src, land) in enumerate(zip(srcs, lands)):
            copies.append(pltpu.make_async_remote_copy(
                src_ref=src if gather else src.at[pidx], dst_ref=land.at[me],
                send_sem=send_sems.at[a * (N_DEV - 1) + k], recv_sem=recv_sems.at[a * (N_DEV - 1) + k],
                device_id=peer, device_id_type=pl.DeviceIdType.MESH))
    return copies


_HBM = pl.BlockSpec(memory_space=pltpu.HBM)
_SEM = pl.BlockSpec(memory_space=pltpu.SEMAPHORE)
_EFFECT = pltpu.SideEffectType.DATAFLOW_SIDE_EFFECTING


def _own_slabs(name, arrays, gather, dep):
    n, nd = len(arrays), len(dep)
    me = (4 * lax.axis_index("x") + 2 * lax.axis_index("y") + lax.axis_index("c")).astype(jnp.int32).reshape(1)

    def body(me_ref, *refs):
        for a in range(n):
            refs[n + nd + a][...] = refs[a][...]

    def slab(shape):
        return pl.BlockSpec((None,) + tuple(shape), lambda i, me_ref: (me_ref[0],) + (0,) * len(shape))

    def whole(shape):
        return pl.BlockSpec(tuple(shape), lambda i, me_ref: (0,) * len(shape))

    outs = [_sds(((N_DEV,) + a.shape) if gather else a.shape, a.dtype) for a in arrays]
    grid_spec = pltpu.PrefetchScalarGridSpec(
        num_scalar_prefetch=1, grid=(1,),
        in_specs=[whole(a.shape) if gather else slab(a.shape[1:]) for a in arrays] + [_ANY] * nd,
        out_specs=[slab(o.shape[1:]) for o in outs])
    return pl.pallas_call(body, name=name, grid_spec=grid_spec, out_shape=outs, compiler_params=_params())(me, *arrays, *dep)


def _exchange_start(name, arrays, lands, gather):
    n = len(arrays)

    def body(*refs):
        srcs, lnds, send_sems, recv_sems, token = refs[:n], refs[n:2 * n], refs[2 * n], refs[2 * n + 1], refs[-1]
        for cp in _remote_copies(srcs, lnds, send_sems, recv_sems, gather):
            cp.start()
        token[...] = jnp.zeros(token.shape, F32)

    sems = pltpu.SemaphoreType.DMA((n * (N_DEV - 1),))
    thru = [pltpu.HBM(a.shape, a.dtype) for a in list(arrays) + list(lands)]
    res = pl.pallas_call(
        body, name=name, in_specs=[_HBM] * (2 * n),
        out_specs=[_SEM, _SEM] + [_HBM] * (2 * n) + [pl.BlockSpec(memory_space=pltpu.VMEM)],
        out_shape=[sems, sems] + thru + [_sds((8, 128), F32)],
        input_output_aliases={i: 2 + i for i in range(2 * n)},
        compiler_params=pltpu.CompilerParams(has_side_effects=_EFFECT),
    )(*[pltpu.with_memory_space_constraint(a, pltpu.HBM) for a in list(arrays) + list(lands)])
    return res[0], res[1], list(res[2:2 + n]), list(res[2 + n:2 + 2 * n]), res[-1]


def _exchange_wait(name, send_sems, recv_sems, arrays, lands, after, gather):
    n = len(arrays)
    n_after = len(after)

    def body(*refs):
        srcs, lnds = refs[:n], refs[n:2 * n]
        copies = _remote_copies(srcs, lnds, refs[2 * n], refs[2 * n + 1], gather)
        for cp in copies:
            cp.wait_send()
        for cp in copies:
            cp.wait_recv()

    thru = [pltpu.HBM(a.shape, a.dtype) for a in list(arrays) + list(lands)]
    res = pl.pallas_call(
        body, name=name, in_specs=[_HBM] * (2 * n) + [_SEM, _SEM] + [pl.BlockSpec(memory_space=pl.ANY)] * n_after,
        out_specs=[_HBM] * (2 * n), out_shape=thru, input_output_aliases={i: i for i in range(2 * n)},
        compiler_params=pltpu.CompilerParams(has_side_effects=_EFFECT),
    )(*arrays, *lands, send_sems, recv_sems, *after)
    return list(res[n:])


def _adamw_math(g, w, m, v):
    m = ADAM_B1 * m + (1.0 - ADAM_B1) * g
    v = ADAM_B2 * v + (1.0 - ADAM_B2) * (g * g)
    m_hat = m / (1.0 - ADAM_B1 ** ADAM_STEP)
    v_hat = v / (1.0 - ADAM_B2 ** ADAM_STEP)
    delta = -ADAM_LR * (m_hat / (jnp.sqrt(v_hat) + ADAM_EPS) + ADAM_WD * w)
    return delta, m, v


def _sum_adamw(name, recv, row_off, w, m, v, tr, layer=0):
    width = recv.shape[-1]
    w2, m2, v2 = (a.reshape(a.shape[0], -1, width) for a in (w, m, v))
    rows = w2.shape[1]
    base = row_off // tr

    def body(r_ref, w_ref, m_ref, v_ref, g_ref, d_ref, mo_ref, vo_ref):
        g = r_ref[0].astype(F32)
        for src in range(1, N_DEV):
            g = g + r_ref[src].astype(F32)
        delta, mn, vn = _adamw_math(g, w_ref[...], m_ref[...], v_ref[...])
        g_ref[...] = g
        d_ref[...] = delta
        mo_ref[...] = mn
        vo_ref[...] = vn

    blk = pl.BlockSpec((tr, width), lambda i: (i, 0))
    wblk = pl.BlockSpec((None, tr, width), lambda i: (layer, i, 0))
    return pl.pallas_call(
        body, name=name, grid=(rows // tr,),
        in_specs=[pl.BlockSpec((N_DEV, tr, width), lambda i: (0, base + i, 0)), wblk, wblk, wblk],
        out_specs=[blk] * 4, out_shape=[_sds((rows, width), F32)] * 4, compiler_params=_params())(recv, w2, m2, v2)


_WEIGHTS = ("pool_norm", "pool_w_in", "pool_w_grp", "pool_scale", "pool_w_out", "conv_norm", "conv_w_in", "conv_w", "conv_w_out",
            "mla_norm", "mla_w_in", "mla_q_norm", "mla_w_q_up", "mla_kv_norm", "mla_w_kv_up", "mla_w_out", "final_norm")


def _step(x, positions, loss_target, p, m, v):
    gathers, tokens, dep = [], [], []
    for group, arrays in enumerate(_pack_groups(p)):
        lands = _own_slabs(f"gather{group}_own", arrays, True, dep)
        ssem, rsem, arrays, lands, token = _exchange_start(f"gather{group}_start", arrays, lands, True)
        gathers.append((ssem, rsem, arrays, lands))
        tokens.append(token)
        dep = [token]
    state = {}

    def wait_group(group, after):
        return _exchange_wait(f"gather{group}_wait", *gathers[group], after, True)

    def get_w(layer, after):
        if layer == 0:
            bufs = wait_group(0, list(tokens))
            state["small"] = _small_views(bufs[1])
            rest = lambda later: dict(zip(("g_grp", "g_out"), wait_group(1, later)))
        else:
            bufs = wait_group(layer + 1, after)
            rest = lambda later: dict(g_grp=bufs[1], g_out=bufs[2])
        w = _layer_weights(layer, bufs, state["small"], p["conv_norm"])
        if layer in (0, 3):
            w["rest"] = rest
        return w

    scatters, small_grads = {}, {}

    def put_g(layer, g):
        if layer == 4:
            small_grads[0] = g
            arrays = [_pack_small_grads(small_grads)]
        else:
            small_grads[layer] = g
            arrays = _grad_group(layer, g)
        lands = _own_slabs(f"scatter{layer}_own", arrays, False, [])
        ssem, rsem, arrays, lands, token = _exchange_start(f"scatter{layer}_start", arrays, lands, False)
        scatters[layer] = (ssem, rsem, arrays, lands)
        tokens.append(token)
        return [token]

    loss, grad_x = _local_step(x[0], positions[0], loss_target[0], p["final_norm"], get_w, put_g)

    def adam(name, recv, row_off, key, tr, layer=0):
        return _sum_adamw("adam_" + name, recv, row_off, p[key], m[key], v[key], tr, layer)

    res, after = {}, [tokens[-1]]
    for layer in (3, 2, 1, 0):
        recv = _exchange_wait(f"scatter{layer}_wait", *scatters[layer], after, False)
        if layer in (0, 3):
            l = 0 if layer == 0 else 1
            res["pool_w_in", l] = adam(f"pool_w_in{l}", recv[0], 0, "pool_w_in", 256, l)
            res["pool_w_grp", l] = adam(f"pool_w_grp{l}", recv[1], 0, "pool_w_grp", 256, l)
            res["pool_w_out", l] = adam(f"pool_w_out{l}", recv[2], 0, "pool_w_out", 256, l)
            after = [res["pool_w_out", l][1]]
        elif layer == 1:
            res["conv_w_in", 0] = adam("conv_w_in", recv[0], 0, "conv_w_in", 256)
            res["conv_w_out", 0] = adam("conv_w_out", recv[1], 0, "conv_w_out", 256)
            after = [res["conv_w_out", 0][1]]
        else:
            res["mla_w_in", 0] = adam("mla_w_in", recv[0], 0, "mla_w_in", 256)
            res["mla_w_q_up", 0] = adam("mla_w_q_up", recv[1], 0, "mla_w_q_up", 384)
            res["mla_w_kv_up", 0] = adam("mla_w_kv_up", recv[2], 0, "mla_w_kv_up", 256)
            res["mla_w_out", 0] = adam("mla_w_out", recv[3], 0, "mla_w_out", 256)
            after = [res["mla_w_out", 0][1]]
    recv = _exchange_wait("scatter4_wait", *scatters[4], after, False)
    small = _sum_adamw("adam_small", recv[0], 0, _pack_small(p, SMALL_ROWS_RS, True)[None], _pack_small(m, SMALL_ROWS_RS, True)[None],
                       _pack_small(v, SMALL_ROWS_RS, True)[None], SMALL_ROWS_RS)
    small = [_unpack_small(a, True) for a in small]
    final = {k: tuple(part[k] for part in small) for k in _SMALL_SHARDED + _SMALL_REPLICATED}
    for k in _WEIGHTS:
        if k not in final:
            layers = [res[k, l] for l in range(p[k].shape[0])]
            final[k] = tuple(jnp.stack([lay[part] for lay in layers]).reshape(p[k].shape) for part in range(4))
    res = final

    loss = lax.psum(loss, ("x", "y", "c"))
    out = [loss, grad_x[None]]
    for part in range(4):
        out += [res[k][part] for k in _WEIGHTS]
    return tuple(out)


def kernel(x, positions, pool_norm, pool_w_in, pool_w_grp, pool_scale, pool_w_out, conv_norm, conv_w_in, conv_w, conv_w_out, mla_norm, mla_w_in, mla_q_norm, mla_w_q_up, mla_kv_norm, mla_w_kv_up, mla_w_out, final_norm, loss_target, m_pool_norm, m_pool_w_in, m_pool_w_grp, m_pool_scale, m_pool_w_out, m_conv_norm, m_conv_w_in, m_conv_w, m_conv_w_out, m_mla_norm, m_mla_w_in, m_mla_q_norm, m_mla_w_q_up, m_mla_kv_norm, m_mla_w_kv_up, m_mla_w_out, m_final_norm, v_pool_norm, v_pool_w_in, v_pool_w_grp, v_pool_scale, v_pool_w_out, v_conv_norm, v_conv_w_in, v_conv_w, v_conv_w_out, v_mla_norm, v_mla_w_in, v_mla_q_norm, v_mla_w_q_up, v_mla_kv_norm, v_mla_w_kv_up, v_mla_w_out, v_final_norm):
    p = dict(pool_norm=pool_norm, pool_w_in=pool_w_in, pool_w_grp=pool_w_grp, pool_scale=pool_scale, pool_w_out=pool_w_out,
             conv_norm=conv_norm, conv_w_in=conv_w_in, conv_w=conv_w, conv_w_out=conv_w_out, mla_norm=mla_norm, mla_w_in=mla_w_in,
             mla_q_norm=mla_q_norm, mla_w_q_up=mla_w_q_up, mla_kv_norm=mla_kv_norm, mla_w_kv_up=mla_w_kv_up, mla_w_out=mla_w_out,
             final_norm=final_norm)
    m = dict(pool_norm=m_pool_norm, pool_w_in=m_pool_w_in, pool_w_grp=m_pool_w_grp, pool_scale=m_pool_scale, pool_w_out=m_pool_w_out,
             conv_norm=m_conv_norm, conv_w_in=m_conv_w_in, conv_w=m_conv_w, conv_w_out=m_conv_w_out, mla_norm=m_mla_norm,
             mla_w_in=m_mla_w_in, mla_q_norm=m_mla_q_norm, mla_w_q_up=m_mla_w_q_up, mla_kv_norm=m_mla_kv_norm,
             mla_w_kv_up=m_mla_w_kv_up, mla_w_out=m_mla_w_out, final_norm=m_final_norm)
    v = dict(pool_norm=v_pool_norm, pool_w_in=v_pool_w_in, pool_w_grp=v_pool_w_grp, pool_scale=v_pool_scale, pool_w_out=v_pool_w_out,
             conv_norm=v_conv_norm, conv_w_in=v_conv_w_in, conv_w=v_conv_w, conv_w_out=v_conv_w_out, mla_norm=v_mla_norm,
             mla_w_in=v_mla_w_in, mla_q_norm=v_mla_q_norm, mla_w_q_up=v_mla_w_q_up, mla_kv_norm=v_mla_kv_norm,
             mla_w_kv_up=v_mla_w_kv_up, mla_w_out=v_mla_w_out, final_norm=v_final_norm)
    return _step(x, positions, loss_target, p, m, v)
```

```python
import functools

import jax
import jax.numpy as jnp
from jax import lax
from jax.experimental import pallas as pl
from jax.experimental.pallas import tpu as pltpu

BF = jnp.bfloat16
F32 = jnp.float32

N_DEV = 8
D_MODEL = 1024
D_INNER = 2048
POOL_WINDOWS = (2, 4, 8, 16)
POOL_GROUP = 512
N_HEADS = 16
QK_NOPE = 128
QK_ROPE = 64
QK_DIM = QK_NOPE + QK_ROPE
V_DIM = 128
Q_RANK = 384
KV_RANK = 256
ATTN_SCALE = QK_DIM ** -0.5
LOG2_E = 1.4426950408889634
LN_2 = 0.6931471805599453
Q_PRESCALE = ATTN_SCALE * LOG2_E
ATTN_TILE = 512
ATTN_HEADS_PER_STEP = 2
ROPE_BASE = 10000.0
NORM_EPS = 1e-6
NEG_BIG = -1e30

ADAM_LR = 0.001
ADAM_B1 = 0.9
ADAM_B2 = 0.999
ADAM_EPS = 1e-08
ADAM_WD = 0.01
ADAM_STEP = 10

VMEM_LIMIT_BYTES = 52 * 1024 * 1024
IN_PROJ_ROWS = 1024
POOL_HALO = 32
CONV_HALO = 16

NN = (((1,), (0,)), ((), ()))
NT = (((1,), (1,)), ((), ()))
TN = (((0,), (0,)), ((), ()))

SMALL_ROWS_AG = 16
SMALL_ROWS_RS = 32


def _sds(shape, dtype):
    return jax.ShapeDtypeStruct(tuple(shape), dtype)


def _params():
    return pltpu.CompilerParams(vmem_limit_bytes=VMEM_LIMIT_BYTES)


_ANY = pl.BlockSpec(memory_space=pl.ANY)


def _dot(a, b, dims):
    return lax.dot_general(a, b, dims, preferred_element_type=F32)


def _sig(z):
    return 1.0 / (1.0 + jnp.exp(-z))


def _silu_and_grad(z):
    sig = _sig(z)
    return z * sig, sig * (1.0 + z * (1.0 - sig))


def _rope_swap(x, p):
    pb = p.astype(BF)
    hi = x.astype(BF)
    r1 = x - hi.astype(F32)
    mid = r1.astype(BF)
    lo = (r1 - mid.astype(F32)).astype(BF)
    return (_dot(hi, pb, NN) + _dot(mid, pb, NN)) + _dot(lo, pb, NN)


def _rope_fwd(x, cosf, sinf, p):
    return x * cosf + _rope_swap(x, p) * sinf


def _rope_bwd(dy, cosf, sinf, p):
    return dy * cosf + _rope_swap(dy * sinf, p)


def _rms_bwd(dxn, x, g, res):
    r = lax.rsqrt(jnp.mean(x * x, axis=-1, keepdims=True) + NORM_EPS)
    v = dxn * g
    dx = r * v - x * ((r * r * r) * jnp.mean(v * x, axis=-1, keepdims=True))
    if res is not None:
        dx = dx + res
    dg = jnp.sum(dxn * (x * r), axis=0, keepdims=True)
    return dx, dg


def _accumulate(ref, val, step):
    @pl.when(step == 0)
    def _():
        ref[...] = val

    @pl.when(step > 0)
    def _():
        ref[...] += val


def _mm(name, grid, ins, in_specs, outs, out_specs, dims, epi, red=None, acc_shape=None):
    n_in, n_out = len(ins), len(outs)
    n_red = None if red is None else grid[red]

    def body(*refs):
        in_refs, out_refs = refs[:n_in], refs[n_in:n_in + n_out]
        pids = tuple(pl.program_id(ax) for ax in range(len(grid)))
        a, b = in_refs[0][...], in_refs[1][...]
        if a.ndim == 3:
            a = a.reshape(-1, a.shape[-1])
        if b.ndim == 3:
            b = b.reshape(-1, b.shape[-1])
        part = _dot(a.astype(BF), b.astype(BF), dims)
        if red is None:
            epi(part, in_refs[2:], out_refs, pids)
        else:
            acc = refs[n_in + n_out]
            k = pids[red]
            _accumulate(acc, part, k)

            @pl.when(k == n_red - 1)
            def _():
                epi(acc[...], in_refs[2:], out_refs, pids)

    scratch = [] if red is None else [pltpu.VMEM(acc_shape, F32)]
    return pl.pallas_call(body, name=name, grid=grid, in_specs=in_specs, out_specs=out_specs, out_shape=outs,
                          scratch_shapes=scratch, compiler_params=_params())(*ins)


def _store(part, extra, outs, pids):
    outs[0][...] = part.astype(outs[0].dtype)


def _rms_fwd(name, x, g, tm):
    s, d = x.shape

    def body(x_ref, g_ref, o_ref):
        xv = x_ref[...]
        r = lax.rsqrt(jnp.mean(xv * xv, axis=-1, keepdims=True) + NORM_EPS)
        o_ref[...] = ((xv * r) * g_ref[...]).astype(BF)

    return pl.pallas_call(body, name=name, grid=(s // tm,),
                          in_specs=[pl.BlockSpec((tm, d), lambda i: (i, 0)), pl.BlockSpec((1, d), lambda i: (0, 0))],
                          out_specs=pl.BlockSpec((tm, d), lambda i: (i, 0)), out_shape=_sds((s, d), BF),
                          compiler_params=_params())(x, g.reshape(1, d))


def _tn(name, a, b, out_shape, out_block, out_index, a_cols, b_cols, grid, a_index, b_index, dep=()):
    s = a.shape[-2]
    a_block = (s, a_cols) if a.ndim == 2 else (None, s, a_cols)
    b_block = (s, b_cols) if b.ndim == 2 else (None, s, b_cols)

    def epi(part, extra, outs, pids):
        outs[0][...] = part.astype(BF).reshape(outs[0].shape)

    return _mm(name, grid, [a, b] + list(dep), [pl.BlockSpec(a_block, a_index), pl.BlockSpec(b_block, b_index)] + [_ANY] * len(dep),
               [_sds(out_shape, BF)], [pl.BlockSpec(out_block, out_index)], TN, epi)[0]


def _pool_window_fwd(name, h, tm):
    s = h.shape[0]
    hb = POOL_HALO

    def body(u_ref, halo_ref, o_ref, e_ref, a_ref, b_ref):
        i = pl.program_id(0)
        row = lax.broadcasted_iota(jnp.int32, (tm, 1), 0) + i * tm
        for g, w in enumerate(POOL_WINDOWS):
            cs = slice(g * POOL_GROUP, (g + 1) * POOL_GROUP)
            e_ref[0:hb, :] = jnp.where(i > 0, halo_ref[:, cs].astype(F32), 0.0)
            e_ref[hb:, :] = u_ref[:, cs].astype(F32)
            src, bufs = e_ref, (a_ref, b_ref)
            for lv in range(1, w.bit_length()):
                dst, st, sh = bufs[(lv - 1) % 2], 8 * lv, 2 ** (lv - 1)
                n = hb + tm - st
                dst[st:, :] = src[st:, :] + src[pl.ds(st - sh, n), :]
                src = dst
            cnt = jnp.minimum(row + 1, w).astype(F32)
            o_ref[:, cs] = (src[hb:, :] / cnt - u_ref[:, cs].astype(F32)).astype(BF)

    per = tm // hb
    return pl.pallas_call(
        body, name=name, grid=(s // tm,),
        in_specs=[pl.BlockSpec((tm, D_INNER), lambda i: (i, 0)),
                  pl.BlockSpec((hb, D_INNER), lambda i: (jnp.maximum(i * per - 1, 0), 0))],
        out_specs=pl.BlockSpec((tm, D_INNER), lambda i: (i, 0)), out_shape=_sds((s, D_INNER), BF),
        scratch_shapes=[pltpu.VMEM((hb + tm, POOL_GROUP), F32)] * 3, compiler_params=_params())(h, h)


def _pool_window_bwd(name, dp, tm, dh):
    s = dp.shape[0]
    nt = s // tm
    hb = POOL_HALO

    def body(d_ref, halo_ref, dh_in_ref, o_ref, e_ref, a_ref, b_ref):
        i = pl.program_id(0)
        row = lax.broadcasted_iota(jnp.int32, (tm, 1), 0) + i * tm
        hrow = lax.broadcasted_iota(jnp.int32, (hb, 1), 0) + (i + 1) * tm
        for g, w in enumerate(POOL_WINDOWS):
            cs = slice(g * POOL_GROUP, (g + 1) * POOL_GROUP)
            e_ref[0:tm, :] = d_ref[:, cs] / jnp.minimum(row + 1, w).astype(F32)
            e_ref[tm:, :] = jnp.where(i < nt - 1, halo_ref[:, cs] / jnp.minimum(hrow + 1, w).astype(F32), 0.0)
            src, bufs = e_ref, (a_ref, b_ref)
            for lv in range(1, w.bit_length()):
                dst, sh = bufs[(lv - 1) % 2], 2 ** (lv - 1)
                n = tm + hb - 8 * lv
                dst[0:n, :] = src[0:n, :] + src[pl.ds(sh, n), :]
                src = dst
            o_ref[:, cs] = (src[0:tm, :] - d_ref[:, cs]).astype(BF)

    per = tm // hb
    last = s // hb - 1
    return pl.pallas_call(
        body, name=name, grid=(nt,),
        in_specs=[pl.BlockSpec((tm, D_INNER), lambda i: (i, 0)),
                  pl.BlockSpec((hb, D_INNER), lambda i: (jnp.minimum((i + 1) * per, last), 0)), _ANY],
        out_specs=pl.BlockSpec((tm, D_INNER), lambda i: (i, 0)), out_shape=_sds(dh.shape, BF),
        input_output_aliases={2: 0},
        scratch_shapes=[pltpu.VMEM((hb + tm, POOL_GROUP), F32)] * 3, compiler_params=_params())(dp, dp, dh)


def _grp_block():
    return pl.BlockSpec((N_DEV, 64, POOL_GROUP), lambda i, g: (0, g, 0))


def _pool_fwd(x, l, w, tm):
    s = x.shape[0]
    nt = s // tm
    n = f"pool{l}"
    xn = _rms_fwd(n + "_rms", x, w["norm"], tm)
    ti = IN_PROJ_ROWS if s % IN_PROJ_ROWS == 0 else tm
    (h,) = _mm(n + "_in", (s // ti, 8), [xn, w["g_in"]],
               [pl.BlockSpec((ti, D_MODEL), lambda i, j: (i, 0)), pl.BlockSpec((None, D_MODEL, 512), lambda i, j: (j, 0, 0))],
               [_sds((s, 2 * D_INNER), BF)], [pl.BlockSpec((ti, 512), lambda i, j: (i, j))], NN, _store)
    pooled = _pool_window_fwd(n + "_win", h, tm)
    w = dict(w, **w["rest"]([h]))

    def gate(part, extra, outs, pids):
        z = extra[0][...].astype(F32)
        outs[0][...] = ((part * extra[1][...]) * (z * _sig(z))).astype(BF)

    (gated,) = _mm(n + "_grp", (nt, 4), [pooled, w["g_grp"], h, w["scale"].reshape(1, D_INNER)],
                   [pl.BlockSpec((tm, 512), lambda i, g: (i, g)), _grp_block(),
                    pl.BlockSpec((tm, 512), lambda i, g: (i, 4 + g)), pl.BlockSpec((1, 512), lambda i, g: (0, g))],
                   [_sds((s, D_INNER), BF)], [pl.BlockSpec((tm, 512), lambda i, g: (i, g))], NN, gate)
    y = _out_proj(n + "_out", gated, w["g_out"], 0, x, tm)
    return y, dict(x=x, xn=xn, h=h, pooled=pooled, gated=gated, w=w)


def _out_proj(name, gated, g1024, row_block, x, tm):
    s = x.shape[0]

    def epi(part, extra, outs, pids):
        outs[0][...] = part + extra[0][...]

    return _mm(name, (s // tm, 2), [gated, g1024, x],
               [pl.BlockSpec((tm, D_INNER), lambda i, j: (i, 0)), pl.BlockSpec((N_DEV, 256, 512), lambda i, j: (0, row_block, j)),
                pl.BlockSpec((tm, 512), lambda i, j: (i, j))],
               [_sds((s, D_MODEL), F32)], [pl.BlockSpec((tm, 512), lambda i, j: (i, j))], NN, epi)[0]


def _w_out_nt_block(row_block):
    return pl.BlockSpec((2, 256, D_MODEL), lambda j, i: (j, row_block, 0))


def _in_proj_bwd(name, dh, wbuf, w_index, n_k, x, g, dy, tm, dep=()):
    s = x.shape[0]
    tm = IN_PROJ_ROWS if s % IN_PROJ_ROWS == 0 else tm

    def epi(acc, extra, outs, pids):
        dx, dg = _rms_bwd(acc, extra[0][...], extra[1][...], extra[2][...])
        outs[0][...] = dx
        outs[1][...] = dx.astype(BF)
        _accumulate(outs[2], dg, pids[0])

    row = lambda i, k: (i, 0)
    return _mm(name, (s // tm, n_k), [dh, wbuf, x, g.reshape(1, D_MODEL), dy] + list(dep),
               [pl.BlockSpec((tm, 512), lambda i, k: (i, k)), pl.BlockSpec((None, D_MODEL, 512), w_index),
                pl.BlockSpec((tm, D_MODEL), row), pl.BlockSpec((1, D_MODEL), lambda i, k: (0, 0)), pl.BlockSpec((tm, D_MODEL), row)]
               + [_ANY] * len(dep),
               [_sds((s, D_MODEL), F32), _sds((s, D_MODEL), BF), _sds((1, D_MODEL), F32)],
               [pl.BlockSpec((tm, D_MODEL), row), pl.BlockSpec((tm, D_MODEL), row), pl.BlockSpec((1, D_MODEL), lambda i, k: (0, 0))],
               NT, epi, red=1, acc_shape=(tm, D_MODEL))


def _w_out_grad(name, gated, dyb):
    s = gated.shape[0]
    return _tn(name, gated, dyb, (D_INNER, D_MODEL), (512, D_MODEL), lambda i: (i, 0), 512, D_MODEL, (4,),
               lambda i: (0, i), lambda i: (0, 0))


def _pool_bwd(dy, dyb, l, w, sv, tm, dep, early=None):
    s = dy.shape[0]
    nt = s // tm
    n = f"pool{l}b"
    h, pooled = sv["h"], sv["pooled"]
    scale = w["scale"].reshape(1, D_INNER)

    def gate_bwd(part, extra, outs, pids):
        z, sc = extra[0][...].astype(F32), extra[1][...]
        wg = extra[3][...].reshape(POOL_GROUP, POOL_GROUP)
        mpv = _dot(extra[2][...], wg, NN)
        sz, dsz = _silu_and_grad(z)
        dm = part * sz
        dmp = (dm * sc).astype(BF)
        outs[0][...] = dmp
        outs[1][...] = (part * (mpv * sc) * dsz).astype(BF)
        _accumulate(outs[2], jnp.sum(dm * mpv, axis=0, keepdims=True), pids[1])
        outs[3][...] = _dot(dmp, wg, NT)

    tile = lambda j, i: (i, j)
    dmp, dz, dscale, dpool = _mm(
        n + "_out", (4, nt), [dyb, w["g_out"], h, scale, pooled, w["g_grp"]] + dep,
        [pl.BlockSpec((tm, D_MODEL), lambda j, i: (i, 0)), _w_out_nt_block(0),
         pl.BlockSpec((tm, 512), lambda j, i: (i, 4 + j)), pl.BlockSpec((1, 512), lambda j, i: (0, j)), pl.BlockSpec((tm, 512), tile),
         pl.BlockSpec((N_DEV, 64, POOL_GROUP), lambda j, i: (0, j, 0))] + [_ANY] * len(dep),
        [_sds((s, D_INNER), BF), _sds((s, 2 * D_INNER), BF), _sds((1, D_INNER), F32), _sds((s, D_INNER), F32)],
        [pl.BlockSpec((tm, 512), tile), pl.BlockSpec((tm, 512), lambda j, i: (i, 4 + j)), pl.BlockSpec((1, 512), lambda j, i: (0, j)),
         pl.BlockSpec((tm, 512), tile)],
        NT, gate_bwd)
    g_out = _w_out_grad(n + "_gout", sv["gated"], dyb).reshape(N_DEV, 256, D_MODEL)
    g_grp = _tn(n + "_ggrp", pooled, dmp, (N_DEV, 256, 512), (N_DEV, 64, 512), lambda g: (0, g, 0),
                512, 512, (4,), lambda g: (0, g), lambda g: (0, g))
    dep = early(dict(g_grp=g_grp, g_out=g_out)) if early is not None else ()
    dh = _pool_window_bwd(n + "_win", dpool, tm, dz)
    g_in = _tn(n + "_gin", sv["xn"], dh, (N_DEV, D_MODEL, 512), (None, D_MODEL, 512), lambda j: (j, 0, 0),
               D_MODEL, 512, (8,), lambda j: (0, 0), lambda j: (0, j), dep)
    dep = early(dict(g_in=g_in)) if early is not None else ()
    dx, dxb, dnorm = _in_proj_bwd(n + "_in", dh, w["g_in"], lambda i, k: (k, 0, 0), 8, sv["x"], w["norm"], dy, tm, dep)
    return dx, dxb, dict(g_in=g_in, g_grp=g_grp, g_out=g_out, norm=dnorm[0], scale=dscale[0])


def _conv_in_index(i, j):
    return (j // 2, 0, j % 2)


def _conv_fwd(x, w, tm):
    s = x.shape[0]
    nt = s // tm
    xn = _rms_fwd("conv_rms", x, w["norm"], tm)
    ti = IN_PROJ_ROWS if s % IN_PROJ_ROWS == 0 else tm
    (h,) = _mm("conv_in", (s // ti, 16), [xn, w["g_in"]],
               [pl.BlockSpec((ti, D_MODEL), lambda i, j: (i, 0)), pl.BlockSpec((None, D_MODEL, 512), _conv_in_index)],
               [_sds((s, 4 * D_INNER), BF)], [pl.BlockSpec((ti, 512), lambda i, j: (i, j))], NN, _store)
    per = tm // CONV_HALO

    def body(b_ref, c_ref, h_ref, z_ref, cp_ref, hp_ref, w_ref, o_ref, e_ref):
        i = pl.program_id(0)
        ch = c_ref[...].astype(F32) * h_ref[...].astype(F32)
        e_ref[0:CONV_HALO, :] = jnp.where(i > 0, cp_ref[...].astype(F32) * hp_ref[...].astype(F32), 0.0)
        e_ref[CONV_HALO:, :] = ch
        co = (w_ref[2:3, :] * ch + w_ref[1:2, :] * e_ref[pl.ds(CONV_HALO - 1, tm), :]
              + w_ref[0:1, :] * e_ref[pl.ds(CONV_HALO - 2, tm), :])
        z = z_ref[...].astype(F32)
        o_ref[...] = ((b_ref[...].astype(F32) * co) * (z * _sig(z))).astype(BF)

    def col(q):
        return pl.BlockSpec((tm, 512), lambda i, j: (i, 4 * q + j))

    def prev(q):
        return pl.BlockSpec((CONV_HALO, 512), lambda i, j: (jnp.maximum(i * per - 1, 0), 4 * q + j))

    gated = pl.pallas_call(
        body, name="conv_mix", grid=(nt, 4),
        in_specs=[col(0), col(1), col(2), col(3), prev(1), prev(2), pl.BlockSpec((3, 512), lambda i, j: (0, j))],
        out_specs=pl.BlockSpec((tm, 512), lambda i, j: (i, j)), out_shape=_sds((s, D_INNER), BF),
        scratch_shapes=[pltpu.VMEM((CONV_HALO + tm, 512), F32)], compiler_params=_params())(h, h, h, h, h, h, w["conv_w"])
    w = dict(w, **w["rest"]([gated]))
    y = _out_proj("conv_out", gated, w["g_out"], 0, x, tm)
    return y, dict(x=x, xn=xn, h=h, gated=gated, w=w)


def _conv_bwd(dy, dyb, w, sv, tm, dep):
    s = dy.shape[0]
    nt = s // tm
    h = sv["h"]
    (dg,) = _mm("convb_out", (4, nt), [dyb, w["g_out"]] + dep,
                [pl.BlockSpec((tm, D_MODEL), lambda j, i: (i, 0)), _w_out_nt_block(0)] + [_ANY] * len(dep),
                [_sds((s, D_INNER), F32)], [pl.BlockSpec((tm, 512), lambda j, i: (i, j))], NT, _store)
    per = tm // CONV_HALO
    last = s // CONV_HALO - 1

    def body(dg_ref, b_ref, c_ref, h_ref, z_ref, cp_ref, hp_ref, dgn_ref, bn_ref, zn_ref, w_ref,
             dall_ref, dw_ref, e_ref, f_ref):
        db_ref, dc_ref, dh_ref, dz_ref = (dall_ref.at[:, q * 512:(q + 1) * 512] for q in range(4))
        i = pl.program_id(1)
        w0, w1, w2 = w_ref[0:1, :], w_ref[1:2, :], w_ref[2:3, :]
        c, hh, b = c_ref[...].astype(F32), h_ref[...].astype(F32), b_ref[...].astype(F32)
        ch = c * hh
        e_ref[0:CONV_HALO, :] = jnp.where(i > 0, cp_ref[...].astype(F32) * hp_ref[...].astype(F32), 0.0)
        e_ref[CONV_HALO:, :] = ch
        ch1 = e_ref[pl.ds(CONV_HALO - 1, tm), :]
        ch2 = e_ref[pl.ds(CONV_HALO - 2, tm), :]
        co = w2 * ch + w1 * ch1 + w0 * ch2
        sz, dsz = _silu_and_grad(z_ref[...].astype(F32))
        dgv = dg_ref[...]
        dyv = dgv * sz
        dz_ref[...] = (dgv * (b * co) * dsz).astype(BF)
        db_ref[...] = (dyv * co).astype(BF)
        dco = dyv * b
        zn = zn_ref[...].astype(F32)
        f_ref[0:tm, :] = dco
        f_ref[tm:, :] = jnp.where(i < nt - 1, dgn_ref[...] * (zn * _sig(zn)) * bn_ref[...].astype(F32), 0.0)
        dch = w2 * dco + w1 * f_ref[pl.ds(1, tm), :] + w0 * f_ref[pl.ds(2, tm), :]
        dc_ref[...] = (dch * hh).astype(BF)
        dh_ref[...] = (dch * c).astype(BF)
        for tap, shifted in enumerate((ch2, ch1, ch)):
            _accumulate(dw_ref.at[tap:tap + 1, :], jnp.sum(dco * shifted, axis=0, keepdims=True), i)

    def col(q):
        return pl.BlockSpec((tm, 512), lambda j, i: (i, 4 * q + j))

    def prev(q):
        return pl.BlockSpec((CONV_HALO, 512), lambda j, i: (jnp.maximum(i * per - 1, 0), 4 * q + j))

    def nxt(q):
        return pl.BlockSpec((CONV_HALO, 512), lambda j, i: (jnp.minimum((i + 1) * per, last), 4 * q + j))

    tile = pl.BlockSpec((tm, 512), lambda j, i: (i, j))
    wspec = pl.BlockSpec((3, 512), lambda j, i: (0, j))
    dh, dw = pl.pallas_call(
        body, name="convb_mix", grid=(4, nt),
        in_specs=[tile, col(0), col(1), col(2), col(3), prev(1), prev(2), nxt(0), nxt(0), nxt(3), wspec],
        out_specs=[pl.BlockSpec((tm, D_INNER), lambda j, i: (i, j)), wspec],
        out_shape=[_sds((s, 4 * D_INNER), BF), _sds((3, D_INNER), F32)],
        scratch_shapes=[pltpu.VMEM((CONV_HALO + tm, 512), F32)] * 2, compiler_params=_params(),
    )(dg, h, h, h, h, h, h, dg, h, h, w["conv_w"])

    def w_block(kp):
        k = 4 * (kp % 4) + kp // 4
        return (k // 2, 0, k % 2)

    dx, dxb, dnorm = _in_proj_bwd("convb_in", dh, w["g_in"], lambda i, kp: w_block(kp), 16, sv["x"], w["norm"], dy, tm)
    g_in = _tn("convb_gin", sv["xn"], dh, (N_DEV, D_MODEL, D_MODEL), (None, D_MODEL, 512), lambda j: (j // 2, 0, j % 2),
               D_MODEL, 512, (16,), lambda j: (0, 0), lambda j: (0, 4 * (j % 4) + j // 4))
    g_out = _w_out_grad("convb_gout", sv["gated"], dyb)
    return dx, dxb, dict(g_in=g_in, g_out=g_out.reshape(N_DEV, 256, D_MODEL), norm=dnorm[0], conv_w=dw)


def _attn_tiles(s):
    t = min(ATTN_TILE, s)
    return t, s // t


def _causal_keep(t, keys_on_rows):
    r = lax.broadcasted_iota(jnp.int32, (t, t), 0)
    c = lax.broadcasted_iota(jnp.int32, (t, t), 1)
    return (r <= c) if keys_on_rows else (c <= r)


def _mla_fwd(x, w, rope, tm):
    s = x.shape[0]
    nt = s // tm
    cosf, sinf, perm = rope
    xn = _rms_fwd("mla_rms", x, w["norm"], tm)

    def in_body(xn_ref, wq_ref, wkv_ref, wkr_ref, wz_ref, gq_ref, gkv_ref, cos_ref, sin_ref, p_ref,
                ql_ref, kvl_ref, qn_ref, kvn_ref, krr_ref, z_ref):
        xv = xn_ref[...]
        ql = _dot(xv, wq_ref[...], NN)
        kvl = _dot(xv, wkv_ref[...], NN)
        ql_ref[...] = ql
        kvl_ref[...] = kvl
        rq = lax.rsqrt(jnp.mean(ql * ql, axis=-1, keepdims=True) + NORM_EPS)
        qn_ref[...] = ((ql * rq) * gq_ref[...]).astype(BF)
        rkv = lax.rsqrt(jnp.mean(kvl * kvl, axis=-1, keepdims=True) + NORM_EPS)
        kvn_ref[...] = ((kvl * rkv) * gkv_ref[...]).astype(BF)
        kr = _dot(xv, wkr_ref[...], NN)
        krr_ref[...] = _rope_fwd(kr, cos_ref[...], sin_ref[...], p_ref[...]).astype(BF)
        z_ref[...] = _dot(xv, wz_ref[...], NN).astype(BF)

    def full(a):
        return pl.BlockSpec(a.shape, lambda i: (0,) * a.ndim)

    def rows(c):
        return pl.BlockSpec((tm, c), lambda i: (i, 0))

    gq, gkv = w["q_norm"].reshape(1, Q_RANK), w["kv_norm"].reshape(1, KV_RANK)
    q_lat, kv_lat, qn, kvn, krr, z = pl.pallas_call(
        in_body, name="mla_in", grid=(nt,),
        in_specs=[rows(D_MODEL), full(w["w_q"]), full(w["w_kv"]), full(w["w_kr"]), full(w["w_z"]), full(gq), full(gkv),
                  rows(QK_ROPE), rows(QK_ROPE), full(perm)],
        out_specs=[rows(Q_RANK), rows(KV_RANK), rows(Q_RANK), rows(KV_RANK), rows(QK_ROPE), rows(D_INNER)],
        out_shape=[_sds((s, Q_RANK), F32), _sds((s, KV_RANK), F32), _sds((s, Q_RANK), BF), _sds((s, KV_RANK), BF),
                   _sds((s, QK_ROPE), BF), _sds((s, D_INNER), BF)],
        compiler_params=_params())(xn, w["w_q"], w["w_kv"], w["w_kr"], w["w_z"], gq, gkv, cosf, sinf, perm)

    def q_epi(part, extra, outs, pids):
        outs[0][:, 0:QK_NOPE] = (part[:, 0:QK_NOPE] * Q_PRESCALE).astype(BF)
        roped = _rope_fwd(part[:, QK_NOPE:QK_DIM], extra[0][...], extra[1][...], extra[2][...])
        outs[0][:, QK_NOPE:QK_DIM] = (roped * Q_PRESCALE).astype(BF)

    tp = IN_PROJ_ROWS if s % IN_PROJ_ROWS == 0 else tm
    rope_row = pl.BlockSpec((tp, QK_ROPE), lambda h, i: (i, 0))
    (q,) = _mm("mla_qup", (N_HEADS, s // tp), [qn, w["w_qh"], cosf, sinf, perm],
               [pl.BlockSpec((tp, Q_RANK), lambda h, i: (i, 0)), pl.BlockSpec((None, Q_RANK, QK_DIM), lambda h, i: (h, 0, 0)),
                rope_row, rope_row, pl.BlockSpec((QK_ROPE, QK_ROPE), lambda h, i: (0, 0))],
               [_sds((N_HEADS, s, QK_DIM), BF)], [pl.BlockSpec((None, tp, QK_DIM), lambda h, i: (h, i, 0))], NN, q_epi)

    def kv_epi(part, extra, outs, pids):
        outs[0][:, 0:QK_NOPE] = part[:, 0:QK_NOPE].astype(BF)
        outs[0][:, QK_NOPE:QK_DIM] = extra[0][...]
        outs[1][...] = part[:, QK_NOPE:].astype(BF)

    k, v = _mm("mla_kvup", (N_HEADS, s // tp), [kvn, w["g512"], krr],
               [pl.BlockSpec((tp, KV_RANK), lambda h, i: (i, 0)),
                pl.BlockSpec((None, KV_RANK, 256), lambda h, i: (h // 2, 0, h % 2)), rope_row],
               [_sds((N_HEADS, s, QK_DIM), BF), _sds((N_HEADS, s, V_DIM), BF)],
               [pl.BlockSpec((None, tp, QK_DIM), lambda h, i: (h, i, 0)), pl.BlockSpec((None, tp, V_DIM), lambda h, i: (h, i, 0))],
               NN, kv_epi)

    t, nq = _attn_tiles(s)

    def attn_body(q_ref, k_ref, v_ref, z_ref, o_ref, g_ref, lse_ref):
        i = pl.program_id(1)

        def block(j, carry, masked):
            start = pl.multiple_of(j * t, t)
            out = []
            for hh, (m, lsum, acc) in enumerate(carry):
                sc = _dot(q_ref[hh], k_ref[hh, pl.ds(start, t), :], NT)
                if masked:
                    sc = jnp.where(_causal_keep(t, False), sc, NEG_BIG)
                mn = jnp.maximum(m, jnp.max(sc, axis=-1, keepdims=True))
                alpha = jnp.exp2(m - mn)
                p = jnp.exp2(sc - mn)
                lsum = alpha * lsum + jnp.sum(p, axis=-1, keepdims=True)
                acc = alpha * acc + _dot(p.astype(BF), v_ref[hh, pl.ds(start, t), :], NN)
                out.append((mn, lsum, acc))
            return tuple(out)

        init = ((jnp.full((t, 1), NEG_BIG, F32), jnp.zeros((t, 1), F32), jnp.zeros((t, V_DIM), F32)),) * ATTN_HEADS_PER_STEP
        carry = lax.fori_loop(0, i, lambda j, c: block(j, c, False), init)
        for hh, (m, lsum, acc) in enumerate(block(i, carry, True)):
            cols = slice(hh * V_DIM, (hh + 1) * V_DIM)
            o = acc / lsum
            z = z_ref[:, cols].astype(F32)
            o_ref[:, cols] = o
            g_ref[:, cols] = (o * (z * _sig(z))).astype(BF)
            lse_ref[hh] = m + jnp.log(lsum) * LOG2_E

    hp = ATTN_HEADS_PER_STEP
    head_col = pl.BlockSpec((t, hp * V_DIM), lambda h, i: (i, h))
    o, gated, lse = pl.pallas_call(
        attn_body, name="mla_attn", grid=(N_HEADS // hp, nq),
        in_specs=[pl.BlockSpec((hp, t, QK_DIM), lambda h, i: (h, i, 0)), pl.BlockSpec((hp, s, QK_DIM), lambda h, i: (h, 0, 0)),
                  pl.BlockSpec((hp, s, V_DIM), lambda h, i: (h, 0, 0)), head_col],
        out_specs=[head_col, head_col, pl.BlockSpec((hp, t, 1), lambda h, i: (h, i, 0))],
        out_shape=[_sds((s, D_INNER), F32), _sds((s, D_INNER), BF), _sds((N_HEADS, s, 1), F32)],
        compiler_params=_params())(q, k, v, z)
    y = _out_proj("mla_out", gated, w["g1024"], 0, x, tm)
    return y, dict(x=x, xn=xn, q_lat=q_lat, kv_lat=kv_lat, qn=qn, kvn=kvn, z=z, q=q, k=k, v=v, o=o, lse=lse, gated=gated)


def _mla_bwd(dy, dyb, w, sv, rope, tm, dep):
    s = dy.shape[0]
    nt = s // tm
    cosf, sinf, perm = rope
    t, nq = _attn_tiles(s)
    q, k, v, lse = sv["q"], sv["k"], sv["v"], sv["lse"]

    def gate_bwd(part, extra, outs, pids):
        z, o = extra[0][...].astype(F32), extra[1][...]
        sz, dsz = _silu_and_grad(z)
        do = part * sz
        outs[0][...] = do.astype(BF)
        outs[1][...] = (part * o * dsz).astype(BF)
        prod = do * o
        for hh in range(4):
            outs[2][hh] = jnp.sum(prod[:, hh * V_DIM:(hh + 1) * V_DIM], axis=-1, keepdims=True)

    tile = lambda j, i: (i, j)
    dob, dz, delta = _mm(
        "mlab_out", (4, nt), [dyb, w["g1024"], sv["z"], sv["o"]] + dep,
        [pl.BlockSpec((tm, D_MODEL), lambda j, i: (i, 0)), _w_out_nt_block(0),
         pl.BlockSpec((tm, 512), tile), pl.BlockSpec((tm, 512), tile)] + [_ANY] * len(dep),
        [_sds((s, D_INNER), BF), _sds((s, D_INNER), BF), _sds((N_HEADS, s, 1), F32)],
        [pl.BlockSpec((tm, 512), tile), pl.BlockSpec((tm, 512), tile), pl.BlockSpec((4, tm, 1), lambda j, i: (j, i, 0))],
        NT, gate_bwd)

    def attn_bwd_body(k_ref, v_ref, q_ref, do_ref, lse_ref, dl_ref, cos_ref, sin_ref, p_ref, dkv_ref, dkr_ref, dq_ref, dq_acc):
        j = pl.program_id(1)
        kb, vb = k_ref[...], v_ref[...]

        @pl.when(j == 0)
        def _():
            dq_acc[...] = jnp.zeros(dq_acc.shape, F32)

        def block(i, carry, masked):
            dk, dv = carry
            rows = pl.ds(pl.multiple_of(i * t, t), t)
            qb, dob_ = q_ref[rows, :], do_ref[rows, :]
            st = _dot(kb, qb, NT)
            if masked:
                st = jnp.where(_causal_keep(t, True), st, NEG_BIG)
            pt = jnp.exp2(st - lse_ref[i])
            dv = dv + _dot(pt.astype(BF), dob_, NN)
            dst = (pt * (_dot(vb, dob_, NT) - dl_ref[i])).astype(BF)
            dk = dk + _dot(dst, qb, NN)
            dq_acc[rows, :] += _dot(dst, kb, TN)
            return dk, dv

        carry = block(j, (jnp.zeros((t, QK_DIM), F32), jnp.zeros((t, V_DIM), F32)), True)
        dk, dv = lax.fori_loop(j + 1, nq, lambda i, c: block(i, c, False), carry)
        dk = dk * LN_2
        dkv_ref[:, 0:QK_NOPE] = dk[:, 0:QK_NOPE].astype(BF)
        dkv_ref[:, QK_NOPE:] = dv.astype(BF)
        dkr_ref[...] = dk[:, QK_NOPE:]

        @pl.when(j == nq - 1)
        def _():
            for c in range(nq):
                rows = slice(c * t, (c + 1) * t)
                dq = dq_acc[rows, :] * ATTN_SCALE
                dq_ref[rows, 0:QK_NOPE] = dq[:, 0:QK_NOPE].astype(BF)
                dq_ref[rows, QK_NOPE:] = _rope_bwd(dq[:, QK_NOPE:], cos_ref[rows, :], sin_ref[rows, :], p_ref[...]).astype(BF)

    row_stats = pl.BlockSpec((None, nq, 1, t), lambda h, j: (h, 0, 0, 0))
    seq_rope = pl.BlockSpec((s, QK_ROPE), lambda h, j: (0, 0))
    head_seq = pl.BlockSpec((None, s, QK_DIM), lambda h, j: (h, 0, 0))
    dkv, dkr_h, dq = pl.pallas_call(
        attn_bwd_body, name="mlab_attn", grid=(N_HEADS, nq),
        in_specs=[pl.BlockSpec((None, t, QK_DIM), lambda h, j: (h, j, 0)), pl.BlockSpec((None, t, V_DIM), lambda h, j: (h, j, 0)),
                  head_seq, pl.BlockSpec((s, V_DIM), lambda h, j: (0, h)), row_stats, row_stats, seq_rope, seq_rope,
                  pl.BlockSpec((QK_ROPE, QK_ROPE), lambda h, j: (0, 0))],
        out_specs=[pl.BlockSpec((t, 2 * V_DIM), lambda h, j: (j, h)), pl.BlockSpec((None, t, QK_ROPE), lambda h, j: (h, j, 0)), head_seq],
        out_shape=[_sds((s, N_HEADS * 2 * V_DIM), BF), _sds((N_HEADS, s, QK_ROPE), F32), _sds((N_HEADS, s, QK_DIM), BF)],
        scratch_shapes=[pltpu.VMEM((s, QK_DIM), F32)],
        compiler_params=_params())(k, v, q, dob, lse.reshape(N_HEADS, nq, 1, t), delta.reshape(N_HEADS, nq, 1, t), cosf, sinf, perm)

    def dkr_body(d_ref, cos_ref, sin_ref, p_ref, o_ref):
        tot = d_ref[0]
        for hh in range(1, N_HEADS):
            tot = tot + d_ref[hh]
        o_ref[...] = _rope_bwd(tot, cos_ref[...], sin_ref[...], p_ref[...]).astype(BF)

    r64 = pl.BlockSpec((tm, QK_ROPE), lambda i: (i, 0))
    dkr = pl.pallas_call(
        dkr_body, name="mlab_dkr", grid=(nt,),
        in_specs=[pl.BlockSpec((N_HEADS, tm, QK_ROPE), lambda i: (0, i, 0)), r64, r64, pl.BlockSpec((QK_ROPE, QK_ROPE), lambda i: (0, 0))],
        out_specs=r64, out_shape=_sds((s, QK_ROPE), BF), compiler_params=_params())(dkr_h, cosf, sinf, perm)

    def lat_epi(acc, extra, outs, pids):
        dx, dg = _rms_bwd(acc, extra[0][...], extra[1][...], None)
        outs[0][...] = dx.astype(BF)
        _accumulate(outs[1], dg, pids[0])

    tp = IN_PROJ_ROWS if s % IN_PROJ_ROWS == 0 else tm

    def lat_bwd(name, a, a_spec, b, b_spec, n_k, lat, g, rank):
        row = lambda i, k: (i, 0)
        one = lambda i, k: (0, 0)
        return _mm(name, (s // tp, n_k), [a, b, lat, g.reshape(1, rank)],
                   [a_spec, b_spec, pl.BlockSpec((tp, rank), row), pl.BlockSpec((1, rank), one)],
                   [_sds((s, rank), BF), _sds((1, rank), F32)], [pl.BlockSpec((tp, rank), row), pl.BlockSpec((1, rank), one)],
                   NT, lat_epi, red=1, acc_shape=(tp, rank))

    d_ql, g_qnorm = lat_bwd("mlab_qup", dq, pl.BlockSpec((None, tp, QK_DIM), lambda i, h: (h, i, 0)),
                            w["w_qh"], pl.BlockSpec((None, Q_RANK, QK_DIM), lambda i, h: (h, 0, 0)), N_HEADS,
                            sv["q_lat"], w["q_norm"], Q_RANK)
    d_kvl, g_kvnorm = lat_bwd("mlab_kvup", dkv, pl.BlockSpec((tp, 512), lambda i, kk: (i, kk)),
                              w["g512"], pl.BlockSpec((None, KV_RANK, 512), lambda i, kk: (kk, 0, 0)), N_DEV,
                              sv["kv_lat"], w["kv_norm"], KV_RANK)

    def in_bwd(dql_ref, dkvl_ref, dkr_ref, dz_ref, wq_ref, wkv_ref, wkr_ref, wz_ref, x_ref, g_ref, dy_ref, dx_ref, dxb_ref, dg_ref):
        acc = (_dot(dql_ref[...], wq_ref[...], NT) + _dot(dkvl_ref[...], wkv_ref[...], NT)
               + _dot(dkr_ref[...], wkr_ref[...], NT) + _dot(dz_ref[...], wz_ref[...], NT))
        dx, dg = _rms_bwd(acc, x_ref[...], g_ref[...], dy_ref[...])
        dx_ref[...] = dx
        dxb_ref[...] = dx.astype(BF)
        _accumulate(dg_ref, dg, pl.program_id(0))

    def full(a):
        return pl.BlockSpec(a.shape, lambda i: (0,) * a.ndim)

    def rows(c):
        return pl.BlockSpec((tm, c), lambda i: (i, 0))

    gm = w["norm"].reshape(1, D_MODEL)
    dx, dxb, g_norm = pl.pallas_call(
        in_bwd, name="mlab_in", grid=(nt,),
        in_specs=[rows(Q_RANK), rows(KV_RANK), rows(QK_ROPE), rows(D_INNER), full(w["w_q"]), full(w["w_kv"]), full(w["w_kr"]),
                  full(w["w_z"]), rows(D_MODEL), full(gm), rows(D_MODEL)],
        out_specs=[rows(D_MODEL), rows(D_MODEL), full(gm)],
        out_shape=[_sds((s, D_MODEL), F32), _sds((s, D_MODEL), BF), _sds((1, D_MODEL), F32)],
        compiler_params=_params())(d_ql, d_kvl, dkr, dz, w["w_q"], w["w_kv"], w["w_kr"], w["w_z"], sv["x"], gm, dy)

    xn = sv["xn"]
    one = lambda j: (0, 0)
    g_q = _tn("mlab_gq", xn, d_ql, (D_MODEL, Q_RANK), (D_MODEL, Q_RANK), one, D_MODEL, Q_RANK, (1,), one, one)
    g_kv = _tn("mlab_gkv", xn, d_kvl, (D_MODEL, KV_RANK), (D_MODEL, KV_RANK), one, D_MODEL, KV_RANK, (1,), one, one)
    g_kr = _tn("mlab_gkr", xn, dkr, (D_MODEL, QK_ROPE), (D_MODEL, QK_ROPE), one, D_MODEL, QK_ROPE, (1,), one, one)
    g_z = _tn("mlab_gz", xn, dz, (D_MODEL, D_INNER), (D_MODEL, 512), lambda j: (0, j), D_MODEL, 512, (4,), one, lambda j: (0, j))
    g_in = jnp.concatenate([g_q, g_kv, g_kr, g_z], axis=1)
    g_qh = _tn("mlab_gqup", sv["qn"], dq, (N_HEADS, Q_RANK, QK_DIM), (None, Q_RANK, QK_DIM), lambda h: (h, 0, 0),
               Q_RANK, QK_DIM, (N_HEADS,), lambda h: (0, 0), lambda h: (h, 0, 0))
    g_kvup = _tn("mlab_gkvup", sv["kvn"], dkv, (N_DEV, KV_RANK, 512), (None, KV_RANK, 512), lambda j: (j, 0, 0),
                 KV_RANK, 512, (N_DEV,), lambda j: (0, 0), lambda j: (0, j))
    g_out = _w_out_grad("mlab_gout", sv["gated"], dyb)
    s384 = g_qh.reshape(N_DEV, 2, Q_RANK, QK_DIM).transpose(0, 2, 1, 3).reshape(N_DEV, Q_RANK, 2 * QK_DIM)
    s344 = g_in.reshape(D_MODEL, N_DEV, 344).transpose(1, 0, 2)
    return dx, dxb, dict(s344=s344, s384=s384, s512=g_kvup, s1024=g_out.reshape(N_DEV, 256, D_MODEL),
                         norm=g_norm[0], q_norm=g_qnorm[0], kv_norm=g_kvnorm[0])


def _loss_head(x, g, target, tm):
    s, d = x.shape

    def body(x_ref, g_ref, t_ref, dx_ref, dxb_ref, dg_ref, loss_ref):
        i = pl.program_id(0)
        xv, gv = x_ref[...], g_ref[...]
        r = lax.rsqrt(jnp.mean(xv * xv, axis=-1, keepdims=True) + NORM_EPS)
        err = (xv * r) * gv - t_ref[...]
        part = 0.5 * jnp.sum(jnp.mean(err * err, axis=-1, keepdims=True), axis=0, keepdims=True)
        dx, dg = _rms_bwd(err * (1.0 / d), xv, gv, None)
        dx_ref[...] = dx
        dxb_ref[...] = dx.astype(BF)
        _accumulate(dg_ref, dg, i)
        _accumulate(loss_ref, jnp.broadcast_to(part, loss_ref.shape), i)

    row = pl.BlockSpec((tm, d), lambda i: (i, 0))
    one = pl.BlockSpec((1, d), lambda i: (0, 0))
    return pl.pallas_call(
        body, name="loss_head", grid=(s // tm,), in_specs=[row, one, row],
        out_specs=[row, row, one, pl.BlockSpec((8, 128), lambda i: (0, 0))],
        out_shape=[_sds((s, d), F32), _sds((s, d), BF), _sds((1, d), F32), _sds((8, 128), F32)],
        compiler_params=_params())(x, g.reshape(1, d), target)


def _rope_tables(pos):
    inv_freq = ROPE_BASE ** (-jnp.arange(0, QK_ROPE, 2, dtype=F32) / QK_ROPE)
    ang = pos.astype(F32)[:, None] * inv_freq
    cos, sin = jnp.cos(ang), jnp.sin(ang)
    idx = jnp.arange(QK_ROPE)
    perm = (idx[:, None] == (idx[None, :] + QK_ROPE // 2) % QK_ROPE).astype(F32)
    return jnp.concatenate([cos, cos], axis=1), jnp.concatenate([-sin, sin], axis=1), perm


def _local_step(x, pos, target, final_norm, get_w, put_g):
    s = x.shape[0]
    tm = min(512, s)
    rope = _rope_tables(pos)
    w0 = get_w(0, [])
    x1, sv0 = _pool_fwd(x, 0, w0, tm)
    w1 = get_w(1, [x1])
    x2, sv1 = _conv_fwd(x1, w1, tm)
    w2 = get_w(2, [x2])
    x3, sv2 = _mla_fwd(x2, w2, rope, tm)
    w3 = get_w(3, [x3])
    x4, sv3 = _pool_fwd(x3, 1, w3, tm)
    d4, d4b, g_final, loss = _loss_head(x4, final_norm, target, tm)
    d3, d3b, gp1 = _pool_bwd(d4, d4b, 1, sv3["w"], sv3, tm, [])
    dep = put_g(3, gp1)
    d2, d2b, gm = _mla_bwd(d3, d3b, w2, sv2, rope, tm, dep)
    dep = put_g(2, gm)
    d1, d1b, gc = _conv_bwd(d2, d2b, sv1["w"], sv1, tm, dep)
    dep = put_g(1, gc)
    d0, _, gp0 = _pool_bwd(d1, d1b, 0, sv0["w"], sv0, tm, dep, early=lambda big: put_g(0, big))
    put_g(4, dict(gp0, final_norm=g_final[0]))
    return loss[0, 0], d0


def _pack_groups(p):
    bf = lambda a: a.astype(BF)
    grp = lambda l: bf(p["pool_w_grp"][l].reshape(4 * 64, POOL_GROUP))
    return [[bf(p["pool_w_in"][0]), _pack_small(p, SMALL_ROWS_AG)],
            [grp(0), bf(p["pool_w_out"][0])],
            [bf(p["conv_w_in"][0])],
            [bf(p["conv_w_out"][0])],
            [bf(p[k][0]) for k in ("mla_w_in", "mla_w_q_up", "mla_w_kv_up", "mla_w_out")],
            [bf(p["pool_w_in"][1]), grp(1), bf(p["pool_w_out"][1])]]


_SMALL_SHARDED = ("pool_norm", "pool_scale", "mla_norm", "mla_q_norm", "mla_kv_norm", "conv_w")
_SMALL_REPLICATED = ("conv_norm", "final_norm")


def _pack_small(p, rows, with_replicated=False):
    parts = [p[k].reshape(-1) for k in _SMALL_SHARDED]
    if with_replicated:
        parts += [p[k].reshape(-1) for k in _SMALL_REPLICATED]
    flat = jnp.concatenate(parts)
    return jnp.pad(flat, (0, rows * 128 - flat.shape[0])).reshape(rows, 128)


_SMALL_SHARD_SHAPES = dict(pool_norm=(2, 128), pool_scale=(2, 256), mla_norm=(1, 128), mla_q_norm=(1, 48),
                           mla_kv_norm=(1, 32), conv_w=(1, 3, 256), conv_norm=(1, 1024), final_norm=(1024,))


def _unpack_small(buf, with_replicated=False):
    flat = buf.reshape(-1)
    out, off = {}, 0
    for k in _SMALL_SHARDED + (_SMALL_REPLICATED if with_replicated else ()):
        shp = _SMALL_SHARD_SHAPES[k]
        n = 1
        for d in shp:
            n *= d
        out[k] = flat[off:off + n].reshape(shp)
        off += n
    return out


def _small_views(gsmall):
    flat = gsmall.reshape(N_DEV, -1)

    def cols(off, rows, width):
        return flat[:, off:off + rows * width].reshape(N_DEV, rows, width).transpose(1, 0, 2).reshape(rows, N_DEV * width)

    return dict(pool_norm=cols(0, 2, 128), pool_scale=cols(256, 2, 256), mla_norm=cols(768, 1, 128)[0],
                q_norm=cols(896, 1, 48)[0], kv_norm=cols(944, 1, 32)[0], conv_w=cols(976, 3, 256))


def _layer_weights(layer, bufs, small, conv_norm):
    if layer in (0, 3):
        l = 0 if layer == 0 else 1
        return dict(g_in=bufs[0], norm=small["pool_norm"][l], scale=small["pool_scale"][l])
    if layer == 1:
        return dict(g_in=bufs[0], norm=conv_norm.reshape(D_MODEL), conv_w=small["conv_w"])
    g344, g384, g512, g1024 = bufs
    w_in = g344.transpose(1, 0, 2).reshape(D_MODEL, N_DEV * 344)
    return dict(
        g512=g512, g1024=g1024,
        w_q=w_in[:, :Q_RANK], w_kv=w_in[:, Q_RANK:Q_RANK + KV_RANK],
        w_kr=w_in[:, Q_RANK + KV_RANK:Q_RANK + KV_RANK + QK_ROPE], w_z=w_in[:, Q_RANK + KV_RANK + QK_ROPE:],
        w_qh=g384.reshape(N_DEV, Q_RANK, 2, QK_DIM).transpose(0, 2, 1, 3).reshape(N_HEADS, Q_RANK, QK_DIM),
        norm=small["mla_norm"], q_norm=small["q_norm"], kv_norm=small["kv_norm"])


_GRAD_KEYS = {0: ("g_in", "g_grp", "g_out"), 3: ("g_in", "g_grp", "g_out"), 1: ("g_in", "g_out"), 2: ("s344", "s384", "s512", "s1024")}
_GRAD_PARAM = {0: dict(g_in="pool_w_in", g_grp="pool_w_grp", g_out="pool_w_out"), 1: dict(g_in="conv_w_in", g_out="conv_w_out"),
               2: dict(s344="mla_w_in", s384="mla_w_q_up", s512="mla_w_kv_up", s1024="mla_w_out")}
_GRAD_PARAM[3] = _GRAD_PARAM[0]


def _grad_group(layer, g):
    keys = tuple(k for k in _GRAD_KEYS[layer] if k in g)
    return keys, [g[k] for k in keys]


def _pack_small_grads(g):
    def split(a, rows, width):
        return a.reshape(rows, N_DEV, width).transpose(1, 0, 2).reshape(N_DEV, rows * width)

    rep = lambda a: jnp.broadcast_to(a.reshape(1, -1), (N_DEV, a.size))
    flat = jnp.concatenate([
        split(jnp.stack([g[0]["norm"], g[3]["norm"]]), 2, 128), split(jnp.stack([g[0]["scale"], g[3]["scale"]]), 2, 256),
        split(g[2]["norm"], 1, 128), split(g[2]["q_norm"], 1, 48), split(g[2]["kv_norm"], 1, 32), split(g[1]["conv_w"], 3, 256),
        rep(g[1]["norm"]), rep(g[0]["final_norm"])], axis=1)
    return jnp.pad(flat, ((0, 0), (0, SMALL_ROWS_RS * 128 - flat.shape[1]))).reshape(N_DEV, SMALL_ROWS_RS, 128)


def _peers(x, y, c):
    for k in range(1, N_DEV):
        px = 1 - x if k & 4 else x
        py = 1 - y if k & 2 else y
        pc = 1 - c if k & 1 else c
        yield k - 1, (px, py, pc), 4 * px + 2 * py + pc


def _remote_copies(srcs, lands, send_sems, recv_sems, gather):
    x, y, c = lax.axis_index("x"), lax.axis_index("y"), lax.axis_index("c")
    me = 4 * x + 2 * y + c
    copies = []
    for k, peer, pidx in _peers(x, y, c):
        for a, (src, land) in enumerate(zip(srcs, lands)):
            copies.append(pltpu.make_async_remote_copy(
                src_ref=src if gather else src.at[pidx], dst_ref=land.at[me],
                send_sem=send_sems.at[a * (N_DEV - 1) + k], recv_sem=recv_sems.at[a * (N_DEV - 1) + k],
                device_id=peer, device_id_type=pl.DeviceIdType.MESH))
    return copies


_HBM = pl.BlockSpec(memory_space=pltpu.HBM)
_SEM = pl.BlockSpec(memory_space=pltpu.SEMAPHORE)
_EFFECT = pltpu.SideEffectType.DATAFLOW_SIDE_EFFECTING


def _own_slabs(name, arrays, gather, dep):
    n, nd = len(arrays), len(dep)
    me = (4 * lax.axis_index("x") + 2 * lax.axis_index("y") + lax.axis_index("c")).astype(jnp.int32).reshape(1)

    def body(me_ref, *refs):
        for a in range(n):
            refs[n + nd + a][...] = refs[a][...]

    def slab(shape):
        return pl.BlockSpec((None,) + tuple(shape), lambda i, me_ref: (me_ref[0],) + (0,) * len(shape))

    def whole(shape):
        return pl.BlockSpec(tuple(shape), lambda i, me_ref: (0,) * len(shape))

    outs = [_sds(((N_DEV,) + a.shape) if gather else a.shape, a.dtype) for a in arrays]
    grid_spec = pltpu.PrefetchScalarGridSpec(
        num_scalar_prefetch=1, grid=(1,),
        in_specs=[whole(a.shape) if gather else slab(a.shape[1:]) for a in arrays] + [_ANY] * nd,
        out_specs=[slab(o.shape[1:]) for o in outs])
    return pl.pallas_call(body, name=name, grid_spec=grid_spec, out_shape=outs, compiler_params=_params())(me, *arrays, *dep)


def _exchange_start(name, arrays, lands, gather):
    n = len(arrays)

    def body(*refs):
        srcs, lnds, send_sems, recv_sems, token = refs[:n], refs[n:2 * n], refs[2 * n], refs[2 * n + 1], refs[-1]
        for cp in _remote_copies(srcs, lnds, send_sems, recv_sems, gather):
            cp.start()
        token[...] = jnp.zeros(token.shape, F32)

    sems = pltpu.SemaphoreType.DMA((n * (N_DEV - 1),))
    thru = [pltpu.HBM(a.shape, a.dtype) for a in list(arrays) + list(lands)]
    res = pl.pallas_call(
        body, name=name, in_specs=[_HBM] * (2 * n),
        out_specs=[_SEM, _SEM] + [_HBM] * (2 * n) + [pl.BlockSpec(memory_space=pltpu.VMEM)],
        out_shape=[sems, sems] + thru + [_sds((8, 128), F32)],
        input_output_aliases={i: 2 + i for i in range(2 * n)},
        compiler_params=pltpu.CompilerParams(has_side_effects=_EFFECT),
    )(*[pltpu.with_memory_space_constraint(a, pltpu.HBM) for a in list(arrays) + list(lands)])
    return res[0], res[1], list(res[2:2 + n]), list(res[2 + n:2 + 2 * n]), res[-1]


def _exchange_wait(name, send_sems, recv_sems, arrays, lands, after, gather):
    n = len(arrays)
    n_after = len(after)

    def body(*refs):
        srcs, lnds = refs[:n], refs[n:2 * n]
        copies = _remote_copies(srcs, lnds, refs[2 * n], refs[2 * n + 1], gather)
        for cp in copies:
            cp.wait_send()
        for cp in copies:
            cp.wait_recv()

    thru = [pltpu.HBM(a.shape, a.dtype) for a in list(arrays) + list(lands)]
    res = pl.pallas_call(
        body, name=name, in_specs=[_HBM] * (2 * n) + [_SEM, _SEM] + [pl.BlockSpec(memory_space=pl.ANY)] * n_after,
        out_specs=[_HBM] * (2 * n), out_shape=thru, input_output_aliases={i: i for i in range(2 * n)},
        compiler_params=pltpu.CompilerParams(has_side_effects=_EFFECT),
    )(*arrays, *lands, send_sems, recv_sems, *after)
    return list(res[n:])


def _adamw_math(g, w, m, v):
    m = ADAM_B1 * m + (1.0 - ADAM_B1) * g
    v = ADAM_B2 * v + (1.0 - ADAM_B2) * (g * g)
    m_hat = m / (1.0 - ADAM_B1 ** ADAM_STEP)
    v_hat = v / (1.0 - ADAM_B2 ** ADAM_STEP)
    delta = -ADAM_LR * (m_hat / (jnp.sqrt(v_hat) + ADAM_EPS) + ADAM_WD * w)
    return delta, m, v


def _sum_adamw(name, recv, row_off, w, m, v, tr, layer=0):
    width = recv.shape[-1]
    w2, m2, v2 = (a.reshape(a.shape[0], -1, width) for a in (w, m, v))
    rows = w2.shape[1]
    base = row_off // tr

    def body(r_ref, w_ref, m_ref, v_ref, g_ref, d_ref, mo_ref, vo_ref):
        g = r_ref[0].astype(F32)
        for src in range(1, N_DEV):
            g = g + r_ref[src].astype(F32)
        delta, mn, vn = _adamw_math(g, w_ref[...], m_ref[...], v_ref[...])
        g_ref[...] = g
        d_ref[...] = delta
        mo_ref[...] = mn
        vo_ref[...] = vn

    blk = pl.BlockSpec((tr, width), lambda i: (i, 0))
    wblk = pl.BlockSpec((None, tr, width), lambda i: (layer, i, 0))
    return pl.pallas_call(
        body, name=name, grid=(rows // tr,),
        in_specs=[pl.BlockSpec((N_DEV, tr, width), lambda i: (0, base + i, 0)), wblk, wblk, wblk],
        out_specs=[blk] * 4, out_shape=[_sds((rows, width), F32)] * 4, compiler_params=_params())(recv, w2, m2, v2)


_WEIGHTS = ("pool_norm", "pool_w_in", "pool_w_grp", "pool_scale", "pool_w_out", "conv_norm", "conv_w_in", "conv_w", "conv_w_out",
            "mla_norm", "mla_w_in", "mla_q_norm", "mla_w_q_up", "mla_kv_norm", "mla_w_kv_up", "mla_w_out", "final_norm")


def _step(x, positions, loss_target, p, m, v):
    gathers, tokens, dep = [], [], []
    for group, arrays in enumerate(_pack_groups(p)):
        lands = _own_slabs(f"gather{group}_own", arrays, True, dep)
        ssem, rsem, arrays, lands, token = _exchange_start(f"gather{group}_start", arrays, lands, True)
        gathers.append((ssem, rsem, arrays, lands))
        tokens.append(token)
        dep = [token]
    state = {}

    def wait_group(group, after):
        return _exchange_wait(f"gather{group}_wait", *gathers[group], after, True)

    def get_w(layer, after):
        if layer == 0:
            bufs = wait_group(0, list(tokens))
            state["small"] = _small_views(bufs[1])
            rest = lambda later: dict(zip(("g_grp", "g_out"), wait_group(1, later)))
        elif layer == 1:
            bufs = wait_group(2, after)
            rest = lambda later: dict(g_out=wait_group(3, later)[0])
        elif layer == 2:
            bufs = wait_group(4, after)
        else:
            bufs = wait_group(5, after)
            rest = lambda later: dict(g_grp=bufs[1], g_out=bufs[2])
        w = _layer_weights(layer, bufs, state["small"], p["conv_norm"])
        if layer != 2:
            w["rest"] = rest
        return w

    scatters, small_grads = [], {}

    def put_g(layer, g):
        if layer == 4:
            small_grads[0] = g
            keys, arrays = ("small",), [_pack_small_grads(small_grads)]
        else:
            small_grads[layer] = g
            keys, arrays = _grad_group(layer, g)
        n = len(scatters)
        lands = _own_slabs(f"scatter{n}_own", arrays, False, [])
        ssem, rsem, arrays, lands, token = _exchange_start(f"scatter{n}_start", arrays, lands, False)
        scatters.append((layer, keys, (ssem, rsem, arrays, lands)))
        tokens.append(token)
        return [token]

    loss, grad_x = _local_step(x[0], positions[0], loss_target[0], p["final_norm"], get_w, put_g)

    res, after = {}, [tokens[-1]]
    for n, (layer, keys, handles) in enumerate(scatters):
        recv = _exchange_wait(f"scatter{n}_wait", *handles, after, False)
        if layer == 4:
            break
        l = 1 if layer == 3 else 0
        for key, buf in zip(keys, recv):
            name = _GRAD_PARAM[layer][key]
            tr = min(256, buf.shape[1]) if name != "mla_w_q_up" else buf.shape[1]
            res[name, l] = _sum_adamw(f"adam_{name}{l}", buf, 0, p[name], m[name], v[name], tr, l)
        after = [res[name, l][1]]
    small = _sum_adamw("adam_small", recv[0], 0, _pack_small(p, SMALL_ROWS_RS, True)[None], _pack_small(m, SMALL_ROWS_RS, True)[None],
                       _pack_small(v, SMALL_ROWS_RS, True)[None], SMALL_ROWS_RS)
    small = [_unpack_small(a, True) for a in small]
    final = {k: tuple(part[k] for part in small) for k in _SMALL_SHARDED + _SMALL_REPLICATED}
    for k in _WEIGHTS:
        if k not in final:
            layers = [res[k, l] for l in range(p[k].shape[0])]
            final[k] = tuple(jnp.stack([lay[part] for lay in layers]).reshape(p[k].shape) for part in range(4))
    res = final

    loss = lax.psum(loss, ("x", "y", "c"))
    out = [loss, grad_x[None]]
    for part in range(4):
        out += [res[k][part] for k in _WEIGHTS]
    return tuple(out)


def kernel(x, positions, pool_norm, pool_w_in, pool_w_grp, pool_scale, pool_w_out, conv_norm, conv_w_in, conv_w, conv_w_out, mla_norm, mla_w_in, mla_q_norm, mla_w_q_up, mla_kv_norm, mla_w_kv_up, mla_w_out, final_norm, loss_target, m_pool_norm, m_pool_w_in, m_pool_w_grp, m_pool_scale, m_pool_w_out, m_conv_norm, m_conv_w_in, m_conv_w, m_conv_w_out, m_mla_norm, m_mla_w_in, m_mla_q_norm, m_mla_w_q_up, m_mla_kv_norm, m_mla_w_kv_up, m_mla_w_out, m_final_norm, v_pool_norm, v_pool_w_in, v_pool_w_grp, v_pool_scale, v_pool_w_out, v_conv_norm, v_conv_w_in, v_conv_w, v_conv_w_out, v_mla_norm, v_mla_w_in, v_mla_q_norm, v_mla_w_q_up, v_mla_kv_norm, v_mla_w_kv_up, v_mla_w_out, v_final_norm):
    p = dict(pool_norm=pool_norm, pool_w_in=pool_w_in, pool_w_grp=pool_w_grp, pool_scale=pool_scale, pool_w_out=pool_w_out,
             conv_norm=conv_norm, conv_w_in=conv_w_in, conv_w=conv_w, conv_w_out=conv_w_out, mla_norm=mla_norm, mla_w_in=mla_w_in,
             mla_q_norm=mla_q_norm, mla_w_q_up=mla_w_q_up, mla_kv_norm=mla_kv_norm, mla_w_kv_up=mla_w_kv_up, mla_w_out=mla_w_out,
             final_norm=final_norm)
    m = dict(pool_norm=m_pool_norm, pool_w_in=m_pool_w_in, pool_w_grp=m_pool_w_grp, pool_scale=m_pool_scale, pool_w_out=m_pool_w_out,
             conv_norm=m_conv_norm, conv_w_in=m_conv_w_in, conv_w=m_conv_w, conv_w_out=m_conv_w_out, mla_norm=m_mla_norm,
             mla_w_in=m_mla_w_in, mla_q_norm=m_mla_q_norm, mla_w_q_up=m_mla_w_q_up, mla_kv_norm=m_mla_kv_norm,
             mla_w_kv_up=m_mla_w_kv_up, mla_w_out=m_mla_w_out, final_norm=m_final_norm)
    v = dict(pool_norm=v_pool_norm, pool_w_in=v_pool_w_in, pool_w_grp=v_pool_w_grp, pool_scale=v_pool_scale, pool_w_out=v_pool_w_out,
             conv_norm=v_conv_norm, conv_w_in=v_conv_w_in, conv_w=v_conv_w, conv_w_out=v_conv_w_out, mla_norm=v_mla_norm,
             mla_w_in=v_mla_w_in, mla_q_norm=v_mla_q_norm, mla_w_q_up=v_mla_w_q_up, mla_kv_norm=v_mla_kv_norm,
             mla_w_kv_up=v_mla_w_kv_up, mla_w_out=v_mla_w_out, final_norm=v_final_norm)
    return _step(x, positions, loss_target, p, m, v)
```

```python
import functools

import jax
import jax.numpy as jnp
from jax import lax
from jax.experimental import pallas as pl
from jax.experimental.pallas import tpu as pltpu

BF = jnp.bfloat16
F32 = jnp.float32

N_DEV = 8
D_MODEL = 1024
D_INNER = 2048
POOL_WINDOWS = (2, 4, 8, 16)
POOL_GROUP = 512
N_HEADS = 16
QK_NOPE = 128
QK_ROPE = 64
QK_DIM = QK_NOPE + QK_ROPE
V_DIM = 128
Q_RANK = 384
KV_RANK = 256
ATTN_SCALE = QK_DIM ** -0.5
LOG2_E = 1.4426950408889634
LN_2 = 0.6931471805599453
Q_PRESCALE = ATTN_SCALE * LOG2_E
ATTN_TILE = 512
ATTN_HEADS_PER_STEP = 2
ROPE_BASE = 10000.0
NORM_EPS = 1e-6
NEG_BIG = -1e30

ADAM_LR = 0.001
ADAM_B1 = 0.9
ADAM_B2 = 0.999
ADAM_EPS = 1e-08
ADAM_WD = 0.01
ADAM_STEP = 10

VMEM_LIMIT_BYTES = 52 * 1024 * 1024
IN_PROJ_ROWS = 1024
POOL_HALO = 32
CONV_HALO = 16

NN = (((1,), (0,)), ((), ()))
NT = (((1,), (1,)), ((), ()))
TN = (((0,), (0,)), ((), ()))

SMALL_ROWS_AG = 16
SMALL_ROWS_RS = 32


def _sds(shape, dtype):
    return jax.ShapeDtypeStruct(tuple(shape), dtype)


def _params():
    return pltpu.CompilerParams(vmem_limit_bytes=VMEM_LIMIT_BYTES)


_ANY = pl.BlockSpec(memory_space=pl.ANY)


def _dot(a, b, dims):
    return lax.dot_general(a, b, dims, preferred_element_type=F32)


def _sig(z):
    return 1.0 / (1.0 + jnp.exp(-z))


def _silu_and_grad(z):
    sig = _sig(z)
    return z * sig, sig * (1.0 + z * (1.0 - sig))


def _rope_swap(x, p):
    pb = p.astype(BF)
    hi = x.astype(BF)
    r1 = x - hi.astype(F32)
    mid = r1.astype(BF)
    lo = (r1 - mid.astype(F32)).astype(BF)
    return (_dot(hi, pb, NN) + _dot(mid, pb, NN)) + _dot(lo, pb, NN)


def _rope_fwd(x, cosf, sinf, p):
    return x * cosf + _rope_swap(x, p) * sinf


def _rope_bwd(dy, cosf, sinf, p):
    return dy * cosf + _rope_swap(dy * sinf, p)


def _rms_bwd(dxn, x, g, res):
    r = lax.rsqrt(jnp.mean(x * x, axis=-1, keepdims=True) + NORM_EPS)
    v = dxn * g
    dx = r * v - x * ((r * r * r) * jnp.mean(v * x, axis=-1, keepdims=True))
    if res is not None:
        dx = dx + res
    dg = jnp.sum(dxn * (x * r), axis=0, keepdims=True)
    return dx, dg


def _accumulate(ref, val, step):
    @pl.when(step == 0)
    def _():
        ref[...] = val

    @pl.when(step > 0)
    def _():
        ref[...] += val


def _mm(name, grid, ins, in_specs, outs, out_specs, dims, epi, red=None, acc_shape=None):
    n_in, n_out = len(ins), len(outs)
    n_red = None if red is None else grid[red]

    def body(*refs):
        in_refs, out_refs = refs[:n_in], refs[n_in:n_in + n_out]
        pids = tuple(pl.program_id(ax) for ax in range(len(grid)))
        a, b = in_refs[0][...], in_refs[1][...]
        if a.ndim == 3:
            a = a.reshape(-1, a.shape[-1])
        if b.ndim == 3:
            b = b.reshape(-1, b.shape[-1])
        part = _dot(a.astype(BF), b.astype(BF), dims)
        if red is None:
            epi(part, in_refs[2:], out_refs, pids)
        else:
            acc = refs[n_in + n_out]
            k = pids[red]
            _accumulate(acc, part, k)

            @pl.when(k == n_red - 1)
            def _():
                epi(acc[...], in_refs[2:], out_refs, pids)

    scratch = [] if red is None else [pltpu.VMEM(acc_shape, F32)]
    return pl.pallas_call(body, name=name, grid=grid, in_specs=in_specs, out_specs=out_specs, out_shape=outs,
                          scratch_shapes=scratch, compiler_params=_params())(*ins)


def _store(part, extra, outs, pids):
    outs[0][...] = part.astype(outs[0].dtype)


def _rms_fwd(name, x, g, tm):
    s, d = x.shape

    def body(x_ref, g_ref, o_ref):
        xv = x_ref[...]
        r = lax.rsqrt(jnp.mean(xv * xv, axis=-1, keepdims=True) + NORM_EPS)
        o_ref[...] = ((xv * r) * g_ref[...]).astype(BF)

    return pl.pallas_call(body, name=name, grid=(s // tm,),
                          in_specs=[pl.BlockSpec((tm, d), lambda i: (i, 0)), pl.BlockSpec((1, d), lambda i: (0, 0))],
                          out_specs=pl.BlockSpec((tm, d), lambda i: (i, 0)), out_shape=_sds((s, d), BF),
                          compiler_params=_params())(x, g.reshape(1, d))


def _tn(name, a, b, out_shape, out_block, out_index, a_cols, b_cols, grid, a_index, b_index, dep=()):
    s = a.shape[-2]
    a_block = (s, a_cols) if a.ndim == 2 else (None, s, a_cols)
    b_block = (s, b_cols) if b.ndim == 2 else (None, s, b_cols)

    def epi(part, extra, outs, pids):
        outs[0][...] = part.astype(BF).reshape(outs[0].shape)

    return _mm(name, grid, [a, b] + list(dep), [pl.BlockSpec(a_block, a_index), pl.BlockSpec(b_block, b_index)] + [_ANY] * len(dep),
               [_sds(out_shape, BF)], [pl.BlockSpec(out_block, out_index)], TN, epi)[0]


def _pool_window_fwd(name, h, tm):
    s = h.shape[0]
    hb = POOL_HALO

    def body(u_ref, halo_ref, o_ref, e_ref, a_ref, b_ref):
        i = pl.program_id(0)
        row = lax.broadcasted_iota(jnp.int32, (tm, 1), 0) + i * tm
        for g, w in enumerate(POOL_WINDOWS):
            cs = slice(g * POOL_GROUP, (g + 1) * POOL_GROUP)
            e_ref[0:hb, :] = jnp.where(i > 0, halo_ref[:, cs].astype(F32), 0.0)
            e_ref[hb:, :] = u_ref[:, cs].astype(F32)
            src, bufs = e_ref, (a_ref, b_ref)
            for lv in range(1, w.bit_length()):
                dst, st, sh = bufs[(lv - 1) % 2], 8 * lv, 2 ** (lv - 1)
                n = hb + tm - st
                dst[st:, :] = src[st:, :] + src[pl.ds(st - sh, n), :]
                src = dst
            cnt = jnp.minimum(row + 1, w).astype(F32)
            o_ref[:, cs] = (src[hb:, :] / cnt - u_ref[:, cs].astype(F32)).astype(BF)

    per = tm // hb
    return pl.pallas_call(
        body, name=name, grid=(s // tm,),
        in_specs=[pl.BlockSpec((tm, D_INNER), lambda i: (i, 0)),
                  pl.BlockSpec((hb, D_INNER), lambda i: (jnp.maximum(i * per - 1, 0), 0))],
        out_specs=pl.BlockSpec((tm, D_INNER), lambda i: (i, 0)), out_shape=_sds((s, D_INNER), BF),
        scratch_shapes=[pltpu.VMEM((hb + tm, POOL_GROUP), F32)] * 3, compiler_params=_params())(h, h)


def _pool_window_bwd(name, dp, tm, dh):
    s = dp.shape[0]
    nt = s // tm
    hb = POOL_HALO

    def body(d_ref, halo_ref, dh_in_ref, o_ref, e_ref, a_ref, b_ref):
        i = pl.program_id(0)
        row = lax.broadcasted_iota(jnp.int32, (tm, 1), 0) + i * tm
        hrow = lax.broadcasted_iota(jnp.int32, (hb, 1), 0) + (i + 1) * tm
        for g, w in enumerate(POOL_WINDOWS):
            cs = slice(g * POOL_GROUP, (g + 1) * POOL_GROUP)
            e_ref[0:tm, :] = d_ref[:, cs] / jnp.minimum(row + 1, w).astype(F32)
            e_ref[tm:, :] = jnp.where(i < nt - 1, halo_ref[:, cs] / jnp.minimum(hrow + 1, w).astype(F32), 0.0)
            src, bufs = e_ref, (a_ref, b_ref)
            for lv in range(1, w.bit_length()):
                dst, sh = bufs[(lv - 1) % 2], 2 ** (lv - 1)
                n = tm + hb - 8 * lv
                dst[0:n, :] = src[0:n, :] + src[pl.ds(sh, n), :]
                src = dst
            o_ref[:, cs] = (src[0:tm, :] - d_ref[:, cs]).astype(BF)

    per = tm // hb
    last = s // hb - 1
    return pl.pallas_call(
        body, name=name, grid=(nt,),
        in_specs=[pl.BlockSpec((tm, D_INNER), lambda i: (i, 0)),
                  pl.BlockSpec((hb, D_INNER), lambda i: (jnp.minimum((i + 1) * per, last), 0)), _ANY],
        out_specs=pl.BlockSpec((tm, D_INNER), lambda i: (i, 0)), out_shape=_sds(dh.shape, BF),
        input_output_aliases={2: 0},
        scratch_shapes=[pltpu.VMEM((hb + tm, POOL_GROUP), F32)] * 3, compiler_params=_params())(dp, dp, dh)


def _grp_block():
    return pl.BlockSpec((N_DEV, 64, POOL_GROUP), lambda i, g: (0, g, 0))


def _pool_fwd(x, l, w, tm):
    s = x.shape[0]
    nt = s // tm
    n = f"pool{l}"
    xn = _rms_fwd(n + "_rms", x, w["norm"], tm)
    ti = IN_PROJ_ROWS if s % IN_PROJ_ROWS == 0 else tm
    (h,) = _mm(n + "_in", (s // ti, 8), [xn, w["g_in"]],
               [pl.BlockSpec((ti, D_MODEL), lambda i, j: (i, 0)), pl.BlockSpec((None, D_MODEL, 512), lambda i, j: (j, 0, 0))],
               [_sds((s, 2 * D_INNER), BF)], [pl.BlockSpec((ti, 512), lambda i, j: (i, j))], NN, _store)
    pooled = _pool_window_fwd(n + "_win", h, tm)
    w = dict(w, **w["rest"]([h]))

    def gate(part, extra, outs, pids):
        z = extra[0][...].astype(F32)
        outs[0][...] = ((part * extra[1][...]) * (z * _sig(z))).astype(BF)

    (gated,) = _mm(n + "_grp", (nt, 4), [pooled, w["g_grp"], h, w["scale"].reshape(1, D_INNER)],
                   [pl.BlockSpec((tm, 512), lambda i, g: (i, g)), _grp_block(),
                    pl.BlockSpec((tm, 512), lambda i, g: (i, 4 + g)), pl.BlockSpec((1, 512), lambda i, g: (0, g))],
                   [_sds((s, D_INNER), BF)], [pl.BlockSpec((tm, 512), lambda i, g: (i, g))], NN, gate)
    y = _out_proj(n + "_out", gated, w["g_out"], 0, x, tm)
    return y, dict(x=x, xn=xn, h=h, pooled=pooled, gated=gated, w=w)


def _out_proj(name, gated, g1024, row_block, x, tm):
    s = x.shape[0]

    def epi(part, extra, outs, pids):
        outs[0][...] = part + extra[0][...]

    return _mm(name, (s // tm, 2), [gated, g1024, x],
               [pl.BlockSpec((tm, D_INNER), lambda i, j: (i, 0)), pl.BlockSpec((N_DEV, 256, 512), lambda i, j: (0, row_block, j)),
                pl.BlockSpec((tm, 512), lambda i, j: (i, j))],
               [_sds((s, D_MODEL), F32)], [pl.BlockSpec((tm, 512), lambda i, j: (i, j))], NN, epi)[0]


def _w_out_nt_block(row_block):
    return pl.BlockSpec((2, 256, D_MODEL), lambda j, i: (j, row_block, 0))


def _in_proj_bwd(name, dh, wbuf, w_index, n_k, x, g, dy, tm, dep=()):
    s = x.shape[0]
    tm = IN_PROJ_ROWS if s % IN_PROJ_ROWS == 0 else tm

    def epi(acc, extra, outs, pids):
        dx, dg = _rms_bwd(acc, extra[0][...], extra[1][...], extra[2][...])
        outs[0][...] = dx
        outs[1][...] = dx.astype(BF)
        _accumulate(outs[2], dg, pids[0])

    row = lambda i, k: (i, 0)
    return _mm(name, (s // tm, n_k), [dh, wbuf, x, g.reshape(1, D_MODEL), dy] + list(dep),
               [pl.BlockSpec((tm, 512), lambda i, k: (i, k)), pl.BlockSpec((None, D_MODEL, 512), w_index),
                pl.BlockSpec((tm, D_MODEL), row), pl.BlockSpec((1, D_MODEL), lambda i, k: (0, 0)), pl.BlockSpec((tm, D_MODEL), row)]
               + [_ANY] * len(dep),
               [_sds((s, D_MODEL), F32), _sds((s, D_MODEL), BF), _sds((1, D_MODEL), F32)],
               [pl.BlockSpec((tm, D_MODEL), row), pl.BlockSpec((tm, D_MODEL), row), pl.BlockSpec((1, D_MODEL), lambda i, k: (0, 0))],
               NT, epi, red=1, acc_shape=(tm, D_MODEL))


def _w_out_grad(name, gated, dyb):
    s = gated.shape[0]
    return _tn(name, gated, dyb, (D_INNER, D_MODEL), (512, D_MODEL), lambda i: (i, 0), 512, D_MODEL, (4,),
               lambda i: (0, i), lambda i: (0, 0))


def _pool_bwd(dy, dyb, l, w, sv, tm, dep, early=None):
    s = dy.shape[0]
    nt = s // tm
    n = f"pool{l}b"
    h, pooled = sv["h"], sv["pooled"]
    scale = w["scale"].reshape(1, D_INNER)

    def gate_bwd(part, extra, outs, pids):
        z, sc = extra[0][...].astype(F32), extra[1][...]
        wg = extra[3][...].reshape(POOL_GROUP, POOL_GROUP)
        mpv = _dot(extra[2][...], wg, NN)
        sz, dsz = _silu_and_grad(z)
        dm = part * sz
        dmp = (dm * sc).astype(BF)
        outs[0][...] = dmp
        outs[1][...] = (part * (mpv * sc) * dsz).astype(BF)
        _accumulate(outs[2], jnp.sum(dm * mpv, axis=0, keepdims=True), pids[1])
        outs[3][...] = _dot(dmp, wg, NT)

    tile = lambda j, i: (i, j)
    dmp, dz, dscale, dpool = _mm(
        n + "_out", (4, nt), [dyb, w["g_out"], h, scale, pooled, w["g_grp"]] + dep,
        [pl.BlockSpec((tm, D_MODEL), lambda j, i: (i, 0)), _w_out_nt_block(0),
         pl.BlockSpec((tm, 512), lambda j, i: (i, 4 + j)), pl.BlockSpec((1, 512), lambda j, i: (0, j)), pl.BlockSpec((tm, 512), tile),
         pl.BlockSpec((N_DEV, 64, POOL_GROUP), lambda j, i: (0, j, 0))] + [_ANY] * len(dep),
        [_sds((s, D_INNER), BF), _sds((s, 2 * D_INNER), BF), _sds((1, D_INNER), F32), _sds((s, D_INNER), F32)],
        [pl.BlockSpec((tm, 512), tile), pl.BlockSpec((tm, 512), lambda j, i: (i, 4 + j)), pl.BlockSpec((1, 512), lambda j, i: (0, j)),
         pl.BlockSpec((tm, 512), tile)],
        NT, gate_bwd)
    g_out = _w_out_grad(n + "_gout", sv["gated"], dyb).reshape(N_DEV, 256, D_MODEL)
    g_grp = _tn(n + "_ggrp", pooled, dmp, (N_DEV, 256, 512), (N_DEV, 64, 512), lambda g: (0, g, 0),
                512, 512, (4,), lambda g: (0, g), lambda g: (0, g))
    dep = early(dict(g_grp=g_grp, g_out=g_out)) if early is not None else ()
    dh = _pool_window_bwd(n + "_win", dpool, tm, dz)
    g_in = _tn(n + "_gin", sv["xn"], dh, (N_DEV, D_MODEL, 512), (None, D_MODEL, 512), lambda j: (j, 0, 0),
               D_MODEL, 512, (8,), lambda j: (0, 0), lambda j: (0, j), dep)
    dep = early(dict(g_in=g_in)) if early is not None else ()
    dx, dxb, dnorm = _in_proj_bwd(n + "_in", dh, w["g_in"], lambda i, k: (k, 0, 0), 8, sv["x"], w["norm"], dy, tm, dep)
    return dx, dxb, dict(g_in=g_in, g_grp=g_grp, g_out=g_out, norm=dnorm[0], scale=dscale[0])


def _conv_in_index(i, j):
    return (j // 2, 0, j % 2)


def _conv_fwd(x, w, tm):
    s = x.shape[0]
    nt = s // tm
    xn = _rms_fwd("conv_rms", x, w["norm"], tm)
    ti = IN_PROJ_ROWS if s % IN_PROJ_ROWS == 0 else tm
    (h,) = _mm("conv_in", (s // ti, 16), [xn, w["g_in"]],
               [pl.BlockSpec((ti, D_MODEL), lambda i, j: (i, 0)), pl.BlockSpec((None, D_MODEL, 512), _conv_in_index)],
               [_sds((s, 4 * D_INNER), BF)], [pl.BlockSpec((ti, 512), lambda i, j: (i, j))], NN, _store)
    per = tm // CONV_HALO

    def body(b_ref, c_ref, h_ref, z_ref, cp_ref, hp_ref, w_ref, o_ref, e_ref):
        i = pl.program_id(0)
        ch = c_ref[...].astype(F32) * h_ref[...].astype(F32)
        e_ref[0:CONV_HALO, :] = jnp.where(i > 0, cp_ref[...].astype(F32) * hp_ref[...].astype(F32), 0.0)
        e_ref[CONV_HALO:, :] = ch
        co = (w_ref[2:3, :] * ch + w_ref[1:2, :] * e_ref[pl.ds(CONV_HALO - 1, tm), :]
              + w_ref[0:1, :] * e_ref[pl.ds(CONV_HALO - 2, tm), :])
        z = z_ref[...].astype(F32)
        o_ref[...] = ((b_ref[...].astype(F32) * co) * (z * _sig(z))).astype(BF)

    def col(q):
        return pl.BlockSpec((tm, 512), lambda i, j: (i, 4 * q + j))

    def prev(q):
        return pl.BlockSpec((CONV_HALO, 512), lambda i, j: (jnp.maximum(i * per - 1, 0), 4 * q + j))

    gated = pl.pallas_call(
        body, name="conv_mix", grid=(nt, 4),
        in_specs=[col(0), col(1), col(2), col(3), prev(1), prev(2), pl.BlockSpec((3, 512), lambda i, j: (0, j))],
        out_specs=pl.BlockSpec((tm, 512), lambda i, j: (i, j)), out_shape=_sds((s, D_INNER), BF),
        scratch_shapes=[pltpu.VMEM((CONV_HALO + tm, 512), F32)], compiler_params=_params())(h, h, h, h, h, h, w["conv_w"])
    w = dict(w, **w["rest"]([gated]))
    y = _out_proj("conv_out", gated, w["g_out"], 0, x, tm)
    return y, dict(x=x, xn=xn, h=h, gated=gated, w=w)


def _conv_bwd(dy, dyb, w, sv, tm, dep):
    s = dy.shape[0]
    nt = s // tm
    h = sv["h"]
    per = tm // CONV_HALO
    last = s // CONV_HALO - 1
    n_dep = len(dep)

    def body(dy_ref, dyn_ref, wo_ref, b_ref, c_ref, h_ref, z_ref, cp_ref, hp_ref, bn_ref, zn_ref, w_ref, *rest):
        dall_ref, dw_ref, e_ref, f_ref = rest[n_dep:]
        db_ref, dc_ref, dh_ref, dz_ref = (dall_ref.at[:, q * 512:(q + 1) * 512] for q in range(4))
        i = pl.program_id(1)
        wo = wo_ref[...].reshape(512, D_MODEL)
        dg_tile = _dot(dy_ref[...], wo, NT)
        dg_next = _dot(dyn_ref[...], wo, NT)
        w0, w1, w2 = w_ref[0:1, :], w_ref[1:2, :], w_ref[2:3, :]
        c, hh, b = c_ref[...].astype(F32), h_ref[...].astype(F32), b_ref[...].astype(F32)
        ch = c * hh
        e_ref[0:CONV_HALO, :] = jnp.where(i > 0, cp_ref[...].astype(F32) * hp_ref[...].astype(F32), 0.0)
        e_ref[CONV_HALO:, :] = ch
        ch1 = e_ref[pl.ds(CONV_HALO - 1, tm), :]
        ch2 = e_ref[pl.ds(CONV_HALO - 2, tm), :]
        co = w2 * ch + w1 * ch1 + w0 * ch2
        sz, dsz = _silu_and_grad(z_ref[...].astype(F32))
        dgv = dg_tile
        dyv = dgv * sz
        dz_ref[...] = (dgv * (b * co) * dsz).astype(BF)
        db_ref[...] = (dyv * co).astype(BF)
        dco = dyv * b
        zn = zn_ref[...].astype(F32)
        f_ref[0:tm, :] = dco
        f_ref[tm:, :] = jnp.where(i < nt - 1, dg_next * (zn * _sig(zn)) * bn_ref[...].astype(F32), 0.0)
        dch = w2 * dco + w1 * f_ref[pl.ds(1, tm), :] + w0 * f_ref[pl.ds(2, tm), :]
        dc_ref[...] = (dch * hh).astype(BF)
        dh_ref[...] = (dch * c).astype(BF)
        for tap, shifted in enumerate((ch2, ch1, ch)):
            _accumulate(dw_ref.at[tap:tap + 1, :], jnp.sum(dco * shifted, axis=0, keepdims=True), i)

    def col(q):
        return pl.BlockSpec((tm, 512), lambda j, i: (i, 4 * q + j))

    def prev(q):
        return pl.BlockSpec((CONV_HALO, 512), lambda j, i: (jnp.maximum(i * per - 1, 0), 4 * q + j))

    def nxt(q):
        return pl.BlockSpec((CONV_HALO, 512), lambda j, i: (jnp.minimum((i + 1) * per, last), 4 * q + j))

    wspec = pl.BlockSpec((3, 512), lambda j, i: (0, j))
    dy_tile = pl.BlockSpec((tm, D_MODEL), lambda j, i: (i, 0))
    dy_next = pl.BlockSpec((CONV_HALO, D_MODEL), lambda j, i: (jnp.minimum((i + 1) * per, last), 0))
    dh, dw = pl.pallas_call(
        body, name="convb_mix", grid=(4, nt),
        in_specs=[dy_tile, dy_next, _w_out_nt_block(0), col(0), col(1), col(2), col(3), prev(1), prev(2), nxt(0), nxt(3), wspec]
        + [_ANY] * n_dep,
        out_specs=[pl.BlockSpec((tm, D_INNER), lambda j, i: (i, j)), wspec],
        out_shape=[_sds((s, 4 * D_INNER), BF), _sds((3, D_INNER), F32)],
        scratch_shapes=[pltpu.VMEM((CONV_HALO + tm, 512), F32)] * 2, compiler_params=_params(),
    )(dyb, dyb, w["g_out"], h, h, h, h, h, h, h, h, w["conv_w"], *dep)

    def w_block(kp):
        k = 4 * (kp % 4) + kp // 4
        return (k // 2, 0, k % 2)

    dx, dxb, dnorm = _in_proj_bwd("convb_in", dh, w["g_in"], lambda i, kp: w_block(kp), 16, sv["x"], w["norm"], dy, tm)
    g_in = _tn("convb_gin", sv["xn"], dh, (N_DEV, D_MODEL, D_MODEL), (None, D_MODEL, 512), lambda j: (j // 2, 0, j % 2),
               D_MODEL, 512, (16,), lambda j: (0, 0), lambda j: (0, 4 * (j % 4) + j // 4))
    g_out = _w_out_grad("convb_gout", sv["gated"], dyb)
    return dx, dxb, dict(g_in=g_in, g_out=g_out.reshape(N_DEV, 256, D_MODEL), norm=dnorm[0], conv_w=dw)


def _attn_tiles(s):
    t = min(ATTN_TILE, s)
    return t, s // t


def _causal_keep(t, keys_on_rows):
    r = lax.broadcasted_iota(jnp.int32, (t, t), 0)
    c = lax.broadcasted_iota(jnp.int32, (t, t), 1)
    return (r <= c) if keys_on_rows else (c <= r)


def _mla_fwd(x, w, rope, tm):
    s = x.shape[0]
    nt = s // tm
    cosf, sinf, perm = rope
    xn = _rms_fwd("mla_rms", x, w["norm"], tm)

    def in_body(xn_ref, wq_ref, wkv_ref, wkr_ref, wz_ref, gq_ref, gkv_ref, cos_ref, sin_ref, p_ref,
                ql_ref, kvl_ref, qn_ref, kvn_ref, krr_ref, z_ref):
        xv = xn_ref[...]
        ql = _dot(xv, wq_ref[...], NN)
        kvl = _dot(xv, wkv_ref[...], NN)
        ql_ref[...] = ql
        kvl_ref[...] = kvl
        rq = lax.rsqrt(jnp.mean(ql * ql, axis=-1, keepdims=True) + NORM_EPS)
        qn_ref[...] = ((ql * rq) * gq_ref[...]).astype(BF)
        rkv = lax.rsqrt(jnp.mean(kvl * kvl, axis=-1, keepdims=True) + NORM_EPS)
        kvn_ref[...] = ((kvl * rkv) * gkv_ref[...]).astype(BF)
        kr = _dot(xv, wkr_ref[...], NN)
        krr_ref[...] = _rope_fwd(kr, cos_ref[...], sin_ref[...], p_ref[...]).astype(BF)
        z_ref[...] = _dot(xv, wz_ref[...], NN).astype(BF)

    def full(a):
        return pl.BlockSpec(a.shape, lambda i: (0,) * a.ndim)

    def rows(c):
        return pl.BlockSpec((tm, c), lambda i: (i, 0))

    gq, gkv = w["q_norm"].reshape(1, Q_RANK), w["kv_norm"].reshape(1, KV_RANK)
    q_lat, kv_lat, qn, kvn, krr, z = pl.pallas_call(
        in_body, name="mla_in", grid=(nt,),
        in_specs=[rows(D_MODEL), full(w["w_q"]), full(w["w_kv"]), full(w["w_kr"]), full(w["w_z"]), full(gq), full(gkv),
                  rows(QK_ROPE), rows(QK_ROPE), full(perm)],
        out_specs=[rows(Q_RANK), rows(KV_RANK), rows(Q_RANK), rows(KV_RANK), rows(QK_ROPE), rows(D_INNER)],
        out_shape=[_sds((s, Q_RANK), F32), _sds((s, KV_RANK), F32), _sds((s, Q_RANK), BF), _sds((s, KV_RANK), BF),
                   _sds((s, QK_ROPE), BF), _sds((s, D_INNER), BF)],
        compiler_params=_params())(xn, w["w_q"], w["w_kv"], w["w_kr"], w["w_z"], gq, gkv, cosf, sinf, perm)

    def q_epi(part, extra, outs, pids):
        outs[0][:, 0:QK_NOPE] = (part[:, 0:QK_NOPE] * Q_PRESCALE).astype(BF)
        roped = _rope_fwd(part[:, QK_NOPE:QK_DIM], extra[0][...], extra[1][...], extra[2][...])
        outs[0][:, QK_NOPE:QK_DIM] = (roped * Q_PRESCALE).astype(BF)

    tp = IN_PROJ_ROWS if s % IN_PROJ_ROWS == 0 else tm
    rope_row = pl.BlockSpec((tp, QK_ROPE), lambda h, i: (i, 0))
    (q,) = _mm("mla_qup", (N_HEADS, s // tp), [qn, w["w_qh"], cosf, sinf, perm],
               [pl.BlockSpec((tp, Q_RANK), lambda h, i: (i, 0)), pl.BlockSpec((None, Q_RANK, QK_DIM), lambda h, i: (h, 0, 0)),
                rope_row, rope_row, pl.BlockSpec((QK_ROPE, QK_ROPE), lambda h, i: (0, 0))],
               [_sds((N_HEADS, s, QK_DIM), BF)], [pl.BlockSpec((None, tp, QK_DIM), lambda h, i: (h, i, 0))], NN, q_epi)

    def kv_epi(part, extra, outs, pids):
        outs[0][:, 0:QK_NOPE] = part[:, 0:QK_NOPE].astype(BF)
        outs[0][:, QK_NOPE:QK_DIM] = extra[0][...]
        outs[1][...] = part[:, QK_NOPE:].astype(BF)

    k, v = _mm("mla_kvup", (N_HEADS, s // tp), [kvn, w["g512"], krr],
               [pl.BlockSpec((tp, KV_RANK), lambda h, i: (i, 0)),
                pl.BlockSpec((None, KV_RANK, 256), lambda h, i: (h // 2, 0, h % 2)), rope_row],
               [_sds((N_HEADS, s, QK_DIM), BF), _sds((N_HEADS, s, V_DIM), BF)],
               [pl.BlockSpec((None, tp, QK_DIM), lambda h, i: (h, i, 0)), pl.BlockSpec((None, tp, V_DIM), lambda h, i: (h, i, 0))],
               NN, kv_epi)

    t, nq = _attn_tiles(s)

    def attn_body(q_ref, k_ref, v_ref, z_ref, o_ref, g_ref, lse_ref):
        i = pl.program_id(1)

        def block(j, carry, masked):
            start = pl.multiple_of(j * t, t)
            out = []
            for hh, (m, lsum, acc) in enumerate(carry):
                sc = _dot(q_ref[hh], k_ref[hh, pl.ds(start, t), :], NT)
                if masked:
                    sc = jnp.where(_causal_keep(t, False), sc, NEG_BIG)
                mn = jnp.maximum(m, jnp.max(sc, axis=-1, keepdims=True))
                alpha = jnp.exp2(m - mn)
                p = jnp.exp2(sc - mn)
                lsum = alpha * lsum + jnp.sum(p, axis=-1, keepdims=True)
                acc = alpha * acc + _dot(p.astype(BF), v_ref[hh, pl.ds(start, t), :], NN)
                out.append((mn, lsum, acc))
            return tuple(out)

        init = ((jnp.full((t, 1), NEG_BIG, F32), jnp.zeros((t, 1), F32), jnp.zeros((t, V_DIM), F32)),) * ATTN_HEADS_PER_STEP
        carry = lax.fori_loop(0, i, lambda j, c: block(j, c, False), init)
        for hh, (m, lsum, acc) in enumerate(block(i, carry, True)):
            cols = slice(hh * V_DIM, (hh + 1) * V_DIM)
            o = acc / lsum
            z = z_ref[:, cols].astype(F32)
            o_ref[:, cols] = o
            g_ref[:, cols] = (o * (z * _sig(z))).astype(BF)
            lse_ref[hh] = m + jnp.log(lsum) * LOG2_E

    hp = ATTN_HEADS_PER_STEP
    head_col = pl.BlockSpec((t, hp * V_DIM), lambda h, i: (i, h))
    o, gated, lse = pl.pallas_call(
        attn_body, name="mla_attn", grid=(N_HEADS // hp, nq),
        in_specs=[pl.BlockSpec((hp, t, QK_DIM), lambda h, i: (h, i, 0)), pl.BlockSpec((hp, s, QK_DIM), lambda h, i: (h, 0, 0)),
                  pl.BlockSpec((hp, s, V_DIM), lambda h, i: (h, 0, 0)), head_col],
        out_specs=[head_col, head_col, pl.BlockSpec((hp, t, 1), lambda h, i: (h, i, 0))],
        out_shape=[_sds((s, D_INNER), F32), _sds((s, D_INNER), BF), _sds((N_HEADS, s, 1), F32)],
        compiler_params=_params())(q, k, v, z)
    y = _out_proj("mla_out", gated, w["g1024"], 0, x, tm)
    return y, dict(x=x, xn=xn, q_lat=q_lat, kv_lat=kv_lat, qn=qn, kvn=kvn, z=z, q=q, k=k, v=v, o=o, lse=lse, gated=gated)


def _mla_bwd(dy, dyb, w, sv, rope, tm, dep):
    s = dy.shape[0]
    nt = s // tm
    cosf, sinf, perm = rope
    t, nq = _attn_tiles(s)
    q, k, v, lse = sv["q"], sv["k"], sv["v"], sv["lse"]

    def gate_bwd(part, extra, outs, pids):
        z, o = extra[0][...].astype(F32), extra[1][...]
        sz, dsz = _silu_and_grad(z)
        do = part * sz
        outs[0][...] = do.astype(BF)
        outs[1][...] = (part * o * dsz).astype(BF)
        prod = do * o
        for hh in range(4):
            outs[2][hh] = jnp.sum(prod[:, hh * V_DIM:(hh + 1) * V_DIM], axis=-1, keepdims=True)

    tile = lambda j, i: (i, j)
    dob, dz, delta = _mm(
        "mlab_out", (4, nt), [dyb, w["g1024"], sv["z"], sv["o"]] + dep,
        [pl.BlockSpec((tm, D_MODEL), lambda j, i: (i, 0)), _w_out_nt_block(0),
         pl.BlockSpec((tm, 512), tile), pl.BlockSpec((tm, 512), tile)] + [_ANY] * len(dep),
        [_sds((s, D_INNER), BF), _sds((s, D_INNER), BF), _sds((N_HEADS, s, 1), F32)],
        [pl.BlockSpec((tm, 512), tile), pl.BlockSpec((tm, 512), tile), pl.BlockSpec((4, tm, 1), lambda j, i: (j, i, 0))],
        NT, gate_bwd)

    hp = ATTN_HEADS_PER_STEP

    def attn_bwd_body(k_ref, v_ref, q_ref, do_ref, lse_ref, dl_ref, cos_ref, sin_ref, p_ref, dkv_ref, dkr_ref, dq_ref, dq_acc):
        j = pl.program_id(1)

        @pl.when(j == 0)
        def _():
            dq_acc[...] = jnp.zeros(dq_acc.shape, F32)

        def block(i, carry, masked):
            rows = pl.ds(pl.multiple_of(i * t, t), t)
            out = []
            for hh, (dk, dv) in enumerate(carry):
                kb, vb = k_ref[hh], v_ref[hh]
                qb, dob_ = q_ref[hh, rows, :], do_ref[rows, hh * V_DIM:(hh + 1) * V_DIM]
                st = _dot(kb, qb, NT)
                if masked:
                    st = jnp.where(_causal_keep(t, True), st, NEG_BIG)
                pt = jnp.exp2(st - lse_ref[hh, i])
                dv = dv + _dot(pt.astype(BF), dob_, NN)
                dst = (pt * (_dot(vb, dob_, NT) - dl_ref[hh, i])).astype(BF)
                dk = dk + _dot(dst, qb, NN)
                dq_acc[hh, rows, :] += _dot(dst, kb, TN)
                out.append((dk, dv))
            return tuple(out)

        init = ((jnp.zeros((t, QK_DIM), F32), jnp.zeros((t, V_DIM), F32)),) * hp
        carry = block(j, init, True)
        carry = lax.fori_loop(j + 1, nq, lambda i, c: block(i, c, False), carry)
        for hh, (dk, dv) in enumerate(carry):
            dk = dk * LN_2
            base = hh * 2 * V_DIM
            dkv_ref[:, base:base + QK_NOPE] = dk[:, 0:QK_NOPE].astype(BF)
            dkv_ref[:, base + QK_NOPE:base + 2 * V_DIM] = dv.astype(BF)
            dkr_ref[hh] = dk[:, QK_NOPE:]

        @pl.when(j == nq - 1)
        def _():
            for hh in range(hp):
                for c in range(nq):
                    rows = slice(c * t, (c + 1) * t)
                    dq = dq_acc[hh, rows, :] * ATTN_SCALE
                    dq_ref[hh, rows, 0:QK_NOPE] = dq[:, 0:QK_NOPE].astype(BF)
                    dq_ref[hh, rows, QK_NOPE:] = _rope_bwd(dq[:, QK_NOPE:], cos_ref[rows, :], sin_ref[rows, :], p_ref[...]).astype(BF)

    row_stats = pl.BlockSpec((hp, nq, 1, t), lambda h, j: (h, 0, 0, 0))
    seq_rope = pl.BlockSpec((s, QK_ROPE), lambda h, j: (0, 0))
    head_seq = pl.BlockSpec((hp, s, QK_DIM), lambda h, j: (h, 0, 0))
    dkv, dkr_h, dq = pl.pallas_call(
        attn_bwd_body, name="mlab_attn", grid=(N_HEADS // hp, nq),
        in_specs=[pl.BlockSpec((hp, t, QK_DIM), lambda h, j: (h, j, 0)), pl.BlockSpec((hp, t, V_DIM), lambda h, j: (h, j, 0)),
                  head_seq, pl.BlockSpec((s, hp * V_DIM), lambda h, j: (0, h)), row_stats, row_stats, seq_rope, seq_rope,
                  pl.BlockSpec((QK_ROPE, QK_ROPE), lambda h, j: (0, 0))],
        out_specs=[pl.BlockSpec((t, hp * 2 * V_DIM), lambda h, j: (j, h)), pl.BlockSpec((hp, t, QK_ROPE), lambda h, j: (h, j, 0)), head_seq],
        out_shape=[_sds((s, N_HEADS * 2 * V_DIM), BF), _sds((N_HEADS, s, QK_ROPE), F32), _sds((N_HEADS, s, QK_DIM), BF)],
        scratch_shapes=[pltpu.VMEM((hp, s, QK_DIM), F32)],
        compiler_params=_params())(k, v, q, dob, lse.reshape(N_HEADS, nq, 1, t), delta.reshape(N_HEADS, nq, 1, t), cosf, sinf, perm)

    def dkr_body(d_ref, cos_ref, sin_ref, p_ref, o_ref):
        tot = d_ref[0]
        for hh in range(1, N_HEADS):
            tot = tot + d_ref[hh]
        o_ref[...] = _rope_bwd(tot, cos_ref[...], sin_ref[...], p_ref[...]).astype(BF)

    r64 = pl.BlockSpec((tm, QK_ROPE), lambda i: (i, 0))
    dkr = pl.pallas_call(
        dkr_body, name="mlab_dkr", grid=(nt,),
        in_specs=[pl.BlockSpec((N_HEADS, tm, QK_ROPE), lambda i: (0, i, 0)), r64, r64, pl.BlockSpec((QK_ROPE, QK_ROPE), lambda i: (0, 0))],
        out_specs=r64, out_shape=_sds((s, QK_ROPE), BF), compiler_params=_params())(dkr_h, cosf, sinf, perm)

    def lat_epi(acc, extra, outs, pids):
        dx, dg = _rms_bwd(acc, extra[0][...], extra[1][...], None)
        outs[0][...] = dx.astype(BF)
        _accumulate(outs[1], dg, pids[0])

    tp = IN_PROJ_ROWS if s % IN_PROJ_ROWS == 0 else tm

    def lat_bwd(name, a, a_spec, b, b_spec, n_k, lat, g, rank):
        row = lambda i, k: (i, 0)
        one = lambda i, k: (0, 0)
        return _mm(name, (s // tp, n_k), [a, b, lat, g.reshape(1, rank)],
                   [a_spec, b_spec, pl.BlockSpec((tp, rank), row), pl.BlockSpec((1, rank), one)],
                   [_sds((s, rank), BF), _sds((1, rank), F32)], [pl.BlockSpec((tp, rank), row), pl.BlockSpec((1, rank), one)],
                   NT, lat_epi, red=1, acc_shape=(tp, rank))

    d_ql, g_qnorm = lat_bwd("mlab_qup", dq, pl.BlockSpec((None, tp, QK_DIM), lambda i, h: (h, i, 0)),
                            w["w_qh"], pl.BlockSpec((None, Q_RANK, QK_DIM), lambda i, h: (h, 0, 0)), N_HEADS,
                            sv["q_lat"], w["q_norm"], Q_RANK)
    d_kvl, g_kvnorm = lat_bwd("mlab_kvup", dkv, pl.BlockSpec((tp, 512), lambda i, kk: (i, kk)),
                              w["g512"], pl.BlockSpec((None, KV_RANK, 512), lambda i, kk: (kk, 0, 0)), N_DEV,
                              sv["kv_lat"], w["kv_norm"], KV_RANK)

    def in_bwd(dql_ref, dkvl_ref, dkr_ref, dz_ref, wq_ref, wkv_ref, wkr_ref, wz_ref, x_ref, g_ref, dy_ref, dx_ref, dxb_ref, dg_ref):
        acc = (_dot(dql_ref[...], wq_ref[...], NT) + _dot(dkvl_ref[...], wkv_ref[...], NT)
               + _dot(dkr_ref[...], wkr_ref[...], NT) + _dot(dz_ref[...], wz_ref[...], NT))
        dx, dg = _rms_bwd(acc, x_ref[...], g_ref[...], dy_ref[...])
        dx_ref[...] = dx
        dxb_ref[...] = dx.astype(BF)
        _accumulate(dg_ref, dg, pl.program_id(0))

    def full(a):
        return pl.BlockSpec(a.shape, lambda i: (0,) * a.ndim)

    def rows(c):
        return pl.BlockSpec((tm, c), lambda i: (i, 0))

    gm = w["norm"].reshape(1, D_MODEL)
    dx, dxb, g_norm = pl.pallas_call(
        in_bwd, name="mlab_in", grid=(nt,),
        in_specs=[rows(Q_RANK), rows(KV_RANK), rows(QK_ROPE), rows(D_INNER), full(w["w_q"]), full(w["w_kv"]), full(w["w_kr"]),
                  full(w["w_z"]), rows(D_MODEL), full(gm), rows(D_MODEL)],
        out_specs=[rows(D_MODEL), rows(D_MODEL), full(gm)],
        out_shape=[_sds((s, D_MODEL), F32), _sds((s, D_MODEL), BF), _sds((1, D_MODEL), F32)],
        compiler_params=_params())(d_ql, d_kvl, dkr, dz, w["w_q"], w["w_kv"], w["w_kr"], w["w_z"], sv["x"], gm, dy)

    xn = sv["xn"]
    one = lambda j: (0, 0)
    g_q = _tn("mlab_gq", xn, d_ql, (D_MODEL, Q_RANK), (D_MODEL, Q_RANK), one, D_MODEL, Q_RANK, (1,), one, one)
    g_kv = _tn("mlab_gkv", xn, d_kvl, (D_MODEL, KV_RANK), (D_MODEL, KV_RANK), one, D_MODEL, KV_RANK, (1,), one, one)
    g_kr = _tn("mlab_gkr", xn, dkr, (D_MODEL, QK_ROPE), (D_MODEL, QK_ROPE), one, D_MODEL, QK_ROPE, (1,), one, one)
    g_z = _tn("mlab_gz", xn, dz, (D_MODEL, D_INNER), (D_MODEL, 512), lambda j: (0, j), D_MODEL, 512, (4,), one, lambda j: (0, j))
    g_in = jnp.concatenate([g_q, g_kv, g_kr, g_z], axis=1)
    g_qh = _tn("mlab_gqup", sv["qn"], dq, (N_HEADS, Q_RANK, QK_DIM), (None, Q_RANK, QK_DIM), lambda h: (h, 0, 0),
               Q_RANK, QK_DIM, (N_HEADS,), lambda h: (0, 0), lambda h: (h, 0, 0))
    g_kvup = _tn("mlab_gkvup", sv["kvn"], dkv, (N_DEV, KV_RANK, 512), (None, KV_RANK, 512), lambda j: (j, 0, 0),
                 KV_RANK, 512, (N_DEV,), lambda j: (0, 0), lambda j: (0, j))
    g_out = _w_out_grad("mlab_gout", sv["gated"], dyb)
    s384 = g_qh.reshape(N_DEV, 2, Q_RANK, QK_DIM).transpose(0, 2, 1, 3).reshape(N_DEV, Q_RANK, 2 * QK_DIM)
    s344 = g_in.reshape(D_MODEL, N_DEV, 344).transpose(1, 0, 2)
    return dx, dxb, dict(s344=s344, s384=s384, s512=g_kvup, s1024=g_out.reshape(N_DEV, 256, D_MODEL),
                         norm=g_norm[0], q_norm=g_qnorm[0], kv_norm=g_kvnorm[0])


def _loss_head(x, g, target, tm):
    s, d = x.shape

    def body(x_ref, g_ref, t_ref, dx_ref, dxb_ref, dg_ref, loss_ref):
        i = pl.program_id(0)
        xv, gv = x_ref[...], g_ref[...]
        r = lax.rsqrt(jnp.mean(xv * xv, axis=-1, keepdims=True) + NORM_EPS)
        err = (xv * r) * gv - t_ref[...]
        part = 0.5 * jnp.sum(jnp.mean(err * err, axis=-1, keepdims=True), axis=0, keepdims=True)
        dx, dg = _rms_bwd(err * (1.0 / d), xv, gv, None)
        dx_ref[...] = dx
        dxb_ref[...] = dx.astype(BF)
        _accumulate(dg_ref, dg, i)
        _accumulate(loss_ref, jnp.broadcast_to(part, loss_ref.shape), i)

    row = pl.BlockSpec((tm, d), lambda i: (i, 0))
    one = pl.BlockSpec((1, d), lambda i: (0, 0))
    return pl.pallas_call(
        body, name="loss_head", grid=(s // tm,), in_specs=[row, one, row],
        out_specs=[row, row, one, pl.BlockSpec((8, 128), lambda i: (0, 0))],
        out_shape=[_sds((s, d), F32), _sds((s, d), BF), _sds((1, d), F32), _sds((8, 128), F32)],
        compiler_params=_params())(x, g.reshape(1, d), target)


def _rope_tables(pos):
    inv_freq = ROPE_BASE ** (-jnp.arange(0, QK_ROPE, 2, dtype=F32) / QK_ROPE)
    ang = pos.astype(F32)[:, None] * inv_freq
    cos, sin = jnp.cos(ang), jnp.sin(ang)
    idx = jnp.arange(QK_ROPE)
    perm = (idx[:, None] == (idx[None, :] + QK_ROPE // 2) % QK_ROPE).astype(F32)
    return jnp.concatenate([cos, cos], axis=1), jnp.concatenate([-sin, sin], axis=1), perm


def _local_step(x, pos, target, final_norm, get_w, put_g):
    s = x.shape[0]
    tm = min(512, s)
    rope = _rope_tables(pos)
    w0 = get_w(0, [])
    x1, sv0 = _pool_fwd(x, 0, w0, tm)
    w1 = get_w(1, [x1])
    x2, sv1 = _conv_fwd(x1, w1, tm)
    w2 = get_w(2, [x2])
    x3, sv2 = _mla_fwd(x2, w2, rope, tm)
    w3 = get_w(3, [x3])
    x4, sv3 = _pool_fwd(x3, 1, w3, tm)
    d4, d4b, g_final, loss = _loss_head(x4, final_norm, target, tm)
    d3, d3b, gp1 = _pool_bwd(d4, d4b, 1, sv3["w"], sv3, tm, [])
    dep = put_g(3, gp1)
    d2, d2b, gm = _mla_bwd(d3, d3b, w2, sv2, rope, tm, dep)
    dep = put_g(2, gm)
    d1, d1b, gc = _conv_bwd(d2, d2b, sv1["w"], sv1, tm, dep)
    dep = put_g(1, gc)
    d0, _, gp0 = _pool_bwd(d1, d1b, 0, sv0["w"], sv0, tm, dep, early=lambda big: put_g(0, big))
    put_g(4, dict(gp0, final_norm=g_final[0]))
    return loss[0, 0], d0


def _pack_groups(p):
    bf = lambda a: a.astype(BF)
    grp = lambda l: bf(p["pool_w_grp"][l].reshape(4 * 64, POOL_GROUP))
    return [[bf(p["pool_w_in"][0]), _pack_small(p, SMALL_ROWS_AG)],
            [grp(0), bf(p["pool_w_out"][0])],
            [bf(p["conv_w_in"][0])],
            [bf(p["conv_w_out"][0])],
            [bf(p[k][0]) for k in ("mla_w_in", "mla_w_q_up", "mla_w_kv_up", "mla_w_out")],
            [bf(p["pool_w_in"][1]), grp(1), bf(p["pool_w_out"][1])]]


_SMALL_SHARDED = ("pool_norm", "pool_scale", "mla_norm", "mla_q_norm", "mla_kv_norm", "conv_w")
_SMALL_REPLICATED = ("conv_norm", "final_norm")


def _pack_small(p, rows, with_replicated=False):
    parts = [p[k].reshape(-1) for k in _SMALL_SHARDED]
    if with_replicated:
        parts += [p[k].reshape(-1) for k in _SMALL_REPLICATED]
    flat = jnp.concatenate(parts)
    return jnp.pad(flat, (0, rows * 128 - flat.shape[0])).reshape(rows, 128)


_SMALL_SHARD_SHAPES = dict(pool_norm=(2, 128), pool_scale=(2, 256), mla_norm=(1, 128), mla_q_norm=(1, 48),
                           mla_kv_norm=(1, 32), conv_w=(1, 3, 256), conv_norm=(1, 1024), final_norm=(1024,))


def _unpack_small(buf, with_replicated=False):
    flat = buf.reshape(-1)
    out, off = {}, 0
    for k in _SMALL_SHARDED + (_SMALL_REPLICATED if with_replicated else ()):
        shp = _SMALL_SHARD_SHAPES[k]
        n = 1
        for d in shp:
            n *= d
        out[k] = flat[off:off + n].reshape(shp)
        off += n
    return out


def _small_views(gsmall):
    flat = gsmall.reshape(N_DEV, -1)

    def cols(off, rows, width):
        return flat[:, off:off + rows * width].reshape(N_DEV, rows, width).transpose(1, 0, 2).reshape(rows, N_DEV * width)

    return dict(pool_norm=cols(0, 2, 128), pool_scale=cols(256, 2, 256), mla_norm=cols(768, 1, 128)[0],
                q_norm=cols(896, 1, 48)[0], kv_norm=cols(944, 1, 32)[0], conv_w=cols(976, 3, 256))


def _layer_weights(layer, bufs, small, conv_norm):
    if layer in (0, 3):
        l = 0 if layer == 0 else 1
        return dict(g_in=bufs[0], norm=small["pool_norm"][l], scale=small["pool_scale"][l])
    if layer == 1:
        return dict(g_in=bufs[0], norm=conv_norm.reshape(D_MODEL), conv_w=small["conv_w"])
    g344, g384, g512, g1024 = bufs
    w_in = g344.transpose(1, 0, 2).reshape(D_MODEL, N_DEV * 344)
    return dict(
        g512=g512, g1024=g1024,
        w_q=w_in[:, :Q_RANK], w_kv=w_in[:, Q_RANK:Q_RANK + KV_RANK],
        w_kr=w_in[:, Q_RANK + KV_RANK:Q_RANK + KV_RANK + QK_ROPE], w_z=w_in[:, Q_RANK + KV_RANK + QK_ROPE:],
        w_qh=g384.reshape(N_DEV, Q_RANK, 2, QK_DIM).transpose(0, 2, 1, 3).reshape(N_HEADS, Q_RANK, QK_DIM),
        norm=small["mla_norm"], q_norm=small["q_norm"], kv_norm=small["kv_norm"])


_GRAD_KEYS = {0: ("g_in", "g_grp", "g_out"), 3: ("g_in", "g_grp", "g_out"), 1: ("g_in", "g_out"), 2: ("s344", "s384", "s512", "s1024")}
_GRAD_PARAM = {0: dict(g_in="pool_w_in", g_grp="pool_w_grp", g_out="pool_w_out"), 1: dict(g_in="conv_w_in", g_out="conv_w_out"),
               2: dict(s344="mla_w_in", s384="mla_w_q_up", s512="mla_w_kv_up", s1024="mla_w_out")}
_GRAD_PARAM[3] = _GRAD_PARAM[0]


def _grad_group(layer, g):
    keys = tuple(k for k in _GRAD_KEYS[layer] if k in g)
    return keys, [g[k] for k in keys]


def _pack_small_grads(g):
    def split(a, rows, width):
        return a.reshape(rows, N_DEV, width).transpose(1, 0, 2).reshape(N_DEV, rows * width)

    rep = lambda a: jnp.broadcast_to(a.reshape(1, -1), (N_DEV, a.size))
    flat = jnp.concatenate([
        split(jnp.stack([g[0]["norm"], g[3]["norm"]]), 2, 128), split(jnp.stack([g[0]["scale"], g[3]["scale"]]), 2, 256),
        split(g[2]["norm"], 1, 128), split(g[2]["q_norm"], 1, 48), split(g[2]["kv_norm"], 1, 32), split(g[1]["conv_w"], 3, 256),
        rep(g[1]["norm"]), rep(g[0]["final_norm"])], axis=1)
    return jnp.pad(flat, ((0, 0), (0, SMALL_ROWS_RS * 128 - flat.shape[1]))).reshape(N_DEV, SMALL_ROWS_RS, 128)


def _peers(x, y, c):
    for k in range(1, N_DEV):
        px = 1 - x if k & 4 else x
        py = 1 - y if k & 2 else y
        pc = 1 - c if k & 1 else c
        yield k - 1, (px, py, pc), 4 * px + 2 * py + pc


def _remote_copies(srcs, lands, send_sems, recv_sems, gather):
    x, y, c = lax.axis_index("x"), lax.axis_index("y"), lax.axis_index("c")
    me = 4 * x + 2 * y + c
    copies = []
    for k, peer, pidx in _peers(x, y, c):
        for a, (src, land) in enumerate(zip(srcs, lands)):
            copies.append(pltpu.make_async_remote_copy(
                src_ref=src if gather else src.at[pidx], dst_ref=land.at[me],
                send_sem=send_sems.at[a * (N_DEV - 1) + k], recv_sem=recv_sems.at[a * (N_DEV - 1) + k],
                device_id=peer, device_id_type=pl.DeviceIdType.MESH))
    return copies


_HBM = pl.BlockSpec(memory_space=pltpu.HBM)
_SEM = pl.BlockSpec(memory_space=pltpu.SEMAPHORE)
_EFFECT = pltpu.SideEffectType.DATAFLOW_SIDE_EFFECTING


def _own_slabs(name, arrays, gather, dep):
    n, nd = len(arrays), len(dep)
    me = (4 * lax.axis_index("x") + 2 * lax.axis_index("y") + lax.axis_index("c")).astype(jnp.int32).reshape(1)

    def body(me_ref, *refs):
        for a in range(n):
            refs[n + nd + a][...] = refs[a][...]

    def slab(shape):
        return pl.BlockSpec((None,) + tuple(shape), lambda i, me_ref: (me_ref[0],) + (0,) * len(shape))

    def whole(shape):
        return pl.BlockSpec(tuple(shape), lambda i, me_ref: (0,) * len(shape))

    outs = [_sds(((N_DEV,) + a.shape) if gather else a.shape, a.dtype) for a in arrays]
    grid_spec = pltpu.PrefetchScalarGridSpec(
        num_scalar_prefetch=1, grid=(1,),
        in_specs=[whole(a.shape) if gather else slab(a.shape[1:]) for a in arrays] + [_ANY] * nd,
        out_specs=[slab(o.shape[1:]) for o in outs])
    return pl.pallas_call(body, name=name, grid_spec=grid_spec, out_shape=outs, compiler_params=_params())(me, *arrays, *dep)


def _exchange_start(name, arrays, lands, gather):
    n = len(arrays)

    def body(*refs):
        srcs, lnds, send_sems, recv_sems, token = refs[:n], refs[n:2 * n], refs[2 * n], refs[2 * n + 1], refs[-1]
        for cp in _remote_copies(srcs, lnds, send_sems, recv_sems, gather):
            cp.start()
        token[...] = jnp.zeros(token.shape, F32)

    sems = pltpu.SemaphoreType.DMA((n * (N_DEV - 1),))
    thru = [pltpu.HBM(a.shape, a.dtype) for a in list(arrays) + list(lands)]
    res = pl.pallas_call(
        body, name=name, in_specs=[_HBM] * (2 * n),
        out_specs=[_SEM, _SEM] + [_HBM] * (2 * n) + [pl.BlockSpec(memory_space=pltpu.VMEM)],
        out_shape=[sems, sems] + thru + [_sds((8, 128), F32)],
        input_output_aliases={i: 2 + i for i in range(2 * n)},
        compiler_params=pltpu.CompilerParams(has_side_effects=_EFFECT),
    )(*[pltpu.with_memory_space_constraint(a, pltpu.HBM) for a in list(arrays) + list(lands)])
    return res[0], res[1], list(res[2:2 + n]), list(res[2 + n:2 + 2 * n]), res[-1]


def _exchange_wait(name, send_sems, recv_sems, arrays, lands, after, gather):
    n = len(arrays)
    n_after = len(after)

    def body(*refs):
        srcs, lnds = refs[:n], refs[n:2 * n]
        copies = _remote_copies(srcs, lnds, refs[2 * n], refs[2 * n + 1], gather)
        for cp in copies:
            cp.wait_send()
        for cp in copies:
            cp.wait_recv()

    thru = [pltpu.HBM(a.shape, a.dtype) for a in list(arrays) + list(lands)]
    res = pl.pallas_call(
        body, name=name, in_specs=[_HBM] * (2 * n) + [_SEM, _SEM] + [pl.BlockSpec(memory_space=pl.ANY)] * n_after,
        out_specs=[_HBM] * (2 * n), out_shape=thru, input_output_aliases={i: i for i in range(2 * n)},
        compiler_params=pltpu.CompilerParams(has_side_effects=_EFFECT),
    )(*arrays, *lands, send_sems, recv_sems, *after)
    return list(res[n:])


def _adamw_math(g, w, m, v):
    m = ADAM_B1 * m + (1.0 - ADAM_B1) * g
    v = ADAM_B2 * v + (1.0 - ADAM_B2) * (g * g)
    m_hat = m / (1.0 - ADAM_B1 ** ADAM_STEP)
    v_hat = v / (1.0 - ADAM_B2 ** ADAM_STEP)
    delta = -ADAM_LR * (m_hat / (jnp.sqrt(v_hat) + ADAM_EPS) + ADAM_WD * w)
    return delta, m, v


def _sum_adamw(name, recv, row_off, w, m, v, tr, layer=0):
    width = recv.shape[-1]
    w2, m2, v2 = (a.reshape(a.shape[0], -1, width) for a in (w, m, v))
    rows = w2.shape[1]
    base = row_off // tr

    def body(r_ref, w_ref, m_ref, v_ref, g_ref, d_ref, mo_ref, vo_ref):
        g = r_ref[0].astype(F32)
        for src in range(1, N_DEV):
            g = g + r_ref[src].astype(F32)
        delta, mn, vn = _adamw_math(g, w_ref[...], m_ref[...], v_ref[...])
        g_ref[...] = g
        d_ref[...] = delta
        mo_ref[...] = mn
        vo_ref[...] = vn

    blk = pl.BlockSpec((tr, width), lambda i: (i, 0))
    wblk = pl.BlockSpec((None, tr, width), lambda i: (layer, i, 0))
    return pl.pallas_call(
        body, name=name, grid=(rows // tr,),
        in_specs=[pl.BlockSpec((N_DEV, tr, width), lambda i: (0, base + i, 0)), wblk, wblk, wblk],
        out_specs=[blk] * 4, out_shape=[_sds((rows, width), F32)] * 4, compiler_params=_params())(recv, w2, m2, v2)


_WEIGHTS = ("pool_norm", "pool_w_in", "pool_w_grp", "pool_scale", "pool_w_out", "conv_norm", "conv_w_in", "conv_w", "conv_w_out",
            "mla_norm", "mla_w_in", "mla_q_norm", "mla_w_q_up", "mla_kv_norm", "mla_w_kv_up", "mla_w_out", "final_norm")


def _step(x, positions, loss_target, p, m, v):
    gathers, tokens, dep = [], [], []
    for group, arrays in enumerate(_pack_groups(p)):
        lands = _own_slabs(f"gather{group}_own", arrays, True, dep)
        ssem, rsem, arrays, lands, token = _exchange_start(f"gather{group}_start", arrays, lands, True)
        gathers.append((ssem, rsem, arrays, lands))
        tokens.append(token)
        dep = [token]
    state = {}

    def wait_group(group, after):
        return _exchange_wait(f"gather{group}_wait", *gathers[group], after, True)

    def get_w(layer, after):
        if layer == 0:
            bufs = wait_group(0, list(tokens))
            state["small"] = _small_views(bufs[1])
            rest = lambda later: dict(zip(("g_grp", "g_out"), wait_group(1, later)))
        elif layer == 1:
            bufs = wait_group(2, after)
            rest = lambda later: dict(g_out=wait_group(3, later)[0])
        elif layer == 2:
            bufs = wait_group(4, after)
        else:
            bufs = wait_group(5, after)
            rest = lambda later: dict(g_grp=bufs[1], g_out=bufs[2])
        w = _layer_weights(layer, bufs, state["small"], p["conv_norm"])
        if layer != 2:
            w["rest"] = rest
        return w

    scatters, small_grads = [], {}

    def put_g(layer, g):
        if layer == 4:
            small_grads[0] = g
            keys, arrays = ("small",), [_pack_small_grads(small_grads)]
        else:
            small_grads[layer] = g
            keys, arrays = _grad_group(layer, g)
        n = len(scatters)
        lands = _own_slabs(f"scatter{n}_own", arrays, False, [])
        ssem, rsem, arrays, lands, token = _exchange_start(f"scatter{n}_start", arrays, lands, False)
        scatters.append((layer, keys, (ssem, rsem, arrays, lands)))
        tokens.append(token)
        return [token]

    loss, grad_x = _local_step(x[0], positions[0], loss_target[0], p["final_norm"], get_w, put_g)

    res, after = {}, [tokens[-1]]
    for n, (layer, keys, handles) in enumerate(scatters):
        recv = _exchange_wait(f"scatter{n}_wait", *handles, after, False)
        if layer == 4:
            break
        l = 1 if layer == 3 else 0
        for key, buf in zip(keys, recv):
            name = _GRAD_PARAM[layer][key]
            tr = min(256, buf.shape[1]) if name != "mla_w_q_up" else buf.shape[1]
            res[name, l] = _sum_adamw(f"adam_{name}{l}", buf, 0, p[name], m[name], v[name], tr, l)
        after = [res[name, l][1]]
    small = _sum_adamw("adam_small", recv[0], 0, _pack_small(p, SMALL_ROWS_RS, True)[None], _pack_small(m, SMALL_ROWS_RS, True)[None],
                       _pack_small(v, SMALL_ROWS_RS, True)[None], SMALL_ROWS_RS)
    small = [_unpack_small(a, True) for a in small]
    final = {k: tuple(part[k] for part in small) for k in _SMALL_SHARDED + _SMALL_REPLICATED}
    for k in _WEIGHTS:
        if k not in final:
            layers = [res[k, l] for l in range(p[k].shape[0])]
            final[k] = tuple(jnp.stack([lay[part] for lay in layers]).reshape(p[k].shape) for part in range(4))
    res = final

    loss = lax.psum(loss, ("x", "y", "c"))
    out = [loss, grad_x[None]]
    for part in range(4):
        out += [res[k][part] for k in _WEIGHTS]
    return tuple(out)


def kernel(x, positions, pool_norm, pool_w_in, pool_w_grp, pool_scale, pool_w_out, conv_norm, conv_w_in, conv_w, conv_w_out, mla_norm, mla_w_in, mla_q_norm, mla_w_q_up, mla_kv_norm, mla_w_kv_up, mla_w_out, final_norm, loss_target, m_pool_norm, m_pool_w_in, m_pool_w_grp, m_pool_scale, m_pool_w_out, m_conv_norm, m_conv_w_in, m_conv_w, m_conv_w_out, m_mla_norm, m_mla_w_in, m_mla_q_norm, m_mla_w_q_up, m_mla_kv_norm, m_mla_w_kv_up, m_mla_w_out, m_final_norm, v_pool_norm, v_pool_w_in, v_pool_w_grp, v_pool_scale, v_pool_w_out, v_conv_norm, v_conv_w_in, v_conv_w, v_conv_w_out, v_mla_norm, v_mla_w_in, v_mla_q_norm, v_mla_w_q_up, v_mla_kv_norm, v_mla_w_kv_up, v_mla_w_out, v_final_norm):
    p = dict(pool_norm=pool_norm, pool_w_in=pool_w_in, pool_w_grp=pool_w_grp, pool_scale=pool_scale, pool_w_out=pool_w_out,
             conv_norm=conv_norm, conv_w_in=conv_w_in, conv_w=conv_w, conv_w_out=conv_w_out, mla_norm=mla_norm, mla_w_in=mla_w_in,
             mla_q_norm=mla_q_norm, mla_w_q_up=mla_w_q_up, mla_kv_norm=mla_kv_norm, mla_w_kv_up=mla_w_kv_up, mla_w_out=mla_w_out,
             final_norm=final_norm)
    m = dict(pool_norm=m_pool_norm, pool_w_in=m_pool_w_in, pool_w_grp=m_pool_w_grp, pool_scale=m_pool_scale, pool_w_out=m_pool_w_out,
             conv_norm=m_conv_norm, conv_w_in=m_conv_w_in, conv_w=m_conv_w, conv_w_out=m_conv_w_out, mla_norm=m_mla_norm,
             mla_w_in=m_mla_w_in, mla_q_norm=m_mla_q_norm, mla_w_q_up=m_mla_w_q_up, mla_kv_norm=m_mla_kv_norm,
             mla_w_kv_up=m_mla_w_kv_up, mla_w_out=m_mla_w_out, final_norm=m_final_norm)
    v = dict(pool_norm=v_pool_norm, pool_w_in=v_pool_w_in, pool_w_grp=v_pool_w_grp, pool_scale=v_pool_scale, pool_w_out=v_pool_w_out,
             conv_norm=v_conv_norm, conv_w_in=v_conv_w_in, conv_w=v_conv_w, conv_w_out=v_conv_w_out, mla_norm=v_mla_norm,
             mla_w_in=v_mla_w_in, mla_q_norm=v_mla_q_norm, mla_w_q_up=v_mla_w_q_up, mla_kv_norm=v_mla_kv_norm,
             mla_w_kv_up=v_mla_w_kv_up, mla_w_out=v_mla_w_out, final_norm=v_final_norm)
    return _step(x, positions, loss_target, p, m, v)
```

```python
import functools

import jax
import jax.numpy as jnp
from jax import lax
from jax.experimental import pallas as pl
from jax.experimental.pallas import tpu as pltpu

BF = jnp.bfloat16
F32 = jnp.float32

N_DEV = 8
D_MODEL = 1024
D_INNER = 2048
POOL_WINDOWS = (2, 4, 8, 16)
POOL_GROUP = 512
N_HEADS = 16
QK_NOPE = 128
QK_ROPE = 64
QK_DIM = QK_NOPE + QK_ROPE
V_DIM = 128
Q_RANK = 384
KV_RANK = 256
ATTN_SCALE = QK_DIM ** -0.5
LOG2_E = 1.4426950408889634
LN_2 = 0.6931471805599453
Q_PRESCALE = ATTN_SCALE * LOG2_E
ATTN_TILE = 512
ATTN_HEADS_PER_STEP = 2
ROPE_BASE = 10000.0
NORM_EPS = 1e-6
NEG_BIG = -1e30

ADAM_LR = 0.001
ADAM_B1 = 0.9
ADAM_B2 = 0.999
ADAM_EPS = 1e-08
ADAM_WD = 0.01
ADAM_STEP = 10

VMEM_LIMIT_BYTES = 52 * 1024 * 1024
IN_PROJ_ROWS = 1024
POOL_HALO = 32
CONV_HALO = 16

NN = (((1,), (0,)), ((), ()))
NT = (((1,), (1,)), ((), ()))
TN = (((0,), (0,)), ((), ()))

SMALL_ROWS_AG = 16
SMALL_ROWS_RS = 32


def _sds(shape, dtype):
    return jax.ShapeDtypeStruct(tuple(shape), dtype)


def _params():
    return pltpu.CompilerParams(vmem_limit_bytes=VMEM_LIMIT_BYTES)


_ANY = pl.BlockSpec(memory_space=pl.ANY)


def _dot(a, b, dims):
    return lax.dot_general(a, b, dims, preferred_element_type=F32)


def _sig(z):
    return 1.0 / (1.0 + jnp.exp(-z))


def _silu_and_grad(z):
    sig = _sig(z)
    return z * sig, sig * (1.0 + z * (1.0 - sig))


def _to_row(col):
    return jnp.broadcast_to(col, (col.shape[0], 128)).T[0:1, :]


def _rope_swap(x, p):
    pb = p.astype(BF)
    hi = x.astype(BF)
    r1 = x - hi.astype(F32)
    mid = r1.astype(BF)
    lo = (r1 - mid.astype(F32)).astype(BF)
    return (_dot(hi, pb, NN) + _dot(mid, pb, NN)) + _dot(lo, pb, NN)


def _rope_fwd(x, cosf, sinf, p):
    return x * cosf + _rope_swap(x, p) * sinf


def _rope_bwd(dy, cosf, sinf, p):
    return dy * cosf + _rope_swap(dy * sinf, p)


def _rms_bwd(dxn, x, g, res):
    r = lax.rsqrt(jnp.mean(x * x, axis=-1, keepdims=True) + NORM_EPS)
    v = dxn * g
    dx = r * v - x * ((r * r * r) * jnp.mean(v * x, axis=-1, keepdims=True))
    if res is not None:
        dx = dx + res
    dg = jnp.sum(dxn * (x * r), axis=0, keepdims=True)
    return dx, dg


def _accumulate(ref, val, step):
    @pl.when(step == 0)
    def _():
        ref[...] = val

    @pl.when(step > 0)
    def _():
        ref[...] += val


def _mm(name, grid, ins, in_specs, outs, out_specs, dims, epi, red=None, acc_shape=None):
    n_in, n_out = len(ins), len(outs)
    n_red = None if red is None else grid[red]

    def body(*refs):
        in_refs, out_refs = refs[:n_in], refs[n_in:n_in + n_out]
        pids = tuple(pl.program_id(ax) for ax in range(len(grid)))
        a, b = in_refs[0][...], in_refs[1][...]
        if a.ndim == 3:
            a = a.reshape(-1, a.shape[-1])
        if b.ndim == 3:
            b = b.reshape(-1, b.shape[-1])
        part = _dot(a.astype(BF), b.astype(BF), dims)
        if red is None:
            epi(part, in_refs[2:], out_refs, pids)
        else:
            acc = refs[n_in + n_out]
            k = pids[red]
            _accumulate(acc, part, k)

            @pl.when(k == n_red - 1)
            def _():
                epi(acc[...], in_refs[2:], out_refs, pids)

    scratch = [] if red is None else [pltpu.VMEM(acc_shape, F32)]
    return pl.pallas_call(body, name=name, grid=grid, in_specs=in_specs, out_specs=out_specs, out_shape=outs,
                          scratch_shapes=scratch, compiler_params=_params())(*ins)


def _store(part, extra, outs, pids):
    outs[0][...] = part.astype(outs[0].dtype)


def _rms_fwd(name, x, g, tm):
    s, d = x.shape

    def body(x_ref, g_ref, o_ref):
        xv = x_ref[...]
        r = lax.rsqrt(jnp.mean(xv * xv, axis=-1, keepdims=True) + NORM_EPS)
        o_ref[...] = ((xv * r) * g_ref[...]).astype(BF)

    return pl.pallas_call(body, name=name, grid=(s // tm,),
                          in_specs=[pl.BlockSpec((tm, d), lambda i: (i, 0)), pl.BlockSpec((1, d), lambda i: (0, 0))],
                          out_specs=pl.BlockSpec((tm, d), lambda i: (i, 0)), out_shape=_sds((s, d), BF),
                          compiler_params=_params())(x, g.reshape(1, d))


def _norm_in_proj(name, x, g, wbuf, w_index, n_j, tm):
    s = x.shape[0]
    ti = IN_PROJ_ROWS if s % IN_PROJ_ROWS == 0 else tm

    def body(x_ref, g_ref, w_ref, h_ref, xn_ref):
        @pl.when(pl.program_id(1) == 0)
        def _():
            xv = x_ref[...]
            r = lax.rsqrt(jnp.mean(xv * xv, axis=-1, keepdims=True) + NORM_EPS)
            xn_ref[...] = ((xv * r) * g_ref[...]).astype(BF)

        h_ref[...] = _dot(xn_ref[...], w_ref[...], NN).astype(BF)

    row = lambda i, j: (i, 0)
    return pl.pallas_call(
        body, name=name, grid=(s // ti, n_j),
        in_specs=[pl.BlockSpec((ti, D_MODEL), row), pl.BlockSpec((1, D_MODEL), lambda i, j: (0, 0)),
                  pl.BlockSpec((None, D_MODEL, 512), w_index)],
        out_specs=[pl.BlockSpec((ti, 512), lambda i, j: (i, j)), pl.BlockSpec((ti, D_MODEL), row)],
        out_shape=[_sds((s, n_j * 512), BF), _sds((s, D_MODEL), BF)], compiler_params=_params())(x, g.reshape(1, D_MODEL), wbuf)


def _tn(name, a, b, out_shape, out_block, out_index, a_cols, b_cols, grid, a_index, b_index, dep=()):
    s = a.shape[-2]
    a_block = (s, a_cols) if a.ndim == 2 else (None, s, a_cols)
    b_block = (s, b_cols) if b.ndim == 2 else (None, s, b_cols)

    def epi(part, extra, outs, pids):
        outs[0][...] = part.astype(BF).reshape(outs[0].shape)

    return _mm(name, grid, [a, b] + list(dep), [pl.BlockSpec(a_block, a_index), pl.BlockSpec(b_block, b_index)] + [_ANY] * len(dep),
               [_sds(out_shape, BF)], [pl.BlockSpec(out_block, out_index)], TN, epi)[0]


def _pool_window_fwd(name, h, tm):
    s = h.shape[0]
    hb = POOL_HALO

    def body(u_ref, halo_ref, o_ref, e_ref, a_ref, b_ref):
        i = pl.program_id(0)
        row = lax.broadcasted_iota(jnp.int32, (tm, 1), 0) + i * tm
        for g, w in enumerate(POOL_WINDOWS):
            cs = slice(g * POOL_GROUP, (g + 1) * POOL_GROUP)
            e_ref[0:hb, :] = jnp.where(i > 0, halo_ref[:, cs].astype(F32), 0.0)
            e_ref[hb:, :] = u_ref[:, cs].astype(F32)
            src, bufs = e_ref, (a_ref, b_ref)
            for lv in range(1, w.bit_length()):
                dst, st, sh = bufs[(lv - 1) % 2], 8 * lv, 2 ** (lv - 1)
                n = hb + tm - st
                dst[st:, :] = src[st:, :] + src[pl.ds(st - sh, n), :]
                src = dst
            cnt = jnp.minimum(row + 1, w).astype(F32)
            o_ref[:, cs] = (src[hb:, :] / cnt - u_ref[:, cs].astype(F32)).astype(BF)

    per = tm // hb
    return pl.pallas_call(
        body, name=name, grid=(s // tm,),
        in_specs=[pl.BlockSpec((tm, D_INNER), lambda i: (i, 0)),
                  pl.BlockSpec((hb, D_INNER), lambda i: (jnp.maximum(i * per - 1, 0), 0))],
        out_specs=pl.BlockSpec((tm, D_INNER), lambda i: (i, 0)), out_shape=_sds((s, D_INNER), BF),
        scratch_shapes=[pltpu.VMEM((hb + tm, POOL_GROUP), F32)] * 3, compiler_params=_params())(h, h)


def _pool_window_bwd(name, dp, tm, dh):
    s = dp.shape[0]
    nt = s // tm
    hb = POOL_HALO

    def body(d_ref, halo_ref, dh_in_ref, o_ref, e_ref, a_ref, b_ref):
        i = pl.program_id(0)
        row = lax.broadcasted_iota(jnp.int32, (tm, 1), 0) + i * tm
        hrow = lax.broadcasted_iota(jnp.int32, (hb, 1), 0) + (i + 1) * tm
        for g, w in enumerate(POOL_WINDOWS):
            cs = slice(g * POOL_GROUP, (g + 1) * POOL_GROUP)
            e_ref[0:tm, :] = d_ref[:, cs] / jnp.minimum(row + 1, w).astype(F32)
            e_ref[tm:, :] = jnp.where(i < nt - 1, halo_ref[:, cs] / jnp.minimum(hrow + 1, w).astype(F32), 0.0)
            src, bufs = e_ref, (a_ref, b_ref)
            for lv in range(1, w.bit_length()):
                dst, sh = bufs[(lv - 1) % 2], 2 ** (lv - 1)
                n = tm + hb - 8 * lv
                dst[0:n, :] = src[0:n, :] + src[pl.ds(sh, n), :]
                src = dst
            o_ref[:, cs] = (src[0:tm, :] - d_ref[:, cs]).astype(BF)

    per = tm // hb
    last = s // hb - 1
    return pl.pallas_call(
        body, name=name, grid=(nt,),
        in_specs=[pl.BlockSpec((tm, D_INNER), lambda i: (i, 0)),
                  pl.BlockSpec((hb, D_INNER), lambda i: (jnp.minimum((i + 1) * per, last), 0)), _ANY],
        out_specs=pl.BlockSpec((tm, D_INNER), lambda i: (i, 0)), out_shape=_sds(dh.shape, BF),
        input_output_aliases={2: 0},
        scratch_shapes=[pltpu.VMEM((hb + tm, POOL_GROUP), F32)] * 3, compiler_params=_params())(dp, dp, dh)


def _grp_block():
    return pl.BlockSpec((N_DEV, 64, POOL_GROUP), lambda i, g: (0, g, 0))


def _pool_fwd(x, l, w, tm):
    s = x.shape[0]
    nt = s // tm
    n = f"pool{l}"
    h, xn = _norm_in_proj(n + "_in", x, w["norm"], w["g_in"], lambda i, j: (j, 0, 0), 8, tm)
    pooled = _pool_window_fwd(n + "_win", h, tm)
    w = dict(w, **w["rest"]([h]))

    def gate(part, extra, outs, pids):
        z = extra[0][...].astype(F32)
        outs[0][...] = ((part * extra[1][...]) * (z * _sig(z))).astype(BF)

    (gated,) = _mm(n + "_grp", (nt, 4), [pooled, w["g_grp"], h, w["scale"].reshape(1, D_INNER)],
                   [pl.BlockSpec((tm, 512), lambda i, g: (i, g)), _grp_block(),
                    pl.BlockSpec((tm, 512), lambda i, g: (i, 4 + g)), pl.BlockSpec((1, 512), lambda i, g: (0, g))],
                   [_sds((s, D_INNER), BF)], [pl.BlockSpec((tm, 512), lambda i, g: (i, g))], NN, gate)
    y = _out_proj(n + "_out", gated, w["g_out"], 0, x, tm)
    return y, dict(x=x, xn=xn, h=h, pooled=pooled, gated=gated, w=w)


def _out_proj(name, gated, g1024, row_block, x, tm):
    s = x.shape[0]

    def epi(part, extra, outs, pids):
        outs[0][...] = part + extra[0][...]

    row = pl.BlockSpec((tm, D_MODEL), lambda i: (i, 0))
    return _mm(name, (s // tm,), [gated, g1024, x],
               [pl.BlockSpec((tm, D_INNER), lambda i: (i, 0)), pl.BlockSpec((N_DEV, 256, D_MODEL), lambda i: (0, row_block, 0)), row],
               [_sds((s, D_MODEL), F32)], [row], NN, epi)[0]


def _w_out_nt_block(row_block):
    return pl.BlockSpec((2, 256, D_MODEL), lambda j, i: (j, row_block, 0))


def _in_proj_bwd(name, dh, wbuf, w_index, n_k, x, g, dy, tm, dep=()):
    s = x.shape[0]
    tm = IN_PROJ_ROWS if s % IN_PROJ_ROWS == 0 else tm

    def epi(acc, extra, outs, pids):
        dx, dg = _rms_bwd(acc, extra[0][...], extra[1][...], extra[2][...])
        outs[0][...] = dx
        outs[1][...] = dx.astype(BF)
        _accumulate(outs[2], dg, pids[0])

    row = lambda i, k: (i, 0)
    return _mm(name, (s // tm, n_k), [dh, wbuf, x, g.reshape(1, D_MODEL), dy] + list(dep),
               [pl.BlockSpec((tm, 512), lambda i, k: (i, k)), pl.BlockSpec((None, D_MODEL, 512), w_index),
                pl.BlockSpec((tm, D_MODEL), row), pl.BlockSpec((1, D_MODEL), lambda i, k: (0, 0)), pl.BlockSpec((tm, D_MODEL), row)]
               + [_ANY] * len(dep),
               [_sds((s, D_MODEL), F32), _sds((s, D_MODEL), BF), _sds((1, D_MODEL), F32)],
               [pl.BlockSpec((tm, D_MODEL), row), pl.BlockSpec((tm, D_MODEL), row), pl.BlockSpec((1, D_MODEL), lambda i, k: (0, 0))],
               NT, epi, red=1, acc_shape=(tm, D_MODEL))


def _w_out_grad(name, gated, dyb):
    s = gated.shape[0]
    return _tn(name, gated, dyb, (D_INNER, D_MODEL), (512, D_MODEL), lambda i: (i, 0), 512, D_MODEL, (4,),
               lambda i: (0, i), lambda i: (0, 0))


def _pool_bwd(dy, dyb, l, w, sv, tm, dep, early=None):
    s = dy.shape[0]
    nt = s // tm
    n = f"pool{l}b"
    h, pooled = sv["h"], sv["pooled"]
    scale = w["scale"].reshape(1, D_INNER)

    def gate_bwd(part, extra, outs, pids):
        z, sc = extra[0][...].astype(F32), extra[1][...]
        wg = extra[3][...].reshape(POOL_GROUP, POOL_GROUP)
        mpv = _dot(extra[2][...], wg, NN)
        sz, dsz = _silu_and_grad(z)
        dm = part * sz
        dmp = (dm * sc).astype(BF)
        outs[0][...] = dmp
        outs[1][...] = (part * (mpv * sc) * dsz).astype(BF)
        _accumulate(outs[2], jnp.sum(dm * mpv, axis=0, keepdims=True), pids[1])
        outs[3][...] = _dot(dmp, wg, NT)

    tile = lambda j, i: (i, j)
    dmp, dz, dscale, dpool = _mm(
        n + "_out", (4, nt), [dyb, w["g_out"], h, scale, pooled, w["g_grp"]] + dep,
        [pl.BlockSpec((tm, D_MODEL), lambda j, i: (i, 0)), _w_out_nt_block(0),
         pl.BlockSpec((tm, 512), lambda j, i: (i, 4 + j)), pl.BlockSpec((1, 512), lambda j, i: (0, j)), pl.BlockSpec((tm, 512), tile),
         pl.BlockSpec((N_DEV, 64, POOL_GROUP), lambda j, i: (0, j, 0))] + [_ANY] * len(dep),
        [_sds((s, D_INNER), BF), _sds((s, 2 * D_INNER), BF), _sds((1, D_INNER), F32), _sds((s, D_INNER), F32)],
        [pl.BlockSpec((tm, 512), tile), pl.BlockSpec((tm, 512), lambda j, i: (i, 4 + j)), pl.BlockSpec((1, 512), lambda j, i: (0, j)),
         pl.BlockSpec((tm, 512), tile)],
        NT, gate_bwd)
    g_out = _w_out_grad(n + "_gout", sv["gated"], dyb).reshape(N_DEV, 256, D_MODEL)
    g_grp = _tn(n + "_ggrp", pooled, dmp, (N_DEV, 256, 512), (N_DEV, 64, 512), lambda g: (0, g, 0),
                512, 512, (4,), lambda g: (0, g), lambda g: (0, g))
    dep = early(dict(g_grp=g_grp, g_out=g_out)) if early is not None else ()
    dh = _pool_window_bwd(n + "_win", dpool, tm, dz)
    g_in = _tn(n + "_gin", sv["xn"], dh, (N_DEV, D_MODEL, 512), (None, D_MODEL, 512), lambda j: (j, 0, 0),
               D_MODEL, 512, (8,), lambda j: (0, 0), lambda j: (0, j), dep)
    dep = early(dict(g_in=g_in)) if early is not None else ()
    dx, dxb, dnorm = _in_proj_bwd(n + "_in", dh, w["g_in"], lambda i, k: (k, 0, 0), 8, sv["x"], w["norm"], dy, tm, dep)
    return dx, dxb, dict(g_in=g_in, g_grp=g_grp, g_out=g_out, norm=dnorm[0], scale=dscale[0])


def _conv_in_index(i, j):
    return (j // 2, 0, j % 2)


def _conv_fwd(x, w, tm):
    s = x.shape[0]
    nt = s // tm
    h, xn = _norm_in_proj("conv_in", x, w["norm"], w["g_in"], _conv_in_index, 16, tm)
    per = tm // CONV_HALO

    def body(b_ref, c_ref, h_ref, z_ref, cp_ref, hp_ref, w_ref, o_ref, e_ref):
        i = pl.program_id(0)
        ch = c_ref[...].astype(F32) * h_ref[...].astype(F32)
        e_ref[0:CONV_HALO, :] = jnp.where(i > 0, cp_ref[...].astype(F32) * hp_ref[...].astype(F32), 0.0)
        e_ref[CONV_HALO:, :] = ch
        co = (w_ref[2:3, :] * ch + w_ref[1:2, :] * e_ref[pl.ds(CONV_HALO - 1, tm), :]
              + w_ref[0:1, :] * e_ref[pl.ds(CONV_HALO - 2, tm), :])
        z = z_ref[...].astype(F32)
        o_ref[...] = ((b_ref[...].astype(F32) * co) * (z * _sig(z))).astype(BF)

    def col(q):
        return pl.BlockSpec((tm, 512), lambda i, j: (i, 4 * q + j))

    def prev(q):
        return pl.BlockSpec((CONV_HALO, 512), lambda i, j: (jnp.maximum(i * per - 1, 0), 4 * q + j))

    gated = pl.pallas_call(
        body, name="conv_mix", grid=(nt, 4),
        in_specs=[col(0), col(1), col(2), col(3), prev(1), prev(2), pl.BlockSpec((3, 512), lambda i, j: (0, j))],
        out_specs=pl.BlockSpec((tm, 512), lambda i, j: (i, j)), out_shape=_sds((s, D_INNER), BF),
        scratch_shapes=[pltpu.VMEM((CONV_HALO + tm, 512), F32)], compiler_params=_params())(h, h, h, h, h, h, w["conv_w"])
    w = dict(w, **w["rest"]([gated]))
    y = _out_proj("conv_out", gated, w["g_out"], 0, x, tm)
    return y, dict(x=x, xn=xn, h=h, gated=gated, w=w)


def _conv_bwd(dy, dyb, w, sv, tm, dep):
    s = dy.shape[0]
    nt = s // tm
    h = sv["h"]
    per = tm // CONV_HALO
    last = s // CONV_HALO - 1
    n_dep = len(dep)

    def body(dy_ref, dyn_ref, wo_ref, b_ref, c_ref, h_ref, z_ref, cp_ref, hp_ref, bn_ref, zn_ref, w_ref, *rest):
        dall_ref, dw_ref, e_ref, f_ref = rest[n_dep:]
        db_ref, dc_ref, dh_ref, dz_ref = (dall_ref.at[:, q * 512:(q + 1) * 512] for q in range(4))
        i = pl.program_id(1)
        wo = wo_ref[...].reshape(512, D_MODEL)
        dg_tile = _dot(dy_ref[...], wo, NT)
        dg_next = _dot(dyn_ref[...], wo, NT)
        w0, w1, w2 = w_ref[0:1, :], w_ref[1:2, :], w_ref[2:3, :]
        c, hh, b = c_ref[...].astype(F32), h_ref[...].astype(F32), b_ref[...].astype(F32)
        ch = c * hh
        e_ref[0:CONV_HALO, :] = jnp.where(i > 0, cp_ref[...].astype(F32) * hp_ref[...].astype(F32), 0.0)
        e_ref[CONV_HALO:, :] = ch
        ch1 = e_ref[pl.ds(CONV_HALO - 1, tm), :]
        ch2 = e_ref[pl.ds(CONV_HALO - 2, tm), :]
        co = w2 * ch + w1 * ch1 + w0 * ch2
        sz, dsz = _silu_and_grad(z_ref[...].astype(F32))
        dgv = dg_tile
        dyv = dgv * sz
        dz_ref[...] = (dgv * (b * co) * dsz).astype(BF)
        db_ref[...] = (dyv * co).astype(BF)
        dco = dyv * b
        zn = zn_ref[...].astype(F32)
        f_ref[0:tm, :] = dco
        f_ref[tm:, :] = jnp.where(i < nt - 1, dg_next * (zn * _sig(zn)) * bn_ref[...].astype(F32), 0.0)
        dch = w2 * dco + w1 * f_ref[pl.ds(1, tm), :] + w0 * f_ref[pl.ds(2, tm), :]
        dc_ref[...] = (dch * hh).astype(BF)
        dh_ref[...] = (dch * c).astype(BF)
        for tap, shifted in enumerate((ch2, ch1, ch)):
            _accumulate(dw_ref.at[tap:tap + 1, :], jnp.sum(dco * shifted, axis=0, keepdims=True), i)

    def col(q):
        return pl.BlockSpec((tm, 512), lambda j, i: (i, 4 * q + j))

    def prev(q):
        return pl.BlockSpec((CONV_HALO, 512), lambda j, i: (jnp.maximum(i * per - 1, 0), 4 * q + j))

    def nxt(q):
        return pl.BlockSpec((CONV_HALO, 512), lambda j, i: (jnp.minimum((i + 1) * per, last), 4 * q + j))

    wspec = pl.BlockSpec((3, 512), lambda j, i: (0, j))
    dy_tile = pl.BlockSpec((tm, D_MODEL), lambda j, i: (i, 0))
    dy_next = pl.BlockSpec((CONV_HALO, D_MODEL), lambda j, i: (jnp.minimum((i + 1) * per, last), 0))
    dh, dw = pl.pallas_call(
        body, name="convb_mix", grid=(4, nt),
        in_specs=[dy_tile, dy_next, _w_out_nt_block(0), col(0), col(1), col(2), col(3), prev(1), prev(2), nxt(0), nxt(3), wspec]
        + [_ANY] * n_dep,
        out_specs=[pl.BlockSpec((tm, D_INNER), lambda j, i: (i, j)), wspec],
        out_shape=[_sds((s, 4 * D_INNER), BF), _sds((3, D_INNER), F32)],
        scratch_shapes=[pltpu.VMEM((CONV_HALO + tm, 512), F32)] * 2, compiler_params=_params(),
    )(dyb, dyb, w["g_out"], h, h, h, h, h, h, h, h, w["conv_w"], *dep)

    def w_block(kp):
        k = 4 * (kp % 4) + kp // 4
        return (k // 2, 0, k % 2)

    dx, dxb, dnorm = _in_proj_bwd("convb_in", dh, w["g_in"], lambda i, kp: w_block(kp), 16, sv["x"], w["norm"], dy, tm)
    g_in = _tn("convb_gin", sv["xn"], dh, (N_DEV, D_MODEL, D_MODEL), (None, D_MODEL, 512), lambda j: (j // 2, 0, j % 2),
               D_MODEL, 512, (16,), lambda j: (0, 0), lambda j: (0, 4 * (j % 4) + j // 4))
    g_out = _w_out_grad("convb_gout", sv["gated"], dyb)
    return dx, dxb, dict(g_in=g_in, g_out=g_out.reshape(N_DEV, 256, D_MODEL), norm=dnorm[0], conv_w=dw)


def _attn_tiles(s):
    t = min(ATTN_TILE, s)
    return t, s // t


def _causal_keep(t, keys_on_rows):
    r = lax.broadcasted_iota(jnp.int32, (t, t), 0)
    c = lax.broadcasted_iota(jnp.int32, (t, t), 1)
    return (r <= c) if keys_on_rows else (c <= r)


def _mla_fwd(x, w, rope, tm):
    s = x.shape[0]
    nt = s // tm
    cosf, sinf, perm = rope
    xn = _rms_fwd("mla_rms", x, w["norm"], tm)

    def in_body(xn_ref, wq_ref, wkv_ref, wkr_ref, wz_ref, gq_ref, gkv_ref, cos_ref, sin_ref, p_ref,
                ql_ref, kvl_ref, qn_ref, kvn_ref, krr_ref, z_ref):
        xv = xn_ref[...]
        ql = _dot(xv, wq_ref[...], NN)
        kvl = _dot(xv, wkv_ref[...], NN)
        ql_ref[...] = ql
        kvl_ref[...] = kvl
        rq = lax.rsqrt(jnp.mean(ql * ql, axis=-1, keepdims=True) + NORM_EPS)
        qn_ref[...] = ((ql * rq) * gq_ref[...]).astype(BF)
        rkv = lax.rsqrt(jnp.mean(kvl * kvl, axis=-1, keepdims=True) + NORM_EPS)
        kvn_ref[...] = ((kvl * rkv) * gkv_ref[...]).astype(BF)
        kr = _dot(xv, wkr_ref[...], NN)
        krr_ref[...] = _rope_fwd(kr, cos_ref[...], sin_ref[...], p_ref[...]).astype(BF)
        z_ref[...] = _dot(xv, wz_ref[...], NN).astype(BF)

    def full(a):
        return pl.BlockSpec(a.shape, lambda i: (0,) * a.ndim)

    def rows(c):
        return pl.BlockSpec((tm, c), lambda i: (i, 0))

    gq, gkv = w["q_norm"].reshape(1, Q_RANK), w["kv_norm"].reshape(1, KV_RANK)
    q_lat, kv_lat, qn, kvn, krr, z = pl.pallas_call(
        in_body, name="mla_in", grid=(nt,),
        in_specs=[rows(D_MODEL), full(w["w_q"]), full(w["w_kv"]), full(w["w_kr"]), full(w["w_z"]), full(gq), full(gkv),
                  rows(QK_ROPE), rows(QK_ROPE), full(perm)],
        out_specs=[rows(Q_RANK), rows(KV_RANK), rows(Q_RANK), rows(KV_RANK), rows(QK_ROPE), rows(D_INNER)],
        out_shape=[_sds((s, Q_RANK), F32), _sds((s, KV_RANK), F32), _sds((s, Q_RANK), BF), _sds((s, KV_RANK), BF),
                   _sds((s, QK_ROPE), BF), _sds((s, D_INNER), BF)],
        compiler_params=_params())(xn, w["w_q"], w["w_kv"], w["w_kr"], w["w_z"], gq, gkv, cosf, sinf, perm)

    def q_epi(part, extra, outs, pids):
        outs[0][:, 0:QK_NOPE] = (part[:, 0:QK_NOPE] * Q_PRESCALE).astype(BF)
        roped = _rope_fwd(part[:, QK_NOPE:QK_DIM], extra[0][...], extra[1][...], extra[2][...])
        outs[0][:, QK_NOPE:QK_DIM] = (roped * Q_PRESCALE).astype(BF)

    tp = IN_PROJ_ROWS if s % IN_PROJ_ROWS == 0 else tm
    rope_row = pl.BlockSpec((tp, QK_ROPE), lambda h, i: (i, 0))
    (q,) = _mm("mla_qup", (N_HEADS, s // tp), [qn, w["w_qh"], cosf, sinf, perm],
               [pl.BlockSpec((tp, Q_RANK), lambda h, i: (i, 0)), pl.BlockSpec((None, Q_RANK, QK_DIM), lambda h, i: (h, 0, 0)),
                rope_row, rope_row, pl.BlockSpec((QK_ROPE, QK_ROPE), lambda h, i: (0, 0))],
               [_sds((N_HEADS, s, QK_DIM), BF)], [pl.BlockSpec((None, tp, QK_DIM), lambda h, i: (h, i, 0))], NN, q_epi)

    def kv_epi(part, extra, outs, pids):
        outs[0][:, 0:QK_NOPE] = part[:, 0:QK_NOPE].astype(BF)
        outs[0][:, QK_NOPE:QK_DIM] = extra[0][...]
        outs[1][...] = part[:, QK_NOPE:].astype(BF)

    k, v = _mm("mla_kvup", (N_HEADS, s // tp), [kvn, w["g512"], krr],
               [pl.BlockSpec((tp, KV_RANK), lambda h, i: (i, 0)),
                pl.BlockSpec((None, KV_RANK, 256), lambda h, i: (h // 2, 0, h % 2)), rope_row],
               [_sds((N_HEADS, s, QK_DIM), BF), _sds((N_HEADS, s, V_DIM), BF)],
               [pl.BlockSpec((None, tp, QK_DIM), lambda h, i: (h, i, 0)), pl.BlockSpec((None, tp, V_DIM), lambda h, i: (h, i, 0))],
               NN, kv_epi)

    t, nq = _attn_tiles(s)

    def attn_body(q_ref, k_ref, v_ref, z_ref, o_ref, g_ref, lse_ref):
        i = pl.program_id(1)

        def block(j, carry, masked):
            start = pl.multiple_of(j * t, t)
            out = []
            for hh, (m, lsum, acc) in enumerate(carry):
                sc = _dot(q_ref[hh], k_ref[hh, pl.ds(start, t), :], NT)
                if masked:
                    sc = jnp.where(_causal_keep(t, False), sc, NEG_BIG)
                mn = jnp.maximum(m, jnp.max(sc, axis=-1, keepdims=True))
                alpha = jnp.exp2(m - mn)
                p = jnp.exp2(sc - mn)
                lsum = alpha * lsum + jnp.sum(p, axis=-1, keepdims=True)
                acc = alpha * acc + _dot(p.astype(BF), v_ref[hh, pl.ds(start, t), :], NN)
                out.append((mn, lsum, acc))
            return tuple(out)

        init = ((jnp.full((t, 1), NEG_BIG, F32), jnp.zeros((t, 1), F32), jnp.zeros((t, V_DIM), F32)),) * ATTN_HEADS_PER_STEP
        carry = lax.fori_loop(0, i, lambda j, c: block(j, c, False), init)
        for hh, (m, lsum, acc) in enumerate(block(i, carry, True)):
            cols = slice(hh * V_DIM, (hh + 1) * V_DIM)
            o = acc / lsum
            z = z_ref[:, cols].astype(F32)
            o_ref[:, cols] = o
            g_ref[:, cols] = (o * (z * _sig(z))).astype(BF)
            lse_ref[hh] = _to_row(m + jnp.log(lsum) * LOG2_E)

    hp = ATTN_HEADS_PER_STEP
    head_col = pl.BlockSpec((t, hp * V_DIM), lambda h, i: (i, h))
    o, gated, lse = pl.pallas_call(
        attn_body, name="mla_attn", grid=(N_HEADS // hp, nq),
        in_specs=[pl.BlockSpec((hp, t, QK_DIM), lambda h, i: (h, i, 0)), pl.BlockSpec((hp, s, QK_DIM), lambda h, i: (h, 0, 0)),
                  pl.BlockSpec((hp, s, V_DIM), lambda h, i: (h, 0, 0)), head_col],
        out_specs=[head_col, head_col, pl.BlockSpec((hp, None, 1, t), lambda h, i: (h, i, 0, 0))],
        out_shape=[_sds((s, D_INNER), F32), _sds((s, D_INNER), BF), _sds((N_HEADS, nq, 1, t), F32)],
        compiler_params=_params())(q, k, v, z)
    y = _out_proj("mla_out", gated, w["g1024"], 0, x, tm)
    return y, dict(x=x, xn=xn, q_lat=q_lat, kv_lat=kv_lat, qn=qn, kvn=kvn, z=z, q=q, k=k, v=v, o=o, lse=lse, gated=gated)


def _mla_bwd(dy, dyb, w, sv, rope, tm, dep):
    s = dy.shape[0]
    nt = s // tm
    cosf, sinf, perm = rope
    t, nq = _attn_tiles(s)
    assert t == tm, "the row statistics of the backward are laid out per attention tile"
    q, k, v, lse = sv["q"], sv["k"], sv["v"], sv["lse"]

    def gate_bwd(part, extra, outs, pids):
        z, o = extra[0][...].astype(F32), extra[1][...]
        sz, dsz = _silu_and_grad(z)
        do = part * sz
        outs[0][...] = do.astype(BF)
        outs[1][...] = (part * o * dsz).astype(BF)
        prod = do * o
        for hh in range(4):
            outs[2][hh] = _to_row(jnp.sum(prod[:, hh * V_DIM:(hh + 1) * V_DIM], axis=-1, keepdims=True))

    tile = lambda j, i: (i, j)
    dob, dz, delta = _mm(
        "mlab_out", (4, nt), [dyb, w["g1024"], sv["z"], sv["o"]] + dep,
        [pl.BlockSpec((tm, D_MODEL), lambda j, i: (i, 0)), _w_out_nt_block(0),
         pl.BlockSpec((tm, 512), tile), pl.BlockSpec((tm, 512), tile)] + [_ANY] * len(dep),
        [_sds((s, D_INNER), BF), _sds((s, D_INNER), BF), _sds((N_HEADS, nt, 1, tm), F32)],
        [pl.BlockSpec((tm, 512), tile), pl.BlockSpec((tm, 512), tile), pl.BlockSpec((4, None, 1, tm), lambda j, i: (j, i, 0, 0))],
        NT, gate_bwd)

    hp = ATTN_HEADS_PER_STEP

    def attn_bwd_body(k_ref, v_ref, q_ref, do_ref, lse_ref, dl_ref, cos_ref, sin_ref, p_ref, dkv_ref, dkr_ref, dq_ref, dq_acc):
        j = pl.program_id(1)

        @pl.when(j == 0)
        def _():
            dq_acc[...] = jnp.zeros(dq_acc.shape, F32)

        def block(i, carry, masked):
            rows = pl.ds(pl.multiple_of(i * t, t), t)
            out = []
            for hh, (dk, dv) in enumerate(carry):
                kb, vb = k_ref[hh], v_ref[hh]
                qb, dob_ = q_ref[hh, rows, :], do_ref[rows, hh * V_DIM:(hh + 1) * V_DIM]
                st = _dot(kb, qb, NT)
                if masked:
                    st = jnp.where(_causal_keep(t, True), st, NEG_BIG)
                pt = jnp.exp2(st - lse_ref[hh, i])
                dv = dv + _dot(pt.astype(BF), dob_, NN)
                dst = (pt * (_dot(vb, dob_, NT) - dl_ref[hh, i])).astype(BF)
                dk = dk + _dot(dst, qb, NN)
                dq_acc[hh, rows, :] += _dot(dst, kb, TN)
                out.append((dk, dv))
            return tuple(out)

        init = ((jnp.zeros((t, QK_DIM), F32), jnp.zeros((t, V_DIM), F32)),) * hp
        carry = block(j, init, True)
        carry = lax.fori_loop(j + 1, nq, lambda i, c: block(i, c, False), carry)
        for hh, (dk, dv) in enumerate(carry):
            dk = dk * LN_2
            base = hh * 2 * V_DIM
            dkv_ref[:, base:base + QK_NOPE] = dk[:, 0:QK_NOPE].astype(BF)
            dkv_ref[:, base + QK_NOPE:base + 2 * V_DIM] = dv.astype(BF)
            dkr_ref[hh] = dk[:, QK_NOPE:]

        @pl.when(j == nq - 1)
        def _():
            for hh in range(hp):
                for c in range(nq):
                    rows = slice(c * t, (c + 1) * t)
                    dq = dq_acc[hh, rows, :] * ATTN_SCALE
                    dq_ref[hh, rows, 0:QK_NOPE] = dq[:, 0:QK_NOPE].astype(BF)
                    dq_ref[hh, rows, QK_NOPE:] = _rope_bwd(dq[:, QK_NOPE:], cos_ref[rows, :], sin_ref[rows, :], p_ref[...]).astype(BF)

    row_stats = pl.BlockSpec((hp, nq, 1, t), lambda h, j: (h, 0, 0, 0))
    seq_rope = pl.BlockSpec((s, QK_ROPE), lambda h, j: (0, 0))
    head_seq = pl.BlockSpec((hp, s, QK_DIM), lambda h, j: (h, 0, 0))
    dkv, dkr_h, dq = pl.pallas_call(
        attn_bwd_body, name="mlab_attn", grid=(N_HEADS // hp, nq),
        in_specs=[pl.BlockSpec((hp, t, QK_DIM), lambda h, j: (h, j, 0)), pl.BlockSpec((hp, t, V_DIM), lambda h, j: (h, j, 0)),
                  head_seq, pl.BlockSpec((s, hp * V_DIM), lambda h, j: (0, h)), row_stats, row_stats, seq_rope, seq_rope,
                  pl.BlockSpec((QK_ROPE, QK_ROPE), lambda h, j: (0, 0))],
        out_specs=[pl.BlockSpec((t, hp * 2 * V_DIM), lambda h, j: (j, h)), pl.BlockSpec((hp, t, QK_ROPE), lambda h, j: (h, j, 0)), head_seq],
        out_shape=[_sds((s, N_HEADS * 2 * V_DIM), BF), _sds((N_HEADS, s, QK_ROPE), F32), _sds((N_HEADS, s, QK_DIM), BF)],
        scratch_shapes=[pltpu.VMEM((hp, s, QK_DIM), F32)],
        compiler_params=_params())(k, v, q, dob, lse, delta, cosf, sinf, perm)

    def dkr_body(d_ref, cos_ref, sin_ref, p_ref, o_ref):
        tot = d_ref[0]
        for hh in range(1, N_HEADS):
            tot = tot + d_ref[hh]
        o_ref[...] = _rope_bwd(tot, cos_ref[...], sin_ref[...], p_ref[...]).astype(BF)

    r64 = pl.BlockSpec((tm, QK_ROPE), lambda i: (i, 0))
    dkr = pl.pallas_call(
        dkr_body, name="mlab_dkr", grid=(nt,),
        in_specs=[pl.BlockSpec((N_HEADS, tm, QK_ROPE), lambda i: (0, i, 0)), r64, r64, pl.BlockSpec((QK_ROPE, QK_ROPE), lambda i: (0, 0))],
        out_specs=r64, out_shape=_sds((s, QK_ROPE), BF), compiler_params=_params())(dkr_h, cosf, sinf, perm)

    def lat_epi(acc, extra, outs, pids):
        dx, dg = _rms_bwd(acc, extra[0][...], extra[1][...], None)
        outs[0][...] = dx.astype(BF)
        _accumulate(outs[1], dg, pids[0])

    tp = IN_PROJ_ROWS if s % IN_PROJ_ROWS == 0 else tm

    def lat_bwd(name, a, a_spec, b, b_spec, n_k, lat, g, rank):
        row = lambda i, k: (i, 0)
        one = lambda i, k: (0, 0)
        return _mm(name, (s // tp, n_k), [a, b, lat, g.reshape(1, rank)],
                   [a_spec, b_spec, pl.BlockSpec((tp, rank), row), pl.BlockSpec((1, rank), one)],
                   [_sds((s, rank), BF), _sds((1, rank), F32)], [pl.BlockSpec((tp, rank), row), pl.BlockSpec((1, rank), one)],
                   NT, lat_epi, red=1, acc_shape=(tp, rank))

    d_ql, g_qnorm = lat_bwd("mlab_qup", dq, pl.BlockSpec((None, tp, QK_DIM), lambda i, h: (h, i, 0)),
                            w["w_qh"], pl.BlockSpec((None, Q_RANK, QK_DIM), lambda i, h: (h, 0, 0)), N_HEADS,
                            sv["q_lat"], w["q_norm"], Q_RANK)
    d_kvl, g_kvnorm = lat_bwd("mlab_kvup", dkv, pl.BlockSpec((tp, 512), lambda i, kk: (i, kk)),
                              w["g512"], pl.BlockSpec((None, KV_RANK, 512), lambda i, kk: (kk, 0, 0)), N_DEV,
                              sv["kv_lat"], w["kv_norm"], KV_RANK)

    def in_bwd(dql_ref, dkvl_ref, dkr_ref, dz_ref, wq_ref, wkv_ref, wkr_ref, wz_ref, x_ref, g_ref, dy_ref, dx_ref, dxb_ref, dg_ref):
        acc = (_dot(dql_ref[...], wq_ref[...], NT) + _dot(dkvl_ref[...], wkv_ref[...], NT)
               + _dot(dkr_ref[...], wkr_ref[...], NT) + _dot(dz_ref[...], wz_ref[...], NT))
        dx, dg = _rms_bwd(acc, x_ref[...], g_ref[...], dy_ref[...])
        dx_ref[...] = dx
        dxb_ref[...] = dx.astype(BF)
        _accumulate(dg_ref, dg, pl.program_id(0))

    def full(a):
        return pl.BlockSpec(a.shape, lambda i: (0,) * a.ndim)

    def rows(c):
        return pl.BlockSpec((tm, c), lambda i: (i, 0))

    gm = w["norm"].reshape(1, D_MODEL)
    dx, dxb, g_norm = pl.pallas_call(
        in_bwd, name="mlab_in", grid=(nt,),
        in_specs=[rows(Q_RANK), rows(KV_RANK), rows(QK_ROPE), rows(D_INNER), full(w["w_q"]), full(w["w_kv"]), full(w["w_kr"]),
                  full(w["w_z"]), rows(D_MODEL), full(gm), rows(D_MODEL)],
        out_specs=[rows(D_MODEL), rows(D_MODEL), full(gm)],
        out_shape=[_sds((s, D_MODEL), F32), _sds((s, D_MODEL), BF), _sds((1, D_MODEL), F32)],
        compiler_params=_params())(d_ql, d_kvl, dkr, dz, w["w_q"], w["w_kv"], w["w_kr"], w["w_z"], sv["x"], gm, dy)

    xn = sv["xn"]
    one = lambda j: (0, 0)
    g_q = _tn("mlab_gq", xn, d_ql, (D_MODEL, Q_RANK), (D_MODEL, Q_RANK), one, D_MODEL, Q_RANK, (1,), one, one)
    g_kv = _tn("mlab_gkv", xn, d_kvl, (D_MODEL, KV_RANK), (D_MODEL, KV_RANK), one, D_MODEL, KV_RANK, (1,), one, one)
    g_kr = _tn("mlab_gkr", xn, dkr, (D_MODEL, QK_ROPE), (D_MODEL, QK_ROPE), one, D_MODEL, QK_ROPE, (1,), one, one)
    g_z = _tn("mlab_gz", xn, dz, (D_MODEL, D_INNER), (D_MODEL, 512), lambda j: (0, j), D_MODEL, 512, (4,), one, lambda j: (0, j))
    g_in = jnp.concatenate([g_q, g_kv, g_kr, g_z], axis=1)
    g_qh = _tn("mlab_gqup", sv["qn"], dq, (N_HEADS, Q_RANK, QK_DIM), (None, Q_RANK, QK_DIM), lambda h: (h, 0, 0),
               Q_RANK, QK_DIM, (N_HEADS,), lambda h: (0, 0), lambda h: (h, 0, 0))
    g_kvup = _tn("mlab_gkvup", sv["kvn"], dkv, (N_DEV, KV_RANK, 512), (None, KV_RANK, 512), lambda j: (j, 0, 0),
                 KV_RANK, 512, (N_DEV,), lambda j: (0, 0), lambda j: (0, j))
    g_out = _w_out_grad("mlab_gout", sv["gated"], dyb)
    s384 = g_qh.reshape(N_DEV, 2, Q_RANK, QK_DIM).transpose(0, 2, 1, 3).reshape(N_DEV, Q_RANK, 2 * QK_DIM)
    s344 = g_in.reshape(D_MODEL, N_DEV, 344).transpose(1, 0, 2)
    return dx, dxb, dict(s344=s344, s384=s384, s512=g_kvup, s1024=g_out.reshape(N_DEV, 256, D_MODEL),
                         norm=g_norm[0], q_norm=g_qnorm[0], kv_norm=g_kvnorm[0])


def _loss_head(x, g, target, tm):
    s, d = x.shape

    def body(x_ref, g_ref, t_ref, dx_ref, dxb_ref, dg_ref, loss_ref):
        i = pl.program_id(0)
        xv, gv = x_ref[...], g_ref[...]
        r = lax.rsqrt(jnp.mean(xv * xv, axis=-1, keepdims=True) + NORM_EPS)
        err = (xv * r) * gv - t_ref[...]
        part = 0.5 * jnp.sum(jnp.mean(err * err, axis=-1, keepdims=True), axis=0, keepdims=True)
        dx, dg = _rms_bwd(err * (1.0 / d), xv, gv, None)
        dx_ref[...] = dx
        dxb_ref[...] = dx.astype(BF)
        _accumulate(dg_ref, dg, i)
        _accumulate(loss_ref, jnp.broadcast_to(part, loss_ref.shape), i)

    row = pl.BlockSpec((tm, d), lambda i: (i, 0))
    one = pl.BlockSpec((1, d), lambda i: (0, 0))
    return pl.pallas_call(
        body, name="loss_head", grid=(s // tm,), in_specs=[row, one, row],
        out_specs=[row, row, one, pl.BlockSpec((8, 128), lambda i: (0, 0))],
        out_shape=[_sds((s, d), F32), _sds((s, d), BF), _sds((1, d), F32), _sds((8, 128), F32)],
        compiler_params=_params())(x, g.reshape(1, d), target)


def _rope_tables(pos):
    inv_freq = ROPE_BASE ** (-jnp.arange(0, QK_ROPE, 2, dtype=F32) / QK_ROPE)
    ang = pos.astype(F32)[:, None] * inv_freq
    cos, sin = jnp.cos(ang), jnp.sin(ang)
    idx = jnp.arange(QK_ROPE)
    perm = (idx[:, None] == (idx[None, :] + QK_ROPE // 2) % QK_ROPE).astype(F32)
    return jnp.concatenate([cos, cos], axis=1), jnp.concatenate([-sin, sin], axis=1), perm


def _local_step(x, pos, target, final_norm, get_w, put_g):
    s = x.shape[0]
    tm = min(512, s)
    rope = _rope_tables(pos)
    w0 = get_w(0, [])
    x1, sv0 = _pool_fwd(x, 0, w0, tm)
    w1 = get_w(1, [x1])
    x2, sv1 = _conv_fwd(x1, w1, tm)
    w2 = get_w(2, [x2])
    x3, sv2 = _mla_fwd(x2, w2, rope, tm)
    w3 = get_w(3, [x3])
    x4, sv3 = _pool_fwd(x3, 1, w3, tm)
    d4, d4b, g_final, loss = _loss_head(x4, final_norm, target, tm)
    d3, d3b, gp1 = _pool_bwd(d4, d4b, 1, sv3["w"], sv3, tm, [])
    dep = put_g(3, gp1)
    d2, d2b, gm = _mla_bwd(d3, d3b, w2, sv2, rope, tm, dep)
    dep = put_g(2, gm)
    d1, d1b, gc = _conv_bwd(d2, d2b, sv1["w"], sv1, tm, dep)
    dep = put_g(1, gc)
    d0, _, gp0 = _pool_bwd(d1, d1b, 0, sv0["w"], sv0, tm, dep, early=lambda big: put_g(0, big))
    put_g(4, dict(gp0, final_norm=g_final[0]))
    return loss[0, 0], d0


def _pack_groups(p):
    bf = lambda a: a.astype(BF)
    grp = lambda l: bf(p["pool_w_grp"][l].reshape(4 * 64, POOL_GROUP))
    return [[bf(p["pool_w_in"][0]), _pack_small(p, SMALL_ROWS_AG)],
            [grp(0), bf(p["pool_w_out"][0])],
            [bf(p["conv_w_in"][0])],
            [bf(p["conv_w_out"][0])],
            [bf(p[k][0]) for k in ("mla_w_in", "mla_w_q_up", "mla_w_kv_up", "mla_w_out")],
            [bf(p["pool_w_in"][1]), grp(1), bf(p["pool_w_out"][1])]]


_SMALL_SHARDED = ("pool_norm", "pool_scale", "mla_norm", "mla_q_norm", "mla_kv_norm", "conv_w")
_SMALL_REPLICATED = ("conv_norm", "final_norm")


def _pack_small(p, rows, with_replicated=False):
    parts = [p[k].reshape(-1) for k in _SMALL_SHARDED]
    if with_replicated:
        parts += [p[k].reshape(-1) for k in _SMALL_REPLICATED]
    flat = jnp.concatenate(parts)
    return jnp.pad(flat, (0, rows * 128 - flat.shape[0])).reshape(rows, 128)


_SMALL_SHARD_SHAPES = dict(pool_norm=(2, 128), pool_scale=(2, 256), mla_norm=(1, 128), mla_q_norm=(1, 48),
                           mla_kv_norm=(1, 32), conv_w=(1, 3, 256), conv_norm=(1, 1024), final_norm=(1024,))


def _unpack_small(buf, with_replicated=False):
    flat = buf.reshape(-1)
    out, off = {}, 0
    for k in _SMALL_SHARDED + (_SMALL_REPLICATED if with_replicated else ()):
        shp = _SMALL_SHARD_SHAPES[k]
        n = 1
        for d in shp:
            n *= d
        out[k] = flat[off:off + n].reshape(shp)
        off += n
    return out


def _small_views(gsmall):
    flat = gsmall.reshape(N_DEV, -1)

    def cols(off, rows, width):
        return flat[:, off:off + rows * width].reshape(N_DEV, rows, width).transpose(1, 0, 2).reshape(rows, N_DEV * width)

    return dict(pool_norm=cols(0, 2, 128), pool_scale=cols(256, 2, 256), mla_norm=cols(768, 1, 128)[0],
                q_norm=cols(896, 1, 48)[0], kv_norm=cols(944, 1, 32)[0], conv_w=cols(976, 3, 256))


def _layer_weights(layer, bufs, small, conv_norm):
    if layer in (0, 3):
        l = 0 if layer == 0 else 1
        return dict(g_in=bufs[0], norm=small["pool_norm"][l], scale=small["pool_scale"][l])
    if layer == 1:
        return dict(g_in=bufs[0], norm=conv_norm.reshape(D_MODEL), conv_w=small["conv_w"])
    g344, g384, g512, g1024 = bufs
    w_in = g344.transpose(1, 0, 2).reshape(D_MODEL, N_DEV * 344)
    return dict(
        g512=g512, g1024=g1024,
        w_q=w_in[:, :Q_RANK], w_kv=w_in[:, Q_RANK:Q_RANK + KV_RANK],
        w_kr=w_in[:, Q_RANK + KV_RANK:Q_RANK + KV_RANK + QK_ROPE], w_z=w_in[:, Q_RANK + KV_RANK + QK_ROPE:],
        w_qh=g384.reshape(N_DEV, Q_RANK, 2, QK_DIM).transpose(0, 2, 1, 3).reshape(N_HEADS, Q_RANK, QK_DIM),
        norm=small["mla_norm"], q_norm=small["q_norm"], kv_norm=small["kv_norm"])


_GRAD_KEYS = {0: ("g_in", "g_grp", "g_out"), 3: ("g_in", "g_grp", "g_out"), 1: ("g_in", "g_out"), 2: ("s344", "s384", "s512", "s1024")}
_GRAD_PARAM = {0: dict(g_in="pool_w_in", g_grp="pool_w_grp", g_out="pool_w_out"), 1: dict(g_in="conv_w_in", g_out="conv_w_out"),
               2: dict(s344="mla_w_in", s384="mla_w_q_up", s512="mla_w_kv_up", s1024="mla_w_out")}
_GRAD_PARAM[3] = _GRAD_PARAM[0]


def _grad_group(layer, g):
    keys = tuple(k for k in _GRAD_KEYS[layer] if k in g)
    return keys, [g[k] for k in keys]


def _pack_small_grads(g):
    def split(a, rows, width):
        return a.reshape(rows, N_DEV, width).transpose(1, 0, 2).reshape(N_DEV, rows * width)

    rep = lambda a: jnp.broadcast_to(a.reshape(1, -1), (N_DEV, a.size))
    flat = jnp.concatenate([
        split(jnp.stack([g[0]["norm"], g[3]["norm"]]), 2, 128), split(jnp.stack([g[0]["scale"], g[3]["scale"]]), 2, 256),
        split(g[2]["norm"], 1, 128), split(g[2]["q_norm"], 1, 48), split(g[2]["kv_norm"], 1, 32), split(g[1]["conv_w"], 3, 256),
        rep(g[1]["norm"]), rep(g[0]["final_norm"])], axis=1)
    return jnp.pad(flat, ((0, 0), (0, SMALL_ROWS_RS * 128 - flat.shape[1]))).reshape(N_DEV, SMALL_ROWS_RS, 128)


def _peers(x, y, c):
    for k in range(1, N_DEV):
        px = 1 - x if k & 4 else x
        py = 1 - y if k & 2 else y
        pc = 1 - c if k & 1 else c
        yield k - 1, (px, py, pc), 4 * px + 2 * py + pc


def _remote_copies(srcs, lands, send_sems, recv_sems, gather):
    x, y, c = lax.axis_index("x"), lax.axis_index("y"), lax.axis_index("c")
    me = 4 * x + 2 * y + c
    copies = []
    for k, peer, pidx in _peers(x, y, c):
        for a, (src, land) in enumerate(zip(srcs, lands)):
            copies.append(pltpu.make_async_remote_copy(
                src_ref=src if gather else src.at[pidx], dst_ref=land.at[me],
                send_sem=send_sems.at[a * (N_DEV - 1) + k], recv_sem=recv_sems.at[a * (N_DEV - 1) + k],
                device_id=peer, device_id_type=pl.DeviceIdType.MESH))
    return copies


_HBM = pl.BlockSpec(memory_space=pltpu.HBM)
_SEM = pl.BlockSpec(memory_space=pltpu.SEMAPHORE)
_EFFECT = pltpu.SideEffectType.DATAFLOW_SIDE_EFFECTING


def _own_slabs(name, arrays, gather, dep):
    n, nd = len(arrays), len(dep)
    me = (4 * lax.axis_index("x") + 2 * lax.axis_index("y") + lax.axis_index("c")).astype(jnp.int32).reshape(1)

    def body(me_ref, *refs):
        for a in range(n):
            refs[n + nd + a][...] = refs[a][...]

    def slab(shape):
        return pl.BlockSpec((None,) + tuple(shape), lambda i, me_ref: (me_ref[0],) + (0,) * len(shape))

    def whole(shape):
        return pl.BlockSpec(tuple(shape), lambda i, me_ref: (0,) * len(shape))

    outs = [_sds(((N_DEV,) + a.shape) if gather else a.shape, a.dtype) for a in arrays]
    grid_spec = pltpu.PrefetchScalarGridSpec(
        num_scalar_prefetch=1, grid=(1,),
        in_specs=[whole(a.shape) if gather else slab(a.shape[1:]) for a in arrays] + [_ANY] * nd,
        out_specs=[slab(o.shape[1:]) for o in outs])
    return pl.pallas_call(body, name=name, grid_spec=grid_spec, out_shape=outs, compiler_params=_params())(me, *arrays, *dep)


def _exchange_start(name, arrays, lands, gather):
    n = len(arrays)

    def body(*refs):
        srcs, lnds, send_sems, recv_sems, token = refs[:n], refs[n:2 * n], refs[2 * n], refs[2 * n + 1], refs[-1]
        for cp in _remote_copies(srcs, lnds, send_sems, recv_sems, gather):
            cp.start()
        token[...] = jnp.zeros(token.shape, F32)

    sems = pltpu.SemaphoreType.DMA((n * (N_DEV - 1),))
    thru = [pltpu.HBM(a.shape, a.dtype) for a in list(arrays) + list(lands)]
    res = pl.pallas_call(
        body, name=name, in_specs=[_HBM] * (2 * n),
        out_specs=[_SEM, _SEM] + [_HBM] * (2 * n) + [pl.BlockSpec(memory_space=pltpu.VMEM)],
        out_shape=[sems, sems] + thru + [_sds((8, 128), F32)],
        input_output_aliases={i: 2 + i for i in range(2 * n)},
        compiler_params=pltpu.CompilerParams(has_side_effects=_EFFECT),
    )(*[pltpu.with_memory_space_constraint(a, pltpu.HBM) for a in list(arrays) + list(lands)])
    return res[0], res[1], list(res[2:2 + n]), list(res[2 + n:2 + 2 * n]), res[-1]


def _exchange_wait(name, send_sems, recv_sems, arrays, lands, after, gather):
    n = len(arrays)
    n_after = len(after)

    def body(*refs):
        srcs, lnds = refs[:n], refs[n:2 * n]
        copies = _remote_copies(srcs, lnds, refs[2 * n], refs[2 * n + 1], gather)
        for cp in copies:
            cp.wait_send()
        for cp in copies:
            cp.wait_recv()

    thru = [pltpu.HBM(a.shape, a.dtype) for a in list(arrays) + list(lands)]
    res = pl.pallas_call(
        body, name=name, in_specs=[_HBM] * (2 * n) + [_SEM, _SEM] + [pl.BlockSpec(memory_space=pl.ANY)] * n_after,
        out_specs=[_HBM] * (2 * n), out_shape=thru, input_output_aliases={i: i for i in range(2 * n)},
        compiler_params=pltpu.CompilerParams(has_side_effects=_EFFECT),
    )(*arrays, *lands, send_sems, recv_sems, *after)
    return list(res[n:])


def _adamw_math(g, w, m, v):
    m = ADAM_B1 * m + (1.0 - ADAM_B1) * g
    v = ADAM_B2 * v + (1.0 - ADAM_B2) * (g * g)
    m_hat = m / (1.0 - ADAM_B1 ** ADAM_STEP)
    v_hat = v / (1.0 - ADAM_B2 ** ADAM_STEP)
    delta = -ADAM_LR * (m_hat / (jnp.sqrt(v_hat) + ADAM_EPS) + ADAM_WD * w)
    return delta, m, v


def _sum_adamw(name, recv, row_off, w, m, v, tr, layer=0):
    width = recv.shape[-1]
    w2, m2, v2 = (a.reshape(a.shape[0], -1, width) for a in (w, m, v))
    rows = w2.shape[1]
    base = row_off // tr

    def body(r_ref, w_ref, m_ref, v_ref, g_ref, d_ref, mo_ref, vo_ref):
        g = r_ref[0].astype(F32)
        for src in range(1, N_DEV):
            g = g + r_ref[src].astype(F32)
        delta, mn, vn = _adamw_math(g, w_ref[...], m_ref[...], v_ref[...])
        g_ref[...] = g
        d_ref[...] = delta
        mo_ref[...] = mn
        vo_ref[...] = vn

    blk = pl.BlockSpec((tr, width), lambda i: (i, 0))
    wblk = pl.BlockSpec((None, tr, width), lambda i: (layer, i, 0))
    return pl.pallas_call(
        body, name=name, grid=(rows // tr,),
        in_specs=[pl.BlockSpec((N_DEV, tr, width), lambda i: (0, base + i, 0)), wblk, wblk, wblk],
        out_specs=[blk] * 4, out_shape=[_sds((rows, width), F32)] * 4, compiler_params=_params())(recv, w2, m2, v2)


_WEIGHTS = ("pool_norm", "pool_w_in", "pool_w_grp", "pool_scale", "pool_w_out", "conv_norm", "conv_w_in", "conv_w", "conv_w_out",
            "mla_norm", "mla_w_in", "mla_q_norm", "mla_w_q_up", "mla_kv_norm", "mla_w_kv_up", "mla_w_out", "final_norm")


def _step(x, positions, loss_target, p, m, v):
    gathers, tokens, dep = [], [], []
    for group, arrays in enumerate(_pack_groups(p)):
        lands = _own_slabs(f"gather{group}_own", arrays, True, dep)
        ssem, rsem, arrays, lands, token = _exchange_start(f"gather{group}_start", arrays, lands, True)
        gathers.append((ssem, rsem, arrays, lands))
        tokens.append(token)
        dep = [token]
    state = {}

    def wait_group(group, after):
        return _exchange_wait(f"gather{group}_wait", *gathers[group], after, True)

    def get_w(layer, after):
        if layer == 0:
            bufs = wait_group(0, list(tokens))
            state["small"] = _small_views(bufs[1])
            rest = lambda later: dict(zip(("g_grp", "g_out"), wait_group(1, later)))
        elif layer == 1:
            bufs = wait_group(2, after)
            rest = lambda later: dict(g_out=wait_group(3, later)[0])
        elif layer == 2:
            bufs = wait_group(4, after)
        else:
            bufs = wait_group(5, after)
            rest = lambda later: dict(g_grp=bufs[1], g_out=bufs[2])
        w = _layer_weights(layer, bufs, state["small"], p["conv_norm"])
        if layer != 2:
            w["rest"] = rest
        return w

    scatters, small_grads = [], {}

    def put_g(layer, g):
        if layer == 4:
            small_grads[0] = g
            keys, arrays = ("small",), [_pack_small_grads(small_grads)]
        else:
            small_grads[layer] = g
            keys, arrays = _grad_group(layer, g)
        n = len(scatters)
        lands = _own_slabs(f"scatter{n}_own", arrays, False, [])
        ssem, rsem, arrays, lands, token = _exchange_start(f"scatter{n}_start", arrays, lands, False)
        scatters.append((layer, keys, (ssem, rsem, arrays, lands)))
        tokens.append(token)
        return [token]

    loss, grad_x = _local_step(x[0], positions[0], loss_target[0], p["final_norm"], get_w, put_g)

    res, after = {}, [tokens[-1]]
    for n, (layer, keys, handles) in enumerate(scatters):
        recv = _exchange_wait(f"scatter{n}_wait", *handles, after, False)
        if layer == 4:
            break
        l = 1 if layer == 3 else 0
        for key, buf in zip(keys, recv):
            name = _GRAD_PARAM[layer][key]
            tr = min(256, buf.shape[1]) if name != "mla_w_q_up" else buf.shape[1]
            res[name, l] = _sum_adamw(f"adam_{name}{l}", buf, 0, p[name], m[name], v[name], tr, l)
        after = [res[name, l][1]]
    small = _sum_adamw("adam_small", recv[0], 0, _pack_small(p, SMALL_ROWS_RS, True)[None], _pack_small(m, SMALL_ROWS_RS, True)[None],
                       _pack_small(v, SMALL_ROWS_RS, True)[None], SMALL_ROWS_RS)
    small = [_unpack_small(a, True) for a in small]
    final = {k: tuple(part[k] for part in small) for k in _SMALL_SHARDED + _SMALL_REPLICATED}
    for k in _WEIGHTS:
        if k not in final:
            layers = [res[k, l] for l in range(p[k].shape[0])]
            final[k] = tuple(jnp.stack([lay[part] for lay in layers]).reshape(p[k].shape) for part in range(4))
    res = final

    loss = lax.psum(loss, ("x", "y", "c"))
    out = [loss, grad_x[None]]
    for part in range(4):
        out += [res[k][part] for k in _WEIGHTS]
    return tuple(out)


def kernel(x, positions, pool_norm, pool_w_in, pool_w_grp, pool_scale, pool_w_out, conv_norm, conv_w_in, conv_w, conv_w_out, mla_norm, mla_w_in, mla_q_norm, mla_w_q_up, mla_kv_norm, mla_w_kv_up, mla_w_out, final_norm, loss_target, m_pool_norm, m_pool_w_in, m_pool_w_grp, m_pool_scale, m_pool_w_out, m_conv_norm, m_conv_w_in, m_conv_w, m_conv_w_out, m_mla_norm, m_mla_w_in, m_mla_q_norm, m_mla_w_q_up, m_mla_kv_norm, m_mla_w_kv_up, m_mla_w_out, m_final_norm, v_pool_norm, v_pool_w_in, v_pool_w_grp, v_pool_scale, v_pool_w_out, v_conv_norm, v_conv_w_in, v_conv_w, v_conv_w_out, v_mla_norm, v_mla_w_in, v_mla_q_norm, v_mla_w_q_up, v_mla_kv_norm, v_mla_w_kv_up, v_mla_w_out, v_final_norm):
    p = dict(pool_norm=pool_norm, pool_w_in=pool_w_in, pool_w_grp=pool_w_grp, pool_scale=pool_scale, pool_w_out=pool_w_out,
             conv_norm=conv_norm, conv_w_in=conv_w_in, conv_w=conv_w, conv_w_out=conv_w_out, mla_norm=mla_norm, mla_w_in=mla_w_in,
             mla_q_norm=mla_q_norm, mla_w_q_up=mla_w_q_up, mla_kv_norm=mla_kv_norm, mla_w_kv_up=mla_w_kv_up, mla_w_out=mla_w_out,
             final_norm=final_norm)
    m = dict(pool_norm=m_pool_norm, pool_w_in=m_pool_w_in, pool_w_grp=m_pool_w_grp, pool_scale=m_pool_scale, pool_w_out=m_pool_w_out,
             conv_norm=m_conv_norm, conv_w_in=m_conv_w_in, conv_w=m_conv_w, conv_w_out=m_conv_w_out, mla_norm=m_mla_norm,
             mla_w_in=m_mla_w_in, mla_q_norm=m_mla_q_norm, mla_w_q_up=m_mla_w_q_up, mla_kv_norm=m_mla_kv_norm,
             mla_w_kv_up=m_mla_w_kv_up, mla_w_out=m_mla_w_out, final_norm=m_final_norm)
    v = dict(pool_norm=v_pool_norm, pool_w_in=v_pool_w_in, pool_w_grp=v_pool_w_grp, pool_scale=v_pool_scale, pool_w_out=v_pool_w_out,
             conv_norm=v_conv_norm, conv_w_in=v_conv_w_in, conv_w=v_conv_w, conv_w_out=v_conv_w_out, mla_norm=v_mla_norm,
             mla_w_in=v_mla_w_in, mla_q_norm=v_mla_q_norm, mla_w_q_up=v_mla_w_q_up, mla_kv_norm=v_mla_kv_norm,
             mla_w_kv_up=v_mla_w_kv_up, mla_w_out=v_mla_w_out, final_norm=v_final_norm)
    return _step(x, positions, loss_target, p, m, v)
```

```python
import functools

import jax
import jax.numpy as jnp
from jax import lax
from jax.experimental import pallas as pl
from jax.experimental.pallas import tpu as pltpu

BF = jnp.bfloat16
F32 = jnp.float32

N_DEV = 8
D_MODEL = 1024
D_INNER = 2048
POOL_WINDOWS = (2, 4, 8, 16)
POOL_GROUP = 512
N_HEADS = 16
QK_NOPE = 128
QK_ROPE = 64
QK_DIM = QK_NOPE + QK_ROPE
V_DIM = 128
Q_RANK = 384
KV_RANK = 256
ATTN_SCALE = QK_DIM ** -0.5
LOG2_E = 1.4426950408889634
LN_2 = 0.6931471805599453
Q_PRESCALE = ATTN_SCALE * LOG2_E
ATTN_TILE = 512
ATTN_HEADS_PER_STEP = 2
ROPE_BASE = 10000.0
NORM_EPS = 1e-6
NEG_BIG = -1e30

ADAM_LR = 0.001
ADAM_B1 = 0.9
ADAM_B2 = 0.999
ADAM_EPS = 1e-08
ADAM_WD = 0.01
ADAM_STEP = 10

VMEM_LIMIT_BYTES = 52 * 1024 * 1024
IN_PROJ_ROWS = 1024
POOL_HALO = 32
CONV_HALO = 16

NN = (((1,), (0,)), ((), ()))
NT = (((1,), (1,)), ((), ()))
TN = (((0,), (0,)), ((), ()))

SMALL_ROWS_AG = 16
SMALL_ROWS_RS = 32


def _sds(shape, dtype):
    return jax.ShapeDtypeStruct(tuple(shape), dtype)


def _params():
    return pltpu.CompilerParams(vmem_limit_bytes=VMEM_LIMIT_BYTES)


_ANY = pl.BlockSpec(memory_space=pl.ANY)


def _dot(a, b, dims):
    return lax.dot_general(a, b, dims, preferred_element_type=F32)


def _sig(z):
    return 1.0 / (1.0 + jnp.exp(-z))


def _silu_and_grad(z):
    sig = _sig(z)
    return z * sig, sig * (1.0 + z * (1.0 - sig))


def _to_row(col):
    return jnp.broadcast_to(col, (col.shape[0], 128)).T[0:1, :]


def _rope_swap(x, p):
    pb = p.astype(BF)
    hi = x.astype(BF)
    r1 = x - hi.astype(F32)
    mid = r1.astype(BF)
    lo = (r1 - mid.astype(F32)).astype(BF)
    return (_dot(hi, pb, NN) + _dot(mid, pb, NN)) + _dot(lo, pb, NN)


def _rope_fwd(x, cosf, sinf, p):
    return x * cosf + _rope_swap(x, p) * sinf


def _rope_bwd(dy, cosf, sinf, p):
    return dy * cosf + _rope_swap(dy * sinf, p)


def _rms_bwd(dxn, x, g, res):
    r = lax.rsqrt(jnp.mean(x * x, axis=-1, keepdims=True) + NORM_EPS)
    v = dxn * g
    dx = r * v - x * ((r * r * r) * jnp.mean(v * x, axis=-1, keepdims=True))
    if res is not None:
        dx = dx + res
    dg = jnp.sum(dxn * (x * r), axis=0, keepdims=True)
    return dx, dg


def _accumulate(ref, val, step):
    @pl.when(step == 0)
    def _():
        ref[...] = val

    @pl.when(step > 0)
    def _():
        ref[...] += val


def _mm(name, grid, ins, in_specs, outs, out_specs, dims, epi, red=None, acc_shape=None):
    n_in, n_out = len(ins), len(outs)
    n_red = None if red is None else grid[red]

    def body(*refs):
        in_refs, out_refs = refs[:n_in], refs[n_in:n_in + n_out]
        pids = tuple(pl.program_id(ax) for ax in range(len(grid)))
        a, b = in_refs[0][...], in_refs[1][...]
        if a.ndim == 3:
            a = a.reshape(-1, a.shape[-1])
        if b.ndim == 3:
            b = b.reshape(-1, b.shape[-1])
        part = _dot(a.astype(BF), b.astype(BF), dims)
        if red is None:
            epi(part, in_refs[2:], out_refs, pids)
        else:
            acc = refs[n_in + n_out]
            k = pids[red]
            _accumulate(acc, part, k)

            @pl.when(k == n_red - 1)
            def _():
                epi(acc[...], in_refs[2:], out_refs, pids)

    scratch = [] if red is None else [pltpu.VMEM(acc_shape, F32)]
    return pl.pallas_call(body, name=name, grid=grid, in_specs=in_specs, out_specs=out_specs, out_shape=outs,
                          scratch_shapes=scratch, compiler_params=_params())(*ins)


def _store(part, extra, outs, pids):
    outs[0][...] = part.astype(outs[0].dtype)


def _rms_fwd(name, x, g, tm):
    s, d = x.shape

    def body(x_ref, g_ref, o_ref):
        xv = x_ref[...]
        r = lax.rsqrt(jnp.mean(xv * xv, axis=-1, keepdims=True) + NORM_EPS)
        o_ref[...] = ((xv * r) * g_ref[...]).astype(BF)

    return pl.pallas_call(body, name=name, grid=(s // tm,),
                          in_specs=[pl.BlockSpec((tm, d), lambda i: (i, 0)), pl.BlockSpec((1, d), lambda i: (0, 0))],
                          out_specs=pl.BlockSpec((tm, d), lambda i: (i, 0)), out_shape=_sds((s, d), BF),
                          compiler_params=_params())(x, g.reshape(1, d))


def _norm_in_proj(name, x, g, wbuf, w_index, n_j, tm):
    s = x.shape[0]
    ti = IN_PROJ_ROWS if s % IN_PROJ_ROWS == 0 else tm

    def body(x_ref, g_ref, w_ref, h_ref, xn_ref):
        @pl.when(pl.program_id(1) == 0)
        def _():
            xv = x_ref[...]
            r = lax.rsqrt(jnp.mean(xv * xv, axis=-1, keepdims=True) + NORM_EPS)
            xn_ref[...] = ((xv * r) * g_ref[...]).astype(BF)

        h_ref[...] = _dot(xn_ref[...], w_ref[...], NN).astype(BF)

    row = lambda i, j: (i, 0)
    return pl.pallas_call(
        body, name=name, grid=(s // ti, n_j),
        in_specs=[pl.BlockSpec((ti, D_MODEL), row), pl.BlockSpec((1, D_MODEL), lambda i, j: (0, 0)),
                  pl.BlockSpec((None, D_MODEL, 512), w_index)],
        out_specs=[pl.BlockSpec((ti, 512), lambda i, j: (i, j)), pl.BlockSpec((ti, D_MODEL), row)],
        out_shape=[_sds((s, n_j * 512), BF), _sds((s, D_MODEL), BF)], compiler_params=_params())(x, g.reshape(1, D_MODEL), wbuf)


def _tn(name, a, b, out_shape, out_block, out_index, a_cols, b_cols, grid, a_index, b_index, dep=()):
    s = a.shape[-2]
    a_block = (s, a_cols) if a.ndim == 2 else (None, s, a_cols)
    b_block = (s, b_cols) if b.ndim == 2 else (None, s, b_cols)

    def epi(part, extra, outs, pids):
        outs[0][...] = part.astype(BF).reshape(outs[0].shape)

    return _mm(name, grid, [a, b] + list(dep), [pl.BlockSpec(a_block, a_index), pl.BlockSpec(b_block, b_index)] + [_ANY] * len(dep),
               [_sds(out_shape, BF)], [pl.BlockSpec(out_block, out_index)], TN, epi)[0]


def _pool_window_fwd(name, h, tm):
    s = h.shape[0]
    hb = POOL_HALO

    def body(u_ref, halo_ref, o_ref, e_ref, a_ref, b_ref):
        i = pl.program_id(0)
        row = lax.broadcasted_iota(jnp.int32, (tm, 1), 0) + i * tm
        for g, w in enumerate(POOL_WINDOWS):
            cs = slice(g * POOL_GROUP, (g + 1) * POOL_GROUP)
            e_ref[0:hb, :] = jnp.where(i > 0, halo_ref[:, cs].astype(F32), 0.0)
            e_ref[hb:, :] = u_ref[:, cs].astype(F32)
            src, bufs = e_ref, (a_ref, b_ref)
            for lv in range(1, w.bit_length()):
                dst, st, sh = bufs[(lv - 1) % 2], 8 * lv, 2 ** (lv - 1)
                n = hb + tm - st
                dst[st:, :] = src[st:, :] + src[pl.ds(st - sh, n), :]
                src = dst
            cnt = jnp.minimum(row + 1, w).astype(F32)
            o_ref[:, cs] = (src[hb:, :] / cnt - u_ref[:, cs].astype(F32)).astype(BF)

    per = tm // hb
    return pl.pallas_call(
        body, name=name, grid=(s // tm,),
        in_specs=[pl.BlockSpec((tm, D_INNER), lambda i: (i, 0)),
                  pl.BlockSpec((hb, D_INNER), lambda i: (jnp.maximum(i * per - 1, 0), 0))],
        out_specs=pl.BlockSpec((tm, D_INNER), lambda i: (i, 0)), out_shape=_sds((s, D_INNER), BF),
        scratch_shapes=[pltpu.VMEM((hb + tm, POOL_GROUP), F32)] * 3, compiler_params=_params())(h, h)


def _pool_window_bwd(name, dp, tm, dh):
    s = dp.shape[0]
    nt = s // tm
    hb = POOL_HALO

    def body(d_ref, halo_ref, dh_in_ref, o_ref, e_ref, a_ref, b_ref):
        i = pl.program_id(0)
        row = lax.broadcasted_iota(jnp.int32, (tm, 1), 0) + i * tm
        hrow = lax.broadcasted_iota(jnp.int32, (hb, 1), 0) + (i + 1) * tm
        for g, w in enumerate(POOL_WINDOWS):
            cs = slice(g * POOL_GROUP, (g + 1) * POOL_GROUP)
            e_ref[0:tm, :] = d_ref[:, cs] / jnp.minimum(row + 1, w).astype(F32)
            e_ref[tm:, :] = jnp.where(i < nt - 1, halo_ref[:, cs] / jnp.minimum(hrow + 1, w).astype(F32), 0.0)
            src, bufs = e_ref, (a_ref, b_ref)
            for lv in range(1, w.bit_length()):
                dst, sh = bufs[(lv - 1) % 2], 2 ** (lv - 1)
                n = tm + hb - 8 * lv
                dst[0:n, :] = src[0:n, :] + src[pl.ds(sh, n), :]
                src = dst
            o_ref[:, cs] = (src[0:tm, :] - d_ref[:, cs]).astype(BF)

    per = tm // hb
    last = s // hb - 1
    return pl.pallas_call(
        body, name=name, grid=(nt,),
        in_specs=[pl.BlockSpec((tm, D_INNER), lambda i: (i, 0)),
                  pl.BlockSpec((hb, D_INNER), lambda i: (jnp.minimum((i + 1) * per, last), 0)), _ANY],
        out_specs=pl.BlockSpec((tm, D_INNER), lambda i: (i, 0)), out_shape=_sds(dh.shape, BF),
        input_output_aliases={2: 0},
        scratch_shapes=[pltpu.VMEM((hb + tm, POOL_GROUP), F32)] * 3, compiler_params=_params())(dp, dp, dh)


def _grp_block():
    return pl.BlockSpec((N_DEV, 64, POOL_GROUP), lambda i, g: (0, g, 0))


def _pool_fwd(x, l, w, tm):
    s = x.shape[0]
    nt = s // tm
    n = f"pool{l}"
    h, xn = _norm_in_proj(n + "_in", x, w["norm"], w["g_in"], lambda i, j: (j, 0, 0), 8, tm)
    pooled = _pool_window_fwd(n + "_win", h, tm)
    w = dict(w, **w["rest"]([h]))

    def gate(part, extra, outs, pids):
        z = extra[0][...].astype(F32)
        outs[0][...] = ((part * extra[1][...]) * (z * _sig(z))).astype(BF)

    (gated,) = _mm(n + "_grp", (nt, 4), [pooled, w["g_grp"], h, w["scale"].reshape(1, D_INNER)],
                   [pl.BlockSpec((tm, 512), lambda i, g: (i, g)), _grp_block(),
                    pl.BlockSpec((tm, 512), lambda i, g: (i, 4 + g)), pl.BlockSpec((1, 512), lambda i, g: (0, g))],
                   [_sds((s, D_INNER), BF)], [pl.BlockSpec((tm, 512), lambda i, g: (i, g))], NN, gate)
    y = _out_proj(n + "_out", gated, w["g_out"], 0, x, tm)
    return y, dict(x=x, xn=xn, h=h, pooled=pooled, gated=gated, w=w)


def _out_proj(name, gated, g1024, row_block, x, tm):
    s = x.shape[0]

    def epi(part, extra, outs, pids):
        outs[0][...] = part + extra[0][...]

    row = pl.BlockSpec((tm, D_MODEL), lambda i: (i, 0))
    return _mm(name, (s // tm,), [gated, g1024, x],
               [pl.BlockSpec((tm, D_INNER), lambda i: (i, 0)), pl.BlockSpec((N_DEV, 256, D_MODEL), lambda i: (0, row_block, 0)), row],
               [_sds((s, D_MODEL), F32)], [row], NN, epi)[0]


def _w_out_nt_block(row_block):
    return pl.BlockSpec((2, 256, D_MODEL), lambda j, i: (j, row_block, 0))


def _in_proj_bwd(name, dh, wbuf, w_index, n_k, x, g, dy, tm, dep=()):
    s = x.shape[0]
    tm = IN_PROJ_ROWS if s % IN_PROJ_ROWS == 0 else tm

    def epi(acc, extra, outs, pids):
        dx, dg = _rms_bwd(acc, extra[0][...], extra[1][...], extra[2][...])
        outs[0][...] = dx
        outs[1][...] = dx.astype(BF)
        _accumulate(outs[2], dg, pids[0])

    row = lambda i, k: (i, 0)
    return _mm(name, (s // tm, n_k), [dh, wbuf, x, g.reshape(1, D_MODEL), dy] + list(dep),
               [pl.BlockSpec((tm, 512), lambda i, k: (i, k)), pl.BlockSpec((None, D_MODEL, 512), w_index),
                pl.BlockSpec((tm, D_MODEL), row), pl.BlockSpec((1, D_MODEL), lambda i, k: (0, 0)), pl.BlockSpec((tm, D_MODEL), row)]
               + [_ANY] * len(dep),
               [_sds((s, D_MODEL), F32), _sds((s, D_MODEL), BF), _sds((1, D_MODEL), F32)],
               [pl.BlockSpec((tm, D_MODEL), row), pl.BlockSpec((tm, D_MODEL), row), pl.BlockSpec((1, D_MODEL), lambda i, k: (0, 0))],
               NT, epi, red=1, acc_shape=(tm, D_MODEL))


def _w_out_grad(name, gated, dyb):
    s = gated.shape[0]
    return _tn(name, gated, dyb, (D_INNER, D_MODEL), (512, D_MODEL), lambda i: (i, 0), 512, D_MODEL, (4,),
               lambda i: (0, i), lambda i: (0, 0))


def _pool_bwd(dy, dyb, l, w, sv, tm, dep, early=None):
    s = dy.shape[0]
    nt = s // tm
    n = f"pool{l}b"
    h, pooled = sv["h"], sv["pooled"]
    scale = w["scale"].reshape(1, D_INNER)

    def gate_bwd(part, extra, outs, pids):
        z, sc = extra[0][...].astype(F32), extra[1][...]
        wg = extra[3][...].reshape(POOL_GROUP, POOL_GROUP)
        mpv = _dot(extra[2][...], wg, NN)
        sz, dsz = _silu_and_grad(z)
        dm = part * sz
        dmp = (dm * sc).astype(BF)
        outs[0][...] = dmp
        outs[1][...] = (part * (mpv * sc) * dsz).astype(BF)
        _accumulate(outs[2], jnp.sum(dm * mpv, axis=0, keepdims=True), pids[1])
        outs[3][...] = _dot(dmp, wg, NT)

    tile = lambda j, i: (i, j)
    dmp, dz, dscale, dpool = _mm(
        n + "_out", (4, nt), [dyb, w["g_out"], h, scale, pooled, w["g_grp"]] + dep,
        [pl.BlockSpec((tm, D_MODEL), lambda j, i: (i, 0)), _w_out_nt_block(0),
         pl.BlockSpec((tm, 512), lambda j, i: (i, 4 + j)), pl.BlockSpec((1, 512), lambda j, i: (0, j)), pl.BlockSpec((tm, 512), tile),
         pl.BlockSpec((N_DEV, 64, POOL_GROUP), lambda j, i: (0, j, 0))] + [_ANY] * len(dep),
        [_sds((s, D_INNER), BF), _sds((s, 2 * D_INNER), BF), _sds((1, D_INNER), F32), _sds((s, D_INNER), F32)],
        [pl.BlockSpec((tm, 512), tile), pl.BlockSpec((tm, 512), lambda j, i: (i, 4 + j)), pl.BlockSpec((1, 512), lambda j, i: (0, j)),
         pl.BlockSpec((tm, 512), tile)],
        NT, gate_bwd)
    g_out = _w_out_grad(n + "_gout", sv["gated"], dyb).reshape(N_DEV, 256, D_MODEL)
    g_grp = _tn(n + "_ggrp", pooled, dmp, (N_DEV, 256, 512), (N_DEV, 64, 512), lambda g: (0, g, 0),
                512, 512, (4,), lambda g: (0, g), lambda g: (0, g))
    dep = early(dict(g_grp=g_grp, g_out=g_out)) if early is not None else ()
    dh = _pool_window_bwd(n + "_win", dpool, tm, dz)
    g_in = _tn(n + "_gin", sv["xn"], dh, (N_DEV, D_MODEL, 512), (None, D_MODEL, 512), lambda j: (j, 0, 0),
               D_MODEL, 512, (8,), lambda j: (0, 0), lambda j: (0, j), dep)
    dep = early(dict(g_in=g_in)) if early is not None else ()
    dx, dxb, dnorm = _in_proj_bwd(n + "_in", dh, w["g_in"], lambda i, k: (k, 0, 0), 8, sv["x"], w["norm"], dy, tm, dep)
    return dx, dxb, dict(g_in=g_in, g_grp=g_grp, g_out=g_out, norm=dnorm[0], scale=dscale[0])


def _conv_in_index(i, j):
    return (j // 2, 0, j % 2)


def _conv_fwd(x, w, tm):
    s = x.shape[0]
    nt = s // tm
    h, xn = _norm_in_proj("conv_in", x, w["norm"], w["g_in"], _conv_in_index, 16, tm)
    per = tm // CONV_HALO

    def body(b_ref, c_ref, h_ref, z_ref, cp_ref, hp_ref, w_ref, o_ref, e_ref):
        i = pl.program_id(0)
        ch = c_ref[...].astype(F32) * h_ref[...].astype(F32)
        e_ref[0:CONV_HALO, :] = jnp.where(i > 0, cp_ref[...].astype(F32) * hp_ref[...].astype(F32), 0.0)
        e_ref[CONV_HALO:, :] = ch
        co = (w_ref[2:3, :] * ch + w_ref[1:2, :] * e_ref[pl.ds(CONV_HALO - 1, tm), :]
              + w_ref[0:1, :] * e_ref[pl.ds(CONV_HALO - 2, tm), :])
        z = z_ref[...].astype(F32)
        o_ref[...] = ((b_ref[...].astype(F32) * co) * (z * _sig(z))).astype(BF)

    def col(q):
        return pl.BlockSpec((tm, 512), lambda i, j: (i, 4 * q + j))

    def prev(q):
        return pl.BlockSpec((CONV_HALO, 512), lambda i, j: (jnp.maximum(i * per - 1, 0), 4 * q + j))

    gated = pl.pallas_call(
        body, name="conv_mix", grid=(nt, 4),
        in_specs=[col(0), col(1), col(2), col(3), prev(1), prev(2), pl.BlockSpec((3, 512), lambda i, j: (0, j))],
        out_specs=pl.BlockSpec((tm, 512), lambda i, j: (i, j)), out_shape=_sds((s, D_INNER), BF),
        scratch_shapes=[pltpu.VMEM((CONV_HALO + tm, 512), F32)], compiler_params=_params())(h, h, h, h, h, h, w["conv_w"])
    w = dict(w, **w["rest"]([gated]))
    y = _out_proj("conv_out", gated, w["g_out"], 0, x, tm)
    return y, dict(x=x, xn=xn, h=h, gated=gated, w=w)


def _conv_bwd(dy, dyb, w, sv, tm, dep, early):
    s = dy.shape[0]
    nt = s // tm
    h = sv["h"]
    per = tm // CONV_HALO
    last = s // CONV_HALO - 1
    n_dep = len(dep)

    def body(dy_ref, dyn_ref, wo_ref, b_ref, c_ref, h_ref, z_ref, cp_ref, hp_ref, bn_ref, zn_ref, w_ref, *rest):
        dall_ref, dw_ref, e_ref, f_ref = rest[n_dep:]
        db_ref, dc_ref, dh_ref, dz_ref = (dall_ref.at[:, q * 512:(q + 1) * 512] for q in range(4))
        i = pl.program_id(1)
        wo = wo_ref[...].reshape(512, D_MODEL)
        dg_tile = _dot(dy_ref[...], wo, NT)
        dg_next = _dot(dyn_ref[...], wo, NT)
        w0, w1, w2 = w_ref[0:1, :], w_ref[1:2, :], w_ref[2:3, :]
        c, hh, b = c_ref[...].astype(F32), h_ref[...].astype(F32), b_ref[...].astype(F32)
        ch = c * hh
        e_ref[0:CONV_HALO, :] = jnp.where(i > 0, cp_ref[...].astype(F32) * hp_ref[...].astype(F32), 0.0)
        e_ref[CONV_HALO:, :] = ch
        ch1 = e_ref[pl.ds(CONV_HALO - 1, tm), :]
        ch2 = e_ref[pl.ds(CONV_HALO - 2, tm), :]
        co = w2 * ch + w1 * ch1 + w0 * ch2
        sz, dsz = _silu_and_grad(z_ref[...].astype(F32))
        dgv = dg_tile
        dyv = dgv * sz
        dz_ref[...] = (dgv * (b * co) * dsz).astype(BF)
        db_ref[...] = (dyv * co).astype(BF)
        dco = dyv * b
        zn = zn_ref[...].astype(F32)
        f_ref[0:tm, :] = dco
        f_ref[tm:, :] = jnp.where(i < nt - 1, dg_next * (zn * _sig(zn)) * bn_ref[...].astype(F32), 0.0)
        dch = w2 * dco + w1 * f_ref[pl.ds(1, tm), :] + w0 * f_ref[pl.ds(2, tm), :]
        dc_ref[...] = (dch * hh).astype(BF)
        dh_ref[...] = (dch * c).astype(BF)
        for tap, shifted in enumerate((ch2, ch1, ch)):
            _accumulate(dw_ref.at[tap:tap + 1, :], jnp.sum(dco * shifted, axis=0, keepdims=True), i)

    def col(q):
        return pl.BlockSpec((tm, 512), lambda j, i: (i, 4 * q + j))

    def prev(q):
        return pl.BlockSpec((CONV_HALO, 512), lambda j, i: (jnp.maximum(i * per - 1, 0), 4 * q + j))

    def nxt(q):
        return pl.BlockSpec((CONV_HALO, 512), lambda j, i: (jnp.minimum((i + 1) * per, last), 4 * q + j))

    wspec = pl.BlockSpec((3, 512), lambda j, i: (0, j))
    dy_tile = pl.BlockSpec((tm, D_MODEL), lambda j, i: (i, 0))
    dy_next = pl.BlockSpec((CONV_HALO, D_MODEL), lambda j, i: (jnp.minimum((i + 1) * per, last), 0))
    dh, dw = pl.pallas_call(
        body, name="convb_mix", grid=(4, nt),
        in_specs=[dy_tile, dy_next, _w_out_nt_block(0), col(0), col(1), col(2), col(3), prev(1), prev(2), nxt(0), nxt(3), wspec]
        + [_ANY] * n_dep,
        out_specs=[pl.BlockSpec((tm, D_INNER), lambda j, i: (i, j)), wspec],
        out_shape=[_sds((s, 4 * D_INNER), BF), _sds((3, D_INNER), F32)],
        scratch_shapes=[pltpu.VMEM((CONV_HALO + tm, 512), F32)] * 2, compiler_params=_params(),
    )(dyb, dyb, w["g_out"], h, h, h, h, h, h, h, h, w["conv_w"], *dep)

    def w_block(kp):
        k = 4 * (kp % 4) + kp // 4
        return (k // 2, 0, k % 2)

    g_in = _tn("convb_gin", sv["xn"], dh, (N_DEV, D_MODEL, D_MODEL), (None, D_MODEL, 512), lambda j: (j // 2, 0, j % 2),
               D_MODEL, 512, (16,), lambda j: (0, 0), lambda j: (0, 4 * (j % 4) + j // 4))
    g_out = _w_out_grad("convb_gout", sv["gated"], dyb).reshape(N_DEV, 256, D_MODEL)
    dep = early(dict(g_in=g_in, g_out=g_out))
    dx, dxb, dnorm = _in_proj_bwd("convb_in", dh, w["g_in"], lambda i, kp: w_block(kp), 16, sv["x"], w["norm"], dy, tm, dep)
    return dx, dxb, dict(g_in=g_in, g_out=g_out, norm=dnorm[0], conv_w=dw)


def _attn_tiles(s):
    t = min(ATTN_TILE, s)
    return t, s // t


def _causal_keep(t, keys_on_rows):
    r = lax.broadcasted_iota(jnp.int32, (t, t), 0)
    c = lax.broadcasted_iota(jnp.int32, (t, t), 1)
    return (r <= c) if keys_on_rows else (c <= r)


def _mla_fwd(x, w, rope, tm):
    s = x.shape[0]
    nt = s // tm
    cosf, sinf, perm = rope
    xn = _rms_fwd("mla_rms", x, w["norm"], tm)

    def in_body(xn_ref, wq_ref, wkv_ref, wkr_ref, wz_ref, gq_ref, gkv_ref, cos_ref, sin_ref, p_ref,
                ql_ref, kvl_ref, qn_ref, kvn_ref, krr_ref, z_ref):
        xv = xn_ref[...]
        ql = _dot(xv, wq_ref[...], NN)
        kvl = _dot(xv, wkv_ref[...], NN)
        ql_ref[...] = ql
        kvl_ref[...] = kvl
        rq = lax.rsqrt(jnp.mean(ql * ql, axis=-1, keepdims=True) + NORM_EPS)
        qn_ref[...] = ((ql * rq) * gq_ref[...]).astype(BF)
        rkv = lax.rsqrt(jnp.mean(kvl * kvl, axis=-1, keepdims=True) + NORM_EPS)
        kvn_ref[...] = ((kvl * rkv) * gkv_ref[...]).astype(BF)
        kr = _dot(xv, wkr_ref[...], NN)
        krr_ref[...] = _rope_fwd(kr, cos_ref[...], sin_ref[...], p_ref[...]).astype(BF)
        z_ref[...] = _dot(xv, wz_ref[...], NN).astype(BF)

    def full(a):
        return pl.BlockSpec(a.shape, lambda i: (0,) * a.ndim)

    def rows(c):
        return pl.BlockSpec((tm, c), lambda i: (i, 0))

    gq, gkv = w["q_norm"].reshape(1, Q_RANK), w["kv_norm"].reshape(1, KV_RANK)
    q_lat, kv_lat, qn, kvn, krr, z = pl.pallas_call(
        in_body, name="mla_in", grid=(nt,),
        in_specs=[rows(D_MODEL), full(w["w_q"]), full(w["w_kv"]), full(w["w_kr"]), full(w["w_z"]), full(gq), full(gkv),
                  rows(QK_ROPE), rows(QK_ROPE), full(perm)],
        out_specs=[rows(Q_RANK), rows(KV_RANK), rows(Q_RANK), rows(KV_RANK), rows(QK_ROPE), rows(D_INNER)],
        out_shape=[_sds((s, Q_RANK), F32), _sds((s, KV_RANK), F32), _sds((s, Q_RANK), BF), _sds((s, KV_RANK), BF),
                   _sds((s, QK_ROPE), BF), _sds((s, D_INNER), BF)],
        compiler_params=_params())(xn, w["w_q"], w["w_kv"], w["w_kr"], w["w_z"], gq, gkv, cosf, sinf, perm)

    def q_epi(part, extra, outs, pids):
        outs[0][:, 0:QK_NOPE] = (part[:, 0:QK_NOPE] * Q_PRESCALE).astype(BF)
        roped = _rope_fwd(part[:, QK_NOPE:QK_DIM], extra[0][...], extra[1][...], extra[2][...])
        outs[0][:, QK_NOPE:QK_DIM] = (roped * Q_PRESCALE).astype(BF)

    tp = IN_PROJ_ROWS if s % IN_PROJ_ROWS == 0 else tm
    rope_row = pl.BlockSpec((tp, QK_ROPE), lambda h, i: (i, 0))
    (q,) = _mm("mla_qup", (N_HEADS, s // tp), [qn, w["w_qh"], cosf, sinf, perm],
               [pl.BlockSpec((tp, Q_RANK), lambda h, i: (i, 0)), pl.BlockSpec((None, Q_RANK, QK_DIM), lambda h, i: (h, 0, 0)),
                rope_row, rope_row, pl.BlockSpec((QK_ROPE, QK_ROPE), lambda h, i: (0, 0))],
               [_sds((N_HEADS, s, QK_DIM), BF)], [pl.BlockSpec((None, tp, QK_DIM), lambda h, i: (h, i, 0))], NN, q_epi)

    def kv_epi(part, extra, outs, pids):
        outs[0][:, 0:QK_NOPE] = part[:, 0:QK_NOPE].astype(BF)
        outs[0][:, QK_NOPE:QK_DIM] = extra[0][...]
        outs[1][...] = part[:, QK_NOPE:].astype(BF)

    k, v = _mm("mla_kvup", (N_HEADS, s // tp), [kvn, w["g512"], krr],
               [pl.BlockSpec((tp, KV_RANK), lambda h, i: (i, 0)),
                pl.BlockSpec((None, KV_RANK, 256), lambda h, i: (h // 2, 0, h % 2)), rope_row],
               [_sds((N_HEADS, s, QK_DIM), BF), _sds((N_HEADS, s, V_DIM), BF)],
               [pl.BlockSpec((None, tp, QK_DIM), lambda h, i: (h, i, 0)), pl.BlockSpec((None, tp, V_DIM), lambda h, i: (h, i, 0))],
               NN, kv_epi)

    t, nq = _attn_tiles(s)

    def attn_body(q_ref, k_ref, v_ref, z_ref, o_ref, g_ref, lse_ref):
        i = pl.program_id(1)

        def block(j, carry, masked):
            start = pl.multiple_of(j * t, t)
            out = []
            for hh, (m, lsum, acc) in enumerate(carry):
                sc = _dot(q_ref[hh], k_ref[hh, pl.ds(start, t), :], NT)
                if masked:
                    sc = jnp.where(_causal_keep(t, False), sc, NEG_BIG)
                mn = jnp.maximum(m, jnp.max(sc, axis=-1, keepdims=True))
                alpha = jnp.exp2(m - mn)
                p = jnp.exp2(sc - mn)
                lsum = alpha * lsum + jnp.sum(p, axis=-1, keepdims=True)
                acc = alpha * acc + _dot(p.astype(BF), v_ref[hh, pl.ds(start, t), :], NN)
                out.append((mn, lsum, acc))
            return tuple(out)

        init = ((jnp.full((t, 1), NEG_BIG, F32), jnp.zeros((t, 1), F32), jnp.zeros((t, V_DIM), F32)),) * ATTN_HEADS_PER_STEP
        carry = lax.fori_loop(0, i, lambda j, c: block(j, c, False), init)
        for hh, (m, lsum, acc) in enumerate(block(i, carry, True)):
            cols = slice(hh * V_DIM, (hh + 1) * V_DIM)
            o = acc / lsum
            z = z_ref[:, cols].astype(F32)
            o_ref[:, cols] = o
            g_ref[:, cols] = (o * (z * _sig(z))).astype(BF)
            lse_ref[hh] = _to_row(m + jnp.log(lsum) * LOG2_E)

    hp = ATTN_HEADS_PER_STEP
    head_col = pl.BlockSpec((t, hp * V_DIM), lambda h, i: (i, h))
    o, gated, lse = pl.pallas_call(
        attn_body, name="mla_attn", grid=(N_HEADS // hp, nq),
        in_specs=[pl.BlockSpec((hp, t, QK_DIM), lambda h, i: (h, i, 0)), pl.BlockSpec((hp, s, QK_DIM), lambda h, i: (h, 0, 0)),
                  pl.BlockSpec((hp, s, V_DIM), lambda h, i: (h, 0, 0)), head_col],
        out_specs=[head_col, head_col, pl.BlockSpec((hp, None, 1, t), lambda h, i: (h, i, 0, 0))],
        out_shape=[_sds((s, D_INNER), F32), _sds((s, D_INNER), BF), _sds((N_HEADS, nq, 1, t), F32)],
        compiler_params=_params())(q, k, v, z)
    y = _out_proj("mla_out", gated, w["g1024"], 0, x, tm)
    return y, dict(x=x, xn=xn, q_lat=q_lat, kv_lat=kv_lat, qn=qn, kvn=kvn, z=z, q=q, k=k, v=v, o=o, lse=lse, gated=gated)


def _mla_bwd(dy, dyb, w, sv, rope, tm, dep):
    s = dy.shape[0]
    nt = s // tm
    cosf, sinf, perm = rope
    t, nq = _attn_tiles(s)
    assert t == tm, "the row statistics of the backward are laid out per attention tile"
    q, k, v, lse = sv["q"], sv["k"], sv["v"], sv["lse"]

    def gate_bwd(part, extra, outs, pids):
        z, o = extra[0][...].astype(F32), extra[1][...]
        sz, dsz = _silu_and_grad(z)
        do = part * sz
        outs[0][...] = do.astype(BF)
        outs[1][...] = (part * o * dsz).astype(BF)
        prod = do * o
        for hh in range(4):
            outs[2][hh] = _to_row(jnp.sum(prod[:, hh * V_DIM:(hh + 1) * V_DIM], axis=-1, keepdims=True))

    tile = lambda j, i: (i, j)
    dob, dz, delta = _mm(
        "mlab_out", (4, nt), [dyb, w["g1024"], sv["z"], sv["o"]] + dep,
        [pl.BlockSpec((tm, D_MODEL), lambda j, i: (i, 0)), _w_out_nt_block(0),
         pl.BlockSpec((tm, 512), tile), pl.BlockSpec((tm, 512), tile)] + [_ANY] * len(dep),
        [_sds((s, D_INNER), BF), _sds((s, D_INNER), BF), _sds((N_HEADS, nt, 1, tm), F32)],
        [pl.BlockSpec((tm, 512), tile), pl.BlockSpec((tm, 512), tile), pl.BlockSpec((4, None, 1, tm), lambda j, i: (j, i, 0, 0))],
        NT, gate_bwd)

    hp = ATTN_HEADS_PER_STEP

    def attn_bwd_body(k_ref, v_ref, q_ref, do_ref, lse_ref, dl_ref, cos_ref, sin_ref, p_ref, dkv_ref, dkr_ref, dq_ref, dq_acc):
        j = pl.program_id(1)

        @pl.when(j == 0)
        def _():
            dq_acc[...] = jnp.zeros(dq_acc.shape, F32)

        def block(i, carry, masked):
            rows = pl.ds(pl.multiple_of(i * t, t), t)
            out = []
            for hh, (dk, dv) in enumerate(carry):
                kb, vb = k_ref[hh], v_ref[hh]
                qb, dob_ = q_ref[hh, rows, :], do_ref[rows, hh * V_DIM:(hh + 1) * V_DIM]
                st = _dot(kb, qb, NT)
                if masked:
                    st = jnp.where(_causal_keep(t, True), st, NEG_BIG)
                pt = jnp.exp2(st - lse_ref[hh, i])
                dv = dv + _dot(pt.astype(BF), dob_, NN)
                dst = (pt * (_dot(vb, dob_, NT) - dl_ref[hh, i])).astype(BF)
                dk = dk + _dot(dst, qb, NN)
                dq_acc[hh, rows, :] += _dot(dst, kb, TN)
                out.append((dk, dv))
            return tuple(out)

        init = ((jnp.zeros((t, QK_DIM), F32), jnp.zeros((t, V_DIM), F32)),) * hp
        carry = block(j, init, True)
        carry = lax.fori_loop(j + 1, nq, lambda i, c: block(i, c, False), carry)
        for hh, (dk, dv) in enumerate(carry):
            dk = dk * LN_2
            base = hh * 2 * V_DIM
            dkv_ref[:, base:base + QK_NOPE] = dk[:, 0:QK_NOPE].astype(BF)
            dkv_ref[:, base + QK_NOPE:base + 2 * V_DIM] = dv.astype(BF)
            dkr_ref[hh] = dk[:, QK_NOPE:]

        @pl.when(j == nq - 1)
        def _():
            for hh in range(hp):
                for c in range(nq):
                    rows = slice(c * t, (c + 1) * t)
                    dq = dq_acc[hh, rows, :] * ATTN_SCALE
                    dq_ref[hh, rows, 0:QK_NOPE] = dq[:, 0:QK_NOPE].astype(BF)
                    dq_ref[hh, rows, QK_NOPE:] = _rope_bwd(dq[:, QK_NOPE:], cos_ref[rows, :], sin_ref[rows, :], p_ref[...]).astype(BF)

    row_stats = pl.BlockSpec((hp, nq, 1, t), lambda h, j: (h, 0, 0, 0))
    seq_rope = pl.BlockSpec((s, QK_ROPE), lambda h, j: (0, 0))
    head_seq = pl.BlockSpec((hp, s, QK_DIM), lambda h, j: (h, 0, 0))
    dkv, dkr_h, dq = pl.pallas_call(
        attn_bwd_body, name="mlab_attn", grid=(N_HEADS // hp, nq),
        in_specs=[pl.BlockSpec((hp, t, QK_DIM), lambda h, j: (h, j, 0)), pl.BlockSpec((hp, t, V_DIM), lambda h, j: (h, j, 0)),
                  head_seq, pl.BlockSpec((s, hp * V_DIM), lambda h, j: (0, h)), row_stats, row_stats, seq_rope, seq_rope,
                  pl.BlockSpec((QK_ROPE, QK_ROPE), lambda h, j: (0, 0))],
        out_specs=[pl.BlockSpec((t, hp * 2 * V_DIM), lambda h, j: (j, h)), pl.BlockSpec((hp, t, QK_ROPE), lambda h, j: (h, j, 0)), head_seq],
        out_shape=[_sds((s, N_HEADS * 2 * V_DIM), BF), _sds((N_HEADS, s, QK_ROPE), F32), _sds((N_HEADS, s, QK_DIM), BF)],
        scratch_shapes=[pltpu.VMEM((hp, s, QK_DIM), F32)],
        compiler_params=_params())(k, v, q, dob, lse, delta, cosf, sinf, perm)

    def dkr_body(d_ref, cos_ref, sin_ref, p_ref, o_ref):
        tot = d_ref[0]
        for hh in range(1, N_HEADS):
            tot = tot + d_ref[hh]
        o_ref[...] = _rope_bwd(tot, cos_ref[...], sin_ref[...], p_ref[...]).astype(BF)

    r64 = pl.BlockSpec((tm, QK_ROPE), lambda i: (i, 0))
    dkr = pl.pallas_call(
        dkr_body, name="mlab_dkr", grid=(nt,),
        in_specs=[pl.BlockSpec((N_HEADS, tm, QK_ROPE), lambda i: (0, i, 0)), r64, r64, pl.BlockSpec((QK_ROPE, QK_ROPE), lambda i: (0, 0))],
        out_specs=r64, out_shape=_sds((s, QK_ROPE), BF), compiler_params=_params())(dkr_h, cosf, sinf, perm)

    def lat_epi(acc, extra, outs, pids):
        dx, dg = _rms_bwd(acc, extra[0][...], extra[1][...], None)
        outs[0][...] = dx.astype(BF)
        _accumulate(outs[1], dg, pids[0])

    tp = IN_PROJ_ROWS if s % IN_PROJ_ROWS == 0 else tm

    def lat_bwd(name, a, a_spec, b, b_spec, n_k, lat, g, rank):
        row = lambda i, k: (i, 0)
        one = lambda i, k: (0, 0)
        return _mm(name, (s // tp, n_k), [a, b, lat, g.reshape(1, rank)],
                   [a_spec, b_spec, pl.BlockSpec((tp, rank), row), pl.BlockSpec((1, rank), one)],
                   [_sds((s, rank), BF), _sds((1, rank), F32)], [pl.BlockSpec((tp, rank), row), pl.BlockSpec((1, rank), one)],
                   NT, lat_epi, red=1, acc_shape=(tp, rank))

    d_ql, g_qnorm = lat_bwd("mlab_qup", dq, pl.BlockSpec((None, tp, QK_DIM), lambda i, h: (h, i, 0)),
                            w["w_qh"], pl.BlockSpec((None, Q_RANK, QK_DIM), lambda i, h: (h, 0, 0)), N_HEADS,
                            sv["q_lat"], w["q_norm"], Q_RANK)
    d_kvl, g_kvnorm = lat_bwd("mlab_kvup", dkv, pl.BlockSpec((tp, 512), lambda i, kk: (i, kk)),
                              w["g512"], pl.BlockSpec((None, KV_RANK, 512), lambda i, kk: (kk, 0, 0)), N_DEV,
                              sv["kv_lat"], w["kv_norm"], KV_RANK)

    def in_bwd(dql_ref, dkvl_ref, dkr_ref, dz_ref, wq_ref, wkv_ref, wkr_ref, wz_ref, x_ref, g_ref, dy_ref, dx_ref, dxb_ref, dg_ref):
        acc = (_dot(dql_ref[...], wq_ref[...], NT) + _dot(dkvl_ref[...], wkv_ref[...], NT)
               + _dot(dkr_ref[...], wkr_ref[...], NT) + _dot(dz_ref[...], wz_ref[...], NT))
        dx, dg = _rms_bwd(acc, x_ref[...], g_ref[...], dy_ref[...])
        dx_ref[...] = dx
        dxb_ref[...] = dx.astype(BF)
        _accumulate(dg_ref, dg, pl.program_id(0))

    def full(a):
        return pl.BlockSpec(a.shape, lambda i: (0,) * a.ndim)

    def rows(c):
        return pl.BlockSpec((tm, c), lambda i: (i, 0))

    gm = w["norm"].reshape(1, D_MODEL)
    dx, dxb, g_norm = pl.pallas_call(
        in_bwd, name="mlab_in", grid=(nt,),
        in_specs=[rows(Q_RANK), rows(KV_RANK), rows(QK_ROPE), rows(D_INNER), full(w["w_q"]), full(w["w_kv"]), full(w["w_kr"]),
                  full(w["w_z"]), rows(D_MODEL), full(gm), rows(D_MODEL)],
        out_specs=[rows(D_MODEL), rows(D_MODEL), full(gm)],
        out_shape=[_sds((s, D_MODEL), F32), _sds((s, D_MODEL), BF), _sds((1, D_MODEL), F32)],
        compiler_params=_params())(d_ql, d_kvl, dkr, dz, w["w_q"], w["w_kv"], w["w_kr"], w["w_z"], sv["x"], gm, dy)

    xn = sv["xn"]
    one = lambda j: (0, 0)
    g_q = _tn("mlab_gq", xn, d_ql, (D_MODEL, Q_RANK), (D_MODEL, Q_RANK), one, D_MODEL, Q_RANK, (1,), one, one)
    g_kv = _tn("mlab_gkv", xn, d_kvl, (D_MODEL, KV_RANK), (D_MODEL, KV_RANK), one, D_MODEL, KV_RANK, (1,), one, one)
    g_kr = _tn("mlab_gkr", xn, dkr, (D_MODEL, QK_ROPE), (D_MODEL, QK_ROPE), one, D_MODEL, QK_ROPE, (1,), one, one)
    g_z = _tn("mlab_gz", xn, dz, (D_MODEL, D_INNER), (D_MODEL, 512), lambda j: (0, j), D_MODEL, 512, (4,), one, lambda j: (0, j))
    g_in = jnp.concatenate([g_q, g_kv, g_kr, g_z], axis=1)
    g_qh = _tn("mlab_gqup", sv["qn"], dq, (N_HEADS, Q_RANK, QK_DIM), (None, Q_RANK, QK_DIM), lambda h: (h, 0, 0),
               Q_RANK, QK_DIM, (N_HEADS,), lambda h: (0, 0), lambda h: (h, 0, 0))
    g_kvup = _tn("mlab_gkvup", sv["kvn"], dkv, (N_DEV, KV_RANK, 512), (None, KV_RANK, 512), lambda j: (j, 0, 0),
                 KV_RANK, 512, (N_DEV,), lambda j: (0, 0), lambda j: (0, j))
    g_out = _w_out_grad("mlab_gout", sv["gated"], dyb)
    s384 = g_qh.reshape(N_DEV, 2, Q_RANK, QK_DIM).transpose(0, 2, 1, 3).reshape(N_DEV, Q_RANK, 2 * QK_DIM)
    s344 = g_in.reshape(D_MODEL, N_DEV, 344).transpose(1, 0, 2)
    return dx, dxb, dict(s344=s344, s384=s384, s512=g_kvup, s1024=g_out.reshape(N_DEV, 256, D_MODEL),
                         norm=g_norm[0], q_norm=g_qnorm[0], kv_norm=g_kvnorm[0])


def _loss_head(x, g, target, tm):
    s, d = x.shape

    def body(x_ref, g_ref, t_ref, dx_ref, dxb_ref, dg_ref, loss_ref):
        i = pl.program_id(0)
        xv, gv = x_ref[...], g_ref[...]
        r = lax.rsqrt(jnp.mean(xv * xv, axis=-1, keepdims=True) + NORM_EPS)
        err = (xv * r) * gv - t_ref[...]
        part = 0.5 * jnp.sum(jnp.mean(err * err, axis=-1, keepdims=True), axis=0, keepdims=True)
        dx, dg = _rms_bwd(err * (1.0 / d), xv, gv, None)
        dx_ref[...] = dx
        dxb_ref[...] = dx.astype(BF)
        _accumulate(dg_ref, dg, i)
        _accumulate(loss_ref, jnp.broadcast_to(part, loss_ref.shape), i)

    row = pl.BlockSpec((tm, d), lambda i: (i, 0))
    one = pl.BlockSpec((1, d), lambda i: (0, 0))
    return pl.pallas_call(
        body, name="loss_head", grid=(s // tm,), in_specs=[row, one, row],
        out_specs=[row, row, one, pl.BlockSpec((8, 128), lambda i: (0, 0))],
        out_shape=[_sds((s, d), F32), _sds((s, d), BF), _sds((1, d), F32), _sds((8, 128), F32)],
        compiler_params=_params())(x, g.reshape(1, d), target)


def _rope_tables(pos):
    inv_freq = ROPE_BASE ** (-jnp.arange(0, QK_ROPE, 2, dtype=F32) / QK_ROPE)
    ang = pos.astype(F32)[:, None] * inv_freq
    cos, sin = jnp.cos(ang), jnp.sin(ang)
    idx = jnp.arange(QK_ROPE)
    perm = (idx[:, None] == (idx[None, :] + QK_ROPE // 2) % QK_ROPE).astype(F32)
    return jnp.concatenate([cos, cos], axis=1), jnp.concatenate([-sin, sin], axis=1), perm


def _local_step(x, pos, target, final_norm, get_w, put_g):
    s = x.shape[0]
    tm = min(512, s)
    rope = _rope_tables(pos)
    w0 = get_w(0, [])
    x1, sv0 = _pool_fwd(x, 0, w0, tm)
    w1 = get_w(1, [x1])
    x2, sv1 = _conv_fwd(x1, w1, tm)
    w2 = get_w(2, [x2])
    x3, sv2 = _mla_fwd(x2, w2, rope, tm)
    w3 = get_w(3, [x3])
    x4, sv3 = _pool_fwd(x3, 1, w3, tm)
    d4, d4b, g_final, loss = _loss_head(x4, final_norm, target, tm)
    d3, d3b, gp1 = _pool_bwd(d4, d4b, 1, sv3["w"], sv3, tm, [])
    dep = put_g(3, gp1)
    d2, d2b, gm = _mla_bwd(d3, d3b, w2, sv2, rope, tm, dep)
    dep = put_g(2, gm)
    sent = {}

    def send_conv(part):
        sent["dep"] = put_g(1, part)
        return sent["dep"]

    d1, d1b, gc = _conv_bwd(d2, d2b, sv1["w"], sv1, tm, dep, send_conv)
    d0, _, gp0 = _pool_bwd(d1, d1b, 0, sv0["w"], sv0, tm, sent["dep"], early=lambda part: put_g(0, part))
    put_g(4, {0: dict(gp0, final_norm=g_final[0]), 1: gc, 2: gm, 3: gp1})
    return loss[0, 0], d0


def _pack_groups(p):
    bf = lambda a: a.astype(BF)
    grp = lambda l: bf(p["pool_w_grp"][l].reshape(4 * 64, POOL_GROUP))
    return [[bf(p["pool_w_in"][0]), _pack_small(p, SMALL_ROWS_AG)],
            [grp(0), bf(p["pool_w_out"][0])],
            [bf(p["conv_w_in"][0])],
            [bf(p["conv_w_out"][0])],
            [bf(p[k][0]) for k in ("mla_w_in", "mla_w_q_up", "mla_w_kv_up", "mla_w_out")],
            [bf(p["pool_w_in"][1]), grp(1), bf(p["pool_w_out"][1])]]


_SMALL_SHARDED = ("pool_norm", "pool_scale", "mla_norm", "mla_q_norm", "mla_kv_norm", "conv_w")
_SMALL_REPLICATED = ("conv_norm", "final_norm")


def _pack_small(p, rows, with_replicated=False):
    parts = [p[k].reshape(-1) for k in _SMALL_SHARDED]
    if with_replicated:
        parts += [p[k].reshape(-1) for k in _SMALL_REPLICATED]
    flat = jnp.concatenate(parts)
    return jnp.pad(flat, (0, rows * 128 - flat.shape[0])).reshape(rows, 128)


_SMALL_SHARD_SHAPES = dict(pool_norm=(2, 128), pool_scale=(2, 256), mla_norm=(1, 128), mla_q_norm=(1, 48),
                           mla_kv_norm=(1, 32), conv_w=(1, 3, 256), conv_norm=(1, 1024), final_norm=(1024,))


def _unpack_small(buf, with_replicated=False):
    flat = buf.reshape(-1)
    out, off = {}, 0
    for k in _SMALL_SHARDED + (_SMALL_REPLICATED if with_replicated else ()):
        shp = _SMALL_SHARD_SHAPES[k]
        n = 1
        for d in shp:
            n *= d
        out[k] = flat[off:off + n].reshape(shp)
        off += n
    return out


def _small_views(gsmall):
    flat = gsmall.reshape(N_DEV, -1)

    def cols(off, rows, width):
        return flat[:, off:off + rows * width].reshape(N_DEV, rows, width).transpose(1, 0, 2).reshape(rows, N_DEV * width)

    return dict(pool_norm=cols(0, 2, 128), pool_scale=cols(256, 2, 256), mla_norm=cols(768, 1, 128)[0],
                q_norm=cols(896, 1, 48)[0], kv_norm=cols(944, 1, 32)[0], conv_w=cols(976, 3, 256))


def _layer_weights(layer, bufs, small, conv_norm):
    if layer in (0, 3):
        l = 0 if layer == 0 else 1
        return dict(g_in=bufs[0], norm=small["pool_norm"][l], scale=small["pool_scale"][l])
    if layer == 1:
        return dict(g_in=bufs[0], norm=conv_norm.reshape(D_MODEL), conv_w=small["conv_w"])
    g344, g384, g512, g1024 = bufs
    w_in = g344.transpose(1, 0, 2).reshape(D_MODEL, N_DEV * 344)
    return dict(
        g512=g512, g1024=g1024,
        w_q=w_in[:, :Q_RANK], w_kv=w_in[:, Q_RANK:Q_RANK + KV_RANK],
        w_kr=w_in[:, Q_RANK + KV_RANK:Q_RANK + KV_RANK + QK_ROPE], w_z=w_in[:, Q_RANK + KV_RANK + QK_ROPE:],
        w_qh=g384.reshape(N_DEV, Q_RANK, 2, QK_DIM).transpose(0, 2, 1, 3).reshape(N_HEADS, Q_RANK, QK_DIM),
        norm=small["mla_norm"], q_norm=small["q_norm"], kv_norm=small["kv_norm"])


_GRAD_KEYS = {0: ("g_in", "g_grp", "g_out"), 3: ("g_in", "g_grp", "g_out"), 1: ("g_in", "g_out"), 2: ("s344", "s384", "s512", "s1024")}
_GRAD_PARAM = {0: dict(g_in="pool_w_in", g_grp="pool_w_grp", g_out="pool_w_out"), 1: dict(g_in="conv_w_in", g_out="conv_w_out"),
               2: dict(s344="mla_w_in", s384="mla_w_q_up", s512="mla_w_kv_up", s1024="mla_w_out")}
_GRAD_PARAM[3] = _GRAD_PARAM[0]


def _grad_group(layer, g):
    keys = tuple(k for k in _GRAD_KEYS[layer] if k in g)
    return keys, [g[k] for k in keys]


def _pack_small_grads(g):
    def split(a, rows, width):
        return a.reshape(rows, N_DEV, width).transpose(1, 0, 2).reshape(N_DEV, rows * width)

    rep = lambda a: jnp.broadcast_to(a.reshape(1, -1), (N_DEV, a.size))
    flat = jnp.concatenate([
        split(jnp.stack([g[0]["norm"], g[3]["norm"]]), 2, 128), split(jnp.stack([g[0]["scale"], g[3]["scale"]]), 2, 256),
        split(g[2]["norm"], 1, 128), split(g[2]["q_norm"], 1, 48), split(g[2]["kv_norm"], 1, 32), split(g[1]["conv_w"], 3, 256),
        rep(g[1]["norm"]), rep(g[0]["final_norm"])], axis=1)
    return jnp.pad(flat, ((0, 0), (0, SMALL_ROWS_RS * 128 - flat.shape[1]))).reshape(N_DEV, SMALL_ROWS_RS, 128)


def _peers(x, y, c):
    for k in range(1, N_DEV):
        px = 1 - x if k & 4 else x
        py = 1 - y if k & 2 else y
        pc = 1 - c if k & 1 else c
        yield k - 1, (px, py, pc), 4 * px + 2 * py + pc


def _remote_copies(srcs, lands, send_sems, recv_sems, gather):
    x, y, c = lax.axis_index("x"), lax.axis_index("y"), lax.axis_index("c")
    me = 4 * x + 2 * y + c
    copies = []
    for k, peer, pidx in _peers(x, y, c):
        for a, (src, land) in enumerate(zip(srcs, lands)):
            copies.append(pltpu.make_async_remote_copy(
                src_ref=src if gather else src.at[pidx], dst_ref=land.at[me],
                send_sem=send_sems.at[a * (N_DEV - 1) + k], recv_sem=recv_sems.at[a * (N_DEV - 1) + k],
                device_id=peer, device_id_type=pl.DeviceIdType.MESH))
    return copies


_HBM = pl.BlockSpec(memory_space=pltpu.HBM)
_SEM = pl.BlockSpec(memory_space=pltpu.SEMAPHORE)
_EFFECT = pltpu.SideEffectType.DATAFLOW_SIDE_EFFECTING


def _own_slabs(name, arrays, gather, dep):
    n, nd = len(arrays), len(dep)
    me = (4 * lax.axis_index("x") + 2 * lax.axis_index("y") + lax.axis_index("c")).astype(jnp.int32).reshape(1)

    def body(me_ref, *refs):
        for a in range(n):
            refs[n + nd + a][...] = refs[a][...]

    def slab(shape):
        return pl.BlockSpec((None,) + tuple(shape), lambda i, me_ref: (me_ref[0],) + (0,) * len(shape))

    def whole(shape):
        return pl.BlockSpec(tuple(shape), lambda i, me_ref: (0,) * len(shape))

    outs = [_sds(((N_DEV,) + a.shape) if gather else a.shape, a.dtype) for a in arrays]
    grid_spec = pltpu.PrefetchScalarGridSpec(
        num_scalar_prefetch=1, grid=(1,),
        in_specs=[whole(a.shape) if gather else slab(a.shape[1:]) for a in arrays] + [_ANY] * nd,
        out_specs=[slab(o.shape[1:]) for o in outs])
    return pl.pallas_call(body, name=name, grid_spec=grid_spec, out_shape=outs, compiler_params=_params())(me, *arrays, *dep)


def _exchange_start(name, arrays, lands, gather):
    n = len(arrays)

    def body(*refs):
        srcs, lnds, send_sems, recv_sems, token = refs[:n], refs[n:2 * n], refs[2 * n], refs[2 * n + 1], refs[-1]
        for cp in _remote_copies(srcs, lnds, send_sems, recv_sems, gather):
            cp.start()
        token[...] = jnp.zeros(token.shape, F32)

    sems = pltpu.SemaphoreType.DMA((n * (N_DEV - 1),))
    thru = [pltpu.HBM(a.shape, a.dtype) for a in list(arrays) + list(lands)]
    res = pl.pallas_call(
        body, name=name, in_specs=[_HBM] * (2 * n),
        out_specs=[_SEM, _SEM] + [_HBM] * (2 * n) + [pl.BlockSpec(memory_space=pltpu.VMEM)],
        out_shape=[sems, sems] + thru + [_sds((8, 128), F32)],
        input_output_aliases={i: 2 + i for i in range(2 * n)},
        compiler_params=pltpu.CompilerParams(has_side_effects=_EFFECT),
    )(*[pltpu.with_memory_space_constraint(a, pltpu.HBM) for a in list(arrays) + list(lands)])
    return res[0], res[1], list(res[2:2 + n]), list(res[2 + n:2 + 2 * n]), res[-1]


def _exchange_wait(name, send_sems, recv_sems, arrays, lands, after, gather):
    n = len(arrays)
    n_after = len(after)

    def body(*refs):
        srcs, lnds = refs[:n], refs[n:2 * n]
        copies = _remote_copies(srcs, lnds, refs[2 * n], refs[2 * n + 1], gather)
        for cp in copies:
            cp.wait_send()
        for cp in copies:
            cp.wait_recv()

    thru = [pltpu.HBM(a.shape, a.dtype) for a in list(arrays) + list(lands)]
    res = pl.pallas_call(
        body, name=name, in_specs=[_HBM] * (2 * n) + [_SEM, _SEM] + [pl.BlockSpec(memory_space=pl.ANY)] * n_after,
        out_specs=[_HBM] * (2 * n), out_shape=thru, input_output_aliases={i: i for i in range(2 * n)},
        compiler_params=pltpu.CompilerParams(has_side_effects=_EFFECT),
    )(*arrays, *lands, send_sems, recv_sems, *after)
    return list(res[n:])


def _adamw_math(g, w, m, v):
    m = ADAM_B1 * m + (1.0 - ADAM_B1) * g
    v = ADAM_B2 * v + (1.0 - ADAM_B2) * (g * g)
    m_hat = m / (1.0 - ADAM_B1 ** ADAM_STEP)
    v_hat = v / (1.0 - ADAM_B2 ** ADAM_STEP)
    delta = -ADAM_LR * (m_hat / (jnp.sqrt(v_hat) + ADAM_EPS) + ADAM_WD * w)
    return delta, m, v


def _sum_adamw(name, recv, row_off, w, m, v, tr, layer=0):
    width = recv.shape[-1]
    w2, m2, v2 = (a.reshape(a.shape[0], -1, width) for a in (w, m, v))
    rows = w2.shape[1]
    base = row_off // tr

    def body(r_ref, w_ref, m_ref, v_ref, g_ref, d_ref, mo_ref, vo_ref):
        g = r_ref[0].astype(F32)
        for src in range(1, N_DEV):
            g = g + r_ref[src].astype(F32)
        delta, mn, vn = _adamw_math(g, w_ref[...], m_ref[...], v_ref[...])
        g_ref[...] = g
        d_ref[...] = delta
        mo_ref[...] = mn
        vo_ref[...] = vn

    blk = pl.BlockSpec((tr, width), lambda i: (i, 0))
    wblk = pl.BlockSpec((None, tr, width), lambda i: (layer, i, 0))
    return pl.pallas_call(
        body, name=name, grid=(rows // tr,),
        in_specs=[pl.BlockSpec((N_DEV, tr, width), lambda i: (0, base + i, 0)), wblk, wblk, wblk],
        out_specs=[blk] * 4, out_shape=[_sds((rows, width), F32)] * 4, compiler_params=_params())(recv, w2, m2, v2)


_WEIGHTS = ("pool_norm", "pool_w_in", "pool_w_grp", "pool_scale", "pool_w_out", "conv_norm", "conv_w_in", "conv_w", "conv_w_out",
            "mla_norm", "mla_w_in", "mla_q_norm", "mla_w_q_up", "mla_kv_norm", "mla_w_kv_up", "mla_w_out", "final_norm")


def _step(x, positions, loss_target, p, m, v):
    gathers, tokens, dep = [], [], []
    for group, arrays in enumerate(_pack_groups(p)):
        lands = _own_slabs(f"gather{group}_own", arrays, True, dep)
        ssem, rsem, arrays, lands, token = _exchange_start(f"gather{group}_start", arrays, lands, True)
        gathers.append((ssem, rsem, arrays, lands))
        tokens.append(token)
        dep = [token]
    state = {}

    def wait_group(group, after):
        return _exchange_wait(f"gather{group}_wait", *gathers[group], after, True)

    def get_w(layer, after):
        if layer == 0:
            bufs = wait_group(0, list(tokens))
            state["small"] = _small_views(bufs[1])
            rest = lambda later: dict(zip(("g_grp", "g_out"), wait_group(1, later)))
        elif layer == 1:
            bufs = wait_group(2, after)
            rest = lambda later: dict(g_out=wait_group(3, later)[0])
        elif layer == 2:
            bufs = wait_group(4, after)
        else:
            bufs = wait_group(5, after)
            rest = lambda later: dict(g_grp=bufs[1], g_out=bufs[2])
        w = _layer_weights(layer, bufs, state["small"], p["conv_norm"])
        if layer != 2:
            w["rest"] = rest
        return w

    scatters = []

    def put_g(layer, g):
        if layer == 4:
            keys, arrays = ("small",), [_pack_small_grads(g)]
        else:
            keys, arrays = _grad_group(layer, g)
        n = len(scatters)
        lands = _own_slabs(f"scatter{n}_own", arrays, False, [])
        ssem, rsem, arrays, lands, token = _exchange_start(f"scatter{n}_start", arrays, lands, False)
        scatters.append((layer, keys, (ssem, rsem, arrays, lands)))
        tokens.append(token)
        return [token]

    loss, grad_x = _local_step(x[0], positions[0], loss_target[0], p["final_norm"], get_w, put_g)

    res, after = {}, [tokens[-1]]
    for n, (layer, keys, handles) in enumerate(scatters):
        recv = _exchange_wait(f"scatter{n}_wait", *handles, after, False)
        if layer == 4:
            break
        l = 1 if layer == 3 else 0
        for key, buf in zip(keys, recv):
            name = _GRAD_PARAM[layer][key]
            tr = min(256, buf.shape[1]) if name != "mla_w_q_up" else buf.shape[1]
            res[name, l] = _sum_adamw(f"adam_{name}{l}", buf, 0, p[name], m[name], v[name], tr, l)
        after = [res[name, l][1]]
    small = _sum_adamw("adam_small", recv[0], 0, _pack_small(p, SMALL_ROWS_RS, True)[None], _pack_small(m, SMALL_ROWS_RS, True)[None],
                       _pack_small(v, SMALL_ROWS_RS, True)[None], SMALL_ROWS_RS)
    small = [_unpack_small(a, True) for a in small]
    final = {k: tuple(part[k] for part in small) for k in _SMALL_SHARDED + _SMALL_REPLICATED}
    for k in _WEIGHTS:
        if k not in final:
            layers = [res[k, l] for l in range(p[k].shape[0])]
            final[k] = tuple(jnp.stack([lay[part] for lay in layers]).reshape(p[k].shape) for part in range(4))
    res = final

    loss = lax.psum(loss, ("x", "y", "c"))
    out = [loss, grad_x[None]]
    for part in range(4):
        out += [res[k][part] for k in _WEIGHTS]
    return tuple(out)


def kernel(x, positions, pool_norm, pool_w_in, pool_w_grp, pool_scale, pool_w_out, conv_norm, conv_w_in, conv_w, conv_w_out, mla_norm, mla_w_in, mla_q_norm, mla_w_q_up, mla_kv_norm, mla_w_kv_up, mla_w_out, final_norm, loss_target, m_pool_norm, m_pool_w_in, m_pool_w_grp, m_pool_scale, m_pool_w_out, m_conv_norm, m_conv_w_in, m_conv_w, m_conv_w_out, m_mla_norm, m_mla_w_in, m_mla_q_norm, m_mla_w_q_up, m_mla_kv_norm, m_mla_w_kv_up, m_mla_w_out, m_final_norm, v_pool_norm, v_pool_w_in, v_pool_w_grp, v_pool_scale, v_pool_w_out, v_conv_norm, v_conv_w_in, v_conv_w, v_conv_w_out, v_mla_norm, v_mla_w_in, v_mla_q_norm, v_mla_w_q_up, v_mla_kv_norm, v_mla_w_kv_up, v_mla_w_out, v_final_norm):
    p = dict(pool_norm=pool_norm, pool_w_in=pool_w_in, pool_w_grp=pool_w_grp, pool_scale=pool_scale, pool_w_out=pool_w_out,
             conv_norm=conv_norm, conv_w_in=conv_w_in, conv_w=conv_w, conv_w_out=conv_w_out, mla_norm=mla_norm, mla_w_in=mla_w_in,
             mla_q_norm=mla_q_norm, mla_w_q_up=mla_w_q_up, mla_kv_norm=mla_kv_norm, mla_w_kv_up=mla_w_kv_up, mla_w_out=mla_w_out,
             final_norm=final_norm)
    m = dict(pool_norm=m_pool_norm, pool_w_in=m_pool_w_in, pool_w_grp=m_pool_w_grp, pool_scale=m_pool_scale, pool_w_out=m_pool_w_out,
             conv_norm=m_conv_norm, conv_w_in=m_conv_w_in, conv_w=m_conv_w, conv_w_out=m_conv_w_out, mla_norm=m_mla_norm,
             mla_w_in=m_mla_w_in, mla_q_norm=m_mla_q_norm, mla_w_q_up=m_mla_w_q_up, mla_kv_norm=m_mla_kv_norm,
             mla_w_kv_up=m_mla_w_kv_up, mla_w_out=m_mla_w_out, final_norm=m_final_norm)
    v = dict(pool_norm=v_pool_norm, pool_w_in=v_pool_w_in, pool_w_grp=v_pool_w_grp, pool_scale=v_pool_scale, pool_w_out=v_pool_w_out,
             conv_norm=v_conv_norm, conv_w_in=v_conv_w_in, conv_w=v_conv_w, conv_w_out=v_conv_w_out, mla_norm=v_mla_norm,
             mla_w_in=v_mla_w_in, mla_q_norm=v_mla_q_norm, mla_w_q_up=v_mla_w_q_up, mla_kv_norm=v_mla_kv_norm,
             mla_w_kv_up=v_mla_w_kv_up, mla_w_out=v_mla_w_out, final_norm=v_final_norm)
    return _step(x, positions, loss_target, p, m, v)
```

```python
import functools

import jax
import jax.numpy as jnp
from jax import lax
from jax.experimental import pallas as pl
from jax.experimental.pallas import tpu as pltpu

BF = jnp.bfloat16
F32 = jnp.float32

N_DEV = 8
D_MODEL = 1024
D_INNER = 2048
POOL_WINDOWS = (2, 4, 8, 16)
POOL_GROUP = 512
N_HEADS = 16
QK_NOPE = 128
QK_ROPE = 64
QK_DIM = QK_NOPE + QK_ROPE
V_DIM = 128
Q_RANK = 384
KV_RANK = 256
ATTN_SCALE = QK_DIM ** -0.5
LOG2_E = 1.4426950408889634
LN_2 = 0.6931471805599453
Q_PRESCALE = ATTN_SCALE * LOG2_E
ATTN_TILE = 512
ATTN_HEADS_PER_STEP = 2
ROPE_BASE = 10000.0
NORM_EPS = 1e-6
NEG_BIG = -1e30

ADAM_LR = 0.001
ADAM_B1 = 0.9
ADAM_B2 = 0.999
ADAM_EPS = 1e-08
ADAM_WD = 0.01
ADAM_STEP = 10

VMEM_LIMIT_BYTES = 52 * 1024 * 1024
IN_PROJ_ROWS = 1024
POOL_HALO = 32
CONV_HALO = 16

NN = (((1,), (0,)), ((), ()))
NT = (((1,), (1,)), ((), ()))
TN = (((0,), (0,)), ((), ()))

TWO_LEVEL_GROUPS = 3
SMALL_ROWS_AG = 16
SMALL_ROWS_RS = 32


def _sds(shape, dtype):
    return jax.ShapeDtypeStruct(tuple(shape), dtype)


def _params():
    return pltpu.CompilerParams(vmem_limit_bytes=VMEM_LIMIT_BYTES)


_ANY = pl.BlockSpec(memory_space=pl.ANY)


def _dot(a, b, dims):
    return lax.dot_general(a, b, dims, preferred_element_type=F32)


def _sig(z):
    return 1.0 / (1.0 + jnp.exp(-z))


def _silu_and_grad(z):
    sig = _sig(z)
    return z * sig, sig * (1.0 + z * (1.0 - sig))


def _to_row(col):
    return jnp.broadcast_to(col, (col.shape[0], 128)).T[0:1, :]


def _rope_swap(x, p):
    pb = p.astype(BF)
    hi = x.astype(BF)
    r1 = x - hi.astype(F32)
    mid = r1.astype(BF)
    lo = (r1 - mid.astype(F32)).astype(BF)
    return (_dot(hi, pb, NN) + _dot(mid, pb, NN)) + _dot(lo, pb, NN)


def _rope_fwd(x, cosf, sinf, p):
    return x * cosf + _rope_swap(x, p) * sinf


def _rope_bwd(dy, cosf, sinf, p):
    return dy * cosf + _rope_swap(dy * sinf, p)


def _rms_bwd(dxn, x, g, res):
    r = lax.rsqrt(jnp.mean(x * x, axis=-1, keepdims=True) + NORM_EPS)
    v = dxn * g
    dx = r * v - x * ((r * r * r) * jnp.mean(v * x, axis=-1, keepdims=True))
    if res is not None:
        dx = dx + res
    dg = jnp.sum(dxn * (x * r), axis=0, keepdims=True)
    return dx, dg


def _accumulate(ref, val, step):
    @pl.when(step == 0)
    def _():
        ref[...] = val

    @pl.when(step > 0)
    def _():
        ref[...] += val


def _mm(name, grid, ins, in_specs, outs, out_specs, dims, epi, red=None, acc_shape=None):
    n_in, n_out = len(ins), len(outs)
    n_red = None if red is None else grid[red]

    def body(*refs):
        in_refs, out_refs = refs[:n_in], refs[n_in:n_in + n_out]
        pids = tuple(pl.program_id(ax) for ax in range(len(grid)))
        a, b = in_refs[0][...], in_refs[1][...]
        if a.ndim == 3:
            a = a.reshape(-1, a.shape[-1])
        if b.ndim == 3:
            b = b.reshape(-1, b.shape[-1])
        part = _dot(a.astype(BF), b.astype(BF), dims)
        if red is None:
            epi(part, in_refs[2:], out_refs, pids)
        else:
            acc = refs[n_in + n_out]
            k = pids[red]
            _accumulate(acc, part, k)

            @pl.when(k == n_red - 1)
            def _():
                epi(acc[...], in_refs[2:], out_refs, pids)

    scratch = [] if red is None else [pltpu.VMEM(acc_shape, F32)]
    return pl.pallas_call(body, name=name, grid=grid, in_specs=in_specs, out_specs=out_specs, out_shape=outs,
                          scratch_shapes=scratch, compiler_params=_params())(*ins)


def _store(part, extra, outs, pids):
    outs[0][...] = part.astype(outs[0].dtype)


def _rms_fwd(name, x, g, tm):
    s, d = x.shape

    def body(x_ref, g_ref, o_ref):
        xv = x_ref[...]
        r = lax.rsqrt(jnp.mean(xv * xv, axis=-1, keepdims=True) + NORM_EPS)
        o_ref[...] = ((xv * r) * g_ref[...]).astype(BF)

    return pl.pallas_call(body, name=name, grid=(s // tm,),
                          in_specs=[pl.BlockSpec((tm, d), lambda i: (i, 0)), pl.BlockSpec((1, d), lambda i: (0, 0))],
                          out_specs=pl.BlockSpec((tm, d), lambda i: (i, 0)), out_shape=_sds((s, d), BF),
                          compiler_params=_params())(x, g.reshape(1, d))


def _norm_in_proj(name, x, g, wbuf, w_index, n_j, tm):
    s = x.shape[0]
    ti = IN_PROJ_ROWS if s % IN_PROJ_ROWS == 0 else tm

    def body(x_ref, g_ref, w_ref, h_ref, xn_ref):
        @pl.when(pl.program_id(1) == 0)
        def _():
            xv = x_ref[...]
            r = lax.rsqrt(jnp.mean(xv * xv, axis=-1, keepdims=True) + NORM_EPS)
            xn_ref[...] = ((xv * r) * g_ref[...]).astype(BF)

        h_ref[...] = _dot(xn_ref[...], w_ref[...], NN).astype(BF)

    row = lambda i, j: (i, 0)
    return pl.pallas_call(
        body, name=name, grid=(s // ti, n_j),
        in_specs=[pl.BlockSpec((ti, D_MODEL), row), pl.BlockSpec((1, D_MODEL), lambda i, j: (0, 0)),
                  pl.BlockSpec((None, D_MODEL, 512), w_index)],
        out_specs=[pl.BlockSpec((ti, 512), lambda i, j: (i, j)), pl.BlockSpec((ti, D_MODEL), row)],
        out_shape=[_sds((s, n_j * 512), BF), _sds((s, D_MODEL), BF)], compiler_params=_params())(x, g.reshape(1, D_MODEL), wbuf)


def _tn(name, a, b, out_shape, out_block, out_index, a_cols, b_cols, grid, a_index, b_index, dep=()):
    s = a.shape[-2]
    a_block = (s, a_cols) if a.ndim == 2 else (None, s, a_cols)
    b_block = (s, b_cols) if b.ndim == 2 else (None, s, b_cols)

    def epi(part, extra, outs, pids):
        outs[0][...] = part.astype(BF).reshape(outs[0].shape)

    return _mm(name, grid, [a, b] + list(dep), [pl.BlockSpec(a_block, a_index), pl.BlockSpec(b_block, b_index)] + [_ANY] * len(dep),
               [_sds(out_shape, BF)], [pl.BlockSpec(out_block, out_index)], TN, epi)[0]


def _pool_window_fwd(name, h, tm):
    s = h.shape[0]
    hb = POOL_HALO

    def body(u_ref, halo_ref, o_ref, e_ref, a_ref, b_ref):
        i = pl.program_id(0)
        row = lax.broadcasted_iota(jnp.int32, (tm, 1), 0) + i * tm
        for g, w in enumerate(POOL_WINDOWS):
            cs = slice(g * POOL_GROUP, (g + 1) * POOL_GROUP)
            e_ref[0:hb, :] = jnp.where(i > 0, halo_ref[:, cs].astype(F32), 0.0)
            e_ref[hb:, :] = u_ref[:, cs].astype(F32)
            src, bufs = e_ref, (a_ref, b_ref)
            for lv in range(1, w.bit_length()):
                dst, st, sh = bufs[(lv - 1) % 2], 8 * lv, 2 ** (lv - 1)
                n = hb + tm - st
                dst[st:, :] = src[st:, :] + src[pl.ds(st - sh, n), :]
                src = dst
            cnt = jnp.minimum(row + 1, w).astype(F32)
            o_ref[:, cs] = (src[hb:, :] / cnt - u_ref[:, cs].astype(F32)).astype(BF)

    per = tm // hb
    return pl.pallas_call(
        body, name=name, grid=(s // tm,),
        in_specs=[pl.BlockSpec((tm, D_INNER), lambda i: (i, 0)),
                  pl.BlockSpec((hb, D_INNER), lambda i: (jnp.maximum(i * per - 1, 0), 0))],
        out_specs=pl.BlockSpec((tm, D_INNER), lambda i: (i, 0)), out_shape=_sds((s, D_INNER), BF),
        scratch_shapes=[pltpu.VMEM((hb + tm, POOL_GROUP), F32)] * 3, compiler_params=_params())(h, h)


def _pool_window_bwd(name, dp, tm, dh):
    s = dp.shape[0]
    nt = s // tm
    hb = POOL_HALO

    def body(d_ref, halo_ref, dh_in_ref, o_ref, e_ref, a_ref, b_ref):
        i = pl.program_id(0)
        row = lax.broadcasted_iota(jnp.int32, (tm, 1), 0) + i * tm
        hrow = lax.broadcasted_iota(jnp.int32, (hb, 1), 0) + (i + 1) * tm
        for g, w in enumerate(POOL_WINDOWS):
            cs = slice(g * POOL_GROUP, (g + 1) * POOL_GROUP)
            e_ref[0:tm, :] = d_ref[:, cs] / jnp.minimum(row + 1, w).astype(F32)
            e_ref[tm:, :] = jnp.where(i < nt - 1, halo_ref[:, cs] / jnp.minimum(hrow + 1, w).astype(F32), 0.0)
            src, bufs = e_ref, (a_ref, b_ref)
            for lv in range(1, w.bit_length()):
                dst, sh = bufs[(lv - 1) % 2], 2 ** (lv - 1)
                n = tm + hb - 8 * lv
                dst[0:n, :] = src[0:n, :] + src[pl.ds(sh, n), :]
                src = dst
            o_ref[:, cs] = (src[0:tm, :] - d_ref[:, cs]).astype(BF)

    per = tm // hb
    last = s // hb - 1
    return pl.pallas_call(
        body, name=name, grid=(nt,),
        in_specs=[pl.BlockSpec((tm, D_INNER), lambda i: (i, 0)),
                  pl.BlockSpec((hb, D_INNER), lambda i: (jnp.minimum((i + 1) * per, last), 0)), _ANY],
        out_specs=pl.BlockSpec((tm, D_INNER), lambda i: (i, 0)), out_shape=_sds(dh.shape, BF),
        input_output_aliases={2: 0},
        scratch_shapes=[pltpu.VMEM((hb + tm, POOL_GROUP), F32)] * 3, compiler_params=_params())(dp, dp, dh)


def _grp_block():
    return pl.BlockSpec((N_DEV, 64, POOL_GROUP), lambda i, g: (0, g, 0))


def _pool_fwd(x, l, w, tm):
    s = x.shape[0]
    nt = s // tm
    n = f"pool{l}"
    h, xn = _norm_in_proj(n + "_in", x, w["norm"], w["g_in"], lambda i, j: (j, 0, 0), 8, tm)
    pooled = _pool_window_fwd(n + "_win", h, tm)
    w = dict(w, **w["rest"]([h]))

    def gate(part, extra, outs, pids):
        z = extra[0][...].astype(F32)
        outs[0][...] = ((part * extra[1][...]) * (z * _sig(z))).astype(BF)

    (gated,) = _mm(n + "_grp", (nt, 4), [pooled, w["g_grp"], h, w["scale"].reshape(1, D_INNER)],
                   [pl.BlockSpec((tm, 512), lambda i, g: (i, g)), _grp_block(),
                    pl.BlockSpec((tm, 512), lambda i, g: (i, 4 + g)), pl.BlockSpec((1, 512), lambda i, g: (0, g))],
                   [_sds((s, D_INNER), BF)], [pl.BlockSpec((tm, 512), lambda i, g: (i, g))], NN, gate)
    y = _out_proj(n + "_out", gated, w["g_out"], 0, x, tm)
    return y, dict(x=x, xn=xn, h=h, pooled=pooled, gated=gated, w=w)


def _out_proj(name, gated, g1024, row_block, x, tm):
    s = x.shape[0]

    def epi(part, extra, outs, pids):
        outs[0][...] = part + extra[0][...]

    row = pl.BlockSpec((tm, D_MODEL), lambda i: (i, 0))
    return _mm(name, (s // tm,), [gated, g1024, x],
               [pl.BlockSpec((tm, D_INNER), lambda i: (i, 0)), pl.BlockSpec((N_DEV, 256, D_MODEL), lambda i: (0, row_block, 0)), row],
               [_sds((s, D_MODEL), F32)], [row], NN, epi)[0]


def _w_out_nt_block(row_block):
    return pl.BlockSpec((2, 256, D_MODEL), lambda j, i: (j, row_block, 0))


def _in_proj_bwd(name, dh, wbuf, w_index, n_k, x, g, dy, tm, dep=()):
    s = x.shape[0]
    tm = IN_PROJ_ROWS if s % IN_PROJ_ROWS == 0 else tm

    def epi(acc, extra, outs, pids):
        dx, dg = _rms_bwd(acc, extra[0][...], extra[1][...], extra[2][...])
        outs[0][...] = dx
        outs[1][...] = dx.astype(BF)
        _accumulate(outs[2], dg, pids[0])

    row = lambda i, k: (i, 0)
    return _mm(name, (s // tm, n_k), [dh, wbuf, x, g.reshape(1, D_MODEL), dy] + list(dep),
               [pl.BlockSpec((tm, 512), lambda i, k: (i, k)), pl.BlockSpec((None, D_MODEL, 512), w_index),
                pl.BlockSpec((tm, D_MODEL), row), pl.BlockSpec((1, D_MODEL), lambda i, k: (0, 0)), pl.BlockSpec((tm, D_MODEL), row)]
               + [_ANY] * len(dep),
               [_sds((s, D_MODEL), F32), _sds((s, D_MODEL), BF), _sds((1, D_MODEL), F32)],
               [pl.BlockSpec((tm, D_MODEL), row), pl.BlockSpec((tm, D_MODEL), row), pl.BlockSpec((1, D_MODEL), lambda i, k: (0, 0))],
               NT, epi, red=1, acc_shape=(tm, D_MODEL))


def _w_out_grad(name, gated, dyb):
    s = gated.shape[0]
    return _tn(name, gated, dyb, (D_INNER, D_MODEL), (512, D_MODEL), lambda i: (i, 0), 512, D_MODEL, (4,),
               lambda i: (0, i), lambda i: (0, 0))


def _pool_bwd(dy, dyb, l, w, sv, tm, dep, early=None):
    s = dy.shape[0]
    nt = s // tm
    n = f"pool{l}b"
    h, pooled = sv["h"], sv["pooled"]
    scale = w["scale"].reshape(1, D_INNER)

    def gate_bwd(part, extra, outs, pids):
        z, sc = extra[0][...].astype(F32), extra[1][...]
        wg = extra[3][...].reshape(POOL_GROUP, POOL_GROUP)
        mpv = _dot(extra[2][...], wg, NN)
        sz, dsz = _silu_and_grad(z)
        dm = part * sz
        dmp = (dm * sc).astype(BF)
        outs[0][...] = dmp
        outs[1][...] = (part * (mpv * sc) * dsz).astype(BF)
        _accumulate(outs[2], jnp.sum(dm * mpv, axis=0, keepdims=True), pids[1])
        outs[3][...] = _dot(dmp, wg, NT)

    tile = lambda j, i: (i, j)
    dmp, dz, dscale, dpool = _mm(
        n + "_out", (4, nt), [dyb, w["g_out"], h, scale, pooled, w["g_grp"]] + dep,
        [pl.BlockSpec((tm, D_MODEL), lambda j, i: (i, 0)), _w_out_nt_block(0),
         pl.BlockSpec((tm, 512), lambda j, i: (i, 4 + j)), pl.BlockSpec((1, 512), lambda j, i: (0, j)), pl.BlockSpec((tm, 512), tile),
         pl.BlockSpec((N_DEV, 64, POOL_GROUP), lambda j, i: (0, j, 0))] + [_ANY] * len(dep),
        [_sds((s, D_INNER), BF), _sds((s, 2 * D_INNER), BF), _sds((1, D_INNER), F32), _sds((s, D_INNER), F32)],
        [pl.BlockSpec((tm, 512), tile), pl.BlockSpec((tm, 512), lambda j, i: (i, 4 + j)), pl.BlockSpec((1, 512), lambda j, i: (0, j)),
         pl.BlockSpec((tm, 512), tile)],
        NT, gate_bwd)
    g_out = _w_out_grad(n + "_gout", sv["gated"], dyb).reshape(N_DEV, 256, D_MODEL)
    g_grp = _tn(n + "_ggrp", pooled, dmp, (N_DEV, 256, 512), (N_DEV, 64, 512), lambda g: (0, g, 0),
                512, 512, (4,), lambda g: (0, g), lambda g: (0, g))
    dep = early(dict(g_grp=g_grp, g_out=g_out)) if early is not None else ()
    dh = _pool_window_bwd(n + "_win", dpool, tm, dz)
    g_in = _tn(n + "_gin", sv["xn"], dh, (N_DEV, D_MODEL, 512), (None, D_MODEL, 512), lambda j: (j, 0, 0),
               D_MODEL, 512, (8,), lambda j: (0, 0), lambda j: (0, j), dep)
    dep = early(dict(g_in=g_in)) if early is not None else ()
    dx, dxb, dnorm = _in_proj_bwd(n + "_in", dh, w["g_in"], lambda i, k: (k, 0, 0), 8, sv["x"], w["norm"], dy, tm, dep)
    return dx, dxb, dict(g_in=g_in, g_grp=g_grp, g_out=g_out, norm=dnorm[0], scale=dscale[0])


def _conv_in_index(i, j):
    return (j // 2, 0, j % 2)


def _conv_fwd(x, w, tm):
    s = x.shape[0]
    nt = s // tm
    h, xn = _norm_in_proj("conv_in", x, w["norm"], w["g_in"], _conv_in_index, 16, tm)
    per = tm // CONV_HALO

    def body(b_ref, c_ref, h_ref, z_ref, cp_ref, hp_ref, w_ref, o_ref, e_ref):
        i = pl.program_id(0)
        ch = c_ref[...].astype(F32) * h_ref[...].astype(F32)
        e_ref[0:CONV_HALO, :] = jnp.where(i > 0, cp_ref[...].astype(F32) * hp_ref[...].astype(F32), 0.0)
        e_ref[CONV_HALO:, :] = ch
        co = (w_ref[2:3, :] * ch + w_ref[1:2, :] * e_ref[pl.ds(CONV_HALO - 1, tm), :]
              + w_ref[0:1, :] * e_ref[pl.ds(CONV_HALO - 2, tm), :])
        z = z_ref[...].astype(F32)
        o_ref[...] = ((b_ref[...].astype(F32) * co) * (z * _sig(z))).astype(BF)

    def col(q):
        return pl.BlockSpec((tm, 512), lambda i, j: (i, 4 * q + j))

    def prev(q):
        return pl.BlockSpec((CONV_HALO, 512), lambda i, j: (jnp.maximum(i * per - 1, 0), 4 * q + j))

    gated = pl.pallas_call(
        body, name="conv_mix", grid=(nt, 4),
        in_specs=[col(0), col(1), col(2), col(3), prev(1), prev(2), pl.BlockSpec((3, 512), lambda i, j: (0, j))],
        out_specs=pl.BlockSpec((tm, 512), lambda i, j: (i, j)), out_shape=_sds((s, D_INNER), BF),
        scratch_shapes=[pltpu.VMEM((CONV_HALO + tm, 512), F32)], compiler_params=_params())(h, h, h, h, h, h, w["conv_w"])
    w = dict(w, **w["rest"]([gated]))
    y = _out_proj("conv_out", gated, w["g_out"], 0, x, tm)
    return y, dict(x=x, xn=xn, h=h, gated=gated, w=w)


def _conv_bwd(dy, dyb, w, sv, tm, dep, early):
    s = dy.shape[0]
    nt = s // tm
    h = sv["h"]
    per = tm // CONV_HALO
    last = s // CONV_HALO - 1
    n_dep = len(dep)

    def body(dy_ref, dyn_ref, wo_ref, b_ref, c_ref, h_ref, z_ref, cp_ref, hp_ref, bn_ref, zn_ref, w_ref, *rest):
        dall_ref, dw_ref, e_ref, f_ref = rest[n_dep:]
        db_ref, dc_ref, dh_ref, dz_ref = (dall_ref.at[:, q * 512:(q + 1) * 512] for q in range(4))
        i = pl.program_id(1)
        wo = wo_ref[...].reshape(512, D_MODEL)
        dg_tile = _dot(dy_ref[...], wo, NT)
        dg_next = _dot(dyn_ref[...], wo, NT)
        w0, w1, w2 = w_ref[0:1, :], w_ref[1:2, :], w_ref[2:3, :]
        c, hh, b = c_ref[...].astype(F32), h_ref[...].astype(F32), b_ref[...].astype(F32)
        ch = c * hh
        e_ref[0:CONV_HALO, :] = jnp.where(i > 0, cp_ref[...].astype(F32) * hp_ref[...].astype(F32), 0.0)
        e_ref[CONV_HALO:, :] = ch
        ch1 = e_ref[pl.ds(CONV_HALO - 1, tm), :]
        ch2 = e_ref[pl.ds(CONV_HALO - 2, tm), :]
        co = w2 * ch + w1 * ch1 + w0 * ch2
        sz, dsz = _silu_and_grad(z_ref[...].astype(F32))
        dgv = dg_tile
        dyv = dgv * sz
        dz_ref[...] = (dgv * (b * co) * dsz).astype(BF)
        db_ref[...] = (dyv * co).astype(BF)
        dco = dyv * b
        zn = zn_ref[...].astype(F32)
        f_ref[0:tm, :] = dco
        f_ref[tm:, :] = jnp.where(i < nt - 1, dg_next * (zn * _sig(zn)) * bn_ref[...].astype(F32), 0.0)
        dch = w2 * dco + w1 * f_ref[pl.ds(1, tm), :] + w0 * f_ref[pl.ds(2, tm), :]
        dc_ref[...] = (dch * hh).astype(BF)
        dh_ref[...] = (dch * c).astype(BF)
        for tap, shifted in enumerate((ch2, ch1, ch)):
            _accumulate(dw_ref.at[tap:tap + 1, :], jnp.sum(dco * shifted, axis=0, keepdims=True), i)

    def col(q):
        return pl.BlockSpec((tm, 512), lambda j, i: (i, 4 * q + j))

    def prev(q):
        return pl.BlockSpec((CONV_HALO, 512), lambda j, i: (jnp.maximum(i * per - 1, 0), 4 * q + j))

    def nxt(q):
        return pl.BlockSpec((CONV_HALO, 512), lambda j, i: (jnp.minimum((i + 1) * per, last), 4 * q + j))

    wspec = pl.BlockSpec((3, 512), lambda j, i: (0, j))
    dy_tile = pl.BlockSpec((tm, D_MODEL), lambda j, i: (i, 0))
    dy_next = pl.BlockSpec((CONV_HALO, D_MODEL), lambda j, i: (jnp.minimum((i + 1) * per, last), 0))
    dh, dw = pl.pallas_call(
        body, name="convb_mix", grid=(4, nt),
        in_specs=[dy_tile, dy_next, _w_out_nt_block(0), col(0), col(1), col(2), col(3), prev(1), prev(2), nxt(0), nxt(3), wspec]
        + [_ANY] * n_dep,
        out_specs=[pl.BlockSpec((tm, D_INNER), lambda j, i: (i, j)), wspec],
        out_shape=[_sds((s, 4 * D_INNER), BF), _sds((3, D_INNER), F32)],
        scratch_shapes=[pltpu.VMEM((CONV_HALO + tm, 512), F32)] * 2, compiler_params=_params(),
    )(dyb, dyb, w["g_out"], h, h, h, h, h, h, h, h, w["conv_w"], *dep)

    def w_block(kp):
        k = 4 * (kp % 4) + kp // 4
        return (k // 2, 0, k % 2)

    g_in = _tn("convb_gin", sv["xn"], dh, (N_DEV, D_MODEL, D_MODEL), (None, D_MODEL, 512), lambda j: (j // 2, 0, j % 2),
               D_MODEL, 512, (16,), lambda j: (0, 0), lambda j: (0, 4 * (j % 4) + j // 4))
    g_out = _w_out_grad("convb_gout", sv["gated"], dyb).reshape(N_DEV, 256, D_MODEL)
    dep = early(dict(g_in=g_in, g_out=g_out))
    dx, dxb, dnorm = _in_proj_bwd("convb_in", dh, w["g_in"], lambda i, kp: w_block(kp), 16, sv["x"], w["norm"], dy, tm, dep)
    return dx, dxb, dict(g_in=g_in, g_out=g_out, norm=dnorm[0], conv_w=dw)


def _attn_tiles(s):
    t = min(ATTN_TILE, s)
    return t, s // t


def _causal_keep(t, keys_on_rows):
    r = lax.broadcasted_iota(jnp.int32, (t, t), 0)
    c = lax.broadcasted_iota(jnp.int32, (t, t), 1)
    return (r <= c) if keys_on_rows else (c <= r)


def _mla_fwd(x, w, rope, tm):
    s = x.shape[0]
    nt = s // tm
    cosf, sinf, perm = rope
    xn = _rms_fwd("mla_rms", x, w["norm"], tm)

    def in_body(xn_ref, wq_ref, wkv_ref, wkr_ref, wz_ref, gq_ref, gkv_ref, cos_ref, sin_ref, p_ref,
                ql_ref, kvl_ref, qn_ref, kvn_ref, krr_ref, z_ref):
        xv = xn_ref[...]
        ql = _dot(xv, wq_ref[...], NN)
        kvl = _dot(xv, wkv_ref[...], NN)
        ql_ref[...] = ql
        kvl_ref[...] = kvl
        rq = lax.rsqrt(jnp.mean(ql * ql, axis=-1, keepdims=True) + NORM_EPS)
        qn_ref[...] = ((ql * rq) * gq_ref[...]).astype(BF)
        rkv = lax.rsqrt(jnp.mean(kvl * kvl, axis=-1, keepdims=True) + NORM_EPS)
        kvn_ref[...] = ((kvl * rkv) * gkv_ref[...]).astype(BF)
        kr = _dot(xv, wkr_ref[...], NN)
        krr_ref[...] = _rope_fwd(kr, cos_ref[...], sin_ref[...], p_ref[...]).astype(BF)
        z_ref[...] = _dot(xv, wz_ref[...], NN).astype(BF)

    def full(a):
        return pl.BlockSpec(a.shape, lambda i: (0,) * a.ndim)

    def rows(c):
        return pl.BlockSpec((tm, c), lambda i: (i, 0))

    gq, gkv = w["q_norm"].reshape(1, Q_RANK), w["kv_norm"].reshape(1, KV_RANK)
    q_lat, kv_lat, qn, kvn, krr, z = pl.pallas_call(
        in_body, name="mla_in", grid=(nt,),
        in_specs=[rows(D_MODEL), full(w["w_q"]), full(w["w_kv"]), full(w["w_kr"]), full(w["w_z"]), full(gq), full(gkv),
                  rows(QK_ROPE), rows(QK_ROPE), full(perm)],
        out_specs=[rows(Q_RANK), rows(KV_RANK), rows(Q_RANK), rows(KV_RANK), rows(QK_ROPE), rows(D_INNER)],
        out_shape=[_sds((s, Q_RANK), F32), _sds((s, KV_RANK), F32), _sds((s, Q_RANK), BF), _sds((s, KV_RANK), BF),
                   _sds((s, QK_ROPE), BF), _sds((s, D_INNER), BF)],
        compiler_params=_params())(xn, w["w_q"], w["w_kv"], w["w_kr"], w["w_z"], gq, gkv, cosf, sinf, perm)

    def q_epi(part, extra, outs, pids):
        outs[0][:, 0:QK_NOPE] = (part[:, 0:QK_NOPE] * Q_PRESCALE).astype(BF)
        roped = _rope_fwd(part[:, QK_NOPE:QK_DIM], extra[0][...], extra[1][...], extra[2][...])
        outs[0][:, QK_NOPE:QK_DIM] = (roped * Q_PRESCALE).astype(BF)

    tp = IN_PROJ_ROWS if s % IN_PROJ_ROWS == 0 else tm
    rope_row = pl.BlockSpec((tp, QK_ROPE), lambda h, i: (i, 0))
    (q,) = _mm("mla_qup", (N_HEADS, s // tp), [qn, w["w_qh"], cosf, sinf, perm],
               [pl.BlockSpec((tp, Q_RANK), lambda h, i: (i, 0)), pl.BlockSpec((None, Q_RANK, QK_DIM), lambda h, i: (h, 0, 0)),
                rope_row, rope_row, pl.BlockSpec((QK_ROPE, QK_ROPE), lambda h, i: (0, 0))],
               [_sds((N_HEADS, s, QK_DIM), BF)], [pl.BlockSpec((None, tp, QK_DIM), lambda h, i: (h, i, 0))], NN, q_epi)

    def kv_epi(part, extra, outs, pids):
        outs[0][:, 0:QK_NOPE] = part[:, 0:QK_NOPE].astype(BF)
        outs[0][:, QK_NOPE:QK_DIM] = extra[0][...]
        outs[1][...] = part[:, QK_NOPE:].astype(BF)

    k, v = _mm("mla_kvup", (N_HEADS, s // tp), [kvn, w["g512"], krr],
               [pl.BlockSpec((tp, KV_RANK), lambda h, i: (i, 0)),
                pl.BlockSpec((None, KV_RANK, 256), lambda h, i: (h // 2, 0, h % 2)), rope_row],
               [_sds((N_HEADS, s, QK_DIM), BF), _sds((N_HEADS, s, V_DIM), BF)],
               [pl.BlockSpec((None, tp, QK_DIM), lambda h, i: (h, i, 0)), pl.BlockSpec((None, tp, V_DIM), lambda h, i: (h, i, 0))],
               NN, kv_epi)

    t, nq = _attn_tiles(s)

    def attn_body(q_ref, k_ref, v_ref, z_ref, o_ref, g_ref, lse_ref):
        i = pl.program_id(1)

        def block(j, carry, masked):
            start = pl.multiple_of(j * t, t)
            out = []
            for hh, (m, lsum, acc) in enumerate(carry):
                sc = _dot(q_ref[hh], k_ref[hh, pl.ds(start, t), :], NT)
                if masked:
                    sc = jnp.where(_causal_keep(t, False), sc, NEG_BIG)
                mn = jnp.maximum(m, jnp.max(sc, axis=-1, keepdims=True))
                alpha = jnp.exp2(m - mn)
                p = jnp.exp2(sc - mn)
                lsum = alpha * lsum + jnp.sum(p, axis=-1, keepdims=True)
                acc = alpha * acc + _dot(p.astype(BF), v_ref[hh, pl.ds(start, t), :], NN)
                out.append((mn, lsum, acc))
            return tuple(out)

        init = ((jnp.full((t, 1), NEG_BIG, F32), jnp.zeros((t, 1), F32), jnp.zeros((t, V_DIM), F32)),) * ATTN_HEADS_PER_STEP
        carry = lax.fori_loop(0, i, lambda j, c: block(j, c, False), init)
        for hh, (m, lsum, acc) in enumerate(block(i, carry, True)):
            cols = slice(hh * V_DIM, (hh + 1) * V_DIM)
            o = acc / lsum
            z = z_ref[:, cols].astype(F32)
            o_ref[:, cols] = o
            g_ref[:, cols] = (o * (z * _sig(z))).astype(BF)
            lse_ref[hh] = _to_row(m + jnp.log(lsum) * LOG2_E)

    hp = ATTN_HEADS_PER_STEP
    head_col = pl.BlockSpec((t, hp * V_DIM), lambda h, i: (i, h))
    o, gated, lse = pl.pallas_call(
        attn_body, name="mla_attn", grid=(N_HEADS // hp, nq),
        in_specs=[pl.BlockSpec((hp, t, QK_DIM), lambda h, i: (h, i, 0)), pl.BlockSpec((hp, s, QK_DIM), lambda h, i: (h, 0, 0)),
                  pl.BlockSpec((hp, s, V_DIM), lambda h, i: (h, 0, 0)), head_col],
        out_specs=[head_col, head_col, pl.BlockSpec((hp, None, 1, t), lambda h, i: (h, i, 0, 0))],
        out_shape=[_sds((s, D_INNER), F32), _sds((s, D_INNER), BF), _sds((N_HEADS, nq, 1, t), F32)],
        compiler_params=_params())(q, k, v, z)
    y = _out_proj("mla_out", gated, w["g1024"], 0, x, tm)
    return y, dict(x=x, xn=xn, q_lat=q_lat, kv_lat=kv_lat, qn=qn, kvn=kvn, z=z, q=q, k=k, v=v, o=o, lse=lse, gated=gated)


def _mla_bwd(dy, dyb, w, sv, rope, tm, dep):
    s = dy.shape[0]
    nt = s // tm
    cosf, sinf, perm = rope
    t, nq = _attn_tiles(s)
    assert t == tm, "the row statistics of the backward are laid out per attention tile"
    q, k, v, lse = sv["q"], sv["k"], sv["v"], sv["lse"]

    def gate_bwd(part, extra, outs, pids):
        z, o = extra[0][...].astype(F32), extra[1][...]
        sz, dsz = _silu_and_grad(z)
        do = part * sz
        outs[0][...] = do.astype(BF)
        outs[1][...] = (part * o * dsz).astype(BF)
        prod = do * o
        for hh in range(4):
            outs[2][hh] = _to_row(jnp.sum(prod[:, hh * V_DIM:(hh + 1) * V_DIM], axis=-1, keepdims=True))

    tile = lambda j, i: (i, j)
    dob, dz, delta = _mm(
        "mlab_out", (4, nt), [dyb, w["g1024"], sv["z"], sv["o"]] + dep,
        [pl.BlockSpec((tm, D_MODEL), lambda j, i: (i, 0)), _w_out_nt_block(0),
         pl.BlockSpec((tm, 512), tile), pl.BlockSpec((tm, 512), tile)] + [_ANY] * len(dep),
        [_sds((s, D_INNER), BF), _sds((s, D_INNER), BF), _sds((N_HEADS, nt, 1, tm), F32)],
        [pl.BlockSpec((tm, 512), tile), pl.BlockSpec((tm, 512), tile), pl.BlockSpec((4, None, 1, tm), lambda j, i: (j, i, 0, 0))],
        NT, gate_bwd)

    hp = ATTN_HEADS_PER_STEP

    def attn_bwd_body(k_ref, v_ref, q_ref, do_ref, lse_ref, dl_ref, cos_ref, sin_ref, p_ref, dkv_ref, dkr_ref, dq_ref, dq_acc):
        j = pl.program_id(1)

        @pl.when(j == 0)
        def _():
            dq_acc[...] = jnp.zeros(dq_acc.shape, F32)

        def block(i, carry, masked):
            rows = pl.ds(pl.multiple_of(i * t, t), t)
            out = []
            for hh, (dk, dv) in enumerate(carry):
                kb, vb = k_ref[hh], v_ref[hh]
                qb, dob_ = q_ref[hh, rows, :], do_ref[rows, hh * V_DIM:(hh + 1) * V_DIM]
                st = _dot(kb, qb, NT)
                if masked:
                    st = jnp.where(_causal_keep(t, True), st, NEG_BIG)
                pt = jnp.exp2(st - lse_ref[hh, i])
                dv = dv + _dot(pt.astype(BF), dob_, NN)
                dst = (pt * (_dot(vb, dob_, NT) - dl_ref[hh, i])).astype(BF)
                dk = dk + _dot(dst, qb, NN)
                dq_acc[hh, rows, :] += _dot(dst, kb, TN)
                out.append((dk, dv))
            return tuple(out)

        init = ((jnp.zeros((t, QK_DIM), F32), jnp.zeros((t, V_DIM), F32)),) * hp
        carry = block(j, init, True)
        carry = lax.fori_loop(j + 1, nq, lambda i, c: block(i, c, False), carry)
        for hh, (dk, dv) in enumerate(carry):
            dk = dk * LN_2
            base = hh * 2 * V_DIM
            dkv_ref[:, base:base + QK_NOPE] = dk[:, 0:QK_NOPE].astype(BF)
            dkv_ref[:, base + QK_NOPE:base + 2 * V_DIM] = dv.astype(BF)
            dkr_ref[hh] = dk[:, QK_NOPE:]

        @pl.when(j == nq - 1)
        def _():
            for hh in range(hp):
                for c in range(nq):
                    rows = slice(c * t, (c + 1) * t)
                    dq = dq_acc[hh, rows, :] * ATTN_SCALE
                    dq_ref[hh, rows, 0:QK_NOPE] = dq[:, 0:QK_NOPE].astype(BF)
                    dq_ref[hh, rows, QK_NOPE:] = _rope_bwd(dq[:, QK_NOPE:], cos_ref[rows, :], sin_ref[rows, :], p_ref[...]).astype(BF)

    row_stats = pl.BlockSpec((hp, nq, 1, t), lambda h, j: (h, 0, 0, 0))
    seq_rope = pl.BlockSpec((s, QK_ROPE), lambda h, j: (0, 0))
    head_seq = pl.BlockSpec((hp, s, QK_DIM), lambda h, j: (h, 0, 0))
    dkv, dkr_h, dq = pl.pallas_call(
        attn_bwd_body, name="mlab_attn", grid=(N_HEADS // hp, nq),
        in_specs=[pl.BlockSpec((hp, t, QK_DIM), lambda h, j: (h, j, 0)), pl.BlockSpec((hp, t, V_DIM), lambda h, j: (h, j, 0)),
                  head_seq, pl.BlockSpec((s, hp * V_DIM), lambda h, j: (0, h)), row_stats, row_stats, seq_rope, seq_rope,
                  pl.BlockSpec((QK_ROPE, QK_ROPE), lambda h, j: (0, 0))],
        out_specs=[pl.BlockSpec((t, hp * 2 * V_DIM), lambda h, j: (j, h)), pl.BlockSpec((hp, t, QK_ROPE), lambda h, j: (h, j, 0)), head_seq],
        out_shape=[_sds((s, N_HEADS * 2 * V_DIM), BF), _sds((N_HEADS, s, QK_ROPE), F32), _sds((N_HEADS, s, QK_DIM), BF)],
        scratch_shapes=[pltpu.VMEM((hp, s, QK_DIM), F32)],
        compiler_params=_params())(k, v, q, dob, lse, delta, cosf, sinf, perm)

    def dkr_body(d_ref, cos_ref, sin_ref, p_ref, o_ref):
        tot = d_ref[0]
        for hh in range(1, N_HEADS):
            tot = tot + d_ref[hh]
        o_ref[...] = _rope_bwd(tot, cos_ref[...], sin_ref[...], p_ref[...]).astype(BF)

    r64 = pl.BlockSpec((tm, QK_ROPE), lambda i: (i, 0))
    dkr = pl.pallas_call(
        dkr_body, name="mlab_dkr", grid=(nt,),
        in_specs=[pl.BlockSpec((N_HEADS, tm, QK_ROPE), lambda i: (0, i, 0)), r64, r64, pl.BlockSpec((QK_ROPE, QK_ROPE), lambda i: (0, 0))],
        out_specs=r64, out_shape=_sds((s, QK_ROPE), BF), compiler_params=_params())(dkr_h, cosf, sinf, perm)

    def lat_epi(acc, extra, outs, pids):
        dx, dg = _rms_bwd(acc, extra[0][...], extra[1][...], None)
        outs[0][...] = dx.astype(BF)
        _accumulate(outs[1], dg, pids[0])

    tp = IN_PROJ_ROWS if s % IN_PROJ_ROWS == 0 else tm

    def lat_bwd(name, a, a_spec, b, b_spec, n_k, lat, g, rank):
        row = lambda i, k: (i, 0)
        one = lambda i, k: (0, 0)
        return _mm(name, (s // tp, n_k), [a, b, lat, g.reshape(1, rank)],
                   [a_spec, b_spec, pl.BlockSpec((tp, rank), row), pl.BlockSpec((1, rank), one)],
                   [_sds((s, rank), BF), _sds((1, rank), F32)], [pl.BlockSpec((tp, rank), row), pl.BlockSpec((1, rank), one)],
                   NT, lat_epi, red=1, acc_shape=(tp, rank))

    d_ql, g_qnorm = lat_bwd("mlab_qup", dq, pl.BlockSpec((None, tp, QK_DIM), lambda i, h: (h, i, 0)),
                            w["w_qh"], pl.BlockSpec((None, Q_RANK, QK_DIM), lambda i, h: (h, 0, 0)), N_HEADS,
                            sv["q_lat"], w["q_norm"], Q_RANK)
    d_kvl, g_kvnorm = lat_bwd("mlab_kvup", dkv, pl.BlockSpec((tp, 512), lambda i, kk: (i, kk)),
                              w["g512"], pl.BlockSpec((None, KV_RANK, 512), lambda i, kk: (kk, 0, 0)), N_DEV,
                              sv["kv_lat"], w["kv_norm"], KV_RANK)

    def in_bwd(dql_ref, dkvl_ref, dkr_ref, dz_ref, wq_ref, wkv_ref, wkr_ref, wz_ref, x_ref, g_ref, dy_ref, dx_ref, dxb_ref, dg_ref):
        acc = (_dot(dql_ref[...], wq_ref[...], NT) + _dot(dkvl_ref[...], wkv_ref[...], NT)
               + _dot(dkr_ref[...], wkr_ref[...], NT) + _dot(dz_ref[...], wz_ref[...], NT))
        dx, dg = _rms_bwd(acc, x_ref[...], g_ref[...], dy_ref[...])
        dx_ref[...] = dx
        dxb_ref[...] = dx.astype(BF)
        _accumulate(dg_ref, dg, pl.program_id(0))

    def full(a):
        return pl.BlockSpec(a.shape, lambda i: (0,) * a.ndim)

    def rows(c):
        return pl.BlockSpec((tm, c), lambda i: (i, 0))

    gm = w["norm"].reshape(1, D_MODEL)
    dx, dxb, g_norm = pl.pallas_call(
        in_bwd, name="mlab_in", grid=(nt,),
        in_specs=[rows(Q_RANK), rows(KV_RANK), rows(QK_ROPE), rows(D_INNER), full(w["w_q"]), full(w["w_kv"]), full(w["w_kr"]),
                  full(w["w_z"]), rows(D_MODEL), full(gm), rows(D_MODEL)],
        out_specs=[rows(D_MODEL), rows(D_MODEL), full(gm)],
        out_shape=[_sds((s, D_MODEL), F32), _sds((s, D_MODEL), BF), _sds((1, D_MODEL), F32)],
        compiler_params=_params())(d_ql, d_kvl, dkr, dz, w["w_q"], w["w_kv"], w["w_kr"], w["w_z"], sv["x"], gm, dy)

    xn = sv["xn"]
    one = lambda j: (0, 0)
    g_q = _tn("mlab_gq", xn, d_ql, (D_MODEL, Q_RANK), (D_MODEL, Q_RANK), one, D_MODEL, Q_RANK, (1,), one, one)
    g_kv = _tn("mlab_gkv", xn, d_kvl, (D_MODEL, KV_RANK), (D_MODEL, KV_RANK), one, D_MODEL, KV_RANK, (1,), one, one)
    g_kr = _tn("mlab_gkr", xn, dkr, (D_MODEL, QK_ROPE), (D_MODEL, QK_ROPE), one, D_MODEL, QK_ROPE, (1,), one, one)
    g_z = _tn("mlab_gz", xn, dz, (D_MODEL, D_INNER), (D_MODEL, 512), lambda j: (0, j), D_MODEL, 512, (4,), one, lambda j: (0, j))
    g_in = jnp.concatenate([g_q, g_kv, g_kr, g_z], axis=1)
    g_qh = _tn("mlab_gqup", sv["qn"], dq, (N_HEADS, Q_RANK, QK_DIM), (None, Q_RANK, QK_DIM), lambda h: (h, 0, 0),
               Q_RANK, QK_DIM, (N_HEADS,), lambda h: (0, 0), lambda h: (h, 0, 0))
    g_kvup = _tn("mlab_gkvup", sv["kvn"], dkv, (N_DEV, KV_RANK, 512), (None, KV_RANK, 512), lambda j: (j, 0, 0),
                 KV_RANK, 512, (N_DEV,), lambda j: (0, 0), lambda j: (0, j))
    g_out = _w_out_grad("mlab_gout", sv["gated"], dyb)
    s384 = g_qh.reshape(N_DEV, 2, Q_RANK, QK_DIM).transpose(0, 2, 1, 3).reshape(N_DEV, Q_RANK, 2 * QK_DIM)
    s344 = g_in.reshape(D_MODEL, N_DEV, 344).transpose(1, 0, 2)
    return dx, dxb, dict(s344=s344, s384=s384, s512=g_kvup, s1024=g_out.reshape(N_DEV, 256, D_MODEL),
                         norm=g_norm[0], q_norm=g_qnorm[0], kv_norm=g_kvnorm[0])


def _loss_head(x, g, target, tm):
    s, d = x.shape

    def body(x_ref, g_ref, t_ref, dx_ref, dxb_ref, dg_ref, loss_ref):
        i = pl.program_id(0)
        xv, gv = x_ref[...], g_ref[...]
        r = lax.rsqrt(jnp.mean(xv * xv, axis=-1, keepdims=True) + NORM_EPS)
        err = (xv * r) * gv - t_ref[...]
        part = 0.5 * jnp.sum(jnp.mean(err * err, axis=-1, keepdims=True), axis=0, keepdims=True)
        dx, dg = _rms_bwd(err * (1.0 / d), xv, gv, None)
        dx_ref[...] = dx
        dxb_ref[...] = dx.astype(BF)
        _accumulate(dg_ref, dg, i)
        _accumulate(loss_ref, jnp.broadcast_to(part, loss_ref.shape), i)

    row = pl.BlockSpec((tm, d), lambda i: (i, 0))
    one = pl.BlockSpec((1, d), lambda i: (0, 0))
    return pl.pallas_call(
        body, name="loss_head", grid=(s // tm,), in_specs=[row, one, row],
        out_specs=[row, row, one, pl.BlockSpec((8, 128), lambda i: (0, 0))],
        out_shape=[_sds((s, d), F32), _sds((s, d), BF), _sds((1, d), F32), _sds((8, 128), F32)],
        compiler_params=_params())(x, g.reshape(1, d), target)


def _rope_tables(pos):
    inv_freq = ROPE_BASE ** (-jnp.arange(0, QK_ROPE, 2, dtype=F32) / QK_ROPE)
    ang = pos.astype(F32)[:, None] * inv_freq
    cos, sin = jnp.cos(ang), jnp.sin(ang)
    idx = jnp.arange(QK_ROPE)
    perm = (idx[:, None] == (idx[None, :] + QK_ROPE // 2) % QK_ROPE).astype(F32)
    return jnp.concatenate([cos, cos], axis=1), jnp.concatenate([-sin, sin], axis=1), perm


def _local_step(x, pos, target, final_norm, get_w, put_g):
    s = x.shape[0]
    tm = min(512, s)
    rope = _rope_tables(pos)
    w0 = get_w(0, [])
    x1, sv0 = _pool_fwd(x, 0, w0, tm)
    w1 = get_w(1, [x1])
    x2, sv1 = _conv_fwd(x1, w1, tm)
    w2 = get_w(2, [x2])
    x3, sv2 = _mla_fwd(x2, w2, rope, tm)
    w3 = get_w(3, [x3])
    x4, sv3 = _pool_fwd(x3, 1, w3, tm)
    d4, d4b, g_final, loss = _loss_head(x4, final_norm, target, tm)
    d3, d3b, gp1 = _pool_bwd(d4, d4b, 1, sv3["w"], sv3, tm, [])
    dep = put_g(3, gp1)
    d2, d2b, gm = _mla_bwd(d3, d3b, w2, sv2, rope, tm, dep)
    dep = put_g(2, gm)
    sent = {}

    def send_conv(part):
        sent["dep"] = put_g(1, part)
        return sent["dep"]

    d1, d1b, gc = _conv_bwd(d2, d2b, sv1["w"], sv1, tm, dep, send_conv)
    d0, _, gp0 = _pool_bwd(d1, d1b, 0, sv0["w"], sv0, tm, sent["dep"], early=lambda part: put_g(0, part))
    put_g(4, {0: dict(gp0, final_norm=g_final[0]), 1: gc, 2: gm, 3: gp1})
    return loss[0, 0], d0


def _pack_groups(p):
    bf = lambda a: a.astype(BF)
    grp = lambda l: bf(p["pool_w_grp"][l].reshape(4 * 64, POOL_GROUP))
    return [[bf(p["pool_w_in"][0]), _pack_small(p, SMALL_ROWS_AG)],
            [grp(0), bf(p["pool_w_out"][0])],
            [bf(p["conv_w_in"][0])],
            [bf(p["conv_w_out"][0])],
            [bf(p[k][0]) for k in ("mla_w_in", "mla_w_q_up", "mla_w_kv_up", "mla_w_out")],
            [bf(p["pool_w_in"][1]), grp(1), bf(p["pool_w_out"][1])]]


_SMALL_SHARDED = ("pool_norm", "pool_scale", "mla_norm", "mla_q_norm", "mla_kv_norm", "conv_w")
_SMALL_REPLICATED = ("conv_norm", "final_norm")


def _pack_small(p, rows, with_replicated=False):
    parts = [p[k].reshape(-1) for k in _SMALL_SHARDED]
    if with_replicated:
        parts += [p[k].reshape(-1) for k in _SMALL_REPLICATED]
    flat = jnp.concatenate(parts)
    return jnp.pad(flat, (0, rows * 128 - flat.shape[0])).reshape(rows, 128)


_SMALL_SHARD_SHAPES = dict(pool_norm=(2, 128), pool_scale=(2, 256), mla_norm=(1, 128), mla_q_norm=(1, 48),
                           mla_kv_norm=(1, 32), conv_w=(1, 3, 256), conv_norm=(1, 1024), final_norm=(1024,))


def _unpack_small(buf, with_replicated=False):
    flat = buf.reshape(-1)
    out, off = {}, 0
    for k in _SMALL_SHARDED + (_SMALL_REPLICATED if with_replicated else ()):
        shp = _SMALL_SHARD_SHAPES[k]
        n = 1
        for d in shp:
            n *= d
        out[k] = flat[off:off + n].reshape(shp)
        off += n
    return out


def _small_views(gsmall):
    flat = gsmall.reshape(N_DEV, -1)

    def cols(off, rows, width):
        return flat[:, off:off + rows * width].reshape(N_DEV, rows, width).transpose(1, 0, 2).reshape(rows, N_DEV * width)

    return dict(pool_norm=cols(0, 2, 128), pool_scale=cols(256, 2, 256), mla_norm=cols(768, 1, 128)[0],
                q_norm=cols(896, 1, 48)[0], kv_norm=cols(944, 1, 32)[0], conv_w=cols(976, 3, 256))


def _layer_weights(layer, bufs, small, conv_norm):
    if layer in (0, 3):
        l = 0 if layer == 0 else 1
        return dict(g_in=bufs[0], norm=small["pool_norm"][l], scale=small["pool_scale"][l])
    if layer == 1:
        return dict(g_in=bufs[0], norm=conv_norm.reshape(D_MODEL), conv_w=small["conv_w"])
    g344, g384, g512, g1024 = bufs
    w_in = g344.transpose(1, 0, 2).reshape(D_MODEL, N_DEV * 344)
    return dict(
        g512=g512, g1024=g1024,
        w_q=w_in[:, :Q_RANK], w_kv=w_in[:, Q_RANK:Q_RANK + KV_RANK],
        w_kr=w_in[:, Q_RANK + KV_RANK:Q_RANK + KV_RANK + QK_ROPE], w_z=w_in[:, Q_RANK + KV_RANK + QK_ROPE:],
        w_qh=g384.reshape(N_DEV, Q_RANK, 2, QK_DIM).transpose(0, 2, 1, 3).reshape(N_HEADS, Q_RANK, QK_DIM),
        norm=small["mla_norm"], q_norm=small["q_norm"], kv_norm=small["kv_norm"])


_GRAD_KEYS = {0: ("g_in", "g_grp", "g_out"), 3: ("g_in", "g_grp", "g_out"), 1: ("g_in", "g_out"), 2: ("s344", "s384", "s512", "s1024")}
_GRAD_PARAM = {0: dict(g_in="pool_w_in", g_grp="pool_w_grp", g_out="pool_w_out"), 1: dict(g_in="conv_w_in", g_out="conv_w_out"),
               2: dict(s344="mla_w_in", s384="mla_w_q_up", s512="mla_w_kv_up", s1024="mla_w_out")}
_GRAD_PARAM[3] = _GRAD_PARAM[0]


def _grad_group(layer, g):
    keys = tuple(k for k in _GRAD_KEYS[layer] if k in g)
    return keys, [g[k] for k in keys]


def _pack_small_grads(g):
    def split(a, rows, width):
        return a.reshape(rows, N_DEV, width).transpose(1, 0, 2).reshape(N_DEV, rows * width)

    rep = lambda a: jnp.broadcast_to(a.reshape(1, -1), (N_DEV, a.size))
    flat = jnp.concatenate([
        split(jnp.stack([g[0]["norm"], g[3]["norm"]]), 2, 128), split(jnp.stack([g[0]["scale"], g[3]["scale"]]), 2, 256),
        split(g[2]["norm"], 1, 128), split(g[2]["q_norm"], 1, 48), split(g[2]["kv_norm"], 1, 32), split(g[1]["conv_w"], 3, 256),
        rep(g[1]["norm"]), rep(g[0]["final_norm"])], axis=1)
    return jnp.pad(flat, ((0, 0), (0, SMALL_ROWS_RS * 128 - flat.shape[1]))).reshape(N_DEV, SMALL_ROWS_RS, 128)


def _peers(x, y, c):
    for k in range(1, N_DEV):
        px = 1 - x if k & 4 else x
        py = 1 - y if k & 2 else y
        pc = 1 - c if k & 1 else c
        yield k - 1, (px, py, pc), 4 * px + 2 * py + pc


_SIBLING = (0,)
_ICI_DIRECT = (1, 3, 5)


def _remote_copies(srcs, lands, send_sems, recv_sems, gather, ks=None):
    x, y, c = lax.axis_index("x"), lax.axis_index("y"), lax.axis_index("c")
    me = 4 * x + 2 * y + c
    copies = []
    for k, peer, pidx in _peers(x, y, c):
        if ks is not None and k not in ks:
            continue
        for a, (src, land) in enumerate(zip(srcs, lands)):
            copies.append(pltpu.make_async_remote_copy(
                src_ref=src if gather else src.at[pidx], dst_ref=land.at[me],
                send_sem=send_sems.at[a * (N_DEV - 1) + k], recv_sem=recv_sems.at[a * (N_DEV - 1) + k],
                device_id=peer, device_id_type=pl.DeviceIdType.MESH))
    return copies


def _relay_copies(lands, send_sems, recv_sems):
    x, y, c = lax.axis_index("x"), lax.axis_index("y"), lax.axis_index("c")
    copies = []
    for j, k in enumerate(_ICI_DIRECT):
        px = 1 - x if (k + 1) & 4 else x
        py = 1 - y if (k + 1) & 2 else y
        slot = 4 * px + 2 * py + c
        for a, land in enumerate(lands):
            copies.append(pltpu.make_async_remote_copy(
                src_ref=land.at[slot], dst_ref=land.at[slot],
                send_sem=send_sems.at[a * len(_ICI_DIRECT) + j], recv_sem=recv_sems.at[a * len(_ICI_DIRECT) + j],
                device_id=(x, y, 1 - c), device_id_type=pl.DeviceIdType.MESH))
    return copies


_HBM = pl.BlockSpec(memory_space=pltpu.HBM)
_SEM = pl.BlockSpec(memory_space=pltpu.SEMAPHORE)
_EFFECT = pltpu.SideEffectType.DATAFLOW_SIDE_EFFECTING


def _own_slabs(name, arrays, gather, dep):
    n, nd = len(arrays), len(dep)
    me = (4 * lax.axis_index("x") + 2 * lax.axis_index("y") + lax.axis_index("c")).astype(jnp.int32).reshape(1)

    def body(me_ref, *refs):
        for a in range(n):
            refs[n + nd + a][...] = refs[a][...]

    def slab(shape):
        return pl.BlockSpec((None,) + tuple(shape), lambda i, me_ref: (me_ref[0],) + (0,) * len(shape))

    def whole(shape):
        return pl.BlockSpec(tuple(shape), lambda i, me_ref: (0,) * len(shape))

    outs = [_sds(((N_DEV,) + a.shape) if gather else a.shape, a.dtype) for a in arrays]
    grid_spec = pltpu.PrefetchScalarGridSpec(
        num_scalar_prefetch=1, grid=(1,),
        in_specs=[whole(a.shape) if gather else slab(a.shape[1:]) for a in arrays] + [_ANY] * nd,
        out_specs=[slab(o.shape[1:]) for o in outs])
    return pl.pallas_call(body, name=name, grid_spec=grid_spec, out_shape=outs, compiler_params=_params())(me, *arrays, *dep)


def _exchange_start(name, arrays, lands, gather, ks=None):
    n = len(arrays)

    def body(*refs):
        srcs, lnds, send_sems, recv_sems, token = refs[:n], refs[n:2 * n], refs[2 * n], refs[2 * n + 1], refs[-1]
        for cp in _remote_copies(srcs, lnds, send_sems, recv_sems, gather, ks):
            cp.start()
        token[...] = jnp.zeros(token.shape, F32)

    sems = pltpu.SemaphoreType.DMA((n * (N_DEV - 1),))
    thru = [pltpu.HBM(a.shape, a.dtype) for a in list(arrays) + list(lands)]
    res = pl.pallas_call(
        body, name=name, in_specs=[_HBM] * (2 * n),
        out_specs=[_SEM, _SEM] + [_HBM] * (2 * n) + [pl.BlockSpec(memory_space=pltpu.VMEM)],
        out_shape=[sems, sems] + thru + [_sds((8, 128), F32)],
        input_output_aliases={i: 2 + i for i in range(2 * n)},
        compiler_params=pltpu.CompilerParams(has_side_effects=_EFFECT),
    )(*[pltpu.with_memory_space_constraint(a, pltpu.HBM) for a in list(arrays) + list(lands)])
    return res[0], res[1], list(res[2:2 + n]), list(res[2 + n:2 + 2 * n]), res[-1]


def _exchange_wait(name, send_sems, recv_sems, arrays, lands, after, gather):
    n = len(arrays)
    n_after = len(after)

    def body(*refs):
        srcs, lnds = refs[:n], refs[n:2 * n]
        copies = _remote_copies(srcs, lnds, refs[2 * n], refs[2 * n + 1], gather)
        for cp in copies:
            cp.wait_send()
        for cp in copies:
            cp.wait_recv()

    thru = [pltpu.HBM(a.shape, a.dtype) for a in list(arrays) + list(lands)]
    res = pl.pallas_call(
        body, name=name, in_specs=[_HBM] * (2 * n) + [_SEM, _SEM] + [pl.BlockSpec(memory_space=pl.ANY)] * n_after,
        out_specs=[_HBM] * (2 * n), out_shape=thru, input_output_aliases={i: i for i in range(2 * n)},
        compiler_params=pltpu.CompilerParams(has_side_effects=_EFFECT),
    )(*arrays, *lands, send_sems, recv_sems, *after)
    return list(res[n:])


def _gather_relay(name, send_sems, recv_sems, arrays, lands, after):
    n, n_after = len(arrays), len(after)

    def body(*refs):
        srcs, lnds, first_send, first_recv = refs[:n], refs[n:2 * n], refs[2 * n], refs[2 * n + 1]
        send2, recv2, token = refs[2 * n + 2 + n_after], refs[2 * n + 3 + n_after], refs[-1]
        for cp in _remote_copies(srcs, lnds, first_send, first_recv, True, _ICI_DIRECT):
            cp.wait_recv()
        for cp in _relay_copies(lnds, send2, recv2):
            cp.start()
        token[...] = jnp.zeros(token.shape, F32)

    sems = pltpu.SemaphoreType.DMA((n * len(_ICI_DIRECT),))
    thru = [pltpu.HBM(a.shape, a.dtype) for a in list(arrays) + list(lands)]
    res = pl.pallas_call(
        body, name=name, in_specs=[_HBM] * (2 * n) + [_SEM, _SEM] + [_ANY] * n_after,
        out_specs=[_SEM, _SEM] + [_HBM] * (2 * n) + [pl.BlockSpec(memory_space=pltpu.VMEM)],
        out_shape=[sems, sems] + thru + [_sds((8, 128), F32)],
        input_output_aliases={i: 2 + i for i in range(2 * n)},
        compiler_params=pltpu.CompilerParams(has_side_effects=_EFFECT),
    )(*arrays, *lands, send_sems, recv_sems, *after)
    return res[0], res[1], list(res[2:2 + n]), list(res[2 + n:2 + 2 * n]), res[-1]


def _gather_wait2(name, send_sems, recv_sems, send2, recv2, arrays, lands, after):
    n, n_after = len(arrays), len(after)

    def body(*refs):
        srcs, lnds = refs[:n], refs[n:2 * n]
        s1, r1, s2, r2 = refs[2 * n:2 * n + 4]
        for cp in _remote_copies(srcs, lnds, s1, r1, True, _SIBLING + _ICI_DIRECT):
            cp.wait_send()
        for cp in _remote_copies(srcs, lnds, s1, r1, True, _SIBLING):
            cp.wait_recv()
        relay = _relay_copies(lnds, s2, r2)
        for cp in relay:
            cp.wait_send()
        for cp in relay:
            cp.wait_recv()

    thru = [pltpu.HBM(a.shape, a.dtype) for a in list(arrays) + list(lands)]
    res = pl.pallas_call(
        body, name=name, in_specs=[_HBM] * (2 * n) + [_SEM] * 4 + [_ANY] * n_after,
        out_specs=[_HBM] * (2 * n), out_shape=thru, input_output_aliases={i: i for i in range(2 * n)},
        compiler_params=pltpu.CompilerParams(has_side_effects=_EFFECT),
    )(*arrays, *lands, send_sems, recv_sems, send2, recv2, *after)
    return list(res[n:])


def _adamw_math(g, w, m, v):
    m = ADAM_B1 * m + (1.0 - ADAM_B1) * g
    v = ADAM_B2 * v + (1.0 - ADAM_B2) * (g * g)
    m_hat = m / (1.0 - ADAM_B1 ** ADAM_STEP)
    v_hat = v / (1.0 - ADAM_B2 ** ADAM_STEP)
    delta = -ADAM_LR * (m_hat / (jnp.sqrt(v_hat) + ADAM_EPS) + ADAM_WD * w)
    return delta, m, v


def _sum_adamw(name, recv, row_off, w, m, v, tr, layer=0):
    width = recv.shape[-1]
    w2, m2, v2 = (a.reshape(a.shape[0], -1, width) for a in (w, m, v))
    rows = w2.shape[1]
    base = row_off // tr

    def body(r_ref, w_ref, m_ref, v_ref, g_ref, d_ref, mo_ref, vo_ref):
        g = r_ref[0].astype(F32)
        for src in range(1, N_DEV):
            g = g + r_ref[src].astype(F32)
        delta, mn, vn = _adamw_math(g, w_ref[...], m_ref[...], v_ref[...])
        g_ref[...] = g
        d_ref[...] = delta
        mo_ref[...] = mn
        vo_ref[...] = vn

    blk = pl.BlockSpec((tr, width), lambda i: (i, 0))
    wblk = pl.BlockSpec((None, tr, width), lambda i: (layer, i, 0))
    return pl.pallas_call(
        body, name=name, grid=(rows // tr,),
        in_specs=[pl.BlockSpec((N_DEV, tr, width), lambda i: (0, base + i, 0)), wblk, wblk, wblk],
        out_specs=[blk] * 4, out_shape=[_sds((rows, width), F32)] * 4, compiler_params=_params())(recv, w2, m2, v2)


_WEIGHTS = ("pool_norm", "pool_w_in", "pool_w_grp", "pool_scale", "pool_w_out", "conv_norm", "conv_w_in", "conv_w", "conv_w_out",
            "mla_norm", "mla_w_in", "mla_q_norm", "mla_w_q_up", "mla_kv_norm", "mla_w_kv_up", "mla_w_out", "final_norm")


def _step(x, positions, loss_target, p, m, v):
    gathers, tokens, dep = [], [], []
    for group, arrays in enumerate(_pack_groups(p)):
        lands = _own_slabs(f"gather{group}_own", arrays, True, dep)
        ks = _SIBLING + _ICI_DIRECT if group < TWO_LEVEL_GROUPS else None
        ssem, rsem, arrays, lands, token = _exchange_start(f"gather{group}_start", arrays, lands, True, ks)
        gathers.append((ssem, rsem, arrays, lands))
        tokens.append(token)
        dep = [token]
    state = {}

    def wait_group(group, after):
        if group >= TWO_LEVEL_GROUPS:
            return _exchange_wait(f"gather{group}_wait", *gathers[group], after, True)
        ssem, rsem, arrays, lands = gathers[group]
        send2, recv2, arrays, lands, token = _gather_relay(f"gather{group}_relay", ssem, rsem, arrays, lands, after)
        return _gather_wait2(f"gather{group}_wait", ssem, rsem, send2, recv2, arrays, lands, [token])

    def get_w(layer, after):
        if layer == 0:
            bufs = wait_group(0, list(tokens))
            state["small"] = _small_views(bufs[1])
            rest = lambda later: dict(zip(("g_grp", "g_out"), wait_group(1, later)))
        elif layer == 1:
            bufs = wait_group(2, after)
            rest = lambda later: dict(g_out=wait_group(3, later)[0])
        elif layer == 2:
            bufs = wait_group(4, after)
        else:
            bufs = wait_group(5, after)
            rest = lambda later: dict(g_grp=bufs[1], g_out=bufs[2])
        w = _layer_weights(layer, bufs, state["small"], p["conv_norm"])
        if layer != 2:
            w["rest"] = rest
        return w

    scatters = []

    def put_g(layer, g):
        if layer == 4:
            keys, arrays = ("small",), [_pack_small_grads(g)]
        else:
            keys, arrays = _grad_group(layer, g)
        n = len(scatters)
        lands = _own_slabs(f"scatter{n}_own", arrays, False, [])
        ssem, rsem, arrays, lands, token = _exchange_start(f"scatter{n}_start", arrays, lands, False)
        scatters.append((layer, keys, (ssem, rsem, arrays, lands)))
        tokens.append(token)
        return [token]

    loss, grad_x = _local_step(x[0], positions[0], loss_target[0], p["final_norm"], get_w, put_g)

    res, after = {}, [tokens[-1]]
    for n, (layer, keys, handles) in enumerate(scatters):
        recv = _exchange_wait(f"scatter{n}_wait", *handles, after, False)
        if layer == 4:
            break
        l = 1 if layer == 3 else 0
        for key, buf in zip(keys, recv):
            name = _GRAD_PARAM[layer][key]
            tr = min(256, buf.shape[1]) if name != "mla_w_q_up" else buf.shape[1]
            res[name, l] = _sum_adamw(f"adam_{name}{l}", buf, 0, p[name], m[name], v[name], tr, l)
        after = [res[name, l][1]]
    small = _sum_adamw("adam_small", recv[0], 0, _pack_small(p, SMALL_ROWS_RS, True)[None], _pack_small(m, SMALL_ROWS_RS, True)[None],
                       _pack_small(v, SMALL_ROWS_RS, True)[None], SMALL_ROWS_RS)
    small = [_unpack_small(a, True) for a in small]
    final = {k: tuple(part[k] for part in small) for k in _SMALL_SHARDED + _SMALL_REPLICATED}
    for k in _WEIGHTS:
        if k not in final:
            layers = [res[k, l] for l in range(p[k].shape[0])]
            final[k] = tuple(jnp.stack([lay[part] for lay in layers]).reshape(p[k].shape) for part in range(4))
    res = final

    loss = lax.psum(loss, ("x", "y", "c"))
    out = [loss, grad_x[None]]
    for part in range(4):
        out += [res[k][part] for k in _WEIGHTS]
    return tuple(out)


def kernel(x, positions, pool_norm, pool_w_in, pool_w_grp, pool_scale, pool_w_out, conv_norm, conv_w_in, conv_w, conv_w_out, mla_norm, mla_w_in, mla_q_norm, mla_w_q_up, mla_kv_norm, mla_w_kv_up, mla_w_out, final_norm, loss_target, m_pool_norm, m_pool_w_in, m_pool_w_grp, m_pool_scale, m_pool_w_out, m_conv_norm, m_conv_w_in, m_conv_w, m_conv_w_out, m_mla_norm, m_mla_w_in, m_mla_q_norm, m_mla_w_q_up, m_mla_kv_norm, m_mla_w_kv_up, m_mla_w_out, m_final_norm, v_pool_norm, v_pool_w_in, v_pool_w_grp, v_pool_scale, v_pool_w_out, v_conv_norm, v_conv_w_in, v_conv_w, v_conv_w_out, v_mla_norm, v_mla_w_in, v_mla_q_norm, v_mla_w_q_up, v_mla_kv_norm, v_mla_w_kv_up, v_mla_w_out, v_final_norm):
    p = dict(pool_norm=pool_norm, pool_w_in=pool_w_in, pool_w_grp=pool_w_grp, pool_scale=pool_scale, pool_w_out=pool_w_out,
             conv_norm=conv_norm, conv_w_in=conv_w_in, conv_w=conv_w, conv_w_out=conv_w_out, mla_norm=mla_norm, mla_w_in=mla_w_in,
             mla_q_norm=mla_q_norm, mla_w_q_up=mla_w_q_up, mla_kv_norm=mla_kv_norm, mla_w_kv_up=mla_w_kv_up, mla_w_out=mla_w_out,
             final_norm=final_norm)
    m = dict(pool_norm=m_pool_norm, pool_w_in=m_pool_w_in, pool_w_grp=m_pool_w_grp, pool_scale=m_pool_scale, pool_w_out=m_pool_w_out,
             conv_norm=m_conv_norm, conv_w_in=m_conv_w_in, conv_w=m_conv_w, conv_w_out=m_conv_w_out, mla_norm=m_mla_norm,
             mla_w_in=m_mla_w_in, mla_q_norm=m_mla_q_norm, mla_w_q_up=m_mla_w_q_up, mla_kv_norm=m_mla_kv_norm,
             mla_w_kv_up=m_mla_w_kv_up, mla_w_out=m_mla_w_out, final_norm=m_final_norm)
    v = dict(pool_norm=v_pool_norm, pool_w_in=v_pool_w_in, pool_w_grp=v_pool_w_grp, pool_scale=v_pool_scale, pool_w_out=v_pool_w_out,
             conv_norm=v_conv_norm, conv_w_in=v_conv_w_in, conv_w=v_conv_w, conv_w_out=v_conv_w_out, mla_norm=v_mla_norm,
             mla_w_in=v_mla_w_in, mla_q_norm=v_mla_q_norm, mla_w_q_up=v_mla_w_q_up, mla_kv_norm=v_mla_kv_norm,
             mla_w_kv_up=v_mla_w_kv_up, mla_w_out=v_mla_w_out, final_norm=v_final_norm)
    return _step(x, positions, loss_target, p, m, v)
```

```python
import functools

import jax
import jax.numpy as jnp
from jax import lax
from jax.experimental import pallas as pl
from jax.experimental.pallas import tpu as pltpu

BF = jnp.bfloat16
F32 = jnp.float32

N_DEV = 8
D_MODEL = 1024
D_INNER = 2048
POOL_WINDOWS = (2, 4, 8, 16)
POOL_GROUP = 512
N_HEADS = 16
QK_NOPE = 128
QK_ROPE = 64
QK_DIM = QK_NOPE + QK_ROPE
V_DIM = 128
Q_RANK = 384
KV_RANK = 256
ATTN_SCALE = QK_DIM ** -0.5
LOG2_E = 1.4426950408889634
LN_2 = 0.6931471805599453
Q_PRESCALE = ATTN_SCALE * LOG2_E
ATTN_TILE = 512
ATTN_HEADS_PER_STEP = 2
ROPE_BASE = 10000.0
NORM_EPS = 1e-6
NEG_BIG = -1e30

ADAM_LR = 0.001
ADAM_B1 = 0.9
ADAM_B2 = 0.999
ADAM_EPS = 1e-08
ADAM_WD = 0.01
ADAM_STEP = 10

VMEM_LIMIT_BYTES = 52 * 1024 * 1024
IN_PROJ_ROWS = 1024
POOL_HALO = 32
CONV_HALO = 16

NN = (((1,), (0,)), ((), ()))
NT = (((1,), (1,)), ((), ()))
TN = (((0,), (0,)), ((), ()))

TWO_LEVEL_GROUPS = 3
SMALL_ROWS_AG = 16
SMALL_ROWS_RS = 32


def _sds(shape, dtype):
    return jax.ShapeDtypeStruct(tuple(shape), dtype)


def _params():
    return pltpu.CompilerParams(vmem_limit_bytes=VMEM_LIMIT_BYTES)


_ANY = pl.BlockSpec(memory_space=pl.ANY)


def _dot(a, b, dims):
    return lax.dot_general(a, b, dims, preferred_element_type=F32)


def _sig(z):
    return 1.0 / (1.0 + jnp.exp(-z))


def _silu_and_grad(z):
    sig = _sig(z)
    return z * sig, sig * (1.0 + z * (1.0 - sig))


def _to_row(col):
    return jnp.broadcast_to(col, (col.shape[0], 128)).T[0:1, :]


def _rope_swap(x, p):
    pb = p.astype(BF)
    hi = x.astype(BF)
    r1 = x - hi.astype(F32)
    mid = r1.astype(BF)
    lo = (r1 - mid.astype(F32)).astype(BF)
    return (_dot(hi, pb, NN) + _dot(mid, pb, NN)) + _dot(lo, pb, NN)


def _rope_fwd(x, cosf, sinf, p):
    return x * cosf + _rope_swap(x, p) * sinf


def _rope_bwd(dy, cosf, sinf, p):
    return dy * cosf + _rope_swap(dy * sinf, p)


def _rms_bwd(dxn, x, g, res):
    r = lax.rsqrt(jnp.mean(x * x, axis=-1, keepdims=True) + NORM_EPS)
    v = dxn * g
    dx = r * v - x * ((r * r * r) * jnp.mean(v * x, axis=-1, keepdims=True))
    if res is not None:
        dx = dx + res
    dg = jnp.sum(dxn * (x * r), axis=0, keepdims=True)
    return dx, dg


def _accumulate(ref, val, step):
    @pl.when(step == 0)
    def _():
        ref[...] = val

    @pl.when(step > 0)
    def _():
        ref[...] += val


def _mm(name, grid, ins, in_specs, outs, out_specs, dims, epi, red=None, acc_shape=None):
    n_in, n_out = len(ins), len(outs)
    n_red = None if red is None else grid[red]

    def body(*refs):
        in_refs, out_refs = refs[:n_in], refs[n_in:n_in + n_out]
        pids = tuple(pl.program_id(ax) for ax in range(len(grid)))
        a, b = in_refs[0][...], in_refs[1][...]
        if a.ndim == 3:
            a = a.reshape(-1, a.shape[-1])
        if b.ndim == 3:
            b = b.reshape(-1, b.shape[-1])
        part = _dot(a.astype(BF), b.astype(BF), dims)
        if red is None:
            epi(part, in_refs[2:], out_refs, pids)
        else:
            acc = refs[n_in + n_out]
            k = pids[red]
            _accumulate(acc, part, k)

            @pl.when(k == n_red - 1)
            def _():
                epi(acc[...], in_refs[2:], out_refs, pids)

    scratch = [] if red is None else [pltpu.VMEM(acc_shape, F32)]
    return pl.pallas_call(body, name=name, grid=grid, in_specs=in_specs, out_specs=out_specs, out_shape=outs,
                          scratch_shapes=scratch, compiler_params=_params())(*ins)


def _store(part, extra, outs, pids):
    outs[0][...] = part.astype(outs[0].dtype)


def _rms_fwd(name, x, g, tm):
    s, d = x.shape

    def body(x_ref, g_ref, o_ref):
        xv = x_ref[...]
        r = lax.rsqrt(jnp.mean(xv * xv, axis=-1, keepdims=True) + NORM_EPS)
        o_ref[...] = ((xv * r) * g_ref[...]).astype(BF)

    return pl.pallas_call(body, name=name, grid=(s // tm,),
                          in_specs=[pl.BlockSpec((tm, d), lambda i: (i, 0)), pl.BlockSpec((1, d), lambda i: (0, 0))],
                          out_specs=pl.BlockSpec((tm, d), lambda i: (i, 0)), out_shape=_sds((s, d), BF),
                          compiler_params=_params())(x, g.reshape(1, d))


def _norm_in_proj(name, x, g, wbuf, w_index, n_j, tm):
    s = x.shape[0]
    ti = IN_PROJ_ROWS if s % IN_PROJ_ROWS == 0 else tm

    def body(x_ref, g_ref, w_ref, h_ref, xn_ref):
        @pl.when(pl.program_id(1) == 0)
        def _():
            xv = x_ref[...]
            r = lax.rsqrt(jnp.mean(xv * xv, axis=-1, keepdims=True) + NORM_EPS)
            xn_ref[...] = ((xv * r) * g_ref[...]).astype(BF)

        h_ref[...] = _dot(xn_ref[...], w_ref[...], NN).astype(BF)

    row = lambda i, j: (i, 0)
    return pl.pallas_call(
        body, name=name, grid=(s // ti, n_j),
        in_specs=[pl.BlockSpec((ti, D_MODEL), row), pl.BlockSpec((1, D_MODEL), lambda i, j: (0, 0)),
                  pl.BlockSpec((None, D_MODEL, 512), w_index)],
        out_specs=[pl.BlockSpec((ti, 512), lambda i, j: (i, j)), pl.BlockSpec((ti, D_MODEL), row)],
        out_shape=[_sds((s, n_j * 512), BF), _sds((s, D_MODEL), BF)], compiler_params=_params())(x, g.reshape(1, D_MODEL), wbuf)


def _tn(name, a, b, out_shape, out_block, out_index, a_cols, b_cols, grid, a_index, b_index, dep=()):
    s = a.shape[-2]
    a_block = (s, a_cols) if a.ndim == 2 else (None, s, a_cols)
    b_block = (s, b_cols) if b.ndim == 2 else (None, s, b_cols)

    def epi(part, extra, outs, pids):
        outs[0][...] = part.astype(BF).reshape(outs[0].shape)

    return _mm(name, grid, [a, b] + list(dep), [pl.BlockSpec(a_block, a_index), pl.BlockSpec(b_block, b_index)] + [_ANY] * len(dep),
               [_sds(out_shape, BF)], [pl.BlockSpec(out_block, out_index)], TN, epi)[0]


def _pool_window_fwd(name, h, tm):
    s = h.shape[0]
    hb = POOL_HALO

    def body(u_ref, halo_ref, o_ref, e_ref, a_ref, b_ref):
        i = pl.program_id(0)
        row = lax.broadcasted_iota(jnp.int32, (tm, 1), 0) + i * tm
        for g, w in enumerate(POOL_WINDOWS):
            cs = slice(g * POOL_GROUP, (g + 1) * POOL_GROUP)
            e_ref[0:hb, :] = jnp.where(i > 0, halo_ref[:, cs].astype(F32), 0.0)
            e_ref[hb:, :] = u_ref[:, cs].astype(F32)
            src, bufs = e_ref, (a_ref, b_ref)
            for lv in range(1, w.bit_length()):
                dst, st, sh = bufs[(lv - 1) % 2], 8 * lv, 2 ** (lv - 1)
                n = hb + tm - st
                dst[st:, :] = src[st:, :] + src[pl.ds(st - sh, n), :]
                src = dst
            cnt = jnp.minimum(row + 1, w).astype(F32)
            o_ref[:, cs] = (src[hb:, :] / cnt - u_ref[:, cs].astype(F32)).astype(BF)

    per = tm // hb
    return pl.pallas_call(
        body, name=name, grid=(s // tm,),
        in_specs=[pl.BlockSpec((tm, D_INNER), lambda i: (i, 0)),
                  pl.BlockSpec((hb, D_INNER), lambda i: (jnp.maximum(i * per - 1, 0), 0))],
        out_specs=pl.BlockSpec((tm, D_INNER), lambda i: (i, 0)), out_shape=_sds((s, D_INNER), BF),
        scratch_shapes=[pltpu.VMEM((hb + tm, POOL_GROUP), F32)] * 3, compiler_params=_params())(h, h)


def _pool_window_bwd(name, dp, tm, dh):
    s = dp.shape[0]
    nt = s // tm
    hb = POOL_HALO

    def body(d_ref, halo_ref, dh_in_ref, o_ref, e_ref, a_ref, b_ref):
        i = pl.program_id(0)
        row = lax.broadcasted_iota(jnp.int32, (tm, 1), 0) + i * tm
        hrow = lax.broadcasted_iota(jnp.int32, (hb, 1), 0) + (i + 1) * tm
        for g, w in enumerate(POOL_WINDOWS):
            cs = slice(g * POOL_GROUP, (g + 1) * POOL_GROUP)
            e_ref[0:tm, :] = d_ref[:, cs].astype(F32) / jnp.minimum(row + 1, w).astype(F32)
            e_ref[tm:, :] = jnp.where(i < nt - 1, halo_ref[:, cs].astype(F32) / jnp.minimum(hrow + 1, w).astype(F32), 0.0)
            src, bufs = e_ref, (a_ref, b_ref)
            for lv in range(1, w.bit_length()):
                dst, sh = bufs[(lv - 1) % 2], 2 ** (lv - 1)
                n = tm + hb - 8 * lv
                dst[0:n, :] = src[0:n, :] + src[pl.ds(sh, n), :]
                src = dst
            o_ref[:, cs] = (src[0:tm, :] - d_ref[:, cs].astype(F32)).astype(BF)

    per = tm // hb
    last = s // hb - 1
    return pl.pallas_call(
        body, name=name, grid=(nt,),
        in_specs=[pl.BlockSpec((tm, D_INNER), lambda i: (i, 0)),
                  pl.BlockSpec((hb, D_INNER), lambda i: (jnp.minimum((i + 1) * per, last), 0)), _ANY],
        out_specs=pl.BlockSpec((tm, D_INNER), lambda i: (i, 0)), out_shape=_sds(dh.shape, BF),
        input_output_aliases={2: 0},
        scratch_shapes=[pltpu.VMEM((hb + tm, POOL_GROUP), F32)] * 3, compiler_params=_params())(dp, dp, dh)


def _grp_block():
    return pl.BlockSpec((N_DEV, 64, POOL_GROUP), lambda i, g: (0, g, 0))


def _pool_fwd(x, l, w, tm):
    s = x.shape[0]
    nt = s // tm
    n = f"pool{l}"
    h, xn = _norm_in_proj(n + "_in", x, w["norm"], w["g_in"], lambda i, j: (j, 0, 0), 8, tm)
    pooled = _pool_window_fwd(n + "_win", h, tm)
    w = dict(w, **w["rest"]([h]))

    def gate(part, extra, outs, pids):
        z = extra[0][...].astype(F32)
        outs[0][...] = ((part * extra[1][...]) * (z * _sig(z))).astype(BF)

    tg = IN_PROJ_ROWS if s % IN_PROJ_ROWS == 0 else tm
    (gated,) = _mm(n + "_grp", (s // tg, 4), [pooled, w["g_grp"], h, w["scale"].reshape(1, D_INNER)],
                   [pl.BlockSpec((tg, 512), lambda i, g: (i, g)), _grp_block(),
                    pl.BlockSpec((tg, 512), lambda i, g: (i, 4 + g)), pl.BlockSpec((1, 512), lambda i, g: (0, g))],
                   [_sds((s, D_INNER), BF)], [pl.BlockSpec((tg, 512), lambda i, g: (i, g))], NN, gate)
    y = _out_proj(n + "_out", gated, w["g_out"], 0, x, tm)
    return y, dict(x=x, xn=xn, h=h, pooled=pooled, gated=gated, w=w)


def _out_proj(name, gated, g1024, row_block, x, tm):
    s = x.shape[0]

    def epi(part, extra, outs, pids):
        outs[0][...] = part + extra[0][...]

    row = pl.BlockSpec((tm, D_MODEL), lambda i: (i, 0))
    return _mm(name, (s // tm,), [gated, g1024, x],
               [pl.BlockSpec((tm, D_INNER), lambda i: (i, 0)), pl.BlockSpec((N_DEV, 256, D_MODEL), lambda i: (0, row_block, 0)), row],
               [_sds((s, D_MODEL), F32)], [row], NN, epi)[0]


def _w_out_nt_block(row_block):
    return pl.BlockSpec((2, 256, D_MODEL), lambda j, i: (j, row_block, 0))


def _in_proj_bwd(name, dh, wbuf, w_index, n_k, x, g, dy, tm, dep=()):
    s = x.shape[0]
    tm = IN_PROJ_ROWS if s % IN_PROJ_ROWS == 0 else tm

    def epi(acc, extra, outs, pids):
        dx, dg = _rms_bwd(acc, extra[0][...], extra[1][...], extra[2][...])
        outs[0][...] = dx
        outs[1][...] = dx.astype(BF)
        _accumulate(outs[2], dg, pids[0])

    row = lambda i, k: (i, 0)
    return _mm(name, (s // tm, n_k), [dh, wbuf, x, g.reshape(1, D_MODEL), dy] + list(dep),
               [pl.BlockSpec((tm, 512), lambda i, k: (i, k)), pl.BlockSpec((None, D_MODEL, 512), w_index),
                pl.BlockSpec((tm, D_MODEL), row), pl.BlockSpec((1, D_MODEL), lambda i, k: (0, 0)), pl.BlockSpec((tm, D_MODEL), row)]
               + [_ANY] * len(dep),
               [_sds((s, D_MODEL), F32), _sds((s, D_MODEL), BF), _sds((1, D_MODEL), F32)],
               [pl.BlockSpec((tm, D_MODEL), row), pl.BlockSpec((tm, D_MODEL), row), pl.BlockSpec((1, D_MODEL), lambda i, k: (0, 0))],
               NT, epi, red=1, acc_shape=(tm, D_MODEL))


def _w_out_grad(name, gated, dyb):
    s = gated.shape[0]
    return _tn(name, gated, dyb, (D_INNER, D_MODEL), (512, D_MODEL), lambda i: (i, 0), 512, D_MODEL, (4,),
               lambda i: (0, i), lambda i: (0, 0))


def _pool_bwd(dy, dyb, l, w, sv, tm, dep, early=None):
    s = dy.shape[0]
    nt = s // tm
    n = f"pool{l}b"
    h, pooled = sv["h"], sv["pooled"]
    scale = w["scale"].reshape(1, D_INNER)

    def gate_bwd(part, extra, outs, pids):
        z, sc = extra[0][...].astype(F32), extra[1][...]
        wg = extra[3][...].reshape(POOL_GROUP, POOL_GROUP)
        mpv = _dot(extra[2][...], wg, NN)
        sz, dsz = _silu_and_grad(z)
        dm = part * sz
        dmp = (dm * sc).astype(BF)
        outs[0][...] = dmp
        outs[1][...] = (part * (mpv * sc) * dsz).astype(BF)
        _accumulate(outs[2], jnp.sum(dm * mpv, axis=0, keepdims=True), pids[1])
        outs[3][...] = _dot(dmp, wg, NT).astype(BF)

    tile = lambda j, i: (i, j)
    dmp, dz, dscale, dpool = _mm(
        n + "_out", (4, nt), [dyb, w["g_out"], h, scale, pooled, w["g_grp"]] + dep,
        [pl.BlockSpec((tm, D_MODEL), lambda j, i: (i, 0)), _w_out_nt_block(0),
         pl.BlockSpec((tm, 512), lambda j, i: (i, 4 + j)), pl.BlockSpec((1, 512), lambda j, i: (0, j)), pl.BlockSpec((tm, 512), tile),
         pl.BlockSpec((N_DEV, 64, POOL_GROUP), lambda j, i: (0, j, 0))] + [_ANY] * len(dep),
        [_sds((s, D_INNER), BF), _sds((s, 2 * D_INNER), BF), _sds((1, D_INNER), F32), _sds((s, D_INNER), BF)],
        [pl.BlockSpec((tm, 512), tile), pl.BlockSpec((tm, 512), lambda j, i: (i, 4 + j)), pl.BlockSpec((1, 512), lambda j, i: (0, j)),
         pl.BlockSpec((tm, 512), tile)],
        NT, gate_bwd)
    g_out = _w_out_grad(n + "_gout", sv["gated"], dyb).reshape(N_DEV, 256, D_MODEL)
    g_grp = _tn(n + "_ggrp", pooled, dmp, (N_DEV, 256, 512), (N_DEV, 64, 512), lambda g: (0, g, 0),
                512, 512, (4,), lambda g: (0, g), lambda g: (0, g))
    dep = early(dict(g_grp=g_grp, g_out=g_out)) if early is not None else ()
    dh = _pool_window_bwd(n + "_win", dpool, tm, dz)
    g_in = _tn(n + "_gin", sv["xn"], dh, (N_DEV, D_MODEL, 512), (None, D_MODEL, 512), lambda j: (j, 0, 0),
               D_MODEL, 512, (8,), lambda j: (0, 0), lambda j: (0, j), dep)
    dep = early(dict(g_in=g_in)) if early is not None else ()
    dx, dxb, dnorm = _in_proj_bwd(n + "_in", dh, w["g_in"], lambda i, k: (k, 0, 0), 8, sv["x"], w["norm"], dy, tm, dep)
    return dx, dxb, dict(g_in=g_in, g_grp=g_grp, g_out=g_out, norm=dnorm[0], scale=dscale[0])


def _conv_in_index(i, j):
    return (j // 2, 0, j % 2)


def _conv_fwd(x, w, tm):
    s = x.shape[0]
    nt = s // tm
    h, xn = _norm_in_proj("conv_in", x, w["norm"], w["g_in"], _conv_in_index, 16, tm)
    per = tm // CONV_HALO

    def body(b_ref, c_ref, h_ref, z_ref, cp_ref, hp_ref, w_ref, o_ref, e_ref):
        i = pl.program_id(0)
        ch = c_ref[...].astype(F32) * h_ref[...].astype(F32)
        e_ref[0:CONV_HALO, :] = jnp.where(i > 0, cp_ref[...].astype(F32) * hp_ref[...].astype(F32), 0.0)
        e_ref[CONV_HALO:, :] = ch
        co = (w_ref[2:3, :] * ch + w_ref[1:2, :] * e_ref[pl.ds(CONV_HALO - 1, tm), :]
              + w_ref[0:1, :] * e_ref[pl.ds(CONV_HALO - 2, tm), :])
        z = z_ref[...].astype(F32)
        o_ref[...] = ((b_ref[...].astype(F32) * co) * (z * _sig(z))).astype(BF)

    def col(q):
        return pl.BlockSpec((tm, 512), lambda i, j: (i, 4 * q + j))

    def prev(q):
        return pl.BlockSpec((CONV_HALO, 512), lambda i, j: (jnp.maximum(i * per - 1, 0), 4 * q + j))

    gated = pl.pallas_call(
        body, name="conv_mix", grid=(nt, 4),
        in_specs=[col(0), col(1), col(2), col(3), prev(1), prev(2), pl.BlockSpec((3, 512), lambda i, j: (0, j))],
        out_specs=pl.BlockSpec((tm, 512), lambda i, j: (i, j)), out_shape=_sds((s, D_INNER), BF),
        scratch_shapes=[pltpu.VMEM((CONV_HALO + tm, 512), F32)], compiler_params=_params())(h, h, h, h, h, h, w["conv_w"])
    w = dict(w, **w["rest"]([gated]))
    y = _out_proj("conv_out", gated, w["g_out"], 0, x, tm)
    return y, dict(x=x, xn=xn, h=h, gated=gated, w=w)


def _conv_bwd(dy, dyb, w, sv, tm, dep, early):
    s = dy.shape[0]
    nt = s // tm
    h = sv["h"]
    per = tm // CONV_HALO
    last = s // CONV_HALO - 1
    n_dep = len(dep)

    def body(dy_ref, dyn_ref, wo_ref, b_ref, c_ref, h_ref, z_ref, cp_ref, hp_ref, bn_ref, zn_ref, w_ref, *rest):
        dall_ref, dw_ref, e_ref, f_ref = rest[n_dep:]
        db_ref, dc_ref, dh_ref, dz_ref = (dall_ref.at[:, q * 512:(q + 1) * 512] for q in range(4))
        i = pl.program_id(1)
        wo = wo_ref[...].reshape(512, D_MODEL)
        dg_tile = _dot(dy_ref[...], wo, NT)
        dg_next = _dot(dyn_ref[...], wo, NT)
        w0, w1, w2 = w_ref[0:1, :], w_ref[1:2, :], w_ref[2:3, :]
        c, hh, b = c_ref[...].astype(F32), h_ref[...].astype(F32), b_ref[...].astype(F32)
        ch = c * hh
        e_ref[0:CONV_HALO, :] = jnp.where(i > 0, cp_ref[...].astype(F32) * hp_ref[...].astype(F32), 0.0)
        e_ref[CONV_HALO:, :] = ch
        ch1 = e_ref[pl.ds(CONV_HALO - 1, tm), :]
        ch2 = e_ref[pl.ds(CONV_HALO - 2, tm), :]
        co = w2 * ch + w1 * ch1 + w0 * ch2
        sz, dsz = _silu_and_grad(z_ref[...].astype(F32))
        dgv = dg_tile
        dyv = dgv * sz
        dz_ref[...] = (dgv * (b * co) * dsz).astype(BF)
        db_ref[...] = (dyv * co).astype(BF)
        dco = dyv * b
        zn = zn_ref[...].astype(F32)
        f_ref[0:tm, :] = dco
        f_ref[tm:, :] = jnp.where(i < nt - 1, dg_next * (zn * _sig(zn)) * bn_ref[...].astype(F32), 0.0)
        dch = w2 * dco + w1 * f_ref[pl.ds(1, tm), :] + w0 * f_ref[pl.ds(2, tm), :]
        dc_ref[...] = (dch * hh).astype(BF)
        dh_ref[...] = (dch * c).astype(BF)
        for tap, shifted in enumerate((ch2, ch1, ch)):
            _accumulate(dw_ref.at[tap:tap + 1, :], jnp.sum(dco * shifted, axis=0, keepdims=True), i)

    def col(q):
        return pl.BlockSpec((tm, 512), lambda j, i: (i, 4 * q + j))

    def prev(q):
        return pl.BlockSpec((CONV_HALO, 512), lambda j, i: (jnp.maximum(i * per - 1, 0), 4 * q + j))

    def nxt(q):
        return pl.BlockSpec((CONV_HALO, 512), lambda j, i: (jnp.minimum((i + 1) * per, last), 4 * q + j))

    wspec = pl.BlockSpec((3, 512), lambda j, i: (0, j))
    dy_tile = pl.BlockSpec((tm, D_MODEL), lambda j, i: (i, 0))
    dy_next = pl.BlockSpec((CONV_HALO, D_MODEL), lambda j, i: (jnp.minimum((i + 1) * per, last), 0))
    dh, dw = pl.pallas_call(
        body, name="convb_mix", grid=(4, nt),
        in_specs=[dy_tile, dy_next, _w_out_nt_block(0), col(0), col(1), col(2), col(3), prev(1), prev(2), nxt(0), nxt(3), wspec]
        + [_ANY] * n_dep,
        out_specs=[pl.BlockSpec((tm, D_INNER), lambda j, i: (i, j)), wspec],
        out_shape=[_sds((s, 4 * D_INNER), BF), _sds((3, D_INNER), F32)],
        scratch_shapes=[pltpu.VMEM((CONV_HALO + tm, 512), F32)] * 2, compiler_params=_params(),
    )(dyb, dyb, w["g_out"], h, h, h, h, h, h, h, h, w["conv_w"], *dep)

    def w_block(kp):
        k = 4 * (kp % 4) + kp // 4
        return (k // 2, 0, k % 2)

    g_in = _tn("convb_gin", sv["xn"], dh, (N_DEV, D_MODEL, D_MODEL), (None, D_MODEL, 512), lambda j: (j // 2, 0, j % 2),
               D_MODEL, 512, (16,), lambda j: (0, 0), lambda j: (0, 4 * (j % 4) + j // 4))
    g_out = _w_out_grad("convb_gout", sv["gated"], dyb).reshape(N_DEV, 256, D_MODEL)
    dep = early(dict(g_in=g_in, g_out=g_out))
    dx, dxb, dnorm = _in_proj_bwd("convb_in", dh, w["g_in"], lambda i, kp: w_block(kp), 16, sv["x"], w["norm"], dy, tm, dep)
    return dx, dxb, dict(g_in=g_in, g_out=g_out, norm=dnorm[0], conv_w=dw)


def _attn_tiles(s):
    t = min(ATTN_TILE, s)
    return t, s // t


def _causal_keep(t, keys_on_rows):
    r = lax.broadcasted_iota(jnp.int32, (t, t), 0)
    c = lax.broadcasted_iota(jnp.int32, (t, t), 1)
    return (r <= c) if keys_on_rows else (c <= r)


def _mla_fwd(x, w, rope, tm):
    s = x.shape[0]
    nt = s // tm
    cosf, sinf, perm = rope
    xn = _rms_fwd("mla_rms", x, w["norm"], tm)

    def in_body(xn_ref, wq_ref, wkv_ref, wkr_ref, wz_ref, gq_ref, gkv_ref, cos_ref, sin_ref, p_ref,
                ql_ref, kvl_ref, qn_ref, kvn_ref, krr_ref, z_ref):
        xv = xn_ref[...]
        ql = _dot(xv, wq_ref[...], NN)
        kvl = _dot(xv, wkv_ref[...], NN)
        ql_ref[...] = ql
        kvl_ref[...] = kvl
        rq = lax.rsqrt(jnp.mean(ql * ql, axis=-1, keepdims=True) + NORM_EPS)
        qn_ref[...] = ((ql * rq) * gq_ref[...]).astype(BF)
        rkv = lax.rsqrt(jnp.mean(kvl * kvl, axis=-1, keepdims=True) + NORM_EPS)
        kvn_ref[...] = ((kvl * rkv) * gkv_ref[...]).astype(BF)
        kr = _dot(xv, wkr_ref[...], NN)
        krr_ref[...] = _rope_fwd(kr, cos_ref[...], sin_ref[...], p_ref[...]).astype(BF)
        z_ref[...] = _dot(xv, wz_ref[...], NN).astype(BF)

    def full(a):
        return pl.BlockSpec(a.shape, lambda i: (0,) * a.ndim)

    def rows(c):
        return pl.BlockSpec((tm, c), lambda i: (i, 0))

    gq, gkv = w["q_norm"].reshape(1, Q_RANK), w["kv_norm"].reshape(1, KV_RANK)
    q_lat, kv_lat, qn, kvn, krr, z = pl.pallas_call(
        in_body, name="mla_in", grid=(nt,),
        in_specs=[rows(D_MODEL), full(w["w_q"]), full(w["w_kv"]), full(w["w_kr"]), full(w["w_z"]), full(gq), full(gkv),
                  rows(QK_ROPE), rows(QK_ROPE), full(perm)],
        out_specs=[rows(Q_RANK), rows(KV_RANK), rows(Q_RANK), rows(KV_RANK), rows(QK_ROPE), rows(D_INNER)],
        out_shape=[_sds((s, Q_RANK), F32), _sds((s, KV_RANK), F32), _sds((s, Q_RANK), BF), _sds((s, KV_RANK), BF),
                   _sds((s, QK_ROPE), BF), _sds((s, D_INNER), BF)],
        compiler_params=_params())(xn, w["w_q"], w["w_kv"], w["w_kr"], w["w_z"], gq, gkv, cosf, sinf, perm)

    def q_epi(part, extra, outs, pids):
        r = part[:, 2 * QK_NOPE:]
        lane = lax.broadcasted_iota(jnp.int32, r.shape, 1)
        swapped = jnp.where((lane & (QK_ROPE - 1)) < QK_ROPE // 2, pltpu.roll(r, 2 * QK_ROPE - QK_ROPE // 2, axis=1),
                            pltpu.roll(r, QK_ROPE // 2, axis=1))
        roped = (r * extra[0][...] + swapped * extra[1][...]) * Q_PRESCALE
        for hh in range(2):
            outs[0][hh, :, 0:QK_NOPE] = (part[:, hh * QK_NOPE:(hh + 1) * QK_NOPE] * Q_PRESCALE).astype(BF)
            outs[0][hh, :, QK_NOPE:QK_DIM] = roped[:, hh * QK_ROPE:(hh + 1) * QK_ROPE].astype(BF)

    tp = IN_PROJ_ROWS if s % IN_PROJ_ROWS == 0 else tm
    rope_row = pl.BlockSpec((tp, QK_ROPE), lambda h, i: (i, 0))
    rope_pair = pl.BlockSpec((tp, 2 * QK_ROPE), lambda h, i: (i, 0))
    cos2, sin2 = jnp.concatenate([cosf, cosf], axis=1), jnp.concatenate([sinf, sinf], axis=1)
    (q,) = _mm("mla_qup", (N_HEADS // 2, s // tp), [qn, w["w_qpair"], cos2, sin2],
               [pl.BlockSpec((tp, Q_RANK), lambda h, i: (i, 0)), pl.BlockSpec((None, Q_RANK, 2 * QK_DIM), lambda h, i: (h, 0, 0)),
                rope_pair, rope_pair],
               [_sds((N_HEADS, s, QK_DIM), BF)], [pl.BlockSpec((2, tp, QK_DIM), lambda h, i: (h, i, 0))], NN, q_epi)

    def kv_epi(part, extra, outs, pids):
        outs[0][:, 0:QK_NOPE] = part[:, 0:QK_NOPE].astype(BF)
        outs[0][:, QK_NOPE:QK_DIM] = extra[0][...]
        outs[1][...] = part[:, QK_NOPE:].astype(BF)

    k, v = _mm("mla_kvup", (N_HEADS, s // tp), [kvn, w["g512"], krr],
               [pl.BlockSpec((tp, KV_RANK), lambda h, i: (i, 0)),
                pl.BlockSpec((None, KV_RANK, 256), lambda h, i: (h // 2, 0, h % 2)), rope_row],
               [_sds((N_HEADS, s, QK_DIM), BF), _sds((N_HEADS, s, V_DIM), BF)],
               [pl.BlockSpec((None, tp, QK_DIM), lambda h, i: (h, i, 0)), pl.BlockSpec((None, tp, V_DIM), lambda h, i: (h, i, 0))],
               NN, kv_epi)

    t, nq = _attn_tiles(s)

    def attn_body(q_ref, k_ref, v_ref, z_ref, o_ref, g_ref, lse_ref):
        i = pl.program_id(1)

        def block(j, carry, masked):
            start = pl.multiple_of(j * t, t)
            out = []
            for hh, (m, lsum, acc) in enumerate(carry):
                sc = _dot(q_ref[hh], k_ref[hh, pl.ds(start, t), :], NT)
                if masked:
                    sc = jnp.where(_causal_keep(t, False), sc, NEG_BIG)
                mn = jnp.maximum(m, jnp.max(sc, axis=-1, keepdims=True))
                alpha = jnp.exp2(m - mn)
                p = jnp.exp2(sc - mn)
                lsum = alpha * lsum + jnp.sum(p, axis=-1, keepdims=True)
                acc = alpha * acc + _dot(p.astype(BF), v_ref[hh, pl.ds(start, t), :], NN)
                out.append((mn, lsum, acc))
            return tuple(out)

        init = ((jnp.full((t, 1), NEG_BIG, F32), jnp.zeros((t, 1), F32), jnp.zeros((t, V_DIM), F32)),) * ATTN_HEADS_PER_STEP
        carry = lax.fori_loop(0, i, lambda j, c: block(j, c, False), init)
        for hh, (m, lsum, acc) in enumerate(block(i, carry, True)):
            cols = slice(hh * V_DIM, (hh + 1) * V_DIM)
            o = acc / lsum
            z = z_ref[:, cols].astype(F32)
            o_ref[:, cols] = o
            g_ref[:, cols] = (o * (z * _sig(z))).astype(BF)
            lse_ref[hh] = _to_row(m + jnp.log(lsum) * LOG2_E)

    hp = ATTN_HEADS_PER_STEP
    head_col = pl.BlockSpec((t, hp * V_DIM), lambda h, i: (i, h))
    o, gated, lse = pl.pallas_call(
        attn_body, name="mla_attn", grid=(N_HEADS // hp, nq),
        in_specs=[pl.BlockSpec((hp, t, QK_DIM), lambda h, i: (h, i, 0)), pl.BlockSpec((hp, s, QK_DIM), lambda h, i: (h, 0, 0)),
                  pl.BlockSpec((hp, s, V_DIM), lambda h, i: (h, 0, 0)), head_col],
        out_specs=[head_col, head_col, pl.BlockSpec((hp, None, 1, t), lambda h, i: (h, i, 0, 0))],
        out_shape=[_sds((s, D_INNER), F32), _sds((s, D_INNER), BF), _sds((N_HEADS, nq, 1, t), F32)],
        compiler_params=_params())(q, k, v, z)
    y = _out_proj("mla_out", gated, w["g1024"], 0, x, tm)
    return y, dict(x=x, xn=xn, q_lat=q_lat, kv_lat=kv_lat, qn=qn, kvn=kvn, z=z, q=q, k=k, v=v, o=o, lse=lse, gated=gated)


def _mla_bwd(dy, dyb, w, sv, rope, tm, dep):
    s = dy.shape[0]
    nt = s // tm
    cosf, sinf, perm = rope
    t, nq = _attn_tiles(s)
    assert t == tm, "the row statistics of the backward are laid out per attention tile"
    q, k, v, lse = sv["q"], sv["k"], sv["v"], sv["lse"]

    def gate_bwd(part, extra, outs, pids):
        z, o = extra[0][...].astype(F32), extra[1][...]
        sz, dsz = _silu_and_grad(z)
        do = part * sz
        outs[0][...] = do.astype(BF)
        outs[1][...] = (part * o * dsz).astype(BF)
        prod = do * o
        for hh in range(4):
            outs[2][hh] = _to_row(jnp.sum(prod[:, hh * V_DIM:(hh + 1) * V_DIM], axis=-1, keepdims=True))

    tile = lambda j, i: (i, j)
    dob, dz, delta = _mm(
        "mlab_out", (4, nt), [dyb, w["g1024"], sv["z"], sv["o"]] + dep,
        [pl.BlockSpec((tm, D_MODEL), lambda j, i: (i, 0)), _w_out_nt_block(0),
         pl.BlockSpec((tm, 512), tile), pl.BlockSpec((tm, 512), tile)] + [_ANY] * len(dep),
        [_sds((s, D_INNER), BF), _sds((s, D_INNER), BF), _sds((N_HEADS, nt, 1, tm), F32)],
        [pl.BlockSpec((tm, 512), tile), pl.BlockSpec((tm, 512), tile), pl.BlockSpec((4, None, 1, tm), lambda j, i: (j, i, 0, 0))],
        NT, gate_bwd)

    hp = ATTN_HEADS_PER_STEP

    def attn_bwd_body(k_ref, v_ref, q_ref, do_ref, lse_ref, dl_ref, cos_ref, sin_ref, p_ref, dkv_ref, dkr_ref, dq_ref, dq_acc):
        j = pl.program_id(1)

        @pl.when(j == 0)
        def _():
            dq_acc[...] = jnp.zeros(dq_acc.shape, F32)

        def block(i, carry, masked):
            rows = pl.ds(pl.multiple_of(i * t, t), t)
            out = []
            for hh, (dk, dv) in enumerate(carry):
                kb, vb = k_ref[hh], v_ref[hh]
                qb, dob_ = q_ref[hh, rows, :], do_ref[rows, hh * V_DIM:(hh + 1) * V_DIM]
                st = _dot(kb, qb, NT)
                if masked:
                    st = jnp.where(_causal_keep(t, True), st, NEG_BIG)
                pt = jnp.exp2(st - lse_ref[hh, i])
                dv = dv + _dot(pt.astype(BF), dob_, NN)
                dst = (pt * (_dot(vb, dob_, NT) - dl_ref[hh, i])).astype(BF)
                dk = dk + _dot(dst, qb, NN)
                dq_acc[hh, rows, :] += _dot(dst, kb, TN)
                out.append((dk, dv))
            return tuple(out)

        init = ((jnp.zeros((t, QK_DIM), F32), jnp.zeros((t, V_DIM), F32)),) * hp
        carry = block(j, init, True)
        carry = lax.fori_loop(j + 1, nq, lambda i, c: block(i, c, False), carry)
        for hh, (dk, dv) in enumerate(carry):
            dk = dk * LN_2
            base = hh * 2 * V_DIM
            dkv_ref[:, base:base + QK_NOPE] = dk[:, 0:QK_NOPE].astype(BF)
            dkv_ref[:, base + QK_NOPE:base + 2 * V_DIM] = dv.astype(BF)
            dkr_ref[hh] = dk[:, QK_NOPE:]

        @pl.when(j == nq - 1)
        def _():
            for hh in range(hp):
                for c in range(nq):
                    rows = slice(c * t, (c + 1) * t)
                    dq = dq_acc[hh, rows, :] * ATTN_SCALE
                    dq_ref[hh, rows, 0:QK_NOPE] = dq[:, 0:QK_NOPE].astype(BF)
                    dq_ref[hh, rows, QK_NOPE:] = _rope_bwd(dq[:, QK_NOPE:], cos_ref[rows, :], sin_ref[rows, :], p_ref[...]).astype(BF)

    row_stats = pl.BlockSpec((hp, nq, 1, t), lambda h, j: (h, 0, 0, 0))
    seq_rope = pl.BlockSpec((s, QK_ROPE), lambda h, j: (0, 0))
    head_seq = pl.BlockSpec((hp, s, QK_DIM), lambda h, j: (h, 0, 0))
    dkv, dkr_h, dq = pl.pallas_call(
        attn_bwd_body, name="mlab_attn", grid=(N_HEADS // hp, nq),
        in_specs=[pl.BlockSpec((hp, t, QK_DIM), lambda h, j: (h, j, 0)), pl.BlockSpec((hp, t, V_DIM), lambda h, j: (h, j, 0)),
                  head_seq, pl.BlockSpec((s, hp * V_DIM), lambda h, j: (0, h)), row_stats, row_stats, seq_rope, seq_rope,
                  pl.BlockSpec((QK_ROPE, QK_ROPE), lambda h, j: (0, 0))],
        out_specs=[pl.BlockSpec((t, hp * 2 * V_DIM), lambda h, j: (j, h)), pl.BlockSpec((hp, t, QK_ROPE), lambda h, j: (h, j, 0)), head_seq],
        out_shape=[_sds((s, N_HEADS * 2 * V_DIM), BF), _sds((N_HEADS, s, QK_ROPE), F32), _sds((N_HEADS, s, QK_DIM), BF)],
        scratch_shapes=[pltpu.VMEM((hp, s, QK_DIM), F32)],
        compiler_params=_params())(k, v, q, dob, lse, delta, cosf, sinf, perm)

    def dkr_body(d_ref, cos_ref, sin_ref, p_ref, o_ref):
        tot = d_ref[0]
        for hh in range(1, N_HEADS):
            tot = tot + d_ref[hh]
        o_ref[...] = _rope_bwd(tot, cos_ref[...], sin_ref[...], p_ref[...]).astype(BF)

    r64 = pl.BlockSpec((tm, QK_ROPE), lambda i: (i, 0))
    dkr = pl.pallas_call(
        dkr_body, name="mlab_dkr", grid=(nt,),
        in_specs=[pl.BlockSpec((N_HEADS, tm, QK_ROPE), lambda i: (0, i, 0)), r64, r64, pl.BlockSpec((QK_ROPE, QK_ROPE), lambda i: (0, 0))],
        out_specs=r64, out_shape=_sds((s, QK_ROPE), BF), compiler_params=_params())(dkr_h, cosf, sinf, perm)

    def lat_epi(acc, extra, outs, pids):
        dx, dg = _rms_bwd(acc, extra[0][...], extra[1][...], None)
        outs[0][...] = dx.astype(BF)
        _accumulate(outs[1], dg, pids[0])

    tp = IN_PROJ_ROWS if s % IN_PROJ_ROWS == 0 else tm

    def lat_bwd(name, a, a_spec, b, b_spec, n_k, lat, g, rank):
        row = lambda i, k: (i, 0)
        one = lambda i, k: (0, 0)
        return _mm(name, (s // tp, n_k), [a, b, lat, g.reshape(1, rank)],
                   [a_spec, b_spec, pl.BlockSpec((tp, rank), row), pl.BlockSpec((1, rank), one)],
                   [_sds((s, rank), BF), _sds((1, rank), F32)], [pl.BlockSpec((tp, rank), row), pl.BlockSpec((1, rank), one)],
                   NT, lat_epi, red=1, acc_shape=(tp, rank))

    d_ql, g_qnorm = lat_bwd("mlab_qup", dq, pl.BlockSpec((None, tp, QK_DIM), lambda i, h: (h, i, 0)),
                            w["w_qh"], pl.BlockSpec((None, Q_RANK, QK_DIM), lambda i, h: (h, 0, 0)), N_HEADS,
                            sv["q_lat"], w["q_norm"], Q_RANK)
    d_kvl, g_kvnorm = lat_bwd("mlab_kvup", dkv, pl.BlockSpec((tp, 512), lambda i, kk: (i, kk)),
                              w["g512"], pl.BlockSpec((None, KV_RANK, 512), lambda i, kk: (kk, 0, 0)), N_DEV,
                              sv["kv_lat"], w["kv_norm"], KV_RANK)

    def in_bwd(dql_ref, dkvl_ref, dkr_ref, dz_ref, wq_ref, wkv_ref, wkr_ref, wz_ref, x_ref, g_ref, dy_ref, dx_ref, dxb_ref, dg_ref):
        acc = (_dot(dql_ref[...], wq_ref[...], NT) + _dot(dkvl_ref[...], wkv_ref[...], NT)
               + _dot(dkr_ref[...], wkr_ref[...], NT) + _dot(dz_ref[...], wz_ref[...], NT))
        dx, dg = _rms_bwd(acc, x_ref[...], g_ref[...], dy_ref[...])
        dx_ref[...] = dx
        dxb_ref[...] = dx.astype(BF)
        _accumulate(dg_ref, dg, pl.program_id(0))

    def full(a):
        return pl.BlockSpec(a.shape, lambda i: (0,) * a.ndim)

    def rows(c):
        return pl.BlockSpec((tm, c), lambda i: (i, 0))

    gm = w["norm"].reshape(1, D_MODEL)
    dx, dxb, g_norm = pl.pallas_call(
        in_bwd, name="mlab_in", grid=(nt,),
        in_specs=[rows(Q_RANK), rows(KV_RANK), rows(QK_ROPE), rows(D_INNER), full(w["w_q"]), full(w["w_kv"]), full(w["w_kr"]),
                  full(w["w_z"]), rows(D_MODEL), full(gm), rows(D_MODEL)],
        out_specs=[rows(D_MODEL), rows(D_MODEL), full(gm)],
        out_shape=[_sds((s, D_MODEL), F32), _sds((s, D_MODEL), BF), _sds((1, D_MODEL), F32)],
        compiler_params=_params())(d_ql, d_kvl, dkr, dz, w["w_q"], w["w_kv"], w["w_kr"], w["w_z"], sv["x"], gm, dy)

    xn = sv["xn"]
    one = lambda j: (0, 0)
    g_q = _tn("mlab_gq", xn, d_ql, (D_MODEL, Q_RANK), (D_MODEL, Q_RANK), one, D_MODEL, Q_RANK, (1,), one, one)
    g_kv = _tn("mlab_gkv", xn, d_kvl, (D_MODEL, KV_RANK), (D_MODEL, KV_RANK), one, D_MODEL, KV_RANK, (1,), one, one)
    g_kr = _tn("mlab_gkr", xn, dkr, (D_MODEL, QK_ROPE), (D_MODEL, QK_ROPE), one, D_MODEL, QK_ROPE, (1,), one, one)
    g_z = _tn("mlab_gz", xn, dz, (D_MODEL, D_INNER), (D_MODEL, 512), lambda j: (0, j), D_MODEL, 512, (4,), one, lambda j: (0, j))
    g_in = jnp.concatenate([g_q, g_kv, g_kr, g_z], axis=1)
    g_qh = _tn("mlab_gqup", sv["qn"], dq, (N_HEADS, Q_RANK, QK_DIM), (None, Q_RANK, QK_DIM), lambda h: (h, 0, 0),
               Q_RANK, QK_DIM, (N_HEADS,), lambda h: (0, 0), lambda h: (h, 0, 0))
    g_kvup = _tn("mlab_gkvup", sv["kvn"], dkv, (N_DEV, KV_RANK, 512), (None, KV_RANK, 512), lambda j: (j, 0, 0),
                 KV_RANK, 512, (N_DEV,), lambda j: (0, 0), lambda j: (0, j))
    g_out = _w_out_grad("mlab_gout", sv["gated"], dyb)
    s384 = g_qh.reshape(N_DEV, 2, Q_RANK, QK_DIM).transpose(0, 2, 1, 3).reshape(N_DEV, Q_RANK, 2 * QK_DIM)
    s344 = g_in.reshape(D_MODEL, N_DEV, 344).transpose(1, 0, 2)
    return dx, dxb, dict(s344=s344, s384=s384, s512=g_kvup, s1024=g_out.reshape(N_DEV, 256, D_MODEL),
                         norm=g_norm[0], q_norm=g_qnorm[0], kv_norm=g_kvnorm[0])


def _loss_head(x, g, target, tm):
    s, d = x.shape

    def body(x_ref, g_ref, t_ref, dx_ref, dxb_ref, dg_ref, loss_ref):
        i = pl.program_id(0)
        xv, gv = x_ref[...], g_ref[...]
        r = lax.rsqrt(jnp.mean(xv * xv, axis=-1, keepdims=True) + NORM_EPS)
        err = (xv * r) * gv - t_ref[...]
        part = 0.5 * jnp.sum(jnp.mean(err * err, axis=-1, keepdims=True), axis=0, keepdims=True)
        dx, dg = _rms_bwd(err * (1.0 / d), xv, gv, None)
        dx_ref[...] = dx
        dxb_ref[...] = dx.astype(BF)
        _accumulate(dg_ref, dg, i)
        _accumulate(loss_ref, jnp.broadcast_to(part, loss_ref.shape), i)

    row = pl.BlockSpec((tm, d), lambda i: (i, 0))
    one = pl.BlockSpec((1, d), lambda i: (0, 0))
    return pl.pallas_call(
        body, name="loss_head", grid=(s // tm,), in_specs=[row, one, row],
        out_specs=[row, row, one, pl.BlockSpec((8, 128), lambda i: (0, 0))],
        out_shape=[_sds((s, d), F32), _sds((s, d), BF), _sds((1, d), F32), _sds((8, 128), F32)],
        compiler_params=_params())(x, g.reshape(1, d), target)


def _rope_tables(pos):
    inv_freq = ROPE_BASE ** (-jnp.arange(0, QK_ROPE, 2, dtype=F32) / QK_ROPE)
    ang = pos.astype(F32)[:, None] * inv_freq
    cos, sin = jnp.cos(ang), jnp.sin(ang)
    idx = jnp.arange(QK_ROPE)
    perm = (idx[:, None] == (idx[None, :] + QK_ROPE // 2) % QK_ROPE).astype(F32)
    return jnp.concatenate([cos, cos], axis=1), jnp.concatenate([-sin, sin], axis=1), perm


def _local_step(x, pos, target, final_norm, get_w, put_g):
    s = x.shape[0]
    tm = min(512, s)
    rope = _rope_tables(pos)
    w0 = get_w(0, [])
    x1, sv0 = _pool_fwd(x, 0, w0, tm)
    w1 = get_w(1, [x1])
    x2, sv1 = _conv_fwd(x1, w1, tm)
    w2 = get_w(2, [x2])
    x3, sv2 = _mla_fwd(x2, w2, rope, tm)
    w3 = get_w(3, [x3])
    x4, sv3 = _pool_fwd(x3, 1, w3, tm)
    d4, d4b, g_final, loss = _loss_head(x4, final_norm, target, tm)
    d3, d3b, gp1 = _pool_bwd(d4, d4b, 1, sv3["w"], sv3, tm, [])
    dep = put_g(3, gp1)
    d2, d2b, gm = _mla_bwd(d3, d3b, w2, sv2, rope, tm, dep)
    dep = put_g(2, gm)
    sent = {}

    def send_conv(part):
        sent["dep"] = put_g(1, part)
        return sent["dep"]

    d1, d1b, gc = _conv_bwd(d2, d2b, sv1["w"], sv1, tm, dep, send_conv)
    d0, _, gp0 = _pool_bwd(d1, d1b, 0, sv0["w"], sv0, tm, sent["dep"], early=lambda part: put_g(0, part))
    put_g(4, {0: dict(gp0, final_norm=g_final[0]), 1: gc, 2: gm, 3: gp1})
    return loss[0, 0], d0


def _pack_groups(p):
    bf = lambda a: a.astype(BF)
    grp = lambda l: bf(p["pool_w_grp"][l].reshape(4 * 64, POOL_GROUP))
    return [[bf(p["pool_w_in"][0]), _pack_small(p, SMALL_ROWS_AG)],
            [grp(0), bf(p["pool_w_out"][0])],
            [bf(p["conv_w_in"][0])],
            [bf(p["conv_w_out"][0])],
            [bf(p[k][0]) for k in ("mla_w_in", "mla_w_q_up", "mla_w_kv_up", "mla_w_out")],
            [bf(p["pool_w_in"][1]), grp(1), bf(p["pool_w_out"][1])]]


_SMALL_SHARDED = ("pool_norm", "pool_scale", "mla_norm", "mla_q_norm", "mla_kv_norm", "conv_w")
_SMALL_REPLICATED = ("conv_norm", "final_norm")


def _pack_small(p, rows, with_replicated=False):
    parts = [p[k].reshape(-1) for k in _SMALL_SHARDED]
    if with_replicated:
        parts += [p[k].reshape(-1) for k in _SMALL_REPLICATED]
    flat = jnp.concatenate(parts)
    return jnp.pad(flat, (0, rows * 128 - flat.shape[0])).reshape(rows, 128)


_SMALL_SHARD_SHAPES = dict(pool_norm=(2, 128), pool_scale=(2, 256), mla_norm=(1, 128), mla_q_norm=(1, 48),
                           mla_kv_norm=(1, 32), conv_w=(1, 3, 256), conv_norm=(1, 1024), final_norm=(1024,))


def _unpack_small(buf, with_replicated=False):
    flat = buf.reshape(-1)
    out, off = {}, 0
    for k in _SMALL_SHARDED + (_SMALL_REPLICATED if with_replicated else ()):
        shp = _SMALL_SHARD_SHAPES[k]
        n = 1
        for d in shp:
            n *= d
        out[k] = flat[off:off + n].reshape(shp)
        off += n
    return out


def _small_views(gsmall):
    flat = gsmall.reshape(N_DEV, -1)

    def cols(off, rows, width):
        return flat[:, off:off + rows * width].reshape(N_DEV, rows, width).transpose(1, 0, 2).reshape(rows, N_DEV * width)

    return dict(pool_norm=cols(0, 2, 128), pool_scale=cols(256, 2, 256), mla_norm=cols(768, 1, 128)[0],
                q_norm=cols(896, 1, 48)[0], kv_norm=cols(944, 1, 32)[0], conv_w=cols(976, 3, 256))


def _pair_columns(g384):
    heads = g384.reshape(N_DEV, Q_RANK, 2, QK_DIM)
    nope = heads[..., :QK_NOPE].reshape(N_DEV, Q_RANK, 2 * QK_NOPE)
    rope = heads[..., QK_NOPE:].reshape(N_DEV, Q_RANK, 2 * QK_ROPE)
    return jnp.concatenate([nope, rope], axis=-1)


def _layer_weights(layer, bufs, small, conv_norm):
    if layer in (0, 3):
        l = 0 if layer == 0 else 1
        return dict(g_in=bufs[0], norm=small["pool_norm"][l], scale=small["pool_scale"][l])
    if layer == 1:
        return dict(g_in=bufs[0], norm=conv_norm.reshape(D_MODEL), conv_w=small["conv_w"])
    g344, g384, g512, g1024 = bufs
    w_in = g344.transpose(1, 0, 2).reshape(D_MODEL, N_DEV * 344)
    return dict(
        g512=g512, g1024=g1024,
        w_q=w_in[:, :Q_RANK], w_kv=w_in[:, Q_RANK:Q_RANK + KV_RANK],
        w_kr=w_in[:, Q_RANK + KV_RANK:Q_RANK + KV_RANK + QK_ROPE], w_z=w_in[:, Q_RANK + KV_RANK + QK_ROPE:],
        w_qh=g384.reshape(N_DEV, Q_RANK, 2, QK_DIM).transpose(0, 2, 1, 3).reshape(N_HEADS, Q_RANK, QK_DIM),
        w_qpair=_pair_columns(g384),
        norm=small["mla_norm"], q_norm=small["q_norm"], kv_norm=small["kv_norm"])


_GRAD_KEYS = {0: ("g_in", "g_grp", "g_out"), 3: ("g_in", "g_grp", "g_out"), 1: ("g_in", "g_out"), 2: ("s344", "s384", "s512", "s1024")}
_GRAD_PARAM = {0: dict(g_in="pool_w_in", g_grp="pool_w_grp", g_out="pool_w_out"), 1: dict(g_in="conv_w_in", g_out="conv_w_out"),
               2: dict(s344="mla_w_in", s384="mla_w_q_up", s512="mla_w_kv_up", s1024="mla_w_out")}
_GRAD_PARAM[3] = _GRAD_PARAM[0]


def _grad_group(layer, g):
    keys = tuple(k for k in _GRAD_KEYS[layer] if k in g)
    return keys, [g[k] for k in keys]


def _pack_small_grads(g):
    def split(a, rows, width):
        return a.reshape(rows, N_DEV, width).transpose(1, 0, 2).reshape(N_DEV, rows * width)

    rep = lambda a: jnp.broadcast_to(a.reshape(1, -1), (N_DEV, a.size))
    flat = jnp.concatenate([
        split(jnp.stack([g[0]["norm"], g[3]["norm"]]), 2, 128), split(jnp.stack([g[0]["scale"], g[3]["scale"]]), 2, 256),
        split(g[2]["norm"], 1, 128), split(g[2]["q_norm"], 1, 48), split(g[2]["kv_norm"], 1, 32), split(g[1]["conv_w"], 3, 256),
        rep(g[1]["norm"]), rep(g[0]["final_norm"])], axis=1)
    return jnp.pad(flat, ((0, 0), (0, SMALL_ROWS_RS * 128 - flat.shape[1]))).reshape(N_DEV, SMALL_ROWS_RS, 128)


def _peers(x, y, c):
    for k in range(1, N_DEV):
        px = 1 - x if k & 4 else x
        py = 1 - y if k & 2 else y
        pc = 1 - c if k & 1 else c
        yield k - 1, (px, py, pc), 4 * px + 2 * py + pc


_SIBLING = (0,)
_ICI_DIRECT = (1, 3, 5)


def _remote_copies(srcs, lands, send_sems, recv_sems, gather, ks=None):
    x, y, c = lax.axis_index("x"), lax.axis_index("y"), lax.axis_index("c")
    me = 4 * x + 2 * y + c
    copies = []
    for k, peer, pidx in _peers(x, y, c):
        if ks is not None and k not in ks:
            continue
        for a, (src, land) in enumerate(zip(srcs, lands)):
            copies.append(pltpu.make_async_remote_copy(
                src_ref=src if gather else src.at[pidx], dst_ref=land.at[me],
                send_sem=send_sems.at[a * (N_DEV - 1) + k], recv_sem=recv_sems.at[a * (N_DEV - 1) + k],
                device_id=peer, device_id_type=pl.DeviceIdType.MESH))
    return copies


def _relay_copies(lands, send_sems, recv_sems):
    x, y, c = lax.axis_index("x"), lax.axis_index("y"), lax.axis_index("c")
    copies = []
    for j, k in enumerate(_ICI_DIRECT):
        px = 1 - x if (k + 1) & 4 else x
        py = 1 - y if (k + 1) & 2 else y
        slot = 4 * px + 2 * py + c
        for a, land in enumerate(lands):
            copies.append(pltpu.make_async_remote_copy(
                src_ref=land.at[slot], dst_ref=land.at[slot],
                send_sem=send_sems.at[a * len(_ICI_DIRECT) + j], recv_sem=recv_sems.at[a * len(_ICI_DIRECT) + j],
                device_id=(x, y, 1 - c), device_id_type=pl.DeviceIdType.MESH))
    return copies


_HBM = pl.BlockSpec(memory_space=pltpu.HBM)
_SEM = pl.BlockSpec(memory_space=pltpu.SEMAPHORE)
_EFFECT = pltpu.SideEffectType.DATAFLOW_SIDE_EFFECTING


def _own_slabs(name, arrays, gather, dep):
    n, nd = len(arrays), len(dep)
    me = (4 * lax.axis_index("x") + 2 * lax.axis_index("y") + lax.axis_index("c")).astype(jnp.int32).reshape(1)

    def body(me_ref, *refs):
        for a in range(n):
            refs[n + nd + a][...] = refs[a][...]

    def slab(shape):
        return pl.BlockSpec((None,) + tuple(shape), lambda i, me_ref: (me_ref[0],) + (0,) * len(shape))

    def whole(shape):
        return pl.BlockSpec(tuple(shape), lambda i, me_ref: (0,) * len(shape))

    outs = [_sds(((N_DEV,) + a.shape) if gather else a.shape, a.dtype) for a in arrays]
    grid_spec = pltpu.PrefetchScalarGridSpec(
        num_scalar_prefetch=1, grid=(1,),
        in_specs=[whole(a.shape) if gather else slab(a.shape[1:]) for a in arrays] + [_ANY] * nd,
        out_specs=[slab(o.shape[1:]) for o in outs])
    return pl.pallas_call(body, name=name, grid_spec=grid_spec, out_shape=outs, compiler_params=_params())(me, *arrays, *dep)


def _exchange_start(name, arrays, lands, gather, ks=None):
    n = len(arrays)

    def body(*refs):
        srcs, lnds, send_sems, recv_sems, token = refs[:n], refs[n:2 * n], refs[2 * n], refs[2 * n + 1], refs[-1]
        for cp in _remote_copies(srcs, lnds, send_sems, recv_sems, gather, ks):
            cp.start()
        token[...] = jnp.zeros(token.shape, F32)

    sems = pltpu.SemaphoreType.DMA((n * (N_DEV - 1),))
    thru = [pltpu.HBM(a.shape, a.dtype) for a in list(arrays) + list(lands)]
    res = pl.pallas_call(
        body, name=name, in_specs=[_HBM] * (2 * n),
        out_specs=[_SEM, _SEM] + [_HBM] * (2 * n) + [pl.BlockSpec(memory_space=pltpu.VMEM)],
        out_shape=[sems, sems] + thru + [_sds((8, 128), F32)],
        input_output_aliases={i: 2 + i for i in range(2 * n)},
        compiler_params=pltpu.CompilerParams(has_side_effects=_EFFECT),
    )(*[pltpu.with_memory_space_constraint(a, pltpu.HBM) for a in list(arrays) + list(lands)])
    return res[0], res[1], list(res[2:2 + n]), list(res[2 + n:2 + 2 * n]), res[-1]


def _exchange_wait(name, send_sems, recv_sems, arrays, lands, after, gather):
    n = len(arrays)
    n_after = len(after)

    def body(*refs):
        srcs, lnds = refs[:n], refs[n:2 * n]
        copies = _remote_copies(srcs, lnds, refs[2 * n], refs[2 * n + 1], gather)
        for cp in copies:
            cp.wait_send()
        for cp in copies:
            cp.wait_recv()

    thru = [pltpu.HBM(a.shape, a.dtype) for a in list(arrays) + list(lands)]
    res = pl.pallas_call(
        body, name=name, in_specs=[_HBM] * (2 * n) + [_SEM, _SEM] + [pl.BlockSpec(memory_space=pl.ANY)] * n_after,
        out_specs=[_HBM] * (2 * n), out_shape=thru, input_output_aliases={i: i for i in range(2 * n)},
        compiler_params=pltpu.CompilerParams(has_side_effects=_EFFECT),
    )(*arrays, *lands, send_sems, recv_sems, *after)
    return list(res[n:])


def _gather_relay(name, send_sems, recv_sems, arrays, lands, after):
    n, n_after = len(arrays), len(after)

    def body(*refs):
        srcs, lnds, first_send, first_recv = refs[:n], refs[n:2 * n], refs[2 * n], refs[2 * n + 1]
        send2, recv2, token = refs[2 * n + 2 + n_after], refs[2 * n + 3 + n_after], refs[-1]
        for cp in _remote_copies(srcs, lnds, first_send, first_recv, True, _ICI_DIRECT):
            cp.wait_recv()
        for cp in _relay_copies(lnds, send2, recv2):
            cp.start()
        token[...] = jnp.zeros(token.shape, F32)

    sems = pltpu.SemaphoreType.DMA((n * len(_ICI_DIRECT),))
    thru = [pltpu.HBM(a.shape, a.dtype) for a in list(arrays) + list(lands)]
    res = pl.pallas_call(
        body, name=name, in_specs=[_HBM] * (2 * n) + [_SEM, _SEM] + [_ANY] * n_after,
        out_specs=[_SEM, _SEM] + [_HBM] * (2 * n) + [pl.BlockSpec(memory_space=pltpu.VMEM)],
        out_shape=[sems, sems] + thru + [_sds((8, 128), F32)],
        input_output_aliases={i: 2 + i for i in range(2 * n)},
        compiler_params=pltpu.CompilerParams(has_side_effects=_EFFECT),
    )(*arrays, *lands, send_sems, recv_sems, *after)
    return res[0], res[1], list(res[2:2 + n]), list(res[2 + n:2 + 2 * n]), res[-1]


def _gather_wait2(name, send_sems, recv_sems, send2, recv2, arrays, lands, after):
    n, n_after = len(arrays), len(after)

    def body(*refs):
        srcs, lnds = refs[:n], refs[n:2 * n]
        s1, r1, s2, r2 = refs[2 * n:2 * n + 4]
        for cp in _remote_copies(srcs, lnds, s1, r1, True, _SIBLING + _ICI_DIRECT):
            cp.wait_send()
        for cp in _remote_copies(srcs, lnds, s1, r1, True, _SIBLING):
            cp.wait_recv()
        relay = _relay_copies(lnds, s2, r2)
        for cp in relay:
            cp.wait_send()
        for cp in relay:
            cp.wait_recv()

    thru = [pltpu.HBM(a.shape, a.dtype) for a in list(arrays) + list(lands)]
    res = pl.pallas_call(
        body, name=name, in_specs=[_HBM] * (2 * n) + [_SEM] * 4 + [_ANY] * n_after,
        out_specs=[_HBM] * (2 * n), out_shape=thru, input_output_aliases={i: i for i in range(2 * n)},
        compiler_params=pltpu.CompilerParams(has_side_effects=_EFFECT),
    )(*arrays, *lands, send_sems, recv_sems, send2, recv2, *after)
    return list(res[n:])


def _adamw_math(g, w, m, v):
    m = ADAM_B1 * m + (1.0 - ADAM_B1) * g
    v = ADAM_B2 * v + (1.0 - ADAM_B2) * (g * g)
    m_hat = m / (1.0 - ADAM_B1 ** ADAM_STEP)
    v_hat = v / (1.0 - ADAM_B2 ** ADAM_STEP)
    delta = -ADAM_LR * (m_hat / (jnp.sqrt(v_hat) + ADAM_EPS) + ADAM_WD * w)
    return delta, m, v


def _sum_adamw(name, recv, row_off, w, m, v, tr, layer=0):
    width = recv.shape[-1]
    w2, m2, v2 = (a.reshape(a.shape[0], -1, width) for a in (w, m, v))
    rows = w2.shape[1]
    base = row_off // tr

    def body(r_ref, w_ref, m_ref, v_ref, g_ref, d_ref, mo_ref, vo_ref):
        g = r_ref[0].astype(F32)
        for src in range(1, N_DEV):
            g = g + r_ref[src].astype(F32)
        delta, mn, vn = _adamw_math(g, w_ref[...], m_ref[...], v_ref[...])
        g_ref[...] = g
        d_ref[...] = delta
        mo_ref[...] = mn
        vo_ref[...] = vn

    blk = pl.BlockSpec((tr, width), lambda i: (i, 0))
    wblk = pl.BlockSpec((None, tr, width), lambda i: (layer, i, 0))
    return pl.pallas_call(
        body, name=name, grid=(rows // tr,),
        in_specs=[pl.BlockSpec((N_DEV, tr, width), lambda i: (0, base + i, 0)), wblk, wblk, wblk],
        out_specs=[blk] * 4, out_shape=[_sds((rows, width), F32)] * 4, compiler_params=_params())(recv, w2, m2, v2)


_WEIGHTS = ("pool_norm", "pool_w_in", "pool_w_grp", "pool_scale", "pool_w_out", "conv_norm", "conv_w_in", "conv_w", "conv_w_out",
            "mla_norm", "mla_w_in", "mla_q_norm", "mla_w_q_up", "mla_kv_norm", "mla_w_kv_up", "mla_w_out", "final_norm")


def _step(x, positions, loss_target, p, m, v):
    gathers, tokens, dep = [], [], []
    for group, arrays in enumerate(_pack_groups(p)):
        lands = _own_slabs(f"gather{group}_own", arrays, True, dep)
        ks = _SIBLING + _ICI_DIRECT if group < TWO_LEVEL_GROUPS else None
        ssem, rsem, arrays, lands, token = _exchange_start(f"gather{group}_start", arrays, lands, True, ks)
        gathers.append((ssem, rsem, arrays, lands))
        tokens.append(token)
        dep = [token]
    state = {}

    def wait_group(group, after):
        if group >= TWO_LEVEL_GROUPS:
            return _exchange_wait(f"gather{group}_wait", *gathers[group], after, True)
        ssem, rsem, arrays, lands = gathers[group]
        send2, recv2, arrays, lands, token = _gather_relay(f"gather{group}_relay", ssem, rsem, arrays, lands, after)
        return _gather_wait2(f"gather{group}_wait", ssem, rsem, send2, recv2, arrays, lands, [token])

    def get_w(layer, after):
        if layer == 0:
            bufs = wait_group(0, list(tokens))
            state["small"] = _small_views(bufs[1])
            rest = lambda later: dict(zip(("g_grp", "g_out"), wait_group(1, later)))
        elif layer == 1:
            bufs = wait_group(2, after)
            rest = lambda later: dict(g_out=wait_group(3, later)[0])
        elif layer == 2:
            bufs = wait_group(4, after)
        else:
            bufs = wait_group(5, after)
            rest = lambda later: dict(g_grp=bufs[1], g_out=bufs[2])
        w = _layer_weights(layer, bufs, state["small"], p["conv_norm"])
        if layer != 2:
            w["rest"] = rest
        return w

    scatters = []

    def put_g(layer, g):
        if layer == 4:
            keys, arrays = ("small",), [_pack_small_grads(g)]
        else:
            keys, arrays = _grad_group(layer, g)
        n = len(scatters)
        lands = _own_slabs(f"scatter{n}_own", arrays, False, [])
        ssem, rsem, arrays, lands, token = _exchange_start(f"scatter{n}_start", arrays, lands, False)
        scatters.append((layer, keys, (ssem, rsem, arrays, lands)))
        tokens.append(token)
        return [token]

    loss, grad_x = _local_step(x[0], positions[0], loss_target[0], p["final_norm"], get_w, put_g)

    res, after = {}, [tokens[-1]]
    for n, (layer, keys, handles) in enumerate(scatters):
        recv = _exchange_wait(f"scatter{n}_wait", *handles, after, False)
        if layer == 4:
            break
        l = 1 if layer == 3 else 0
        for key, buf in zip(keys, recv):
            name = _GRAD_PARAM[layer][key]
            tr = min(256, buf.shape[1]) if name != "mla_w_q_up" else buf.shape[1]
            res[name, l] = _sum_adamw(f"adam_{name}{l}", buf, 0, p[name], m[name], v[name], tr, l)
        after = [res[name, l][1]]
    small = _sum_adamw("adam_small", recv[0], 0, _pack_small(p, SMALL_ROWS_RS, True)[None], _pack_small(m, SMALL_ROWS_RS, True)[None],
                       _pack_small(v, SMALL_ROWS_RS, True)[None], SMALL_ROWS_RS)
    small = [_unpack_small(a, True) for a in small]
    final = {k: tuple(part[k] for part in small) for k in _SMALL_SHARDED + _SMALL_REPLICATED}
    for k in _WEIGHTS:
        if k not in final:
            layers = [res[k, l] for l in range(p[k].shape[0])]
            final[k] = tuple(jnp.stack([lay[part] for lay in layers]).reshape(p[k].shape) for part in range(4))
    res = final

    loss = lax.psum(loss, ("x", "y", "c"))
    out = [loss, grad_x[None]]
    for part in range(4):
        out += [res[k][part] for k in _WEIGHTS]
    return tuple(out)


def kernel(x, positions, pool_norm, pool_w_in, pool_w_grp, pool_scale, pool_w_out, conv_norm, conv_w_in, conv_w, conv_w_out, mla_norm, mla_w_in, mla_q_norm, mla_w_q_up, mla_kv_norm, mla_w_kv_up, mla_w_out, final_norm, loss_target, m_pool_norm, m_pool_w_in, m_pool_w_grp, m_pool_scale, m_pool_w_out, m_conv_norm, m_conv_w_in, m_conv_w, m_conv_w_out, m_mla_norm, m_mla_w_in, m_mla_q_norm, m_mla_w_q_up, m_mla_kv_norm, m_mla_w_kv_up, m_mla_w_out, m_final_norm, v_pool_norm, v_pool_w_in, v_pool_w_grp, v_pool_scale, v_pool_w_out, v_conv_norm, v_conv_w_in, v_conv_w, v_conv_w_out, v_mla_norm, v_mla_w_in, v_mla_q_norm, v_mla_w_q_up, v_mla_kv_norm, v_mla_w_kv_up, v_mla_w_out, v_final_norm):
    p = dict(pool_norm=pool_norm, pool_w_in=pool_w_in, pool_w_grp=pool_w_grp, pool_scale=pool_scale, pool_w_out=pool_w_out,
             conv_norm=conv_norm, conv_w_in=conv_w_in, conv_w=conv_w, conv_w_out=conv_w_out, mla_norm=mla_norm, mla_w_in=mla_w_in,
             mla_q_norm=mla_q_norm, mla_w_q_up=mla_w_q_up, mla_kv_norm=mla_kv_norm, mla_w_kv_up=mla_w_kv_up, mla_w_out=mla_w_out,
             final_norm=final_norm)
    m = dict(pool_norm=m_pool_norm, pool_w_in=m_pool_w_in, pool_w_grp=m_pool_w_grp, pool_scale=m_pool_scale, pool_w_out=m_pool_w_out,
             conv_norm=m_conv_norm, conv_w_in=m_conv_w_in, conv_w=m_conv_w, conv_w_out=m_conv_w_out, mla_norm=m_mla_norm,
             mla_w_in=m_mla_w_in, mla_q_norm=m_mla_q_norm, mla_w_q_up=m_mla_w_q_up, mla_kv_norm=m_mla_kv_norm,
             mla_w_kv_up=m_mla_w_kv_up, mla_w_out=m_mla_w_out, final_norm=m_final_norm)
    v = dict(pool_norm=v_pool_norm, pool_w_in=v_pool_w_in, pool_w_grp=v_pool_w_grp, pool_scale=v_pool_scale, pool_w_out=v_pool_w_out,
             conv_norm=v_conv_norm, conv_w_in=v_conv_w_in, conv_w=v_conv_w, conv_w_out=v_conv_w_out, mla_norm=v_mla_norm,
             mla_w_in=v_mla_w_in, mla_q_norm=v_mla_q_norm, mla_w_q_up=v_mla_w_q_up, mla_kv_norm=v_mla_kv_norm,
             mla_w_kv_up=v_mla_w_kv_up, mla_w_out=v_mla_w_out, final_norm=v_final_norm)
    return _step(x, positions, loss_target, p, m, v)
```

```python
import functools

import jax
import jax.numpy as jnp
from jax import lax
from jax.experimental import pallas as pl
from jax.experimental.pallas import tpu as pltpu

BF = jnp.bfloat16
F32 = jnp.float32

N_DEV = 8
D_MODEL = 1024
D_INNER = 2048
POOL_WINDOWS = (2, 4, 8, 16)
POOL_GROUP = 512
N_HEADS = 16
QK_NOPE = 128
QK_ROPE = 64
QK_DIM = QK_NOPE + QK_ROPE
V_DIM = 128
Q_RANK = 384
KV_RANK = 256
ATTN_SCALE = QK_DIM ** -0.5
LOG2_E = 1.4426950408889634
LN_2 = 0.6931471805599453
Q_PRESCALE = ATTN_SCALE * LOG2_E
ATTN_TILE = 512
ATTN_HEADS_PER_STEP = 2
ROPE_BASE = 10000.0
NORM_EPS = 1e-6
NEG_BIG = -1e30

ADAM_LR = 0.001
ADAM_B1 = 0.9
ADAM_B2 = 0.999
ADAM_EPS = 1e-08
ADAM_WD = 0.01
ADAM_STEP = 10

VMEM_LIMIT_BYTES = 52 * 1024 * 1024
IN_PROJ_ROWS = 1024
POOL_HALO = 32
CONV_HALO = 16

NN = (((1,), (0,)), ((), ()))
NT = (((1,), (1,)), ((), ()))
TN = (((0,), (0,)), ((), ()))

TWO_LEVEL_GROUPS = 3
SMALL_ROWS_AG = 16
SMALL_ROWS_RS = 32


def _sds(shape, dtype):
    return jax.ShapeDtypeStruct(tuple(shape), dtype)


def _params():
    return pltpu.CompilerParams(vmem_limit_bytes=VMEM_LIMIT_BYTES)


_ANY = pl.BlockSpec(memory_space=pl.ANY)


def _dot(a, b, dims):
    return lax.dot_general(a, b, dims, preferred_element_type=F32)


def _sig(z):
    return 1.0 / (1.0 + jnp.exp(-z))


def _silu_and_grad(z):
    sig = _sig(z)
    return z * sig, sig * (1.0 + z * (1.0 - sig))


def _to_row(col):
    return jnp.broadcast_to(col, (col.shape[0], 128)).T[0:1, :]


def _rope_swap(x, p):
    pb = p.astype(BF)
    hi = x.astype(BF)
    r1 = x - hi.astype(F32)
    mid = r1.astype(BF)
    lo = (r1 - mid.astype(F32)).astype(BF)
    return (_dot(hi, pb, NN) + _dot(mid, pb, NN)) + _dot(lo, pb, NN)


def _rope_fwd(x, cosf, sinf, p):
    return x * cosf + _rope_swap(x, p) * sinf


def _rope_bwd(dy, cosf, sinf, p):
    return dy * cosf + _rope_swap(dy * sinf, p)


def _rms_bwd(dxn, x, g, res):
    r = lax.rsqrt(jnp.mean(x * x, axis=-1, keepdims=True) + NORM_EPS)
    v = dxn * g
    dx = r * v - x * ((r * r * r) * jnp.mean(v * x, axis=-1, keepdims=True))
    if res is not None:
        dx = dx + res
    dg = jnp.sum(dxn * (x * r), axis=0, keepdims=True)
    return dx, dg


def _accumulate(ref, val, step):
    @pl.when(step == 0)
    def _():
        ref[...] = val

    @pl.when(step > 0)
    def _():
        ref[...] += val


def _mm(name, grid, ins, in_specs, outs, out_specs, dims, epi, red=None, acc_shape=None):
    n_in, n_out = len(ins), len(outs)
    n_red = None if red is None else grid[red]

    def body(*refs):
        in_refs, out_refs = refs[:n_in], refs[n_in:n_in + n_out]
        pids = tuple(pl.program_id(ax) for ax in range(len(grid)))
        a, b = in_refs[0][...], in_refs[1][...]
        if a.ndim == 3:
            a = a.reshape(-1, a.shape[-1])
        if b.ndim == 3:
            b = b.reshape(-1, b.shape[-1])
        part = _dot(a.astype(BF), b.astype(BF), dims)
        if red is None:
            epi(part, in_refs[2:], out_refs, pids)
        else:
            acc = refs[n_in + n_out]
            k = pids[red]
            _accumulate(acc, part, k)

            @pl.when(k == n_red - 1)
            def _():
                epi(acc[...], in_refs[2:], out_refs, pids)

    scratch = [] if red is None else [pltpu.VMEM(acc_shape, F32)]
    return pl.pallas_call(body, name=name, grid=grid, in_specs=in_specs, out_specs=out_specs, out_shape=outs,
                          scratch_shapes=scratch, compiler_params=_params())(*ins)


def _store(part, extra, outs, pids):
    outs[0][...] = part.astype(outs[0].dtype)


def _rms_fwd(name, x, g, tm):
    s, d = x.shape

    def body(x_ref, g_ref, o_ref):
        xv = x_ref[...]
        r = lax.rsqrt(jnp.mean(xv * xv, axis=-1, keepdims=True) + NORM_EPS)
        o_ref[...] = ((xv * r) * g_ref[...]).astype(BF)

    return pl.pallas_call(body, name=name, grid=(s // tm,),
                          in_specs=[pl.BlockSpec((tm, d), lambda i: (i, 0)), pl.BlockSpec((1, d), lambda i: (0, 0))],
                          out_specs=pl.BlockSpec((tm, d), lambda i: (i, 0)), out_shape=_sds((s, d), BF),
                          compiler_params=_params())(x, g.reshape(1, d))


def _norm_in_proj(name, x, g, wbuf, w_index, n_j, tm):
    s = x.shape[0]
    ti = IN_PROJ_ROWS if s % IN_PROJ_ROWS == 0 else tm

    def body(x_ref, g_ref, w_ref, h_ref, xn_ref):
        @pl.when(pl.program_id(1) == 0)
        def _():
            xv = x_ref[...]
            r = lax.rsqrt(jnp.mean(xv * xv, axis=-1, keepdims=True) + NORM_EPS)
            xn_ref[...] = ((xv * r) * g_ref[...]).astype(BF)

        h_ref[...] = _dot(xn_ref[...], w_ref[...], NN).astype(BF)

    row = lambda i, j: (i, 0)
    return pl.pallas_call(
        body, name=name, grid=(s // ti, n_j),
        in_specs=[pl.BlockSpec((ti, D_MODEL), row), pl.BlockSpec((1, D_MODEL), lambda i, j: (0, 0)),
                  pl.BlockSpec((None, D_MODEL, 512), w_index)],
        out_specs=[pl.BlockSpec((ti, 512), lambda i, j: (i, j)), pl.BlockSpec((ti, D_MODEL), row)],
        out_shape=[_sds((s, n_j * 512), BF), _sds((s, D_MODEL), BF)], compiler_params=_params())(x, g.reshape(1, D_MODEL), wbuf)


def _tn(name, a, b, out_shape, out_block, out_index, a_cols, b_cols, grid, a_index, b_index, dep=()):
    s = a.shape[-2]
    a_block = (s, a_cols) if a.ndim == 2 else (None, s, a_cols)
    b_block = (s, b_cols) if b.ndim == 2 else (None, s, b_cols)

    def epi(part, extra, outs, pids):
        outs[0][...] = part.astype(BF).reshape(outs[0].shape)

    return _mm(name, grid, [a, b] + list(dep), [pl.BlockSpec(a_block, a_index), pl.BlockSpec(b_block, b_index)] + [_ANY] * len(dep),
               [_sds(out_shape, BF)], [pl.BlockSpec(out_block, out_index)], TN, epi)[0]


def _pool_window_fwd(name, h, tm):
    s = h.shape[0]
    hb = POOL_HALO

    def body(u_ref, halo_ref, o_ref, e_ref, a_ref, b_ref):
        i = pl.program_id(0)
        row = lax.broadcasted_iota(jnp.int32, (tm, 1), 0) + i * tm
        for g, w in enumerate(POOL_WINDOWS):
            cs = slice(g * POOL_GROUP, (g + 1) * POOL_GROUP)
            e_ref[0:hb, :] = jnp.where(i > 0, halo_ref[:, cs].astype(F32), 0.0)
            e_ref[hb:, :] = u_ref[:, cs].astype(F32)
            src, bufs = e_ref, (a_ref, b_ref)
            for lv in range(1, w.bit_length()):
                dst, st, sh = bufs[(lv - 1) % 2], 8 * lv, 2 ** (lv - 1)
                n = hb + tm - st
                dst[st:, :] = src[st:, :] + src[pl.ds(st - sh, n), :]
                src = dst
            cnt = jnp.minimum(row + 1, w).astype(F32)
            o_ref[:, cs] = (src[hb:, :] / cnt - u_ref[:, cs].astype(F32)).astype(BF)

    per = tm // hb
    return pl.pallas_call(
        body, name=name, grid=(s // tm,),
        in_specs=[pl.BlockSpec((tm, D_INNER), lambda i: (i, 0)),
                  pl.BlockSpec((hb, D_INNER), lambda i: (jnp.maximum(i * per - 1, 0), 0))],
        out_specs=pl.BlockSpec((tm, D_INNER), lambda i: (i, 0)), out_shape=_sds((s, D_INNER), BF),
        scratch_shapes=[pltpu.VMEM((hb + tm, POOL_GROUP), F32)] * 3, compiler_params=_params())(h, h)


def _pool_window_bwd(name, dp, tm, dh):
    s = dp.shape[0]
    nt = s // tm
    hb = POOL_HALO

    def body(d_ref, halo_ref, dh_in_ref, o_ref, e_ref, a_ref, b_ref):
        i = pl.program_id(0)
        row = lax.broadcasted_iota(jnp.int32, (tm, 1), 0) + i * tm
        hrow = lax.broadcasted_iota(jnp.int32, (hb, 1), 0) + (i + 1) * tm
        for g, w in enumerate(POOL_WINDOWS):
            cs = slice(g * POOL_GROUP, (g + 1) * POOL_GROUP)
            e_ref[0:tm, :] = d_ref[:, cs].astype(F32) / jnp.minimum(row + 1, w).astype(F32)
            e_ref[tm:, :] = jnp.where(i < nt - 1, halo_ref[:, cs].astype(F32) / jnp.minimum(hrow + 1, w).astype(F32), 0.0)
            src, bufs = e_ref, (a_ref, b_ref)
            for lv in range(1, w.bit_length()):
                dst, sh = bufs[(lv - 1) % 2], 2 ** (lv - 1)
                n = tm + hb - 8 * lv
                dst[0:n, :] = src[0:n, :] + src[pl.ds(sh, n), :]
                src = dst
            o_ref[:, cs] = (src[0:tm, :] - d_ref[:, cs].astype(F32)).astype(BF)

    per = tm // hb
    last = s // hb - 1
    return pl.pallas_call(
        body, name=name, grid=(nt,),
        in_specs=[pl.BlockSpec((tm, D_INNER), lambda i: (i, 0)),
                  pl.BlockSpec((hb, D_INNER), lambda i: (jnp.minimum((i + 1) * per, last), 0)), _ANY],
        out_specs=pl.BlockSpec((tm, D_INNER), lambda i: (i, 0)), out_shape=_sds(dh.shape, BF),
        input_output_aliases={2: 0},
        scratch_shapes=[pltpu.VMEM((hb + tm, POOL_GROUP), F32)] * 3, compiler_params=_params())(dp, dp, dh)


def _grp_block():
    return pl.BlockSpec((N_DEV, 64, POOL_GROUP), lambda i, g: (0, g, 0))


def _pool_fwd(x, l, w, tm):
    s = x.shape[0]
    nt = s // tm
    n = f"pool{l}"
    h, xn = _norm_in_proj(n + "_in", x, w["norm"], w["g_in"], lambda i, j: (j, 0, 0), 8, tm)
    pooled = _pool_window_fwd(n + "_win", h, tm)
    w = dict(w, **w["rest"]([h]))

    def gate(part, extra, outs, pids):
        z = extra[0][...].astype(F32)
        outs[0][...] = ((part * extra[1][...]) * (z * _sig(z))).astype(BF)

    tg = IN_PROJ_ROWS if s % IN_PROJ_ROWS == 0 else tm
    (gated,) = _mm(n + "_grp", (s // tg, 4), [pooled, w["g_grp"], h, w["scale"].reshape(1, D_INNER)],
                   [pl.BlockSpec((tg, 512), lambda i, g: (i, g)), _grp_block(),
                    pl.BlockSpec((tg, 512), lambda i, g: (i, 4 + g)), pl.BlockSpec((1, 512), lambda i, g: (0, g))],
                   [_sds((s, D_INNER), BF)], [pl.BlockSpec((tg, 512), lambda i, g: (i, g))], NN, gate)
    y = _out_proj(n + "_out", gated, w["g_out"], 0, x, tm)
    return y, dict(x=x, xn=xn, h=h, pooled=pooled, gated=gated, w=w)


def _out_proj(name, gated, g1024, row_block, x, tm):
    s = x.shape[0]

    def epi(part, extra, outs, pids):
        outs[0][...] = part + extra[0][...]

    row = pl.BlockSpec((tm, D_MODEL), lambda i: (i, 0))
    return _mm(name, (s // tm,), [gated, g1024, x],
               [pl.BlockSpec((tm, D_INNER), lambda i: (i, 0)), pl.BlockSpec((N_DEV, 256, D_MODEL), lambda i: (0, row_block, 0)), row],
               [_sds((s, D_MODEL), F32)], [row], NN, epi)[0]


def _w_out_nt_block(row_block):
    return pl.BlockSpec((2, 256, D_MODEL), lambda j, i: (j, row_block, 0))


def _in_proj_bwd(name, dh, wbuf, w_index, n_k, x, g, dy, tm, dep=()):
    s = x.shape[0]
    tm = IN_PROJ_ROWS if s % IN_PROJ_ROWS == 0 else tm

    def epi(acc, extra, outs, pids):
        dx, dg = _rms_bwd(acc, extra[0][...], extra[1][...], extra[2][...])
        outs[0][...] = dx
        outs[1][...] = dx.astype(BF)
        _accumulate(outs[2], dg, pids[0])

    row = lambda i, k: (i, 0)
    return _mm(name, (s // tm, n_k), [dh, wbuf, x, g.reshape(1, D_MODEL), dy] + list(dep),
               [pl.BlockSpec((tm, 512), lambda i, k: (i, k)), pl.BlockSpec((None, D_MODEL, 512), w_index),
                pl.BlockSpec((tm, D_MODEL), row), pl.BlockSpec((1, D_MODEL), lambda i, k: (0, 0)), pl.BlockSpec((tm, D_MODEL), row)]
               + [_ANY] * len(dep),
               [_sds((s, D_MODEL), F32), _sds((s, D_MODEL), BF), _sds((1, D_MODEL), F32)],
               [pl.BlockSpec((tm, D_MODEL), row), pl.BlockSpec((tm, D_MODEL), row), pl.BlockSpec((1, D_MODEL), lambda i, k: (0, 0))],
               NT, epi, red=1, acc_shape=(tm, D_MODEL))


def _w_out_grad(name, gated, dyb):
    s = gated.shape[0]
    return _tn(name, gated, dyb, (D_INNER, D_MODEL), (512, D_MODEL), lambda i: (i, 0), 512, D_MODEL, (4,),
               lambda i: (0, i), lambda i: (0, 0))


def _pool_bwd(dy, dyb, l, w, sv, tm, dep, early=None):
    s = dy.shape[0]
    nt = s // tm
    n = f"pool{l}b"
    h, pooled = sv["h"], sv["pooled"]
    scale = w["scale"].reshape(1, D_INNER)

    def gate_bwd(part, extra, outs, pids):
        z, sc = extra[0][...].astype(F32), extra[1][...]
        wg = extra[3][...].reshape(POOL_GROUP, POOL_GROUP)
        mpv = _dot(extra[2][...], wg, NN)
        sz, dsz = _silu_and_grad(z)
        dm = part * sz
        dmp = (dm * sc).astype(BF)
        outs[0][...] = dmp
        outs[1][...] = (part * (mpv * sc) * dsz).astype(BF)
        _accumulate(outs[2], jnp.sum(dm * mpv, axis=0, keepdims=True), pids[1])
        outs[3][...] = _dot(dmp, wg, NT).astype(BF)

    tile = lambda j, i: (i, j)
    dmp, dz, dscale, dpool = _mm(
        n + "_out", (4, nt), [dyb, w["g_out"], h, scale, pooled, w["g_grp"]] + dep,
        [pl.BlockSpec((tm, D_MODEL), lambda j, i: (i, 0)), _w_out_nt_block(0),
         pl.BlockSpec((tm, 512), lambda j, i: (i, 4 + j)), pl.BlockSpec((1, 512), lambda j, i: (0, j)), pl.BlockSpec((tm, 512), tile),
         pl.BlockSpec((N_DEV, 64, POOL_GROUP), lambda j, i: (0, j, 0))] + [_ANY] * len(dep),
        [_sds((s, D_INNER), BF), _sds((s, 2 * D_INNER), BF), _sds((1, D_INNER), F32), _sds((s, D_INNER), BF)],
        [pl.BlockSpec((tm, 512), tile), pl.BlockSpec((tm, 512), lambda j, i: (i, 4 + j)), pl.BlockSpec((1, 512), lambda j, i: (0, j)),
         pl.BlockSpec((tm, 512), tile)],
        NT, gate_bwd)
    g_out = _w_out_grad(n + "_gout", sv["gated"], dyb).reshape(N_DEV, 256, D_MODEL)
    g_grp = _tn(n + "_ggrp", pooled, dmp, (N_DEV, 256, 512), (N_DEV, 64, 512), lambda g: (0, g, 0),
                512, 512, (4,), lambda g: (0, g), lambda g: (0, g))
    dep = early(dict(g_grp=g_grp, g_out=g_out)) if early is not None else ()
    dh = _pool_window_bwd(n + "_win", dpool, tm, dz)
    g_in = _tn(n + "_gin", sv["xn"], dh, (N_DEV, D_MODEL, 512), (None, D_MODEL, 512), lambda j: (j, 0, 0),
               D_MODEL, 512, (8,), lambda j: (0, 0), lambda j: (0, j), dep)
    dep = early(dict(g_in=g_in)) if early is not None else ()
    dx, dxb, dnorm = _in_proj_bwd(n + "_in", dh, w["g_in"], lambda i, k: (k, 0, 0), 8, sv["x"], w["norm"], dy, tm, dep)
    return dx, dxb, dict(g_in=g_in, g_grp=g_grp, g_out=g_out, norm=dnorm[0], scale=dscale[0])


def _conv_in_index(i, j):
    return (j // 2, 0, j % 2)


def _conv_fwd(x, w, tm):
    s = x.shape[0]
    nt = s // tm
    h, xn = _norm_in_proj("conv_in", x, w["norm"], w["g_in"], _conv_in_index, 16, tm)
    per = tm // CONV_HALO

    def body(b_ref, c_ref, h_ref, z_ref, cp_ref, hp_ref, w_ref, o_ref, e_ref):
        i = pl.program_id(0)
        ch = c_ref[...].astype(F32) * h_ref[...].astype(F32)
        e_ref[0:CONV_HALO, :] = jnp.where(i > 0, cp_ref[...].astype(F32) * hp_ref[...].astype(F32), 0.0)
        e_ref[CONV_HALO:, :] = ch
        co = (w_ref[2:3, :] * ch + w_ref[1:2, :] * e_ref[pl.ds(CONV_HALO - 1, tm), :]
              + w_ref[0:1, :] * e_ref[pl.ds(CONV_HALO - 2, tm), :])
        z = z_ref[...].astype(F32)
        o_ref[...] = ((b_ref[...].astype(F32) * co) * (z * _sig(z))).astype(BF)

    def col(q):
        return pl.BlockSpec((tm, 512), lambda i, j: (i, 4 * q + j))

    def prev(q):
        return pl.BlockSpec((CONV_HALO, 512), lambda i, j: (jnp.maximum(i * per - 1, 0), 4 * q + j))

    gated = pl.pallas_call(
        body, name="conv_mix", grid=(nt, 4),
        in_specs=[col(0), col(1), col(2), col(3), prev(1), prev(2), pl.BlockSpec((3, 512), lambda i, j: (0, j))],
        out_specs=pl.BlockSpec((tm, 512), lambda i, j: (i, j)), out_shape=_sds((s, D_INNER), BF),
        scratch_shapes=[pltpu.VMEM((CONV_HALO + tm, 512), F32)], compiler_params=_params())(h, h, h, h, h, h, w["conv_w"])
    w = dict(w, **w["rest"]([gated]))
    y = _out_proj("conv_out", gated, w["g_out"], 0, x, tm)
    return y, dict(x=x, xn=xn, h=h, gated=gated, w=w)


def _conv_bwd(dy, dyb, w, sv, tm, dep, early):
    s = dy.shape[0]
    nt = s // tm
    h = sv["h"]
    per = tm // CONV_HALO
    last = s // CONV_HALO - 1
    n_dep = len(dep)

    def body(dy_ref, dyn_ref, wo_ref, b_ref, c_ref, h_ref, z_ref, cp_ref, hp_ref, bn_ref, zn_ref, w_ref, *rest):
        dall_ref, dw_ref, e_ref, f_ref = rest[n_dep:]
        db_ref, dc_ref, dh_ref, dz_ref = (dall_ref.at[:, q * 512:(q + 1) * 512] for q in range(4))
        i = pl.program_id(1)
        wo = wo_ref[...].reshape(512, D_MODEL)
        dg_tile = _dot(dy_ref[...], wo, NT)
        dg_next = _dot(dyn_ref[...], wo, NT)
        w0, w1, w2 = w_ref[0:1, :], w_ref[1:2, :], w_ref[2:3, :]
        c, hh, b = c_ref[...].astype(F32), h_ref[...].astype(F32), b_ref[...].astype(F32)
        ch = c * hh
        e_ref[0:CONV_HALO, :] = jnp.where(i > 0, cp_ref[...].astype(F32) * hp_ref[...].astype(F32), 0.0)
        e_ref[CONV_HALO:, :] = ch
        ch1 = e_ref[pl.ds(CONV_HALO - 1, tm), :]
        ch2 = e_ref[pl.ds(CONV_HALO - 2, tm), :]
        co = w2 * ch + w1 * ch1 + w0 * ch2
        sz, dsz = _silu_and_grad(z_ref[...].astype(F32))
        dgv = dg_tile
        dyv = dgv * sz
        dz_ref[...] = (dgv * (b * co) * dsz).astype(BF)
        db_ref[...] = (dyv * co).astype(BF)
        dco = dyv * b
        zn = zn_ref[...].astype(F32)
        f_ref[0:tm, :] = dco
        f_ref[tm:, :] = jnp.where(i < nt - 1, dg_next * (zn * _sig(zn)) * bn_ref[...].astype(F32), 0.0)
        dch = w2 * dco + w1 * f_ref[pl.ds(1, tm), :] + w0 * f_ref[pl.ds(2, tm), :]
        dc_ref[...] = (dch * hh).astype(BF)
        dh_ref[...] = (dch * c).astype(BF)
        for tap, shifted in enumerate((ch2, ch1, ch)):
            _accumulate(dw_ref.at[tap:tap + 1, :], jnp.sum(dco * shifted, axis=0, keepdims=True), i)

    def col(q):
        return pl.BlockSpec((tm, 512), lambda j, i: (i, 4 * q + j))

    def prev(q):
        return pl.BlockSpec((CONV_HALO, 512), lambda j, i: (jnp.maximum(i * per - 1, 0), 4 * q + j))

    def nxt(q):
        return pl.BlockSpec((CONV_HALO, 512), lambda j, i: (jnp.minimum((i + 1) * per, last), 4 * q + j))

    wspec = pl.BlockSpec((3, 512), lambda j, i: (0, j))
    dy_tile = pl.BlockSpec((tm, D_MODEL), lambda j, i: (i, 0))
    dy_next = pl.BlockSpec((CONV_HALO, D_MODEL), lambda j, i: (jnp.minimum((i + 1) * per, last), 0))
    dh, dw = pl.pallas_call(
        body, name="convb_mix", grid=(4, nt),
        in_specs=[dy_tile, dy_next, _w_out_nt_block(0), col(0), col(1), col(2), col(3), prev(1), prev(2), nxt(0), nxt(3), wspec]
        + [_ANY] * n_dep,
        out_specs=[pl.BlockSpec((tm, D_INNER), lambda j, i: (i, j)), wspec],
        out_shape=[_sds((s, 4 * D_INNER), BF), _sds((3, D_INNER), F32)],
        scratch_shapes=[pltpu.VMEM((CONV_HALO + tm, 512), F32)] * 2, compiler_params=_params(),
    )(dyb, dyb, w["g_out"], h, h, h, h, h, h, h, h, w["conv_w"], *dep)

    def w_block(kp):
        k = 4 * (kp % 4) + kp // 4
        return (k // 2, 0, k % 2)

    g_in = _tn("convb_gin", sv["xn"], dh, (N_DEV, D_MODEL, D_MODEL), (None, D_MODEL, 512), lambda j: (j // 2, 0, j % 2),
               D_MODEL, 512, (16,), lambda j: (0, 0), lambda j: (0, 4 * (j % 4) + j // 4))
    g_out = _w_out_grad("convb_gout", sv["gated"], dyb).reshape(N_DEV, 256, D_MODEL)
    dep = early(dict(g_in=g_in, g_out=g_out))
    dx, dxb, dnorm = _in_proj_bwd("convb_in", dh, w["g_in"], lambda i, kp: w_block(kp), 16, sv["x"], w["norm"], dy, tm, dep)
    return dx, dxb, dict(g_in=g_in, g_out=g_out, norm=dnorm[0], conv_w=dw)


def _attn_tiles(s):
    t = min(ATTN_TILE, s)
    return t, s // t


def _causal_keep(t, keys_on_rows):
    r = lax.broadcasted_iota(jnp.int32, (t, t), 0)
    c = lax.broadcasted_iota(jnp.int32, (t, t), 1)
    return (r <= c) if keys_on_rows else (c <= r)


def _mla_fwd(x, w, rope, tm):
    s = x.shape[0]
    nt = s // tm
    cosf, sinf, perm = rope
    xn = _rms_fwd("mla_rms", x, w["norm"], tm)

    def in_body(xn_ref, wq_ref, wkv_ref, wkr_ref, wz_ref, gq_ref, gkv_ref, cos_ref, sin_ref, p_ref,
                ql_ref, kvl_ref, qn_ref, kvn_ref, krr_ref, z_ref):
        xv = xn_ref[...]
        ql = _dot(xv, wq_ref[...], NN)
        kvl = _dot(xv, wkv_ref[...], NN)
        ql_ref[...] = ql
        kvl_ref[...] = kvl
        rq = lax.rsqrt(jnp.mean(ql * ql, axis=-1, keepdims=True) + NORM_EPS)
        qn_ref[...] = ((ql * rq) * gq_ref[...]).astype(BF)
        rkv = lax.rsqrt(jnp.mean(kvl * kvl, axis=-1, keepdims=True) + NORM_EPS)
        kvn_ref[...] = ((kvl * rkv) * gkv_ref[...]).astype(BF)
        kr = _dot(xv, wkr_ref[...], NN)
        krr_ref[...] = _rope_fwd(kr, cos_ref[...], sin_ref[...], p_ref[...]).astype(BF)
        z_ref[...] = _dot(xv, wz_ref[...], NN).astype(BF)

    def full(a):
        return pl.BlockSpec(a.shape, lambda i: (0,) * a.ndim)

    def rows(c):
        return pl.BlockSpec((tm, c), lambda i: (i, 0))

    gq, gkv = w["q_norm"].reshape(1, Q_RANK), w["kv_norm"].reshape(1, KV_RANK)
    q_lat, kv_lat, qn, kvn, krr, z = pl.pallas_call(
        in_body, name="mla_in", grid=(nt,),
        in_specs=[rows(D_MODEL), full(w["w_q"]), full(w["w_kv"]), full(w["w_kr"]), full(w["w_z"]), full(gq), full(gkv),
                  rows(QK_ROPE), rows(QK_ROPE), full(perm)],
        out_specs=[rows(Q_RANK), rows(KV_RANK), rows(Q_RANK), rows(KV_RANK), rows(QK_ROPE), rows(D_INNER)],
        out_shape=[_sds((s, Q_RANK), F32), _sds((s, KV_RANK), F32), _sds((s, Q_RANK), BF), _sds((s, KV_RANK), BF),
                   _sds((s, QK_ROPE), BF), _sds((s, D_INNER), BF)],
        compiler_params=_params())(xn, w["w_q"], w["w_kv"], w["w_kr"], w["w_z"], gq, gkv, cosf, sinf, perm)

    def q_epi(part, extra, outs, pids):
        r = part[:, 2 * QK_NOPE:]
        lane = lax.broadcasted_iota(jnp.int32, r.shape, 1)
        swapped = jnp.where((lane & (QK_ROPE - 1)) < QK_ROPE // 2, pltpu.roll(r, 2 * QK_ROPE - QK_ROPE // 2, axis=1),
                            pltpu.roll(r, QK_ROPE // 2, axis=1))
        roped = (r * extra[0][...] + swapped * extra[1][...]) * Q_PRESCALE
        for hh in range(2):
            outs[0][hh, :, 0:QK_NOPE] = (part[:, hh * QK_NOPE:(hh + 1) * QK_NOPE] * Q_PRESCALE).astype(BF)
            outs[0][hh, :, QK_NOPE:QK_DIM] = roped[:, hh * QK_ROPE:(hh + 1) * QK_ROPE].astype(BF)

    tp = IN_PROJ_ROWS if s % IN_PROJ_ROWS == 0 else tm
    rope_row = pl.BlockSpec((tp, QK_ROPE), lambda h, i: (i, 0))
    rope_pair = pl.BlockSpec((tp, 2 * QK_ROPE), lambda h, i: (i, 0))
    cos2, sin2 = jnp.concatenate([cosf, cosf], axis=1), jnp.concatenate([sinf, sinf], axis=1)
    (q,) = _mm("mla_qup", (N_HEADS // 2, s // tp), [qn, w["w_qpair"], cos2, sin2],
               [pl.BlockSpec((tp, Q_RANK), lambda h, i: (i, 0)), pl.BlockSpec((None, Q_RANK, 2 * QK_DIM), lambda h, i: (h, 0, 0)),
                rope_pair, rope_pair],
               [_sds((N_HEADS, s, QK_DIM), BF)], [pl.BlockSpec((2, tp, QK_DIM), lambda h, i: (h, i, 0))], NN, q_epi)

    def kv_epi(part, extra, outs, pids):
        outs[0][:, 0:QK_NOPE] = part[:, 0:QK_NOPE].astype(BF)
        outs[0][:, QK_NOPE:QK_DIM] = extra[0][...]
        outs[1][...] = part[:, QK_NOPE:].astype(BF)

    k, v = _mm("mla_kvup", (N_HEADS, s // tp), [kvn, w["g512"], krr],
               [pl.BlockSpec((tp, KV_RANK), lambda h, i: (i, 0)),
                pl.BlockSpec((None, KV_RANK, 256), lambda h, i: (h // 2, 0, h % 2)), rope_row],
               [_sds((N_HEADS, s, QK_DIM), BF), _sds((N_HEADS, s, V_DIM), BF)],
               [pl.BlockSpec((None, tp, QK_DIM), lambda h, i: (h, i, 0)), pl.BlockSpec((None, tp, V_DIM), lambda h, i: (h, i, 0))],
               NN, kv_epi)

    t, nq = _attn_tiles(s)

    def attn_body(q_ref, k_ref, v_ref, z_ref, o_ref, g_ref, lse_ref):
        i = pl.program_id(1)

        def block(j, carry, masked):
            start = pl.multiple_of(j * t, t)
            out = []
            for hh, (m, lsum, acc) in enumerate(carry):
                sc = _dot(q_ref[hh], k_ref[hh, pl.ds(start, t), :], NT)
                if masked:
                    sc = jnp.where(_causal_keep(t, False), sc, NEG_BIG)
                mn = jnp.maximum(m, jnp.max(sc, axis=-1, keepdims=True))
                alpha = jnp.exp2(m - mn)
                p = jnp.exp2(sc - mn)
                lsum = alpha * lsum + jnp.sum(p, axis=-1, keepdims=True)
                acc = alpha * acc + _dot(p.astype(BF), v_ref[hh, pl.ds(start, t), :], NN)
                out.append((mn, lsum, acc))
            return tuple(out)

        init = ((jnp.full((t, 1), NEG_BIG, F32), jnp.zeros((t, 1), F32), jnp.zeros((t, V_DIM), F32)),) * ATTN_HEADS_PER_STEP
        carry = lax.fori_loop(0, i, lambda j, c: block(j, c, False), init)
        for hh, (m, lsum, acc) in enumerate(block(i, carry, True)):
            cols = slice(hh * V_DIM, (hh + 1) * V_DIM)
            o = acc / lsum
            z = z_ref[:, cols].astype(F32)
            o_ref[:, cols] = o
            g_ref[:, cols] = (o * (z * _sig(z))).astype(BF)
            lse_ref[hh] = _to_row(m + jnp.log(lsum) * LOG2_E)

    hp = ATTN_HEADS_PER_STEP
    head_col = pl.BlockSpec((t, hp * V_DIM), lambda h, i: (i, h))
    o, gated, lse = pl.pallas_call(
        attn_body, name="mla_attn", grid=(N_HEADS // hp, nq),
        in_specs=[pl.BlockSpec((hp, t, QK_DIM), lambda h, i: (h, i, 0)), pl.BlockSpec((hp, s, QK_DIM), lambda h, i: (h, 0, 0)),
                  pl.BlockSpec((hp, s, V_DIM), lambda h, i: (h, 0, 0)), head_col],
        out_specs=[head_col, head_col, pl.BlockSpec((hp, None, 1, t), lambda h, i: (h, i, 0, 0))],
        out_shape=[_sds((s, D_INNER), F32), _sds((s, D_INNER), BF), _sds((N_HEADS, nq, 1, t), F32)],
        compiler_params=_params())(q, k, v, z)
    y = _out_proj("mla_out", gated, w["g1024"], 0, x, tm)
    return y, dict(x=x, xn=xn, q_lat=q_lat, kv_lat=kv_lat, qn=qn, kvn=kvn, z=z, q=q, k=k, v=v, o=o, lse=lse, gated=gated)


def _mla_bwd(dy, dyb, w, sv, rope, tm, dep):
    s = dy.shape[0]
    nt = s // tm
    cosf, sinf, perm = rope
    t, nq = _attn_tiles(s)
    assert t == tm, "the row statistics of the backward are laid out per attention tile"
    q, k, v, lse = sv["q"], sv["k"], sv["v"], sv["lse"]

    def gate_bwd(part, extra, outs, pids):
        z, o = extra[0][...].astype(F32), extra[1][...]
        sz, dsz = _silu_and_grad(z)
        do = part * sz
        outs[0][...] = do.astype(BF)
        outs[1][...] = (part * o * dsz).astype(BF)
        prod = do * o
        for hh in range(4):
            outs[2][hh] = _to_row(jnp.sum(prod[:, hh * V_DIM:(hh + 1) * V_DIM], axis=-1, keepdims=True))

    tile = lambda j, i: (i, j)
    dob, dz, delta = _mm(
        "mlab_out", (4, nt), [dyb, w["g1024"], sv["z"], sv["o"]] + dep,
        [pl.BlockSpec((tm, D_MODEL), lambda j, i: (i, 0)), _w_out_nt_block(0),
         pl.BlockSpec((tm, 512), tile), pl.BlockSpec((tm, 512), tile)] + [_ANY] * len(dep),
        [_sds((s, D_INNER), BF), _sds((s, D_INNER), BF), _sds((N_HEADS, nt, 1, tm), F32)],
        [pl.BlockSpec((tm, 512), tile), pl.BlockSpec((tm, 512), tile), pl.BlockSpec((4, None, 1, tm), lambda j, i: (j, i, 0, 0))],
        NT, gate_bwd)

    hp = ATTN_HEADS_PER_STEP

    def attn_bwd_body(k_ref, v_ref, q_ref, do_ref, lse_ref, dl_ref, cos_ref, sin_ref, p_ref, dkv_ref, dkr_ref, dq_ref, dq_acc):
        j = pl.program_id(1)

        @pl.when(j == 0)
        def _():
            dq_acc[...] = jnp.zeros(dq_acc.shape, F32)

        def block(i, carry, masked):
            rows = pl.ds(pl.multiple_of(i * t, t), t)
            out = []
            for hh, (dk, dv) in enumerate(carry):
                kb, vb = k_ref[hh], v_ref[hh]
                qb, dob_ = q_ref[hh, rows, :], do_ref[rows, hh * V_DIM:(hh + 1) * V_DIM]
                st = _dot(kb, qb, NT)
                if masked:
                    st = jnp.where(_causal_keep(t, True), st, NEG_BIG)
                pt = jnp.exp2(st - lse_ref[hh, i])
                dv = dv + _dot(pt.astype(BF), dob_, NN)
                dst = (pt * (_dot(vb, dob_, NT) - dl_ref[hh, i])).astype(BF)
                dk = dk + _dot(dst, qb, NN)
                dq_acc[hh, rows, :] += _dot(dst, kb, TN)
                out.append((dk, dv))
            return tuple(out)

        init = ((jnp.zeros((t, QK_DIM), F32), jnp.zeros((t, V_DIM), F32)),) * hp
        carry = block(j, init, True)
        carry = lax.fori_loop(j + 1, nq, lambda i, c: block(i, c, False), carry)
        for hh, (dk, dv) in enumerate(carry):
            dk = dk * LN_2
            base = hh * 2 * V_DIM
            dkv_ref[:, base:base + QK_NOPE] = dk[:, 0:QK_NOPE].astype(BF)
            dkv_ref[:, base + QK_NOPE:base + 2 * V_DIM] = dv.astype(BF)
            dkr_ref[hh] = dk[:, QK_NOPE:]

        @pl.when(j == nq - 1)
        def _():
            for hh in range(hp):
                for c in range(nq):
                    rows = slice(c * t, (c + 1) * t)
                    dq = dq_acc[hh, rows, :] * ATTN_SCALE
                    dq_ref[hh, rows, 0:QK_NOPE] = dq[:, 0:QK_NOPE].astype(BF)
                    dq_ref[hh, rows, QK_NOPE:] = _rope_bwd(dq[:, QK_NOPE:], cos_ref[rows, :], sin_ref[rows, :], p_ref[...]).astype(BF)

    row_stats = pl.BlockSpec((hp, nq, 1, t), lambda h, j: (h, 0, 0, 0))
    seq_rope = pl.BlockSpec((s, QK_ROPE), lambda h, j: (0, 0))
    head_seq = pl.BlockSpec((hp, s, QK_DIM), lambda h, j: (h, 0, 0))
    dkv, dkr_h, dq = pl.pallas_call(
        attn_bwd_body, name="mlab_attn", grid=(N_HEADS // hp, nq),
        in_specs=[pl.BlockSpec((hp, t, QK_DIM), lambda h, j: (h, j, 0)), pl.BlockSpec((hp, t, V_DIM), lambda h, j: (h, j, 0)),
                  head_seq, pl.BlockSpec((s, hp * V_DIM), lambda h, j: (0, h)), row_stats, row_stats, seq_rope, seq_rope,
                  pl.BlockSpec((QK_ROPE, QK_ROPE), lambda h, j: (0, 0))],
        out_specs=[pl.BlockSpec((t, hp * 2 * V_DIM), lambda h, j: (j, h)), pl.BlockSpec((hp, t, QK_ROPE), lambda h, j: (h, j, 0)), head_seq],
        out_shape=[_sds((s, N_HEADS * 2 * V_DIM), BF), _sds((N_HEADS, s, QK_ROPE), F32), _sds((N_HEADS, s, QK_DIM), BF)],
        scratch_shapes=[pltpu.VMEM((hp, s, QK_DIM), F32)],
        compiler_params=_params())(k, v, q, dob, lse, delta, cosf, sinf, perm)

    def dkr_body(d_ref, cos_ref, sin_ref, p_ref, o_ref):
        tot = d_ref[0]
        for hh in range(1, N_HEADS):
            tot = tot + d_ref[hh]
        o_ref[...] = _rope_bwd(tot, cos_ref[...], sin_ref[...], p_ref[...]).astype(BF)

    r64 = pl.BlockSpec((tm, QK_ROPE), lambda i: (i, 0))
    dkr = pl.pallas_call(
        dkr_body, name="mlab_dkr", grid=(nt,),
        in_specs=[pl.BlockSpec((N_HEADS, tm, QK_ROPE), lambda i: (0, i, 0)), r64, r64, pl.BlockSpec((QK_ROPE, QK_ROPE), lambda i: (0, 0))],
        out_specs=r64, out_shape=_sds((s, QK_ROPE), BF), compiler_params=_params())(dkr_h, cosf, sinf, perm)

    def lat_epi(acc, extra, outs, pids):
        dx, dg = _rms_bwd(acc, extra[0][...], extra[1][...], None)
        outs[0][...] = dx.astype(BF)
        _accumulate(outs[1], dg, pids[0])

    tp = IN_PROJ_ROWS if s % IN_PROJ_ROWS == 0 else tm

    def lat_bwd(name, a, a_spec, b, b_spec, n_k, lat, g, rank):
        row = lambda i, k: (i, 0)
        one = lambda i, k: (0, 0)
        return _mm(name, (s // tp, n_k), [a, b, lat, g.reshape(1, rank)],
                   [a_spec, b_spec, pl.BlockSpec((tp, rank), row), pl.BlockSpec((1, rank), one)],
                   [_sds((s, rank), BF), _sds((1, rank), F32)], [pl.BlockSpec((tp, rank), row), pl.BlockSpec((1, rank), one)],
                   NT, lat_epi, red=1, acc_shape=(tp, rank))

    d_ql, g_qnorm = lat_bwd("mlab_qup", dq, pl.BlockSpec((None, tp, QK_DIM), lambda i, h: (h, i, 0)),
                            w["w_qh"], pl.BlockSpec((None, Q_RANK, QK_DIM), lambda i, h: (h, 0, 0)), N_HEADS,
                            sv["q_lat"], w["q_norm"], Q_RANK)
    d_kvl, g_kvnorm = lat_bwd("mlab_kvup", dkv, pl.BlockSpec((tp, 512), lambda i, kk: (i, kk)),
                              w["g512"], pl.BlockSpec((None, KV_RANK, 512), lambda i, kk: (kk, 0, 0)), N_DEV,
                              sv["kv_lat"], w["kv_norm"], KV_RANK)

    def in_bwd(dql_ref, dkvl_ref, dkr_ref, dz_ref, wq_ref, wkv_ref, wkr_ref, wz_ref, x_ref, g_ref, dy_ref, dx_ref, dxb_ref, dg_ref):
        acc = (_dot(dql_ref[...], wq_ref[...], NT) + _dot(dkvl_ref[...], wkv_ref[...], NT)
               + _dot(dkr_ref[...], wkr_ref[...], NT) + _dot(dz_ref[...], wz_ref[...], NT))
        dx, dg = _rms_bwd(acc, x_ref[...], g_ref[...], dy_ref[...])
        dx_ref[...] = dx
        dxb_ref[...] = dx.astype(BF)
        _accumulate(dg_ref, dg, pl.program_id(0))

    def full(a):
        return pl.BlockSpec(a.shape, lambda i: (0,) * a.ndim)

    def rows(c):
        return pl.BlockSpec((tm, c), lambda i: (i, 0))

    gm = w["norm"].reshape(1, D_MODEL)
    dx, dxb, g_norm = pl.pallas_call(
        in_bwd, name="mlab_in", grid=(nt,),
        in_specs=[rows(Q_RANK), rows(KV_RANK), rows(QK_ROPE), rows(D_INNER), full(w["w_q"]), full(w["w_kv"]), full(w["w_kr"]),
                  full(w["w_z"]), rows(D_MODEL), full(gm), rows(D_MODEL)],
        out_specs=[rows(D_MODEL), rows(D_MODEL), full(gm)],
        out_shape=[_sds((s, D_MODEL), F32), _sds((s, D_MODEL), BF), _sds((1, D_MODEL), F32)],
        compiler_params=_params())(d_ql, d_kvl, dkr, dz, w["w_q"], w["w_kv"], w["w_kr"], w["w_z"], sv["x"], gm, dy)

    xn = sv["xn"]
    one = lambda j: (0, 0)
    g_q = _tn("mlab_gq", xn, d_ql, (D_MODEL, Q_RANK), (D_MODEL, Q_RANK), one, D_MODEL, Q_RANK, (1,), one, one)
    g_kv = _tn("mlab_gkv", xn, d_kvl, (D_MODEL, KV_RANK), (D_MODEL, KV_RANK), one, D_MODEL, KV_RANK, (1,), one, one)
    g_kr = _tn("mlab_gkr", xn, dkr, (D_MODEL, QK_ROPE), (D_MODEL, QK_ROPE), one, D_MODEL, QK_ROPE, (1,), one, one)
    g_z = _tn("mlab_gz", xn, dz, (D_MODEL, D_INNER), (D_MODEL, 512), lambda j: (0, j), D_MODEL, 512, (4,), one, lambda j: (0, j))
    g_in = jnp.concatenate([g_q, g_kv, g_kr, g_z], axis=1)
    g_qh = _tn("mlab_gqup", sv["qn"], dq, (N_HEADS, Q_RANK, QK_DIM), (None, Q_RANK, QK_DIM), lambda h: (h, 0, 0),
               Q_RANK, QK_DIM, (N_HEADS,), lambda h: (0, 0), lambda h: (h, 0, 0))
    g_kvup = _tn("mlab_gkvup", sv["kvn"], dkv, (N_DEV, KV_RANK, 512), (None, KV_RANK, 512), lambda j: (j, 0, 0),
                 KV_RANK, 512, (N_DEV,), lambda j: (0, 0), lambda j: (0, j))
    g_out = _w_out_grad("mlab_gout", sv["gated"], dyb)
    s384 = g_qh.reshape(N_DEV, 2, Q_RANK, QK_DIM).transpose(0, 2, 1, 3).reshape(N_DEV, Q_RANK, 2 * QK_DIM)
    s344 = g_in.reshape(D_MODEL, N_DEV, 344).transpose(1, 0, 2)
    return dx, dxb, dict(s344=s344, s384=s384, s512=g_kvup, s1024=g_out.reshape(N_DEV, 256, D_MODEL),
                         norm=g_norm[0], q_norm=g_qnorm[0], kv_norm=g_kvnorm[0])


def _loss_head(x, g, target, tm):
    s, d = x.shape

    def body(x_ref, g_ref, t_ref, dx_ref, dxb_ref, dg_ref, loss_ref):
        i = pl.program_id(0)
        xv, gv = x_ref[...], g_ref[...]
        r = lax.rsqrt(jnp.mean(xv * xv, axis=-1, keepdims=True) + NORM_EPS)
        err = (xv * r) * gv - t_ref[...]
        part = 0.5 * jnp.sum(jnp.mean(err * err, axis=-1, keepdims=True), axis=0, keepdims=True)
        dx, dg = _rms_bwd(err * (1.0 / d), xv, gv, None)
        dx_ref[...] = dx
        dxb_ref[...] = dx.astype(BF)
        _accumulate(dg_ref, dg, i)
        _accumulate(loss_ref, jnp.broadcast_to(part, loss_ref.shape), i)

    row = pl.BlockSpec((tm, d), lambda i: (i, 0))
    one = pl.BlockSpec((1, d), lambda i: (0, 0))
    return pl.pallas_call(
        body, name="loss_head", grid=(s // tm,), in_specs=[row, one, row],
        out_specs=[row, row, one, pl.BlockSpec((8, 128), lambda i: (0, 0))],
        out_shape=[_sds((s, d), F32), _sds((s, d), BF), _sds((1, d), F32), _sds((8, 128), F32)],
        compiler_params=_params())(x, g.reshape(1, d), target)


def _rope_tables(pos):
    inv_freq = ROPE_BASE ** (-jnp.arange(0, QK_ROPE, 2, dtype=F32) / QK_ROPE)
    ang = pos.astype(F32)[:, None] * inv_freq
    cos, sin = jnp.cos(ang), jnp.sin(ang)
    idx = jnp.arange(QK_ROPE)
    perm = (idx[:, None] == (idx[None, :] + QK_ROPE // 2) % QK_ROPE).astype(F32)
    return jnp.concatenate([cos, cos], axis=1), jnp.concatenate([-sin, sin], axis=1), perm


def _local_step(x, pos, target, final_norm, get_w, put_g):
    s = x.shape[0]
    tm = min(512, s)
    tl = IN_PROJ_ROWS if s % IN_PROJ_ROWS == 0 else tm
    rope = _rope_tables(pos)
    w0 = get_w(0, [])
    x1, sv0 = _pool_fwd(x, 0, w0, tl)
    w1 = get_w(1, [x1])
    x2, sv1 = _conv_fwd(x1, w1, tl)
    w2 = get_w(2, [x2])
    x3, sv2 = _mla_fwd(x2, w2, rope, tm)
    w3 = get_w(3, [x3])
    x4, sv3 = _pool_fwd(x3, 1, w3, tl)
    d4, d4b, g_final, loss = _loss_head(x4, final_norm, target, tm)
    d3, d3b, gp1 = _pool_bwd(d4, d4b, 1, sv3["w"], sv3, tl, [])
    dep = put_g(3, gp1)
    d2, d2b, gm = _mla_bwd(d3, d3b, w2, sv2, rope, tm, dep)
    dep = put_g(2, gm)
    sent = {}

    def send_conv(part):
        sent["dep"] = put_g(1, part)
        return sent["dep"]

    d1, d1b, gc = _conv_bwd(d2, d2b, sv1["w"], sv1, tl, dep, send_conv)
    d0, _, gp0 = _pool_bwd(d1, d1b, 0, sv0["w"], sv0, tl, sent["dep"], early=lambda part: put_g(0, part))
    put_g(4, {0: dict(gp0, final_norm=g_final[0]), 1: gc, 2: gm, 3: gp1})
    return loss[0, 0], d0


def _pack_groups(p):
    bf = lambda a: a.astype(BF)
    grp = lambda l: bf(p["pool_w_grp"][l].reshape(4 * 64, POOL_GROUP))
    return [[bf(p["pool_w_in"][0]), _pack_small(p, SMALL_ROWS_AG)],
            [grp(0), bf(p["pool_w_out"][0])],
            [bf(p["conv_w_in"][0])],
            [bf(p["conv_w_out"][0])],
            [bf(p[k][0]) for k in ("mla_w_in", "mla_w_q_up", "mla_w_kv_up", "mla_w_out")],
            [bf(p["pool_w_in"][1]), grp(1), bf(p["pool_w_out"][1])]]


_SMALL_SHARDED = ("pool_norm", "pool_scale", "mla_norm", "mla_q_norm", "mla_kv_norm", "conv_w")
_SMALL_REPLICATED = ("conv_norm", "final_norm")


def _pack_small(p, rows, with_replicated=False):
    parts = [p[k].reshape(-1) for k in _SMALL_SHARDED]
    if with_replicated:
        parts += [p[k].reshape(-1) for k in _SMALL_REPLICATED]
    flat = jnp.concatenate(parts)
    return jnp.pad(flat, (0, rows * 128 - flat.shape[0])).reshape(rows, 128)


_SMALL_SHARD_SHAPES = dict(pool_norm=(2, 128), pool_scale=(2, 256), mla_norm=(1, 128), mla_q_norm=(1, 48),
                           mla_kv_norm=(1, 32), conv_w=(1, 3, 256), conv_norm=(1, 1024), final_norm=(1024,))


def _unpack_small(buf, with_replicated=False):
    flat = buf.reshape(-1)
    out, off = {}, 0
    for k in _SMALL_SHARDED + (_SMALL_REPLICATED if with_replicated else ()):
        shp = _SMALL_SHARD_SHAPES[k]
        n = 1
        for d in shp:
            n *= d
        out[k] = flat[off:off + n].reshape(shp)
        off += n
    return out


def _small_views(gsmall):
    flat = gsmall.reshape(N_DEV, -1)

    def cols(off, rows, width):
        return flat[:, off:off + rows * width].reshape(N_DEV, rows, width).transpose(1, 0, 2).reshape(rows, N_DEV * width)

    return dict(pool_norm=cols(0, 2, 128), pool_scale=cols(256, 2, 256), mla_norm=cols(768, 1, 128)[0],
                q_norm=cols(896, 1, 48)[0], kv_norm=cols(944, 1, 32)[0], conv_w=cols(976, 3, 256))


def _pair_columns(g384):
    heads = g384.reshape(N_DEV, Q_RANK, 2, QK_DIM)
    nope = heads[..., :QK_NOPE].reshape(N_DEV, Q_RANK, 2 * QK_NOPE)
    rope = heads[..., QK_NOPE:].reshape(N_DEV, Q_RANK, 2 * QK_ROPE)
    return jnp.concatenate([nope, rope], axis=-1)


def _layer_weights(layer, bufs, small, conv_norm):
    if layer in (0, 3):
        l = 0 if layer == 0 else 1
        return dict(g_in=bufs[0], norm=small["pool_norm"][l], scale=small["pool_scale"][l])
    if layer == 1:
        return dict(g_in=bufs[0], norm=conv_norm.reshape(D_MODEL), conv_w=small["conv_w"])
    g344, g384, g512, g1024 = bufs
    w_in = g344.transpose(1, 0, 2).reshape(D_MODEL, N_DEV * 344)
    return dict(
        g512=g512, g1024=g1024,
        w_q=w_in[:, :Q_RANK], w_kv=w_in[:, Q_RANK:Q_RANK + KV_RANK],
        w_kr=w_in[:, Q_RANK + KV_RANK:Q_RANK + KV_RANK + QK_ROPE], w_z=w_in[:, Q_RANK + KV_RANK + QK_ROPE:],
        w_qh=g384.reshape(N_DEV, Q_RANK, 2, QK_DIM).transpose(0, 2, 1, 3).reshape(N_HEADS, Q_RANK, QK_DIM),
        w_qpair=_pair_columns(g384),
        norm=small["mla_norm"], q_norm=small["q_norm"], kv_norm=small["kv_norm"])


_GRAD_KEYS = {0: ("g_in", "g_grp", "g_out"), 3: ("g_in", "g_grp", "g_out"), 1: ("g_in", "g_out"), 2: ("s344", "s384", "s512", "s1024")}
_GRAD_PARAM = {0: dict(g_in="pool_w_in", g_grp="pool_w_grp", g_out="pool_w_out"), 1: dict(g_in="conv_w_in", g_out="conv_w_out"),
               2: dict(s344="mla_w_in", s384="mla_w_q_up", s512="mla_w_kv_up", s1024="mla_w_out")}
_GRAD_PARAM[3] = _GRAD_PARAM[0]


def _grad_group(layer, g):
    keys = tuple(k for k in _GRAD_KEYS[layer] if k in g)
    return keys, [g[k] for k in keys]


def _pack_small_grads(g):
    def split(a, rows, width):
        return a.reshape(rows, N_DEV, width).transpose(1, 0, 2).reshape(N_DEV, rows * width)

    rep = lambda a: jnp.broadcast_to(a.reshape(1, -1), (N_DEV, a.size))
    flat = jnp.concatenate([
        split(jnp.stack([g[0]["norm"], g[3]["norm"]]), 2, 128), split(jnp.stack([g[0]["scale"], g[3]["scale"]]), 2, 256),
        split(g[2]["norm"], 1, 128), split(g[2]["q_norm"], 1, 48), split(g[2]["kv_norm"], 1, 32), split(g[1]["conv_w"], 3, 256),
        rep(g[1]["norm"]), rep(g[0]["final_norm"])], axis=1)
    return jnp.pad(flat, ((0, 0), (0, SMALL_ROWS_RS * 128 - flat.shape[1]))).reshape(N_DEV, SMALL_ROWS_RS, 128)


def _peers(x, y, c):
    for k in range(1, N_DEV):
        px = 1 - x if k & 4 else x
        py = 1 - y if k & 2 else y
        pc = 1 - c if k & 1 else c
        yield k - 1, (px, py, pc), 4 * px + 2 * py + pc


_SIBLING = (0,)
_ICI_DIRECT = (1, 3, 5)


def _remote_copies(srcs, lands, send_sems, recv_sems, gather, ks=None):
    x, y, c = lax.axis_index("x"), lax.axis_index("y"), lax.axis_index("c")
    me = 4 * x + 2 * y + c
    copies = []
    for k, peer, pidx in _peers(x, y, c):
        if ks is not None and k not in ks:
            continue
        for a, (src, land) in enumerate(zip(srcs, lands)):
            copies.append(pltpu.make_async_remote_copy(
                src_ref=src if gather else src.at[pidx], dst_ref=land.at[me],
                send_sem=send_sems.at[a * (N_DEV - 1) + k], recv_sem=recv_sems.at[a * (N_DEV - 1) + k],
                device_id=peer, device_id_type=pl.DeviceIdType.MESH))
    return copies


def _relay_copies(lands, send_sems, recv_sems):
    x, y, c = lax.axis_index("x"), lax.axis_index("y"), lax.axis_index("c")
    copies = []
    for j, k in enumerate(_ICI_DIRECT):
        px = 1 - x if (k + 1) & 4 else x
        py = 1 - y if (k + 1) & 2 else y
        slot = 4 * px + 2 * py + c
        for a, land in enumerate(lands):
            copies.append(pltpu.make_async_remote_copy(
                src_ref=land.at[slot], dst_ref=land.at[slot],
                send_sem=send_sems.at[a * len(_ICI_DIRECT) + j], recv_sem=recv_sems.at[a * len(_ICI_DIRECT) + j],
                device_id=(x, y, 1 - c), device_id_type=pl.DeviceIdType.MESH))
    return copies


_HBM = pl.BlockSpec(memory_space=pltpu.HBM)
_SEM = pl.BlockSpec(memory_space=pltpu.SEMAPHORE)
_EFFECT = pltpu.SideEffectType.DATAFLOW_SIDE_EFFECTING


def _own_slabs(name, arrays, gather, dep):
    n, nd = len(arrays), len(dep)
    me = (4 * lax.axis_index("x") + 2 * lax.axis_index("y") + lax.axis_index("c")).astype(jnp.int32).reshape(1)

    def body(me_ref, *refs):
        for a in range(n):
            refs[n + nd + a][...] = refs[a][...]

    def slab(shape):
        return pl.BlockSpec((None,) + tuple(shape), lambda i, me_ref: (me_ref[0],) + (0,) * len(shape))

    def whole(shape):
        return pl.BlockSpec(tuple(shape), lambda i, me_ref: (0,) * len(shape))

    outs = [_sds(((N_DEV,) + a.shape) if gather else a.shape, a.dtype) for a in arrays]
    grid_spec = pltpu.PrefetchScalarGridSpec(
        num_scalar_prefetch=1, grid=(1,),
        in_specs=[whole(a.shape) if gather else slab(a.shape[1:]) for a in arrays] + [_ANY] * nd,
        out_specs=[slab(o.shape[1:]) for o in outs])
    return pl.pallas_call(body, name=name, grid_spec=grid_spec, out_shape=outs, compiler_params=_params())(me, *arrays, *dep)


def _exchange_start(name, arrays, lands, gather, ks=None):
    n = len(arrays)

    def body(*refs):
        srcs, lnds, send_sems, recv_sems, token = refs[:n], refs[n:2 * n], refs[2 * n], refs[2 * n + 1], refs[-1]
        for cp in _remote_copies(srcs, lnds, send_sems, recv_sems, gather, ks):
            cp.start()
        token[...] = jnp.zeros(token.shape, F32)

    sems = pltpu.SemaphoreType.DMA((n * (N_DEV - 1),))
    thru = [pltpu.HBM(a.shape, a.dtype) for a in list(arrays) + list(lands)]
    res = pl.pallas_call(
        body, name=name, in_specs=[_HBM] * (2 * n),
        out_specs=[_SEM, _SEM] + [_HBM] * (2 * n) + [pl.BlockSpec(memory_space=pltpu.VMEM)],
        out_shape=[sems, sems] + thru + [_sds((8, 128), F32)],
        input_output_aliases={i: 2 + i for i in range(2 * n)},
        compiler_params=pltpu.CompilerParams(has_side_effects=_EFFECT),
    )(*[pltpu.with_memory_space_constraint(a, pltpu.HBM) for a in list(arrays) + list(lands)])
    return res[0], res[1], list(res[2:2 + n]), list(res[2 + n:2 + 2 * n]), res[-1]


def _exchange_wait(name, send_sems, recv_sems, arrays, lands, after, gather):
    n = len(arrays)
    n_after = len(after)

    def body(*refs):
        srcs, lnds = refs[:n], refs[n:2 * n]
        copies = _remote_copies(srcs, lnds, refs[2 * n], refs[2 * n + 1], gather)
        for cp in copies:
            cp.wait_send()
        for cp in copies:
            cp.wait_recv()

    thru = [pltpu.HBM(a.shape, a.dtype) for a in list(arrays) + list(lands)]
    res = pl.pallas_call(
        body, name=name, in_specs=[_HBM] * (2 * n) + [_SEM, _SEM] + [pl.BlockSpec(memory_space=pl.ANY)] * n_after,
        out_specs=[_HBM] * (2 * n), out_shape=thru, input_output_aliases={i: i for i in range(2 * n)},
        compiler_params=pltpu.CompilerParams(has_side_effects=_EFFECT),
    )(*arrays, *lands, send_sems, recv_sems, *after)
    return list(res[n:])


def _gather_relay(name, send_sems, recv_sems, arrays, lands, after):
    n, n_after = len(arrays), len(after)

    def body(*refs):
        srcs, lnds, first_send, first_recv = refs[:n], refs[n:2 * n], refs[2 * n], refs[2 * n + 1]
        send2, recv2, token = refs[2 * n + 2 + n_after], refs[2 * n + 3 + n_after], refs[-1]
        for cp in _remote_copies(srcs, lnds, first_send, first_recv, True, _ICI_DIRECT):
            cp.wait_recv()
        for cp in _relay_copies(lnds, send2, recv2):
            cp.start()
        token[...] = jnp.zeros(token.shape, F32)

    sems = pltpu.SemaphoreType.DMA((n * len(_ICI_DIRECT),))
    thru = [pltpu.HBM(a.shape, a.dtype) for a in list(arrays) + list(lands)]
    res = pl.pallas_call(
        body, name=name, in_specs=[_HBM] * (2 * n) + [_SEM, _SEM] + [_ANY] * n_after,
        out_specs=[_SEM, _SEM] + [_HBM] * (2 * n) + [pl.BlockSpec(memory_space=pltpu.VMEM)],
        out_shape=[sems, sems] + thru + [_sds((8, 128), F32)],
        input_output_aliases={i: 2 + i for i in range(2 * n)},
        compiler_params=pltpu.CompilerParams(has_side_effects=_EFFECT),
    )(*arrays, *lands, send_sems, recv_sems, *after)
    return res[0], res[1], list(res[2:2 + n]), list(res[2 + n:2 + 2 * n]), res[-1]


def _gather_wait2(name, send_sems, recv_sems, send2, recv2, arrays, lands, after):
    n, n_after = len(arrays), len(after)

    def body(*refs):
        srcs, lnds = refs[:n], refs[n:2 * n]
        s1, r1, s2, r2 = refs[2 * n:2 * n + 4]
        for cp in _remote_copies(srcs, lnds, s1, r1, True, _SIBLING + _ICI_DIRECT):
            cp.wait_send()
        for cp in _remote_copies(srcs, lnds, s1, r1, True, _SIBLING):
            cp.wait_recv()
        relay = _relay_copies(lnds, s2, r2)
        for cp in relay:
            cp.wait_send()
        for cp in relay:
            cp.wait_recv()

    thru = [pltpu.HBM(a.shape, a.dtype) for a in list(arrays) + list(lands)]
    res = pl.pallas_call(
        body, name=name, in_specs=[_HBM] * (2 * n) + [_SEM] * 4 + [_ANY] * n_after,
        out_specs=[_HBM] * (2 * n), out_shape=thru, input_output_aliases={i: i for i in range(2 * n)},
        compiler_params=pltpu.CompilerParams(has_side_effects=_EFFECT),
    )(*arrays, *lands, send_sems, recv_sems, send2, recv2, *after)
    return list(res[n:])


def _adamw_math(g, w, m, v):
    m = ADAM_B1 * m + (1.0 - ADAM_B1) * g
    v = ADAM_B2 * v + (1.0 - ADAM_B2) * (g * g)
    m_hat = m / (1.0 - ADAM_B1 ** ADAM_STEP)
    v_hat = v / (1.0 - ADAM_B2 ** ADAM_STEP)
    delta = -ADAM_LR * (m_hat / (jnp.sqrt(v_hat) + ADAM_EPS) + ADAM_WD * w)
    return delta, m, v


def _sum_adamw(name, recv, row_off, w, m, v, tr, layer=0):
    width = recv.shape[-1]
    w2, m2, v2 = (a.reshape(a.shape[0], -1, width) for a in (w, m, v))
    rows = w2.shape[1]
    base = row_off // tr

    def body(r_ref, w_ref, m_ref, v_ref, g_ref, d_ref, mo_ref, vo_ref):
        g = r_ref[0].astype(F32)
        for src in range(1, N_DEV):
            g = g + r_ref[src].astype(F32)
        delta, mn, vn = _adamw_math(g, w_ref[...], m_ref[...], v_ref[...])
        g_ref[...] = g
        d_ref[...] = delta
        mo_ref[...] = mn
        vo_ref[...] = vn

    blk = pl.BlockSpec((tr, width), lambda i: (i, 0))
    wblk = pl.BlockSpec((None, tr, width), lambda i: (layer, i, 0))
    return pl.pallas_call(
        body, name=name, grid=(rows // tr,),
        in_specs=[pl.BlockSpec((N_DEV, tr, width), lambda i: (0, base + i, 0)), wblk, wblk, wblk],
        out_specs=[blk] * 4, out_shape=[_sds((rows, width), F32)] * 4, compiler_params=_params())(recv, w2, m2, v2)


_WEIGHTS = ("pool_norm", "pool_w_in", "pool_w_grp", "pool_scale", "pool_w_out", "conv_norm", "conv_w_in", "conv_w", "conv_w_out",
            "mla_norm", "mla_w_in", "mla_q_norm", "mla_w_q_up", "mla_kv_norm", "mla_w_kv_up", "mla_w_out", "final_norm")


def _step(x, positions, loss_target, p, m, v):
    gathers, tokens, dep = [], [], []
    for group, arrays in enumerate(_pack_groups(p)):
        lands = _own_slabs(f"gather{group}_own", arrays, True, dep)
        ks = _SIBLING + _ICI_DIRECT if group < TWO_LEVEL_GROUPS else None
        ssem, rsem, arrays, lands, token = _exchange_start(f"gather{group}_start", arrays, lands, True, ks)
        gathers.append((ssem, rsem, arrays, lands))
        tokens.append(token)
        dep = [token]
    state = {}

    def wait_group(group, after):
        if group >= TWO_LEVEL_GROUPS:
            return _exchange_wait(f"gather{group}_wait", *gathers[group], after, True)
        ssem, rsem, arrays, lands = gathers[group]
        send2, recv2, arrays, lands, token = _gather_relay(f"gather{group}_relay", ssem, rsem, arrays, lands, after)
        return _gather_wait2(f"gather{group}_wait", ssem, rsem, send2, recv2, arrays, lands, [token])

    def get_w(layer, after):
        if layer == 0:
            bufs = wait_group(0, list(tokens))
            state["small"] = _small_views(bufs[1])
            rest = lambda later: dict(zip(("g_grp", "g_out"), wait_group(1, later)))
        elif layer == 1:
            bufs = wait_group(2, after)
            rest = lambda later: dict(g_out=wait_group(3, later)[0])
        elif layer == 2:
            bufs = wait_group(4, after)
        else:
            bufs = wait_group(5, after)
            rest = lambda later: dict(g_grp=bufs[1], g_out=bufs[2])
        w = _layer_weights(layer, bufs, state["small"], p["conv_norm"])
        if layer != 2:
            w["rest"] = rest
        return w

    scatters = []

    def put_g(layer, g):
        if layer == 4:
            keys, arrays = ("small",), [_pack_small_grads(g)]
        else:
            keys, arrays = _grad_group(layer, g)
        n = len(scatters)
        lands = _own_slabs(f"scatter{n}_own", arrays, False, [])
        ssem, rsem, arrays, lands, token = _exchange_start(f"scatter{n}_start", arrays, lands, False)
        scatters.append((layer, keys, (ssem, rsem, arrays, lands)))
        tokens.append(token)
        return [token]

    loss, grad_x = _local_step(x[0], positions[0], loss_target[0], p["final_norm"], get_w, put_g)

    res, after = {}, [tokens[-1]]
    for n, (layer, keys, handles) in enumerate(scatters):
        recv = _exchange_wait(f"scatter{n}_wait", *handles, after, False)
        if layer == 4:
            break
        l = 1 if layer == 3 else 0
        for key, buf in zip(keys, recv):
            name = _GRAD_PARAM[layer][key]
            tr = min(256, buf.shape[1]) if name != "mla_w_q_up" else buf.shape[1]
            res[name, l] = _sum_adamw(f"adam_{name}{l}", buf, 0, p[name], m[name], v[name], tr, l)
        after = [res[name, l][1]]
    small = _sum_adamw("adam_small", recv[0], 0, _pack_small(p, SMALL_ROWS_RS, True)[None], _pack_small(m, SMALL_ROWS_RS, True)[None],
                       _pack_small(v, SMALL_ROWS_RS, True)[None], SMALL_ROWS_RS)
    small = [_unpack_small(a, True) for a in small]
    final = {k: tuple(part[k] for part in small) for k in _SMALL_SHARDED + _SMALL_REPLICATED}
    for k in _WEIGHTS:
        if k not in final:
            layers = [res[k, l] for l in range(p[k].shape[0])]
            final[k] = tuple(jnp.stack([lay[part] for lay in layers]).reshape(p[k].shape) for part in range(4))
    res = final

    loss = lax.psum(loss, ("x", "y", "c"))
    out = [loss, grad_x[None]]
    for part in range(4):
        out += [res[k][part] for k in _WEIGHTS]
    return tuple(out)


def kernel(x, positions, pool_norm, pool_w_in, pool_w_grp, pool_scale, pool_w_out, conv_norm, conv_w_in, conv_w, conv_w_out, mla_norm, mla_w_in, mla_q_norm, mla_w_q_up, mla_kv_norm, mla_w_kv_up, mla_w_out, final_norm, loss_target, m_pool_norm, m_pool_w_in, m_pool_w_grp, m_pool_scale, m_pool_w_out, m_conv_norm, m_conv_w_in, m_conv_w, m_conv_w_out, m_mla_norm, m_mla_w_in, m_mla_q_norm, m_mla_w_q_up, m_mla_kv_norm, m_mla_w_kv_up, m_mla_w_out, m_final_norm, v_pool_norm, v_pool_w_in, v_pool_w_grp, v_pool_scale, v_pool_w_out, v_conv_norm, v_conv_w_in, v_conv_w, v_conv_w_out, v_mla_norm, v_mla_w_in, v_mla_q_norm, v_mla_w_q_up, v_mla_kv_norm, v_mla_w_kv_up, v_mla_w_out, v_final_norm):
    p = dict(pool_norm=pool_norm, pool_w_in=pool_w_in, pool_w_grp=pool_w_grp, pool_scale=pool_scale, pool_w_out=pool_w_out,
             conv_norm=conv_norm, conv_w_in=conv_w_in, conv_w=conv_w, conv_w_out=conv_w_out, mla_norm=mla_norm, mla_w_in=mla_w_in,
             mla_q_norm=mla_q_norm, mla_w_q_up=mla_w_q_up, mla_kv_norm=mla_kv_norm, mla_w_kv_up=mla_w_kv_up, mla_w_out=mla_w_out,
             final_norm=final_norm)
    m = dict(pool_norm=m_pool_norm, pool_w_in=m_pool_w_in, pool_w_grp=m_pool_w_grp, pool_scale=m_pool_scale, pool_w_out=m_pool_w_out,
             conv_norm=m_conv_norm, conv_w_in=m_conv_w_in, conv_w=m_conv_w, conv_w_out=m_conv_w_out, mla_norm=m_mla_norm,
             mla_w_in=m_mla_w_in, mla_q_norm=m_mla_q_norm, mla_w_q_up=m_mla_w_q_up, mla_kv_norm=m_mla_kv_norm,
             mla_w_kv_up=m_mla_w_kv_up, mla_w_out=m_mla_w_out, final_norm=m_final_norm)
    v = dict(pool_norm=v_pool_norm, pool_w_in=v_pool_w_in, pool_w_grp=v_pool_w_grp, pool_scale=v_pool_scale, pool_w_out=v_pool_w_out,
             conv_norm=v_conv_norm, conv_w_in=v_conv_w_in, conv_w=v_conv_w, conv_w_out=v_conv_w_out, mla_norm=v_mla_norm,
             mla_w_in=v_mla_w_in, mla_q_norm=v_mla_q_norm, mla_w_q_up=v_mla_w_q_up, mla_kv_norm=v_mla_kv_norm,
             mla_w_kv_up=v_mla_w_kv_up, mla_w_out=v_mla_w_out, final_norm=v_final_norm)
    return _step(x, positions, loss_target, p, m, v)
```

```python
import functools

import jax
import jax.numpy as jnp
from jax import lax
from jax.experimental import pallas as pl
from jax.experimental.pallas import tpu as pltpu

BF = jnp.bfloat16
F32 = jnp.float32

N_DEV = 8
D_MODEL = 1024
D_INNER = 2048
POOL_WINDOWS = (2, 4, 8, 16)
POOL_GROUP = 512
N_HEADS = 16
QK_NOPE = 128
QK_ROPE = 64
QK_DIM = QK_NOPE + QK_ROPE
V_DIM = 128
Q_RANK = 384
KV_RANK = 256
ATTN_SCALE = QK_DIM ** -0.5
LOG2_E = 1.4426950408889634
LN_2 = 0.6931471805599453
Q_PRESCALE = ATTN_SCALE * LOG2_E
ATTN_TILE = 512
ATTN_HEADS_PER_STEP = 2
ROPE_BASE = 10000.0
NORM_EPS = 1e-6
NEG_BIG = -1e30

ADAM_LR = 0.001
ADAM_B1 = 0.9
ADAM_B2 = 0.999
ADAM_EPS = 1e-08
ADAM_WD = 0.01
ADAM_STEP = 10

VMEM_LIMIT_BYTES = 52 * 1024 * 1024
IN_PROJ_ROWS = 1024
POOL_HALO = 32
CONV_HALO = 16

NN = (((1,), (0,)), ((), ()))
NT = (((1,), (1,)), ((), ()))
TN = (((0,), (0,)), ((), ()))

TWO_LEVEL_GROUPS = 3
SMALL_ROWS_AG = 16
SMALL_ROWS_RS = 32


def _sds(shape, dtype):
    return jax.ShapeDtypeStruct(tuple(shape), dtype)


def _params():
    return pltpu.CompilerParams(vmem_limit_bytes=VMEM_LIMIT_BYTES)


_ANY = pl.BlockSpec(memory_space=pl.ANY)


def _dot(a, b, dims):
    return lax.dot_general(a, b, dims, preferred_element_type=F32)


def _sig(z):
    return 1.0 / (1.0 + jnp.exp(-z))


def _silu_and_grad(z):
    sig = _sig(z)
    return z * sig, sig * (1.0 + z * (1.0 - sig))


def _to_row(col):
    return jnp.broadcast_to(col, (col.shape[0], 128)).T[0:1, :]


def _rope_swap(x, p):
    pb = p.astype(BF)
    hi = x.astype(BF)
    r1 = x - hi.astype(F32)
    mid = r1.astype(BF)
    lo = (r1 - mid.astype(F32)).astype(BF)
    return (_dot(hi, pb, NN) + _dot(mid, pb, NN)) + _dot(lo, pb, NN)


def _rope_fwd(x, cosf, sinf, p):
    return x * cosf + _rope_swap(x, p) * sinf


def _rope_bwd(dy, cosf, sinf, p):
    return dy * cosf + _rope_swap(dy * sinf, p)


def _rms_bwd(dxn, x, g, res):
    r = lax.rsqrt(jnp.mean(x * x, axis=-1, keepdims=True) + NORM_EPS)
    v = dxn * g
    dx = r * v - x * ((r * r * r) * jnp.mean(v * x, axis=-1, keepdims=True))
    if res is not None:
        dx = dx + res
    dg = jnp.sum(dxn * (x * r), axis=0, keepdims=True)
    return dx, dg


def _accumulate(ref, val, step):
    @pl.when(step == 0)
    def _():
        ref[...] = val

    @pl.when(step > 0)
    def _():
        ref[...] += val


def _mm(name, grid, ins, in_specs, outs, out_specs, dims, epi, red=None, acc_shape=None):
    n_in, n_out = len(ins), len(outs)
    n_red = None if red is None else grid[red]

    def body(*refs):
        in_refs, out_refs = refs[:n_in], refs[n_in:n_in + n_out]
        pids = tuple(pl.program_id(ax) for ax in range(len(grid)))
        a, b = in_refs[0][...], in_refs[1][...]
        if a.ndim == 3:
            a = a.reshape(-1, a.shape[-1])
        if b.ndim == 3:
            b = b.reshape(-1, b.shape[-1])
        part = _dot(a.astype(BF), b.astype(BF), dims)
        if red is None:
            epi(part, in_refs[2:], out_refs, pids)
        else:
            acc = refs[n_in + n_out]
            k = pids[red]
            _accumulate(acc, part, k)

            @pl.when(k == n_red - 1)
            def _():
                epi(acc[...], in_refs[2:], out_refs, pids)

    scratch = [] if red is None else [pltpu.VMEM(acc_shape, F32)]
    return pl.pallas_call(body, name=name, grid=grid, in_specs=in_specs, out_specs=out_specs, out_shape=outs,
                          scratch_shapes=scratch, compiler_params=_params())(*ins)


def _store(part, extra, outs, pids):
    outs[0][...] = part.astype(outs[0].dtype)


def _rms_fwd(name, x, g, tm):
    s, d = x.shape

    def body(x_ref, g_ref, o_ref):
        xv = x_ref[...]
        r = lax.rsqrt(jnp.mean(xv * xv, axis=-1, keepdims=True) + NORM_EPS)
        o_ref[...] = ((xv * r) * g_ref[...]).astype(BF)

    return pl.pallas_call(body, name=name, grid=(s // tm,),
                          in_specs=[pl.BlockSpec((tm, d), lambda i: (i, 0)), pl.BlockSpec((1, d), lambda i: (0, 0))],
                          out_specs=pl.BlockSpec((tm, d), lambda i: (i, 0)), out_shape=_sds((s, d), BF),
                          compiler_params=_params())(x, g.reshape(1, d))


def _norm_in_proj(name, x, g, wbuf, w_index, n_j, tm):
    s = x.shape[0]
    ti = IN_PROJ_ROWS if s % IN_PROJ_ROWS == 0 else tm

    def body(x_ref, g_ref, w_ref, h_ref, xn_ref):
        @pl.when(pl.program_id(1) == 0)
        def _():
            xv = x_ref[...]
            r = lax.rsqrt(jnp.mean(xv * xv, axis=-1, keepdims=True) + NORM_EPS)
            xn_ref[...] = ((xv * r) * g_ref[...]).astype(BF)

        h_ref[...] = _dot(xn_ref[...], w_ref[...], NN).astype(BF)

    row = lambda i, j: (i, 0)
    return pl.pallas_call(
        body, name=name, grid=(s // ti, n_j),
        in_specs=[pl.BlockSpec((ti, D_MODEL), row), pl.BlockSpec((1, D_MODEL), lambda i, j: (0, 0)),
                  pl.BlockSpec((None, D_MODEL, 512), w_index)],
        out_specs=[pl.BlockSpec((ti, 512), lambda i, j: (i, j)), pl.BlockSpec((ti, D_MODEL), row)],
        out_shape=[_sds((s, n_j * 512), BF), _sds((s, D_MODEL), BF)], compiler_params=_params())(x, g.reshape(1, D_MODEL), wbuf)


def _tn(name, a, b, out_shape, out_block, out_index, a_cols, b_cols, grid, a_index, b_index, dep=()):
    s = a.shape[-2]
    a_block = (s, a_cols) if a.ndim == 2 else (None, s, a_cols)
    b_block = (s, b_cols) if b.ndim == 2 else (None, s, b_cols)

    def epi(part, extra, outs, pids):
        outs[0][...] = part.astype(BF).reshape(outs[0].shape)

    return _mm(name, grid, [a, b] + list(dep), [pl.BlockSpec(a_block, a_index), pl.BlockSpec(b_block, b_index)] + [_ANY] * len(dep),
               [_sds(out_shape, BF)], [pl.BlockSpec(out_block, out_index)], TN, epi)[0]


def _pool_window_fwd(name, h, tm):
    s = h.shape[0]
    hb = POOL_HALO

    def body(u_ref, halo_ref, o_ref, e_ref, a_ref, b_ref):
        i = pl.program_id(0)
        row = lax.broadcasted_iota(jnp.int32, (tm, 1), 0) + i * tm
        for g, w in enumerate(POOL_WINDOWS):
            cs = slice(g * POOL_GROUP, (g + 1) * POOL_GROUP)
            e_ref[0:hb, :] = jnp.where(i > 0, halo_ref[:, cs].astype(F32), 0.0)
            e_ref[hb:, :] = u_ref[:, cs].astype(F32)
            src, bufs = e_ref, (a_ref, b_ref)
            for lv in range(1, w.bit_length()):
                dst, st, sh = bufs[(lv - 1) % 2], 8 * lv, 2 ** (lv - 1)
                n = hb + tm - st
                dst[st:, :] = src[st:, :] + src[pl.ds(st - sh, n), :]
                src = dst
            cnt = jnp.minimum(row + 1, w).astype(F32)
            o_ref[:, cs] = (src[hb:, :] / cnt - u_ref[:, cs].astype(F32)).astype(BF)

    per = tm // hb
    return pl.pallas_call(
        body, name=name, grid=(s // tm,),
        in_specs=[pl.BlockSpec((tm, D_INNER), lambda i: (i, 0)),
                  pl.BlockSpec((hb, D_INNER), lambda i: (jnp.maximum(i * per - 1, 0), 0))],
        out_specs=pl.BlockSpec((tm, D_INNER), lambda i: (i, 0)), out_shape=_sds((s, D_INNER), BF),
        scratch_shapes=[pltpu.VMEM((hb + tm, POOL_GROUP), F32)] * 3, compiler_params=_params())(h, h)


def _pool_window_bwd(name, dp, tm, dh):
    s = dp.shape[0]
    nt = s // tm
    hb = POOL_HALO

    def body(d_ref, halo_ref, dh_in_ref, o_ref, e_ref, a_ref, b_ref):
        i = pl.program_id(0)
        row = lax.broadcasted_iota(jnp.int32, (tm, 1), 0) + i * tm
        hrow = lax.broadcasted_iota(jnp.int32, (hb, 1), 0) + (i + 1) * tm
        for g, w in enumerate(POOL_WINDOWS):
            cs = slice(g * POOL_GROUP, (g + 1) * POOL_GROUP)
            e_ref[0:tm, :] = d_ref[:, cs].astype(F32) / jnp.minimum(row + 1, w).astype(F32)
            e_ref[tm:, :] = jnp.where(i < nt - 1, halo_ref[:, cs].astype(F32) / jnp.minimum(hrow + 1, w).astype(F32), 0.0)
            src, bufs = e_ref, (a_ref, b_ref)
            for lv in range(1, w.bit_length()):
                dst, sh = bufs[(lv - 1) % 2], 2 ** (lv - 1)
                n = tm + hb - 8 * lv
                dst[0:n, :] = src[0:n, :] + src[pl.ds(sh, n), :]
                src = dst
            o_ref[:, cs] = (src[0:tm, :] - d_ref[:, cs].astype(F32)).astype(BF)

    per = tm // hb
    last = s // hb - 1
    return pl.pallas_call(
        body, name=name, grid=(nt,),
        in_specs=[pl.BlockSpec((tm, D_INNER), lambda i: (i, 0)),
                  pl.BlockSpec((hb, D_INNER), lambda i: (jnp.minimum((i + 1) * per, last), 0)), _ANY],
        out_specs=pl.BlockSpec((tm, D_INNER), lambda i: (i, 0)), out_shape=_sds(dh.shape, BF),
        input_output_aliases={2: 0},
        scratch_shapes=[pltpu.VMEM((hb + tm, POOL_GROUP), F32)] * 3, compiler_params=_params())(dp, dp, dh)


def _grp_block():
    return pl.BlockSpec((N_DEV, 64, POOL_GROUP), lambda i, g: (0, g, 0))


def _pool_fwd(x, l, w, tm):
    s = x.shape[0]
    nt = s // tm
    n = f"pool{l}"
    h, xn = _norm_in_proj(n + "_in", x, w["norm"], w["g_in"], lambda i, j: (j, 0, 0), 8, tm)
    pooled = _pool_window_fwd(n + "_win", h, tm)
    w = dict(w, **w["rest"]([h]))

    def gate(part, extra, outs, pids):
        z = extra[0][...].astype(F32)
        outs[0][...] = ((part * extra[1][...]) * (z * _sig(z))).astype(BF)

    tg = IN_PROJ_ROWS if s % IN_PROJ_ROWS == 0 else tm
    (gated,) = _mm(n + "_grp", (s // tg, 4), [pooled, w["g_grp"], h, w["scale"].reshape(1, D_INNER)],
                   [pl.BlockSpec((tg, 512), lambda i, g: (i, g)), _grp_block(),
                    pl.BlockSpec((tg, 512), lambda i, g: (i, 4 + g)), pl.BlockSpec((1, 512), lambda i, g: (0, g))],
                   [_sds((s, D_INNER), BF)], [pl.BlockSpec((tg, 512), lambda i, g: (i, g))], NN, gate)
    y = _out_proj(n + "_out", gated, w["g_out"], 0, x, tm)
    return y, dict(x=x, xn=xn, h=h, pooled=pooled, gated=gated, w=w)


def _out_proj(name, gated, g1024, row_block, x, tm):
    s = x.shape[0]

    def epi(part, extra, outs, pids):
        outs[0][...] = part + extra[0][...]

    row = pl.BlockSpec((tm, D_MODEL), lambda i: (i, 0))
    return _mm(name, (s // tm,), [gated, g1024, x],
               [pl.BlockSpec((tm, D_INNER), lambda i: (i, 0)), pl.BlockSpec((N_DEV, 256, D_MODEL), lambda i: (0, row_block, 0)), row],
               [_sds((s, D_MODEL), F32)], [row], NN, epi)[0]


def _w_out_nt_block(row_block):
    return pl.BlockSpec((2, 256, D_MODEL), lambda j, i: (j, row_block, 0))


def _in_proj_bwd(name, dh, wbuf, w_index, n_k, x, g, dy, tm, dep=()):
    s = x.shape[0]
    tm = IN_PROJ_ROWS if s % IN_PROJ_ROWS == 0 else tm

    def epi(acc, extra, outs, pids):
        dx, dg = _rms_bwd(acc, extra[0][...], extra[1][...], extra[2][...])
        outs[0][...] = dx
        outs[1][...] = dx.astype(BF)
        _accumulate(outs[2], dg, pids[0])

    row = lambda i, k: (i, 0)
    return _mm(name, (s // tm, n_k), [dh, wbuf, x, g.reshape(1, D_MODEL), dy] + list(dep),
               [pl.BlockSpec((tm, 512), lambda i, k: (i, k)), pl.BlockSpec((None, D_MODEL, 512), w_index),
                pl.BlockSpec((tm, D_MODEL), row), pl.BlockSpec((1, D_MODEL), lambda i, k: (0, 0)), pl.BlockSpec((tm, D_MODEL), row)]
               + [_ANY] * len(dep),
               [_sds((s, D_MODEL), F32), _sds((s, D_MODEL), BF), _sds((1, D_MODEL), F32)],
               [pl.BlockSpec((tm, D_MODEL), row), pl.BlockSpec((tm, D_MODEL), row), pl.BlockSpec((1, D_MODEL), lambda i, k: (0, 0))],
               NT, epi, red=1, acc_shape=(tm, D_MODEL))


def _w_out_grad(name, gated, dyb):
    s = gated.shape[0]
    return _tn(name, gated, dyb, (D_INNER, D_MODEL), (512, D_MODEL), lambda i: (i, 0), 512, D_MODEL, (4,),
               lambda i: (0, i), lambda i: (0, 0))


def _pool_bwd(dy, dyb, l, w, sv, tm, dep, early=None):
    s = dy.shape[0]
    nt = s // tm
    n = f"pool{l}b"
    h, pooled = sv["h"], sv["pooled"]
    scale = w["scale"].reshape(1, D_INNER)

    def gate_bwd(part, extra, outs, pids):
        z, sc = extra[0][...].astype(F32), extra[1][...]
        wg = extra[3][...].reshape(POOL_GROUP, POOL_GROUP)
        mpv = _dot(extra[2][...], wg, NN)
        sz, dsz = _silu_and_grad(z)
        dm = part * sz
        dmp = (dm * sc).astype(BF)
        outs[0][...] = dmp
        outs[1][...] = (part * (mpv * sc) * dsz).astype(BF)
        _accumulate(outs[2], jnp.sum(dm * mpv, axis=0, keepdims=True), pids[1])
        outs[3][...] = _dot(dmp, wg, NT).astype(BF)

    tile = lambda j, i: (i, j)
    dmp, dz, dscale, dpool = _mm(
        n + "_out", (4, nt), [dyb, w["g_out"], h, scale, pooled, w["g_grp"]] + dep,
        [pl.BlockSpec((tm, D_MODEL), lambda j, i: (i, 0)), _w_out_nt_block(0),
         pl.BlockSpec((tm, 512), lambda j, i: (i, 4 + j)), pl.BlockSpec((1, 512), lambda j, i: (0, j)), pl.BlockSpec((tm, 512), tile),
         pl.BlockSpec((N_DEV, 64, POOL_GROUP), lambda j, i: (0, j, 0))] + [_ANY] * len(dep),
        [_sds((s, D_INNER), BF), _sds((s, 2 * D_INNER), BF), _sds((1, D_INNER), F32), _sds((s, D_INNER), BF)],
        [pl.BlockSpec((tm, 512), tile), pl.BlockSpec((tm, 512), lambda j, i: (i, 4 + j)), pl.BlockSpec((1, 512), lambda j, i: (0, j)),
         pl.BlockSpec((tm, 512), tile)],
        NT, gate_bwd)
    g_out = _w_out_grad(n + "_gout", sv["gated"], dyb).reshape(N_DEV, 256, D_MODEL)
    g_grp = _tn(n + "_ggrp", pooled, dmp, (N_DEV, 256, 512), (N_DEV, 64, 512), lambda g: (0, g, 0),
                512, 512, (4,), lambda g: (0, g), lambda g: (0, g))
    dep = early(dict(g_grp=g_grp, g_out=g_out)) if early is not None else ()
    dh = _pool_window_bwd(n + "_win", dpool, tm, dz)
    g_in = _tn(n + "_gin", sv["xn"], dh, (N_DEV, D_MODEL, 512), (None, D_MODEL, 512), lambda j: (j, 0, 0),
               D_MODEL, 512, (8,), lambda j: (0, 0), lambda j: (0, j), dep)
    dep = early(dict(g_in=g_in)) if early is not None else ()
    dx, dxb, dnorm = _in_proj_bwd(n + "_in", dh, w["g_in"], lambda i, k: (k, 0, 0), 8, sv["x"], w["norm"], dy, tm, dep)
    return dx, dxb, dict(g_in=g_in, g_grp=g_grp, g_out=g_out, norm=dnorm[0], scale=dscale[0])


def _conv_in_index(i, j):
    return (j // 2, 0, j % 2)


def _conv_fwd(x, w, tm):
    s = x.shape[0]
    nt = s // tm
    h, xn = _norm_in_proj("conv_in", x, w["norm"], w["g_in"], _conv_in_index, 16, tm)
    per = tm // CONV_HALO

    def body(b_ref, c_ref, h_ref, z_ref, cp_ref, hp_ref, w_ref, o_ref, e_ref):
        i = pl.program_id(0)
        ch = c_ref[...].astype(F32) * h_ref[...].astype(F32)
        e_ref[0:CONV_HALO, :] = jnp.where(i > 0, cp_ref[...].astype(F32) * hp_ref[...].astype(F32), 0.0)
        e_ref[CONV_HALO:, :] = ch
        co = (w_ref[2:3, :] * ch + w_ref[1:2, :] * e_ref[pl.ds(CONV_HALO - 1, tm), :]
              + w_ref[0:1, :] * e_ref[pl.ds(CONV_HALO - 2, tm), :])
        z = z_ref[...].astype(F32)
        o_ref[...] = ((b_ref[...].astype(F32) * co) * (z * _sig(z))).astype(BF)

    def col(q):
        return pl.BlockSpec((tm, 512), lambda i, j: (i, 4 * q + j))

    def prev(q):
        return pl.BlockSpec((CONV_HALO, 512), lambda i, j: (jnp.maximum(i * per - 1, 0), 4 * q + j))

    gated = pl.pallas_call(
        body, name="conv_mix", grid=(nt, 4),
        in_specs=[col(0), col(1), col(2), col(3), prev(1), prev(2), pl.BlockSpec((3, 512), lambda i, j: (0, j))],
        out_specs=pl.BlockSpec((tm, 512), lambda i, j: (i, j)), out_shape=_sds((s, D_INNER), BF),
        scratch_shapes=[pltpu.VMEM((CONV_HALO + tm, 512), F32)], compiler_params=_params())(h, h, h, h, h, h, w["conv_w"])
    w = dict(w, **w["rest"]([gated]))
    y = _out_proj("conv_out", gated, w["g_out"], 0, x, tm)
    return y, dict(x=x, xn=xn, h=h, gated=gated, w=w)


def _conv_bwd(dy, dyb, w, sv, tm, dep, early):
    s = dy.shape[0]
    nt = s // tm
    h = sv["h"]
    per = tm // CONV_HALO
    last = s // CONV_HALO - 1
    n_dep = len(dep)

    def body(dy_ref, dyn_ref, wo_ref, b_ref, c_ref, h_ref, z_ref, cp_ref, hp_ref, bn_ref, zn_ref, w_ref, *rest):
        dall_ref, dw_ref, e_ref, f_ref = rest[n_dep:]
        db_ref, dc_ref, dh_ref, dz_ref = (dall_ref.at[:, q * 512:(q + 1) * 512] for q in range(4))
        i = pl.program_id(1)
        wo = wo_ref[...].reshape(512, D_MODEL)
        dg_tile = _dot(dy_ref[...], wo, NT)
        dg_next = _dot(dyn_ref[...], wo, NT)
        w0, w1, w2 = w_ref[0:1, :], w_ref[1:2, :], w_ref[2:3, :]
        c, hh, b = c_ref[...].astype(F32), h_ref[...].astype(F32), b_ref[...].astype(F32)
        ch = c * hh
        e_ref[0:CONV_HALO, :] = jnp.where(i > 0, cp_ref[...].astype(F32) * hp_ref[...].astype(F32), 0.0)
        e_ref[CONV_HALO:, :] = ch
        ch1 = e_ref[pl.ds(CONV_HALO - 1, tm), :]
        ch2 = e_ref[pl.ds(CONV_HALO - 2, tm), :]
        co = w2 * ch + w1 * ch1 + w0 * ch2
        sz, dsz = _silu_and_grad(z_ref[...].astype(F32))
        dgv = dg_tile
        dyv = dgv * sz
        dz_ref[...] = (dgv * (b * co) * dsz).astype(BF)
        db_ref[...] = (dyv * co).astype(BF)
        dco = dyv * b
        zn = zn_ref[...].astype(F32)
        f_ref[0:tm, :] = dco
        f_ref[tm:, :] = jnp.where(i < nt - 1, dg_next * (zn * _sig(zn)) * bn_ref[...].astype(F32), 0.0)
        dch = w2 * dco + w1 * f_ref[pl.ds(1, tm), :] + w0 * f_ref[pl.ds(2, tm), :]
        dc_ref[...] = (dch * hh).astype(BF)
        dh_ref[...] = (dch * c).astype(BF)
        for tap, shifted in enumerate((ch2, ch1, ch)):
            _accumulate(dw_ref.at[tap:tap + 1, :], jnp.sum(dco * shifted, axis=0, keepdims=True), i)

    def col(q):
        return pl.BlockSpec((tm, 512), lambda j, i: (i, 4 * q + j))

    def prev(q):
        return pl.BlockSpec((CONV_HALO, 512), lambda j, i: (jnp.maximum(i * per - 1, 0), 4 * q + j))

    def nxt(q):
        return pl.BlockSpec((CONV_HALO, 512), lambda j, i: (jnp.minimum((i + 1) * per, last), 4 * q + j))

    wspec = pl.BlockSpec((3, 512), lambda j, i: (0, j))
    dy_tile = pl.BlockSpec((tm, D_MODEL), lambda j, i: (i, 0))
    dy_next = pl.BlockSpec((CONV_HALO, D_MODEL), lambda j, i: (jnp.minimum((i + 1) * per, last), 0))
    dh, dw = pl.pallas_call(
        body, name="convb_mix", grid=(4, nt),
        in_specs=[dy_tile, dy_next, _w_out_nt_block(0), col(0), col(1), col(2), col(3), prev(1), prev(2), nxt(0), nxt(3), wspec]
        + [_ANY] * n_dep,
        out_specs=[pl.BlockSpec((tm, D_INNER), lambda j, i: (i, j)), wspec],
        out_shape=[_sds((s, 4 * D_INNER), BF), _sds((3, D_INNER), F32)],
        scratch_shapes=[pltpu.VMEM((CONV_HALO + tm, 512), F32)] * 2, compiler_params=_params(),
    )(dyb, dyb, w["g_out"], h, h, h, h, h, h, h, h, w["conv_w"], *dep)

    def w_block(kp):
        k = 4 * (kp % 4) + kp // 4
        return (k // 2, 0, k % 2)

    g_in = _tn("convb_gin", sv["xn"], dh, (N_DEV, D_MODEL, D_MODEL), (None, D_MODEL, 512), lambda j: (j // 2, 0, j % 2),
               D_MODEL, 512, (16,), lambda j: (0, 0), lambda j: (0, 4 * (j % 4) + j // 4))
    g_out = _w_out_grad("convb_gout", sv["gated"], dyb).reshape(N_DEV, 256, D_MODEL)
    dep = early(dict(g_in=g_in, g_out=g_out))
    dx, dxb, dnorm = _in_proj_bwd("convb_in", dh, w["g_in"], lambda i, kp: w_block(kp), 16, sv["x"], w["norm"], dy, tm, dep)
    return dx, dxb, dict(g_in=g_in, g_out=g_out, norm=dnorm[0], conv_w=dw)


def _attn_tiles(s):
    t = min(ATTN_TILE, s)
    return t, s // t


def _causal_keep(t, keys_on_rows):
    r = lax.broadcasted_iota(jnp.int32, (t, t), 0)
    c = lax.broadcasted_iota(jnp.int32, (t, t), 1)
    return (r <= c) if keys_on_rows else (c <= r)


def _mla_fwd(x, w, rope, tm):
    s = x.shape[0]
    nt = s // tm
    cosf, sinf, perm = rope
    xn = _rms_fwd("mla_rms", x, w["norm"], tm)

    def in_body(xn_ref, wq_ref, wkv_ref, wkr_ref, wz_ref, gq_ref, gkv_ref, cos_ref, sin_ref, p_ref,
                ql_ref, kvl_ref, qn_ref, kvn_ref, krr_ref, z_ref):
        xv = xn_ref[...]
        ql = _dot(xv, wq_ref[...], NN)
        kvl = _dot(xv, wkv_ref[...], NN)
        ql_ref[...] = ql
        kvl_ref[...] = kvl
        rq = lax.rsqrt(jnp.mean(ql * ql, axis=-1, keepdims=True) + NORM_EPS)
        qn_ref[...] = ((ql * rq) * gq_ref[...]).astype(BF)
        rkv = lax.rsqrt(jnp.mean(kvl * kvl, axis=-1, keepdims=True) + NORM_EPS)
        kvn_ref[...] = ((kvl * rkv) * gkv_ref[...]).astype(BF)
        kr = _dot(xv, wkr_ref[...], NN)
        krr_ref[...] = _rope_fwd(kr, cos_ref[...], sin_ref[...], p_ref[...]).astype(BF)
        z_ref[...] = _dot(xv, wz_ref[...], NN).astype(BF)

    def full(a):
        return pl.BlockSpec(a.shape, lambda i: (0,) * a.ndim)

    def rows(c):
        return pl.BlockSpec((tm, c), lambda i: (i, 0))

    gq, gkv = w["q_norm"].reshape(1, Q_RANK), w["kv_norm"].reshape(1, KV_RANK)
    q_lat, kv_lat, qn, kvn, krr, z = pl.pallas_call(
        in_body, name="mla_in", grid=(nt,),
        in_specs=[rows(D_MODEL), full(w["w_q"]), full(w["w_kv"]), full(w["w_kr"]), full(w["w_z"]), full(gq), full(gkv),
                  rows(QK_ROPE), rows(QK_ROPE), full(perm)],
        out_specs=[rows(Q_RANK), rows(KV_RANK), rows(Q_RANK), rows(KV_RANK), rows(QK_ROPE), rows(D_INNER)],
        out_shape=[_sds((s, Q_RANK), F32), _sds((s, KV_RANK), F32), _sds((s, Q_RANK), BF), _sds((s, KV_RANK), BF),
                   _sds((s, QK_ROPE), BF), _sds((s, D_INNER), BF)],
        compiler_params=_params())(xn, w["w_q"], w["w_kv"], w["w_kr"], w["w_z"], gq, gkv, cosf, sinf, perm)

    def q_epi(part, extra, outs, pids):
        r = part[:, 2 * QK_NOPE:]
        lane = lax.broadcasted_iota(jnp.int32, r.shape, 1)
        swapped = jnp.where((lane & (QK_ROPE - 1)) < QK_ROPE // 2, pltpu.roll(r, 2 * QK_ROPE - QK_ROPE // 2, axis=1),
                            pltpu.roll(r, QK_ROPE // 2, axis=1))
        roped = (r * extra[0][...] + swapped * extra[1][...]) * Q_PRESCALE
        for hh in range(2):
            outs[0][hh, :, 0:QK_NOPE] = (part[:, hh * QK_NOPE:(hh + 1) * QK_NOPE] * Q_PRESCALE).astype(BF)
            outs[0][hh, :, QK_NOPE:QK_DIM] = roped[:, hh * QK_ROPE:(hh + 1) * QK_ROPE].astype(BF)

    tp = IN_PROJ_ROWS if s % IN_PROJ_ROWS == 0 else tm
    rope_row = pl.BlockSpec((tp, QK_ROPE), lambda h, i: (i, 0))
    rope_pair = pl.BlockSpec((tp, 2 * QK_ROPE), lambda h, i: (i, 0))
    cos2, sin2 = jnp.concatenate([cosf, cosf], axis=1), jnp.concatenate([sinf, sinf], axis=1)
    (q,) = _mm("mla_qup", (N_HEADS // 2, s // tp), [qn, w["w_qpair"], cos2, sin2],
               [pl.BlockSpec((tp, Q_RANK), lambda h, i: (i, 0)), pl.BlockSpec((None, Q_RANK, 2 * QK_DIM), lambda h, i: (h, 0, 0)),
                rope_pair, rope_pair],
               [_sds((N_HEADS, s, QK_DIM), BF)], [pl.BlockSpec((2, tp, QK_DIM), lambda h, i: (h, i, 0))], NN, q_epi)

    def kv_epi(part, extra, outs, pids):
        for hh in range(2):
            base = hh * (QK_NOPE + V_DIM)
            outs[0][hh, :, 0:QK_NOPE] = part[:, base:base + QK_NOPE].astype(BF)
            outs[0][hh, :, QK_NOPE:QK_DIM] = extra[0][...]
            outs[1][hh] = part[:, base + QK_NOPE:base + QK_NOPE + V_DIM].astype(BF)

    k, v = _mm("mla_kvup", (N_HEADS // 2, s // tp), [kvn, w["g512"], krr],
               [pl.BlockSpec((tp, KV_RANK), lambda h, i: (i, 0)),
                pl.BlockSpec((None, KV_RANK, 512), lambda h, i: (h, 0, 0)), rope_row],
               [_sds((N_HEADS, s, QK_DIM), BF), _sds((N_HEADS, s, V_DIM), BF)],
               [pl.BlockSpec((2, tp, QK_DIM), lambda h, i: (h, i, 0)), pl.BlockSpec((2, tp, V_DIM), lambda h, i: (h, i, 0))],
               NN, kv_epi)

    t, nq = _attn_tiles(s)

    def attn_body(q_ref, k_ref, v_ref, z_ref, o_ref, g_ref, lse_ref):
        i = pl.program_id(1)

        def block(j, carry, masked):
            start = pl.multiple_of(j * t, t)
            out = []
            for hh, (m, lsum, acc) in enumerate(carry):
                sc = _dot(q_ref[hh], k_ref[hh, pl.ds(start, t), :], NT)
                if masked:
                    sc = jnp.where(_causal_keep(t, False), sc, NEG_BIG)
                mn = jnp.maximum(m, jnp.max(sc, axis=-1, keepdims=True))
                alpha = jnp.exp2(m - mn)
                p = jnp.exp2(sc - mn)
                lsum = alpha * lsum + jnp.sum(p, axis=-1, keepdims=True)
                acc = alpha * acc + _dot(p.astype(BF), v_ref[hh, pl.ds(start, t), :], NN)
                out.append((mn, lsum, acc))
            return tuple(out)

        init = ((jnp.full((t, 1), NEG_BIG, F32), jnp.zeros((t, 1), F32), jnp.zeros((t, V_DIM), F32)),) * ATTN_HEADS_PER_STEP
        carry = lax.fori_loop(0, i, lambda j, c: block(j, c, False), init)
        for hh, (m, lsum, acc) in enumerate(block(i, carry, True)):
            cols = slice(hh * V_DIM, (hh + 1) * V_DIM)
            o = acc / lsum
            z = z_ref[:, cols].astype(F32)
            o_ref[:, cols] = o
            g_ref[:, cols] = (o * (z * _sig(z))).astype(BF)
            lse_ref[hh] = _to_row(m + jnp.log(lsum) * LOG2_E)

    hp = ATTN_HEADS_PER_STEP
    head_col = pl.BlockSpec((t, hp * V_DIM), lambda h, i: (i, h))
    o, gated, lse = pl.pallas_call(
        attn_body, name="mla_attn", grid=(N_HEADS // hp, nq),
        in_specs=[pl.BlockSpec((hp, t, QK_DIM), lambda h, i: (h, i, 0)), pl.BlockSpec((hp, s, QK_DIM), lambda h, i: (h, 0, 0)),
                  pl.BlockSpec((hp, s, V_DIM), lambda h, i: (h, 0, 0)), head_col],
        out_specs=[head_col, head_col, pl.BlockSpec((hp, None, 1, t), lambda h, i: (h, i, 0, 0))],
        out_shape=[_sds((s, D_INNER), F32), _sds((s, D_INNER), BF), _sds((N_HEADS, nq, 1, t), F32)],
        compiler_params=_params())(q, k, v, z)
    y = _out_proj("mla_out", gated, w["g1024"], 0, x, tm)
    return y, dict(x=x, xn=xn, q_lat=q_lat, kv_lat=kv_lat, qn=qn, kvn=kvn, z=z, q=q, k=k, v=v, o=o, lse=lse, gated=gated)


def _mla_bwd(dy, dyb, w, sv, rope, tm, dep):
    s = dy.shape[0]
    nt = s // tm
    cosf, sinf, perm = rope
    t, nq = _attn_tiles(s)
    assert t == tm, "the row statistics of the backward are laid out per attention tile"
    q, k, v, lse = sv["q"], sv["k"], sv["v"], sv["lse"]

    def gate_bwd(part, extra, outs, pids):
        z, o = extra[0][...].astype(F32), extra[1][...]
        sz, dsz = _silu_and_grad(z)
        do = part * sz
        outs[0][...] = do.astype(BF)
        outs[1][...] = (part * o * dsz).astype(BF)
        prod = do * o
        for hh in range(4):
            outs[2][hh] = _to_row(jnp.sum(prod[:, hh * V_DIM:(hh + 1) * V_DIM], axis=-1, keepdims=True))

    tile = lambda j, i: (i, j)
    dob, dz, delta = _mm(
        "mlab_out", (4, nt), [dyb, w["g1024"], sv["z"], sv["o"]] + dep,
        [pl.BlockSpec((tm, D_MODEL), lambda j, i: (i, 0)), _w_out_nt_block(0),
         pl.BlockSpec((tm, 512), tile), pl.BlockSpec((tm, 512), tile)] + [_ANY] * len(dep),
        [_sds((s, D_INNER), BF), _sds((s, D_INNER), BF), _sds((N_HEADS, nt, 1, tm), F32)],
        [pl.BlockSpec((tm, 512), tile), pl.BlockSpec((tm, 512), tile), pl.BlockSpec((4, None, 1, tm), lambda j, i: (j, i, 0, 0))],
        NT, gate_bwd)

    hp = ATTN_HEADS_PER_STEP

    def attn_bwd_body(k_ref, v_ref, q_ref, do_ref, lse_ref, dl_ref, cos_ref, sin_ref, p_ref, dkv_ref, dkr_ref, dq_ref, dq_acc):
        j = pl.program_id(1)

        @pl.when(j == 0)
        def _():
            dq_acc[...] = jnp.zeros(dq_acc.shape, F32)

        def block(i, carry, masked):
            rows = pl.ds(pl.multiple_of(i * t, t), t)
            out = []
            for hh, (dk, dv) in enumerate(carry):
                kb, vb = k_ref[hh], v_ref[hh]
                qb, dob_ = q_ref[hh, rows, :], do_ref[rows, hh * V_DIM:(hh + 1) * V_DIM]
                st = _dot(kb, qb, NT)
                if masked:
                    st = jnp.where(_causal_keep(t, True), st, NEG_BIG)
                pt = jnp.exp2(st - lse_ref[hh, i])
                dv = dv + _dot(pt.astype(BF), dob_, NN)
                dst = (pt * (_dot(vb, dob_, NT) - dl_ref[hh, i])).astype(BF)
                dk = dk + _dot(dst, qb, NN)
                dq_acc[hh, rows, :] += _dot(dst, kb, TN)
                out.append((dk, dv))
            return tuple(out)

        init = ((jnp.zeros((t, QK_DIM), F32), jnp.zeros((t, V_DIM), F32)),) * hp
        carry = block(j, init, True)
        carry = lax.fori_loop(j + 1, nq, lambda i, c: block(i, c, False), carry)
        for hh, (dk, dv) in enumerate(carry):
            dk = dk * LN_2
            base = hh * 2 * V_DIM
            dkv_ref[:, base:base + QK_NOPE] = dk[:, 0:QK_NOPE].astype(BF)
            dkv_ref[:, base + QK_NOPE:base + 2 * V_DIM] = dv.astype(BF)
            dkr_ref[hh] = dk[:, QK_NOPE:]

        @pl.when(j == nq - 1)
        def _():
            for hh in range(hp):
                for c in range(nq):
                    rows = slice(c * t, (c + 1) * t)
                    dq = dq_acc[hh, rows, :] * ATTN_SCALE
                    dq_ref[hh, rows, 0:QK_NOPE] = dq[:, 0:QK_NOPE].astype(BF)
                    dq_ref[hh, rows, QK_NOPE:] = _rope_bwd(dq[:, QK_NOPE:], cos_ref[rows, :], sin_ref[rows, :], p_ref[...]).astype(BF)

    row_stats = pl.BlockSpec((hp, nq, 1, t), lambda h, j: (h, 0, 0, 0))
    seq_rope = pl.BlockSpec((s, QK_ROPE), lambda h, j: (0, 0))
    head_seq = pl.BlockSpec((hp, s, QK_DIM), lambda h, j: (h, 0, 0))
    dkv, dkr_h, dq = pl.pallas_call(
        attn_bwd_body, name="mlab_attn", grid=(N_HEADS // hp, nq),
        in_specs=[pl.BlockSpec((hp, t, QK_DIM), lambda h, j: (h, j, 0)), pl.BlockSpec((hp, t, V_DIM), lambda h, j: (h, j, 0)),
                  head_seq, pl.BlockSpec((s, hp * V_DIM), lambda h, j: (0, h)), row_stats, row_stats, seq_rope, seq_rope,
                  pl.BlockSpec((QK_ROPE, QK_ROPE), lambda h, j: (0, 0))],
        out_specs=[pl.BlockSpec((t, hp * 2 * V_DIM), lambda h, j: (j, h)), pl.BlockSpec((hp, t, QK_ROPE), lambda h, j: (h, j, 0)), head_seq],
        out_shape=[_sds((s, N_HEADS * 2 * V_DIM), BF), _sds((N_HEADS, s, QK_ROPE), F32), _sds((N_HEADS, s, QK_DIM), BF)],
        scratch_shapes=[pltpu.VMEM((hp, s, QK_DIM), F32)],
        compiler_params=_params())(k, v, q, dob, lse, delta, cosf, sinf, perm)

    def dkr_body(d_ref, cos_ref, sin_ref, p_ref, o_ref):
        tot = d_ref[0]
        for hh in range(1, N_HEADS):
            tot = tot + d_ref[hh]
        o_ref[...] = _rope_bwd(tot, cos_ref[...], sin_ref[...], p_ref[...]).astype(BF)

    r64 = pl.BlockSpec((tm, QK_ROPE), lambda i: (i, 0))
    dkr = pl.pallas_call(
        dkr_body, name="mlab_dkr", grid=(nt,),
        in_specs=[pl.BlockSpec((N_HEADS, tm, QK_ROPE), lambda i: (0, i, 0)), r64, r64, pl.BlockSpec((QK_ROPE, QK_ROPE), lambda i: (0, 0))],
        out_specs=r64, out_shape=_sds((s, QK_ROPE), BF), compiler_params=_params())(dkr_h, cosf, sinf, perm)

    def lat_epi(acc, extra, outs, pids):
        dx, dg = _rms_bwd(acc, extra[0][...], extra[1][...], None)
        outs[0][...] = dx.astype(BF)
        _accumulate(outs[1], dg, pids[0])

    tp = IN_PROJ_ROWS if s % IN_PROJ_ROWS == 0 else tm

    def lat_bwd(name, a, a_spec, b, b_spec, n_k, lat, g, rank):
        row = lambda i, k: (i, 0)
        one = lambda i, k: (0, 0)
        return _mm(name, (s // tp, n_k), [a, b, lat, g.reshape(1, rank)],
                   [a_spec, b_spec, pl.BlockSpec((tp, rank), row), pl.BlockSpec((1, rank), one)],
                   [_sds((s, rank), BF), _sds((1, rank), F32)], [pl.BlockSpec((tp, rank), row), pl.BlockSpec((1, rank), one)],
                   NT, lat_epi, red=1, acc_shape=(tp, rank))

    d_ql, g_qnorm = lat_bwd("mlab_qup", dq, pl.BlockSpec((None, tp, QK_DIM), lambda i, h: (h, i, 0)),
                            w["w_qh"], pl.BlockSpec((None, Q_RANK, QK_DIM), lambda i, h: (h, 0, 0)), N_HEADS,
                            sv["q_lat"], w["q_norm"], Q_RANK)
    d_kvl, g_kvnorm = lat_bwd("mlab_kvup", dkv, pl.BlockSpec((tp, 512), lambda i, kk: (i, kk)),
                              w["g512"], pl.BlockSpec((None, KV_RANK, 512), lambda i, kk: (kk, 0, 0)), N_DEV,
                              sv["kv_lat"], w["kv_norm"], KV_RANK)

    def in_bwd(dql_ref, dkvl_ref, dkr_ref, dz_ref, wq_ref, wkv_ref, wkr_ref, wz_ref, x_ref, g_ref, dy_ref, dx_ref, dxb_ref, dg_ref):
        acc = (_dot(dql_ref[...], wq_ref[...], NT) + _dot(dkvl_ref[...], wkv_ref[...], NT)
               + _dot(dkr_ref[...], wkr_ref[...], NT) + _dot(dz_ref[...], wz_ref[...], NT))
        dx, dg = _rms_bwd(acc, x_ref[...], g_ref[...], dy_ref[...])
        dx_ref[...] = dx
        dxb_ref[...] = dx.astype(BF)
        _accumulate(dg_ref, dg, pl.program_id(0))

    def full(a):
        return pl.BlockSpec(a.shape, lambda i: (0,) * a.ndim)

    def rows(c):
        return pl.BlockSpec((tm, c), lambda i: (i, 0))

    gm = w["norm"].reshape(1, D_MODEL)
    dx, dxb, g_norm = pl.pallas_call(
        in_bwd, name="mlab_in", grid=(nt,),
        in_specs=[rows(Q_RANK), rows(KV_RANK), rows(QK_ROPE), rows(D_INNER), full(w["w_q"]), full(w["w_kv"]), full(w["w_kr"]),
                  full(w["w_z"]), rows(D_MODEL), full(gm), rows(D_MODEL)],
        out_specs=[rows(D_MODEL), rows(D_MODEL), full(gm)],
        out_shape=[_sds((s, D_MODEL), F32), _sds((s, D_MODEL), BF), _sds((1, D_MODEL), F32)],
        compiler_params=_params())(d_ql, d_kvl, dkr, dz, w["w_q"], w["w_kv"], w["w_kr"], w["w_z"], sv["x"], gm, dy)

    xn = sv["xn"]
    one = lambda j: (0, 0)
    g_q = _tn("mlab_gq", xn, d_ql, (D_MODEL, Q_RANK), (D_MODEL, Q_RANK), one, D_MODEL, Q_RANK, (1,), one, one)
    g_kv = _tn("mlab_gkv", xn, d_kvl, (D_MODEL, KV_RANK), (D_MODEL, KV_RANK), one, D_MODEL, KV_RANK, (1,), one, one)
    g_kr = _tn("mlab_gkr", xn, dkr, (D_MODEL, QK_ROPE), (D_MODEL, QK_ROPE), one, D_MODEL, QK_ROPE, (1,), one, one)
    g_z = _tn("mlab_gz", xn, dz, (D_MODEL, D_INNER), (D_MODEL, 512), lambda j: (0, j), D_MODEL, 512, (4,), one, lambda j: (0, j))
    g_in = jnp.concatenate([g_q, g_kv, g_kr, g_z], axis=1)
    g_qh = _tn("mlab_gqup", sv["qn"], dq, (N_HEADS, Q_RANK, QK_DIM), (None, Q_RANK, QK_DIM), lambda h: (h, 0, 0),
               Q_RANK, QK_DIM, (N_HEADS,), lambda h: (0, 0), lambda h: (h, 0, 0))
    g_kvup = _tn("mlab_gkvup", sv["kvn"], dkv, (N_DEV, KV_RANK, 512), (None, KV_RANK, 512), lambda j: (j, 0, 0),
                 KV_RANK, 512, (N_DEV,), lambda j: (0, 0), lambda j: (0, j))
    g_out = _w_out_grad("mlab_gout", sv["gated"], dyb)
    s384 = g_qh.reshape(N_DEV, 2, Q_RANK, QK_DIM).transpose(0, 2, 1, 3).reshape(N_DEV, Q_RANK, 2 * QK_DIM)
    s344 = g_in.reshape(D_MODEL, N_DEV, 344).transpose(1, 0, 2)
    return dx, dxb, dict(s344=s344, s384=s384, s512=g_kvup, s1024=g_out.reshape(N_DEV, 256, D_MODEL),
                         norm=g_norm[0], q_norm=g_qnorm[0], kv_norm=g_kvnorm[0])


def _loss_head(x, g, target, tm):
    s, d = x.shape

    def body(x_ref, g_ref, t_ref, dx_ref, dxb_ref, dg_ref, loss_ref):
        i = pl.program_id(0)
        xv, gv = x_ref[...], g_ref[...]
        r = lax.rsqrt(jnp.mean(xv * xv, axis=-1, keepdims=True) + NORM_EPS)
        err = (xv * r) * gv - t_ref[...]
        part = 0.5 * jnp.sum(jnp.mean(err * err, axis=-1, keepdims=True), axis=0, keepdims=True)
        dx, dg = _rms_bwd(err * (1.0 / d), xv, gv, None)
        dx_ref[...] = dx
        dxb_ref[...] = dx.astype(BF)
        _accumulate(dg_ref, dg, i)
        _accumulate(loss_ref, jnp.broadcast_to(part, loss_ref.shape), i)

    row = pl.BlockSpec((tm, d), lambda i: (i, 0))
    one = pl.BlockSpec((1, d), lambda i: (0, 0))
    return pl.pallas_call(
        body, name="loss_head", grid=(s // tm,), in_specs=[row, one, row],
        out_specs=[row, row, one, pl.BlockSpec((8, 128), lambda i: (0, 0))],
        out_shape=[_sds((s, d), F32), _sds((s, d), BF), _sds((1, d), F32), _sds((8, 128), F32)],
        compiler_params=_params())(x, g.reshape(1, d), target)


def _rope_tables(pos):
    inv_freq = ROPE_BASE ** (-jnp.arange(0, QK_ROPE, 2, dtype=F32) / QK_ROPE)
    ang = pos.astype(F32)[:, None] * inv_freq
    cos, sin = jnp.cos(ang), jnp.sin(ang)
    idx = jnp.arange(QK_ROPE)
    perm = (idx[:, None] == (idx[None, :] + QK_ROPE // 2) % QK_ROPE).astype(F32)
    return jnp.concatenate([cos, cos], axis=1), jnp.concatenate([-sin, sin], axis=1), perm


def _local_step(x, pos, target, final_norm, get_w, put_g):
    s = x.shape[0]
    tm = min(512, s)
    tl = IN_PROJ_ROWS if s % IN_PROJ_ROWS == 0 else tm
    rope = _rope_tables(pos)
    w0 = get_w(0, [])
    x1, sv0 = _pool_fwd(x, 0, w0, tl)
    w1 = get_w(1, [x1])
    x2, sv1 = _conv_fwd(x1, w1, tl)
    w2 = get_w(2, [x2])
    x3, sv2 = _mla_fwd(x2, w2, rope, tm)
    w3 = get_w(3, [x3])
    x4, sv3 = _pool_fwd(x3, 1, w3, tl)
    d4, d4b, g_final, loss = _loss_head(x4, final_norm, target, tm)
    d3, d3b, gp1 = _pool_bwd(d4, d4b, 1, sv3["w"], sv3, tl, [])
    dep = put_g(3, gp1)
    d2, d2b, gm = _mla_bwd(d3, d3b, w2, sv2, rope, tm, dep)
    dep = put_g(2, gm)
    sent = {}

    def send_conv(part):
        sent["dep"] = put_g(1, part)
        return sent["dep"]

    d1, d1b, gc = _conv_bwd(d2, d2b, sv1["w"], sv1, tl, dep, send_conv)
    d0, _, gp0 = _pool_bwd(d1, d1b, 0, sv0["w"], sv0, tl, sent["dep"], early=lambda part: put_g(0, part))
    put_g(4, {0: dict(gp0, final_norm=g_final[0]), 1: gc, 2: gm, 3: gp1})
    return loss[0, 0], d0


def _pack_groups(p):
    bf = lambda a: a.astype(BF)
    grp = lambda l: bf(p["pool_w_grp"][l].reshape(4 * 64, POOL_GROUP))
    return [[bf(p["pool_w_in"][0]), _pack_small(p, SMALL_ROWS_AG)],
            [grp(0), bf(p["pool_w_out"][0])],
            [bf(p["conv_w_in"][0])],
            [bf(p["conv_w_out"][0])],
            [bf(p[k][0]) for k in ("mla_w_in", "mla_w_q_up", "mla_w_kv_up", "mla_w_out")],
            [bf(p["pool_w_in"][1]), grp(1), bf(p["pool_w_out"][1])]]


_SMALL_SHARDED = ("pool_norm", "pool_scale", "mla_norm", "mla_q_norm", "mla_kv_norm", "conv_w")
_SMALL_REPLICATED = ("conv_norm", "final_norm")


def _pack_small(p, rows, with_replicated=False):
    parts = [p[k].reshape(-1) for k in _SMALL_SHARDED]
    if with_replicated:
        parts += [p[k].reshape(-1) for k in _SMALL_REPLICATED]
    flat = jnp.concatenate(parts)
    return jnp.pad(flat, (0, rows * 128 - flat.shape[0])).reshape(rows, 128)


_SMALL_SHARD_SHAPES = dict(pool_norm=(2, 128), pool_scale=(2, 256), mla_norm=(1, 128), mla_q_norm=(1, 48),
                           mla_kv_norm=(1, 32), conv_w=(1, 3, 256), conv_norm=(1, 1024), final_norm=(1024,))


def _unpack_small(buf, with_replicated=False):
    flat = buf.reshape(-1)
    out, off = {}, 0
    for k in _SMALL_SHARDED + (_SMALL_REPLICATED if with_replicated else ()):
        shp = _SMALL_SHARD_SHAPES[k]
        n = 1
        for d in shp:
            n *= d
        out[k] = flat[off:off + n].reshape(shp)
        off += n
    return out


def _small_views(gsmall):
    flat = gsmall.reshape(N_DEV, -1)

    def cols(off, rows, width):
        return flat[:, off:off + rows * width].reshape(N_DEV, rows, width).transpose(1, 0, 2).reshape(rows, N_DEV * width)

    return dict(pool_norm=cols(0, 2, 128), pool_scale=cols(256, 2, 256), mla_norm=cols(768, 1, 128)[0],
                q_norm=cols(896, 1, 48)[0], kv_norm=cols(944, 1, 32)[0], conv_w=cols(976, 3, 256))


def _pair_columns(g384):
    heads = g384.reshape(N_DEV, Q_RANK, 2, QK_DIM)
    nope = heads[..., :QK_NOPE].reshape(N_DEV, Q_RANK, 2 * QK_NOPE)
    rope = heads[..., QK_NOPE:].reshape(N_DEV, Q_RANK, 2 * QK_ROPE)
    return jnp.concatenate([nope, rope], axis=-1)


def _layer_weights(layer, bufs, small, conv_norm):
    if layer in (0, 3):
        l = 0 if layer == 0 else 1
        return dict(g_in=bufs[0], norm=small["pool_norm"][l], scale=small["pool_scale"][l])
    if layer == 1:
        return dict(g_in=bufs[0], norm=conv_norm.reshape(D_MODEL), conv_w=small["conv_w"])
    g344, g384, g512, g1024 = bufs
    w_in = g344.transpose(1, 0, 2).reshape(D_MODEL, N_DEV * 344)
    return dict(
        g512=g512, g1024=g1024,
        w_q=w_in[:, :Q_RANK], w_kv=w_in[:, Q_RANK:Q_RANK + KV_RANK],
        w_kr=w_in[:, Q_RANK + KV_RANK:Q_RANK + KV_RANK + QK_ROPE], w_z=w_in[:, Q_RANK + KV_RANK + QK_ROPE:],
        w_qh=g384.reshape(N_DEV, Q_RANK, 2, QK_DIM).transpose(0, 2, 1, 3).reshape(N_HEADS, Q_RANK, QK_DIM),
        w_qpair=_pair_columns(g384),
        norm=small["mla_norm"], q_norm=small["q_norm"], kv_norm=small["kv_norm"])


_GRAD_KEYS = {0: ("g_in", "g_grp", "g_out"), 3: ("g_in", "g_grp", "g_out"), 1: ("g_in", "g_out"), 2: ("s344", "s384", "s512", "s1024")}
_GRAD_PARAM = {0: dict(g_in="pool_w_in", g_grp="pool_w_grp", g_out="pool_w_out"), 1: dict(g_in="conv_w_in", g_out="conv_w_out"),
               2: dict(s344="mla_w_in", s384="mla_w_q_up", s512="mla_w_kv_up", s1024="mla_w_out")}
_GRAD_PARAM[3] = _GRAD_PARAM[0]


def _grad_group(layer, g):
    keys = tuple(k for k in _GRAD_KEYS[layer] if k in g)
    return keys, [g[k] for k in keys]


def _pack_small_grads(g):
    def split(a, rows, width):
        return a.reshape(rows, N_DEV, width).transpose(1, 0, 2).reshape(N_DEV, rows * width)

    rep = lambda a: jnp.broadcast_to(a.reshape(1, -1), (N_DEV, a.size))
    flat = jnp.concatenate([
        split(jnp.stack([g[0]["norm"], g[3]["norm"]]), 2, 128), split(jnp.stack([g[0]["scale"], g[3]["scale"]]), 2, 256),
        split(g[2]["norm"], 1, 128), split(g[2]["q_norm"], 1, 48), split(g[2]["kv_norm"], 1, 32), split(g[1]["conv_w"], 3, 256),
        rep(g[1]["norm"]), rep(g[0]["final_norm"])], axis=1)
    return jnp.pad(flat, ((0, 0), (0, SMALL_ROWS_RS * 128 - flat.shape[1]))).reshape(N_DEV, SMALL_ROWS_RS, 128)


def _peers(x, y, c):
    for k in range(1, N_DEV):
        px = 1 - x if k & 4 else x
        py = 1 - y if k & 2 else y
        pc = 1 - c if k & 1 else c
        yield k - 1, (px, py, pc), 4 * px + 2 * py + pc


_SIBLING = (0,)
_ICI_DIRECT = (1, 3, 5)


def _remote_copies(srcs, lands, send_sems, recv_sems, gather, ks=None):
    x, y, c = lax.axis_index("x"), lax.axis_index("y"), lax.axis_index("c")
    me = 4 * x + 2 * y + c
    copies = []
    for k, peer, pidx in _peers(x, y, c):
        if ks is not None and k not in ks:
            continue
        for a, (src, land) in enumerate(zip(srcs, lands)):
            copies.append(pltpu.make_async_remote_copy(
                src_ref=src if gather else src.at[pidx], dst_ref=land.at[me],
                send_sem=send_sems.at[a * (N_DEV - 1) + k], recv_sem=recv_sems.at[a * (N_DEV - 1) + k],
                device_id=peer, device_id_type=pl.DeviceIdType.MESH))
    return copies


def _relay_copies(lands, send_sems, recv_sems):
    x, y, c = lax.axis_index("x"), lax.axis_index("y"), lax.axis_index("c")
    copies = []
    for j, k in enumerate(_ICI_DIRECT):
        px = 1 - x if (k + 1) & 4 else x
        py = 1 - y if (k + 1) & 2 else y
        slot = 4 * px + 2 * py + c
        for a, land in enumerate(lands):
            copies.append(pltpu.make_async_remote_copy(
                src_ref=land.at[slot], dst_ref=land.at[slot],
                send_sem=send_sems.at[a * len(_ICI_DIRECT) + j], recv_sem=recv_sems.at[a * len(_ICI_DIRECT) + j],
                device_id=(x, y, 1 - c), device_id_type=pl.DeviceIdType.MESH))
    return copies


_HBM = pl.BlockSpec(memory_space=pltpu.HBM)
_SEM = pl.BlockSpec(memory_space=pltpu.SEMAPHORE)
_EFFECT = pltpu.SideEffectType.DATAFLOW_SIDE_EFFECTING


def _own_slabs(name, arrays, gather, dep):
    n, nd = len(arrays), len(dep)
    me = (4 * lax.axis_index("x") + 2 * lax.axis_index("y") + lax.axis_index("c")).astype(jnp.int32).reshape(1)

    def body(me_ref, *refs):
        for a in range(n):
            refs[n + nd + a][...] = refs[a][...]

    def slab(shape):
        return pl.BlockSpec((None,) + tuple(shape), lambda i, me_ref: (me_ref[0],) + (0,) * len(shape))

    def whole(shape):
        return pl.BlockSpec(tuple(shape), lambda i, me_ref: (0,) * len(shape))

    outs = [_sds(((N_DEV,) + a.shape) if gather else a.shape, a.dtype) for a in arrays]
    grid_spec = pltpu.PrefetchScalarGridSpec(
        num_scalar_prefetch=1, grid=(1,),
        in_specs=[whole(a.shape) if gather else slab(a.shape[1:]) for a in arrays] + [_ANY] * nd,
        out_specs=[slab(o.shape[1:]) for o in outs])
    return pl.pallas_call(body, name=name, grid_spec=grid_spec, out_shape=outs, compiler_params=_params())(me, *arrays, *dep)


def _exchange_start(name, arrays, lands, gather, ks=None):
    n = len(arrays)

    def body(*refs):
        srcs, lnds, send_sems, recv_sems, token = refs[:n], refs[n:2 * n], refs[2 * n], refs[2 * n + 1], refs[-1]
        for cp in _remote_copies(srcs, lnds, send_sems, recv_sems, gather, ks):
            cp.start()
        token[...] = jnp.zeros(token.shape, F32)

    sems = pltpu.SemaphoreType.DMA((n * (N_DEV - 1),))
    thru = [pltpu.HBM(a.shape, a.dtype) for a in list(arrays) + list(lands)]
    res = pl.pallas_call(
        body, name=name, in_specs=[_HBM] * (2 * n),
        out_specs=[_SEM, _SEM] + [_HBM] * (2 * n) + [pl.BlockSpec(memory_space=pltpu.VMEM)],
        out_shape=[sems, sems] + thru + [_sds((8, 128), F32)],
        input_output_aliases={i: 2 + i for i in range(2 * n)},
        compiler_params=pltpu.CompilerParams(has_side_effects=_EFFECT),
    )(*[pltpu.with_memory_space_constraint(a, pltpu.HBM) for a in list(arrays) + list(lands)])
    return res[0], res[1], list(res[2:2 + n]), list(res[2 + n:2 + 2 * n]), res[-1]


def _exchange_wait(name, send_sems, recv_sems, arrays, lands, after, gather):
    n = len(arrays)
    n_after = len(after)

    def body(*refs):
        srcs, lnds = refs[:n], refs[n:2 * n]
        copies = _remote_copies(srcs, lnds, refs[2 * n], refs[2 * n + 1], gather)
        for cp in copies:
            cp.wait_send()
        for cp in copies:
            cp.wait_recv()

    thru = [pltpu.HBM(a.shape, a.dtype) for a in list(arrays) + list(lands)]
    res = pl.pallas_call(
        body, name=name, in_specs=[_HBM] * (2 * n) + [_SEM, _SEM] + [pl.BlockSpec(memory_space=pl.ANY)] * n_after,
        out_specs=[_HBM] * (2 * n), out_shape=thru, input_output_aliases={i: i for i in range(2 * n)},
        compiler_params=pltpu.CompilerParams(has_side_effects=_EFFECT),
    )(*arrays, *lands, send_sems, recv_sems, *after)
    return list(res[n:])


def _gather_relay(name, send_sems, recv_sems, arrays, lands, after):
    n, n_after = len(arrays), len(after)

    def body(*refs):
        srcs, lnds, first_send, first_recv = refs[:n], refs[n:2 * n], refs[2 * n], refs[2 * n + 1]
        send2, recv2, token = refs[2 * n + 2 + n_after], refs[2 * n + 3 + n_after], refs[-1]
        for cp in _remote_copies(srcs, lnds, first_send, first_recv, True, _ICI_DIRECT):
            cp.wait_recv()
        for cp in _relay_copies(lnds, send2, recv2):
            cp.start()
        token[...] = jnp.zeros(token.shape, F32)

    sems = pltpu.SemaphoreType.DMA((n * len(_ICI_DIRECT),))
    thru = [pltpu.HBM(a.shape, a.dtype) for a in list(arrays) + list(lands)]
    res = pl.pallas_call(
        body, name=name, in_specs=[_HBM] * (2 * n) + [_SEM, _SEM] + [_ANY] * n_after,
        out_specs=[_SEM, _SEM] + [_HBM] * (2 * n) + [pl.BlockSpec(memory_space=pltpu.VMEM)],
        out_shape=[sems, sems] + thru + [_sds((8, 128), F32)],
        input_output_aliases={i: 2 + i for i in range(2 * n)},
        compiler_params=pltpu.CompilerParams(has_side_effects=_EFFECT),
    )(*arrays, *lands, send_sems, recv_sems, *after)
    return res[0], res[1], list(res[2:2 + n]), list(res[2 + n:2 + 2 * n]), res[-1]


def _gather_wait2(name, send_sems, recv_sems, send2, recv2, arrays, lands, after):
    n, n_after = len(arrays), len(after)

    def body(*refs):
        srcs, lnds = refs[:n], refs[n:2 * n]
        s1, r1, s2, r2 = refs[2 * n:2 * n + 4]
        for cp in _remote_copies(srcs, lnds, s1, r1, True, _SIBLING + _ICI_DIRECT):
            cp.wait_send()
        for cp in _remote_copies(srcs, lnds, s1, r1, True, _SIBLING):
            cp.wait_recv()
        relay = _relay_copies(lnds, s2, r2)
        for cp in relay:
            cp.wait_send()
        for cp in relay:
            cp.wait_recv()

    thru = [pltpu.HBM(a.shape, a.dtype) for a in list(arrays) + list(lands)]
    res = pl.pallas_call(
        body, name=name, in_specs=[_HBM] * (2 * n) + [_SEM] * 4 + [_ANY] * n_after,
        out_specs=[_HBM] * (2 * n), out_shape=thru, input_output_aliases={i: i for i in range(2 * n)},
        compiler_params=pltpu.CompilerParams(has_side_effects=_EFFECT),
    )(*arrays, *lands, send_sems, recv_sems, send2, recv2, *after)
    return list(res[n:])


def _adamw_math(g, w, m, v):
    m = ADAM_B1 * m + (1.0 - ADAM_B1) * g
    v = ADAM_B2 * v + (1.0 - ADAM_B2) * (g * g)
    m_hat = m / (1.0 - ADAM_B1 ** ADAM_STEP)
    v_hat = v / (1.0 - ADAM_B2 ** ADAM_STEP)
    delta = -ADAM_LR * (m_hat / (jnp.sqrt(v_hat) + ADAM_EPS) + ADAM_WD * w)
    return delta, m, v


def _sum_adamw(name, recv, row_off, w, m, v, tr, layer=0):
    width = recv.shape[-1]
    w2, m2, v2 = (a.reshape(a.shape[0], -1, width) for a in (w, m, v))
    rows = w2.shape[1]
    base = row_off // tr

    def body(r_ref, w_ref, m_ref, v_ref, g_ref, d_ref, mo_ref, vo_ref):
        g = r_ref[0].astype(F32)
        for src in range(1, N_DEV):
            g = g + r_ref[src].astype(F32)
        delta, mn, vn = _adamw_math(g, w_ref[...], m_ref[...], v_ref[...])
        g_ref[...] = g
        d_ref[...] = delta
        mo_ref[...] = mn
        vo_ref[...] = vn

    blk = pl.BlockSpec((tr, width), lambda i: (i, 0))
    wblk = pl.BlockSpec((None, tr, width), lambda i: (layer, i, 0))
    return pl.pallas_call(
        body, name=name, grid=(rows // tr,),
        in_specs=[pl.BlockSpec((N_DEV, tr, width), lambda i: (0, base + i, 0)), wblk, wblk, wblk],
        out_specs=[blk] * 4, out_shape=[_sds((rows, width), F32)] * 4, compiler_params=_params())(recv, w2, m2, v2)


_WEIGHTS = ("pool_norm", "pool_w_in", "pool_w_grp", "pool_scale", "pool_w_out", "conv_norm", "conv_w_in", "conv_w", "conv_w_out",
            "mla_norm", "mla_w_in", "mla_q_norm", "mla_w_q_up", "mla_kv_norm", "mla_w_kv_up", "mla_w_out", "final_norm")


def _step(x, positions, loss_target, p, m, v):
    gathers, tokens, dep = [], [], []
    for group, arrays in enumerate(_pack_groups(p)):
        lands = _own_slabs(f"gather{group}_own", arrays, True, dep)
        ks = _SIBLING + _ICI_DIRECT if group < TWO_LEVEL_GROUPS else None
        ssem, rsem, arrays, lands, token = _exchange_start(f"gather{group}_start", arrays, lands, True, ks)
        gathers.append((ssem, rsem, arrays, lands))
        tokens.append(token)
        dep = [token]
    state = {}

    def wait_group(group, after):
        if group >= TWO_LEVEL_GROUPS:
            return _exchange_wait(f"gather{group}_wait", *gathers[group], after, True)
        ssem, rsem, arrays, lands = gathers[group]
        send2, recv2, arrays, lands, token = _gather_relay(f"gather{group}_relay", ssem, rsem, arrays, lands, after)
        return _gather_wait2(f"gather{group}_wait", ssem, rsem, send2, recv2, arrays, lands, [token])

    def get_w(layer, after):
        if layer == 0:
            bufs = wait_group(0, list(tokens))
            state["small"] = _small_views(bufs[1])
            rest = lambda later: dict(zip(("g_grp", "g_out"), wait_group(1, later)))
        elif layer == 1:
            bufs = wait_group(2, after)
            rest = lambda later: dict(g_out=wait_group(3, later)[0])
        elif layer == 2:
            bufs = wait_group(4, after)
        else:
            bufs = wait_group(5, after)
            rest = lambda later: dict(g_grp=bufs[1], g_out=bufs[2])
        w = _layer_weights(layer, bufs, state["small"], p["conv_norm"])
        if layer != 2:
            w["rest"] = rest
        return w

    scatters = []

    def put_g(layer, g):
        if layer == 4:
            keys, arrays = ("small",), [_pack_small_grads(g)]
        else:
            keys, arrays = _grad_group(layer, g)
        n = len(scatters)
        lands = _own_slabs(f"scatter{n}_own", arrays, False, [])
        ssem, rsem, arrays, lands, token = _exchange_start(f"scatter{n}_start", arrays, lands, False)
        scatters.append((layer, keys, (ssem, rsem, arrays, lands)))
        tokens.append(token)
        return [token]

    loss, grad_x = _local_step(x[0], positions[0], loss_target[0], p["final_norm"], get_w, put_g)

    res, after = {}, [tokens[-1]]
    for n, (layer, keys, handles) in enumerate(scatters):
        recv = _exchange_wait(f"scatter{n}_wait", *handles, after, False)
        if layer == 4:
            break
        l = 1 if layer == 3 else 0
        for key, buf in zip(keys, recv):
            name = _GRAD_PARAM[layer][key]
            tr = min(256, buf.shape[1]) if name != "mla_w_q_up" else buf.shape[1]
            res[name, l] = _sum_adamw(f"adam_{name}{l}", buf, 0, p[name], m[name], v[name], tr, l)
        after = [res[name, l][1]]
    small = _sum_adamw("adam_small", recv[0], 0, _pack_small(p, SMALL_ROWS_RS, True)[None], _pack_small(m, SMALL_ROWS_RS, True)[None],
                       _pack_small(v, SMALL_ROWS_RS, True)[None], SMALL_ROWS_RS)
    small = [_unpack_small(a, True) for a in small]
    final = {k: tuple(part[k] for part in small) for k in _SMALL_SHARDED + _SMALL_REPLICATED}
    for k in _WEIGHTS:
        if k not in final:
            layers = [res[k, l] for l in range(p[k].shape[0])]
            final[k] = tuple(jnp.stack([lay[part] for lay in layers]).reshape(p[k].shape) for part in range(4))
    res = final

    loss = lax.psum(loss, ("x", "y", "c"))
    out = [loss, grad_x[None]]
    for part in range(4):
        out += [res[k][part] for k in _WEIGHTS]
    return tuple(out)


def kernel(x, positions, pool_norm, pool_w_in, pool_w_grp, pool_scale, pool_w_out, conv_norm, conv_w_in, conv_w, conv_w_out, mla_norm, mla_w_in, mla_q_norm, mla_w_q_up, mla_kv_norm, mla_w_kv_up, mla_w_out, final_norm, loss_target, m_pool_norm, m_pool_w_in, m_pool_w_grp, m_pool_scale, m_pool_w_out, m_conv_norm, m_conv_w_in, m_conv_w, m_conv_w_out, m_mla_norm, m_mla_w_in, m_mla_q_norm, m_mla_w_q_up, m_mla_kv_norm, m_mla_w_kv_up, m_mla_w_out, m_final_norm, v_pool_norm, v_pool_w_in, v_pool_w_grp, v_pool_scale, v_pool_w_out, v_conv_norm, v_conv_w_in, v_conv_w, v_conv_w_out, v_mla_norm, v_mla_w_in, v_mla_q_norm, v_mla_w_q_up, v_mla_kv_norm, v_mla_w_kv_up, v_mla_w_out, v_final_norm):
    p = dict(pool_norm=pool_norm, pool_w_in=pool_w_in, pool_w_grp=pool_w_grp, pool_scale=pool_scale, pool_w_out=pool_w_out,
             conv_norm=conv_norm, conv_w_in=conv_w_in, conv_w=conv_w, conv_w_out=conv_w_out, mla_norm=mla_norm, mla_w_in=mla_w_in,
             mla_q_norm=mla_q_norm, mla_w_q_up=mla_w_q_up, mla_kv_norm=mla_kv_norm, mla_w_kv_up=mla_w_kv_up, mla_w_out=mla_w_out,
             final_norm=final_norm)
    m = dict(pool_norm=m_pool_norm, pool_w_in=m_pool_w_in, pool_w_grp=m_pool_w_grp, pool_scale=m_pool_scale, pool_w_out=m_pool_w_out,
             conv_norm=m_conv_norm, conv_w_in=m_conv_w_in, conv_w=m_conv_w, conv_w_out=m_conv_w_out, mla_norm=m_mla_norm,
             mla_w_in=m_mla_w_in, mla_q_norm=m_mla_q_norm, mla_w_q_up=m_mla_w_q_up, mla_kv_norm=m_mla_kv_norm,
             mla_w_kv_up=m_mla_w_kv_up, mla_w_out=m_mla_w_out, final_norm=m_final_norm)
    v = dict(pool_norm=v_pool_norm, pool_w_in=v_pool_w_in, pool_w_grp=v_pool_w_grp, pool_scale=v_pool_scale, pool_w_out=v_pool_w_out,
             conv_norm=v_conv_norm, conv_w_in=v_conv_w_in, conv_w=v_conv_w, conv_w_out=v_conv_w_out, mla_norm=v_mla_norm,
             mla_w_in=v_mla_w_in, mla_q_norm=v_mla_q_norm, mla_w_q_up=v_mla_w_q_up, mla_kv_norm=v_mla_kv_norm,
             mla_w_kv_up=v_mla_w_kv_up, mla_w_out=v_mla_w_out, final_norm=v_final_norm)
    return _step(x, positions, loss_target, p, m, v)
```

```python
import functools

import jax
import jax.numpy as jnp
from jax import lax
from jax.experimental import pallas as pl
from jax.experimental.pallas import tpu as pltpu

BF = jnp.bfloat16
F32 = jnp.float32

N_DEV = 8
D_MODEL = 1024
D_INNER = 2048
POOL_WINDOWS = (2, 4, 8, 16)
POOL_GROUP = 512
N_HEADS = 16
QK_NOPE = 128
QK_ROPE = 64
QK_DIM = QK_NOPE + QK_ROPE
V_DIM = 128
Q_RANK = 384
KV_RANK = 256
ATTN_SCALE = QK_DIM ** -0.5
LOG2_E = 1.4426950408889634
LN_2 = 0.6931471805599453
Q_PRESCALE = ATTN_SCALE * LOG2_E
ATTN_TILE = 512
ATTN_HEADS_PER_STEP = 2
ATTN_FWD_HEADS = 4
ROPE_BASE = 10000.0
NORM_EPS = 1e-6
NEG_BIG = -1e30

ADAM_LR = 0.001
ADAM_B1 = 0.9
ADAM_B2 = 0.999
ADAM_EPS = 1e-08
ADAM_WD = 0.01
ADAM_STEP = 10

VMEM_LIMIT_BYTES = 52 * 1024 * 1024
IN_PROJ_ROWS = 1024
POOL_HALO = 32
CONV_HALO = 16

NN = (((1,), (0,)), ((), ()))
NT = (((1,), (1,)), ((), ()))
TN = (((0,), (0,)), ((), ()))

TWO_LEVEL_GROUPS = 3
SMALL_ROWS_AG = 16
SMALL_ROWS_RS = 32


def _sds(shape, dtype):
    return jax.ShapeDtypeStruct(tuple(shape), dtype)


def _params():
    return pltpu.CompilerParams(vmem_limit_bytes=VMEM_LIMIT_BYTES)


_ANY = pl.BlockSpec(memory_space=pl.ANY)


def _dot(a, b, dims):
    return lax.dot_general(a, b, dims, preferred_element_type=F32)


def _sig(z):
    return 1.0 / (1.0 + jnp.exp(-z))


def _silu_and_grad(z):
    sig = _sig(z)
    return z * sig, sig * (1.0 + z * (1.0 - sig))


def _to_row(col):
    return jnp.broadcast_to(col, (col.shape[0], 128)).T[0:1, :]


def _rope_swap(x, p):
    pb = p.astype(BF)
    hi = x.astype(BF)
    r1 = x - hi.astype(F32)
    mid = r1.astype(BF)
    lo = (r1 - mid.astype(F32)).astype(BF)
    return (_dot(hi, pb, NN) + _dot(mid, pb, NN)) + _dot(lo, pb, NN)


def _rope_fwd(x, cosf, sinf, p):
    return x * cosf + _rope_swap(x, p) * sinf


def _rope_bwd(dy, cosf, sinf, p):
    return dy * cosf + _rope_swap(dy * sinf, p)


def _rms_bwd(dxn, x, g, res):
    r = lax.rsqrt(jnp.mean(x * x, axis=-1, keepdims=True) + NORM_EPS)
    v = dxn * g
    dx = r * v - x * ((r * r * r) * jnp.mean(v * x, axis=-1, keepdims=True))
    if res is not None:
        dx = dx + res
    dg = jnp.sum(dxn * (x * r), axis=0, keepdims=True)
    return dx, dg


def _accumulate(ref, val, step):
    @pl.when(step == 0)
    def _():
        ref[...] = val

    @pl.when(step > 0)
    def _():
        ref[...] += val


def _mm(name, grid, ins, in_specs, outs, out_specs, dims, epi, red=None, acc_shape=None):
    n_in, n_out = len(ins), len(outs)
    n_red = None if red is None else grid[red]

    def body(*refs):
        in_refs, out_refs = refs[:n_in], refs[n_in:n_in + n_out]
        pids = tuple(pl.program_id(ax) for ax in range(len(grid)))
        a, b = in_refs[0][...], in_refs[1][...]
        if a.ndim == 3:
            a = a.reshape(-1, a.shape[-1])
        if b.ndim == 3:
            b = b.reshape(-1, b.shape[-1])
        part = _dot(a.astype(BF), b.astype(BF), dims)
        if red is None:
            epi(part, in_refs[2:], out_refs, pids)
        else:
            acc = refs[n_in + n_out]
            k = pids[red]
            _accumulate(acc, part, k)

            @pl.when(k == n_red - 1)
            def _():
                epi(acc[...], in_refs[2:], out_refs, pids)

    scratch = [] if red is None else [pltpu.VMEM(acc_shape, F32)]
    return pl.pallas_call(body, name=name, grid=grid, in_specs=in_specs, out_specs=out_specs, out_shape=outs,
                          scratch_shapes=scratch, compiler_params=_params())(*ins)


def _store(part, extra, outs, pids):
    outs[0][...] = part.astype(outs[0].dtype)


def _rms_fwd(name, x, g, tm):
    s, d = x.shape

    def body(x_ref, g_ref, o_ref):
        xv = x_ref[...]
        r = lax.rsqrt(jnp.mean(xv * xv, axis=-1, keepdims=True) + NORM_EPS)
        o_ref[...] = ((xv * r) * g_ref[...]).astype(BF)

    return pl.pallas_call(body, name=name, grid=(s // tm,),
                          in_specs=[pl.BlockSpec((tm, d), lambda i: (i, 0)), pl.BlockSpec((1, d), lambda i: (0, 0))],
                          out_specs=pl.BlockSpec((tm, d), lambda i: (i, 0)), out_shape=_sds((s, d), BF),
                          compiler_params=_params())(x, g.reshape(1, d))


def _norm_in_proj(name, x, g, wbuf, w_index, n_j, tm):
    s = x.shape[0]
    ti = IN_PROJ_ROWS if s % IN_PROJ_ROWS == 0 else tm

    def body(x_ref, g_ref, w_ref, h_ref, xn_ref):
        @pl.when(pl.program_id(1) == 0)
        def _():
            xv = x_ref[...]
            r = lax.rsqrt(jnp.mean(xv * xv, axis=-1, keepdims=True) + NORM_EPS)
            xn_ref[...] = ((xv * r) * g_ref[...]).astype(BF)

        h_ref[...] = _dot(xn_ref[...], w_ref[...], NN).astype(BF)

    row = lambda i, j: (i, 0)
    return pl.pallas_call(
        body, name=name, grid=(s // ti, n_j),
        in_specs=[pl.BlockSpec((ti, D_MODEL), row), pl.BlockSpec((1, D_MODEL), lambda i, j: (0, 0)),
                  pl.BlockSpec((None, D_MODEL, 512), w_index)],
        out_specs=[pl.BlockSpec((ti, 512), lambda i, j: (i, j)), pl.BlockSpec((ti, D_MODEL), row)],
        out_shape=[_sds((s, n_j * 512), BF), _sds((s, D_MODEL), BF)], compiler_params=_params())(x, g.reshape(1, D_MODEL), wbuf)


def _tn(name, a, b, out_shape, out_block, out_index, a_cols, b_cols, grid, a_index, b_index, dep=()):
    s = a.shape[-2]
    a_block = (s, a_cols) if a.ndim == 2 else (None, s, a_cols)
    b_block = (s, b_cols) if b.ndim == 2 else (None, s, b_cols)

    def epi(part, extra, outs, pids):
        outs[0][...] = part.astype(BF).reshape(outs[0].shape)

    return _mm(name, grid, [a, b] + list(dep), [pl.BlockSpec(a_block, a_index), pl.BlockSpec(b_block, b_index)] + [_ANY] * len(dep),
               [_sds(out_shape, BF)], [pl.BlockSpec(out_block, out_index)], TN, epi)[0]


def _pool_window_fwd(name, h, tm):
    s = h.shape[0]
    hb = POOL_HALO

    def body(u_ref, halo_ref, o_ref, e_ref, a_ref, b_ref):
        i = pl.program_id(0)
        row = lax.broadcasted_iota(jnp.int32, (tm, 1), 0) + i * tm
        for g, w in enumerate(POOL_WINDOWS):
            cs = slice(g * POOL_GROUP, (g + 1) * POOL_GROUP)
            e_ref[0:hb, :] = jnp.where(i > 0, halo_ref[:, cs].astype(F32), 0.0)
            e_ref[hb:, :] = u_ref[:, cs].astype(F32)
            src, bufs = e_ref, (a_ref, b_ref)
            for lv in range(1, w.bit_length()):
                dst, st, sh = bufs[(lv - 1) % 2], 8 * lv, 2 ** (lv - 1)
                n = hb + tm - st
                dst[st:, :] = src[st:, :] + src[pl.ds(st - sh, n), :]
                src = dst
            cnt = jnp.minimum(row + 1, w).astype(F32)
            o_ref[:, cs] = (src[hb:, :] / cnt - u_ref[:, cs].astype(F32)).astype(BF)

    per = tm // hb
    return pl.pallas_call(
        body, name=name, grid=(s // tm,),
        in_specs=[pl.BlockSpec((tm, D_INNER), lambda i: (i, 0)),
                  pl.BlockSpec((hb, D_INNER), lambda i: (jnp.maximum(i * per - 1, 0), 0))],
        out_specs=pl.BlockSpec((tm, D_INNER), lambda i: (i, 0)), out_shape=_sds((s, D_INNER), BF),
        scratch_shapes=[pltpu.VMEM((hb + tm, POOL_GROUP), F32)] * 3, compiler_params=_params())(h, h)


def _pool_window_bwd(name, dp, tm, dh):
    s = dp.shape[0]
    nt = s // tm
    hb = POOL_HALO

    def body(d_ref, halo_ref, dh_in_ref, o_ref, e_ref, a_ref, b_ref):
        i = pl.program_id(0)
        row = lax.broadcasted_iota(jnp.int32, (tm, 1), 0) + i * tm
        hrow = lax.broadcasted_iota(jnp.int32, (hb, 1), 0) + (i + 1) * tm
        for g, w in enumerate(POOL_WINDOWS):
            cs = slice(g * POOL_GROUP, (g + 1) * POOL_GROUP)
            e_ref[0:tm, :] = d_ref[:, cs].astype(F32) / jnp.minimum(row + 1, w).astype(F32)
            e_ref[tm:, :] = jnp.where(i < nt - 1, halo_ref[:, cs].astype(F32) / jnp.minimum(hrow + 1, w).astype(F32), 0.0)
            src, bufs = e_ref, (a_ref, b_ref)
            for lv in range(1, w.bit_length()):
                dst, sh = bufs[(lv - 1) % 2], 2 ** (lv - 1)
                n = tm + hb - 8 * lv
                dst[0:n, :] = src[0:n, :] + src[pl.ds(sh, n), :]
                src = dst
            o_ref[:, cs] = (src[0:tm, :] - d_ref[:, cs].astype(F32)).astype(BF)

    per = tm // hb
    last = s // hb - 1
    return pl.pallas_call(
        body, name=name, grid=(nt,),
        in_specs=[pl.BlockSpec((tm, D_INNER), lambda i: (i, 0)),
                  pl.BlockSpec((hb, D_INNER), lambda i: (jnp.minimum((i + 1) * per, last), 0)), _ANY],
        out_specs=pl.BlockSpec((tm, D_INNER), lambda i: (i, 0)), out_shape=_sds(dh.shape, BF),
        input_output_aliases={2: 0},
        scratch_shapes=[pltpu.VMEM((hb + tm, POOL_GROUP), F32)] * 3, compiler_params=_params())(dp, dp, dh)


def _grp_block():
    return pl.BlockSpec((N_DEV, 64, POOL_GROUP), lambda i, g: (0, g, 0))


def _pool_fwd(x, l, w, tm):
    s = x.shape[0]
    nt = s // tm
    n = f"pool{l}"
    h, xn = _norm_in_proj(n + "_in", x, w["norm"], w["g_in"], lambda i, j: (j, 0, 0), 8, tm)
    pooled = _pool_window_fwd(n + "_win", h, tm)
    w = dict(w, **w["rest"]([h]))

    def gate(part, extra, outs, pids):
        z = extra[0][...].astype(F32)
        outs[0][...] = ((part * extra[1][...]) * (z * _sig(z))).astype(BF)

    tg = IN_PROJ_ROWS if s % IN_PROJ_ROWS == 0 else tm
    (gated,) = _mm(n + "_grp", (s // tg, 4), [pooled, w["g_grp"], h, w["scale"].reshape(1, D_INNER)],
                   [pl.BlockSpec((tg, 512), lambda i, g: (i, g)), _grp_block(),
                    pl.BlockSpec((tg, 512), lambda i, g: (i, 4 + g)), pl.BlockSpec((1, 512), lambda i, g: (0, g))],
                   [_sds((s, D_INNER), BF)], [pl.BlockSpec((tg, 512), lambda i, g: (i, g))], NN, gate)
    y = _out_proj(n + "_out", gated, w["g_out"], 0, x, tm)
    return y, dict(x=x, xn=xn, h=h, pooled=pooled, gated=gated, w=w)


def _out_proj(name, gated, g1024, row_block, x, tm):
    s = x.shape[0]

    def epi(part, extra, outs, pids):
        outs[0][...] = part + extra[0][...]

    row = pl.BlockSpec((tm, D_MODEL), lambda i: (i, 0))
    return _mm(name, (s // tm,), [gated, g1024, x],
               [pl.BlockSpec((tm, D_INNER), lambda i: (i, 0)), pl.BlockSpec((N_DEV, 256, D_MODEL), lambda i: (0, row_block, 0)), row],
               [_sds((s, D_MODEL), F32)], [row], NN, epi)[0]


def _w_out_nt_block(row_block):
    return pl.BlockSpec((2, 256, D_MODEL), lambda j, i: (j, row_block, 0))


def _in_proj_bwd(name, dh, wbuf, w_index, n_k, x, g, dy, tm, dep=()):
    s = x.shape[0]
    tm = IN_PROJ_ROWS if s % IN_PROJ_ROWS == 0 else tm

    def epi(acc, extra, outs, pids):
        dx, dg = _rms_bwd(acc, extra[0][...], extra[1][...], extra[2][...])
        outs[0][...] = dx
        outs[1][...] = dx.astype(BF)
        _accumulate(outs[2], dg, pids[0])

    row = lambda i, k: (i, 0)
    return _mm(name, (s // tm, n_k), [dh, wbuf, x, g.reshape(1, D_MODEL), dy] + list(dep),
               [pl.BlockSpec((tm, 512), lambda i, k: (i, k)), pl.BlockSpec((None, D_MODEL, 512), w_index),
                pl.BlockSpec((tm, D_MODEL), row), pl.BlockSpec((1, D_MODEL), lambda i, k: (0, 0)), pl.BlockSpec((tm, D_MODEL), row)]
               + [_ANY] * len(dep),
               [_sds((s, D_MODEL), F32), _sds((s, D_MODEL), BF), _sds((1, D_MODEL), F32)],
               [pl.BlockSpec((tm, D_MODEL), row), pl.BlockSpec((tm, D_MODEL), row), pl.BlockSpec((1, D_MODEL), lambda i, k: (0, 0))],
               NT, epi, red=1, acc_shape=(tm, D_MODEL))


def _w_out_grad(name, gated, dyb):
    s = gated.shape[0]
    return _tn(name, gated, dyb, (D_INNER, D_MODEL), (512, D_MODEL), lambda i: (i, 0), 512, D_MODEL, (4,),
               lambda i: (0, i), lambda i: (0, 0))


def _pool_bwd(dy, dyb, l, w, sv, tm, dep, early=None):
    s = dy.shape[0]
    nt = s // tm
    n = f"pool{l}b"
    h, pooled = sv["h"], sv["pooled"]
    scale = w["scale"].reshape(1, D_INNER)

    def gate_bwd(part, extra, outs, pids):
        z, sc = extra[0][...].astype(F32), extra[1][...]
        wg = extra[3][...].reshape(POOL_GROUP, POOL_GROUP)
        mpv = _dot(extra[2][...], wg, NN)
        sz, dsz = _silu_and_grad(z)
        dm = part * sz
        dmp = (dm * sc).astype(BF)
        outs[0][...] = dmp
        outs[1][...] = (part * (mpv * sc) * dsz).astype(BF)
        _accumulate(outs[2], jnp.sum(dm * mpv, axis=0, keepdims=True), pids[1])
        outs[3][...] = _dot(dmp, wg, NT).astype(BF)

    tile = lambda j, i: (i, j)
    dmp, dz, dscale, dpool = _mm(
        n + "_out", (4, nt), [dyb, w["g_out"], h, scale, pooled, w["g_grp"]] + dep,
        [pl.BlockSpec((tm, D_MODEL), lambda j, i: (i, 0)), _w_out_nt_block(0),
         pl.BlockSpec((tm, 512), lambda j, i: (i, 4 + j)), pl.BlockSpec((1, 512), lambda j, i: (0, j)), pl.BlockSpec((tm, 512), tile),
         pl.BlockSpec((N_DEV, 64, POOL_GROUP), lambda j, i: (0, j, 0))] + [_ANY] * len(dep),
        [_sds((s, D_INNER), BF), _sds((s, 2 * D_INNER), BF), _sds((1, D_INNER), F32), _sds((s, D_INNER), BF)],
        [pl.BlockSpec((tm, 512), tile), pl.BlockSpec((tm, 512), lambda j, i: (i, 4 + j)), pl.BlockSpec((1, 512), lambda j, i: (0, j)),
         pl.BlockSpec((tm, 512), tile)],
        NT, gate_bwd)
    g_out = _w_out_grad(n + "_gout", sv["gated"], dyb).reshape(N_DEV, 256, D_MODEL)
    g_grp = _tn(n + "_ggrp", pooled, dmp, (N_DEV, 256, 512), (N_DEV, 64, 512), lambda g: (0, g, 0),
                512, 512, (4,), lambda g: (0, g), lambda g: (0, g))
    dep = early(dict(g_grp=g_grp, g_out=g_out)) if early is not None else ()
    dh = _pool_window_bwd(n + "_win", dpool, tm, dz)
    g_in = _tn(n + "_gin", sv["xn"], dh, (N_DEV, D_MODEL, 512), (None, D_MODEL, 512), lambda j: (j, 0, 0),
               D_MODEL, 512, (8,), lambda j: (0, 0), lambda j: (0, j), dep)
    dep = early(dict(g_in=g_in)) if early is not None else ()
    dx, dxb, dnorm = _in_proj_bwd(n + "_in", dh, w["g_in"], lambda i, k: (k, 0, 0), 8, sv["x"], w["norm"], dy, tm, dep)
    return dx, dxb, dict(g_in=g_in, g_grp=g_grp, g_out=g_out, norm=dnorm[0], scale=dscale[0])


def _conv_in_index(i, j):
    return (j // 2, 0, j % 2)


def _conv_fwd(x, w, tm):
    s = x.shape[0]
    nt = s // tm
    h, xn = _norm_in_proj("conv_in", x, w["norm"], w["g_in"], _conv_in_index, 16, tm)
    per = tm // CONV_HALO

    def body(b_ref, c_ref, h_ref, z_ref, cp_ref, hp_ref, w_ref, o_ref, e_ref):
        i = pl.program_id(0)
        ch = c_ref[...].astype(F32) * h_ref[...].astype(F32)
        e_ref[0:CONV_HALO, :] = jnp.where(i > 0, cp_ref[...].astype(F32) * hp_ref[...].astype(F32), 0.0)
        e_ref[CONV_HALO:, :] = ch
        co = (w_ref[2:3, :] * ch + w_ref[1:2, :] * e_ref[pl.ds(CONV_HALO - 1, tm), :]
              + w_ref[0:1, :] * e_ref[pl.ds(CONV_HALO - 2, tm), :])
        z = z_ref[...].astype(F32)
        o_ref[...] = ((b_ref[...].astype(F32) * co) * (z * _sig(z))).astype(BF)

    def col(q):
        return pl.BlockSpec((tm, 512), lambda i, j: (i, 4 * q + j))

    def prev(q):
        return pl.BlockSpec((CONV_HALO, 512), lambda i, j: (jnp.maximum(i * per - 1, 0), 4 * q + j))

    gated = pl.pallas_call(
        body, name="conv_mix", grid=(nt, 4),
        in_specs=[col(0), col(1), col(2), col(3), prev(1), prev(2), pl.BlockSpec((3, 512), lambda i, j: (0, j))],
        out_specs=pl.BlockSpec((tm, 512), lambda i, j: (i, j)), out_shape=_sds((s, D_INNER), BF),
        scratch_shapes=[pltpu.VMEM((CONV_HALO + tm, 512), F32)], compiler_params=_params())(h, h, h, h, h, h, w["conv_w"])
    w = dict(w, **w["rest"]([gated]))
    y = _out_proj("conv_out", gated, w["g_out"], 0, x, tm)
    return y, dict(x=x, xn=xn, h=h, gated=gated, w=w)


def _conv_bwd(dy, dyb, w, sv, tm, dep, early):
    s = dy.shape[0]
    nt = s // tm
    h = sv["h"]
    per = tm // CONV_HALO
    last = s // CONV_HALO - 1
    n_dep = len(dep)

    def body(dy_ref, dyn_ref, wo_ref, b_ref, c_ref, h_ref, z_ref, cp_ref, hp_ref, bn_ref, zn_ref, w_ref, *rest):
        dall_ref, dw_ref, e_ref, f_ref = rest[n_dep:]
        db_ref, dc_ref, dh_ref, dz_ref = (dall_ref.at[:, q * 512:(q + 1) * 512] for q in range(4))
        i = pl.program_id(1)
        wo = wo_ref[...].reshape(512, D_MODEL)
        dg_tile = _dot(dy_ref[...], wo, NT)
        dg_next = _dot(dyn_ref[...], wo, NT)
        w0, w1, w2 = w_ref[0:1, :], w_ref[1:2, :], w_ref[2:3, :]
        c, hh, b = c_ref[...].astype(F32), h_ref[...].astype(F32), b_ref[...].astype(F32)
        ch = c * hh
        e_ref[0:CONV_HALO, :] = jnp.where(i > 0, cp_ref[...].astype(F32) * hp_ref[...].astype(F32), 0.0)
        e_ref[CONV_HALO:, :] = ch
        ch1 = e_ref[pl.ds(CONV_HALO - 1, tm), :]
        ch2 = e_ref[pl.ds(CONV_HALO - 2, tm), :]
        co = w2 * ch + w1 * ch1 + w0 * ch2
        sz, dsz = _silu_and_grad(z_ref[...].astype(F32))
        dgv = dg_tile
        dyv = dgv * sz
        dz_ref[...] = (dgv * (b * co) * dsz).astype(BF)
        db_ref[...] = (dyv * co).astype(BF)
        dco = dyv * b
        zn = zn_ref[...].astype(F32)
        f_ref[0:tm, :] = dco
        f_ref[tm:, :] = jnp.where(i < nt - 1, dg_next * (zn * _sig(zn)) * bn_ref[...].astype(F32), 0.0)
        dch = w2 * dco + w1 * f_ref[pl.ds(1, tm), :] + w0 * f_ref[pl.ds(2, tm), :]
        dc_ref[...] = (dch * hh).astype(BF)
        dh_ref[...] = (dch * c).astype(BF)
        for tap, shifted in enumerate((ch2, ch1, ch)):
            _accumulate(dw_ref.at[tap:tap + 1, :], jnp.sum(dco * shifted, axis=0, keepdims=True), i)

    def col(q):
        return pl.BlockSpec((tm, 512), lambda j, i: (i, 4 * q + j))

    def prev(q):
        return pl.BlockSpec((CONV_HALO, 512), lambda j, i: (jnp.maximum(i * per - 1, 0), 4 * q + j))

    def nxt(q):
        return pl.BlockSpec((CONV_HALO, 512), lambda j, i: (jnp.minimum((i + 1) * per, last), 4 * q + j))

    wspec = pl.BlockSpec((3, 512), lambda j, i: (0, j))
    dy_tile = pl.BlockSpec((tm, D_MODEL), lambda j, i: (i, 0))
    dy_next = pl.BlockSpec((CONV_HALO, D_MODEL), lambda j, i: (jnp.minimum((i + 1) * per, last), 0))
    dh, dw = pl.pallas_call(
        body, name="convb_mix", grid=(4, nt),
        in_specs=[dy_tile, dy_next, _w_out_nt_block(0), col(0), col(1), col(2), col(3), prev(1), prev(2), nxt(0), nxt(3), wspec]
        + [_ANY] * n_dep,
        out_specs=[pl.BlockSpec((tm, D_INNER), lambda j, i: (i, j)), wspec],
        out_shape=[_sds((s, 4 * D_INNER), BF), _sds((3, D_INNER), F32)],
        scratch_shapes=[pltpu.VMEM((CONV_HALO + tm, 512), F32)] * 2, compiler_params=_params(),
    )(dyb, dyb, w["g_out"], h, h, h, h, h, h, h, h, w["conv_w"], *dep)

    def w_block(kp):
        k = 4 * (kp % 4) + kp // 4
        return (k // 2, 0, k % 2)

    g_in = _tn("convb_gin", sv["xn"], dh, (N_DEV, D_MODEL, D_MODEL), (None, D_MODEL, 512), lambda j: (j // 2, 0, j % 2),
               D_MODEL, 512, (16,), lambda j: (0, 0), lambda j: (0, 4 * (j % 4) + j // 4))
    g_out = _w_out_grad("convb_gout", sv["gated"], dyb).reshape(N_DEV, 256, D_MODEL)
    dep = early(dict(g_in=g_in, g_out=g_out))
    dx, dxb, dnorm = _in_proj_bwd("convb_in", dh, w["g_in"], lambda i, kp: w_block(kp), 16, sv["x"], w["norm"], dy, tm, dep)
    return dx, dxb, dict(g_in=g_in, g_out=g_out, norm=dnorm[0], conv_w=dw)


def _attn_tiles(s):
    t = min(ATTN_TILE, s)
    return t, s // t


def _causal_keep(t, keys_on_rows):
    r = lax.broadcasted_iota(jnp.int32, (t, t), 0)
    c = lax.broadcasted_iota(jnp.int32, (t, t), 1)
    return (r <= c) if keys_on_rows else (c <= r)


def _mla_fwd(x, w, rope, tm):
    s = x.shape[0]
    nt = s // tm
    cosf, sinf, perm = rope
    xn = _rms_fwd("mla_rms", x, w["norm"], tm)

    def in_body(xn_ref, wq_ref, wkv_ref, wkr_ref, wz_ref, gq_ref, gkv_ref, cos_ref, sin_ref, p_ref,
                ql_ref, kvl_ref, qn_ref, kvn_ref, krr_ref, z_ref):
        xv = xn_ref[...]
        ql = _dot(xv, wq_ref[...], NN)
        kvl = _dot(xv, wkv_ref[...], NN)
        ql_ref[...] = ql
        kvl_ref[...] = kvl
        rq = lax.rsqrt(jnp.mean(ql * ql, axis=-1, keepdims=True) + NORM_EPS)
        qn_ref[...] = ((ql * rq) * gq_ref[...]).astype(BF)
        rkv = lax.rsqrt(jnp.mean(kvl * kvl, axis=-1, keepdims=True) + NORM_EPS)
        kvn_ref[...] = ((kvl * rkv) * gkv_ref[...]).astype(BF)
        kr = _dot(xv, wkr_ref[...], NN)
        krr_ref[...] = _rope_fwd(kr, cos_ref[...], sin_ref[...], p_ref[...]).astype(BF)
        z_ref[...] = _dot(xv, wz_ref[...], NN).astype(BF)

    def full(a):
        return pl.BlockSpec(a.shape, lambda i: (0,) * a.ndim)

    def rows(c):
        return pl.BlockSpec((tm, c), lambda i: (i, 0))

    gq, gkv = w["q_norm"].reshape(1, Q_RANK), w["kv_norm"].reshape(1, KV_RANK)
    q_lat, kv_lat, qn, kvn, krr, z = pl.pallas_call(
        in_body, name="mla_in", grid=(nt,),
        in_specs=[rows(D_MODEL), full(w["w_q"]), full(w["w_kv"]), full(w["w_kr"]), full(w["w_z"]), full(gq), full(gkv),
                  rows(QK_ROPE), rows(QK_ROPE), full(perm)],
        out_specs=[rows(Q_RANK), rows(KV_RANK), rows(Q_RANK), rows(KV_RANK), rows(QK_ROPE), rows(D_INNER)],
        out_shape=[_sds((s, Q_RANK), F32), _sds((s, KV_RANK), F32), _sds((s, Q_RANK), BF), _sds((s, KV_RANK), BF),
                   _sds((s, QK_ROPE), BF), _sds((s, D_INNER), BF)],
        compiler_params=_params())(xn, w["w_q"], w["w_kv"], w["w_kr"], w["w_z"], gq, gkv, cosf, sinf, perm)

    def q_epi(part, extra, outs, pids):
        r = part[:, 2 * QK_NOPE:]
        lane = lax.broadcasted_iota(jnp.int32, r.shape, 1)
        swapped = jnp.where((lane & (QK_ROPE - 1)) < QK_ROPE // 2, pltpu.roll(r, 2 * QK_ROPE - QK_ROPE // 2, axis=1),
                            pltpu.roll(r, QK_ROPE // 2, axis=1))
        roped = (r * extra[0][...] + swapped * extra[1][...]) * Q_PRESCALE
        for hh in range(2):
            outs[0][hh, :, 0:QK_NOPE] = (part[:, hh * QK_NOPE:(hh + 1) * QK_NOPE] * Q_PRESCALE).astype(BF)
            outs[0][hh, :, QK_NOPE:QK_DIM] = roped[:, hh * QK_ROPE:(hh + 1) * QK_ROPE].astype(BF)

    tp = IN_PROJ_ROWS if s % IN_PROJ_ROWS == 0 else tm
    rope_row = pl.BlockSpec((tp, QK_ROPE), lambda h, i: (i, 0))
    rope_pair = pl.BlockSpec((tp, 2 * QK_ROPE), lambda h, i: (i, 0))
    cos2, sin2 = jnp.concatenate([cosf, cosf], axis=1), jnp.concatenate([sinf, sinf], axis=1)
    (q,) = _mm("mla_qup", (N_HEADS // 2, s // tp), [qn, w["w_qpair"], cos2, sin2],
               [pl.BlockSpec((tp, Q_RANK), lambda h, i: (i, 0)), pl.BlockSpec((None, Q_RANK, 2 * QK_DIM), lambda h, i: (h, 0, 0)),
                rope_pair, rope_pair],
               [_sds((N_HEADS, s, QK_DIM), BF)], [pl.BlockSpec((2, tp, QK_DIM), lambda h, i: (h, i, 0))], NN, q_epi)

    def kv_epi(part, extra, outs, pids):
        for hh in range(2):
            base = hh * (QK_NOPE + V_DIM)
            outs[0][hh, :, 0:QK_NOPE] = part[:, base:base + QK_NOPE].astype(BF)
            outs[0][hh, :, QK_NOPE:QK_DIM] = extra[0][...]
            outs[1][hh] = part[:, base + QK_NOPE:base + QK_NOPE + V_DIM].astype(BF)

    k, v = _mm("mla_kvup", (N_HEADS // 2, s // tp), [kvn, w["g512"], krr],
               [pl.BlockSpec((tp, KV_RANK), lambda h, i: (i, 0)),
                pl.BlockSpec((None, KV_RANK, 512), lambda h, i: (h, 0, 0)), rope_row],
               [_sds((N_HEADS, s, QK_DIM), BF), _sds((N_HEADS, s, V_DIM), BF)],
               [pl.BlockSpec((2, tp, QK_DIM), lambda h, i: (h, i, 0)), pl.BlockSpec((2, tp, V_DIM), lambda h, i: (h, i, 0))],
               NN, kv_epi)

    t, nq = _attn_tiles(s)

    def attn_body(q_ref, k_ref, v_ref, z_ref, o_ref, g_ref, lse_ref):
        i = pl.program_id(1)

        def block(j, carry, masked):
            start = pl.multiple_of(j * t, t)
            out = []
            for hh, (m, lsum, acc) in enumerate(carry):
                sc = _dot(q_ref[hh], k_ref[hh, pl.ds(start, t), :], NT)
                if masked:
                    sc = jnp.where(_causal_keep(t, False), sc, NEG_BIG)
                mn = jnp.maximum(m, jnp.max(sc, axis=-1, keepdims=True))
                alpha = jnp.exp2(m - mn)
                p = jnp.exp2(sc - mn)
                lsum = alpha * lsum + jnp.sum(p, axis=-1, keepdims=True)
                acc = alpha * acc + _dot(p.astype(BF), v_ref[hh, pl.ds(start, t), :], NN)
                out.append((mn, lsum, acc))
            return tuple(out)

        init = ((jnp.full((t, 1), NEG_BIG, F32), jnp.zeros((t, 1), F32), jnp.zeros((t, V_DIM), F32)),) * ATTN_FWD_HEADS
        carry = lax.fori_loop(0, i, lambda j, c: block(j, c, False), init)
        for hh, (m, lsum, acc) in enumerate(block(i, carry, True)):
            cols = slice(hh * V_DIM, (hh + 1) * V_DIM)
            o = acc / lsum
            z = z_ref[:, cols].astype(F32)
            o_ref[:, cols] = o
            g_ref[:, cols] = (o * (z * _sig(z))).astype(BF)
            lse_ref[hh] = _to_row(m + jnp.log(lsum) * LOG2_E)

    hp = ATTN_FWD_HEADS
    head_col = pl.BlockSpec((t, hp * V_DIM), lambda h, i: (i, h))
    o, gated, lse = pl.pallas_call(
        attn_body, name="mla_attn", grid=(N_HEADS // hp, nq),
        in_specs=[pl.BlockSpec((hp, t, QK_DIM), lambda h, i: (h, i, 0)), pl.BlockSpec((hp, s, QK_DIM), lambda h, i: (h, 0, 0)),
                  pl.BlockSpec((hp, s, V_DIM), lambda h, i: (h, 0, 0)), head_col],
        out_specs=[head_col, head_col, pl.BlockSpec((hp, None, 1, t), lambda h, i: (h, i, 0, 0))],
        out_shape=[_sds((s, D_INNER), F32), _sds((s, D_INNER), BF), _sds((N_HEADS, nq, 1, t), F32)],
        compiler_params=_params())(q, k, v, z)
    y = _out_proj("mla_out", gated, w["g1024"], 0, x, tm)
    return y, dict(x=x, xn=xn, q_lat=q_lat, kv_lat=kv_lat, qn=qn, kvn=kvn, z=z, q=q, k=k, v=v, o=o, lse=lse, gated=gated)


def _mla_bwd(dy, dyb, w, sv, rope, tm, dep):
    s = dy.shape[0]
    nt = s // tm
    cosf, sinf, perm = rope
    t, nq = _attn_tiles(s)
    assert t == tm, "the row statistics of the backward are laid out per attention tile"
    q, k, v, lse = sv["q"], sv["k"], sv["v"], sv["lse"]

    def gate_bwd(part, extra, outs, pids):
        z, o = extra[0][...].astype(F32), extra[1][...]
        sz, dsz = _silu_and_grad(z)
        do = part * sz
        outs[0][...] = do.astype(BF)
        outs[1][...] = (part * o * dsz).astype(BF)
        prod = do * o
        for hh in range(4):
            outs[2][hh] = _to_row(jnp.sum(prod[:, hh * V_DIM:(hh + 1) * V_DIM], axis=-1, keepdims=True))

    tile = lambda j, i: (i, j)
    dob, dz, delta = _mm(
        "mlab_out", (4, nt), [dyb, w["g1024"], sv["z"], sv["o"]] + dep,
        [pl.BlockSpec((tm, D_MODEL), lambda j, i: (i, 0)), _w_out_nt_block(0),
         pl.BlockSpec((tm, 512), tile), pl.BlockSpec((tm, 512), tile)] + [_ANY] * len(dep),
        [_sds((s, D_INNER), BF), _sds((s, D_INNER), BF), _sds((N_HEADS, nt, 1, tm), F32)],
        [pl.BlockSpec((tm, 512), tile), pl.BlockSpec((tm, 512), tile), pl.BlockSpec((4, None, 1, tm), lambda j, i: (j, i, 0, 0))],
        NT, gate_bwd)

    hp = ATTN_HEADS_PER_STEP

    def attn_bwd_body(k_ref, v_ref, q_ref, do_ref, lse_ref, dl_ref, cos_ref, sin_ref, p_ref, dkv_ref, dkr_ref, dq_ref, dq_acc):
        j = pl.program_id(1)

        @pl.when(j == 0)
        def _():
            dq_acc[...] = jnp.zeros(dq_acc.shape, F32)

        def block(i, carry, masked):
            rows = pl.ds(pl.multiple_of(i * t, t), t)
            out = []
            for hh, (dk, dv) in enumerate(carry):
                kb, vb = k_ref[hh], v_ref[hh]
                qb, dob_ = q_ref[hh, rows, :], do_ref[rows, hh * V_DIM:(hh + 1) * V_DIM]
                st = _dot(kb, qb, NT)
                if masked:
                    st = jnp.where(_causal_keep(t, True), st, NEG_BIG)
                pt = jnp.exp2(st - lse_ref[hh, i])
                dv = dv + _dot(pt.astype(BF), dob_, NN)
                dst = (pt * (_dot(vb, dob_, NT) - dl_ref[hh, i])).astype(BF)
                dk = dk + _dot(dst, qb, NN)
                dq_acc[hh, rows, :] += _dot(dst, kb, TN)
                out.append((dk, dv))
            return tuple(out)

        init = ((jnp.zeros((t, QK_DIM), F32), jnp.zeros((t, V_DIM), F32)),) * hp
        carry = block(j, init, True)
        carry = lax.fori_loop(j + 1, nq, lambda i, c: block(i, c, False), carry)
        for hh, (dk, dv) in enumerate(carry):
            dk = dk * LN_2
            base = hh * 2 * V_DIM
            dkv_ref[:, base:base + QK_NOPE] = dk[:, 0:QK_NOPE].astype(BF)
            dkv_ref[:, base + QK_NOPE:base + 2 * V_DIM] = dv.astype(BF)
            dkr_ref[hh] = dk[:, QK_NOPE:]

        @pl.when(j == nq - 1)
        def _():
            for hh in range(hp):
                for c in range(nq):
                    rows = slice(c * t, (c + 1) * t)
                    dq = dq_acc[hh, rows, :] * ATTN_SCALE
                    dq_ref[hh, rows, 0:QK_NOPE] = dq[:, 0:QK_NOPE].astype(BF)
                    dq_ref[hh, rows, QK_NOPE:] = _rope_bwd(dq[:, QK_NOPE:], cos_ref[rows, :], sin_ref[rows, :], p_ref[...]).astype(BF)

    row_stats = pl.BlockSpec((hp, nq, 1, t), lambda h, j: (h, 0, 0, 0))
    seq_rope = pl.BlockSpec((s, QK_ROPE), lambda h, j: (0, 0))
    head_seq = pl.BlockSpec((hp, s, QK_DIM), lambda h, j: (h, 0, 0))
    dkv, dkr_h, dq = pl.pallas_call(
        attn_bwd_body, name="mlab_attn", grid=(N_HEADS // hp, nq),
        in_specs=[pl.BlockSpec((hp, t, QK_DIM), lambda h, j: (h, j, 0)), pl.BlockSpec((hp, t, V_DIM), lambda h, j: (h, j, 0)),
                  head_seq, pl.BlockSpec((s, hp * V_DIM), lambda h, j: (0, h)), row_stats, row_stats, seq_rope, seq_rope,
                  pl.BlockSpec((QK_ROPE, QK_ROPE), lambda h, j: (0, 0))],
        out_specs=[pl.BlockSpec((t, hp * 2 * V_DIM), lambda h, j: (j, h)), pl.BlockSpec((hp, t, QK_ROPE), lambda h, j: (h, j, 0)), head_seq],
        out_shape=[_sds((s, N_HEADS * 2 * V_DIM), BF), _sds((N_HEADS, s, QK_ROPE), F32), _sds((N_HEADS, s, QK_DIM), BF)],
        scratch_shapes=[pltpu.VMEM((hp, s, QK_DIM), F32)],
        compiler_params=_params())(k, v, q, dob, lse, delta, cosf, sinf, perm)

    def dkr_body(d_ref, cos_ref, sin_ref, p_ref, o_ref):
        tot = d_ref[0]
        for hh in range(1, N_HEADS):
            tot = tot + d_ref[hh]
        o_ref[...] = _rope_bwd(tot, cos_ref[...], sin_ref[...], p_ref[...]).astype(BF)

    r64 = pl.BlockSpec((tm, QK_ROPE), lambda i: (i, 0))
    dkr = pl.pallas_call(
        dkr_body, name="mlab_dkr", grid=(nt,),
        in_specs=[pl.BlockSpec((N_HEADS, tm, QK_ROPE), lambda i: (0, i, 0)), r64, r64, pl.BlockSpec((QK_ROPE, QK_ROPE), lambda i: (0, 0))],
        out_specs=r64, out_shape=_sds((s, QK_ROPE), BF), compiler_params=_params())(dkr_h, cosf, sinf, perm)

    def lat_epi(acc, extra, outs, pids):
        dx, dg = _rms_bwd(acc, extra[0][...], extra[1][...], None)
        outs[0][...] = dx.astype(BF)
        _accumulate(outs[1], dg, pids[0])

    tp = IN_PROJ_ROWS if s % IN_PROJ_ROWS == 0 else tm

    def lat_bwd(name, a, a_spec, b, b_spec, n_k, lat, g, rank):
        row = lambda i, k: (i, 0)
        one = lambda i, k: (0, 0)
        return _mm(name, (s // tp, n_k), [a, b, lat, g.reshape(1, rank)],
                   [a_spec, b_spec, pl.BlockSpec((tp, rank), row), pl.BlockSpec((1, rank), one)],
                   [_sds((s, rank), BF), _sds((1, rank), F32)], [pl.BlockSpec((tp, rank), row), pl.BlockSpec((1, rank), one)],
                   NT, lat_epi, red=1, acc_shape=(tp, rank))

    d_ql, g_qnorm = lat_bwd("mlab_qup", dq, pl.BlockSpec((None, tp, QK_DIM), lambda i, h: (h, i, 0)),
                            w["w_qh"], pl.BlockSpec((None, Q_RANK, QK_DIM), lambda i, h: (h, 0, 0)), N_HEADS,
                            sv["q_lat"], w["q_norm"], Q_RANK)
    d_kvl, g_kvnorm = lat_bwd("mlab_kvup", dkv, pl.BlockSpec((tp, 512), lambda i, kk: (i, kk)),
                              w["g512"], pl.BlockSpec((None, KV_RANK, 512), lambda i, kk: (kk, 0, 0)), N_DEV,
                              sv["kv_lat"], w["kv_norm"], KV_RANK)

    def in_bwd(dql_ref, dkvl_ref, dkr_ref, dz_ref, wq_ref, wkv_ref, wkr_ref, wz_ref, x_ref, g_ref, dy_ref, dx_ref, dxb_ref, dg_ref):
        acc = (_dot(dql_ref[...], wq_ref[...], NT) + _dot(dkvl_ref[...], wkv_ref[...], NT)
               + _dot(dkr_ref[...], wkr_ref[...], NT) + _dot(dz_ref[...], wz_ref[...], NT))
        dx, dg = _rms_bwd(acc, x_ref[...], g_ref[...], dy_ref[...])
        dx_ref[...] = dx
        dxb_ref[...] = dx.astype(BF)
        _accumulate(dg_ref, dg, pl.program_id(0))

    def full(a):
        return pl.BlockSpec(a.shape, lambda i: (0,) * a.ndim)

    def rows(c):
        return pl.BlockSpec((tm, c), lambda i: (i, 0))

    gm = w["norm"].reshape(1, D_MODEL)
    dx, dxb, g_norm = pl.pallas_call(
        in_bwd, name="mlab_in", grid=(nt,),
        in_specs=[rows(Q_RANK), rows(KV_RANK), rows(QK_ROPE), rows(D_INNER), full(w["w_q"]), full(w["w_kv"]), full(w["w_kr"]),
                  full(w["w_z"]), rows(D_MODEL), full(gm), rows(D_MODEL)],
        out_specs=[rows(D_MODEL), rows(D_MODEL), full(gm)],
        out_shape=[_sds((s, D_MODEL), F32), _sds((s, D_MODEL), BF), _sds((1, D_MODEL), F32)],
        compiler_params=_params())(d_ql, d_kvl, dkr, dz, w["w_q"], w["w_kv"], w["w_kr"], w["w_z"], sv["x"], gm, dy)

    xn = sv["xn"]
    one = lambda j: (0, 0)
    g_q = _tn("mlab_gq", xn, d_ql, (D_MODEL, Q_RANK), (D_MODEL, Q_RANK), one, D_MODEL, Q_RANK, (1,), one, one)
    g_kv = _tn("mlab_gkv", xn, d_kvl, (D_MODEL, KV_RANK), (D_MODEL, KV_RANK), one, D_MODEL, KV_RANK, (1,), one, one)
    g_kr = _tn("mlab_gkr", xn, dkr, (D_MODEL, QK_ROPE), (D_MODEL, QK_ROPE), one, D_MODEL, QK_ROPE, (1,), one, one)
    g_z = _tn("mlab_gz", xn, dz, (D_MODEL, D_INNER), (D_MODEL, 512), lambda j: (0, j), D_MODEL, 512, (4,), one, lambda j: (0, j))
    g_in = jnp.concatenate([g_q, g_kv, g_kr, g_z], axis=1)
    g_qh = _tn("mlab_gqup", sv["qn"], dq, (N_HEADS, Q_RANK, QK_DIM), (None, Q_RANK, QK_DIM), lambda h: (h, 0, 0),
               Q_RANK, QK_DIM, (N_HEADS,), lambda h: (0, 0), lambda h: (h, 0, 0))
    g_kvup = _tn("mlab_gkvup", sv["kvn"], dkv, (N_DEV, KV_RANK, 512), (None, KV_RANK, 512), lambda j: (j, 0, 0),
                 KV_RANK, 512, (N_DEV,), lambda j: (0, 0), lambda j: (0, j))
    g_out = _w_out_grad("mlab_gout", sv["gated"], dyb)
    s384 = g_qh.reshape(N_DEV, 2, Q_RANK, QK_DIM).transpose(0, 2, 1, 3).reshape(N_DEV, Q_RANK, 2 * QK_DIM)
    s344 = g_in.reshape(D_MODEL, N_DEV, 344).transpose(1, 0, 2)
    return dx, dxb, dict(s344=s344, s384=s384, s512=g_kvup, s1024=g_out.reshape(N_DEV, 256, D_MODEL),
                         norm=g_norm[0], q_norm=g_qnorm[0], kv_norm=g_kvnorm[0])


def _loss_head(x, g, target, tm):
    s, d = x.shape

    def body(x_ref, g_ref, t_ref, dx_ref, dxb_ref, dg_ref, loss_ref):
        i = pl.program_id(0)
        xv, gv = x_ref[...], g_ref[...]
        r = lax.rsqrt(jnp.mean(xv * xv, axis=-1, keepdims=True) + NORM_EPS)
        err = (xv * r) * gv - t_ref[...]
        part = 0.5 * jnp.sum(jnp.mean(err * err, axis=-1, keepdims=True), axis=0, keepdims=True)
        dx, dg = _rms_bwd(err * (1.0 / d), xv, gv, None)
        dx_ref[...] = dx
        dxb_ref[...] = dx.astype(BF)
        _accumulate(dg_ref, dg, i)
        _accumulate(loss_ref, jnp.broadcast_to(part, loss_ref.shape), i)

    row = pl.BlockSpec((tm, d), lambda i: (i, 0))
    one = pl.BlockSpec((1, d), lambda i: (0, 0))
    return pl.pallas_call(
        body, name="loss_head", grid=(s // tm,), in_specs=[row, one, row],
        out_specs=[row, row, one, pl.BlockSpec((8, 128), lambda i: (0, 0))],
        out_shape=[_sds((s, d), F32), _sds((s, d), BF), _sds((1, d), F32), _sds((8, 128), F32)],
        compiler_params=_params())(x, g.reshape(1, d), target)


def _rope_tables(pos):
    inv_freq = ROPE_BASE ** (-jnp.arange(0, QK_ROPE, 2, dtype=F32) / QK_ROPE)
    ang = pos.astype(F32)[:, None] * inv_freq
    cos, sin = jnp.cos(ang), jnp.sin(ang)
    idx = jnp.arange(QK_ROPE)
    perm = (idx[:, None] == (idx[None, :] + QK_ROPE // 2) % QK_ROPE).astype(F32)
    return jnp.concatenate([cos, cos], axis=1), jnp.concatenate([-sin, sin], axis=1), perm


def _local_step(x, pos, target, final_norm, get_w, put_g):
    s = x.shape[0]
    tm = min(512, s)
    tl = IN_PROJ_ROWS if s % IN_PROJ_ROWS == 0 else tm
    rope = _rope_tables(pos)
    w0 = get_w(0, [])
    x1, sv0 = _pool_fwd(x, 0, w0, tl)
    w1 = get_w(1, [x1])
    x2, sv1 = _conv_fwd(x1, w1, tl)
    w2 = get_w(2, [x2])
    x3, sv2 = _mla_fwd(x2, w2, rope, tm)
    w3 = get_w(3, [x3])
    x4, sv3 = _pool_fwd(x3, 1, w3, tl)
    d4, d4b, g_final, loss = _loss_head(x4, final_norm, target, tm)
    d3, d3b, gp1 = _pool_bwd(d4, d4b, 1, sv3["w"], sv3, tl, [])
    dep = put_g(3, gp1)
    d2, d2b, gm = _mla_bwd(d3, d3b, w2, sv2, rope, tm, dep)
    dep = put_g(2, gm)
    sent = {}

    def send_conv(part):
        sent["dep"] = put_g(1, part)
        return sent["dep"]

    d1, d1b, gc = _conv_bwd(d2, d2b, sv1["w"], sv1, tl, dep, send_conv)
    d0, _, gp0 = _pool_bwd(d1, d1b, 0, sv0["w"], sv0, tl, sent["dep"], early=lambda part: put_g(0, part))
    put_g(4, {0: dict(gp0, final_norm=g_final[0]), 1: gc, 2: gm, 3: gp1})
    return loss[0, 0], d0


def _pack_groups(p):
    bf = lambda a: a.astype(BF)
    grp = lambda l: bf(p["pool_w_grp"][l].reshape(4 * 64, POOL_GROUP))
    return [[bf(p["pool_w_in"][0]), _pack_small(p, SMALL_ROWS_AG)],
            [grp(0), bf(p["pool_w_out"][0])],
            [bf(p["conv_w_in"][0])],
            [bf(p["conv_w_out"][0])],
            [bf(p[k][0]) for k in ("mla_w_in", "mla_w_q_up", "mla_w_kv_up", "mla_w_out")],
            [bf(p["pool_w_in"][1]), grp(1), bf(p["pool_w_out"][1])]]


_SMALL_SHARDED = ("pool_norm", "pool_scale", "mla_norm", "mla_q_norm", "mla_kv_norm", "conv_w")
_SMALL_REPLICATED = ("conv_norm", "final_norm")


def _pack_small(p, rows, with_replicated=False):
    parts = [p[k].reshape(-1) for k in _SMALL_SHARDED]
    if with_replicated:
        parts += [p[k].reshape(-1) for k in _SMALL_REPLICATED]
    flat = jnp.concatenate(parts)
    return jnp.pad(flat, (0, rows * 128 - flat.shape[0])).reshape(rows, 128)


_SMALL_SHARD_SHAPES = dict(pool_norm=(2, 128), pool_scale=(2, 256), mla_norm=(1, 128), mla_q_norm=(1, 48),
                           mla_kv_norm=(1, 32), conv_w=(1, 3, 256), conv_norm=(1, 1024), final_norm=(1024,))


def _unpack_small(buf, with_replicated=False):
    flat = buf.reshape(-1)
    out, off = {}, 0
    for k in _SMALL_SHARDED + (_SMALL_REPLICATED if with_replicated else ()):
        shp = _SMALL_SHARD_SHAPES[k]
        n = 1
        for d in shp:
            n *= d
        out[k] = flat[off:off + n].reshape(shp)
        off += n
    return out


def _small_views(gsmall):
    flat = gsmall.reshape(N_DEV, -1)

    def cols(off, rows, width):
        return flat[:, off:off + rows * width].reshape(N_DEV, rows, width).transpose(1, 0, 2).reshape(rows, N_DEV * width)

    return dict(pool_norm=cols(0, 2, 128), pool_scale=cols(256, 2, 256), mla_norm=cols(768, 1, 128)[0],
                q_norm=cols(896, 1, 48)[0], kv_norm=cols(944, 1, 32)[0], conv_w=cols(976, 3, 256))


def _pair_columns(g384):
    heads = g384.reshape(N_DEV, Q_RANK, 2, QK_DIM)
    nope = heads[..., :QK_NOPE].reshape(N_DEV, Q_RANK, 2 * QK_NOPE)
    rope = heads[..., QK_NOPE:].reshape(N_DEV, Q_RANK, 2 * QK_ROPE)
    return jnp.concatenate([nope, rope], axis=-1)


def _layer_weights(layer, bufs, small, conv_norm):
    if layer in (0, 3):
        l = 0 if layer == 0 else 1
        return dict(g_in=bufs[0], norm=small["pool_norm"][l], scale=small["pool_scale"][l])
    if layer == 1:
        return dict(g_in=bufs[0], norm=conv_norm.reshape(D_MODEL), conv_w=small["conv_w"])
    g344, g384, g512, g1024 = bufs
    w_in = g344.transpose(1, 0, 2).reshape(D_MODEL, N_DEV * 344)
    return dict(
        g512=g512, g1024=g1024,
        w_q=w_in[:, :Q_RANK], w_kv=w_in[:, Q_RANK:Q_RANK + KV_RANK],
        w_kr=w_in[:, Q_RANK + KV_RANK:Q_RANK + KV_RANK + QK_ROPE], w_z=w_in[:, Q_RANK + KV_RANK + QK_ROPE:],
        w_qh=g384.reshape(N_DEV, Q_RANK, 2, QK_DIM).transpose(0, 2, 1, 3).reshape(N_HEADS, Q_RANK, QK_DIM),
        w_qpair=_pair_columns(g384),
        norm=small["mla_norm"], q_norm=small["q_norm"], kv_norm=small["kv_norm"])


_GRAD_KEYS = {0: ("g_in", "g_grp", "g_out"), 3: ("g_in", "g_grp", "g_out"), 1: ("g_in", "g_out"), 2: ("s344", "s384", "s512", "s1024")}
_GRAD_PARAM = {0: dict(g_in="pool_w_in", g_grp="pool_w_grp", g_out="pool_w_out"), 1: dict(g_in="conv_w_in", g_out="conv_w_out"),
               2: dict(s344="mla_w_in", s384="mla_w_q_up", s512="mla_w_kv_up", s1024="mla_w_out")}
_GRAD_PARAM[3] = _GRAD_PARAM[0]


def _grad_group(layer, g):
    keys = tuple(k for k in _GRAD_KEYS[layer] if k in g)
    return keys, [g[k] for k in keys]


def _pack_small_grads(g):
    def split(a, rows, width):
        return a.reshape(rows, N_DEV, width).transpose(1, 0, 2).reshape(N_DEV, rows * width)

    rep = lambda a: jnp.broadcast_to(a.reshape(1, -1), (N_DEV, a.size))
    flat = jnp.concatenate([
        split(jnp.stack([g[0]["norm"], g[3]["norm"]]), 2, 128), split(jnp.stack([g[0]["scale"], g[3]["scale"]]), 2, 256),
        split(g[2]["norm"], 1, 128), split(g[2]["q_norm"], 1, 48), split(g[2]["kv_norm"], 1, 32), split(g[1]["conv_w"], 3, 256),
        rep(g[1]["norm"]), rep(g[0]["final_norm"])], axis=1)
    return jnp.pad(flat, ((0, 0), (0, SMALL_ROWS_RS * 128 - flat.shape[1]))).reshape(N_DEV, SMALL_ROWS_RS, 128)


def _peers(x, y, c):
    for k in range(1, N_DEV):
        px = 1 - x if k & 4 else x
        py = 1 - y if k & 2 else y
        pc = 1 - c if k & 1 else c
        yield k - 1, (px, py, pc), 4 * px + 2 * py + pc


_SIBLING = (0,)
_ICI_DIRECT = (1, 3, 5)


def _remote_copies(srcs, lands, send_sems, recv_sems, gather, ks=None):
    x, y, c = lax.axis_index("x"), lax.axis_index("y"), lax.axis_index("c")
    me = 4 * x + 2 * y + c
    copies = []
    for k, peer, pidx in _peers(x, y, c):
        if ks is not None and k not in ks:
            continue
        for a, (src, land) in enumerate(zip(srcs, lands)):
            copies.append(pltpu.make_async_remote_copy(
                src_ref=src if gather else src.at[pidx], dst_ref=land.at[me],
                send_sem=send_sems.at[a * (N_DEV - 1) + k], recv_sem=recv_sems.at[a * (N_DEV - 1) + k],
                device_id=peer, device_id_type=pl.DeviceIdType.MESH))
    return copies


def _relay_copies(lands, send_sems, recv_sems):
    x, y, c = lax.axis_index("x"), lax.axis_index("y"), lax.axis_index("c")
    copies = []
    for j, k in enumerate(_ICI_DIRECT):
        px = 1 - x if (k + 1) & 4 else x
        py = 1 - y if (k + 1) & 2 else y
        slot = 4 * px + 2 * py + c
        for a, land in enumerate(lands):
            copies.append(pltpu.make_async_remote_copy(
                src_ref=land.at[slot], dst_ref=land.at[slot],
                send_sem=send_sems.at[a * len(_ICI_DIRECT) + j], recv_sem=recv_sems.at[a * len(_ICI_DIRECT) + j],
                device_id=(x, y, 1 - c), device_id_type=pl.DeviceIdType.MESH))
    return copies


_HBM = pl.BlockSpec(memory_space=pltpu.HBM)
_SEM = pl.BlockSpec(memory_space=pltpu.SEMAPHORE)
_EFFECT = pltpu.SideEffectType.DATAFLOW_SIDE_EFFECTING


def _own_slabs(name, arrays, gather, dep):
    n, nd = len(arrays), len(dep)
    me = (4 * lax.axis_index("x") + 2 * lax.axis_index("y") + lax.axis_index("c")).astype(jnp.int32).reshape(1)

    def body(me_ref, *refs):
        for a in range(n):
            refs[n + nd + a][...] = refs[a][...]

    def slab(shape):
        return pl.BlockSpec((None,) + tuple(shape), lambda i, me_ref: (me_ref[0],) + (0,) * len(shape))

    def whole(shape):
        return pl.BlockSpec(tuple(shape), lambda i, me_ref: (0,) * len(shape))

    outs = [_sds(((N_DEV,) + a.shape) if gather else a.shape, a.dtype) for a in arrays]
    grid_spec = pltpu.PrefetchScalarGridSpec(
        num_scalar_prefetch=1, grid=(1,),
        in_specs=[whole(a.shape) if gather else slab(a.shape[1:]) for a in arrays] + [_ANY] * nd,
        out_specs=[slab(o.shape[1:]) for o in outs])
    return pl.pallas_call(body, name=name, grid_spec=grid_spec, out_shape=outs, compiler_params=_params())(me, *arrays, *dep)


def _exchange_start(name, arrays, lands, gather, ks=None):
    n = len(arrays)

    def body(*refs):
        srcs, lnds, send_sems, recv_sems, token = refs[:n], refs[n:2 * n], refs[2 * n], refs[2 * n + 1], refs[-1]
        for cp in _remote_copies(srcs, lnds, send_sems, recv_sems, gather, ks):
            cp.start()
        token[...] = jnp.zeros(token.shape, F32)

    sems = pltpu.SemaphoreType.DMA((n * (N_DEV - 1),))
    thru = [pltpu.HBM(a.shape, a.dtype) for a in list(arrays) + list(lands)]
    res = pl.pallas_call(
        body, name=name, in_specs=[_HBM] * (2 * n),
        out_specs=[_SEM, _SEM] + [_HBM] * (2 * n) + [pl.BlockSpec(memory_space=pltpu.VMEM)],
        out_shape=[sems, sems] + thru + [_sds((8, 128), F32)],
        input_output_aliases={i: 2 + i for i in range(2 * n)},
        compiler_params=pltpu.CompilerParams(has_side_effects=_EFFECT),
    )(*[pltpu.with_memory_space_constraint(a, pltpu.HBM) for a in list(arrays) + list(lands)])
    return res[0], res[1], list(res[2:2 + n]), list(res[2 + n:2 + 2 * n]), res[-1]


def _exchange_wait(name, send_sems, recv_sems, arrays, lands, after, gather):
    n = len(arrays)
    n_after = len(after)

    def body(*refs):
        srcs, lnds = refs[:n], refs[n:2 * n]
        copies = _remote_copies(srcs, lnds, refs[2 * n], refs[2 * n + 1], gather)
        for cp in copies:
            cp.wait_send()
        for cp in copies:
            cp.wait_recv()

    thru = [pltpu.HBM(a.shape, a.dtype) for a in list(arrays) + list(lands)]
    res = pl.pallas_call(
        body, name=name, in_specs=[_HBM] * (2 * n) + [_SEM, _SEM] + [pl.BlockSpec(memory_space=pl.ANY)] * n_after,
        out_specs=[_HBM] * (2 * n), out_shape=thru, input_output_aliases={i: i for i in range(2 * n)},
        compiler_params=pltpu.CompilerParams(has_side_effects=_EFFECT),
    )(*arrays, *lands, send_sems, recv_sems, *after)
    return list(res[n:])


def _gather_relay(name, send_sems, recv_sems, arrays, lands, after):
    n, n_after = len(arrays), len(after)

    def body(*refs):
        srcs, lnds, first_send, first_recv = refs[:n], refs[n:2 * n], refs[2 * n], refs[2 * n + 1]
        send2, recv2, token = refs[2 * n + 2 + n_after], refs[2 * n + 3 + n_after], refs[-1]
        for cp in _remote_copies(srcs, lnds, first_send, first_recv, True, _ICI_DIRECT):
            cp.wait_recv()
        for cp in _relay_copies(lnds, send2, recv2):
            cp.start()
        token[...] = jnp.zeros(token.shape, F32)

    sems = pltpu.SemaphoreType.DMA((n * len(_ICI_DIRECT),))
    thru = [pltpu.HBM(a.shape, a.dtype) for a in list(arrays) + list(lands)]
    res = pl.pallas_call(
        body, name=name, in_specs=[_HBM] * (2 * n) + [_SEM, _SEM] + [_ANY] * n_after,
        out_specs=[_SEM, _SEM] + [_HBM] * (2 * n) + [pl.BlockSpec(memory_space=pltpu.VMEM)],
        out_shape=[sems, sems] + thru + [_sds((8, 128), F32)],
        input_output_aliases={i: 2 + i for i in range(2 * n)},
        compiler_params=pltpu.CompilerParams(has_side_effects=_EFFECT),
    )(*arrays, *lands, send_sems, recv_sems, *after)
    return res[0], res[1], list(res[2:2 + n]), list(res[2 + n:2 + 2 * n]), res[-1]


def _gather_wait2(name, send_sems, recv_sems, send2, recv2, arrays, lands, after):
    n, n_after = len(arrays), len(after)

    def body(*refs):
        srcs, lnds = refs[:n], refs[n:2 * n]
        s1, r1, s2, r2 = refs[2 * n:2 * n + 4]
        for cp in _remote_copies(srcs, lnds, s1, r1, True, _SIBLING + _ICI_DIRECT):
            cp.wait_send()
        for cp in _remote_copies(srcs, lnds, s1, r1, True, _SIBLING):
            cp.wait_recv()
        relay = _relay_copies(lnds, s2, r2)
        for cp in relay:
            cp.wait_send()
        for cp in relay:
            cp.wait_recv()

    thru = [pltpu.HBM(a.shape, a.dtype) for a in list(arrays) + list(lands)]
    res = pl.pallas_call(
        body, name=name, in_specs=[_HBM] * (2 * n) + [_SEM] * 4 + [_ANY] * n_after,
        out_specs=[_HBM] * (2 * n), out_shape=thru, input_output_aliases={i: i for i in range(2 * n)},
        compiler_params=pltpu.CompilerParams(has_side_effects=_EFFECT),
    )(*arrays, *lands, send_sems, recv_sems, send2, recv2, *after)
    return list(res[n:])


def _adamw_math(g, w, m, v):
    m = ADAM_B1 * m + (1.0 - ADAM_B1) * g
    v = ADAM_B2 * v + (1.0 - ADAM_B2) * (g * g)
    m_hat = m / (1.0 - ADAM_B1 ** ADAM_STEP)
    v_hat = v / (1.0 - ADAM_B2 ** ADAM_STEP)
    delta = -ADAM_LR * (m_hat / (jnp.sqrt(v_hat) + ADAM_EPS) + ADAM_WD * w)
    return delta, m, v


def _sum_adamw(name, recv, row_off, w, m, v, tr, layer=0):
    width = recv.shape[-1]
    w2, m2, v2 = (a.reshape(a.shape[0], -1, width) for a in (w, m, v))
    rows = w2.shape[1]
    base = row_off // tr

    def body(r_ref, w_ref, m_ref, v_ref, g_ref, d_ref, mo_ref, vo_ref):
        g = r_ref[0].astype(F32)
        for src in range(1, N_DEV):
            g = g + r_ref[src].astype(F32)
        delta, mn, vn = _adamw_math(g, w_ref[...], m_ref[...], v_ref[...])
        g_ref[...] = g
        d_ref[...] = delta
        mo_ref[...] = mn
        vo_ref[...] = vn

    blk = pl.BlockSpec((tr, width), lambda i: (i, 0))
    wblk = pl.BlockSpec((None, tr, width), lambda i: (layer, i, 0))
    return pl.pallas_call(
        body, name=name, grid=(rows // tr,),
        in_specs=[pl.BlockSpec((N_DEV, tr, width), lambda i: (0, base + i, 0)), wblk, wblk, wblk],
        out_specs=[blk] * 4, out_shape=[_sds((rows, width), F32)] * 4, compiler_params=_params())(recv, w2, m2, v2)


_WEIGHTS = ("pool_norm", "pool_w_in", "pool_w_grp", "pool_scale", "pool_w_out", "conv_norm", "conv_w_in", "conv_w", "conv_w_out",
            "mla_norm", "mla_w_in", "mla_q_norm", "mla_w_q_up", "mla_kv_norm", "mla_w_kv_up", "mla_w_out", "final_norm")


def _step(x, positions, loss_target, p, m, v):
    gathers, tokens, dep = [], [], []
    for group, arrays in enumerate(_pack_groups(p)):
        lands = _own_slabs(f"gather{group}_own", arrays, True, dep)
        ks = _SIBLING + _ICI_DIRECT if group < TWO_LEVEL_GROUPS else None
        ssem, rsem, arrays, lands, token = _exchange_start(f"gather{group}_start", arrays, lands, True, ks)
        gathers.append((ssem, rsem, arrays, lands))
        tokens.append(token)
        dep = [token]
    state = {}

    def wait_group(group, after):
        if group >= TWO_LEVEL_GROUPS:
            return _exchange_wait(f"gather{group}_wait", *gathers[group], after, True)
        ssem, rsem, arrays, lands = gathers[group]
        send2, recv2, arrays, lands, token = _gather_relay(f"gather{group}_relay", ssem, rsem, arrays, lands, after)
        return _gather_wait2(f"gather{group}_wait", ssem, rsem, send2, recv2, arrays, lands, [token])

    def get_w(layer, after):
        if layer == 0:
            bufs = wait_group(0, list(tokens))
            state["small"] = _small_views(bufs[1])
            rest = lambda later: dict(zip(("g_grp", "g_out"), wait_group(1, later)))
        elif layer == 1:
            bufs = wait_group(2, after)
            rest = lambda later: dict(g_out=wait_group(3, later)[0])
        elif layer == 2:
            bufs = wait_group(4, after)
        else:
            bufs = wait_group(5, after)
            rest = lambda later: dict(g_grp=bufs[1], g_out=bufs[2])
        w = _layer_weights(layer, bufs, state["small"], p["conv_norm"])
        if layer != 2:
            w["rest"] = rest
        return w

    scatters = []

    def put_g(layer, g):
        if layer == 4:
            keys, arrays = ("small",), [_pack_small_grads(g)]
        else:
            keys, arrays = _grad_group(layer, g)
        n = len(scatters)
        lands = _own_slabs(f"scatter{n}_own", arrays, False, [])
        ssem, rsem, arrays, lands, token = _exchange_start(f"scatter{n}_start", arrays, lands, False)
        scatters.append((layer, keys, (ssem, rsem, arrays, lands)))
        tokens.append(token)
        return [token]

    loss, grad_x = _local_step(x[0], positions[0], loss_target[0], p["final_norm"], get_w, put_g)

    res, after = {}, [tokens[-1]]
    for n, (layer, keys, handles) in enumerate(scatters):
        recv = _exchange_wait(f"scatter{n}_wait", *handles, after, False)
        if layer == 4:
            break
        l = 1 if layer == 3 else 0
        for key, buf in zip(keys, recv):
            name = _GRAD_PARAM[layer][key]
            tr = min(256, buf.shape[1]) if name != "mla_w_q_up" else buf.shape[1]
            res[name, l] = _sum_adamw(f"adam_{name}{l}", buf, 0, p[name], m[name], v[name], tr, l)
        after = [res[name, l][1]]
    small = _sum_adamw("adam_small", recv[0], 0, _pack_small(p, SMALL_ROWS_RS, True)[None], _pack_small(m, SMALL_ROWS_RS, True)[None],
                       _pack_small(v, SMALL_ROWS_RS, True)[None], SMALL_ROWS_RS)
    small = [_unpack_small(a, True) for a in small]
    final = {k: tuple(part[k] for part in small) for k in _SMALL_SHARDED + _SMALL_REPLICATED}
    for k in _WEIGHTS:
        if k not in final:
            layers = [res[k, l] for l in range(p[k].shape[0])]
            final[k] = tuple(jnp.stack([lay[part] for lay in layers]).reshape(p[k].shape) for part in range(4))
    res = final

    loss = lax.psum(loss, ("x", "y", "c"))
    out = [loss, grad_x[None]]
    for part in range(4):
        out += [res[k][part] for k in _WEIGHTS]
    return tuple(out)


def kernel(x, positions, pool_norm, pool_w_in, pool_w_grp, pool_scale, pool_w_out, conv_norm, conv_w_in, conv_w, conv_w_out, mla_norm, mla_w_in, mla_q_norm, mla_w_q_up, mla_kv_norm, mla_w_kv_up, mla_w_out, final_norm, loss_target, m_pool_norm, m_pool_w_in, m_pool_w_grp, m_pool_scale, m_pool_w_out, m_conv_norm, m_conv_w_in, m_conv_w, m_conv_w_out, m_mla_norm, m_mla_w_in, m_mla_q_norm, m_mla_w_q_up, m_mla_kv_norm, m_mla_w_kv_up, m_mla_w_out, m_final_norm, v_pool_norm, v_pool_w_in, v_pool_w_grp, v_pool_scale, v_pool_w_out, v_conv_norm, v_conv_w_in, v_conv_w, v_conv_w_out, v_mla_norm, v_mla_w_in, v_mla_q_norm, v_mla_w_q_up, v_mla_kv_norm, v_mla_w_kv_up, v_mla_w_out, v_final_norm):
    p = dict(pool_norm=pool_norm, pool_w_in=pool_w_in, pool_w_grp=pool_w_grp, pool_scale=pool_scale, pool_w_out=pool_w_out,
             conv_norm=conv_norm, conv_w_in=conv_w_in, conv_w=conv_w, conv_w_out=conv_w_out, mla_norm=mla_norm, mla_w_in=mla_w_in,
             mla_q_norm=mla_q_norm, mla_w_q_up=mla_w_q_up, mla_kv_norm=mla_kv_norm, mla_w_kv_up=mla_w_kv_up, mla_w_out=mla_w_out,
             final_norm=final_norm)
    m = dict(pool_norm=m_pool_norm, pool_w_in=m_pool_w_in, pool_w_grp=m_pool_w_grp, pool_scale=m_pool_scale, pool_w_out=m_pool_w_out,
             conv_norm=m_conv_norm, conv_w_in=m_conv_w_in, conv_w=m_conv_w, conv_w_out=m_conv_w_out, mla_norm=m_mla_norm,
             mla_w_in=m_mla_w_in, mla_q_norm=m_mla_q_norm, mla_w_q_up=m_mla_w_q_up, mla_kv_norm=m_mla_kv_norm,
             mla_w_kv_up=m_mla_w_kv_up, mla_w_out=m_mla_w_out, final_norm=m_final_norm)
    v = dict(pool_norm=v_pool_norm, pool_w_in=v_pool_w_in, pool_w_grp=v_pool_w_grp, pool_scale=v_pool_scale, pool_w_out=v_pool_w_out,
             conv_norm=v_conv_norm, conv_w_in=v_conv_w_in, conv_w=v_conv_w, conv_w_out=v_conv_w_out, mla_norm=v_mla_norm,
             mla_w_in=v_mla_w_in, mla_q_norm=v_mla_q_norm, mla_w_q_up=v_mla_w_q_up, mla_kv_norm=v_mla_kv_norm,
             mla_w_kv_up=v_mla_w_kv_up, mla_w_out=v_mla_w_out, final_norm=v_final_norm)
    return _step(x, positions, loss_target, p, m, v)
```

```python
import jax
import jax.numpy as jnp
from jax import lax
from jax.experimental import pallas as pl
from jax.experimental.pallas import tpu as pltpu

BF = jnp.bfloat16
F32 = jnp.float32

N_DEV = 8
D_MODEL = 1024
D_INNER = 2048
POOL_WINDOWS = (2, 4, 8, 16)
POOL_GROUP = 512
N_HEADS = 16
QK_NOPE = 128
QK_ROPE = 64
QK_DIM = QK_NOPE + QK_ROPE
V_DIM = 128
Q_RANK = 384
KV_RANK = 256
ATTN_SCALE = QK_DIM ** -0.5
LOG2_E = 1.4426950408889634
LN_2 = 0.6931471805599453
Q_PRESCALE = ATTN_SCALE * LOG2_E
ATTN_TILE = 512
ATTN_HEADS_PER_STEP = 2
ATTN_FWD_HEADS = 4
ROPE_BASE = 10000.0
NORM_EPS = 1e-6
NEG_BIG = -1e30

ADAM_LR = 0.001
ADAM_B1 = 0.9
ADAM_B2 = 0.999
ADAM_EPS = 1e-08
ADAM_WD = 0.01
ADAM_STEP = 10

VMEM_LIMIT_BYTES = 52 * 1024 * 1024
IN_PROJ_ROWS = 1024
POOL_HALO = 32
CONV_HALO = 16

NN = (((1,), (0,)), ((), ()))
NT = (((1,), (1,)), ((), ()))
TN = (((0,), (0,)), ((), ()))

TWO_LEVEL_GROUPS = 3
SMALL_ROWS_AG = 16
SMALL_ROWS_RS = 32


def _sds(shape, dtype):
    return jax.ShapeDtypeStruct(tuple(shape), dtype)


def _params():
    return pltpu.CompilerParams(vmem_limit_bytes=VMEM_LIMIT_BYTES)


_ANY = pl.BlockSpec(memory_space=pl.ANY)


def _dot(a, b, dims):
    return lax.dot_general(a, b, dims, preferred_element_type=F32)


def _sig(z):
    return 1.0 / (1.0 + jnp.exp(-z))


def _silu_and_grad(z):
    sig = _sig(z)
    return z * sig, sig * (1.0 + z * (1.0 - sig))


def _to_row(col):
    return jnp.broadcast_to(col, (col.shape[0], 128)).T[0:1, :]


def _rope_swap(x, p):
    pb = p.astype(BF)
    hi = x.astype(BF)
    r1 = x - hi.astype(F32)
    mid = r1.astype(BF)
    lo = (r1 - mid.astype(F32)).astype(BF)
    return (_dot(hi, pb, NN) + _dot(mid, pb, NN)) + _dot(lo, pb, NN)


def _rope_fwd(x, cosf, sinf, p):
    return x * cosf + _rope_swap(x, p) * sinf


def _rope_bwd(dy, cosf, sinf, p):
    return dy * cosf + _rope_swap(dy * sinf, p)


def _rms_bwd(dxn, x, g, res):
    r = lax.rsqrt(jnp.mean(x * x, axis=-1, keepdims=True) + NORM_EPS)
    v = dxn * g
    dx = r * v - x * ((r * r * r) * jnp.mean(v * x, axis=-1, keepdims=True))
    if res is not None:
        dx = dx + res
    dg = jnp.sum(dxn * (x * r), axis=0, keepdims=True)
    return dx, dg


def _accumulate(ref, val, step):
    @pl.when(step == 0)
    def _():
        ref[...] = val

    @pl.when(step > 0)
    def _():
        ref[...] += val


def _mm(name, grid, ins, in_specs, outs, out_specs, dims, epi, red=None, acc_shape=None):
    n_in, n_out = len(ins), len(outs)
    n_red = None if red is None else grid[red]

    def body(*refs):
        in_refs, out_refs = refs[:n_in], refs[n_in:n_in + n_out]
        pids = tuple(pl.program_id(ax) for ax in range(len(grid)))
        a, b = in_refs[0][...], in_refs[1][...]
        if a.ndim == 3:
            a = a.reshape(-1, a.shape[-1])
        if b.ndim == 3:
            b = b.reshape(-1, b.shape[-1])
        part = _dot(a.astype(BF), b.astype(BF), dims)
        if red is None:
            epi(part, in_refs[2:], out_refs, pids)
        else:
            acc = refs[n_in + n_out]
            k = pids[red]
            _accumulate(acc, part, k)

            @pl.when(k == n_red - 1)
            def _():
                epi(acc[...], in_refs[2:], out_refs, pids)

    scratch = [] if red is None else [pltpu.VMEM(acc_shape, F32)]
    return pl.pallas_call(body, name=name, grid=grid, in_specs=in_specs, out_specs=out_specs, out_shape=outs,
                          scratch_shapes=scratch, compiler_params=_params())(*ins)


def _rms_fwd(name, x, g, tm):
    s, d = x.shape

    def body(x_ref, g_ref, o_ref):
        xv = x_ref[...]
        r = lax.rsqrt(jnp.mean(xv * xv, axis=-1, keepdims=True) + NORM_EPS)
        o_ref[...] = ((xv * r) * g_ref[...]).astype(BF)

    return pl.pallas_call(body, name=name, grid=(s // tm,),
                          in_specs=[pl.BlockSpec((tm, d), lambda i: (i, 0)), pl.BlockSpec((1, d), lambda i: (0, 0))],
                          out_specs=pl.BlockSpec((tm, d), lambda i: (i, 0)), out_shape=_sds((s, d), BF),
                          compiler_params=_params())(x, g.reshape(1, d))


def _norm_in_proj(name, x, g, wbuf, w_index, n_j, tm):
    s = x.shape[0]
    ti = IN_PROJ_ROWS if s % IN_PROJ_ROWS == 0 else tm

    def body(x_ref, g_ref, w_ref, h_ref, xn_ref):
        @pl.when(pl.program_id(1) == 0)
        def _():
            xv = x_ref[...]
            r = lax.rsqrt(jnp.mean(xv * xv, axis=-1, keepdims=True) + NORM_EPS)
            xn_ref[...] = ((xv * r) * g_ref[...]).astype(BF)

        h_ref[...] = _dot(xn_ref[...], w_ref[...], NN).astype(BF)

    row = lambda i, j: (i, 0)
    return pl.pallas_call(
        body, name=name, grid=(s // ti, n_j),
        in_specs=[pl.BlockSpec((ti, D_MODEL), row), pl.BlockSpec((1, D_MODEL), lambda i, j: (0, 0)),
                  pl.BlockSpec((None, D_MODEL, 512), w_index)],
        out_specs=[pl.BlockSpec((ti, 512), lambda i, j: (i, j)), pl.BlockSpec((ti, D_MODEL), row)],
        out_shape=[_sds((s, n_j * 512), BF), _sds((s, D_MODEL), BF)], compiler_params=_params())(x, g.reshape(1, D_MODEL), wbuf)


def _tn(name, a, b, out_shape, out_block, out_index, a_cols, b_cols, grid, a_index, b_index, dep=()):
    s = a.shape[-2]
    a_block = (s, a_cols) if a.ndim == 2 else (None, s, a_cols)
    b_block = (s, b_cols) if b.ndim == 2 else (None, s, b_cols)

    def epi(part, extra, outs, pids):
        outs[0][...] = part.astype(BF).reshape(outs[0].shape)

    return _mm(name, grid, [a, b] + list(dep), [pl.BlockSpec(a_block, a_index), pl.BlockSpec(b_block, b_index)] + [_ANY] * len(dep),
               [_sds(out_shape, BF)], [pl.BlockSpec(out_block, out_index)], TN, epi)[0]


def _pool_window_fwd(name, h, tm):
    s = h.shape[0]
    hb = POOL_HALO

    def body(u_ref, halo_ref, o_ref, e_ref, a_ref, b_ref):
        i = pl.program_id(0)
        row = lax.broadcasted_iota(jnp.int32, (tm, 1), 0) + i * tm
        for g, w in enumerate(POOL_WINDOWS):
            cs = slice(g * POOL_GROUP, (g + 1) * POOL_GROUP)
            e_ref[0:hb, :] = jnp.where(i > 0, halo_ref[:, cs].astype(F32), 0.0)
            e_ref[hb:, :] = u_ref[:, cs].astype(F32)
            src, bufs = e_ref, (a_ref, b_ref)
            for lv in range(1, w.bit_length()):
                dst, st, sh = bufs[(lv - 1) % 2], 8 * lv, 2 ** (lv - 1)
                n = hb + tm - st
                dst[st:, :] = src[st:, :] + src[pl.ds(st - sh, n), :]
                src = dst
            cnt = jnp.minimum(row + 1, w).astype(F32)
            o_ref[:, cs] = (src[hb:, :] / cnt - u_ref[:, cs].astype(F32)).astype(BF)

    per = tm // hb
    return pl.pallas_call(
        body, name=name, grid=(s // tm,),
        in_specs=[pl.BlockSpec((tm, D_INNER), lambda i: (i, 0)),
                  pl.BlockSpec((hb, D_INNER), lambda i: (jnp.maximum(i * per - 1, 0), 0))],
        out_specs=pl.BlockSpec((tm, D_INNER), lambda i: (i, 0)), out_shape=_sds((s, D_INNER), BF),
        scratch_shapes=[pltpu.VMEM((hb + tm, POOL_GROUP), F32)] * 3, compiler_params=_params())(h, h)


def _pool_window_bwd(name, dp, tm, dh):
    s = dp.shape[0]
    nt = s // tm
    hb = POOL_HALO

    def body(d_ref, halo_ref, dh_in_ref, o_ref, e_ref, a_ref, b_ref):
        i = pl.program_id(0)
        row = lax.broadcasted_iota(jnp.int32, (tm, 1), 0) + i * tm
        hrow = lax.broadcasted_iota(jnp.int32, (hb, 1), 0) + (i + 1) * tm
        for g, w in enumerate(POOL_WINDOWS):
            cs = slice(g * POOL_GROUP, (g + 1) * POOL_GROUP)
            e_ref[0:tm, :] = d_ref[:, cs].astype(F32) / jnp.minimum(row + 1, w).astype(F32)
            e_ref[tm:, :] = jnp.where(i < nt - 1, halo_ref[:, cs].astype(F32) / jnp.minimum(hrow + 1, w).astype(F32), 0.0)
            src, bufs = e_ref, (a_ref, b_ref)
            for lv in range(1, w.bit_length()):
                dst, sh = bufs[(lv - 1) % 2], 2 ** (lv - 1)
                n = tm + hb - 8 * lv
                dst[0:n, :] = src[0:n, :] + src[pl.ds(sh, n), :]
                src = dst
            o_ref[:, cs] = (src[0:tm, :] - d_ref[:, cs].astype(F32)).astype(BF)

    per = tm // hb
    last = s // hb - 1
    return pl.pallas_call(
        body, name=name, grid=(nt,),
        in_specs=[pl.BlockSpec((tm, D_INNER), lambda i: (i, 0)),
                  pl.BlockSpec((hb, D_INNER), lambda i: (jnp.minimum((i + 1) * per, last), 0)), _ANY],
        out_specs=pl.BlockSpec((tm, D_INNER), lambda i: (i, 0)), out_shape=_sds(dh.shape, BF),
        input_output_aliases={2: 0},
        scratch_shapes=[pltpu.VMEM((hb + tm, POOL_GROUP), F32)] * 3, compiler_params=_params())(dp, dp, dh)


def _grp_block():
    return pl.BlockSpec((N_DEV, 64, POOL_GROUP), lambda i, g: (0, g, 0))


def _pool_fwd(x, l, w, tm):
    s = x.shape[0]
    nt = s // tm
    n = f"pool{l}"
    h, xn = _norm_in_proj(n + "_in", x, w["norm"], w["g_in"], lambda i, j: (j, 0, 0), 8, tm)
    pooled = _pool_window_fwd(n + "_win", h, tm)
    w = dict(w, **w["rest"]([h]))

    def gate(part, extra, outs, pids):
        z = extra[0][...].astype(F32)
        outs[0][...] = ((part * extra[1][...]) * (z * _sig(z))).astype(BF)

    tg = IN_PROJ_ROWS if s % IN_PROJ_ROWS == 0 else tm
    (gated,) = _mm(n + "_grp", (s // tg, 4), [pooled, w["g_grp"], h, w["scale"].reshape(1, D_INNER)],
                   [pl.BlockSpec((tg, 512), lambda i, g: (i, g)), _grp_block(),
                    pl.BlockSpec((tg, 512), lambda i, g: (i, 4 + g)), pl.BlockSpec((1, 512), lambda i, g: (0, g))],
                   [_sds((s, D_INNER), BF)], [pl.BlockSpec((tg, 512), lambda i, g: (i, g))], NN, gate)
    y = _out_proj(n + "_out", gated, w["g_out"], 0, x, tm)
    return y, dict(x=x, xn=xn, h=h, pooled=pooled, gated=gated, w=w)


def _out_proj(name, gated, g1024, row_block, x, tm):
    s = x.shape[0]

    def epi(part, extra, outs, pids):
        outs[0][...] = part + extra[0][...]

    row = pl.BlockSpec((tm, D_MODEL), lambda i: (i, 0))
    return _mm(name, (s // tm,), [gated, g1024, x],
               [pl.BlockSpec((tm, D_INNER), lambda i: (i, 0)), pl.BlockSpec((N_DEV, 256, D_MODEL), lambda i: (0, row_block, 0)), row],
               [_sds((s, D_MODEL), F32)], [row], NN, epi)[0]


def _w_out_nt_block(row_block):
    return pl.BlockSpec((2, 256, D_MODEL), lambda j, i: (j, row_block, 0))


def _in_proj_bwd(name, dh, wbuf, w_index, n_k, x, g, dy, tm, dep=()):
    s = x.shape[0]
    tm = IN_PROJ_ROWS if s % IN_PROJ_ROWS == 0 else tm

    def epi(acc, extra, outs, pids):
        dx, dg = _rms_bwd(acc, extra[0][...], extra[1][...], extra[2][...])
        outs[0][...] = dx
        outs[1][...] = dx.astype(BF)
        _accumulate(outs[2], dg, pids[0])

    row = lambda i, k: (i, 0)
    return _mm(name, (s // tm, n_k), [dh, wbuf, x, g.reshape(1, D_MODEL), dy] + list(dep),
               [pl.BlockSpec((tm, 512), lambda i, k: (i, k)), pl.BlockSpec((None, D_MODEL, 512), w_index),
                pl.BlockSpec((tm, D_MODEL), row), pl.BlockSpec((1, D_MODEL), lambda i, k: (0, 0)), pl.BlockSpec((tm, D_MODEL), row)]
               + [_ANY] * len(dep),
               [_sds((s, D_MODEL), F32), _sds((s, D_MODEL), BF), _sds((1, D_MODEL), F32)],
               [pl.BlockSpec((tm, D_MODEL), row), pl.BlockSpec((tm, D_MODEL), row), pl.BlockSpec((1, D_MODEL), lambda i, k: (0, 0))],
               NT, epi, red=1, acc_shape=(tm, D_MODEL))


def _w_out_grad(name, gated, dyb):
    s = gated.shape[0]
    return _tn(name, gated, dyb, (D_INNER, D_MODEL), (512, D_MODEL), lambda i: (i, 0), 512, D_MODEL, (4,),
               lambda i: (0, i), lambda i: (0, 0))


def _pool_bwd(dy, dyb, l, w, sv, tm, dep, early=None):
    s = dy.shape[0]
    nt = s // tm
    n = f"pool{l}b"
    h, pooled = sv["h"], sv["pooled"]
    scale = w["scale"].reshape(1, D_INNER)

    def gate_bwd(part, extra, outs, pids):
        z, sc = extra[0][...].astype(F32), extra[1][...]
        wg = extra[3][...].reshape(POOL_GROUP, POOL_GROUP)
        mpv = _dot(extra[2][...], wg, NN)
        sz, dsz = _silu_and_grad(z)
        dm = part * sz
        dmp = (dm * sc).astype(BF)
        outs[0][...] = dmp
        outs[1][...] = (part * (mpv * sc) * dsz).astype(BF)
        _accumulate(outs[2], jnp.sum(dm * mpv, axis=0, keepdims=True), pids[1])
        outs[3][...] = _dot(dmp, wg, NT).astype(BF)

    tile = lambda j, i: (i, j)
    dmp, dz, dscale, dpool = _mm(
        n + "_out", (4, nt), [dyb, w["g_out"], h, scale, pooled, w["g_grp"]] + dep,
        [pl.BlockSpec((tm, D_MODEL), lambda j, i: (i, 0)), _w_out_nt_block(0),
         pl.BlockSpec((tm, 512), lambda j, i: (i, 4 + j)), pl.BlockSpec((1, 512), lambda j, i: (0, j)), pl.BlockSpec((tm, 512), tile),
         pl.BlockSpec((N_DEV, 64, POOL_GROUP), lambda j, i: (0, j, 0))] + [_ANY] * len(dep),
        [_sds((s, D_INNER), BF), _sds((s, 2 * D_INNER), BF), _sds((1, D_INNER), F32), _sds((s, D_INNER), BF)],
        [pl.BlockSpec((tm, 512), tile), pl.BlockSpec((tm, 512), lambda j, i: (i, 4 + j)), pl.BlockSpec((1, 512), lambda j, i: (0, j)),
         pl.BlockSpec((tm, 512), tile)],
        NT, gate_bwd)
    g_out = _w_out_grad(n + "_gout", sv["gated"], dyb).reshape(N_DEV, 256, D_MODEL)
    g_grp = _tn(n + "_ggrp", pooled, dmp, (N_DEV, 256, 512), (N_DEV, 64, 512), lambda g: (0, g, 0),
                512, 512, (4,), lambda g: (0, g), lambda g: (0, g))
    dep = early(dict(g_grp=g_grp, g_out=g_out)) if early is not None else ()
    dh = _pool_window_bwd(n + "_win", dpool, tm, dz)
    g_in = _tn(n + "_gin", sv["xn"], dh, (N_DEV, D_MODEL, 512), (None, D_MODEL, 512), lambda j: (j, 0, 0),
               D_MODEL, 512, (8,), lambda j: (0, 0), lambda j: (0, j), dep)
    dep = early(dict(g_in=g_in)) if early is not None else ()
    dx, dxb, dnorm = _in_proj_bwd(n + "_in", dh, w["g_in"], lambda i, k: (k, 0, 0), 8, sv["x"], w["norm"], dy, tm, dep)
    return dx, dxb, dict(g_in=g_in, g_grp=g_grp, g_out=g_out, norm=dnorm[0], scale=dscale[0])


def _conv_in_index(i, j):
    return (j // 2, 0, j % 2)


def _conv_fwd(x, w, tm):
    s = x.shape[0]
    nt = s // tm
    h, xn = _norm_in_proj("conv_in", x, w["norm"], w["g_in"], _conv_in_index, 16, tm)
    per = tm // CONV_HALO

    def body(b_ref, c_ref, h_ref, z_ref, cp_ref, hp_ref, w_ref, o_ref, e_ref):
        i = pl.program_id(0)
        ch = c_ref[...].astype(F32) * h_ref[...].astype(F32)
        e_ref[0:CONV_HALO, :] = jnp.where(i > 0, cp_ref[...].astype(F32) * hp_ref[...].astype(F32), 0.0)
        e_ref[CONV_HALO:, :] = ch
        co = (w_ref[2:3, :] * ch + w_ref[1:2, :] * e_ref[pl.ds(CONV_HALO - 1, tm), :]
              + w_ref[0:1, :] * e_ref[pl.ds(CONV_HALO - 2, tm), :])
        z = z_ref[...].astype(F32)
        o_ref[...] = ((b_ref[...].astype(F32) * co) * (z * _sig(z))).astype(BF)

    def col(q):
        return pl.BlockSpec((tm, 512), lambda i, j: (i, 4 * q + j))

    def prev(q):
        return pl.BlockSpec((CONV_HALO, 512), lambda i, j: (jnp.maximum(i * per - 1, 0), 4 * q + j))

    gated = pl.pallas_call(
        body, name="conv_mix", grid=(nt, 4),
        in_specs=[col(0), col(1), col(2), col(3), prev(1), prev(2), pl.BlockSpec((3, 512), lambda i, j: (0, j))],
        out_specs=pl.BlockSpec((tm, 512), lambda i, j: (i, j)), out_shape=_sds((s, D_INNER), BF),
        scratch_shapes=[pltpu.VMEM((CONV_HALO + tm, 512), F32)], compiler_params=_params())(h, h, h, h, h, h, w["conv_w"])
    w = dict(w, **w["rest"]([gated]))
    y = _out_proj("conv_out", gated, w["g_out"], 0, x, tm)
    return y, dict(x=x, xn=xn, h=h, gated=gated, w=w)


def _conv_bwd(dy, dyb, w, sv, tm, dep, early):
    s = dy.shape[0]
    nt = s // tm
    h = sv["h"]
    per = tm // CONV_HALO
    last = s // CONV_HALO - 1
    n_dep = len(dep)

    def body(dy_ref, dyn_ref, wo_ref, b_ref, c_ref, h_ref, z_ref, cp_ref, hp_ref, bn_ref, zn_ref, w_ref, *rest):
        dall_ref, dw_ref, e_ref, f_ref = rest[n_dep:]
        db_ref, dc_ref, dh_ref, dz_ref = (dall_ref.at[:, q * 512:(q + 1) * 512] for q in range(4))
        i = pl.program_id(1)
        wo = wo_ref[...].reshape(512, D_MODEL)
        dg_tile = _dot(dy_ref[...], wo, NT)
        dg_next = _dot(dyn_ref[...], wo, NT)
        w0, w1, w2 = w_ref[0:1, :], w_ref[1:2, :], w_ref[2:3, :]
        c, hh, b = c_ref[...].astype(F32), h_ref[...].astype(F32), b_ref[...].astype(F32)
        ch = c * hh
        e_ref[0:CONV_HALO, :] = jnp.where(i > 0, cp_ref[...].astype(F32) * hp_ref[...].astype(F32), 0.0)
        e_ref[CONV_HALO:, :] = ch
        ch1 = e_ref[pl.ds(CONV_HALO - 1, tm), :]
        ch2 = e_ref[pl.ds(CONV_HALO - 2, tm), :]
        co = w2 * ch + w1 * ch1 + w0 * ch2
        sz, dsz = _silu_and_grad(z_ref[...].astype(F32))
        dgv = dg_tile
        dyv = dgv * sz
        dz_ref[...] = (dgv * (b * co) * dsz).astype(BF)
        db_ref[...] = (dyv * co).astype(BF)
        dco = dyv * b
        zn = zn_ref[...].astype(F32)
        f_ref[0:tm, :] = dco
        f_ref[tm:, :] = jnp.where(i < nt - 1, dg_next * (zn * _sig(zn)) * bn_ref[...].astype(F32), 0.0)
        dch = w2 * dco + w1 * f_ref[pl.ds(1, tm), :] + w0 * f_ref[pl.ds(2, tm), :]
        dc_ref[...] = (dch * hh).astype(BF)
        dh_ref[...] = (dch * c).astype(BF)
        for tap, shifted in enumerate((ch2, ch1, ch)):
            _accumulate(dw_ref.at[tap:tap + 1, :], jnp.sum(dco * shifted, axis=0, keepdims=True), i)

    def col(q):
        return pl.BlockSpec((tm, 512), lambda j, i: (i, 4 * q + j))

    def prev(q):
        return pl.BlockSpec((CONV_HALO, 512), lambda j, i: (jnp.maximum(i * per - 1, 0), 4 * q + j))

    def nxt(q):
        return pl.BlockSpec((CONV_HALO, 512), lambda j, i: (jnp.minimum((i + 1) * per, last), 4 * q + j))

    wspec = pl.BlockSpec((3, 512), lambda j, i: (0, j))
    dy_tile = pl.BlockSpec((tm, D_MODEL), lambda j, i: (i, 0))
    dy_next = pl.BlockSpec((CONV_HALO, D_MODEL), lambda j, i: (jnp.minimum((i + 1) * per, last), 0))
    dh, dw = pl.pallas_call(
        body, name="convb_mix", grid=(4, nt),
        in_specs=[dy_tile, dy_next, _w_out_nt_block(0), col(0), col(1), col(2), col(3), prev(1), prev(2), nxt(0), nxt(3), wspec]
        + [_ANY] * n_dep,
        out_specs=[pl.BlockSpec((tm, D_INNER), lambda j, i: (i, j)), wspec],
        out_shape=[_sds((s, 4 * D_INNER), BF), _sds((3, D_INNER), F32)],
        scratch_shapes=[pltpu.VMEM((CONV_HALO + tm, 512), F32)] * 2, compiler_params=_params(),
    )(dyb, dyb, w["g_out"], h, h, h, h, h, h, h, h, w["conv_w"], *dep)

    def w_block(kp):
        k = 4 * (kp % 4) + kp // 4
        return (k // 2, 0, k % 2)

    g_in = _tn("convb_gin", sv["xn"], dh, (N_DEV, D_MODEL, D_MODEL), (None, D_MODEL, 512), lambda j: (j // 2, 0, j % 2),
               D_MODEL, 512, (16,), lambda j: (0, 0), lambda j: (0, 4 * (j % 4) + j // 4))
    g_out = _w_out_grad("convb_gout", sv["gated"], dyb).reshape(N_DEV, 256, D_MODEL)
    dep = early(dict(g_in=g_in, g_out=g_out))
    dx, dxb, dnorm = _in_proj_bwd("convb_in", dh, w["g_in"], lambda i, kp: w_block(kp), 16, sv["x"], w["norm"], dy, tm, dep)
    return dx, dxb, dict(g_in=g_in, g_out=g_out, norm=dnorm[0], conv_w=dw)


def _attn_tiles(s):
    t = min(ATTN_TILE, s)
    return t, s // t


def _causal_keep(t, keys_on_rows):
    r = lax.broadcasted_iota(jnp.int32, (t, t), 0)
    c = lax.broadcasted_iota(jnp.int32, (t, t), 1)
    return (r <= c) if keys_on_rows else (c <= r)


def _mla_fwd(x, w, rope, tm):
    s = x.shape[0]
    nt = s // tm
    cosf, sinf, perm = rope
    def in_body(x_ref, gn_ref, wq_ref, wkv_ref, wkr_ref, wz_ref, gq_ref, gkv_ref, cos_ref, sin_ref, p_ref,
                ql_ref, kvl_ref, qn_ref, kvn_ref, krr_ref, z_ref, xn_ref):
        xf = x_ref[...]
        rx = lax.rsqrt(jnp.mean(xf * xf, axis=-1, keepdims=True) + NORM_EPS)
        xv = ((xf * rx) * gn_ref[...]).astype(BF)
        xn_ref[...] = xv
        ql = _dot(xv, wq_ref[...], NN)
        kvl = _dot(xv, wkv_ref[...], NN)
        ql_ref[...] = ql
        kvl_ref[...] = kvl
        rq = lax.rsqrt(jnp.mean(ql * ql, axis=-1, keepdims=True) + NORM_EPS)
        qn_ref[...] = ((ql * rq) * gq_ref[...]).astype(BF)
        rkv = lax.rsqrt(jnp.mean(kvl * kvl, axis=-1, keepdims=True) + NORM_EPS)
        kvn_ref[...] = ((kvl * rkv) * gkv_ref[...]).astype(BF)
        kr = _dot(xv, wkr_ref[...], NN)
        krr_ref[...] = _rope_fwd(kr, cos_ref[...], sin_ref[...], p_ref[...]).astype(BF)
        z_ref[...] = _dot(xv, wz_ref[...], NN).astype(BF)

    def full(a):
        return pl.BlockSpec(a.shape, lambda i: (0,) * a.ndim)

    def rows(c):
        return pl.BlockSpec((tm, c), lambda i: (i, 0))

    gq, gkv = w["q_norm"].reshape(1, Q_RANK), w["kv_norm"].reshape(1, KV_RANK)
    gn = w["norm"].reshape(1, D_MODEL)
    q_lat, kv_lat, qn, kvn, krr, z, xn = pl.pallas_call(
        in_body, name="mla_in", grid=(nt,),
        in_specs=[rows(D_MODEL), full(gn), full(w["w_q"]), full(w["w_kv"]), full(w["w_kr"]), full(w["w_z"]), full(gq), full(gkv),
                  rows(QK_ROPE), rows(QK_ROPE), full(perm)],
        out_specs=[rows(Q_RANK), rows(KV_RANK), rows(Q_RANK), rows(KV_RANK), rows(QK_ROPE), rows(D_INNER), rows(D_MODEL)],
        out_shape=[_sds((s, Q_RANK), F32), _sds((s, KV_RANK), F32), _sds((s, Q_RANK), BF), _sds((s, KV_RANK), BF),
                   _sds((s, QK_ROPE), BF), _sds((s, D_INNER), BF), _sds((s, D_MODEL), BF)],
        compiler_params=_params())(x, gn, w["w_q"], w["w_kv"], w["w_kr"], w["w_z"], gq, gkv, cosf, sinf, perm)

    def q_epi(part, extra, outs, pids):
        r = part[:, 2 * QK_NOPE:]
        lane = lax.broadcasted_iota(jnp.int32, r.shape, 1)
        swapped = jnp.where((lane & (QK_ROPE - 1)) < QK_ROPE // 2, pltpu.roll(r, 2 * QK_ROPE - QK_ROPE // 2, axis=1),
                            pltpu.roll(r, QK_ROPE // 2, axis=1))
        roped = (r * extra[0][...] + swapped * extra[1][...]) * Q_PRESCALE
        for hh in range(2):
            outs[0][hh, :, 0:QK_NOPE] = (part[:, hh * QK_NOPE:(hh + 1) * QK_NOPE] * Q_PRESCALE).astype(BF)
            outs[0][hh, :, QK_NOPE:QK_DIM] = roped[:, hh * QK_ROPE:(hh + 1) * QK_ROPE].astype(BF)

    tp = IN_PROJ_ROWS if s % IN_PROJ_ROWS == 0 else tm
    rope_row = pl.BlockSpec((tp, QK_ROPE), lambda h, i: (i, 0))
    rope_pair = pl.BlockSpec((tp, 2 * QK_ROPE), lambda h, i: (i, 0))
    cos2, sin2 = jnp.concatenate([cosf, cosf], axis=1), jnp.concatenate([sinf, sinf], axis=1)
    (q,) = _mm("mla_qup", (N_HEADS // 2, s // tp), [qn, w["w_qpair"], cos2, sin2],
               [pl.BlockSpec((tp, Q_RANK), lambda h, i: (i, 0)), pl.BlockSpec((None, Q_RANK, 2 * QK_DIM), lambda h, i: (h, 0, 0)),
                rope_pair, rope_pair],
               [_sds((N_HEADS, s, QK_DIM), BF)], [pl.BlockSpec((2, tp, QK_DIM), lambda h, i: (h, i, 0))], NN, q_epi)

    def kv_epi(part, extra, outs, pids):
        for hh in range(2):
            base = hh * (QK_NOPE + V_DIM)
            outs[0][hh, :, 0:QK_NOPE] = part[:, base:base + QK_NOPE].astype(BF)
            outs[0][hh, :, QK_NOPE:QK_DIM] = extra[0][...]
            outs[1][hh] = part[:, base + QK_NOPE:base + QK_NOPE + V_DIM].astype(BF)

    k, v = _mm("mla_kvup", (N_HEADS // 2, s // tp), [kvn, w["g512"], krr],
               [pl.BlockSpec((tp, KV_RANK), lambda h, i: (i, 0)),
                pl.BlockSpec((None, KV_RANK, 512), lambda h, i: (h, 0, 0)), rope_row],
               [_sds((N_HEADS, s, QK_DIM), BF), _sds((N_HEADS, s, V_DIM), BF)],
               [pl.BlockSpec((2, tp, QK_DIM), lambda h, i: (h, i, 0)), pl.BlockSpec((2, tp, V_DIM), lambda h, i: (h, i, 0))],
               NN, kv_epi)

    t, nq = _attn_tiles(s)

    def attn_body(q_ref, k_ref, v_ref, z_ref, o_ref, g_ref, lse_ref):
        i = pl.program_id(1)

        def block(j, carry, masked):
            start = pl.multiple_of(j * t, t)
            out = []
            for hh, (m, lsum, acc) in enumerate(carry):
                sc = _dot(q_ref[hh], k_ref[hh, pl.ds(start, t), :], NT)
                if masked:
                    sc = jnp.where(_causal_keep(t, False), sc, NEG_BIG)
                mn = jnp.maximum(m, jnp.max(sc, axis=-1, keepdims=True))
                alpha = jnp.exp2(m - mn)
                p = jnp.exp2(sc - mn)
                lsum = alpha * lsum + jnp.sum(p, axis=-1, keepdims=True)
                acc = alpha * acc + _dot(p.astype(BF), v_ref[hh, pl.ds(start, t), :], NN)
                out.append((mn, lsum, acc))
            return tuple(out)

        init = ((jnp.full((t, 1), NEG_BIG, F32), jnp.zeros((t, 1), F32), jnp.zeros((t, V_DIM), F32)),) * ATTN_FWD_HEADS
        carry = lax.fori_loop(0, i, lambda j, c: block(j, c, False), init)
        for hh, (m, lsum, acc) in enumerate(block(i, carry, True)):
            cols = slice(hh * V_DIM, (hh + 1) * V_DIM)
            o = acc / lsum
            z = z_ref[:, cols].astype(F32)
            o_ref[:, cols] = o
            g_ref[:, cols] = (o * (z * _sig(z))).astype(BF)
            lse_ref[hh] = _to_row(m + jnp.log(lsum) * LOG2_E)

    hp = ATTN_FWD_HEADS
    head_col = pl.BlockSpec((t, hp * V_DIM), lambda h, i: (i, h))
    o, gated, lse = pl.pallas_call(
        attn_body, name="mla_attn", grid=(N_HEADS // hp, nq),
        in_specs=[pl.BlockSpec((hp, t, QK_DIM), lambda h, i: (h, i, 0)), pl.BlockSpec((hp, s, QK_DIM), lambda h, i: (h, 0, 0)),
                  pl.BlockSpec((hp, s, V_DIM), lambda h, i: (h, 0, 0)), head_col],
        out_specs=[head_col, head_col, pl.BlockSpec((hp, None, 1, t), lambda h, i: (h, i, 0, 0))],
        out_shape=[_sds((s, D_INNER), F32), _sds((s, D_INNER), BF), _sds((N_HEADS, nq, 1, t), F32)],
        compiler_params=_params())(q, k, v, z)
    y = _out_proj("mla_out", gated, w["g1024"], 0, x, tm)
    return y, dict(x=x, xn=xn, q_lat=q_lat, kv_lat=kv_lat, qn=qn, kvn=kvn, z=z, q=q, k=k, v=v, o=o, lse=lse, gated=gated)


def _mla_bwd(dy, dyb, w, sv, rope, tm, dep):
    s = dy.shape[0]
    nt = s // tm
    cosf, sinf, perm = rope
    t, nq = _attn_tiles(s)
    assert t == tm, "the row statistics of the backward are laid out per attention tile"
    q, k, v, lse = sv["q"], sv["k"], sv["v"], sv["lse"]

    def gate_bwd(part, extra, outs, pids):
        z, o = extra[0][...].astype(F32), extra[1][...]
        sz, dsz = _silu_and_grad(z)
        do = part * sz
        outs[0][...] = do.astype(BF)
        outs[1][...] = (part * o * dsz).astype(BF)
        prod = do * o
        for hh in range(4):
            outs[2][hh] = _to_row(jnp.sum(prod[:, hh * V_DIM:(hh + 1) * V_DIM], axis=-1, keepdims=True))

    tile = lambda j, i: (i, j)
    dob, dz, delta = _mm(
        "mlab_out", (4, nt), [dyb, w["g1024"], sv["z"], sv["o"]] + dep,
        [pl.BlockSpec((tm, D_MODEL), lambda j, i: (i, 0)), _w_out_nt_block(0),
         pl.BlockSpec((tm, 512), tile), pl.BlockSpec((tm, 512), tile)] + [_ANY] * len(dep),
        [_sds((s, D_INNER), BF), _sds((s, D_INNER), BF), _sds((N_HEADS, nt, 1, tm), F32)],
        [pl.BlockSpec((tm, 512), tile), pl.BlockSpec((tm, 512), tile), pl.BlockSpec((4, None, 1, tm), lambda j, i: (j, i, 0, 0))],
        NT, gate_bwd)

    hp = ATTN_HEADS_PER_STEP

    def attn_bwd_body(k_ref, v_ref, q_ref, do_ref, lse_ref, dl_ref, cos_ref, sin_ref, p_ref, dkv_ref, dkr_ref, dq_ref, dq_acc):
        j = pl.program_id(1)

        @pl.when(j == 0)
        def _():
            dq_acc[...] = jnp.zeros(dq_acc.shape, F32)

        def block(i, carry, masked):
            rows = pl.ds(pl.multiple_of(i * t, t), t)
            out = []
            for hh, (dk, dv) in enumerate(carry):
                kb, vb = k_ref[hh], v_ref[hh]
                qb, dob_ = q_ref[hh, rows, :], do_ref[rows, hh * V_DIM:(hh + 1) * V_DIM]
                st = _dot(kb, qb, NT)
                if masked:
                    st = jnp.where(_causal_keep(t, True), st, NEG_BIG)
                pt = jnp.exp2(st - lse_ref[hh, i])
                dv = dv + _dot(pt.astype(BF), dob_, NN)
                dst = (pt * (_dot(vb, dob_, NT) - dl_ref[hh, i])).astype(BF)
                dk = dk + _dot(dst, qb, NN)
                dq_acc[hh, rows, :] += _dot(dst, kb, TN)
                out.append((dk, dv))
            return tuple(out)

        init = ((jnp.zeros((t, QK_DIM), F32), jnp.zeros((t, V_DIM), F32)),) * hp
        carry = block(j, init, True)
        carry = lax.fori_loop(j + 1, nq, lambda i, c: block(i, c, False), carry)
        for hh, (dk, dv) in enumerate(carry):
            dk = dk * LN_2
            base = hh * 2 * V_DIM
            dkv_ref[:, base:base + QK_NOPE] = dk[:, 0:QK_NOPE].astype(BF)
            dkv_ref[:, base + QK_NOPE:base + 2 * V_DIM] = dv.astype(BF)
            dkr_ref[hh] = dk[:, QK_NOPE:]

        @pl.when(j == nq - 1)
        def _():
            for hh in range(hp):
                for c in range(nq):
                    rows = slice(c * t, (c + 1) * t)
                    dq = dq_acc[hh, rows, :] * ATTN_SCALE
                    dq_ref[hh, rows, 0:QK_NOPE] = dq[:, 0:QK_NOPE].astype(BF)
                    dq_ref[hh, rows, QK_NOPE:] = _rope_bwd(dq[:, QK_NOPE:], cos_ref[rows, :], sin_ref[rows, :], p_ref[...]).astype(BF)

    row_stats = pl.BlockSpec((hp, nq, 1, t), lambda h, j: (h, 0, 0, 0))
    seq_rope = pl.BlockSpec((s, QK_ROPE), lambda h, j: (0, 0))
    head_seq = pl.BlockSpec((hp, s, QK_DIM), lambda h, j: (h, 0, 0))
    dkv, dkr_h, dq = pl.pallas_call(
        attn_bwd_body, name="mlab_attn", grid=(N_HEADS // hp, nq),
        in_specs=[pl.BlockSpec((hp, t, QK_DIM), lambda h, j: (h, j, 0)), pl.BlockSpec((hp, t, V_DIM), lambda h, j: (h, j, 0)),
                  head_seq, pl.BlockSpec((s, hp * V_DIM), lambda h, j: (0, h)), row_stats, row_stats, seq_rope, seq_rope,
                  pl.BlockSpec((QK_ROPE, QK_ROPE), lambda h, j: (0, 0))],
        out_specs=[pl.BlockSpec((t, hp * 2 * V_DIM), lambda h, j: (j, h)), pl.BlockSpec((hp, t, QK_ROPE), lambda h, j: (h, j, 0)), head_seq],
        out_shape=[_sds((s, N_HEADS * 2 * V_DIM), BF), _sds((N_HEADS, s, QK_ROPE), F32), _sds((N_HEADS, s, QK_DIM), BF)],
        scratch_shapes=[pltpu.VMEM((hp, s, QK_DIM), F32)],
        compiler_params=_params())(k, v, q, dob, lse, delta, cosf, sinf, perm)

    def dkr_body(d_ref, cos_ref, sin_ref, p_ref, o_ref):
        tot = d_ref[0]
        for hh in range(1, N_HEADS):
            tot = tot + d_ref[hh]
        o_ref[...] = _rope_bwd(tot, cos_ref[...], sin_ref[...], p_ref[...]).astype(BF)

    r64 = pl.BlockSpec((tm, QK_ROPE), lambda i: (i, 0))
    dkr = pl.pallas_call(
        dkr_body, name="mlab_dkr", grid=(nt,),
        in_specs=[pl.BlockSpec((N_HEADS, tm, QK_ROPE), lambda i: (0, i, 0)), r64, r64, pl.BlockSpec((QK_ROPE, QK_ROPE), lambda i: (0, 0))],
        out_specs=r64, out_shape=_sds((s, QK_ROPE), BF), compiler_params=_params())(dkr_h, cosf, sinf, perm)

    def lat_epi(acc, extra, outs, pids):
        dx, dg = _rms_bwd(acc, extra[0][...], extra[1][...], None)
        outs[0][...] = dx.astype(BF)
        _accumulate(outs[1], dg, pids[0])

    tp = IN_PROJ_ROWS if s % IN_PROJ_ROWS == 0 else tm

    def lat_bwd(name, a, a_spec, b, b_spec, n_k, lat, g, rank):
        row = lambda i, k: (i, 0)
        one = lambda i, k: (0, 0)
        return _mm(name, (s // tp, n_k), [a, b, lat, g.reshape(1, rank)],
                   [a_spec, b_spec, pl.BlockSpec((tp, rank), row), pl.BlockSpec((1, rank), one)],
                   [_sds((s, rank), BF), _sds((1, rank), F32)], [pl.BlockSpec((tp, rank), row), pl.BlockSpec((1, rank), one)],
                   NT, lat_epi, red=1, acc_shape=(tp, rank))

    d_ql, g_qnorm = lat_bwd("mlab_qup", dq, pl.BlockSpec((None, tp, QK_DIM), lambda i, h: (h, i, 0)),
                            w["w_qh"], pl.BlockSpec((None, Q_RANK, QK_DIM), lambda i, h: (h, 0, 0)), N_HEADS,
                            sv["q_lat"], w["q_norm"], Q_RANK)
    d_kvl, g_kvnorm = lat_bwd("mlab_kvup", dkv, pl.BlockSpec((tp, 512), lambda i, kk: (i, kk)),
                              w["g512"], pl.BlockSpec((None, KV_RANK, 512), lambda i, kk: (kk, 0, 0)), N_DEV,
                              sv["kv_lat"], w["kv_norm"], KV_RANK)

    def in_bwd(dql_ref, dkvl_ref, dkr_ref, dz_ref, wq_ref, wkv_ref, wkr_ref, wz_ref, x_ref, g_ref, dy_ref, dx_ref, dxb_ref, dg_ref):
        acc = (_dot(dql_ref[...], wq_ref[...], NT) + _dot(dkvl_ref[...], wkv_ref[...], NT)
               + _dot(dkr_ref[...], wkr_ref[...], NT) + _dot(dz_ref[...], wz_ref[...], NT))
        dx, dg = _rms_bwd(acc, x_ref[...], g_ref[...], dy_ref[...])
        dx_ref[...] = dx
        dxb_ref[...] = dx.astype(BF)
        _accumulate(dg_ref, dg, pl.program_id(0))

    def full(a):
        return pl.BlockSpec(a.shape, lambda i: (0,) * a.ndim)

    def rows(c):
        return pl.BlockSpec((tm, c), lambda i: (i, 0))

    gm = w["norm"].reshape(1, D_MODEL)
    dx, dxb, g_norm = pl.pallas_call(
        in_bwd, name="mlab_in", grid=(nt,),
        in_specs=[rows(Q_RANK), rows(KV_RANK), rows(QK_ROPE), rows(D_INNER), full(w["w_q"]), full(w["w_kv"]), full(w["w_kr"]),
                  full(w["w_z"]), rows(D_MODEL), full(gm), rows(D_MODEL)],
        out_specs=[rows(D_MODEL), rows(D_MODEL), full(gm)],
        out_shape=[_sds((s, D_MODEL), F32), _sds((s, D_MODEL), BF), _sds((1, D_MODEL), F32)],
        compiler_params=_params())(d_ql, d_kvl, dkr, dz, w["w_q"], w["w_kv"], w["w_kr"], w["w_z"], sv["x"], gm, dy)

    xn = sv["xn"]
    one = lambda j: (0, 0)
    g_q = _tn("mlab_gq", xn, d_ql, (D_MODEL, Q_RANK), (D_MODEL, Q_RANK), one, D_MODEL, Q_RANK, (1,), one, one)
    g_kv = _tn("mlab_gkv", xn, d_kvl, (D_MODEL, KV_RANK), (D_MODEL, KV_RANK), one, D_MODEL, KV_RANK, (1,), one, one)
    g_kr = _tn("mlab_gkr", xn, dkr, (D_MODEL, QK_ROPE), (D_MODEL, QK_ROPE), one, D_MODEL, QK_ROPE, (1,), one, one)
    g_z = _tn("mlab_gz", xn, dz, (D_MODEL, D_INNER), (D_MODEL, 512), lambda j: (0, j), D_MODEL, 512, (4,), one, lambda j: (0, j))
    g_in = jnp.concatenate([g_q, g_kv, g_kr, g_z], axis=1)
    g_qh = _tn("mlab_gqup", sv["qn"], dq, (N_HEADS, Q_RANK, QK_DIM), (None, Q_RANK, QK_DIM), lambda h: (h, 0, 0),
               Q_RANK, QK_DIM, (N_HEADS,), lambda h: (0, 0), lambda h: (h, 0, 0))
    g_kvup = _tn("mlab_gkvup", sv["kvn"], dkv, (N_DEV, KV_RANK, 512), (None, KV_RANK, 512), lambda j: (j, 0, 0),
                 KV_RANK, 512, (N_DEV,), lambda j: (0, 0), lambda j: (0, j))
    g_out = _w_out_grad("mlab_gout", sv["gated"], dyb)
    s384 = g_qh.reshape(N_DEV, 2, Q_RANK, QK_DIM).transpose(0, 2, 1, 3).reshape(N_DEV, Q_RANK, 2 * QK_DIM)
    s344 = g_in.reshape(D_MODEL, N_DEV, 344).transpose(1, 0, 2)
    return dx, dxb, dict(s344=s344, s384=s384, s512=g_kvup, s1024=g_out.reshape(N_DEV, 256, D_MODEL),
                         norm=g_norm[0], q_norm=g_qnorm[0], kv_norm=g_kvnorm[0])


def _loss_head(x, g, target, tm):
    s, d = x.shape

    def body(x_ref, g_ref, t_ref, dx_ref, dxb_ref, dg_ref, loss_ref):
        i = pl.program_id(0)
        xv, gv = x_ref[...], g_ref[...]
        r = lax.rsqrt(jnp.mean(xv * xv, axis=-1, keepdims=True) + NORM_EPS)
        err = (xv * r) * gv - t_ref[...]
        part = 0.5 * jnp.sum(jnp.mean(err * err, axis=-1, keepdims=True), axis=0, keepdims=True)
        dx, dg = _rms_bwd(err * (1.0 / d), xv, gv, None)
        dx_ref[...] = dx
        dxb_ref[...] = dx.astype(BF)
        _accumulate(dg_ref, dg, i)
        _accumulate(loss_ref, jnp.broadcast_to(part, loss_ref.shape), i)

    row = pl.BlockSpec((tm, d), lambda i: (i, 0))
    one = pl.BlockSpec((1, d), lambda i: (0, 0))
    return pl.pallas_call(
        body, name="loss_head", grid=(s // tm,), in_specs=[row, one, row],
        out_specs=[row, row, one, pl.BlockSpec((8, 128), lambda i: (0, 0))],
        out_shape=[_sds((s, d), F32), _sds((s, d), BF), _sds((1, d), F32), _sds((8, 128), F32)],
        compiler_params=_params())(x, g.reshape(1, d), target)


def _rope_tables(pos):
    inv_freq = ROPE_BASE ** (-jnp.arange(0, QK_ROPE, 2, dtype=F32) / QK_ROPE)
    ang = pos.astype(F32)[:, None] * inv_freq
    cos, sin = jnp.cos(ang), jnp.sin(ang)
    idx = jnp.arange(QK_ROPE)
    perm = (idx[:, None] == (idx[None, :] + QK_ROPE // 2) % QK_ROPE).astype(F32)
    return jnp.concatenate([cos, cos], axis=1), jnp.concatenate([-sin, sin], axis=1), perm


def _local_step(x, pos, target, final_norm, get_w, put_g):
    s = x.shape[0]
    tm = min(512, s)
    tl = IN_PROJ_ROWS if s % IN_PROJ_ROWS == 0 else tm
    rope = _rope_tables(pos)
    w0 = get_w(0, [])
    x1, sv0 = _pool_fwd(x, 0, w0, tl)
    w1 = get_w(1, [x1])
    x2, sv1 = _conv_fwd(x1, w1, tl)
    w2 = get_w(2, [x2])
    x3, sv2 = _mla_fwd(x2, w2, rope, tm)
    w3 = get_w(3, [x3])
    x4, sv3 = _pool_fwd(x3, 1, w3, tl)
    d4, d4b, g_final, loss = _loss_head(x4, final_norm, target, tm)
    d3, d3b, gp1 = _pool_bwd(d4, d4b, 1, sv3["w"], sv3, tl, [])
    dep = put_g(3, gp1)
    d2, d2b, gm = _mla_bwd(d3, d3b, w2, sv2, rope, tm, dep)
    dep = put_g(2, gm)
    sent = {}

    def send_conv(part):
        sent["dep"] = put_g(1, part)
        return sent["dep"]

    d1, d1b, gc = _conv_bwd(d2, d2b, sv1["w"], sv1, tl, dep, send_conv)
    d0, _, gp0 = _pool_bwd(d1, d1b, 0, sv0["w"], sv0, tl, sent["dep"], early=lambda part: put_g(0, part))
    put_g(4, {0: dict(gp0, final_norm=g_final[0]), 1: gc, 2: gm, 3: gp1})
    return loss[0, 0], d0


def _pack_groups(p):
    bf = lambda a: a.astype(BF)
    grp = lambda l: bf(p["pool_w_grp"][l].reshape(4 * 64, POOL_GROUP))
    return [[bf(p["pool_w_in"][0]), _pack_small(p, SMALL_ROWS_AG)],
            [grp(0), bf(p["pool_w_out"][0])],
            [bf(p["conv_w_in"][0])],
            [bf(p["conv_w_out"][0])],
            [bf(p[k][0]) for k in ("mla_w_in", "mla_w_q_up", "mla_w_kv_up", "mla_w_out")],
            [bf(p["pool_w_in"][1]), grp(1), bf(p["pool_w_out"][1])]]


_SMALL_SHARDED = ("pool_norm", "pool_scale", "mla_norm", "mla_q_norm", "mla_kv_norm", "conv_w")
_SMALL_REPLICATED = ("conv_norm", "final_norm")


def _pack_small(p, rows, with_replicated=False):
    parts = [p[k].reshape(-1) for k in _SMALL_SHARDED]
    if with_replicated:
        parts += [p[k].reshape(-1) for k in _SMALL_REPLICATED]
    flat = jnp.concatenate(parts)
    return jnp.pad(flat, (0, rows * 128 - flat.shape[0])).reshape(rows, 128)


_SMALL_SHARD_SHAPES = dict(pool_norm=(2, 128), pool_scale=(2, 256), mla_norm=(1, 128), mla_q_norm=(1, 48),
                           mla_kv_norm=(1, 32), conv_w=(1, 3, 256), conv_norm=(1, 1024), final_norm=(1024,))


def _unpack_small(buf, with_replicated=False):
    flat = buf.reshape(-1)
    out, off = {}, 0
    for k in _SMALL_SHARDED + (_SMALL_REPLICATED if with_replicated else ()):
        shp = _SMALL_SHARD_SHAPES[k]
        n = 1
        for d in shp:
            n *= d
        out[k] = flat[off:off + n].reshape(shp)
        off += n
    return out


def _small_views(gsmall):
    flat = gsmall.reshape(N_DEV, -1)

    def cols(off, rows, width):
        return flat[:, off:off + rows * width].reshape(N_DEV, rows, width).transpose(1, 0, 2).reshape(rows, N_DEV * width)

    return dict(pool_norm=cols(0, 2, 128), pool_scale=cols(256, 2, 256), mla_norm=cols(768, 1, 128)[0],
                q_norm=cols(896, 1, 48)[0], kv_norm=cols(944, 1, 32)[0], conv_w=cols(976, 3, 256))


def _pair_columns(g384):
    heads = g384.reshape(N_DEV, Q_RANK, 2, QK_DIM)
    nope = heads[..., :QK_NOPE].reshape(N_DEV, Q_RANK, 2 * QK_NOPE)
    rope = heads[..., QK_NOPE:].reshape(N_DEV, Q_RANK, 2 * QK_ROPE)
    return jnp.concatenate([nope, rope], axis=-1)


def _layer_weights(layer, bufs, small, conv_norm):
    if layer in (0, 3):
        l = 0 if layer == 0 else 1
        return dict(g_in=bufs[0], norm=small["pool_norm"][l], scale=small["pool_scale"][l])
    if layer == 1:
        return dict(g_in=bufs[0], norm=conv_norm.reshape(D_MODEL), conv_w=small["conv_w"])
    g344, g384, g512, g1024 = bufs
    w_in = g344.transpose(1, 0, 2).reshape(D_MODEL, N_DEV * 344)
    return dict(
        g512=g512, g1024=g1024,
        w_q=w_in[:, :Q_RANK], w_kv=w_in[:, Q_RANK:Q_RANK + KV_RANK],
        w_kr=w_in[:, Q_RANK + KV_RANK:Q_RANK + KV_RANK + QK_ROPE], w_z=w_in[:, Q_RANK + KV_RANK + QK_ROPE:],
        w_qh=g384.reshape(N_DEV, Q_RANK, 2, QK_DIM).transpose(0, 2, 1, 3).reshape(N_HEADS, Q_RANK, QK_DIM),
        w_qpair=_pair_columns(g384),
        norm=small["mla_norm"], q_norm=small["q_norm"], kv_norm=small["kv_norm"])


_GRAD_KEYS = {0: ("g_in", "g_grp", "g_out"), 3: ("g_in", "g_grp", "g_out"), 1: ("g_in", "g_out"), 2: ("s344", "s384", "s512", "s1024")}
_GRAD_PARAM = {0: dict(g_in="pool_w_in", g_grp="pool_w_grp", g_out="pool_w_out"), 1: dict(g_in="conv_w_in", g_out="conv_w_out"),
               2: dict(s344="mla_w_in", s384="mla_w_q_up", s512="mla_w_kv_up", s1024="mla_w_out")}
_GRAD_PARAM[3] = _GRAD_PARAM[0]


def _grad_group(layer, g):
    keys = tuple(k for k in _GRAD_KEYS[layer] if k in g)
    return keys, [g[k] for k in keys]


def _pack_small_grads(g):
    def split(a, rows, width):
        return a.reshape(rows, N_DEV, width).transpose(1, 0, 2).reshape(N_DEV, rows * width)

    rep = lambda a: jnp.broadcast_to(a.reshape(1, -1), (N_DEV, a.size))
    flat = jnp.concatenate([
        split(jnp.stack([g[0]["norm"], g[3]["norm"]]), 2, 128), split(jnp.stack([g[0]["scale"], g[3]["scale"]]), 2, 256),
        split(g[2]["norm"], 1, 128), split(g[2]["q_norm"], 1, 48), split(g[2]["kv_norm"], 1, 32), split(g[1]["conv_w"], 3, 256),
        rep(g[1]["norm"]), rep(g[0]["final_norm"])], axis=1)
    return jnp.pad(flat, ((0, 0), (0, SMALL_ROWS_RS * 128 - flat.shape[1]))).reshape(N_DEV, SMALL_ROWS_RS, 128)


def _peers(x, y, c):
    for k in range(1, N_DEV):
        px = 1 - x if k & 4 else x
        py = 1 - y if k & 2 else y
        pc = 1 - c if k & 1 else c
        yield k - 1, (px, py, pc), 4 * px + 2 * py + pc


_SIBLING = (0,)
_ICI_DIRECT = (1, 3, 5)


def _remote_copies(srcs, lands, send_sems, recv_sems, gather, ks=None):
    x, y, c = lax.axis_index("x"), lax.axis_index("y"), lax.axis_index("c")
    me = 4 * x + 2 * y + c
    copies = []
    for k, peer, pidx in _peers(x, y, c):
        if ks is not None and k not in ks:
            continue
        for a, (src, land) in enumerate(zip(srcs, lands)):
            copies.append(pltpu.make_async_remote_copy(
                src_ref=src if gather else src.at[pidx], dst_ref=land.at[me],
                send_sem=send_sems.at[a * (N_DEV - 1) + k], recv_sem=recv_sems.at[a * (N_DEV - 1) + k],
                device_id=peer, device_id_type=pl.DeviceIdType.MESH))
    return copies


def _relay_copies(lands, send_sems, recv_sems):
    x, y, c = lax.axis_index("x"), lax.axis_index("y"), lax.axis_index("c")
    copies = []
    for j, k in enumerate(_ICI_DIRECT):
        px = 1 - x if (k + 1) & 4 else x
        py = 1 - y if (k + 1) & 2 else y
        slot = 4 * px + 2 * py + c
        for a, land in enumerate(lands):
            copies.append(pltpu.make_async_remote_copy(
                src_ref=land.at[slot], dst_ref=land.at[slot],
                send_sem=send_sems.at[a * len(_ICI_DIRECT) + j], recv_sem=recv_sems.at[a * len(_ICI_DIRECT) + j],
                device_id=(x, y, 1 - c), device_id_type=pl.DeviceIdType.MESH))
    return copies


_HBM = pl.BlockSpec(memory_space=pltpu.HBM)
_SEM = pl.BlockSpec(memory_space=pltpu.SEMAPHORE)
_EFFECT = pltpu.SideEffectType.DATAFLOW_SIDE_EFFECTING


def _own_slabs(name, arrays, gather, dep):
    n, nd = len(arrays), len(dep)
    me = (4 * lax.axis_index("x") + 2 * lax.axis_index("y") + lax.axis_index("c")).astype(jnp.int32).reshape(1)

    def body(me_ref, *refs):
        for a in range(n):
            refs[n + nd + a][...] = refs[a][...]

    def slab(shape):
        return pl.BlockSpec((None,) + tuple(shape), lambda i, me_ref: (me_ref[0],) + (0,) * len(shape))

    def whole(shape):
        return pl.BlockSpec(tuple(shape), lambda i, me_ref: (0,) * len(shape))

    outs = [_sds(((N_DEV,) + a.shape) if gather else a.shape, a.dtype) for a in arrays]
    grid_spec = pltpu.PrefetchScalarGridSpec(
        num_scalar_prefetch=1, grid=(1,),
        in_specs=[whole(a.shape) if gather else slab(a.shape[1:]) for a in arrays] + [_ANY] * nd,
        out_specs=[slab(o.shape[1:]) for o in outs])
    return pl.pallas_call(body, name=name, grid_spec=grid_spec, out_shape=outs, compiler_params=_params())(me, *arrays, *dep)


def _exchange_start(name, arrays, lands, gather, ks=None):
    n = len(arrays)

    def body(*refs):
        srcs, lnds, send_sems, recv_sems, token = refs[:n], refs[n:2 * n], refs[2 * n], refs[2 * n + 1], refs[-1]
        for cp in _remote_copies(srcs, lnds, send_sems, recv_sems, gather, ks):
            cp.start()
        token[...] = jnp.zeros(token.shape, F32)

    sems = pltpu.SemaphoreType.DMA((n * (N_DEV - 1),))
    thru = [pltpu.HBM(a.shape, a.dtype) for a in list(arrays) + list(lands)]
    res = pl.pallas_call(
        body, name=name, in_specs=[_HBM] * (2 * n),
        out_specs=[_SEM, _SEM] + [_HBM] * (2 * n) + [pl.BlockSpec(memory_space=pltpu.VMEM)],
        out_shape=[sems, sems] + thru + [_sds((8, 128), F32)],
        input_output_aliases={i: 2 + i for i in range(2 * n)},
        compiler_params=pltpu.CompilerParams(has_side_effects=_EFFECT),
    )(*[pltpu.with_memory_space_constraint(a, pltpu.HBM) for a in list(arrays) + list(lands)])
    return res[0], res[1], list(res[2:2 + n]), list(res[2 + n:2 + 2 * n]), res[-1]


def _exchange_wait(name, send_sems, recv_sems, arrays, lands, after, gather):
    n = len(arrays)
    n_after = len(after)

    def body(*refs):
        srcs, lnds = refs[:n], refs[n:2 * n]
        copies = _remote_copies(srcs, lnds, refs[2 * n], refs[2 * n + 1], gather)
        for cp in copies:
            cp.wait_send()
        for cp in copies:
            cp.wait_recv()

    thru = [pltpu.HBM(a.shape, a.dtype) for a in list(arrays) + list(lands)]
    res = pl.pallas_call(
        body, name=name, in_specs=[_HBM] * (2 * n) + [_SEM, _SEM] + [pl.BlockSpec(memory_space=pl.ANY)] * n_after,
        out_specs=[_HBM] * (2 * n), out_shape=thru, input_output_aliases={i: i for i in range(2 * n)},
        compiler_params=pltpu.CompilerParams(has_side_effects=_EFFECT),
    )(*arrays, *lands, send_sems, recv_sems, *after)
    return list(res[n:])


def _gather_relay(name, send_sems, recv_sems, arrays, lands, after):
    n, n_after = len(arrays), len(after)

    def body(*refs):
        srcs, lnds, first_send, first_recv = refs[:n], refs[n:2 * n], refs[2 * n], refs[2 * n + 1]
        send2, recv2, token = refs[2 * n + 2 + n_after], refs[2 * n + 3 + n_after], refs[-1]
        for cp in _remote_copies(srcs, lnds, first_send, first_recv, True, _ICI_DIRECT):
            cp.wait_recv()
        for cp in _relay_copies(lnds, send2, recv2):
            cp.start()
        token[...] = jnp.zeros(token.shape, F32)

    sems = pltpu.SemaphoreType.DMA((n * len(_ICI_DIRECT),))
    thru = [pltpu.HBM(a.shape, a.dtype) for a in list(arrays) + list(lands)]
    res = pl.pallas_call(
        body, name=name, in_specs=[_HBM] * (2 * n) + [_SEM, _SEM] + [_ANY] * n_after,
        out_specs=[_SEM, _SEM] + [_HBM] * (2 * n) + [pl.BlockSpec(memory_space=pltpu.VMEM)],
        out_shape=[sems, sems] + thru + [_sds((8, 128), F32)],
        input_output_aliases={i: 2 + i for i in range(2 * n)},
        compiler_params=pltpu.CompilerParams(has_side_effects=_EFFECT),
    )(*arrays, *lands, send_sems, recv_sems, *after)
    return res[0], res[1], list(res[2:2 + n]), list(res[2 + n:2 + 2 * n]), res[-1]


def _gather_wait2(name, send_sems, recv_sems, send2, recv2, arrays, lands, after):
    n, n_after = len(arrays), len(after)

    def body(*refs):
        srcs, lnds = refs[:n], refs[n:2 * n]
        s1, r1, s2, r2 = refs[2 * n:2 * n + 4]
        for cp in _remote_copies(srcs, lnds, s1, r1, True, _SIBLING + _ICI_DIRECT):
            cp.wait_send()
        for cp in _remote_copies(srcs, lnds, s1, r1, True, _SIBLING):
            cp.wait_recv()
        relay = _relay_copies(lnds, s2, r2)
        for cp in relay:
            cp.wait_send()
        for cp in relay:
            cp.wait_recv()

    thru = [pltpu.HBM(a.shape, a.dtype) for a in list(arrays) + list(lands)]
    res = pl.pallas_call(
        body, name=name, in_specs=[_HBM] * (2 * n) + [_SEM] * 4 + [_ANY] * n_after,
        out_specs=[_HBM] * (2 * n), out_shape=thru, input_output_aliases={i: i for i in range(2 * n)},
        compiler_params=pltpu.CompilerParams(has_side_effects=_EFFECT),
    )(*arrays, *lands, send_sems, recv_sems, send2, recv2, *after)
    return list(res[n:])


def _adamw_math(g, w, m, v):
    m = ADAM_B1 * m + (1.0 - ADAM_B1) * g
    v = ADAM_B2 * v + (1.0 - ADAM_B2) * (g * g)
    m_hat = m / (1.0 - ADAM_B1 ** ADAM_STEP)
    v_hat = v / (1.0 - ADAM_B2 ** ADAM_STEP)
    delta = -ADAM_LR * (m_hat / (jnp.sqrt(v_hat) + ADAM_EPS) + ADAM_WD * w)
    return delta, m, v


def _sum_adamw(name, recv, row_off, w, m, v, tr, layer=0):
    width = recv.shape[-1]
    w2, m2, v2 = (a.reshape(a.shape[0], -1, width) for a in (w, m, v))
    rows = w2.shape[1]
    base = row_off // tr

    def body(r_ref, w_ref, m_ref, v_ref, g_ref, d_ref, mo_ref, vo_ref):
        g = r_ref[0].astype(F32)
        for src in range(1, N_DEV):
            g = g + r_ref[src].astype(F32)
        delta, mn, vn = _adamw_math(g, w_ref[...], m_ref[...], v_ref[...])
        g_ref[...] = g
        d_ref[...] = delta
        mo_ref[...] = mn
        vo_ref[...] = vn

    blk = pl.BlockSpec((tr, width), lambda i: (i, 0))
    wblk = pl.BlockSpec((None, tr, width), lambda i: (layer, i, 0))
    return pl.pallas_call(
        body, name=name, grid=(rows // tr,),
        in_specs=[pl.BlockSpec((N_DEV, tr, width), lambda i: (0, base + i, 0)), wblk, wblk, wblk],
        out_specs=[blk] * 4, out_shape=[_sds((rows, width), F32)] * 4, compiler_params=_params())(recv, w2, m2, v2)


_WEIGHTS = ("pool_norm", "pool_w_in", "pool_w_grp", "pool_scale", "pool_w_out", "conv_norm", "conv_w_in", "conv_w", "conv_w_out",
            "mla_norm", "mla_w_in", "mla_q_norm", "mla_w_q_up", "mla_kv_norm", "mla_w_kv_up", "mla_w_out", "final_norm")


def _step(x, positions, loss_target, p, m, v):
    gathers, tokens, dep = [], [], []
    for group, arrays in enumerate(_pack_groups(p)):
        lands = _own_slabs(f"gather{group}_own", arrays, True, dep)
        ks = _SIBLING + _ICI_DIRECT if group < TWO_LEVEL_GROUPS else None
        ssem, rsem, arrays, lands, token = _exchange_start(f"gather{group}_start", arrays, lands, True, ks)
        gathers.append((ssem, rsem, arrays, lands))
        tokens.append(token)
        dep = [token]
    state = {}

    def wait_group(group, after):
        if group >= TWO_LEVEL_GROUPS:
            return _exchange_wait(f"gather{group}_wait", *gathers[group], after, True)
        ssem, rsem, arrays, lands = gathers[group]
        send2, recv2, arrays, lands, token = _gather_relay(f"gather{group}_relay", ssem, rsem, arrays, lands, after)
        return _gather_wait2(f"gather{group}_wait", ssem, rsem, send2, recv2, arrays, lands, [token])

    def get_w(layer, after):
        if layer == 0:
            bufs = wait_group(0, list(tokens))
            state["small"] = _small_views(bufs[1])
            rest = lambda later: dict(zip(("g_grp", "g_out"), wait_group(1, later)))
        elif layer == 1:
            bufs = wait_group(2, after)
            rest = lambda later: dict(g_out=wait_group(3, later)[0])
        elif layer == 2:
            bufs = wait_group(4, after)
        else:
            bufs = wait_group(5, after)
            rest = lambda later: dict(g_grp=bufs[1], g_out=bufs[2])
        w = _layer_weights(layer, bufs, state["small"], p["conv_norm"])
        if layer != 2:
            w["rest"] = rest
        return w

    scatters = []

    def put_g(layer, g):
        if layer == 4:
            keys, arrays = ("small",), [_pack_small_grads(g)]
        else:
            keys, arrays = _grad_group(layer, g)
        n = len(scatters)
        lands = _own_slabs(f"scatter{n}_own", arrays, False, [])
        ssem, rsem, arrays, lands, token = _exchange_start(f"scatter{n}_start", arrays, lands, False)
        scatters.append((layer, keys, (ssem, rsem, arrays, lands)))
        tokens.append(token)
        return [token]

    loss, grad_x = _local_step(x[0], positions[0], loss_target[0], p["final_norm"], get_w, put_g)

    res, after = {}, [tokens[-1]]
    for n, (layer, keys, handles) in enumerate(scatters):
        recv = _exchange_wait(f"scatter{n}_wait", *handles, after, False)
        if layer == 4:
            break
        l = 1 if layer == 3 else 0
        for key, buf in zip(keys, recv):
            name = _GRAD_PARAM[layer][key]
            tr = min(256, buf.shape[1]) if name != "mla_w_q_up" else buf.shape[1]
            res[name, l] = _sum_adamw(f"adam_{name}{l}", buf, 0, p[name], m[name], v[name], tr, l)
        after = [res[name, l][1]]
    small = _sum_adamw("adam_small", recv[0], 0, _pack_small(p, SMALL_ROWS_RS, True)[None], _pack_small(m, SMALL_ROWS_RS, True)[None],
                       _pack_small(v, SMALL_ROWS_RS, True)[None], SMALL_ROWS_RS)
    small = [_unpack_small(a, True) for a in small]
    final = {k: tuple(part[k] for part in small) for k in _SMALL_SHARDED + _SMALL_REPLICATED}
    for k in _WEIGHTS:
        if k not in final:
            layers = [res[k, l] for l in range(p[k].shape[0])]
            final[k] = tuple(jnp.stack([lay[part] for lay in layers]).reshape(p[k].shape) for part in range(4))
    res = final

    loss = lax.psum(loss, ("x", "y", "c"))
    out = [loss, grad_x[None]]
    for part in range(4):
        out += [res[k][part] for k in _WEIGHTS]
    return tuple(out)


def kernel(x, positions, pool_norm, pool_w_in, pool_w_grp, pool_scale, pool_w_out, conv_norm, conv_w_in, conv_w, conv_w_out, mla_norm, mla_w_in, mla_q_norm, mla_w_q_up, mla_kv_norm, mla_w_kv_up, mla_w_out, final_norm, loss_target, m_pool_norm, m_pool_w_in, m_pool_w_grp, m_pool_scale, m_pool_w_out, m_conv_norm, m_conv_w_in, m_conv_w, m_conv_w_out, m_mla_norm, m_mla_w_in, m_mla_q_norm, m_mla_w_q_up, m_mla_kv_norm, m_mla_w_kv_up, m_mla_w_out, m_final_norm, v_pool_norm, v_pool_w_in, v_pool_w_grp, v_pool_scale, v_pool_w_out, v_conv_norm, v_conv_w_in, v_conv_w, v_conv_w_out, v_mla_norm, v_mla_w_in, v_mla_q_norm, v_mla_w_q_up, v_mla_kv_norm, v_mla_w_kv_up, v_mla_w_out, v_final_norm):
    p = dict(pool_norm=pool_norm, pool_w_in=pool_w_in, pool_w_grp=pool_w_grp, pool_scale=pool_scale, pool_w_out=pool_w_out,
             conv_norm=conv_norm, conv_w_in=conv_w_in, conv_w=conv_w, conv_w_out=conv_w_out, mla_norm=mla_norm, mla_w_in=mla_w_in,
             mla_q_norm=mla_q_norm, mla_w_q_up=mla_w_q_up, mla_kv_norm=mla_kv_norm, mla_w_kv_up=mla_w_kv_up, mla_w_out=mla_w_out,
             final_norm=final_norm)
    m = dict(pool_norm=m_pool_norm, pool_w_in=m_pool_w_in, pool_w_grp=m_pool_w_grp, pool_scale=m_pool_scale, pool_w_out=m_pool_w_out,
             conv_norm=m_conv_norm, conv_w_in=m_conv_w_in, conv_w=m_conv_w, conv_w_out=m_conv_w_out, mla_norm=m_mla_norm,
             mla_w_in=m_mla_w_in, mla_q_norm=m_mla_q_norm, mla_w_q_up=m_mla_w_q_up, mla_kv_norm=m_mla_kv_norm,
             mla_w_kv_up=m_mla_w_kv_up, mla_w_out=m_mla_w_out, final_norm=m_final_norm)
    v = dict(pool_norm=v_pool_norm, pool_w_in=v_pool_w_in, pool_w_grp=v_pool_w_grp, pool_scale=v_pool_scale, pool_w_out=v_pool_w_out,
             conv_norm=v_conv_norm, conv_w_in=v_conv_w_in, conv_w=v_conv_w, conv_w_out=v_conv_w_out, mla_norm=v_mla_norm,
             mla_w_in=v_mla_w_in, mla_q_norm=v_mla_q_norm, mla_w_q_up=v_mla_w_q_up, mla_kv_norm=v_mla_kv_norm,
             mla_w_kv_up=v_mla_w_kv_up, mla_w_out=v_mla_w_out, final_norm=v_final_norm)
    return _step(x, positions, loss_target, p, m, v)
```
